```python
import jax, jax.numpy as jnp
from jax import lax
import numpy as np

D_MODEL = 2048
BATCH = 4
SEQ = 2048
DEPTH = 1
DEC_BATCH = 128
DEC_SEQ = 1
PAST_LEN = 16384
PAGE_SIZE = 128

D_RNN = D_MODEL // 2
N_RNN_BLOCKS = 8
RNN_BLOCK = D_RNN // N_RNN_BLOCKS
CONV_W = 4
LRU_C = 8.0
D_POOL = D_MODEL // 2
POOL_WINDOWS = (2, 4, 8, 16)
N_POOL_GROUPS = len(POOL_WINDOWS)
POOL_GROUP = D_POOL // N_POOL_GROUPS
POOL_HIST = max(POOL_WINDOWS) - 1
N_MEM = 256
N_XHEADS = 4
XHEAD_DIM = D_MODEL // 8
D_X = N_XHEADS * XHEAD_DIM
N_BRANCH = 3
D_MIX = D_RNN + D_POOL + D_X
D_IN = 2 * D_MIX + N_BRANCH * D_MODEL
EPS = 1e-6

kernel_name = "hybrid_rglru_pool_memxattn_step"


def rmsnorm(x, g):
    xf = x.astype(jnp.float32)
    y = xf * lax.rsqrt(jnp.mean(xf * xf, axis=-1, keepdims=True) + EPS)
    return (y * g.astype(jnp.float32)).astype(x.dtype)


def causal_conv(x, buf, w, b):
    L = x.shape[1]
    ext = jnp.concatenate([buf.astype(x.dtype), x], axis=1)
    out = b + sum(w[k] * ext[:, k:k + L] for k in range(CONV_W))
    return out, ext[:, -(CONV_W - 1):]


def rg_lru(x, h0, w_a, b_a, w_x, b_x, lam):
    B, L, _ = x.shape
    xb = x.reshape(B, L, N_RNN_BLOCKS, RNN_BLOCK)
    r = jax.nn.sigmoid((jnp.einsum('blnd,nde->blne', xb, w_a).reshape(B, L, D_RNN) + b_a).astype(jnp.float32))
    i = jax.nn.sigmoid((jnp.einsum('blnd,nde->blne', xb, w_x).reshape(B, L, D_RNN) + b_x).astype(jnp.float32))
    log_a = -LRU_C * r * jax.nn.softplus(-lam.astype(jnp.float32))
    a = jnp.exp(log_a)
    mult = jnp.sqrt(-jnp.expm1(2.0 * log_a))
    bterm = mult * i * x.astype(jnp.float32)
    bterm = bterm.at[:, 0].add(a[:, 0] * h0.astype(jnp.float32))

    def combine(lhs, rhs):
        a1, b1 = lhs
        a2, b2 = rhs
        return a1 * a2, a2 * b1 + b2

    _, h = lax.associative_scan(combine, (a, bterm), axis=1)
    return h.astype(x.dtype), h[:, -1].astype(x.dtype)


def pool_mix(x, hist, pos, w_pool, pool_scale):
    L = x.shape[1]
    ext = jnp.concatenate([hist.astype(jnp.float32), x.astype(jnp.float32)], axis=1)
    cs = jnp.concatenate([jnp.zeros_like(ext[:, :1]), jnp.cumsum(ext, axis=1)], axis=1)
    end = cs[:, POOL_HIST + 1:POOL_HIST + 1 + L]
    xf = x.astype(jnp.float32)
    outs = []
    for g, w in enumerate(POOL_WINDOWS):
        c0, c1 = g * POOL_GROUP, (g + 1) * POOL_GROUP
        start = cs[:, POOL_HIST + 1 - w:POOL_HIST + 1 - w + L, c0:c1]
        cnt = jnp.minimum(pos + 1, w).astype(jnp.float32)[None, :, None]
        d = (end[..., c0:c1] - start) / cnt - xf[..., c0:c1]
        outs.append(jnp.einsum('bld,de->ble', d, w_pool[g].astype(jnp.float32)))
    out = jnp.concatenate(outs, axis=-1) * pool_scale.astype(jnp.float32)
    return out.astype(x.dtype), ext[:, -POOL_HIST:].astype(x.dtype)


def mem_kv(mem, g_mem, w_kv):
    B, M, _ = mem.shape
    kv = rmsnorm(mem, g_mem) @ w_kv
    k, v = jnp.split(kv, 2, axis=-1)
    return k.reshape(B, M, N_XHEADS, XHEAD_DIM), v.reshape(B, M, N_XHEADS, XHEAD_DIM)


def cross_attn(q, k, v):
    B, L = q.shape[:2]
    s = jnp.einsum('blhd,bmhd->bhlm', q, k).astype(jnp.float32) * (XHEAD_DIM ** -0.5)
    p = jax.nn.softmax(s, axis=-1)
    o = jnp.einsum('bhlm,bmhd->blhd', p.astype(v.dtype), v)
    return o.reshape(B, L, D_X)


def layer(x, pos, conv_buf, h0, pool_hist, mem_k, mem_v, g_pre, w_in, conv_w, conv_b,
          w_rg_a, b_rg_a, w_rg_x, b_rg_x, lru_lambda, w_pool, pool_scale, w_branch, w_out, g_post):
    B, L, _ = x.shape
    u = rmsnorm(x, g_pre)
    z = u @ w_in
    cuts = [D_RNN, 2 * D_RNN, 2 * D_RNN + D_POOL, 2 * D_RNN + 2 * D_POOL,
            2 * D_RNN + 2 * D_POOL + D_X, 2 * D_MIX]
    xr, gr, xp, gp, q, gx, gates = jnp.split(z, cuts, axis=-1)
    xr_c, new_conv = causal_conv(xr, conv_buf, conv_w, conv_b)
    h, h_last = rg_lru(xr_c, h0, w_rg_a, b_rg_a, w_rg_x, b_rg_x, lru_lambda)
    o_r = h * jax.nn.silu(gr)
    o_p, new_hist = pool_mix(xp, pool_hist, pos, w_pool, pool_scale)
    o_p = o_p * jax.nn.silu(gp)
    o_x = cross_attn(q.reshape(B, L, N_XHEADS, XHEAD_DIM), mem_k, mem_v) * jax.nn.silu(gx)
    y_r = o_r @ w_branch[:D_RNN]
    y_p = o_p @ w_branch[D_RNN:D_RNN + D_POOL]
    y_x = o_x @ w_branch[D_RNN + D_POOL:]
    gs = jax.nn.sigmoid(gates.astype(jnp.float32)).reshape(B, L, N_BRANCH, D_MODEL).astype(x.dtype)
    merged = gs[:, :, 0] * y_r + gs[:, :, 1] * y_p + gs[:, :, 2] * y_x
    out = merged @ w_out
    return x + rmsnorm(out, g_post), new_conv, h_last, new_hist


def setup_inputs(seed: int = 0) -> dict:
    key = jax.random.key(seed)
    ks = jax.random.split(key, 32)
    f32 = jnp.float32

    def nrm(k, shape, scale):
        return jax.random.normal(k, shape, f32) * scale

    a0 = jax.random.uniform(ks[12], (DEPTH, D_RNN), f32, 0.9, 0.999) ** (1.0 / LRU_C)
    return {
        'x_prompt': nrm(ks[0], (BATCH, SEQ, D_MODEL), 1.0),
        'x_sample': nrm(ks[1], (DEC_BATCH, DEC_SEQ, D_MODEL), 1.0),
        'mem_prompt': nrm(ks[2], (BATCH, N_MEM, D_MODEL), 1.0),
        'state_rglru_h': nrm(ks[3], (DEPTH, DEC_BATCH, D_RNN), 0.5),
        'state_conv': nrm(ks[4], (DEPTH, DEC_BATCH, CONV_W - 1, D_RNN), 1.0),
        'state_pool': nrm(ks[5], (DEPTH, DEC_BATCH, POOL_HIST, D_POOL), 1.0),
        'cache_mem_k': nrm(ks[6], (DEPTH, DEC_BATCH, N_MEM, N_XHEADS, XHEAD_DIM), 1.0),
        'cache_mem_v': nrm(ks[7], (DEPTH, DEC_BATCH, N_MEM, N_XHEADS, XHEAD_DIM), 1.0),
        'g_pre': 1.0 + nrm(ks[8], (DEPTH, D_MODEL), 0.05),
        'w_in': nrm(ks[9], (DEPTH, D_MODEL, D_IN), D_MODEL ** -0.5),
        'conv_w': nrm(ks[10], (DEPTH, CONV_W, D_RNN), CONV_W ** -0.5),
        'conv_b': nrm(ks[11], (DEPTH, D_RNN), 0.02),
        'w_rg_a': nrm(ks[13], (DEPTH, N_RNN_BLOCKS, RNN_BLOCK, RNN_BLOCK), RNN_BLOCK ** -0.5),
        'b_rg_a': nrm(ks[14], (DEPTH, D_RNN), 0.02),
        'w_rg_x': nrm(ks[15], (DEPTH, N_RNN_BLOCKS, RNN_BLOCK, RNN_BLOCK), RNN_BLOCK ** -0.5),
        'b_rg_x': nrm(ks[16], (DEPTH, D_RNN), 0.02),
        'lru_lambda': jnp.log(a0) - jnp.log1p(-a0),
        'w_pool': nrm(ks[17], (DEPTH, N_POOL_GROUPS, POOL_GROUP, POOL_GROUP), POOL_GROUP ** -0.5),
        'pool_scale': 1.0 + nrm(ks[18], (DEPTH, D_POOL), 0.05),
        'g_mem': 1.0 + nrm(ks[19], (DEPTH, D_MODEL), 0.05),
        'w_kv': nrm(ks[20], (DEPTH, D_MODEL, 2 * D_X), D_MODEL ** -0.5),
        'w_branch': nrm(ks[21], (DEPTH, D_MIX, D_MODEL), (D_MIX // N_BRANCH) ** -0.5),
        'w_out': nrm(ks[22], (DEPTH, D_MODEL, D_MODEL), D_MODEL ** -0.5),
        'g_post': 1.0 + nrm(ks[23], (DEPTH, D_MODEL), 0.05),
    }


def reference(x_prompt, x_sample, mem_prompt, state_rglru_h, state_conv, state_pool,
              cache_mem_k, cache_mem_v, g_pre, w_in, conv_w, conv_b, w_rg_a, b_rg_a,
              w_rg_x, b_rg_x, lru_lambda, w_pool, pool_scale, g_mem, w_kv, w_branch,
              w_out, g_post):
    B, S, _ = x_prompt.shape
    pos_p = jnp.arange(S, dtype=jnp.int32)
    pos_s = PAST_LEN + jnp.arange(x_sample.shape[1], dtype=jnp.int32)
    yp, ys = x_prompt, x_sample
    hp_l, cp_l, pp_l, mk_l, mv_l, hs_l, cs_l, ps_l = [], [], [], [], [], [], [], []
    for l in range(DEPTH):
        lw = (g_pre[l], w_in[l], conv_w[l], conv_b[l], w_rg_a[l], b_rg_a[l], w_rg_x[l],
              b_rg_x[l], lru_lambda[l], w_pool[l], pool_scale[l], w_branch[l], w_out[l], g_post[l])
        mk, mv = mem_kv(mem_prompt, g_mem[l], w_kv[l])
        zc = jnp.zeros((B, CONV_W - 1, D_RNN), yp.dtype)
        zh = jnp.zeros((B, D_RNN), yp.dtype)
        zp = jnp.zeros((B, POOL_HIST, D_POOL), yp.dtype)
        yp, c_p, h_p, p_p = layer(yp, pos_p, zc, zh, zp, mk, mv, *lw)
        ys, c_s, h_s, p_s = layer(ys, pos_s, state_conv[l], state_rglru_h[l], state_pool[l],
                                  cache_mem_k[l], cache_mem_v[l], *lw)
        hp_l.append(h_p); cp_l.append(c_p); pp_l.append(p_p); mk_l.append(mk); mv_l.append(mv)
        hs_l.append(h_s); cs_l.append(c_s); ps_l.append(p_s)
    new_h_prompt = jnp.stack(hp_l)
    new_conv_prompt = jnp.stack(cp_l)
    new_pool_prompt = jnp.stack(pp_l)
    mem_k_prompt = jnp.stack(mk_l)
    mem_v_prompt = jnp.stack(mv_l)
    new_h_sample = jnp.stack(hs_l)
    new_conv_sample = jnp.stack(cs_l)
    new_pool_sample = jnp.stack(ps_l)
    return (yp, ys, new_h_prompt, new_conv_prompt, new_pool_prompt, mem_k_prompt, mem_v_prompt,
            new_h_sample, new_conv_sample, new_pool_sample)
```

```python
import functools

import jax
import jax.numpy as jnp
from jax import lax
from jax.experimental import pallas as pl
from jax.experimental.pallas import tpu as pltpu

D_MODEL = 2048
PAST_LEN = 16384
D_RNN = 1024
N_RNN_BLOCKS = 8
RNN_BLOCK = D_RNN // N_RNN_BLOCKS
CONV_W = 4
LRU_C = 8.0
D_POOL = 1024
POOL_WINDOWS = (2, 4, 8, 16)
POOL_GROUP = D_POOL // len(POOL_WINDOWS)
POOL_HIST = max(POOL_WINDOWS) - 1
N_MEM = 256
N_XHEADS = 4
XHEAD_DIM = 256
D_X = N_XHEADS * XHEAD_DIM
N_BRANCH = 3
D_MIX = D_RNN + D_POOL + D_X
D_IN = 2 * D_MIX + N_BRANCH * D_MODEL
EPS = 1e-6

SUBLANES = 8
VMEM_LIMIT = 56 * 1024 * 1024

BF16 = jnp.bfloat16
F32 = jnp.float32


def _sigmoid(x):
    return 1.0 / (1.0 + jnp.exp(-x))


def _silu(x):
    return x * _sigmoid(x)


def _softplus(x):
    return jnp.maximum(x, 0.0) + jnp.log1p(jnp.exp(-jnp.abs(x)))


def _rms_scale(x):
    return lax.rsqrt(jnp.mean(x * x, axis=-1, keepdims=True) + EPS)


def _norm_matmul_kernel(x_ref, g_ref, w_ref, o_ref, u_ref):
    @pl.when(pl.program_id(1) == 0)
    def _():
        x = x_ref[...]
        u_ref[...] = (x * _rms_scale(x) * g_ref[...]).astype(BF16)

    o_ref[...] = jnp.dot(u_ref[...], w_ref[...], preferred_element_type=F32)


def _norm_matmul(x, g, w, tm, tn):
    m, k = x.shape
    n = w.shape[1]
    return pl.pallas_call(
        _norm_matmul_kernel,
        grid=(m // tm, n // tn),
        in_specs=[
            pl.BlockSpec((tm, k), lambda i, j: (i, 0)),
            pl.BlockSpec((1, k), lambda i, j: (0, 0)),
            pl.BlockSpec((k, tn), lambda i, j: (0, j)),
        ],
        out_specs=pl.BlockSpec((tm, tn), lambda i, j: (i, j)),
        out_shape=jax.ShapeDtypeStruct((m, n), F32),
        scratch_shapes=[pltpu.VMEM((tm, k), BF16)],
        compiler_params=pltpu.CompilerParams(
            dimension_semantics=("arbitrary", "arbitrary"),
            vmem_limit_bytes=VMEM_LIMIT),
        name="norm_matmul",
    )(x, g, w)


def _rglru_coeffs(xc, wax_ref, ba, bx, sp):
    xcb = xc.astype(BF16)
    a_parts, b_parts = [], []
    for n in range(N_RNN_BLOCKS):
        c0, c1 = n * RNN_BLOCK, (n + 1) * RNN_BLOCK
        ri = jnp.dot(xcb[:, c0:c1], wax_ref[n], preferred_element_type=F32)
        r = _sigmoid(ri[:, :RNN_BLOCK] + ba[:, c0:c1])
        i = _sigmoid(ri[:, RNN_BLOCK:] + bx[:, c0:c1])
        log_a = (-LRU_C) * r * sp[:, c0:c1]
        a = jnp.exp(log_a)
        mult = jnp.sqrt(1.0 - jnp.exp(2.0 * log_a))
        a_parts.append(a)
        b_parts.append(mult * i * xc[:, c0:c1])
    return a_parts, b_parts


def _prompt_mix_kernel(z_ref, k_ref, v_ref, convw_ref, convb_ref, wax_ref, ba_ref, bx_ref,
                       lam_ref, wpool_ref, pscale_ref,
                       o_ref, newh_ref, newconv_ref, newpool_ref,
                       ext_r, ext_p, h_carry, kb_ref, vb_ref, a_scr, b_scr, h_scr, *, tm):
    l = pl.program_id(1)
    last = pl.num_programs(1) - 1
    rpad = SUBLANES
    ppad = 2 * SUBLANES

    @pl.when(l == 0)
    def _():
        ext_r[0:rpad, :] = jnp.zeros((rpad, D_RNN), F32)
        ext_p[0:ppad, :] = jnp.zeros((ppad, D_POOL), F32)
        h_carry[...] = jnp.zeros((SUBLANES, D_RNN), F32)
        kb_ref[...] = k_ref[0].astype(BF16)
        vb_ref[...] = v_ref[0].astype(BF16)

    ext_r[rpad:rpad + tm, :] = z_ref[:, 0:D_RNN]
    xc = convb_ref[...] + convw_ref[CONV_W - 1:CONV_W, :] * ext_r[rpad:rpad + tm, :]
    for k in range(CONV_W - 1):
        sh = CONV_W - 1 - k
        xc = xc + convw_ref[k:k + 1, :] * ext_r[rpad - sh:rpad - sh + tm, :]

    sp = _softplus(-lam_ref[...])
    a_parts, b_parts = _rglru_coeffs(xc, wax_ref, ba_ref[...], bx_ref[...], sp)
    for n in range(N_RNN_BLOCKS):
        c0, c1 = n * RNN_BLOCK, (n + 1) * RNN_BLOCK
        a_scr[:, c0:c1] = a_parts[n]
        b_scr[:, c0:c1] = b_parts[n]

    row = lax.broadcasted_iota(jnp.int32, (SUBLANES, D_RNN), 0)

    def scan_group(g, hprev):
        off = pl.multiple_of(g * SUBLANES, SUBLANES)
        a = a_scr[pl.ds(off, SUBLANES), :]
        b = b_scr[pl.ds(off, SUBLANES), :]
        for s in (1, 2, 4):
            a_sh = jnp.where(row >= s, pltpu.roll(a, s, 0), 1.0)
            b_sh = jnp.where(row >= s, pltpu.roll(b, s, 0), 0.0)
            b = a * b_sh + b
            a = a * a_sh
        h = a * hprev + b
        h_scr[pl.ds(off, SUBLANES), :] = h
        return jnp.broadcast_to(h[SUBLANES - 1:SUBLANES, :], (SUBLANES, D_RNN))

    h_last = lax.fori_loop(0, tm // SUBLANES, scan_group, h_carry[...])
    h_carry[...] = h_last
    o_ref[:, 0:D_RNN] = (h_scr[...] * _silu(z_ref[:, D_RNN:2 * D_RNN])).astype(BF16)

    ext_p[ppad:ppad + tm, :] = z_ref[:, 2 * D_RNN:2 * D_RNN + D_POOL]
    pos1 = l * tm + lax.broadcasted_iota(jnp.int32, (tm, 1), 0) + 1
    for g, w in enumerate(POOL_WINDOWS):
        c0, c1 = g * POOL_GROUP, (g + 1) * POOL_GROUP
        xg = ext_p[ppad:ppad + tm, c0:c1]
        tot = xg
        for j in range(1, w):
            tot = tot + ext_p[ppad - j:ppad - j + tm, c0:c1]
        cnt = jnp.minimum(pos1, w).astype(F32)
        d = tot / cnt - xg
        og = jnp.dot(d.astype(BF16), wpool_ref[g], preferred_element_type=F32)
        gp = z_ref[:, 2 * D_RNN + D_POOL + c0:2 * D_RNN + D_POOL + c1]
        o_ref[:, D_RNN + c0:D_RNN + c1] = (og * pscale_ref[:, c0:c1] * _silu(gp)).astype(BF16)

    qoff = 2 * D_RNN + 2 * D_POOL
    for hd in range(N_XHEADS):
        c0, c1 = hd * XHEAD_DIM, (hd + 1) * XHEAD_DIM
        q = z_ref[:, qoff + c0:qoff + c1].astype(BF16)
        s = lax.dot_general(q, kb_ref[:, c0:c1], (((1,), (1,)), ((), ())),
                            preferred_element_type=F32) * (XHEAD_DIM ** -0.5)
        p = jnp.exp(s - jnp.max(s, axis=-1, keepdims=True))
        p = p / jnp.sum(p, axis=-1, keepdims=True)
        ox = jnp.dot(p.astype(BF16), vb_ref[:, c0:c1], preferred_element_type=F32)
        gx = z_ref[:, qoff + D_X + c0:qoff + D_X + c1]
        o_ref[:, D_RNN + D_POOL + c0:D_RNN + D_POOL + c1] = (ox * _silu(gx)).astype(BF16)

    @pl.when(l == last)
    def _():
        newh_ref[0] = h_scr[tm - 1:tm, :]
        newconv_ref[0] = ext_r[rpad + tm - (CONV_W - 1):rpad + tm, :]
        newpool_ref[0] = ext_p[ppad + tm - POOL_HIST:ppad + tm, :]

    ext_r[0:rpad, :] = ext_r[tm:tm + rpad, :]
    ext_p[0:ppad, :] = ext_p[tm:tm + ppad, :]


def _prompt_mix(z, mem_k, mem_v, conv_w, conv_b, wax, b_a, b_x, lam, wpool, pscale,
                batch, seq, tm):
    nl = seq // tm
    zw = 2 * D_MIX
    const2 = lambda b, l: (0, 0)
    const3 = lambda b, l: (0, 0, 0)
    kern = functools.partial(_prompt_mix_kernel, tm=tm)
    return pl.pallas_call(
        kern,
        grid=(batch, nl),
        in_specs=[
            pl.BlockSpec((tm, zw), lambda b, l: (b * nl + l, 0)),
            pl.BlockSpec((1, N_MEM, D_X), lambda b, l: (b, 0, 0)),
            pl.BlockSpec((1, N_MEM, D_X), lambda b, l: (b, 0, 0)),
            pl.BlockSpec((CONV_W, D_RNN), const2),
            pl.BlockSpec((1, D_RNN), const2),
            pl.BlockSpec((N_RNN_BLOCKS, RNN_BLOCK, 2 * RNN_BLOCK), const3),
            pl.BlockSpec((1, D_RNN), const2),
            pl.BlockSpec((1, D_RNN), const2),
            pl.BlockSpec((1, D_RNN), const2),
            pl.BlockSpec((len(POOL_WINDOWS), POOL_GROUP, POOL_GROUP), const3),
            pl.BlockSpec((1, D_POOL), const2),
        ],
        out_specs=[
            pl.BlockSpec((tm, D_MIX), lambda b, l: (b * nl + l, 0)),
            pl.BlockSpec((1, 1, D_RNN), lambda b, l: (b, 0, 0)),
            pl.BlockSpec((1, CONV_W - 1, D_RNN), lambda b, l: (b, 0, 0)),
            pl.BlockSpec((1, POOL_HIST, D_POOL), lambda b, l: (b, 0, 0)),
        ],
        out_shape=[
            jax.ShapeDtypeStruct((batch * seq, D_MIX), BF16),
            jax.ShapeDtypeStruct((batch, 1, D_RNN), F32),
            jax.ShapeDtypeStruct((batch, CONV_W - 1, D_RNN), F32),
            jax.ShapeDtypeStruct((batch, POOL_HIST, D_POOL), F32),
        ],
        scratch_shapes=[
            pltpu.VMEM((tm + SUBLANES, D_RNN), F32),
            pltpu.VMEM((tm + 2 * SUBLANES, D_POOL), F32),
            pltpu.VMEM((SUBLANES, D_RNN), F32),
            pltpu.VMEM((N_MEM, D_X), BF16),
            pltpu.VMEM((N_MEM, D_X), BF16),
            pltpu.VMEM((tm, D_RNN), F32),
            pltpu.VMEM((tm, D_RNN), F32),
            pltpu.VMEM((tm, D_RNN), F32),
        ],
        compiler_params=pltpu.CompilerParams(
            dimension_semantics=("arbitrary", "arbitrary"),
            vmem_limit_bytes=VMEM_LIMIT),
        name="prompt_mix",
    )(z, mem_k, mem_v, conv_w, conv_b, wax, b_a, b_x, lam, wpool, pscale)


def _sample_attn_kernel(q_ref, k_ref, v_ref, o_ref, *, bb):
    rows = lax.broadcasted_iota(jnp.int32, (SUBLANES, D_X), 0)
    cols = lax.broadcasted_iota(jnp.int32, (SUBLANES, D_X), 1)
    head_mask = (cols // XHEAD_DIM) == rows
    for j in range(bb):
        q = jnp.broadcast_to(q_ref[j:j + 1, :], (SUBLANES, D_X))
        qm = jnp.where(head_mask, q, 0.0).astype(BF16)
        s = lax.dot_general(qm, k_ref[j].astype(BF16), (((1,), (1,)), ((), ())),
                            preferred_element_type=F32) * (XHEAD_DIM ** -0.5)
        p = jnp.exp(s - jnp.max(s, axis=-1, keepdims=True))
        p = p / jnp.sum(p, axis=-1, keepdims=True)
        of = jnp.dot(p.astype(BF16), v_ref[j].astype(BF16), preferred_element_type=F32)
        o_ref[j:j + 1, :] = jnp.sum(jnp.where(head_mask, of, 0.0), axis=0, keepdims=True)


def _sample_attn(z, cache_k, cache_v, bb):
    nb = z.shape[0]
    qblk = (2 * D_RNN + 2 * D_POOL) // D_X
    return pl.pallas_call(
        functools.partial(_sample_attn_kernel, bb=bb),
        grid=(nb // bb,),
        in_specs=[
            pl.BlockSpec((bb, D_X), lambda i: (i, qblk)),
            pl.BlockSpec((bb, N_MEM, D_X), lambda i: (i, 0, 0)),
            pl.BlockSpec((bb, N_MEM, D_X), lambda i: (i, 0, 0)),
        ],
        out_specs=pl.BlockSpec((bb, D_X), lambda i: (i, 0)),
        out_shape=jax.ShapeDtypeStruct((nb, D_X), F32),
        compiler_params=pltpu.CompilerParams(
            dimension_semantics=("arbitrary",),
            vmem_limit_bytes=VMEM_LIMIT),
        name="sample_attn",
    )(z, cache_k, cache_v)


def _sample_mix_kernel(z_ref, attn_ref, conv_ref, h_ref, pool_ref,
                       convw_ref, convb_ref, wax_ref, ba_ref, bx_ref, lam_ref, wpool_ref,
                       pscale_ref, o_ref, newh_ref, newconv_ref, newpool_ref):
    xr = z_ref[:, 0:D_RNN]
    xc = convb_ref[...] + convw_ref[CONV_W - 1:CONV_W, :] * xr
    for k in range(CONV_W - 1):
        xc = xc + convw_ref[k:k + 1, :] * conv_ref[:, k * D_RNN:(k + 1) * D_RNN]
    newconv_ref[:, 0:(CONV_W - 2) * D_RNN] = conv_ref[:, D_RNN:(CONV_W - 1) * D_RNN]
    newconv_ref[:, (CONV_W - 2) * D_RNN:] = xr

    sp = _softplus(-lam_ref[...])
    a_parts, b_parts = _rglru_coeffs(xc, wax_ref, ba_ref[...], bx_ref[...], sp)
    for n in range(N_RNN_BLOCKS):
        c0, c1 = n * RNN_BLOCK, (n + 1) * RNN_BLOCK
        h = a_parts[n] * h_ref[:, c0:c1] + b_parts[n]
        newh_ref[:, c0:c1] = h
        o_ref[:, c0:c1] = (h * _silu(z_ref[:, D_RNN + c0:D_RNN + c1])).astype(BF16)

    xp = z_ref[:, 2 * D_RNN:2 * D_RNN + D_POOL]
    newpool_ref[:, 0:(POOL_HIST - 1) * D_POOL] = pool_ref[:, D_POOL:POOL_HIST * D_POOL]
    newpool_ref[:, (POOL_HIST - 1) * D_POOL:] = xp
    for g, w in enumerate(POOL_WINDOWS):
        c0, c1 = g * POOL_GROUP, (g + 1) * POOL_GROUP
        xg = xp[:, c0:c1]
        tot = xg
        for j in range(1, w):
            hrow = POOL_HIST - j
            tot = tot + pool_ref[:, hrow * D_POOL + c0:hrow * D_POOL + c1]
        cnt = float(min(PAST_LEN + 1, w))
        d = tot / cnt - xg
        og = jnp.dot(d.astype(BF16), wpool_ref[g], preferred_element_type=F32)
        gp = z_ref[:, 2 * D_RNN + D_POOL + c0:2 * D_RNN + D_POOL + c1]
        o_ref[:, D_RNN + c0:D_RNN + c1] = (og * pscale_ref[:, c0:c1] * _silu(gp)).astype(BF16)

    gx = z_ref[:, 2 * D_RNN + 2 * D_POOL + D_X:2 * D_MIX]
    o_ref[:, D_RNN + D_POOL:] = (attn_ref[...] * _silu(gx)).astype(BF16)


def _sample_mix(z, attn, conv, h, pool, conv_w, conv_b, wax, b_a, b_x, lam, wpool, pscale, tb):
    nb = z.shape[0]
    zw = 2 * D_MIX
    rows = lambda i: (i, 0)
    const2 = lambda i: (0, 0)
    const3 = lambda i: (0, 0, 0)
    cw = (CONV_W - 1) * D_RNN
    pw = POOL_HIST * D_POOL
    return pl.pallas_call(
        _sample_mix_kernel,
        grid=(nb // tb,),
        in_specs=[
            pl.BlockSpec((tb, zw), rows),
            pl.BlockSpec((tb, D_X), rows),
            pl.BlockSpec((tb, cw), rows),
            pl.BlockSpec((tb, D_RNN), rows),
            pl.BlockSpec((tb, pw), rows),
            pl.BlockSpec((CONV_W, D_RNN), const2),
            pl.BlockSpec((1, D_RNN), const2),
            pl.BlockSpec((N_RNN_BLOCKS, RNN_BLOCK, 2 * RNN_BLOCK), const3),
            pl.BlockSpec((1, D_RNN), const2),
            pl.BlockSpec((1, D_RNN), const2),
            pl.BlockSpec((1, D_RNN), const2),
            pl.BlockSpec((len(POOL_WINDOWS), POOL_GROUP, POOL_GROUP), const3),
            pl.BlockSpec((1, D_POOL), const2),
        ],
        out_specs=[
            pl.BlockSpec((tb, D_MIX), rows),
            pl.BlockSpec((tb, D_RNN), rows),
            pl.BlockSpec((tb, cw), rows),
            pl.BlockSpec((tb, pw), rows),
        ],
        out_shape=[
            jax.ShapeDtypeStruct((nb, D_MIX), BF16),
            jax.ShapeDtypeStruct((nb, D_RNN), F32),
            jax.ShapeDtypeStruct((nb, cw), F32),
            jax.ShapeDtypeStruct((nb, pw), F32),
        ],
        compiler_params=pltpu.CompilerParams(
            dimension_semantics=("arbitrary",),
            vmem_limit_bytes=VMEM_LIMIT),
        name="sample_mix",
    )(z, attn, conv, h, pool, conv_w, conv_b, wax, b_a, b_x, lam, wpool, pscale)


def _branch_out_kernel(o_ref, gates_ref, x_ref, wb_ref, wo_ref, gpost_ref, y_ref):
    merged = None
    for j, (r0, r1) in enumerate(((0, D_RNN), (D_RNN, D_RNN + D_POOL), (D_RNN + D_POOL, D_MIX))):
        yj = jnp.dot(o_ref[:, r0:r1], wb_ref[r0:r1, :], preferred_element_type=F32)
        term = _sigmoid(gates_ref[:, j * D_MODEL:(j + 1) * D_MODEL]) * yj
        merged = term if merged is None else merged + term
    out = jnp.dot(merged.astype(BF16), wo_ref[...], preferred_element_type=F32)
    y_ref[...] = x_ref[...] + out * _rms_scale(out) * gpost_ref[...]


def _branch_out(o, z, x, wb, wo, g_post, tm):
    m = x.shape[0]
    gw = N_BRANCH * D_MODEL
    gblk = (2 * D_MIX) // gw
    resident = pl.Buffered(1)
    return pl.pallas_call(
        _branch_out_kernel,
        grid=(m // tm,),
        in_specs=[
            pl.BlockSpec((tm, D_MIX), lambda i: (i, 0)),
            pl.BlockSpec((tm, gw), lambda i: (i, gblk)),
            pl.BlockSpec((tm, D_MODEL), lambda i: (i, 0)),
            pl.BlockSpec((D_MIX, D_MODEL), lambda i: (0, 0), pipeline_mode=resident),
            pl.BlockSpec((D_MODEL, D_MODEL), lambda i: (0, 0), pipeline_mode=resident),
            pl.BlockSpec((1, D_MODEL), lambda i: (0, 0)),
        ],
        out_specs=pl.BlockSpec((tm, D_MODEL), lambda i: (i, 0)),
        out_shape=jax.ShapeDtypeStruct((m, D_MODEL), F32),
        compiler_params=pltpu.CompilerParams(
            dimension_semantics=("arbitrary",),
            vmem_limit_bytes=VMEM_LIMIT),
        name="branch_out",
    )(o, z, x, wb, wo, g_post)


def kernel(x_prompt, x_sample, mem_prompt, state_rglru_h, state_conv, state_pool, cache_mem_k, cache_mem_v, g_pre, w_in, conv_w, conv_b, w_rg_a, b_rg_a, w_rg_x, b_rg_x, lru_lambda, w_pool, pool_scale, g_mem, w_kv, w_branch, w_out, g_post):
    batch, seq, _ = x_prompt.shape
    nb = x_sample.shape[0]
    depth = g_pre.shape[0]
    assert depth == 1 and x_sample.shape[1] == 1

    l = 0
    row = lambda v: v.reshape(1, -1)
    w_in_b = w_in[l].astype(BF16)
    w_kv_b = w_kv[l].astype(BF16)
    w_br_b = w_branch[l].astype(BF16)
    w_out_b = w_out[l].astype(BF16)
    wax = jnp.concatenate([w_rg_a[l], w_rg_x[l]], axis=-1).astype(BF16)
    wpool = w_pool[l].astype(BF16)
    mix_params = (conv_w[l], row(conv_b[l]), wax, row(b_rg_a[l]), row(b_rg_x[l]),
                  row(lru_lambda[l]), wpool, row(pool_scale[l]))

    xp2 = x_prompt.reshape(batch * seq, D_MODEL)
    xs2 = x_sample.reshape(nb, D_MODEL)
    mem2 = mem_prompt.reshape(batch * N_MEM, D_MODEL)

    kv = _norm_matmul(mem2, row(g_mem[l]), w_kv_b, tm=512, tn=1024)
    mem_k = kv[:, :D_X].reshape(batch, N_MEM, D_X)
    mem_v = kv[:, D_X:].reshape(batch, N_MEM, D_X)

    z_p = _norm_matmul(xp2, row(g_pre[l]), w_in_b, tm=1024, tn=1024)
    o_p, h_p, c_p, p_p = _prompt_mix(z_p, mem_k, mem_v, *mix_params,
                                     batch=batch, seq=seq, tm=256)
    y_p = _branch_out(o_p, z_p, xp2, w_br_b, w_out_b, row(g_post[l]), tm=256)

    z_s = _norm_matmul(xs2, row(g_pre[l]), w_in_b, tm=nb, tn=1024)
    attn_s = _sample_attn(z_s, cache_mem_k[l].reshape(nb, N_MEM, D_X),
                          cache_mem_v[l].reshape(nb, N_MEM, D_X), bb=8)
    o_s, h_s, c_s, p_s = _sample_mix(
        z_s, attn_s, state_conv[l].reshape(nb, (CONV_W - 1) * D_RNN), state_rglru_h[l],
        state_pool[l].reshape(nb, POOL_HIST * D_POOL), *mix_params, tb=32)
    y_s = _branch_out(o_s, z_s, xs2, w_br_b, w_out_b, row(g_post[l]), tm=nb)

    return (
        y_p.reshape(batch, seq, D_MODEL),
        y_s.reshape(nb, 1, D_MODEL),
        h_p.reshape(1, batch, D_RNN),
        c_p.reshape(1, batch, CONV_W - 1, D_RNN),
        p_p.reshape(1, batch, POOL_HIST, D_POOL),
        mem_k.reshape(1, batch, N_MEM, N_XHEADS, XHEAD_DIM),
        mem_v.reshape(1, batch, N_MEM, N_XHEADS, XHEAD_DIM),
        h_s.reshape(1, nb, D_RNN),
        c_s.reshape(1, nb, CONV_W - 1, D_RNN),
        p_s.reshape(1, nb, POOL_HIST, D_POOL),
    )
```

```python
import functools

import jax
import jax.numpy as jnp
from jax import lax
from jax.experimental import pallas as pl
from jax.experimental.pallas import tpu as pltpu

D_MODEL = 2048
PAST_LEN = 16384
D_RNN = 1024
N_RNN_BLOCKS = 8
RNN_BLOCK = D_RNN // N_RNN_BLOCKS
CONV_W = 4
LRU_C = 8.0
D_POOL = 1024
POOL_WINDOWS = (2, 4, 8, 16)
POOL_GROUP = D_POOL // len(POOL_WINDOWS)
POOL_HIST = max(POOL_WINDOWS) - 1
N_MEM = 256
N_XHEADS = 4
XHEAD_DIM = 256
D_X = N_XHEADS * XHEAD_DIM
N_BRANCH = 3
D_MIX = D_RNN + D_POOL + D_X
D_IN = 2 * D_MIX + N_BRANCH * D_MODEL
EPS = 1e-6

SUBLANES = 8
LANES = 128
VMEM_LIMIT = 56 * 1024 * 1024

BF16 = jnp.bfloat16
F32 = jnp.float32


def _sigmoid(x):
    return 1.0 / (1.0 + jnp.exp(-x))


def _silu(x):
    return x * _sigmoid(x)


def _softplus(x):
    return jnp.maximum(x, 0.0) + jnp.log1p(jnp.exp(-jnp.abs(x)))


def _rms_scale(x):
    return lax.rsqrt(jnp.mean(x * x, axis=-1, keepdims=True) + EPS)


def _norm_matmul_kernel(x_ref, g_ref, w_ref, o_ref, u_ref):
    @pl.when(pl.program_id(1) == 0)
    def _():
        x = x_ref[...]
        u_ref[...] = (x * _rms_scale(x) * g_ref[...]).astype(BF16)

    o_ref[...] = jnp.dot(u_ref[...], w_ref[...], preferred_element_type=F32)


def _norm_matmul(x, g, w, tm, tn):
    m, k = x.shape
    n = w.shape[1]
    return pl.pallas_call(
        _norm_matmul_kernel,
        grid=(m // tm, n // tn),
        in_specs=[
            pl.BlockSpec((tm, k), lambda i, j: (i, 0)),
            pl.BlockSpec((1, k), lambda i, j: (0, 0)),
            pl.BlockSpec((k, tn), lambda i, j: (0, j)),
        ],
        out_specs=pl.BlockSpec((tm, tn), lambda i, j: (i, j)),
        out_shape=jax.ShapeDtypeStruct((m, n), F32),
        scratch_shapes=[pltpu.VMEM((tm, k), BF16)],
        compiler_params=pltpu.CompilerParams(
            dimension_semantics=("arbitrary", "arbitrary"),
            vmem_limit_bytes=VMEM_LIMIT),
        name="norm_matmul",
    )(x, g, w)


def _rglru_coeffs(xc, wax_ref, ba, bx, sp):
    xcb = xc.astype(BF16)
    a_parts, b_parts = [], []
    for n in range(N_RNN_BLOCKS):
        c0, c1 = n * RNN_BLOCK, (n + 1) * RNN_BLOCK
        ri = jnp.dot(xcb[:, c0:c1], wax_ref[n], preferred_element_type=F32)
        r = _sigmoid(ri[:, :RNN_BLOCK] + ba[:, c0:c1])
        i = _sigmoid(ri[:, RNN_BLOCK:] + bx[:, c0:c1])
        log_a = (-LRU_C) * r * sp[:, c0:c1]
        a = jnp.exp(log_a)
        mult = jnp.sqrt(1.0 - jnp.exp(2.0 * log_a))
        a_parts.append(a)
        b_parts.append(mult * i * xc[:, c0:c1])
    return a_parts, b_parts


def _prompt_mix_kernel(z_ref, k_ref, v_ref, convw_ref, convb_ref, wax_ref, ba_ref, bx_ref,
                       lam_ref, wpool_ref, pscale_ref,
                       o_ref, newh_ref, newconv_ref, newpool_ref,
                       ext_r, ext_p, h_carry, kb_ref, vb_ref, a_scr, b_scr, h_scr, *, tm):
    l = pl.program_id(1)
    last = pl.num_programs(1) - 1
    rpad = SUBLANES
    ppad = 2 * SUBLANES

    @pl.when(l == 0)
    def _():
        ext_r[0:rpad, :] = jnp.zeros((rpad, D_RNN), F32)
        ext_p[0:ppad, :] = jnp.zeros((ppad, D_POOL), F32)
        h_carry[...] = jnp.zeros((SUBLANES, D_RNN), F32)
        kb_ref[...] = k_ref[0].astype(BF16)
        vb_ref[...] = v_ref[0].astype(BF16)

    ext_r[rpad:rpad + tm, :] = z_ref[:, 0:D_RNN]
    xc = convb_ref[...] + convw_ref[CONV_W - 1:CONV_W, :] * ext_r[rpad:rpad + tm, :]
    for k in range(CONV_W - 1):
        sh = CONV_W - 1 - k
        xc = xc + convw_ref[k:k + 1, :] * ext_r[rpad - sh:rpad - sh + tm, :]

    sp = _softplus(-lam_ref[...])
    a_parts, b_parts = _rglru_coeffs(xc, wax_ref, ba_ref[...], bx_ref[...], sp)
    for n in range(N_RNN_BLOCKS):
        c0, c1 = n * RNN_BLOCK, (n + 1) * RNN_BLOCK
        a_scr[:, c0:c1] = a_parts[n]
        b_scr[:, c0:c1] = b_parts[n]

    row = lax.broadcasted_iota(jnp.int32, (SUBLANES, D_RNN), 0)

    def scan_group(g, hprev):
        off = pl.multiple_of(g * SUBLANES, SUBLANES)
        a = a_scr[pl.ds(off, SUBLANES), :]
        b = b_scr[pl.ds(off, SUBLANES), :]
        for s in (1, 2, 4):
            a_sh = jnp.where(row >= s, pltpu.roll(a, s, 0), 1.0)
            b_sh = jnp.where(row >= s, pltpu.roll(b, s, 0), 0.0)
            b = a * b_sh + b
            a = a * a_sh
        h = a * hprev + b
        h_scr[pl.ds(off, SUBLANES), :] = h
        return jnp.broadcast_to(h[SUBLANES - 1:SUBLANES, :], (SUBLANES, D_RNN))

    h_last = lax.fori_loop(0, tm // SUBLANES, scan_group, h_carry[...])
    h_carry[...] = h_last
    o_ref[:, 0:D_RNN] = (h_scr[...] * _silu(z_ref[:, D_RNN:2 * D_RNN])).astype(BF16)

    ext_p[ppad:ppad + tm, :] = z_ref[:, 2 * D_RNN:2 * D_RNN + D_POOL]
    pos1 = l * tm + lax.broadcasted_iota(jnp.int32, (tm, 1), 0) + 1
    for g, w in enumerate(POOL_WINDOWS):
        c0, c1 = g * POOL_GROUP, (g + 1) * POOL_GROUP
        xg = ext_p[ppad:ppad + tm, c0:c1]
        tot = xg
        for j in range(1, w):
            tot = tot + ext_p[ppad - j:ppad - j + tm, c0:c1]
        cnt = jnp.minimum(pos1, w).astype(F32)
        d = tot / cnt - xg
        og = jnp.dot(d.astype(BF16), wpool_ref[g], preferred_element_type=F32)
        gp = z_ref[:, 2 * D_RNN + D_POOL + c0:2 * D_RNN + D_POOL + c1]
        o_ref[:, D_RNN + c0:D_RNN + c1] = (og * pscale_ref[:, c0:c1] * _silu(gp)).astype(BF16)

    qoff = 2 * D_RNN + 2 * D_POOL
    for hd in range(N_XHEADS):
        c0, c1 = hd * XHEAD_DIM, (hd + 1) * XHEAD_DIM
        q = z_ref[:, qoff + c0:qoff + c1].astype(BF16)
        s = lax.dot_general(q, kb_ref[:, c0:c1], (((1,), (1,)), ((), ())),
                            preferred_element_type=F32) * (XHEAD_DIM ** -0.5)
        p = jnp.exp(s - jnp.max(s, axis=-1, keepdims=True))
        p = p / jnp.sum(p, axis=-1, keepdims=True)
        ox = jnp.dot(p.astype(BF16), vb_ref[:, c0:c1], preferred_element_type=F32)
        gx = z_ref[:, qoff + D_X + c0:qoff + D_X + c1]
        o_ref[:, D_RNN + D_POOL + c0:D_RNN + D_POOL + c1] = (ox * _silu(gx)).astype(BF16)

    @pl.when(l == last)
    def _():
        newh_ref[0] = h_scr[tm - 1:tm, :]
        newconv_ref[0] = ext_r[rpad + tm - (CONV_W - 1):rpad + tm, :]
        newpool_ref[0] = ext_p[ppad + tm - POOL_HIST:ppad + tm, :]

    ext_r[0:rpad, :] = ext_r[tm:tm + rpad, :]
    ext_p[0:ppad, :] = ext_p[tm:tm + ppad, :]


def _prompt_mix(z, mem_k, mem_v, conv_w, conv_b, wax, b_a, b_x, lam, wpool, pscale,
                batch, seq, tm):
    nl = seq // tm
    zw = 2 * D_MIX
    const2 = lambda b, l: (0, 0)
    const3 = lambda b, l: (0, 0, 0)
    kern = functools.partial(_prompt_mix_kernel, tm=tm)
    return pl.pallas_call(
        kern,
        grid=(batch, nl),
        in_specs=[
            pl.BlockSpec((tm, zw), lambda b, l: (b * nl + l, 0)),
            pl.BlockSpec((1, N_MEM, D_X), lambda b, l: (b, 0, 0)),
            pl.BlockSpec((1, N_MEM, D_X), lambda b, l: (b, 0, 0)),
            pl.BlockSpec((CONV_W, D_RNN), const2),
            pl.BlockSpec((1, D_RNN), const2),
            pl.BlockSpec((N_RNN_BLOCKS, RNN_BLOCK, 2 * RNN_BLOCK), const3),
            pl.BlockSpec((1, D_RNN), const2),
            pl.BlockSpec((1, D_RNN), const2),
            pl.BlockSpec((1, D_RNN), const2),
            pl.BlockSpec((len(POOL_WINDOWS), POOL_GROUP, POOL_GROUP), const3),
            pl.BlockSpec((1, D_POOL), const2),
        ],
        out_specs=[
            pl.BlockSpec((tm, D_MIX), lambda b, l: (b * nl + l, 0)),
            pl.BlockSpec((1, 1, D_RNN), lambda b, l: (b, 0, 0)),
            pl.BlockSpec((1, CONV_W - 1, D_RNN), lambda b, l: (b, 0, 0)),
            pl.BlockSpec((1, POOL_HIST, D_POOL), lambda b, l: (b, 0, 0)),
        ],
        out_shape=[
            jax.ShapeDtypeStruct((batch * seq, D_MIX), BF16),
            jax.ShapeDtypeStruct((batch, 1, D_RNN), F32),
            jax.ShapeDtypeStruct((batch, CONV_W - 1, D_RNN), F32),
            jax.ShapeDtypeStruct((batch, POOL_HIST, D_POOL), F32),
        ],
        scratch_shapes=[
            pltpu.VMEM((tm + SUBLANES, D_RNN), F32),
            pltpu.VMEM((tm + 2 * SUBLANES, D_POOL), F32),
            pltpu.VMEM((SUBLANES, D_RNN), F32),
            pltpu.VMEM((N_MEM, D_X), BF16),
            pltpu.VMEM((N_MEM, D_X), BF16),
            pltpu.VMEM((tm, D_RNN), F32),
            pltpu.VMEM((tm, D_RNN), F32),
            pltpu.VMEM((tm, D_RNN), F32),
        ],
        compiler_params=pltpu.CompilerParams(
            dimension_semantics=("arbitrary", "arbitrary"),
            vmem_limit_bytes=VMEM_LIMIT),
        name="prompt_mix",
    )(z, mem_k, mem_v, conv_w, conv_b, wax, b_a, b_x, lam, wpool, pscale)


def _cache_rows(c):
    nb = c.shape[0]
    c = c.reshape(nb, N_MEM, N_XHEADS, XHEAD_DIM // LANES, LANES)
    return c.transpose(0, 1, 3, 2, 4).reshape(nb, N_MEM * SUBLANES, LANES)


def _sample_attn_kernel(q_ref, k_ref, v_ref, o_ref, *, bb):
    halves = XHEAD_DIM // LANES
    assert halves * N_XHEADS == SUBLANES
    r = lax.broadcasted_iota(jnp.int32, (SUBLANES, LANES), 0)
    c = lax.broadcasted_iota(jnp.int32, (SUBLANES, LANES), 1)
    diag = (c % SUBLANES) == r
    first_half = r < N_XHEADS
    nchunk = N_MEM * SUBLANES // LANES
    for j in range(bb):
        qn = jnp.concatenate(
            [q_ref[j:j + 1, (h * halves + t) * LANES:(h * halves + t + 1) * LANES]
             for t in range(halves) for h in range(N_XHEADS)], axis=0)
        s = lax.dot_general(qn.astype(BF16), k_ref[j].astype(BF16), (((1,), (1,)), ((), ())),
                            preferred_element_type=F32) * (XHEAD_DIM ** -0.5)
        chunks = []
        for ci in range(nchunk):
            sm = jnp.where(diag, s[:, ci * LANES:(ci + 1) * LANES], 0.0)
            other = pltpu.roll(sm, N_XHEADS, 0)
            other = jnp.where(first_half, pltpu.roll(other, LANES - N_XHEADS, 1),
                              pltpu.roll(other, N_XHEADS, 1))
            chunks.append(jnp.where(diag, sm + other, -jnp.inf))
        t_full = jnp.concatenate(chunks, axis=1)
        e = jnp.exp(t_full - jnp.max(t_full, axis=1, keepdims=True))
        p = e / jnp.sum(e, axis=1, keepdims=True)
        o = jnp.dot(p.astype(BF16), v_ref[j].astype(BF16), preferred_element_type=F32)
        for t in range(halves):
            for h in range(N_XHEADS):
                col = (h * halves + t) * LANES
                o_ref[j:j + 1, col:col + LANES] = o[t * N_XHEADS + h:t * N_XHEADS + h + 1, :]


def _sample_attn(z, cache_k, cache_v, bb):
    nb = z.shape[0]
    qblk = (2 * D_RNN + 2 * D_POOL) // D_X
    return pl.pallas_call(
        functools.partial(_sample_attn_kernel, bb=bb),
        grid=(nb // bb,),
        in_specs=[
            pl.BlockSpec((bb, D_X), lambda i: (i, qblk)),
            pl.BlockSpec((bb, N_MEM * SUBLANES, LANES), lambda i: (i, 0, 0)),
            pl.BlockSpec((bb, N_MEM * SUBLANES, LANES), lambda i: (i, 0, 0)),
        ],
        out_specs=pl.BlockSpec((bb, D_X), lambda i: (i, 0)),
        out_shape=jax.ShapeDtypeStruct((nb, D_X), F32),
        compiler_params=pltpu.CompilerParams(
            dimension_semantics=("arbitrary",),
            vmem_limit_bytes=VMEM_LIMIT),
        name="sample_attn",
    )(z, cache_k, cache_v)


def _sample_mix_kernel(z_ref, attn_ref, conv_ref, h_ref, pool_ref,
                       convw_ref, convb_ref, wax_ref, ba_ref, bx_ref, lam_ref, wpool_ref,
                       pscale_ref, o_ref, newh_ref, newconv_ref, newpool_ref):
    xr = z_ref[:, 0:D_RNN]
    xc = convb_ref[...] + convw_ref[CONV_W - 1:CONV_W, :] * xr
    for k in range(CONV_W - 1):
        xc = xc + convw_ref[k:k + 1, :] * conv_ref[k]
    for k in range(CONV_W - 2):
        newconv_ref[k] = conv_ref[k + 1]
    newconv_ref[CONV_W - 2] = xr

    sp = _softplus(-lam_ref[...])
    a_parts, b_parts = _rglru_coeffs(xc, wax_ref, ba_ref[...], bx_ref[...], sp)
    for n in range(N_RNN_BLOCKS):
        c0, c1 = n * RNN_BLOCK, (n + 1) * RNN_BLOCK
        h = a_parts[n] * h_ref[:, c0:c1] + b_parts[n]
        newh_ref[:, c0:c1] = h
        o_ref[:, c0:c1] = (h * _silu(z_ref[:, D_RNN + c0:D_RNN + c1])).astype(BF16)

    xp = z_ref[:, 2 * D_RNN:2 * D_RNN + D_POOL]
    for k in range(POOL_HIST - 1):
        newpool_ref[k] = pool_ref[k + 1]
    newpool_ref[POOL_HIST - 1] = xp
    for g, w in enumerate(POOL_WINDOWS):
        c0, c1 = g * POOL_GROUP, (g + 1) * POOL_GROUP
        xg = xp[:, c0:c1]
        tot = xg
        for j in range(1, w):
            tot = tot + pool_ref[POOL_HIST - j, :, c0:c1]
        cnt = float(min(PAST_LEN + 1, w))
        d = tot / cnt - xg
        og = jnp.dot(d.astype(BF16), wpool_ref[g], preferred_element_type=F32)
        gp = z_ref[:, 2 * D_RNN + D_POOL + c0:2 * D_RNN + D_POOL + c1]
        o_ref[:, D_RNN + c0:D_RNN + c1] = (og * pscale_ref[:, c0:c1] * _silu(gp)).astype(BF16)

    gx = z_ref[:, 2 * D_RNN + 2 * D_POOL + D_X:2 * D_MIX]
    o_ref[:, D_RNN + D_POOL:] = (attn_ref[...] * _silu(gx)).astype(BF16)


def _sample_mix(z, attn, conv, h, pool, conv_w, conv_b, wax, b_a, b_x, lam, wpool, pscale, tb):
    nb = z.shape[0]
    zw = 2 * D_MIX
    rows = lambda i: (i, 0)
    const2 = lambda i: (0, 0)
    const3 = lambda i: (0, 0, 0)
    hist = lambda i: (0, i, 0)
    return pl.pallas_call(
        _sample_mix_kernel,
        grid=(nb // tb,),
        in_specs=[
            pl.BlockSpec((tb, zw), rows),
            pl.BlockSpec((tb, D_X), rows),
            pl.BlockSpec((CONV_W - 1, tb, D_RNN), hist),
            pl.BlockSpec((tb, D_RNN), rows),
            pl.BlockSpec((POOL_HIST, tb, D_POOL), hist),
            pl.BlockSpec((CONV_W, D_RNN), const2),
            pl.BlockSpec((1, D_RNN), const2),
            pl.BlockSpec((N_RNN_BLOCKS, RNN_BLOCK, 2 * RNN_BLOCK), const3),
            pl.BlockSpec((1, D_RNN), const2),
            pl.BlockSpec((1, D_RNN), const2),
            pl.BlockSpec((1, D_RNN), const2),
            pl.BlockSpec((len(POOL_WINDOWS), POOL_GROUP, POOL_GROUP), const3),
            pl.BlockSpec((1, D_POOL), const2),
        ],
        out_specs=[
            pl.BlockSpec((tb, D_MIX), rows),
            pl.BlockSpec((tb, D_RNN), rows),
            pl.BlockSpec((CONV_W - 1, tb, D_RNN), hist),
            pl.BlockSpec((POOL_HIST, tb, D_POOL), hist),
        ],
        out_shape=[
            jax.ShapeDtypeStruct((nb, D_MIX), BF16),
            jax.ShapeDtypeStruct((nb, D_RNN), F32),
            jax.ShapeDtypeStruct((CONV_W - 1, nb, D_RNN), F32),
            jax.ShapeDtypeStruct((POOL_HIST, nb, D_POOL), F32),
        ],
        compiler_params=pltpu.CompilerParams(
            dimension_semantics=("arbitrary",),
            vmem_limit_bytes=VMEM_LIMIT),
        name="sample_mix",
    )(z, attn, conv, h, pool, conv_w, conv_b, wax, b_a, b_x, lam, wpool, pscale)


def _branch_out_kernel(o_ref, gates_ref, x_ref, wb_ref, wo_ref, gpost_ref, y_ref):
    merged = None
    for j, (r0, r1) in enumerate(((0, D_RNN), (D_RNN, D_RNN + D_POOL), (D_RNN + D_POOL, D_MIX))):
        yj = jnp.dot(o_ref[:, r0:r1], wb_ref[r0:r1, :], preferred_element_type=F32)
        term = _sigmoid(gates_ref[:, j * D_MODEL:(j + 1) * D_MODEL]) * yj
        merged = term if merged is None else merged + term
    out = jnp.dot(merged.astype(BF16), wo_ref[...], preferred_element_type=F32)
    y_ref[...] = x_ref[...] + out * _rms_scale(out) * gpost_ref[...]


def _branch_out(o, z, x, wb, wo, g_post, tm):
    m = x.shape[0]
    gw = N_BRANCH * D_MODEL
    gblk = (2 * D_MIX) // gw
    resident = pl.Buffered(1)
    return pl.pallas_call(
        _branch_out_kernel,
        grid=(m // tm,),
        in_specs=[
            pl.BlockSpec((tm, D_MIX), lambda i: (i, 0)),
            pl.BlockSpec((tm, gw), lambda i: (i, gblk)),
            pl.BlockSpec((tm, D_MODEL), lambda i: (i, 0)),
            pl.BlockSpec((D_MIX, D_MODEL), lambda i: (0, 0), pipeline_mode=resident),
            pl.BlockSpec((D_MODEL, D_MODEL), lambda i: (0, 0), pipeline_mode=resident),
            pl.BlockSpec((1, D_MODEL), lambda i: (0, 0)),
        ],
        out_specs=pl.BlockSpec((tm, D_MODEL), lambda i: (i, 0)),
        out_shape=jax.ShapeDtypeStruct((m, D_MODEL), F32),
        compiler_params=pltpu.CompilerParams(
            dimension_semantics=("arbitrary",),
            vmem_limit_bytes=VMEM_LIMIT),
        name="branch_out",
    )(o, z, x, wb, wo, g_post)


def kernel(x_prompt, x_sample, mem_prompt, state_rglru_h, state_conv, state_pool, cache_mem_k, cache_mem_v, g_pre, w_in, conv_w, conv_b, w_rg_a, b_rg_a, w_rg_x, b_rg_x, lru_lambda, w_pool, pool_scale, g_mem, w_kv, w_branch, w_out, g_post):
    batch, seq, _ = x_prompt.shape
    nb = x_sample.shape[0]
    depth = g_pre.shape[0]
    assert depth == 1 and x_sample.shape[1] == 1

    l = 0
    row = lambda v: v.reshape(1, -1)
    w_in_b = w_in[l].astype(BF16)
    w_kv_b = w_kv[l].astype(BF16)
    w_br_b = w_branch[l].astype(BF16)
    w_out_b = w_out[l].astype(BF16)
    wax = jnp.concatenate([w_rg_a[l], w_rg_x[l]], axis=-1).astype(BF16)
    wpool = w_pool[l].astype(BF16)
    mix_params = (conv_w[l], row(conv_b[l]), wax, row(b_rg_a[l]), row(b_rg_x[l]),
                  row(lru_lambda[l]), wpool, row(pool_scale[l]))

    xp2 = x_prompt.reshape(batch * seq, D_MODEL)
    xs2 = x_sample.reshape(nb, D_MODEL)
    mem2 = mem_prompt.reshape(batch * N_MEM, D_MODEL)

    kv = _norm_matmul(mem2, row(g_mem[l]), w_kv_b, tm=512, tn=1024)
    mem_k = kv[:, :D_X].reshape(batch, N_MEM, D_X)
    mem_v = kv[:, D_X:].reshape(batch, N_MEM, D_X)

    z_p = _norm_matmul(xp2, row(g_pre[l]), w_in_b, tm=1024, tn=1024)
    o_p, h_p, c_p, p_p = _prompt_mix(z_p, mem_k, mem_v, *mix_params,
                                     batch=batch, seq=seq, tm=256)
    y_p = _branch_out(o_p, z_p, xp2, w_br_b, w_out_b, row(g_post[l]), tm=256)

    z_s = _norm_matmul(xs2, row(g_pre[l]), w_in_b, tm=nb, tn=1024)
    attn_s = _sample_attn(z_s, _cache_rows(cache_mem_k[l]), _cache_rows(cache_mem_v[l]), bb=8)
    o_s, h_s, c_s, p_s = _sample_mix(
        z_s, attn_s, state_conv[l].transpose(1, 0, 2), state_rglru_h[l],
        state_pool[l].transpose(1, 0, 2), *mix_params, tb=32)
    y_s = _branch_out(o_s, z_s, xs2, w_br_b, w_out_b, row(g_post[l]), tm=nb)

    return (
        y_p.reshape(batch, seq, D_MODEL),
        y_s.reshape(nb, 1, D_MODEL),
        h_p.reshape(1, batch, D_RNN),
        c_p.reshape(1, batch, CONV_W - 1, D_RNN),
        p_p.reshape(1, batch, POOL_HIST, D_POOL),
        mem_k.reshape(1, batch, N_MEM, N_XHEADS, XHEAD_DIM),
        mem_v.reshape(1, batch, N_MEM, N_XHEADS, XHEAD_DIM),
        h_s.reshape(1, nb, D_RNN),
        c_s.transpose(1, 0, 2)[None],
        p_s.transpose(1, 0, 2)[None],
    )
```

```python
import functools

import jax
import jax.numpy as jnp
from jax import lax
from jax.experimental import pallas as pl
from jax.experimental.pallas import tpu as pltpu

D_MODEL = 2048
PAST_LEN = 16384
D_RNN = 1024
N_RNN_BLOCKS = 8
RNN_BLOCK = D_RNN // N_RNN_BLOCKS
CONV_W = 4
LRU_C = 8.0
D_POOL = 1024
POOL_WINDOWS = (2, 4, 8, 16)
POOL_GROUP = D_POOL // len(POOL_WINDOWS)
POOL_HIST = max(POOL_WINDOWS) - 1
N_MEM = 256
N_XHEADS = 4
XHEAD_DIM = 256
D_X = N_XHEADS * XHEAD_DIM
N_BRANCH = 3
D_MIX = D_RNN + D_POOL + D_X
D_IN = 2 * D_MIX + N_BRANCH * D_MODEL
EPS = 1e-6

SUBLANES = 8
LANES = 128
VMEM_LIMIT = 56 * 1024 * 1024

BF16 = jnp.bfloat16
F32 = jnp.float32


def _sigmoid(x):
    return 1.0 / (1.0 + jnp.exp(-x))


def _silu(x):
    return x * _sigmoid(x)


def _softplus(x):
    return jnp.maximum(x, 0.0) + jnp.log1p(jnp.exp(-jnp.abs(x)))


def _rms_scale(x):
    return lax.rsqrt(jnp.mean(x * x, axis=-1, keepdims=True) + EPS)


def _norm_matmul_kernel(x_ref, g_ref, w_ref, o_ref, *rest, emit_w):
    u_ref = rest[-1]

    @pl.when(pl.program_id(1) == 0)
    def _():
        x = x_ref[...]
        u_ref[...] = (x * _rms_scale(x) * g_ref[...]).astype(BF16)

    w = w_ref[...].astype(BF16)
    if emit_w:
        rest[0][...] = w
    o_ref[...] = jnp.dot(u_ref[...], w, preferred_element_type=F32)


def _norm_matmul(x, g, w, tm, tn, emit_w=False):
    m, k = x.shape
    n = w.shape[1]
    out_specs = [pl.BlockSpec((tm, tn), lambda i, j: (i, j))]
    out_shape = [jax.ShapeDtypeStruct((m, n), F32)]
    if emit_w:
        assert m == tm, "the bf16 weight copy is written once per column block"
        out_specs.append(pl.BlockSpec((k, tn), lambda i, j: (0, j)))
        out_shape.append(jax.ShapeDtypeStruct((k, n), BF16))
    res = pl.pallas_call(
        functools.partial(_norm_matmul_kernel, emit_w=emit_w),
        grid=(m // tm, n // tn),
        in_specs=[
            pl.BlockSpec((tm, k), lambda i, j: (i, 0)),
            pl.BlockSpec((1, k), lambda i, j: (0, 0)),
            pl.BlockSpec((k, tn), lambda i, j: (0, j)),
        ],
        out_specs=out_specs,
        out_shape=out_shape,
        scratch_shapes=[pltpu.VMEM((tm, k), BF16)],
        compiler_params=pltpu.CompilerParams(
            dimension_semantics=("arbitrary", "arbitrary"),
            vmem_limit_bytes=VMEM_LIMIT),
        name="norm_matmul",
    )(x, g, w)
    return res if emit_w else res[0]


def _rglru_block(xc, wax, ba, bx, sp):
    ri = jnp.dot(xc.astype(BF16), wax, preferred_element_type=F32)
    r = _sigmoid(ri[:, :RNN_BLOCK] + ba)
    i = _sigmoid(ri[:, RNN_BLOCK:] + bx)
    a = jnp.exp((-LRU_C) * r * sp)
    one_m = 1.0 - a * a
    mult = jnp.where(one_m > 0.0, one_m * lax.rsqrt(one_m), 0.0)
    return a, mult * i * xc


RPAD = SUBLANES
PPAD = 3 * SUBLANES


def _prompt_mix_kernel(z_ref, k_ref, v_ref, convw_ref, convb_ref, wax_ref, ba_ref, bx_ref,
                       lam_ref, wpool_ref, pscale_ref,
                       o_ref, newh_ref, newconv_ref, newpool_ref,
                       ext_r, ext_p, s2_scr, s4_scr, s8_scr, h_carry, kb_ref, vb_ref,
                       a_scr, b_scr, h_scr, *, tm):
    l = pl.program_id(1)
    last = pl.num_programs(1) - 1

    @pl.when(l == 0)
    def _():
        ext_r[0:RPAD, :] = jnp.zeros((RPAD, D_RNN), F32)
        ext_p[0:PPAD, :] = jnp.zeros((PPAD, D_POOL), F32)
        for scr in (s2_scr, s4_scr, s8_scr):
            scr[0:SUBLANES, :] = jnp.zeros((SUBLANES, scr.shape[1]), F32)
        h_carry[...] = jnp.zeros((SUBLANES, D_RNN), F32)
        kb_ref[...] = k_ref[0].astype(BF16)
        vb_ref[...] = v_ref[0].astype(BF16)

    ext_r[RPAD:RPAD + tm, :] = z_ref[:, 0:D_RNN]
    sp = _softplus(-lam_ref[...])
    for n in range(N_RNN_BLOCKS):
        c0, c1 = n * RNN_BLOCK, (n + 1) * RNN_BLOCK
        xc = convb_ref[:, c0:c1] + convw_ref[CONV_W - 1:CONV_W, c0:c1] * ext_r[RPAD:RPAD + tm, c0:c1]
        for k in range(CONV_W - 1):
            sh = CONV_W - 1 - k
            xc = xc + convw_ref[k:k + 1, c0:c1] * ext_r[RPAD - sh:RPAD - sh + tm, c0:c1]
        a, b = _rglru_block(xc, wax_ref[n], ba_ref[:, c0:c1], bx_ref[:, c0:c1], sp[:, c0:c1])
        a_scr[:, c0:c1] = a
        b_scr[:, c0:c1] = b

    row = lax.broadcasted_iota(jnp.int32, (SUBLANES, D_RNN), 0)

    def scan_group(g, hprev):
        off = pl.multiple_of(g * SUBLANES, SUBLANES)
        a = a_scr[pl.ds(off, SUBLANES), :]
        b = b_scr[pl.ds(off, SUBLANES), :]
        for s in (1, 2, 4):
            a_sh = jnp.where(row >= s, pltpu.roll(a, s, 0), 1.0)
            b_sh = jnp.where(row >= s, pltpu.roll(b, s, 0), 0.0)
            b = a * b_sh + b
            a = a * a_sh
        h = a * hprev + b
        h_scr[pl.ds(off, SUBLANES), :] = h
        return jnp.broadcast_to(h[SUBLANES - 1:SUBLANES, :], (SUBLANES, D_RNN))

    h_last = lax.fori_loop(0, tm // SUBLANES, scan_group, h_carry[...])
    h_carry[...] = h_last
    o_ref[:, 0:D_RNN] = (h_scr[...] * _silu(z_ref[:, D_RNN:2 * D_RNN])).astype(BF16)

    ext_p[PPAD:PPAD + tm, :] = z_ref[:, 2 * D_RNN:2 * D_RNN + D_POOL]
    lo, hi, pg = SUBLANES, PPAD + tm, POOL_GROUP
    s2_scr[lo:hi, :] = ext_p[lo:hi, pg:] + ext_p[lo - 1:hi - 1, pg:]
    s4_scr[lo:hi, :] = s2_scr[lo:hi, pg:] + s2_scr[lo - 2:hi - 2, pg:]
    s8_scr[lo:hi, :] = s4_scr[lo:hi, pg:] + s4_scr[lo - 4:hi - 4, pg:]
    totals = (
        ext_p[PPAD:hi, 0:pg] + ext_p[PPAD - 1:hi - 1, 0:pg],
        s2_scr[PPAD:hi, 0:pg] + s2_scr[PPAD - 2:hi - 2, 0:pg],
        s4_scr[PPAD:hi, 0:pg] + s4_scr[PPAD - 4:hi - 4, 0:pg],
        s8_scr[PPAD:hi, 0:pg] + s8_scr[PPAD - 8:hi - 8, 0:pg],
    )
    head = 2 * SUBLANES
    pos1 = l * tm + lax.broadcasted_iota(jnp.int32, (head, POOL_GROUP), 0) + 1
    for g, w in enumerate(POOL_WINDOWS):
        c0, c1 = g * POOL_GROUP, (g + 1) * POOL_GROUP
        xg = ext_p[PPAD:hi, c0:c1]
        tot = totals[g]
        mean = jnp.concatenate(
            [tot[0:head] / jnp.minimum(pos1, w).astype(F32), tot[head:] * (1.0 / w)], axis=0)
        og = jnp.dot((mean - xg).astype(BF16), wpool_ref[g], preferred_element_type=F32)
        gp = z_ref[:, 2 * D_RNN + D_POOL + c0:2 * D_RNN + D_POOL + c1]
        o_ref[:, D_RNN + c0:D_RNN + c1] = (og * pscale_ref[:, c0:c1] * _silu(gp)).astype(BF16)

    qoff = 2 * D_RNN + 2 * D_POOL
    for hd in range(N_XHEADS):
        c0, c1 = hd * XHEAD_DIM, (hd + 1) * XHEAD_DIM
        q = z_ref[:, qoff + c0:qoff + c1].astype(BF16)
        s = lax.dot_general(q, kb_ref[:, c0:c1], (((1,), (1,)), ((), ())),
                            preferred_element_type=F32) * (XHEAD_DIM ** -0.5)
        p = jnp.exp(s - jnp.max(s, axis=-1, keepdims=True))
        p = p / jnp.sum(p, axis=-1, keepdims=True)
        ox = jnp.dot(p.astype(BF16), vb_ref[:, c0:c1], preferred_element_type=F32)
        gx = z_ref[:, qoff + D_X + c0:qoff + D_X + c1]
        o_ref[:, D_RNN + D_POOL + c0:D_RNN + D_POOL + c1] = (ox * _silu(gx)).astype(BF16)

    @pl.when(l == last)
    def _():
        newh_ref[0] = h_scr[tm - 1:tm, :]
        newconv_ref[0] = ext_r[RPAD + tm - (CONV_W - 1):RPAD + tm, :]
        newpool_ref[0] = ext_p[PPAD + tm - POOL_HIST:PPAD + tm, :]

    ext_r[0:RPAD, :] = ext_r[tm:tm + RPAD, :]
    ext_p[SUBLANES:PPAD, :] = ext_p[tm + SUBLANES:tm + PPAD, :]


def _prompt_mix(z, mem_k, mem_v, conv_w, conv_b, wax, b_a, b_x, lam, wpool, pscale,
                batch, seq, tm):
    nl = seq // tm
    zw = 2 * D_MIX
    const2 = lambda b, l: (0, 0)
    const3 = lambda b, l: (0, 0, 0)
    kern = functools.partial(_prompt_mix_kernel, tm=tm)
    return pl.pallas_call(
        kern,
        grid=(batch, nl),
        in_specs=[
            pl.BlockSpec((tm, zw), lambda b, l: (b * nl + l, 0)),
            pl.BlockSpec((1, N_MEM, D_X), lambda b, l: (b, 0, 0)),
            pl.BlockSpec((1, N_MEM, D_X), lambda b, l: (b, 0, 0)),
            pl.BlockSpec((CONV_W, D_RNN), const2),
            pl.BlockSpec((1, D_RNN), const2),
            pl.BlockSpec((N_RNN_BLOCKS, RNN_BLOCK, 2 * RNN_BLOCK), const3),
            pl.BlockSpec((1, D_RNN), const2),
            pl.BlockSpec((1, D_RNN), const2),
            pl.BlockSpec((1, D_RNN), const2),
            pl.BlockSpec((len(POOL_WINDOWS), POOL_GROUP, POOL_GROUP), const3),
            pl.BlockSpec((1, D_POOL), const2),
        ],
        out_specs=[
            pl.BlockSpec((tm, D_MIX), lambda b, l: (b * nl + l, 0)),
            pl.BlockSpec((1, 1, D_RNN), lambda b, l: (b, 0, 0)),
            pl.BlockSpec((1, CONV_W - 1, D_RNN), lambda b, l: (b, 0, 0)),
            pl.BlockSpec((1, POOL_HIST, D_POOL), lambda b, l: (b, 0, 0)),
        ],
        out_shape=[
            jax.ShapeDtypeStruct((batch * seq, D_MIX), BF16),
            jax.ShapeDtypeStruct((batch, 1, D_RNN), F32),
            jax.ShapeDtypeStruct((batch, CONV_W - 1, D_RNN), F32),
            jax.ShapeDtypeStruct((batch, POOL_HIST, D_POOL), F32),
        ],
        scratch_shapes=[
            pltpu.VMEM((tm + RPAD, D_RNN), F32),
            pltpu.VMEM((tm + PPAD, D_POOL), F32),
            pltpu.VMEM((tm + PPAD, D_POOL - POOL_GROUP), F32),
            pltpu.VMEM((tm + PPAD, D_POOL - 2 * POOL_GROUP), F32),
            pltpu.VMEM((tm + PPAD, D_POOL - 3 * POOL_GROUP), F32),
            pltpu.VMEM((SUBLANES, D_RNN), F32),
            pltpu.VMEM((N_MEM, D_X), BF16),
            pltpu.VMEM((N_MEM, D_X), BF16),
            pltpu.VMEM((tm, D_RNN), F32),
            pltpu.VMEM((tm, D_RNN), F32),
            pltpu.VMEM((tm, D_RNN), F32),
        ],
        compiler_params=pltpu.CompilerParams(
            dimension_semantics=("arbitrary", "arbitrary"),
            vmem_limit_bytes=VMEM_LIMIT),
        name="prompt_mix",
    )(z, mem_k, mem_v, conv_w, conv_b, wax, b_a, b_x, lam, wpool, pscale)


def _cache_rows(c):
    nb = c.shape[0]
    c = c.reshape(nb, N_MEM, N_XHEADS, XHEAD_DIM // LANES, LANES)
    return c.transpose(0, 1, 3, 2, 4).reshape(nb, N_MEM * SUBLANES, LANES)


def _sample_attn_kernel(q_ref, k_ref, v_ref, o_ref, *, bb):
    halves = XHEAD_DIM // LANES
    assert halves * N_XHEADS == SUBLANES
    r = lax.broadcasted_iota(jnp.int32, (SUBLANES, LANES), 0)
    c = lax.broadcasted_iota(jnp.int32, (SUBLANES, LANES), 1)
    diag = (c % SUBLANES) == r
    first_half = r < N_XHEADS
    nchunk = N_MEM * SUBLANES // LANES
    for j in range(bb):
        qn = jnp.concatenate(
            [q_ref[j:j + 1, (h * halves + t) * LANES:(h * halves + t + 1) * LANES]
             for t in range(halves) for h in range(N_XHEADS)], axis=0)
        s = lax.dot_general(qn.astype(BF16), k_ref[j].astype(BF16), (((1,), (1,)), ((), ())),
                            preferred_element_type=F32) * (XHEAD_DIM ** -0.5)
        chunks = []
        for ci in range(nchunk):
            sm = jnp.where(diag, s[:, ci * LANES:(ci + 1) * LANES], 0.0)
            other = pltpu.roll(sm, N_XHEADS, 0)
            other = jnp.where(first_half, pltpu.roll(other, LANES - N_XHEADS, 1),
                              pltpu.roll(other, N_XHEADS, 1))
            chunks.append(jnp.where(diag, sm + other, -jnp.inf))
        t_full = jnp.concatenate(chunks, axis=1)
        e = jnp.exp(t_full - jnp.max(t_full, axis=1, keepdims=True))
        p = e / jnp.sum(e, axis=1, keepdims=True)
        o = jnp.dot(p.astype(BF16), v_ref[j].astype(BF16), preferred_element_type=F32)
        for t in range(halves):
            for h in range(N_XHEADS):
                col = (h * halves + t) * LANES
                o_ref[j:j + 1, col:col + LANES] = o[t * N_XHEADS + h:t * N_XHEADS + h + 1, :]


def _sample_attn(z, cache_k, cache_v, bb):
    nb = z.shape[0]
    qblk = (2 * D_RNN + 2 * D_POOL) // D_X
    return pl.pallas_call(
        functools.partial(_sample_attn_kernel, bb=bb),
        grid=(nb // bb,),
        in_specs=[
            pl.BlockSpec((bb, D_X), lambda i: (i, qblk)),
            pl.BlockSpec((bb, N_MEM * SUBLANES, LANES), lambda i: (i, 0, 0)),
            pl.BlockSpec((bb, N_MEM * SUBLANES, LANES), lambda i: (i, 0, 0)),
        ],
        out_specs=pl.BlockSpec((bb, D_X), lambda i: (i, 0)),
        out_shape=jax.ShapeDtypeStruct((nb, D_X), F32),
        compiler_params=pltpu.CompilerParams(
            dimension_semantics=("arbitrary",),
            vmem_limit_bytes=VMEM_LIMIT),
        name="sample_attn",
    )(z, cache_k, cache_v)


def _sample_mix_kernel(z_ref, attn_ref, conv_ref, h_ref, pool_ref,
                       convw_ref, convb_ref, wax_ref, ba_ref, bx_ref, lam_ref, wpool_ref,
                       pscale_ref, o_ref, newh_ref, newconv_ref, newpool_ref):
    xr = z_ref[:, 0:D_RNN]
    xc = convb_ref[...] + convw_ref[CONV_W - 1:CONV_W, :] * xr
    for k in range(CONV_W - 1):
        xc = xc + convw_ref[k:k + 1, :] * conv_ref[k]
    for k in range(CONV_W - 2):
        newconv_ref[k] = conv_ref[k + 1]
    newconv_ref[CONV_W - 2] = xr

    sp = _softplus(-lam_ref[...])
    for n in range(N_RNN_BLOCKS):
        c0, c1 = n * RNN_BLOCK, (n + 1) * RNN_BLOCK
        a, b = _rglru_block(xc[:, c0:c1], wax_ref[n], ba_ref[:, c0:c1], bx_ref[:, c0:c1],
                            sp[:, c0:c1])
        h = a * h_ref[:, c0:c1] + b
        newh_ref[:, c0:c1] = h
        o_ref[:, c0:c1] = (h * _silu(z_ref[:, D_RNN + c0:D_RNN + c1])).astype(BF16)

    xp = z_ref[:, 2 * D_RNN:2 * D_RNN + D_POOL]
    for k in range(POOL_HIST - 1):
        newpool_ref[k] = pool_ref[k + 1]
    newpool_ref[POOL_HIST - 1] = xp
    for g, w in enumerate(POOL_WINDOWS):
        c0, c1 = g * POOL_GROUP, (g + 1) * POOL_GROUP
        xg = xp[:, c0:c1]
        tot = xg
        for j in range(1, w):
            tot = tot + pool_ref[POOL_HIST - j, :, c0:c1]
        cnt = float(min(PAST_LEN + 1, w))
        d = tot / cnt - xg
        og = jnp.dot(d.astype(BF16), wpool_ref[g], preferred_element_type=F32)
        gp = z_ref[:, 2 * D_RNN + D_POOL + c0:2 * D_RNN + D_POOL + c1]
        o_ref[:, D_RNN + c0:D_RNN + c1] = (og * pscale_ref[:, c0:c1] * _silu(gp)).astype(BF16)

    gx = z_ref[:, 2 * D_RNN + 2 * D_POOL + D_X:2 * D_MIX]
    o_ref[:, D_RNN + D_POOL:] = (attn_ref[...] * _silu(gx)).astype(BF16)


def _sample_mix(z, attn, conv, h, pool, conv_w, conv_b, wax, b_a, b_x, lam, wpool, pscale, tb):
    nb = z.shape[0]
    zw = 2 * D_MIX
    rows = lambda i: (i, 0)
    const2 = lambda i: (0, 0)
    const3 = lambda i: (0, 0, 0)
    hist = lambda i: (0, i, 0)
    return pl.pallas_call(
        _sample_mix_kernel,
        grid=(nb // tb,),
        in_specs=[
            pl.BlockSpec((tb, zw), rows),
            pl.BlockSpec((tb, D_X), rows),
            pl.BlockSpec((CONV_W - 1, tb, D_RNN), hist),
            pl.BlockSpec((tb, D_RNN), rows),
            pl.BlockSpec((POOL_HIST, tb, D_POOL), hist),
            pl.BlockSpec((CONV_W, D_RNN), const2),
            pl.BlockSpec((1, D_RNN), const2),
            pl.BlockSpec((N_RNN_BLOCKS, RNN_BLOCK, 2 * RNN_BLOCK), const3),
            pl.BlockSpec((1, D_RNN), const2),
            pl.BlockSpec((1, D_RNN), const2),
            pl.BlockSpec((1, D_RNN), const2),
            pl.BlockSpec((len(POOL_WINDOWS), POOL_GROUP, POOL_GROUP), const3),
            pl.BlockSpec((1, D_POOL), const2),
        ],
        out_specs=[
            pl.BlockSpec((tb, D_MIX), rows),
            pl.BlockSpec((tb, D_RNN), rows),
            pl.BlockSpec((CONV_W - 1, tb, D_RNN), hist),
            pl.BlockSpec((POOL_HIST, tb, D_POOL), hist),
        ],
        out_shape=[
            jax.ShapeDtypeStruct((nb, D_MIX), BF16),
            jax.ShapeDtypeStruct((nb, D_RNN), F32),
            jax.ShapeDtypeStruct((CONV_W - 1, nb, D_RNN), F32),
            jax.ShapeDtypeStruct((POOL_HIST, nb, D_POOL), F32),
        ],
        compiler_params=pltpu.CompilerParams(
            dimension_semantics=("arbitrary",),
            vmem_limit_bytes=VMEM_LIMIT),
        name="sample_mix",
    )(z, attn, conv, h, pool, conv_w, conv_b, wax, b_a, b_x, lam, wpool, pscale)


def _branch_out_kernel(o_ref, gates_ref, x_ref, wb_ref, wo_ref, gpost_ref, y_ref):
    merged = None
    for j, (r0, r1) in enumerate(((0, D_RNN), (D_RNN, D_RNN + D_POOL), (D_RNN + D_POOL, D_MIX))):
        yj = jnp.dot(o_ref[:, r0:r1], wb_ref[r0:r1, :], preferred_element_type=F32)
        term = _sigmoid(gates_ref[:, j * D_MODEL:(j + 1) * D_MODEL]) * yj
        merged = term if merged is None else merged + term
    out = jnp.dot(merged.astype(BF16), wo_ref[...], preferred_element_type=F32)
    y_ref[...] = x_ref[...] + out * _rms_scale(out) * gpost_ref[...]


def _branch_out(o, z, x, wb, wo, g_post, tm):
    m = x.shape[0]
    gw = N_BRANCH * D_MODEL
    gblk = (2 * D_MIX) // gw
    resident = pl.Buffered(1)
    return pl.pallas_call(
        _branch_out_kernel,
        grid=(m // tm,),
        in_specs=[
            pl.BlockSpec((tm, D_MIX), lambda i: (i, 0)),
            pl.BlockSpec((tm, gw), lambda i: (i, gblk)),
            pl.BlockSpec((tm, D_MODEL), lambda i: (i, 0)),
            pl.BlockSpec((D_MIX, D_MODEL), lambda i: (0, 0), pipeline_mode=resident),
            pl.BlockSpec((D_MODEL, D_MODEL), lambda i: (0, 0), pipeline_mode=resident),
            pl.BlockSpec((1, D_MODEL), lambda i: (0, 0)),
        ],
        out_specs=pl.BlockSpec((tm, D_MODEL), lambda i: (i, 0)),
        out_shape=jax.ShapeDtypeStruct((m, D_MODEL), F32),
        compiler_params=pltpu.CompilerParams(
            dimension_semantics=("arbitrary",),
            vmem_limit_bytes=VMEM_LIMIT),
        name="branch_out",
    )(o, z, x, wb, wo, g_post)


def kernel(x_prompt, x_sample, mem_prompt, state_rglru_h, state_conv, state_pool, cache_mem_k, cache_mem_v, g_pre, w_in, conv_w, conv_b, w_rg_a, b_rg_a, w_rg_x, b_rg_x, lru_lambda, w_pool, pool_scale, g_mem, w_kv, w_branch, w_out, g_post):
    batch, seq, _ = x_prompt.shape
    nb = x_sample.shape[0]
    depth = g_pre.shape[0]
    assert depth == 1 and x_sample.shape[1] == 1

    l = 0
    row = lambda v: v.reshape(1, -1)
    w_br_b = w_branch[l].astype(BF16)
    w_out_b = w_out[l].astype(BF16)
    wax = jnp.concatenate([w_rg_a[l], w_rg_x[l]], axis=-1).astype(BF16)
    wpool = w_pool[l].astype(BF16)
    mix_params = (conv_w[l], row(conv_b[l]), wax, row(b_rg_a[l]), row(b_rg_x[l]),
                  row(lru_lambda[l]), wpool, row(pool_scale[l]))

    xp2 = x_prompt.reshape(batch * seq, D_MODEL)
    xs2 = x_sample.reshape(nb, D_MODEL)
    mem2 = mem_prompt.reshape(batch * N_MEM, D_MODEL)

    z_s, w_in_b = _norm_matmul(xs2, row(g_pre[l]), w_in[l], tm=nb, tn=1024, emit_w=True)
    attn_s = _sample_attn(z_s, _cache_rows(cache_mem_k[l]), _cache_rows(cache_mem_v[l]), bb=8)
    o_s, h_s, c_s, p_s = _sample_mix(
        z_s, attn_s, state_conv[l].transpose(1, 0, 2), state_rglru_h[l],
        state_pool[l].transpose(1, 0, 2), *mix_params, tb=32)
    y_s = _branch_out(o_s, z_s, xs2, w_br_b, w_out_b, row(g_post[l]), tm=nb)

    kv = _norm_matmul(mem2, row(g_mem[l]), w_kv[l], tm=512, tn=1024)
    mem_k = kv[:, :D_X].reshape(batch, N_MEM, D_X)
    mem_v = kv[:, D_X:].reshape(batch, N_MEM, D_X)

    z_p = _norm_matmul(xp2, row(g_pre[l]), w_in_b, tm=1024, tn=1024)
    o_p, h_p, c_p, p_p = _prompt_mix(z_p, mem_k, mem_v, *mix_params,
                                     batch=batch, seq=seq, tm=256)
    y_p = _branch_out(o_p, z_p, xp2, w_br_b, w_out_b, row(g_post[l]), tm=256)

    return (
        y_p.reshape(batch, seq, D_MODEL),
        y_s.reshape(nb, 1, D_MODEL),
        h_p.reshape(1, batch, D_RNN),
        c_p.reshape(1, batch, CONV_W - 1, D_RNN),
        p_p.reshape(1, batch, POOL_HIST, D_POOL),
        mem_k.reshape(1, batch, N_MEM, N_XHEADS, XHEAD_DIM),
        mem_v.reshape(1, batch, N_MEM, N_XHEADS, XHEAD_DIM),
        h_s.reshape(1, nb, D_RNN),
        c_s.transpose(1, 0, 2)[None],
        p_s.transpose(1, 0, 2)[None],
    )
```

```python
import functools

import jax
import jax.numpy as jnp
from jax import lax
from jax.experimental import pallas as pl
from jax.experimental.pallas import tpu as pltpu

D_MODEL = 2048
PAST_LEN = 16384
D_RNN = 1024
N_RNN_BLOCKS = 8
RNN_BLOCK = D_RNN // N_RNN_BLOCKS
CONV_W = 4
LRU_C = 8.0
D_POOL = 1024
POOL_WINDOWS = (2, 4, 8, 16)
POOL_GROUP = D_POOL // len(POOL_WINDOWS)
POOL_HIST = max(POOL_WINDOWS) - 1
N_MEM = 256
N_XHEADS = 4
XHEAD_DIM = 256
D_X = N_XHEADS * XHEAD_DIM
N_BRANCH = 3
D_MIX = D_RNN + D_POOL + D_X
D_IN = 2 * D_MIX + N_BRANCH * D_MODEL
EPS = 1e-6

SUBLANES = 8
LANES = 128
VMEM_LIMIT = 56 * 1024 * 1024

BF16 = jnp.bfloat16
F32 = jnp.float32


def _sigmoid(x):
    return 1.0 / (1.0 + jnp.exp(-x))


def _silu(x):
    return x * _sigmoid(x)


def _softplus(x):
    return jnp.maximum(x, 0.0) + jnp.log1p(jnp.exp(-jnp.abs(x)))


def _rms_scale(x):
    return lax.rsqrt(jnp.mean(x * x, axis=-1, keepdims=True) + EPS)


def _norm_matmul_kernel(x_ref, g_ref, w_ref, o_ref, *rest, emit_w):
    u_ref = rest[-1]

    @pl.when(pl.program_id(1) == 0)
    def _():
        x = x_ref[...]
        u_ref[...] = (x * _rms_scale(x) * g_ref[...]).astype(BF16)

    w = w_ref[...].astype(BF16)
    if emit_w:
        rest[0][...] = w
    o_ref[...] = jnp.dot(u_ref[...], w, preferred_element_type=F32)


def _norm_matmul(x, g, w, tm, tn, emit_w=False):
    m, k = x.shape
    n = w.shape[1]
    out_specs = [pl.BlockSpec((tm, tn), lambda i, j: (i, j))]
    out_shape = [jax.ShapeDtypeStruct((m, n), F32)]
    if emit_w:
        assert m == tm, "the bf16 weight copy is written once per column block"
        out_specs.append(pl.BlockSpec((k, tn), lambda i, j: (0, j)))
        out_shape.append(jax.ShapeDtypeStruct((k, n), BF16))
    res = pl.pallas_call(
        functools.partial(_norm_matmul_kernel, emit_w=emit_w),
        grid=(m // tm, n // tn),
        in_specs=[
            pl.BlockSpec((tm, k), lambda i, j: (i, 0)),
            pl.BlockSpec((1, k), lambda i, j: (0, 0)),
            pl.BlockSpec((k, tn), lambda i, j: (0, j)),
        ],
        out_specs=out_specs,
        out_shape=out_shape,
        scratch_shapes=[pltpu.VMEM((tm, k), BF16)],
        compiler_params=pltpu.CompilerParams(
            dimension_semantics=("arbitrary", "arbitrary"),
            vmem_limit_bytes=VMEM_LIMIT),
        name="norm_matmul",
    )(x, g, w)
    return res if emit_w else res[0]


def _rglru_block(xc, wax, ba, bx, sp):
    ri = jnp.dot(xc.astype(BF16), wax, preferred_element_type=F32)
    return _rglru_gates(xc, ri, ba, bx, sp)


def _rglru_gates(xc, ri, ba, bx, sp):
    r = _sigmoid(ri[:, :RNN_BLOCK] + ba)
    i = _sigmoid(ri[:, RNN_BLOCK:] + bx)
    a = jnp.exp((-LRU_C) * r * sp)
    one_m = 1.0 - a * a
    mult = jnp.where(one_m > 0.0, one_m * lax.rsqrt(one_m), 0.0)
    return a, mult * i * xc


RPAD = SUBLANES
PPAD = 3 * SUBLANES


PROJ_TM = 512
PROJ_TN = 1024
SUB = 256
NZ = 2 * D_MIX // PROJ_TN


def _proj_mix_kernel(x_ref, g_ref, w_ref, k_ref, v_ref, convw_ref, convb_ref, wax_ref, ba_ref,
                     bx_ref, lam_ref, wpool_ref, pscale_ref,
                     gates_ref, o_ref, newh_ref, newconv_ref, newpool_ref,
                     u_ref, zbuf, ext_r, ext_p, s2_scr, s4_scr, s8_scr, h_carry, kb_ref, vb_ref,
                     xc_scr, ri_scr, *, tiles_per_seq):
    i = pl.program_id(0)
    j = pl.program_id(1)
    seq_first = (i % tiles_per_seq) == 0
    seq_last = (i % tiles_per_seq) == tiles_per_seq - 1
    nsub = PROJ_TM // SUB

    @pl.when(j == 0)
    def _():
        x = x_ref[...]
        u_ref[...] = (x * _rms_scale(x) * g_ref[...]).astype(BF16)

    @pl.when(jnp.logical_and(j == 0, seq_first))
    def _():
        ext_r[0:RPAD, :] = jnp.zeros((RPAD, D_RNN), F32)
        ext_p[0:PPAD, :] = jnp.zeros((PPAD, D_POOL), F32)
        for scr in (s2_scr, s4_scr, s8_scr):
            scr[0:SUBLANES, :] = jnp.zeros((SUBLANES, scr.shape[1]), F32)
        h_carry[...] = jnp.zeros((SUBLANES, D_RNN), F32)
        kb_ref[...] = k_ref[0].astype(BF16)
        vb_ref[...] = v_ref[0].astype(BF16)

    def project(dst, col0=0):
        for r0 in range(0, PROJ_TM, SUB):
            dst[r0:r0 + SUB, col0:col0 + PROJ_TN] = jnp.dot(
                u_ref[r0:r0 + SUB, :], w_ref[...], preferred_element_type=F32)

    def rglru_pre(s):
        r0 = s * SUB
        ext_r[RPAD:RPAD + SUB, :] = zbuf[r0:r0 + SUB, 0:D_RNN]
        for n in range(N_RNN_BLOCKS):
            c0, c1 = n * RNN_BLOCK, (n + 1) * RNN_BLOCK
            xc = (convb_ref[:, c0:c1]
                  + convw_ref[CONV_W - 1:CONV_W, c0:c1] * ext_r[RPAD:RPAD + SUB, c0:c1])
            for k in range(CONV_W - 1):
                sh = CONV_W - 1 - k
                xc = xc + convw_ref[k:k + 1, c0:c1] * ext_r[RPAD - sh:RPAD - sh + SUB, c0:c1]
            xc_scr[:, c0:c1] = xc
            ri_scr[:, 2 * c0:2 * c1] = jnp.dot(xc.astype(BF16), wax_ref[n],
                                               preferred_element_type=F32)

    def rglru_post(s):
        r0 = s * SUB
        sp = _softplus(-lam_ref[...])
        row = lax.broadcasted_iota(jnp.int32, (SUBLANES, RNN_BLOCK), 0)
        for n in range(N_RNN_BLOCKS):
            c0, c1 = n * RNN_BLOCK, (n + 1) * RNN_BLOCK
            a, b = _rglru_gates(xc_scr[:, c0:c1], ri_scr[:, 2 * c0:2 * c1], ba_ref[:, c0:c1],
                                bx_ref[:, c0:c1], sp[:, c0:c1])
            hprev = h_carry[:, c0:c1]
            hs = []
            for g in range(SUB // SUBLANES):
                ag = a[g * SUBLANES:(g + 1) * SUBLANES]
                bg = b[g * SUBLANES:(g + 1) * SUBLANES]
                for st in (1, 2, 4):
                    a_sh = jnp.where(row >= st, pltpu.roll(ag, st, 0), 1.0)
                    b_sh = jnp.where(row >= st, pltpu.roll(bg, st, 0), 0.0)
                    bg = ag * b_sh + bg
                    ag = ag * a_sh
                h = ag * hprev + bg
                hs.append(h)
                hprev = jnp.broadcast_to(h[SUBLANES - 1:SUBLANES, :], (SUBLANES, RNN_BLOCK))
            h_carry[:, c0:c1] = hprev
            gr = zbuf[r0:r0 + SUB, D_RNN + c0:D_RNN + c1]
            o_ref[r0:r0 + SUB, c0:c1] = (jnp.concatenate(hs, axis=0) * _silu(gr)).astype(BF16)
        if s == nsub - 1:
            @pl.when(seq_last)
            def _():
                newh_ref[0] = h_carry[0:1, :]
                newconv_ref[0] = ext_r[RPAD + SUB - (CONV_W - 1):RPAD + SUB, :]
        ext_r[0:RPAD, :] = ext_r[SUB:SUB + RPAD, :]

    def pool_pre(s):
        del s

    def pool_post(s):
        r0 = s * SUB
        ext_p[PPAD:PPAD + SUB, :] = zbuf[r0:r0 + SUB, 2 * D_RNN:2 * D_RNN + D_POOL]
        lo, hi, pg = SUBLANES, PPAD + SUB, POOL_GROUP
        s2_scr[lo:hi, :] = ext_p[lo:hi, pg:] + ext_p[lo - 1:hi - 1, pg:]
        s4_scr[lo:hi, :] = s2_scr[lo:hi, pg:] + s2_scr[lo - 2:hi - 2, pg:]
        s8_scr[lo:hi, :] = s4_scr[lo:hi, pg:] + s4_scr[lo - 4:hi - 4, pg:]
        totals = (
            ext_p[PPAD:hi, 0:pg] + ext_p[PPAD - 1:hi - 1, 0:pg],
            s2_scr[PPAD:hi, 0:pg] + s2_scr[PPAD - 2:hi - 2, 0:pg],
            s4_scr[PPAD:hi, 0:pg] + s4_scr[PPAD - 4:hi - 4, 0:pg],
            s8_scr[PPAD:hi, 0:pg] + s8_scr[PPAD - 8:hi - 8, 0:pg],
        )
        head = 2 * SUBLANES
        tok0 = (i % tiles_per_seq) * PROJ_TM + r0
        pos1 = tok0 + lax.broadcasted_iota(jnp.int32, (head, POOL_GROUP), 0) + 1
        for g, w in enumerate(POOL_WINDOWS):
            c0, c1 = g * POOL_GROUP, (g + 1) * POOL_GROUP
            xg = ext_p[PPAD:hi, c0:c1]
            tot = totals[g]
            mean = jnp.concatenate(
                [tot[0:head] / jnp.minimum(pos1, w).astype(F32), tot[head:] * (1.0 / w)], axis=0)
            og = jnp.dot((mean - xg).astype(BF16), wpool_ref[g], preferred_element_type=F32)
            gp = zbuf[r0:r0 + SUB, 2 * D_RNN + D_POOL + c0:2 * D_RNN + D_POOL + c1]
            o_ref[r0:r0 + SUB, D_RNN + c0:D_RNN + c1] = (
                og * pscale_ref[:, c0:c1] * _silu(gp)).astype(BF16)
        if s == nsub - 1:
            @pl.when(seq_last)
            def _():
                newpool_ref[0] = ext_p[PPAD + SUB - POOL_HIST:PPAD + SUB, :]
        ext_p[SUBLANES:PPAD, :] = ext_p[SUB + SUBLANES:SUB + PPAD, :]

    qoff = 2 * D_RNN + 2 * D_POOL

    def attn_pre(s):
        r0 = s * SUB
        for hd in range(N_XHEADS):
            c0, c1 = hd * XHEAD_DIM, (hd + 1) * XHEAD_DIM
            q = zbuf[r0:r0 + SUB, qoff + c0:qoff + c1].astype(BF16)
            ri_scr[:, c0:c1] = lax.dot_general(q, kb_ref[:, c0:c1], (((1,), (1,)), ((), ())),
                                               preferred_element_type=F32)

    def attn_post(s):
        r0 = s * SUB
        for hd in range(N_XHEADS):
            c0, c1 = hd * XHEAD_DIM, (hd + 1) * XHEAD_DIM
            sc = ri_scr[:, c0:c1] * (XHEAD_DIM ** -0.5)
            p = jnp.exp(sc - jnp.max(sc, axis=-1, keepdims=True))
            p = p / jnp.sum(p, axis=-1, keepdims=True)
            ox = jnp.dot(p.astype(BF16), vb_ref[:, c0:c1], preferred_element_type=F32)
            gx = zbuf[r0:r0 + SUB, qoff + D_X + c0:qoff + D_X + c1]
            o_ref[r0:r0 + SUB, D_RNN + D_POOL + c0:D_RNN + D_POOL + c1] = (
                ox * _silu(gx)).astype(BF16)

    for jj in range(NZ):
        @pl.when(j == jj)
        def _(jj=jj):
            project(zbuf, jj * PROJ_TN)

    phases = [(s, ph) for s in range(nsub)
              for ph in ((rglru_pre, rglru_post), (pool_pre, pool_post), (attn_pre, attn_post))]
    assert NZ + len(phases) == D_IN // PROJ_TN
    for kk, (s, (pre, post)) in enumerate(phases):
        @pl.when(j == NZ + kk)
        def _(s=s, pre=pre, post=post):
            pre(s)
            project(gates_ref)
            post(s)


def _proj_mix(x, g, w, mem_k, mem_v, conv_w, conv_b, wax, b_a, b_x, lam, wpool, pscale, batch, seq):
    m = batch * seq
    tiles_per_seq = seq // PROJ_TM
    gw = N_BRANCH * D_MODEL
    const2 = lambda i, j: (0, 0)
    const3 = lambda i, j: (0, 0, 0)
    per_seq = lambda i, j: (i // tiles_per_seq, 0, 0)
    return pl.pallas_call(
        functools.partial(_proj_mix_kernel, tiles_per_seq=tiles_per_seq),
        grid=(m // PROJ_TM, D_IN // PROJ_TN),
        in_specs=[
            pl.BlockSpec((PROJ_TM, D_MODEL), lambda i, j: (i, 0)),
            pl.BlockSpec((1, D_MODEL), const2),
            pl.BlockSpec((D_MODEL, PROJ_TN), lambda i, j: (0, j)),
            pl.BlockSpec((1, N_MEM, D_X), per_seq),
            pl.BlockSpec((1, N_MEM, D_X), per_seq),
            pl.BlockSpec((CONV_W, D_RNN), const2),
            pl.BlockSpec((1, D_RNN), const2),
            pl.BlockSpec((N_RNN_BLOCKS, RNN_BLOCK, 2 * RNN_BLOCK), const3),
            pl.BlockSpec((1, D_RNN), const2),
            pl.BlockSpec((1, D_RNN), const2),
            pl.BlockSpec((1, D_RNN), const2),
            pl.BlockSpec((len(POOL_WINDOWS), POOL_GROUP, POOL_GROUP), const3),
            pl.BlockSpec((1, D_POOL), const2),
        ],
        out_specs=[
            pl.BlockSpec((PROJ_TM, PROJ_TN), lambda i, j: (i, jnp.maximum(j - NZ, 0))),
            pl.BlockSpec((PROJ_TM, D_MIX), lambda i, j: (i, 0)),
            pl.BlockSpec((1, 1, D_RNN), per_seq),
            pl.BlockSpec((1, CONV_W - 1, D_RNN), per_seq),
            pl.BlockSpec((1, POOL_HIST, D_POOL), per_seq),
        ],
        out_shape=[
            jax.ShapeDtypeStruct((m, gw), F32),
            jax.ShapeDtypeStruct((m, D_MIX), BF16),
            jax.ShapeDtypeStruct((batch, 1, D_RNN), F32),
            jax.ShapeDtypeStruct((batch, CONV_W - 1, D_RNN), F32),
            jax.ShapeDtypeStruct((batch, POOL_HIST, D_POOL), F32),
        ],
        scratch_shapes=[
            pltpu.VMEM((PROJ_TM, D_MODEL), BF16),
            pltpu.VMEM((PROJ_TM, 2 * D_MIX), F32),
            pltpu.VMEM((SUB + RPAD, D_RNN), F32),
            pltpu.VMEM((SUB + PPAD, D_POOL), F32),
            pltpu.VMEM((SUB + PPAD, D_POOL - POOL_GROUP), F32),
            pltpu.VMEM((SUB + PPAD, D_POOL - 2 * POOL_GROUP), F32),
            pltpu.VMEM((SUB + PPAD, D_POOL - 3 * POOL_GROUP), F32),
            pltpu.VMEM((SUBLANES, D_RNN), F32),
            pltpu.VMEM((N_MEM, D_X), BF16),
            pltpu.VMEM((N_MEM, D_X), BF16),
            pltpu.VMEM((SUB, D_RNN), F32),
            pltpu.VMEM((SUB, 2 * D_RNN), F32),
        ],
        compiler_params=pltpu.CompilerParams(
            dimension_semantics=("arbitrary", "arbitrary"),
            vmem_limit_bytes=VMEM_LIMIT),
        name="proj_mix",
    )(x, g, w, mem_k, mem_v, conv_w, conv_b, wax, b_a, b_x, lam, wpool, pscale)


def _cache_rows(c):
    nb = c.shape[0]
    c = c.reshape(nb, N_MEM, N_XHEADS, XHEAD_DIM // LANES, LANES)
    return c.transpose(0, 1, 3, 2, 4).reshape(nb, N_MEM * SUBLANES, LANES)


def _sample_attn_kernel(q_ref, k_ref, v_ref, o_ref, *, bb):
    halves = XHEAD_DIM // LANES
    assert halves * N_XHEADS == SUBLANES
    r = lax.broadcasted_iota(jnp.int32, (SUBLANES, LANES), 0)
    c = lax.broadcasted_iota(jnp.int32, (SUBLANES, LANES), 1)
    diag = (c % SUBLANES) == r
    first_half = r < N_XHEADS
    nchunk = N_MEM * SUBLANES // LANES
    for j in range(bb):
        qn = jnp.concatenate(
            [q_ref[j:j + 1, (h * halves + t) * LANES:(h * halves + t + 1) * LANES]
             for t in range(halves) for h in range(N_XHEADS)], axis=0)
        s = lax.dot_general(qn.astype(BF16), k_ref[j].astype(BF16), (((1,), (1,)), ((), ())),
                            preferred_element_type=F32) * (XHEAD_DIM ** -0.5)
        chunks = []
        for ci in range(nchunk):
            sm = jnp.where(diag, s[:, ci * LANES:(ci + 1) * LANES], 0.0)
            other = pltpu.roll(sm, N_XHEADS, 0)
            other = jnp.where(first_half, pltpu.roll(other, LANES - N_XHEADS, 1),
                              pltpu.roll(other, N_XHEADS, 1))
            chunks.append(jnp.where(diag, sm + other, -jnp.inf))
        t_full = jnp.concatenate(chunks, axis=1)
        e = jnp.exp(t_full - jnp.max(t_full, axis=1, keepdims=True))
        p = e / jnp.sum(e, axis=1, keepdims=True)
        o = jnp.dot(p.astype(BF16), v_ref[j].astype(BF16), preferred_element_type=F32)
        for t in range(halves):
            for h in range(N_XHEADS):
                col = (h * halves + t) * LANES
                o_ref[j:j + 1, col:col + LANES] = o[t * N_XHEADS + h:t * N_XHEADS + h + 1, :]


def _sample_attn(z, cache_k, cache_v, bb):
    nb = z.shape[0]
    qblk = (2 * D_RNN + 2 * D_POOL) // D_X
    return pl.pallas_call(
        functools.partial(_sample_attn_kernel, bb=bb),
        grid=(nb // bb,),
        in_specs=[
            pl.BlockSpec((bb, D_X), lambda i: (i, qblk)),
            pl.BlockSpec((bb, N_MEM * SUBLANES, LANES), lambda i: (i, 0, 0)),
            pl.BlockSpec((bb, N_MEM * SUBLANES, LANES), lambda i: (i, 0, 0)),
        ],
        out_specs=pl.BlockSpec((bb, D_X), lambda i: (i, 0)),
        out_shape=jax.ShapeDtypeStruct((nb, D_X), F32),
        compiler_params=pltpu.CompilerParams(
            dimension_semantics=("arbitrary",),
            vmem_limit_bytes=VMEM_LIMIT),
        name="sample_attn",
    )(z, cache_k, cache_v)


def _sample_mix_kernel(z_ref, attn_ref, conv_ref, h_ref, pool_ref,
                       convw_ref, convb_ref, wax_ref, ba_ref, bx_ref, lam_ref, wpool_ref,
                       pscale_ref, o_ref, newh_ref, newconv_ref, newpool_ref):
    xr = z_ref[:, 0:D_RNN]
    xc = convb_ref[...] + convw_ref[CONV_W - 1:CONV_W, :] * xr
    for k in range(CONV_W - 1):
        xc = xc + convw_ref[k:k + 1, :] * conv_ref[k]
    for k in range(CONV_W - 2):
        newconv_ref[k] = conv_ref[k + 1]
    newconv_ref[CONV_W - 2] = xr

    sp = _softplus(-lam_ref[...])
    for n in range(N_RNN_BLOCKS):
        c0, c1 = n * RNN_BLOCK, (n + 1) * RNN_BLOCK
        a, b = _rglru_block(xc[:, c0:c1], wax_ref[n], ba_ref[:, c0:c1], bx_ref[:, c0:c1],
                            sp[:, c0:c1])
        h = a * h_ref[:, c0:c1] + b
        newh_ref[:, c0:c1] = h
        o_ref[:, c0:c1] = (h * _silu(z_ref[:, D_RNN + c0:D_RNN + c1])).astype(BF16)

    xp = z_ref[:, 2 * D_RNN:2 * D_RNN + D_POOL]
    for k in range(POOL_HIST - 1):
        newpool_ref[k] = pool_ref[k + 1]
    newpool_ref[POOL_HIST - 1] = xp
    for g, w in enumerate(POOL_WINDOWS):
        c0, c1 = g * POOL_GROUP, (g + 1) * POOL_GROUP
        xg = xp[:, c0:c1]
        tot = xg
        for j in range(1, w):
            tot = tot + pool_ref[POOL_HIST - j, :, c0:c1]
        cnt = float(min(PAST_LEN + 1, w))
        d = tot / cnt - xg
        og = jnp.dot(d.astype(BF16), wpool_ref[g], preferred_element_type=F32)
        gp = z_ref[:, 2 * D_RNN + D_POOL + c0:2 * D_RNN + D_POOL + c1]
        o_ref[:, D_RNN + c0:D_RNN + c1] = (og * pscale_ref[:, c0:c1] * _silu(gp)).astype(BF16)

    gx = z_ref[:, 2 * D_RNN + 2 * D_POOL + D_X:2 * D_MIX]
    o_ref[:, D_RNN + D_POOL:] = (attn_ref[...] * _silu(gx)).astype(BF16)


def _sample_mix(z, attn, conv, h, pool, conv_w, conv_b, wax, b_a, b_x, lam, wpool, pscale, tb):
    nb = z.shape[0]
    zw = 2 * D_MIX
    rows = lambda i: (i, 0)
    const2 = lambda i: (0, 0)
    const3 = lambda i: (0, 0, 0)
    hist = lambda i: (0, i, 0)
    return pl.pallas_call(
        _sample_mix_kernel,
        grid=(nb // tb,),
        in_specs=[
            pl.BlockSpec((tb, zw), rows),
            pl.BlockSpec((tb, D_X), rows),
            pl.BlockSpec((CONV_W - 1, tb, D_RNN), hist),
            pl.BlockSpec((tb, D_RNN), rows),
            pl.BlockSpec((POOL_HIST, tb, D_POOL), hist),
            pl.BlockSpec((CONV_W, D_RNN), const2),
            pl.BlockSpec((1, D_RNN), const2),
            pl.BlockSpec((N_RNN_BLOCKS, RNN_BLOCK, 2 * RNN_BLOCK), const3),
            pl.BlockSpec((1, D_RNN), const2),
            pl.BlockSpec((1, D_RNN), const2),
            pl.BlockSpec((1, D_RNN), const2),
            pl.BlockSpec((len(POOL_WINDOWS), POOL_GROUP, POOL_GROUP), const3),
            pl.BlockSpec((1, D_POOL), const2),
        ],
        out_specs=[
            pl.BlockSpec((tb, D_MIX), rows),
            pl.BlockSpec((tb, D_RNN), rows),
            pl.BlockSpec((CONV_W - 1, tb, D_RNN), hist),
            pl.BlockSpec((POOL_HIST, tb, D_POOL), hist),
        ],
        out_shape=[
            jax.ShapeDtypeStruct((nb, D_MIX), BF16),
            jax.ShapeDtypeStruct((nb, D_RNN), F32),
            jax.ShapeDtypeStruct((CONV_W - 1, nb, D_RNN), F32),
            jax.ShapeDtypeStruct((POOL_HIST, nb, D_POOL), F32),
        ],
        compiler_params=pltpu.CompilerParams(
            dimension_semantics=("arbitrary",),
            vmem_limit_bytes=VMEM_LIMIT),
        name="sample_mix",
    )(z, attn, conv, h, pool, conv_w, conv_b, wax, b_a, b_x, lam, wpool, pscale)


def _branch_out_kernel(o_ref, gates_ref, x_ref, wb_ref, wo_ref, gpost_ref, y_ref):
    merged = None
    for j, (r0, r1) in enumerate(((0, D_RNN), (D_RNN, D_RNN + D_POOL), (D_RNN + D_POOL, D_MIX))):
        yj = jnp.dot(o_ref[:, r0:r1], wb_ref[r0:r1, :], preferred_element_type=F32)
        term = _sigmoid(gates_ref[:, j * D_MODEL:(j + 1) * D_MODEL]) * yj
        merged = term if merged is None else merged + term
    out = jnp.dot(merged.astype(BF16), wo_ref[...], preferred_element_type=F32)
    y_ref[...] = x_ref[...] + out * _rms_scale(out) * gpost_ref[...]


def _branch_out(o, gates, gblk, x, wb, wo, g_post, tm):
    m = x.shape[0]
    gw = N_BRANCH * D_MODEL
    resident = pl.Buffered(1)
    return pl.pallas_call(
        _branch_out_kernel,
        grid=(m // tm,),
        in_specs=[
            pl.BlockSpec((tm, D_MIX), lambda i: (i, 0)),
            pl.BlockSpec((tm, gw), lambda i: (i, gblk)),
            pl.BlockSpec((tm, D_MODEL), lambda i: (i, 0)),
            pl.BlockSpec((D_MIX, D_MODEL), lambda i: (0, 0), pipeline_mode=resident),
            pl.BlockSpec((D_MODEL, D_MODEL), lambda i: (0, 0), pipeline_mode=resident),
            pl.BlockSpec((1, D_MODEL), lambda i: (0, 0)),
        ],
        out_specs=pl.BlockSpec((tm, D_MODEL), lambda i: (i, 0)),
        out_shape=jax.ShapeDtypeStruct((m, D_MODEL), F32),
        compiler_params=pltpu.CompilerParams(
            dimension_semantics=("arbitrary",),
            vmem_limit_bytes=VMEM_LIMIT),
        name="branch_out",
    )(o, gates, x, wb, wo, g_post)


def kernel(x_prompt, x_sample, mem_prompt, state_rglru_h, state_conv, state_pool, cache_mem_k, cache_mem_v, g_pre, w_in, conv_w, conv_b, w_rg_a, b_rg_a, w_rg_x, b_rg_x, lru_lambda, w_pool, pool_scale, g_mem, w_kv, w_branch, w_out, g_post):
    batch, seq, _ = x_prompt.shape
    nb = x_sample.shape[0]
    depth = g_pre.shape[0]
    assert depth == 1 and x_sample.shape[1] == 1

    l = 0
    row = lambda v: v.reshape(1, -1)
    w_br_b = w_branch[l].astype(BF16)
    w_out_b = w_out[l].astype(BF16)
    wax = jnp.concatenate([w_rg_a[l], w_rg_x[l]], axis=-1).astype(BF16)
    wpool = w_pool[l].astype(BF16)
    mix_params = (conv_w[l], row(conv_b[l]), wax, row(b_rg_a[l]), row(b_rg_x[l]),
                  row(lru_lambda[l]), wpool, row(pool_scale[l]))

    xp2 = x_prompt.reshape(batch * seq, D_MODEL)
    xs2 = x_sample.reshape(nb, D_MODEL)
    mem2 = mem_prompt.reshape(batch * N_MEM, D_MODEL)

    z_s, w_in_b = _norm_matmul(xs2, row(g_pre[l]), w_in[l], tm=nb, tn=1024, emit_w=True)
    attn_s = _sample_attn(z_s, _cache_rows(cache_mem_k[l]), _cache_rows(cache_mem_v[l]), bb=8)
    o_s, h_s, c_s, p_s = _sample_mix(
        z_s, attn_s, state_conv[l].transpose(1, 0, 2), state_rglru_h[l],
        state_pool[l].transpose(1, 0, 2), *mix_params, tb=32)
    gate_blk = 2 * D_MIX // (N_BRANCH * D_MODEL)
    y_s = _branch_out(o_s, z_s, gate_blk, xs2, w_br_b, w_out_b, row(g_post[l]), tm=nb)

    kv = _norm_matmul(mem2, row(g_mem[l]), w_kv[l], tm=512, tn=1024)
    mem_k = kv[:, :D_X].reshape(batch, N_MEM, D_X)
    mem_v = kv[:, D_X:].reshape(batch, N_MEM, D_X)

    gates_p, o_p, h_p, c_p, p_p = _proj_mix(xp2, row(g_pre[l]), w_in_b, mem_k, mem_v, *mix_params,
                                            batch=batch, seq=seq)
    y_p = _branch_out(o_p, gates_p, 0, xp2, w_br_b, w_out_b, row(g_post[l]), tm=256)

    return (
        y_p.reshape(batch, seq, D_MODEL),
        y_s.reshape(nb, 1, D_MODEL),
        h_p.reshape(1, batch, D_RNN),
        c_p.reshape(1, batch, CONV_W - 1, D_RNN),
        p_p.reshape(1, batch, POOL_HIST, D_POOL),
        mem_k.reshape(1, batch, N_MEM, N_XHEADS, XHEAD_DIM),
        mem_v.reshape(1, batch, N_MEM, N_XHEADS, XHEAD_DIM),
        h_s.reshape(1, nb, D_RNN),
        c_s.transpose(1, 0, 2)[None],
        p_s.transpose(1, 0, 2)[None],
    )
```

```python
import functools

import jax
import jax.numpy as jnp
from jax import lax
from jax.experimental import pallas as pl
from jax.experimental.pallas import tpu as pltpu

D_MODEL = 2048
PAST_LEN = 16384
D_RNN = 1024
N_RNN_BLOCKS = 8
RNN_BLOCK = D_RNN // N_RNN_BLOCKS
CONV_W = 4
LRU_C = 8.0
D_POOL = 1024
POOL_WINDOWS = (2, 4, 8, 16)
POOL_GROUP = D_POOL // len(POOL_WINDOWS)
POOL_HIST = max(POOL_WINDOWS) - 1
N_MEM = 256
N_XHEADS = 4
XHEAD_DIM = 256
D_X = N_XHEADS * XHEAD_DIM
N_BRANCH = 3
D_MIX = D_RNN + D_POOL + D_X
D_IN = 2 * D_MIX + N_BRANCH * D_MODEL
EPS = 1e-6

SUBLANES = 8
LANES = 128
VMEM_LIMIT = 56 * 1024 * 1024
MIX_TM = 256

BF16 = jnp.bfloat16
F32 = jnp.float32

NEG_LOG2_E = -1.4426950408889634


def _sigmoid(x):
    return 1.0 / (1.0 + jnp.exp2(x * NEG_LOG2_E))


def _silu(x):
    return x * _sigmoid(x)


def _softplus(x):
    return jnp.maximum(x, 0.0) + jnp.log1p(jnp.exp(-jnp.abs(x)))


def _rms_scale(x):
    return lax.rsqrt(jnp.mean(x * x, axis=-1, keepdims=True) + EPS)


def _chunk_interleave():
    nrow = MIX_TM // SUBLANES
    p = jnp.arange(MIX_TM)
    token = (p % SUBLANES) * nrow + p // SUBLANES
    return (token[:, None] == jnp.arange(MIX_TM)[None, :]).astype(BF16)


def _norm_matmul_kernel(x_ref, g_ref, w_ref, *rest, emit_w, interleave):
    rest = list(rest)
    perm_ref = rest.pop(0) if interleave else None
    o_ref = rest.pop(0)
    u_ref = rest[-1]

    @pl.when(pl.program_id(1) == 0)
    def _():
        x = x_ref[...]
        u = (x * _rms_scale(x) * g_ref[...]).astype(BF16)
        if interleave:
            for r0 in range(0, u.shape[0], MIX_TM):
                u_ref[r0:r0 + MIX_TM, :] = jnp.dot(
                    perm_ref[...], u[r0:r0 + MIX_TM], preferred_element_type=F32).astype(BF16)
        else:
            u_ref[...] = u

    w = w_ref[...].astype(BF16)
    if emit_w:
        rest[0][...] = w
    o_ref[...] = jnp.dot(u_ref[...], w, preferred_element_type=F32)


def _norm_matmul(x, g, w, tm, tn, emit_w=False, perm=None):
    m, k = x.shape
    n = w.shape[1]
    in_specs = [
        pl.BlockSpec((tm, k), lambda i, j: (i, 0)),
        pl.BlockSpec((1, k), lambda i, j: (0, 0)),
        pl.BlockSpec((k, tn), lambda i, j: (0, j)),
    ]
    args = [x, g, w]
    if perm is not None:
        in_specs.append(pl.BlockSpec(perm.shape, lambda i, j: (0, 0)))
        args.append(perm)
    out_specs = [pl.BlockSpec((tm, tn), lambda i, j: (i, j))]
    out_shape = [jax.ShapeDtypeStruct((m, n), F32)]
    if emit_w:
        assert m == tm, "the bf16 weight copy is written once per column block"
        out_specs.append(pl.BlockSpec((k, tn), lambda i, j: (0, j)))
        out_shape.append(jax.ShapeDtypeStruct((k, n), BF16))
    res = pl.pallas_call(
        functools.partial(_norm_matmul_kernel, emit_w=emit_w, interleave=perm is not None),
        grid=(m // tm, n // tn),
        in_specs=in_specs,
        out_specs=out_specs,
        out_shape=out_shape,
        scratch_shapes=[pltpu.VMEM((tm, k), BF16)],
        compiler_params=pltpu.CompilerParams(
            dimension_semantics=("arbitrary", "arbitrary"),
            vmem_limit_bytes=VMEM_LIMIT),
        name="norm_matmul",
    )(*args)
    return res if emit_w else res[0]


def _rglru_block(xc, wax, ba, bx, sp):
    ri = jnp.dot(xc.astype(BF16), wax, preferred_element_type=F32)
    r = _sigmoid(ri[:, :RNN_BLOCK] + ba)
    i = _sigmoid(ri[:, RNN_BLOCK:] + bx)
    a = jnp.exp((-LRU_C) * r * sp)
    one_m = 1.0 - a * a
    mult = jnp.where(one_m > 0.0, one_m * lax.rsqrt(one_m), 0.0)
    return a, mult * i * xc


def _prompt_mix_kernel(z_ref, k_ref, v_ref, convw_ref, convb_ref, wax_ref, ba_ref, bx_ref,
                       lam_ref, wpool_ref, pscale_ref,
                       o_ref, newh_ref, newconv_ref, newpool_ref,
                       conv_carry, pool_carry, h_carry, kb_ref, vb_ref, ac_scr, hl_scr, *, tm):
    l = pl.program_id(1)
    last = pl.num_programs(1) - 1
    nrow = tm // SUBLANES

    @pl.when(l == 0)
    def _():
        conv_carry[...] = jnp.zeros(conv_carry.shape, F32)
        pool_carry[...] = jnp.zeros(pool_carry.shape, F32)
        h_carry[...] = jnp.zeros(h_carry.shape, F32)
        kb_ref[...] = k_ref[0].astype(BF16)
        vb_ref[...] = v_ref[0].astype(BF16)

    chunk_id = lax.broadcasted_iota(jnp.int32, (SUBLANES, LANES), 0)
    first_chunk = chunk_id == 0

    def load_groups(col, width=LANES):
        return [z_ref[r * SUBLANES:(r + 1) * SUBLANES, col:col + width] for r in range(nrow)]

    def store_groups(col, rows, width=LANES):
        o_ref[:, col:col + width] = jnp.concatenate(rows, axis=0).astype(BF16)

    def history(tail_group, carry_ref, j, c0):
        tail = pltpu.roll(tail_group, 1, 0)
        prev = jnp.where(first_chunk, carry_ref[j - 1, :, c0:c0 + LANES], tail)
        carry_ref[j - 1, :, c0:c0 + LANES] = tail
        return prev

    sp = _softplus(-lam_ref[...])
    for n in range(N_RNN_BLOCKS):
        c0, c1 = n * RNN_BLOCK, (n + 1) * RNN_BLOCK
        xs = load_groups(c0)
        ext = [history(xs[nrow - j], conv_carry, j, c0) for j in range(CONV_W - 1, 0, -1)] + xs
        cw = [jnp.broadcast_to(convw_ref[k:k + 1, c0:c1], (SUBLANES, LANES)) for k in range(CONV_W)]
        cb = jnp.broadcast_to(convb_ref[:, c0:c1], (SUBLANES, LANES))
        xc = []
        for r in range(nrow):
            acc = cb + cw[0] * ext[r]
            for k in range(1, CONV_W):
                acc = acc + cw[k] * ext[r + k]
            xc.append(acc)
        a, b = _rglru_block(jnp.concatenate(xc, axis=0), wax_ref[n], ba_ref[:, c0:c1],
                            bx_ref[:, c0:c1], sp[:, c0:c1])
        ac_scr[:, c0:c1] = a
        hl_scr[:, c0:c1] = b

    acc_a = ac_scr[0:SUBLANES, :]
    acc_h = hl_scr[0:SUBLANES, :]
    for r in range(1, nrow):
        rows = slice(r * SUBLANES, (r + 1) * SUBLANES)
        ar = ac_scr[rows, :]
        acc_h = ar * acc_h + hl_scr[rows, :]
        acc_a = ar * acc_a
        ac_scr[rows, :] = acc_a
        hl_scr[rows, :] = acc_h
    h_in = h_carry[...]
    entering = []
    for c in range(SUBLANES):
        entering.append(h_in)
        h_in = acc_a[c:c + 1] * h_in + acc_h[c:c + 1]
    h_carry[...] = h_in
    h_enter = jnp.concatenate(entering, axis=0)
    for n in range(N_RNN_BLOCKS):
        c0, c1 = n * RNN_BLOCK, (n + 1) * RNN_BLOCK
        gr = load_groups(D_RNN + c0)
        store_groups(c0, [(hl_scr[r * SUBLANES:(r + 1) * SUBLANES, c0:c1]
                           + ac_scr[r * SUBLANES:(r + 1) * SUBLANES, c0:c1] * h_enter[:, c0:c1])
                          * _silu(gr[r]) for r in range(nrow)])

    pcol = 2 * D_RNN
    blocks = [(w, c0) for g, w in enumerate(POOL_WINDOWS)
              for c0 in range(g * POOL_GROUP, (g + 1) * POOL_GROUP, LANES)]

    def group(c0, r):
        return z_ref[r * SUBLANES:(r + 1) * SUBLANES, pcol + c0:pcol + c0 + LANES]

    def mean_minus_token(tot, w, c0, r):
        if r < w - 1:
            pos1 = l * tm + chunk_id * nrow + (r + 1)
            mean = tot / jnp.minimum(pos1, w).astype(F32)
        else:
            mean = tot * (1.0 / w)
        return mean - group(c0, r)

    hist, tot = {}, {}
    for w, c0 in blocks:
        hist[c0] = [history(group(c0, nrow - j), pool_carry, j, c0) for j in range(1, w)]
        t = group(c0, 0)
        for h in hist[c0]:
            t = t + h
        tot[c0] = t
        hl_scr[0:SUBLANES, c0:c0 + LANES] = mean_minus_token(t, w, c0, 0)
    for r in range(1, nrow):
        for w, c0 in blocks:
            leaving = group(c0, r - w) if r >= w else hist[c0][w - r - 1]
            tot[c0] = tot[c0] + (group(c0, r) - leaving)
            hl_scr[r * SUBLANES:(r + 1) * SUBLANES, c0:c0 + LANES] = mean_minus_token(
                tot[c0], w, c0, r)
    for g, w in enumerate(POOL_WINDOWS):
        c0, c1 = g * POOL_GROUP, (g + 1) * POOL_GROUP
        og = jnp.dot(hl_scr[:, c0:c1].astype(BF16), wpool_ref[g], preferred_element_type=F32)
        gp = z_ref[:, pcol + D_POOL + c0:pcol + D_POOL + c1]
        o_ref[:, D_RNN + c0:D_RNN + c1] = (og * pscale_ref[:, c0:c1] * _silu(gp)).astype(BF16)

    qoff = 2 * D_RNN + 2 * D_POOL
    for hd in range(N_XHEADS):
        c0, c1 = hd * XHEAD_DIM, (hd + 1) * XHEAD_DIM
        q = z_ref[:, qoff + c0:qoff + c1].astype(BF16)
        s = lax.dot_general(q, kb_ref[:, c0:c1], (((1,), (1,)), ((), ())),
                            preferred_element_type=F32) * (XHEAD_DIM ** -0.5)
        p = jnp.exp(s - jnp.max(s, axis=-1, keepdims=True))
        p = p / jnp.sum(p, axis=-1, keepdims=True)
        ox = jnp.dot(p.astype(BF16), vb_ref[:, c0:c1], preferred_element_type=F32)
        gx = z_ref[:, qoff + D_X + c0:qoff + D_X + c1]
        o_ref[:, D_RNN + D_POOL + c0:D_RNN + D_POOL + c1] = (ox * _silu(gx)).astype(BF16)

    @pl.when(l == last)
    def _():
        newh_ref[0] = h_carry[...]
        tail_row = lambda j: (nrow - j) * SUBLANES + SUBLANES - 1
        for j in range(1, CONV_W):
            newconv_ref[0, CONV_W - 1 - j:CONV_W - j, :] = z_ref[tail_row(j):tail_row(j) + 1, 0:D_RNN]
        for j in range(1, POOL_HIST + 1):
            newpool_ref[0, POOL_HIST - j:POOL_HIST - j + 1, :] = (
                z_ref[tail_row(j):tail_row(j) + 1, pcol:pcol + D_POOL])


def _prompt_mix(z, mem_k, mem_v, conv_w, conv_b, wax, b_a, b_x, lam, wpool, pscale,
                batch, seq, tm):
    nl = seq // tm
    zw = 2 * D_MIX
    const2 = lambda b, l: (0, 0)
    const3 = lambda b, l: (0, 0, 0)
    kern = functools.partial(_prompt_mix_kernel, tm=tm)
    return pl.pallas_call(
        kern,
        grid=(batch, nl),
        in_specs=[
            pl.BlockSpec((tm, zw), lambda b, l: (b * nl + l, 0)),
            pl.BlockSpec((1, N_MEM, D_X), lambda b, l: (b, 0, 0)),
            pl.BlockSpec((1, N_MEM, D_X), lambda b, l: (b, 0, 0)),
            pl.BlockSpec((CONV_W, D_RNN), const2),
            pl.BlockSpec((1, D_RNN), const2),
            pl.BlockSpec((N_RNN_BLOCKS, RNN_BLOCK, 2 * RNN_BLOCK), const3),
            pl.BlockSpec((1, D_RNN), const2),
            pl.BlockSpec((1, D_RNN), const2),
            pl.BlockSpec((1, D_RNN), const2),
            pl.BlockSpec((len(POOL_WINDOWS), POOL_GROUP, POOL_GROUP), const3),
            pl.BlockSpec((1, D_POOL), const2),
        ],
        out_specs=[
            pl.BlockSpec((tm, D_MIX), lambda b, l: (b * nl + l, 0)),
            pl.BlockSpec((1, 1, D_RNN), lambda b, l: (b, 0, 0)),
            pl.BlockSpec((1, CONV_W - 1, D_RNN), lambda b, l: (b, 0, 0)),
            pl.BlockSpec((1, POOL_HIST, D_POOL), lambda b, l: (b, 0, 0)),
        ],
        out_shape=[
            jax.ShapeDtypeStruct((batch * seq, D_MIX), BF16),
            jax.ShapeDtypeStruct((batch, 1, D_RNN), F32),
            jax.ShapeDtypeStruct((batch, CONV_W - 1, D_RNN), F32),
            jax.ShapeDtypeStruct((batch, POOL_HIST, D_POOL), F32),
        ],
        scratch_shapes=[
            pltpu.VMEM((CONV_W - 1, SUBLANES, D_RNN), F32),
            pltpu.VMEM((POOL_HIST, SUBLANES, D_POOL), F32),
            pltpu.VMEM((1, D_RNN), F32),
            pltpu.VMEM((N_MEM, D_X), BF16),
            pltpu.VMEM((N_MEM, D_X), BF16),
            pltpu.VMEM((tm, D_RNN), F32),
            pltpu.VMEM((tm, D_RNN), F32),
        ],
        compiler_params=pltpu.CompilerParams(
            dimension_semantics=("arbitrary", "arbitrary"),
            vmem_limit_bytes=VMEM_LIMIT),
        name="prompt_mix",
    )(z, mem_k, mem_v, conv_w, conv_b, wax, b_a, b_x, lam, wpool, pscale)


def _cache_rows(c):
    nb = c.shape[0]
    c = c.reshape(nb, N_MEM, N_XHEADS, XHEAD_DIM // LANES, LANES)
    return c.transpose(0, 1, 3, 2, 4).reshape(nb, N_MEM * SUBLANES, LANES)


def _sample_attn_kernel(q_ref, k_ref, v_ref, o_ref, *, bb):
    halves = XHEAD_DIM // LANES
    assert halves * N_XHEADS == SUBLANES
    r = lax.broadcasted_iota(jnp.int32, (SUBLANES, LANES), 0)
    c = lax.broadcasted_iota(jnp.int32, (SUBLANES, LANES), 1)
    diag = (c % SUBLANES) == r
    first_half = r < N_XHEADS
    nchunk = N_MEM * SUBLANES // LANES
    for j in range(bb):
        qn = jnp.concatenate(
            [q_ref[j:j + 1, (h * halves + t) * LANES:(h * halves + t + 1) * LANES]
             for t in range(halves) for h in range(N_XHEADS)], axis=0)
        s = lax.dot_general(qn.astype(BF16), k_ref[j].astype(BF16), (((1,), (1,)), ((), ())),
                            preferred_element_type=F32) * (XHEAD_DIM ** -0.5)
        chunks = []
        for ci in range(nchunk):
            sm = jnp.where(diag, s[:, ci * LANES:(ci + 1) * LANES], 0.0)
            other = pltpu.roll(sm, N_XHEADS, 0)
            other = jnp.where(first_half, pltpu.roll(other, LANES - N_XHEADS, 1),
                              pltpu.roll(other, N_XHEADS, 1))
            chunks.append(jnp.where(diag, sm + other, -jnp.inf))
        t_full = jnp.concatenate(chunks, axis=1)
        e = jnp.exp(t_full - jnp.max(t_full, axis=1, keepdims=True))
        p = e / jnp.sum(e, axis=1, keepdims=True)
        o = jnp.dot(p.astype(BF16), v_ref[j].astype(BF16), preferred_element_type=F32)
        for t in range(halves):
            for h in range(N_XHEADS):
                col = (h * halves + t) * LANES
                o_ref[j:j + 1, col:col + LANES] = o[t * N_XHEADS + h:t * N_XHEADS + h + 1, :]


def _sample_attn(z, cache_k, cache_v, bb):
    nb = z.shape[0]
    qblk = (2 * D_RNN + 2 * D_POOL) // D_X
    return pl.pallas_call(
        functools.partial(_sample_attn_kernel, bb=bb),
        grid=(nb // bb,),
        in_specs=[
            pl.BlockSpec((bb, D_X), lambda i: (i, qblk)),
            pl.BlockSpec((bb, N_MEM * SUBLANES, LANES), lambda i: (i, 0, 0)),
            pl.BlockSpec((bb, N_MEM * SUBLANES, LANES), lambda i: (i, 0, 0)),
        ],
        out_specs=pl.BlockSpec((bb, D_X), lambda i: (i, 0)),
        out_shape=jax.ShapeDtypeStruct((nb, D_X), F32),
        compiler_params=pltpu.CompilerParams(
            dimension_semantics=("arbitrary",),
            vmem_limit_bytes=VMEM_LIMIT),
        name="sample_attn",
    )(z, cache_k, cache_v)


def _sample_mix_kernel(z_ref, attn_ref, conv_ref, h_ref, pool_ref,
                       convw_ref, convb_ref, wax_ref, ba_ref, bx_ref, lam_ref, wpool_ref,
                       pscale_ref, o_ref, newh_ref, newconv_ref, newpool_ref):
    xr = z_ref[:, 0:D_RNN]
    xc = convb_ref[...] + convw_ref[CONV_W - 1:CONV_W, :] * xr
    for k in range(CONV_W - 1):
        xc = xc + convw_ref[k:k + 1, :] * conv_ref[k]
    for k in range(CONV_W - 2):
        newconv_ref[k] = conv_ref[k + 1]
    newconv_ref[CONV_W - 2] = xr

    sp = _softplus(-lam_ref[...])
    for n in range(N_RNN_BLOCKS):
        c0, c1 = n * RNN_BLOCK, (n + 1) * RNN_BLOCK
        a, b = _rglru_block(xc[:, c0:c1], wax_ref[n], ba_ref[:, c0:c1], bx_ref[:, c0:c1],
                            sp[:, c0:c1])
        h = a * h_ref[:, c0:c1] + b
        newh_ref[:, c0:c1] = h
        o_ref[:, c0:c1] = (h * _silu(z_ref[:, D_RNN + c0:D_RNN + c1])).astype(BF16)

    xp = z_ref[:, 2 * D_RNN:2 * D_RNN + D_POOL]
    for k in range(POOL_HIST - 1):
        newpool_ref[k] = pool_ref[k + 1]
    newpool_ref[POOL_HIST - 1] = xp
    for g, w in enumerate(POOL_WINDOWS):
        c0, c1 = g * POOL_GROUP, (g + 1) * POOL_GROUP
        xg = xp[:, c0:c1]
        tot = xg
        for j in range(1, w):
            tot = tot + pool_ref[POOL_HIST - j, :, c0:c1]
        cnt = float(min(PAST_LEN + 1, w))
        d = tot / cnt - xg
        og = jnp.dot(d.astype(BF16), wpool_ref[g], preferred_element_type=F32)
        gp = z_ref[:, 2 * D_RNN + D_POOL + c0:2 * D_RNN + D_POOL + c1]
        o_ref[:, D_RNN + c0:D_RNN + c1] = (og * pscale_ref[:, c0:c1] * _silu(gp)).astype(BF16)

    gx = z_ref[:, 2 * D_RNN + 2 * D_POOL + D_X:2 * D_MIX]
    o_ref[:, D_RNN + D_POOL:] = (attn_ref[...] * _silu(gx)).astype(BF16)


def _sample_mix(z, attn, conv, h, pool, conv_w, conv_b, wax, b_a, b_x, lam, wpool, pscale, tb):
    nb = z.shape[0]
    zw = 2 * D_MIX
    rows = lambda i: (i, 0)
    const2 = lambda i: (0, 0)
    const3 = lambda i: (0, 0, 0)
    hist = lambda i: (0, i, 0)
    return pl.pallas_call(
        _sample_mix_kernel,
        grid=(nb // tb,),
        in_specs=[
            pl.BlockSpec((tb, zw), rows),
            pl.BlockSpec((tb, D_X), rows),
            pl.BlockSpec((CONV_W - 1, tb, D_RNN), hist),
            pl.BlockSpec((tb, D_RNN), rows),
            pl.BlockSpec((POOL_HIST, tb, D_POOL), hist),
            pl.BlockSpec((CONV_W, D_RNN), const2),
            pl.BlockSpec((1, D_RNN), const2),
            pl.BlockSpec((N_RNN_BLOCKS, RNN_BLOCK, 2 * RNN_BLOCK), const3),
            pl.BlockSpec((1, D_RNN), const2),
            pl.BlockSpec((1, D_RNN), const2),
            pl.BlockSpec((1, D_RNN), const2),
            pl.BlockSpec((len(POOL_WINDOWS), POOL_GROUP, POOL_GROUP), const3),
            pl.BlockSpec((1, D_POOL), const2),
        ],
        out_specs=[
            pl.BlockSpec((tb, D_MIX), rows),
            pl.BlockSpec((tb, D_RNN), rows),
            pl.BlockSpec((CONV_W - 1, tb, D_RNN), hist),
            pl.BlockSpec((POOL_HIST, tb, D_POOL), hist),
        ],
        out_shape=[
            jax.ShapeDtypeStruct((nb, D_MIX), BF16),
            jax.ShapeDtypeStruct((nb, D_RNN), F32),
            jax.ShapeDtypeStruct((CONV_W - 1, nb, D_RNN), F32),
            jax.ShapeDtypeStruct((POOL_HIST, nb, D_POOL), F32),
        ],
        compiler_params=pltpu.CompilerParams(
            dimension_semantics=("arbitrary",),
            vmem_limit_bytes=VMEM_LIMIT),
        name="sample_mix",
    )(z, attn, conv, h, pool, conv_w, conv_b, wax, b_a, b_x, lam, wpool, pscale)


def _branch_out_kernel(o_ref, gates_ref, x_ref, wb_ref, wo_ref, gpost_ref, *rest, interleaved):
    y_ref = rest[-1]
    merged = None
    for j, (r0, r1) in enumerate(((0, D_RNN), (D_RNN, D_RNN + D_POOL), (D_RNN + D_POOL, D_MIX))):
        yj = jnp.dot(o_ref[:, r0:r1], wb_ref[r0:r1, :], preferred_element_type=F32)
        term = _sigmoid(gates_ref[:, j * D_MODEL:(j + 1) * D_MODEL]) * yj
        merged = term if merged is None else merged + term
    merged = merged.astype(BF16)
    if interleaved:
        merged = jnp.dot(rest[0][...], merged, preferred_element_type=F32).astype(BF16)
    out = jnp.dot(merged, wo_ref[...], preferred_element_type=F32)
    y_ref[...] = x_ref[...] + out * _rms_scale(out) * gpost_ref[...]


def _branch_out(o, z, x, wb, wo, g_post, tm, unperm=None):
    m = x.shape[0]
    gw = N_BRANCH * D_MODEL
    gblk = (2 * D_MIX) // gw
    resident = pl.Buffered(1)
    in_specs = [
        pl.BlockSpec((tm, D_MIX), lambda i: (i, 0)),
        pl.BlockSpec((tm, gw), lambda i: (i, gblk)),
        pl.BlockSpec((tm, D_MODEL), lambda i: (i, 0)),
        pl.BlockSpec((D_MIX, D_MODEL), lambda i: (0, 0), pipeline_mode=resident),
        pl.BlockSpec((D_MODEL, D_MODEL), lambda i: (0, 0), pipeline_mode=resident),
        pl.BlockSpec((1, D_MODEL), lambda i: (0, 0)),
    ]
    args = [o, z, x, wb, wo, g_post]
    if unperm is not None:
        assert unperm.shape == (tm, tm)
        in_specs.append(pl.BlockSpec(unperm.shape, lambda i: (0, 0)))
        args.append(unperm)
    return pl.pallas_call(
        functools.partial(_branch_out_kernel, interleaved=unperm is not None),
        grid=(m // tm,),
        in_specs=in_specs,
        out_specs=pl.BlockSpec((tm, D_MODEL), lambda i: (i, 0)),
        out_shape=jax.ShapeDtypeStruct((m, D_MODEL), F32),
        compiler_params=pltpu.CompilerParams(
            dimension_semantics=("arbitrary",),
            vmem_limit_bytes=VMEM_LIMIT),
        name="branch_out",
    )(*args)


def kernel(x_prompt, x_sample, mem_prompt, state_rglru_h, state_conv, state_pool, cache_mem_k, cache_mem_v, g_pre, w_in, conv_w, conv_b, w_rg_a, b_rg_a, w_rg_x, b_rg_x, lru_lambda, w_pool, pool_scale, g_mem, w_kv, w_branch, w_out, g_post):
    batch, seq, _ = x_prompt.shape
    nb = x_sample.shape[0]
    depth = g_pre.shape[0]
    assert depth == 1 and x_sample.shape[1] == 1

    l = 0
    row = lambda v: v.reshape(1, -1)
    w_br_b = w_branch[l].astype(BF16)
    w_out_b = w_out[l].astype(BF16)
    wax = jnp.concatenate([w_rg_a[l], w_rg_x[l]], axis=-1).astype(BF16)
    wpool = w_pool[l].astype(BF16)
    mix_params = (conv_w[l], row(conv_b[l]), wax, row(b_rg_a[l]), row(b_rg_x[l]),
                  row(lru_lambda[l]), wpool, row(pool_scale[l]))

    xp2 = x_prompt.reshape(batch * seq, D_MODEL)
    xs2 = x_sample.reshape(nb, D_MODEL)
    mem2 = mem_prompt.reshape(batch * N_MEM, D_MODEL)

    z_s, w_in_b = _norm_matmul(xs2, row(g_pre[l]), w_in[l], tm=nb, tn=1024, emit_w=True)
    attn_s = _sample_attn(z_s, _cache_rows(cache_mem_k[l]), _cache_rows(cache_mem_v[l]), bb=8)
    o_s, h_s, c_s, p_s = _sample_mix(
        z_s, attn_s, state_conv[l].transpose(1, 0, 2), state_rglru_h[l],
        state_pool[l].transpose(1, 0, 2), *mix_params, tb=32)
    y_s = _branch_out(o_s, z_s, xs2, w_br_b, w_out_b, row(g_post[l]), tm=nb)

    kv = _norm_matmul(mem2, row(g_mem[l]), w_kv[l], tm=512, tn=1024)
    mem_k = kv[:, :D_X].reshape(batch, N_MEM, D_X)
    mem_v = kv[:, D_X:].reshape(batch, N_MEM, D_X)

    perm = _chunk_interleave()
    z_p = _norm_matmul(xp2, row(g_pre[l]), w_in_b, tm=1024, tn=1024, perm=perm)
    o_p, h_p, c_p, p_p = _prompt_mix(z_p, mem_k, mem_v, *mix_params,
                                     batch=batch, seq=seq, tm=MIX_TM)
    y_p = _branch_out(o_p, z_p, xp2, w_br_b, w_out_b, row(g_post[l]), tm=MIX_TM, unperm=perm.T)

    return (
        y_p.reshape(batch, seq, D_MODEL),
        y_s.reshape(nb, 1, D_MODEL),
        h_p.reshape(1, batch, D_RNN),
        c_p.reshape(1, batch, CONV_W - 1, D_RNN),
        p_p.reshape(1, batch, POOL_HIST, D_POOL),
        mem_k.reshape(1, batch, N_MEM, N_XHEADS, XHEAD_DIM),
        mem_v.reshape(1, batch, N_MEM, N_XHEADS, XHEAD_DIM),
        h_s.reshape(1, nb, D_RNN),
        c_s.transpose(1, 0, 2)[None],
        p_s.transpose(1, 0, 2)[None],
    )
```

```python
import functools

import jax
import jax.numpy as jnp
from jax import lax
from jax.experimental import pallas as pl
from jax.experimental.pallas import tpu as pltpu

D_MODEL = 2048
PAST_LEN = 16384
D_RNN = 1024
N_RNN_BLOCKS = 8
RNN_BLOCK = D_RNN // N_RNN_BLOCKS
CONV_W = 4
LRU_C = 8.0
D_POOL = 1024
POOL_WINDOWS = (2, 4, 8, 16)
POOL_GROUP = D_POOL // len(POOL_WINDOWS)
POOL_HIST = max(POOL_WINDOWS) - 1
N_MEM = 256
N_XHEADS = 4
XHEAD_DIM = 256
D_X = N_XHEADS * XHEAD_DIM
N_BRANCH = 3
D_MIX = D_RNN + D_POOL + D_X
D_IN = 2 * D_MIX + N_BRANCH * D_MODEL
EPS = 1e-6

SUBLANES = 8
LANES = 128
VMEM_LIMIT = 56 * 1024 * 1024
MIX_TM = 256

BF16 = jnp.bfloat16
F32 = jnp.float32

NEG_LOG2_E = -1.4426950408889634


def _sigmoid(x):
    return 1.0 / (1.0 + jnp.exp2(x * NEG_LOG2_E))


def _silu(x):
    return x * _sigmoid(x)


def _softplus(x):
    return jnp.maximum(x, 0.0) + jnp.log1p(jnp.exp(-jnp.abs(x)))


def _rms_scale(x):
    return lax.rsqrt(jnp.mean(x * x, axis=-1, keepdims=True) + EPS)


def _chunk_interleave():
    nrow = MIX_TM // SUBLANES
    p = jnp.arange(MIX_TM)
    token = (p % SUBLANES) * nrow + p // SUBLANES
    return (token[:, None] == jnp.arange(MIX_TM)[None, :]).astype(BF16)


def _norm_matmul_kernel(x_ref, g_ref, w_ref, *rest, emit_w, interleave):
    rest = list(rest)
    perm_ref = rest.pop(0) if interleave else None
    o_ref = rest.pop(0)
    u_ref = rest[-1]

    @pl.when(pl.program_id(1) == 0)
    def _():
        x = x_ref[...]
        u = (x * _rms_scale(x) * g_ref[...]).astype(BF16)
        if interleave:
            for r0 in range(0, u.shape[0], MIX_TM):
                u_ref[r0:r0 + MIX_TM, :] = jnp.dot(
                    perm_ref[...], u[r0:r0 + MIX_TM], preferred_element_type=F32).astype(BF16)
        else:
            u_ref[...] = u

    w = w_ref[...].astype(BF16)
    if emit_w:
        rest[0][...] = w
    o_ref[...] = jnp.dot(u_ref[...], w, preferred_element_type=F32)


def _norm_matmul(x, g, w, tm, tn, emit_w=False, perm=None):
    m, k = x.shape
    n = w.shape[1]
    in_specs = [
        pl.BlockSpec((tm, k), lambda i, j: (i, 0)),
        pl.BlockSpec((1, k), lambda i, j: (0, 0)),
        pl.BlockSpec((k, tn), lambda i, j: (0, j)),
    ]
    args = [x, g, w]
    if perm is not None:
        in_specs.append(pl.BlockSpec(perm.shape, lambda i, j: (0, 0)))
        args.append(perm)
    out_specs = [pl.BlockSpec((tm, tn), lambda i, j: (i, j))]
    out_shape = [jax.ShapeDtypeStruct((m, n), F32)]
    if emit_w:
        assert m == tm, "the bf16 weight copy is written once per column block"
        out_specs.append(pl.BlockSpec((k, tn), lambda i, j: (0, j)))
        out_shape.append(jax.ShapeDtypeStruct((k, n), BF16))
    res = pl.pallas_call(
        functools.partial(_norm_matmul_kernel, emit_w=emit_w, interleave=perm is not None),
        grid=(m // tm, n // tn),
        in_specs=in_specs,
        out_specs=out_specs,
        out_shape=out_shape,
        scratch_shapes=[pltpu.VMEM((tm, k), BF16)],
        compiler_params=pltpu.CompilerParams(
            dimension_semantics=("arbitrary", "arbitrary"),
            vmem_limit_bytes=VMEM_LIMIT),
        name="norm_matmul",
    )(*args)
    return res if emit_w else res[0]


def _rglru_block(xc, wax, ba, bx, sp):
    ri = jnp.dot(xc.astype(BF16), wax, preferred_element_type=F32)
    r = _sigmoid(ri[:, :RNN_BLOCK] + ba)
    i = _sigmoid(ri[:, RNN_BLOCK:] + bx)
    a = jnp.exp((-LRU_C) * r * sp)
    one_m = 1.0 - a * a
    mult = jnp.where(one_m > 0.0, one_m * lax.rsqrt(one_m), 0.0)
    return a, mult * i * xc


def _prompt_mix_kernel(z_ref, k_ref, v_ref, convw_ref, convb_ref, wax_ref, ba_ref, bx_ref,
                       lam_ref, wpool_ref, pscale_ref,
                       o_ref, newh_ref, newconv_ref, newpool_ref,
                       conv_carry, pool_carry, h_carry, kb_ref, vb_ref, ac_scr, hl_scr, *, tm):
    l = pl.program_id(1)
    last = pl.num_programs(1) - 1
    nrow = tm // SUBLANES

    @pl.when(l == 0)
    def _():
        conv_carry[...] = jnp.zeros(conv_carry.shape, F32)
        pool_carry[...] = jnp.zeros(pool_carry.shape, F32)
        h_carry[...] = jnp.zeros(h_carry.shape, F32)
        kb_ref[...] = k_ref[0].astype(BF16)
        vb_ref[...] = v_ref[0].astype(BF16)

    chunk_id = lax.broadcasted_iota(jnp.int32, (SUBLANES, LANES), 0)
    first_chunk = chunk_id == 0

    def load_groups(col, width=LANES):
        return [z_ref[r * SUBLANES:(r + 1) * SUBLANES, col:col + width] for r in range(nrow)]

    def store_groups(col, rows, width=LANES):
        o_ref[:, col:col + width] = jnp.concatenate(rows, axis=0).astype(BF16)

    def history(tail_group, carry_ref, j, c0):
        tail = pltpu.roll(tail_group, 1, 0)
        prev = jnp.where(first_chunk, carry_ref[j - 1, :, c0:c0 + LANES], tail)
        carry_ref[j - 1, :, c0:c0 + LANES] = tail
        return prev

    sp = _softplus(-lam_ref[...])
    for n in range(N_RNN_BLOCKS):
        c0, c1 = n * RNN_BLOCK, (n + 1) * RNN_BLOCK
        xs = load_groups(c0)
        ext = [history(xs[nrow - j], conv_carry, j, c0) for j in range(CONV_W - 1, 0, -1)] + xs
        cw = [jnp.broadcast_to(convw_ref[k:k + 1, c0:c1], (SUBLANES, LANES)) for k in range(CONV_W)]
        cb = jnp.broadcast_to(convb_ref[:, c0:c1], (SUBLANES, LANES))
        xc = []
        for r in range(nrow):
            acc = cb + cw[0] * ext[r]
            for k in range(1, CONV_W):
                acc = acc + cw[k] * ext[r + k]
            xc.append(acc)
        a, b = _rglru_block(jnp.concatenate(xc, axis=0), wax_ref[n], ba_ref[:, c0:c1],
                            bx_ref[:, c0:c1], sp[:, c0:c1])
        ac_scr[:, c0:c1] = a
        hl_scr[:, c0:c1] = b

    acc_a = ac_scr[0:SUBLANES, :]
    acc_h = hl_scr[0:SUBLANES, :]
    for r in range(1, nrow):
        rows = slice(r * SUBLANES, (r + 1) * SUBLANES)
        ar = ac_scr[rows, :]
        acc_h = ar * acc_h + hl_scr[rows, :]
        acc_a = ar * acc_a
        ac_scr[rows, :] = acc_a
        hl_scr[rows, :] = acc_h
    h_in = h_carry[...]
    entering = []
    for c in range(SUBLANES):
        entering.append(h_in)
        h_in = acc_a[c:c + 1] * h_in + acc_h[c:c + 1]
    h_carry[...] = h_in
    h_enter = jnp.concatenate(entering, axis=0)
    for n in range(N_RNN_BLOCKS):
        c0, c1 = n * RNN_BLOCK, (n + 1) * RNN_BLOCK
        gr = load_groups(D_RNN + c0)
        store_groups(c0, [(hl_scr[r * SUBLANES:(r + 1) * SUBLANES, c0:c1]
                           + ac_scr[r * SUBLANES:(r + 1) * SUBLANES, c0:c1] * h_enter[:, c0:c1])
                          * _silu(gr[r]) for r in range(nrow)])

    pcol = 2 * D_RNN
    blocks = [(w, c0) for g, w in enumerate(POOL_WINDOWS)
              for c0 in range(g * POOL_GROUP, (g + 1) * POOL_GROUP, LANES)]

    def group(c0, r):
        return z_ref[r * SUBLANES:(r + 1) * SUBLANES, pcol + c0:pcol + c0 + LANES]

    def mean_minus_token(tot, w, c0, r):
        if r < w - 1:
            pos1 = l * tm + chunk_id * nrow + (r + 1)
            mean = tot / jnp.minimum(pos1, w).astype(F32)
        else:
            mean = tot * (1.0 / w)
        return mean - group(c0, r)

    hist, tot = {}, {}
    for w, c0 in blocks:
        hist[c0] = [history(group(c0, nrow - j), pool_carry, j, c0) for j in range(1, w)]
        t = group(c0, 0)
        for h in hist[c0]:
            t = t + h
        tot[c0] = t
        hl_scr[0:SUBLANES, c0:c0 + LANES] = mean_minus_token(t, w, c0, 0)
    for r in range(1, nrow):
        for w, c0 in blocks:
            leaving = group(c0, r - w) if r >= w else hist[c0][w - r - 1]
            tot[c0] = tot[c0] + (group(c0, r) - leaving)
            hl_scr[r * SUBLANES:(r + 1) * SUBLANES, c0:c0 + LANES] = mean_minus_token(
                tot[c0], w, c0, r)
    for g, w in enumerate(POOL_WINDOWS):
        c0, c1 = g * POOL_GROUP, (g + 1) * POOL_GROUP
        og = jnp.dot(hl_scr[:, c0:c1].astype(BF16), wpool_ref[g], preferred_element_type=F32)
        gp = z_ref[:, pcol + D_POOL + c0:pcol + D_POOL + c1]
        o_ref[:, D_RNN + c0:D_RNN + c1] = (og * pscale_ref[:, c0:c1] * _silu(gp)).astype(BF16)

    qoff = 2 * D_RNN + 2 * D_POOL
    for hd in range(N_XHEADS):
        c0, c1 = hd * XHEAD_DIM, (hd + 1) * XHEAD_DIM
        q = z_ref[:, qoff + c0:qoff + c1].astype(BF16)
        s = lax.dot_general(q, kb_ref[:, c0:c1], (((1,), (1,)), ((), ())),
                            preferred_element_type=F32) * (XHEAD_DIM ** -0.5)
        p = jnp.exp(s - jnp.max(s, axis=-1, keepdims=True))
        p = p / jnp.sum(p, axis=-1, keepdims=True)
        ox = jnp.dot(p.astype(BF16), vb_ref[:, c0:c1], preferred_element_type=F32)
        gx = z_ref[:, qoff + D_X + c0:qoff + D_X + c1]
        o_ref[:, D_RNN + D_POOL + c0:D_RNN + D_POOL + c1] = (ox * _silu(gx)).astype(BF16)

    @pl.when(l == last)
    def _():
        newh_ref[0] = h_carry[...]
        tail_row = lambda j: (nrow - j) * SUBLANES + SUBLANES - 1
        for j in range(1, CONV_W):
            newconv_ref[0, CONV_W - 1 - j:CONV_W - j, :] = z_ref[tail_row(j):tail_row(j) + 1, 0:D_RNN]
        for j in range(1, POOL_HIST + 1):
            newpool_ref[0, POOL_HIST - j:POOL_HIST - j + 1, :] = (
                z_ref[tail_row(j):tail_row(j) + 1, pcol:pcol + D_POOL])


def _prompt_mix(z, mem_k, mem_v, conv_w, conv_b, wax, b_a, b_x, lam, wpool, pscale,
                batch, seq, tm):
    nl = seq // tm
    zw = 2 * D_MIX
    const2 = lambda b, l: (0, 0)
    const3 = lambda b, l: (0, 0, 0)
    kern = functools.partial(_prompt_mix_kernel, tm=tm)
    return pl.pallas_call(
        kern,
        grid=(batch, nl),
        in_specs=[
            pl.BlockSpec((tm, zw), lambda b, l: (b * nl + l, 0)),
            pl.BlockSpec((1, N_MEM, D_X), lambda b, l: (b, 0, 0)),
            pl.BlockSpec((1, N_MEM, D_X), lambda b, l: (b, 0, 0)),
            pl.BlockSpec((CONV_W, D_RNN), const2),
            pl.BlockSpec((1, D_RNN), const2),
            pl.BlockSpec((N_RNN_BLOCKS, RNN_BLOCK, 2 * RNN_BLOCK), const3),
            pl.BlockSpec((1, D_RNN), const2),
            pl.BlockSpec((1, D_RNN), const2),
            pl.BlockSpec((1, D_RNN), const2),
            pl.BlockSpec((len(POOL_WINDOWS), POOL_GROUP, POOL_GROUP), const3),
            pl.BlockSpec((1, D_POOL), const2),
        ],
        out_specs=[
            pl.BlockSpec((tm, D_MIX), lambda b, l: (b * nl + l, 0)),
            pl.BlockSpec((1, 1, D_RNN), lambda b, l: (b, 0, 0)),
            pl.BlockSpec((1, CONV_W - 1, D_RNN), lambda b, l: (b, 0, 0)),
            pl.BlockSpec((1, POOL_HIST, D_POOL), lambda b, l: (b, 0, 0)),
        ],
        out_shape=[
            jax.ShapeDtypeStruct((batch * seq, D_MIX), BF16),
            jax.ShapeDtypeStruct((batch, 1, D_RNN), F32),
            jax.ShapeDtypeStruct((batch, CONV_W - 1, D_RNN), F32),
            jax.ShapeDtypeStruct((batch, POOL_HIST, D_POOL), F32),
        ],
        scratch_shapes=[
            pltpu.VMEM((CONV_W - 1, SUBLANES, D_RNN), F32),
            pltpu.VMEM((POOL_HIST, SUBLANES, D_POOL), F32),
            pltpu.VMEM((1, D_RNN), F32),
            pltpu.VMEM((N_MEM, D_X), BF16),
            pltpu.VMEM((N_MEM, D_X), BF16),
            pltpu.VMEM((tm, D_RNN), F32),
            pltpu.VMEM((tm, D_RNN), F32),
        ],
        compiler_params=pltpu.CompilerParams(
            dimension_semantics=("arbitrary", "arbitrary"),
            vmem_limit_bytes=VMEM_LIMIT),
        name="prompt_mix",
    )(z, mem_k, mem_v, conv_w, conv_b, wax, b_a, b_x, lam, wpool, pscale)


def _cache_rows(c):
    nb = c.shape[0]
    c = c.reshape(nb, N_MEM, N_XHEADS, XHEAD_DIM // LANES, LANES)
    return c.transpose(0, 1, 3, 2, 4).reshape(nb, N_MEM * SUBLANES, LANES)


def _sample_attn_kernel(q_ref, k_ref, v_ref, o_ref, *, bb):
    halves = XHEAD_DIM // LANES
    assert halves * N_XHEADS == SUBLANES
    r = lax.broadcasted_iota(jnp.int32, (SUBLANES, LANES), 0)
    c = lax.broadcasted_iota(jnp.int32, (SUBLANES, LANES), 1)
    diag = (c % SUBLANES) == r
    first_half = r < N_XHEADS
    nchunk = N_MEM * SUBLANES // LANES
    scores = []
    for j in range(bb):
        qn = jnp.concatenate(
            [q_ref[j:j + 1, (h * halves + t) * LANES:(h * halves + t + 1) * LANES]
             for t in range(halves) for h in range(N_XHEADS)], axis=0)
        scores.append(lax.dot_general(qn.astype(BF16), k_ref[j].astype(BF16),
                                      (((1,), (1,)), ((), ())), preferred_element_type=F32)
                      * (XHEAD_DIM ** -0.5))
    probs = []
    for j in range(bb):
        s = scores[j]
        chunks = []
        for ci in range(nchunk):
            sm = jnp.where(diag, s[:, ci * LANES:(ci + 1) * LANES], 0.0)
            other = pltpu.roll(sm, N_XHEADS, 0)
            other = jnp.where(first_half, pltpu.roll(other, LANES - N_XHEADS, 1),
                              pltpu.roll(other, N_XHEADS, 1))
            chunks.append(jnp.where(diag, sm + other, -jnp.inf))
        t_full = jnp.concatenate(chunks, axis=1)
        e = jnp.exp(t_full - jnp.max(t_full, axis=1, keepdims=True))
        probs.append((e / jnp.sum(e, axis=1, keepdims=True)).astype(BF16))
    for j in range(bb):
        o = jnp.dot(probs[j], v_ref[j].astype(BF16), preferred_element_type=F32)
        for t in range(halves):
            for h in range(N_XHEADS):
                col = (h * halves + t) * LANES
                o_ref[j:j + 1, col:col + LANES] = o[t * N_XHEADS + h:t * N_XHEADS + h + 1, :]


def _sample_attn(z, cache_k, cache_v, bb):
    nb = z.shape[0]
    qblk = (2 * D_RNN + 2 * D_POOL) // D_X
    return pl.pallas_call(
        functools.partial(_sample_attn_kernel, bb=bb),
        grid=(nb // bb,),
        in_specs=[
            pl.BlockSpec((bb, D_X), lambda i: (i, qblk)),
            pl.BlockSpec((bb, N_MEM * SUBLANES, LANES), lambda i: (i, 0, 0)),
            pl.BlockSpec((bb, N_MEM * SUBLANES, LANES), lambda i: (i, 0, 0)),
        ],
        out_specs=pl.BlockSpec((bb, D_X), lambda i: (i, 0)),
        out_shape=jax.ShapeDtypeStruct((nb, D_X), F32),
        compiler_params=pltpu.CompilerParams(
            dimension_semantics=("arbitrary",),
            vmem_limit_bytes=VMEM_LIMIT),
        name="sample_attn",
    )(z, cache_k, cache_v)


def _sample_mix_kernel(z_ref, attn_ref, conv_ref, h_ref, pool_ref,
                       convw_ref, convb_ref, wax_ref, ba_ref, bx_ref, lam_ref, wpool_ref,
                       pscale_ref, o_ref, newh_ref, newconv_ref, newpool_ref):
    xr = z_ref[:, 0:D_RNN]
    xc = convb_ref[...] + convw_ref[CONV_W - 1:CONV_W, :] * xr
    for k in range(CONV_W - 1):
        xc = xc + convw_ref[k:k + 1, :] * conv_ref[k]
    for k in range(CONV_W - 2):
        newconv_ref[k] = conv_ref[k + 1]
    newconv_ref[CONV_W - 2] = xr

    sp = _softplus(-lam_ref[...])
    for n in range(N_RNN_BLOCKS):
        c0, c1 = n * RNN_BLOCK, (n + 1) * RNN_BLOCK
        a, b = _rglru_block(xc[:, c0:c1], wax_ref[n], ba_ref[:, c0:c1], bx_ref[:, c0:c1],
                            sp[:, c0:c1])
        h = a * h_ref[:, c0:c1] + b
        newh_ref[:, c0:c1] = h
        o_ref[:, c0:c1] = (h * _silu(z_ref[:, D_RNN + c0:D_RNN + c1])).astype(BF16)

    xp = z_ref[:, 2 * D_RNN:2 * D_RNN + D_POOL]
    for k in range(POOL_HIST - 1):
        newpool_ref[k] = pool_ref[k + 1]
    newpool_ref[POOL_HIST - 1] = xp
    for g, w in enumerate(POOL_WINDOWS):
        c0, c1 = g * POOL_GROUP, (g + 1) * POOL_GROUP
        xg = xp[:, c0:c1]
        tot = xg
        for j in range(1, w):
            tot = tot + pool_ref[POOL_HIST - j, :, c0:c1]
        cnt = float(min(PAST_LEN + 1, w))
        d = tot / cnt - xg
        og = jnp.dot(d.astype(BF16), wpool_ref[g], preferred_element_type=F32)
        gp = z_ref[:, 2 * D_RNN + D_POOL + c0:2 * D_RNN + D_POOL + c1]
        o_ref[:, D_RNN + c0:D_RNN + c1] = (og * pscale_ref[:, c0:c1] * _silu(gp)).astype(BF16)

    gx = z_ref[:, 2 * D_RNN + 2 * D_POOL + D_X:2 * D_MIX]
    o_ref[:, D_RNN + D_POOL:] = (attn_ref[...] * _silu(gx)).astype(BF16)


def _sample_mix(z, attn, conv, h, pool, conv_w, conv_b, wax, b_a, b_x, lam, wpool, pscale, tb):
    nb = z.shape[0]
    zw = 2 * D_MIX
    rows = lambda i: (i, 0)
    const2 = lambda i: (0, 0)
    const3 = lambda i: (0, 0, 0)
    hist = lambda i: (0, i, 0)
    return pl.pallas_call(
        _sample_mix_kernel,
        grid=(nb // tb,),
        in_specs=[
            pl.BlockSpec((tb, zw), rows),
            pl.BlockSpec((tb, D_X), rows),
            pl.BlockSpec((CONV_W - 1, tb, D_RNN), hist),
            pl.BlockSpec((tb, D_RNN), rows),
            pl.BlockSpec((POOL_HIST, tb, D_POOL), hist),
            pl.BlockSpec((CONV_W, D_RNN), const2),
            pl.BlockSpec((1, D_RNN), const2),
            pl.BlockSpec((N_RNN_BLOCKS, RNN_BLOCK, 2 * RNN_BLOCK), const3),
            pl.BlockSpec((1, D_RNN), const2),
            pl.BlockSpec((1, D_RNN), const2),
            pl.BlockSpec((1, D_RNN), const2),
            pl.BlockSpec((len(POOL_WINDOWS), POOL_GROUP, POOL_GROUP), const3),
            pl.BlockSpec((1, D_POOL), const2),
        ],
        out_specs=[
            pl.BlockSpec((tb, D_MIX), rows),
            pl.BlockSpec((tb, D_RNN), rows),
            pl.BlockSpec((CONV_W - 1, tb, D_RNN), hist),
            pl.BlockSpec((POOL_HIST, tb, D_POOL), hist),
        ],
        out_shape=[
            jax.ShapeDtypeStruct((nb, D_MIX), BF16),
            jax.ShapeDtypeStruct((nb, D_RNN), F32),
            jax.ShapeDtypeStruct((CONV_W - 1, nb, D_RNN), F32),
            jax.ShapeDtypeStruct((POOL_HIST, nb, D_POOL), F32),
        ],
        compiler_params=pltpu.CompilerParams(
            dimension_semantics=("arbitrary",),
            vmem_limit_bytes=VMEM_LIMIT),
        name="sample_mix",
    )(z, attn, conv, h, pool, conv_w, conv_b, wax, b_a, b_x, lam, wpool, pscale)


def _branch_out_kernel(o_ref, gates_ref, x_ref, wb_ref, wo_ref, gpost_ref, *rest, interleaved):
    y_ref = rest[-1]
    merged = None
    for j, (r0, r1) in enumerate(((0, D_RNN), (D_RNN, D_RNN + D_POOL), (D_RNN + D_POOL, D_MIX))):
        yj = jnp.dot(o_ref[:, r0:r1], wb_ref[r0:r1, :], preferred_element_type=F32)
        term = _sigmoid(gates_ref[:, j * D_MODEL:(j + 1) * D_MODEL]) * yj
        merged = term if merged is None else merged + term
    merged = merged.astype(BF16)
    if interleaved:
        merged = jnp.dot(rest[0][...], merged, preferred_element_type=F32).astype(BF16)
    out = jnp.dot(merged, wo_ref[...], preferred_element_type=F32)
    y_ref[...] = x_ref[...] + out * _rms_scale(out) * gpost_ref[...]


def _branch_out(o, z, x, wb, wo, g_post, tm, unperm=None):
    m = x.shape[0]
    gw = N_BRANCH * D_MODEL
    gblk = (2 * D_MIX) // gw
    resident = pl.Buffered(1)
    in_specs = [
        pl.BlockSpec((tm, D_MIX), lambda i: (i, 0)),
        pl.BlockSpec((tm, gw), lambda i: (i, gblk)),
        pl.BlockSpec((tm, D_MODEL), lambda i: (i, 0)),
        pl.BlockSpec((D_MIX, D_MODEL), lambda i: (0, 0), pipeline_mode=resident),
        pl.BlockSpec((D_MODEL, D_MODEL), lambda i: (0, 0), pipeline_mode=resident),
        pl.BlockSpec((1, D_MODEL), lambda i: (0, 0)),
    ]
    args = [o, z, x, wb, wo, g_post]
    if unperm is not None:
        assert unperm.shape == (tm, tm)
        in_specs.append(pl.BlockSpec(unperm.shape, lambda i: (0, 0)))
        args.append(unperm)
    return pl.pallas_call(
        functools.partial(_branch_out_kernel, interleaved=unperm is not None),
        grid=(m // tm,),
        in_specs=in_specs,
        out_specs=pl.BlockSpec((tm, D_MODEL), lambda i: (i, 0)),
        out_shape=jax.ShapeDtypeStruct((m, D_MODEL), F32),
        compiler_params=pltpu.CompilerParams(
            dimension_semantics=("arbitrary",),
            vmem_limit_bytes=VMEM_LIMIT),
        name="branch_out",
    )(*args)


WROWS = 512
PER_BRANCH = D_RNN // WROWS
assert D_RNN == D_POOL == D_X and D_RNN % WROWS == 0 and D_MODEL % WROWS == 0
N_WB_BLOCKS = N_BRANCH * PER_BRANCH
N_WOUT_BLOCKS = D_MODEL // WROWS


def _branch_out_cast_kernel(o_ref, gates_ref, x_ref, wb_ref, wo_ref, gpost_ref,
                            y_ref, wbb_ref, wob_ref, merged_ref, out_ref):
    s = pl.program_id(0)

    @pl.when(s < N_WB_BLOCKS)
    def _():
        w = wb_ref[...].astype(BF16)
        wbb_ref[...] = w
        term = _sigmoid(gates_ref[...]) * jnp.dot(o_ref[...], w, preferred_element_type=F32)

        @pl.when(s == 0)
        def _():
            merged_ref[...] = term

        @pl.when(s > 0)
        def _():
            merged_ref[...] += term

    for kb in range(N_WOUT_BLOCKS):
        @pl.when(s == N_WB_BLOCKS + kb)
        def _(kb=kb):
            w = wo_ref[...].astype(BF16)
            wob_ref[...] = w
            part = jnp.dot(merged_ref[:, kb * WROWS:(kb + 1) * WROWS].astype(BF16), w,
                           preferred_element_type=F32)
            if kb == 0:
                out_ref[...] = part
            else:
                out_ref[...] += part

    @pl.when(s == N_WB_BLOCKS + N_WOUT_BLOCKS - 1)
    def _():
        out = out_ref[...]
        y_ref[...] = x_ref[...] + out * _rms_scale(out) * gpost_ref[...]


def _branch_out_cast(o, z, x, wb, wo, g_post):
    m = x.shape[0]
    gblk0 = (2 * D_MIX) // D_MODEL
    wb_blk = lambda s: jnp.minimum(s, N_WB_BLOCKS - 1)
    wo_blk = lambda s: jnp.maximum(s - N_WB_BLOCKS, 0)
    return pl.pallas_call(
        _branch_out_cast_kernel,
        grid=(N_WB_BLOCKS + N_WOUT_BLOCKS,),
        in_specs=[
            pl.BlockSpec((m, WROWS), lambda s: (0, wb_blk(s))),
            pl.BlockSpec((m, D_MODEL), lambda s: (0, gblk0 + wb_blk(s) // PER_BRANCH)),
            pl.BlockSpec((m, D_MODEL), lambda s: (0, 0)),
            pl.BlockSpec((WROWS, D_MODEL), lambda s: (wb_blk(s), 0)),
            pl.BlockSpec((WROWS, D_MODEL), lambda s: (wo_blk(s), 0)),
            pl.BlockSpec((1, D_MODEL), lambda s: (0, 0)),
        ],
        out_specs=[
            pl.BlockSpec((m, D_MODEL), lambda s: (0, 0)),
            pl.BlockSpec((WROWS, D_MODEL), lambda s: (wb_blk(s), 0)),
            pl.BlockSpec((WROWS, D_MODEL), lambda s: (wo_blk(s), 0)),
        ],
        out_shape=[
            jax.ShapeDtypeStruct((m, D_MODEL), F32),
            jax.ShapeDtypeStruct(wb.shape, BF16),
            jax.ShapeDtypeStruct(wo.shape, BF16),
        ],
        scratch_shapes=[pltpu.VMEM((m, D_MODEL), F32), pltpu.VMEM((m, D_MODEL), F32)],
        compiler_params=pltpu.CompilerParams(
            dimension_semantics=("arbitrary",),
            vmem_limit_bytes=VMEM_LIMIT),
        name="branch_out_cast",
    )(o, z, x, wb, wo, g_post)


def _mem_kv_kernel(x_ref, g_ref, w_ref, k_ref, v_ref, u_ref):
    j = pl.program_id(1)

    @pl.when(j == 0)
    def _():
        x = x_ref[...]
        u_ref[...] = (x * _rms_scale(x) * g_ref[...]).astype(BF16)

    res = jnp.dot(u_ref[...], w_ref[...].astype(BF16), preferred_element_type=F32)

    @pl.when(j == 0)
    def _():
        k_ref[...] = res

    @pl.when(j == 1)
    def _():
        v_ref[...] = res


def _mem_kv(x, g, w, tm):
    m, k = x.shape
    assert w.shape[1] == 2 * D_X
    half = pl.BlockSpec((tm, D_X), lambda i, j: (i, 0))
    return pl.pallas_call(
        _mem_kv_kernel,
        grid=(m // tm, 2),
        in_specs=[
            pl.BlockSpec((tm, k), lambda i, j: (i, 0)),
            pl.BlockSpec((1, k), lambda i, j: (0, 0)),
            pl.BlockSpec((k, D_X), lambda i, j: (0, j)),
        ],
        out_specs=[half, half],
        out_shape=[jax.ShapeDtypeStruct((m, D_X), F32)] * 2,
        scratch_shapes=[pltpu.VMEM((tm, k), BF16)],
        compiler_params=pltpu.CompilerParams(
            dimension_semantics=("arbitrary", "arbitrary"),
            vmem_limit_bytes=VMEM_LIMIT),
        name="mem_kv",
    )(x, g, w)


def kernel(x_prompt, x_sample, mem_prompt, state_rglru_h, state_conv, state_pool, cache_mem_k, cache_mem_v, g_pre, w_in, conv_w, conv_b, w_rg_a, b_rg_a, w_rg_x, b_rg_x, lru_lambda, w_pool, pool_scale, g_mem, w_kv, w_branch, w_out, g_post):
    batch, seq, _ = x_prompt.shape
    nb = x_sample.shape[0]
    depth = g_pre.shape[0]
    assert depth == 1 and x_sample.shape[1] == 1

    l = 0
    row = lambda v: v.reshape(1, -1)
    wax = jnp.concatenate([w_rg_a[l], w_rg_x[l]], axis=-1).astype(BF16)
    wpool = w_pool[l].astype(BF16)
    mix_params = (conv_w[l], row(conv_b[l]), wax, row(b_rg_a[l]), row(b_rg_x[l]),
                  row(lru_lambda[l]), wpool, row(pool_scale[l]))

    xp2 = x_prompt.reshape(batch * seq, D_MODEL)
    xs2 = x_sample.reshape(nb, D_MODEL)
    mem2 = mem_prompt.reshape(batch * N_MEM, D_MODEL)

    z_s, w_in_b = _norm_matmul(xs2, row(g_pre[l]), w_in[l], tm=nb, tn=1024, emit_w=True)
    attn_s = _sample_attn(z_s, _cache_rows(cache_mem_k[l]), _cache_rows(cache_mem_v[l]), bb=8)
    o_s, h_s, c_s, p_s = _sample_mix(
        z_s, attn_s, state_conv[l].transpose(1, 0, 2), state_rglru_h[l],
        state_pool[l].transpose(1, 0, 2), *mix_params, tb=32)
    y_s, w_br_b, w_out_b = _branch_out_cast(o_s, z_s, xs2, w_branch[l], w_out[l], row(g_post[l]))

    mem_k, mem_v = _mem_kv(mem2, row(g_mem[l]), w_kv[l], tm=512)
    mem_k = mem_k.reshape(batch, N_MEM, D_X)
    mem_v = mem_v.reshape(batch, N_MEM, D_X)

    perm = _chunk_interleave()
    z_p = _norm_matmul(xp2, row(g_pre[l]), w_in_b, tm=1024, tn=1024, perm=perm)
    o_p, h_p, c_p, p_p = _prompt_mix(z_p, mem_k, mem_v, *mix_params,
                                     batch=batch, seq=seq, tm=MIX_TM)
    y_p = _branch_out(o_p, z_p, xp2, w_br_b, w_out_b, row(g_post[l]), tm=MIX_TM, unperm=perm.T)

    return (
        y_p.reshape(batch, seq, D_MODEL),
        y_s.reshape(nb, 1, D_MODEL),
        h_p.reshape(1, batch, D_RNN),
        c_p.reshape(1, batch, CONV_W - 1, D_RNN),
        p_p.reshape(1, batch, POOL_HIST, D_POOL),
        mem_k.reshape(1, batch, N_MEM, N_XHEADS, XHEAD_DIM),
        mem_v.reshape(1, batch, N_MEM, N_XHEADS, XHEAD_DIM),
        h_s.reshape(1, nb, D_RNN),
        c_s.transpose(1, 0, 2)[None],
        p_s.transpose(1, 0, 2)[None],
    )
```

```python
import functools
import math

import jax
import jax.numpy as jnp
from jax import lax
from jax.experimental import pallas as pl
from jax.experimental.pallas import tpu as pltpu

D_MODEL = 2048
PAST_LEN = 16384
D_RNN = 1024
N_RNN_BLOCKS = 8
RNN_BLOCK = D_RNN // N_RNN_BLOCKS
CONV_W = 4
LRU_C = 8.0
D_POOL = 1024
POOL_WINDOWS = (2, 4, 8, 16)
POOL_GROUP = D_POOL // len(POOL_WINDOWS)
POOL_HIST = max(POOL_WINDOWS) - 1
N_MEM = 256
N_XHEADS = 4
XHEAD_DIM = 256
D_X = N_XHEADS * XHEAD_DIM
N_BRANCH = 3
D_MIX = D_RNN + D_POOL + D_X
D_IN = 2 * D_MIX + N_BRANCH * D_MODEL
EPS = 1e-6

SUBLANES = 8
LANES = 128
VMEM_LIMIT = 56 * 1024 * 1024
MIX_TM = 256

BF16 = jnp.bfloat16
F32 = jnp.float32

NEG_LOG2_E = -1.4426950408889634


def _sigmoid(x):
    return 1.0 / (1.0 + jnp.exp2(x * NEG_LOG2_E))


def _silu(x):
    return x * _sigmoid(x)


def _softplus(x):
    return jnp.maximum(x, 0.0) + jnp.log1p(jnp.exp(-jnp.abs(x)))


def _rms_scale(x):
    return lax.rsqrt(jnp.mean(x * x, axis=-1, keepdims=True) + EPS)


def _chunk_interleave():
    nrow = MIX_TM // SUBLANES
    p = jnp.arange(MIX_TM)
    token = (p % SUBLANES) * nrow + p // SUBLANES
    return (token[:, None] == jnp.arange(MIX_TM)[None, :]).astype(BF16)


def _sample_proj_kernel(x_ref, g_ref, w_ref, o_ref, wb_ref, u_ref):
    @pl.when(pl.program_id(0) == 0)
    def _():
        x = x_ref[...]
        u_ref[...] = (x * _rms_scale(x) * g_ref[...]).astype(BF16)

    w = w_ref[...].astype(BF16)
    wb_ref[...] = w
    o_ref[...] = jnp.dot(u_ref[...], w, preferred_element_type=F32)


def _sample_proj(x, g, w, tn):
    m, k = x.shape
    n = w.shape[1]
    return pl.pallas_call(
        _sample_proj_kernel,
        grid=(n // tn,),
        in_specs=[
            pl.BlockSpec((m, k), lambda j: (0, 0)),
            pl.BlockSpec((1, k), lambda j: (0, 0)),
            pl.BlockSpec((k, tn), lambda j: (0, j)),
        ],
        out_specs=[
            pl.BlockSpec((m, tn), lambda j: (0, j)),
            pl.BlockSpec((k, tn), lambda j: (0, j)),
        ],
        out_shape=[
            jax.ShapeDtypeStruct((m, n), F32),
            jax.ShapeDtypeStruct((k, n), BF16),
        ],
        scratch_shapes=[pltpu.VMEM((m, k), BF16)],
        compiler_params=pltpu.CompilerParams(
            dimension_semantics=("arbitrary",),
            vmem_limit_bytes=VMEM_LIMIT),
        name="sample_proj",
    )(x, g, w)


def _rglru_block(xc, wax, ba, bx, sp):
    ri = jnp.dot(xc.astype(BF16), wax, preferred_element_type=F32)
    r = _sigmoid(ri[:, :RNN_BLOCK] + ba)
    i = _sigmoid(ri[:, RNN_BLOCK:] + bx)
    a = jnp.exp((-LRU_C) * r * sp)
    one_m = 1.0 - a * a
    mult = jnp.where(one_m > 0.0, one_m * lax.rsqrt(one_m), 0.0)
    return a, mult * i * xc


def _prompt_mix_kernel(z_ref, k_ref, v_ref, convw_ref, convb_ref, wax_ref, ba_ref, bx_ref,
                       lam_ref, wpool_ref, pscale_ref,
                       o_ref, newh_ref, newconv_ref, newpool_ref,
                       conv_carry, pool_carry, h_carry, kb_ref, vb_ref, ac_scr, hl_scr, *, tm):
    l = pl.program_id(1)
    last = pl.num_programs(1) - 1
    nrow = tm // SUBLANES

    @pl.when(l == 0)
    def _():
        conv_carry[...] = jnp.zeros(conv_carry.shape, F32)
        pool_carry[...] = jnp.zeros(pool_carry.shape, F32)
        h_carry[...] = jnp.zeros(h_carry.shape, F32)
        kb_ref[...] = k_ref[0].astype(BF16)
        vb_ref[...] = v_ref[0].astype(BF16)

    chunk_id = lax.broadcasted_iota(jnp.int32, (SUBLANES, LANES), 0)
    first_chunk = chunk_id == 0

    def load_groups(col, width=LANES):
        return [z_ref[r * SUBLANES:(r + 1) * SUBLANES, col:col + width] for r in range(nrow)]

    def store_groups(col, rows, width=LANES):
        o_ref[:, col:col + width] = jnp.concatenate(rows, axis=0).astype(BF16)

    def history(tail_group, carry_ref, j, c0):
        tail = pltpu.roll(tail_group, 1, 0)
        prev = jnp.where(first_chunk, carry_ref[j - 1, :, c0:c0 + LANES], tail)
        carry_ref[j - 1, :, c0:c0 + LANES] = tail
        return prev

    sp = _softplus(-lam_ref[...])
    for n in range(N_RNN_BLOCKS):
        c0, c1 = n * RNN_BLOCK, (n + 1) * RNN_BLOCK
        xs = load_groups(c0)
        ext = [history(xs[nrow - j], conv_carry, j, c0) for j in range(CONV_W - 1, 0, -1)] + xs
        cw = [jnp.broadcast_to(convw_ref[k:k + 1, c0:c1], (SUBLANES, LANES)) for k in range(CONV_W)]
        cb = jnp.broadcast_to(convb_ref[:, c0:c1], (SUBLANES, LANES))
        xc = []
        for r in range(nrow):
            acc = cb + cw[0] * ext[r]
            for k in range(1, CONV_W):
                acc = acc + cw[k] * ext[r + k]
            xc.append(acc)
        a, b = _rglru_block(jnp.concatenate(xc, axis=0), wax_ref[n], ba_ref[:, c0:c1],
                            bx_ref[:, c0:c1], sp[:, c0:c1])
        ac_scr[:, c0:c1] = a
        hl_scr[:, c0:c1] = b

    acc_a = ac_scr[0:SUBLANES, :]
    acc_h = hl_scr[0:SUBLANES, :]
    for r in range(1, nrow):
        rows = slice(r * SUBLANES, (r + 1) * SUBLANES)
        ar = ac_scr[rows, :]
        acc_h = ar * acc_h + hl_scr[rows, :]
        acc_a = ar * acc_a
        ac_scr[rows, :] = acc_a
        hl_scr[rows, :] = acc_h
    h_in = h_carry[...]
    entering = []
    for c in range(SUBLANES):
        entering.append(h_in)
        h_in = acc_a[c:c + 1] * h_in + acc_h[c:c + 1]
    h_carry[...] = h_in
    h_enter = jnp.concatenate(entering, axis=0)
    for n in range(N_RNN_BLOCKS):
        c0, c1 = n * RNN_BLOCK, (n + 1) * RNN_BLOCK
        gr = load_groups(D_RNN + c0)
        store_groups(c0, [(hl_scr[r * SUBLANES:(r + 1) * SUBLANES, c0:c1]
                           + ac_scr[r * SUBLANES:(r + 1) * SUBLANES, c0:c1] * h_enter[:, c0:c1])
                          * _silu(gr[r]) for r in range(nrow)])

    pcol = 2 * D_RNN
    blocks = [(w, c0) for g, w in enumerate(POOL_WINDOWS)
              for c0 in range(g * POOL_GROUP, (g + 1) * POOL_GROUP, LANES)]

    def group(c0, r):
        return z_ref[r * SUBLANES:(r + 1) * SUBLANES, pcol + c0:pcol + c0 + LANES]

    def mean_minus_token(tot, w, c0, r):
        if r < w - 1:
            pos1 = l * tm + chunk_id * nrow + (r + 1)
            mean = tot / jnp.minimum(pos1, w).astype(F32)
        else:
            mean = tot * (1.0 / w)
        return mean - group(c0, r)

    hist, tot = {}, {}
    for w, c0 in blocks:
        hist[c0] = [history(group(c0, nrow - j), pool_carry, j, c0) for j in range(1, w)]
        t = group(c0, 0)
        for h in hist[c0]:
            t = t + h
        tot[c0] = t
        hl_scr[0:SUBLANES, c0:c0 + LANES] = mean_minus_token(t, w, c0, 0)
    for r in range(1, nrow):
        for w, c0 in blocks:
            leaving = group(c0, r - w) if r >= w else hist[c0][w - r - 1]
            tot[c0] = tot[c0] + (group(c0, r) - leaving)
            hl_scr[r * SUBLANES:(r + 1) * SUBLANES, c0:c0 + LANES] = mean_minus_token(
                tot[c0], w, c0, r)
    for g, w in enumerate(POOL_WINDOWS):
        c0, c1 = g * POOL_GROUP, (g + 1) * POOL_GROUP
        og = jnp.dot(hl_scr[:, c0:c1].astype(BF16), wpool_ref[g], preferred_element_type=F32)
        gp = z_ref[:, pcol + D_POOL + c0:pcol + D_POOL + c1]
        o_ref[:, D_RNN + c0:D_RNN + c1] = (og * pscale_ref[:, c0:c1] * _silu(gp)).astype(BF16)

    qoff = 2 * D_RNN + 2 * D_POOL
    for hd in range(N_XHEADS):
        c0, c1 = hd * XHEAD_DIM, (hd + 1) * XHEAD_DIM
        q = z_ref[:, qoff + c0:qoff + c1].astype(BF16)
        s = lax.dot_general(q, kb_ref[:, c0:c1], (((1,), (1,)), ((), ())),
                            preferred_element_type=F32) * (XHEAD_DIM ** -0.5)
        p = jnp.exp(s - jnp.max(s, axis=-1, keepdims=True))
        p = p / jnp.sum(p, axis=-1, keepdims=True)
        ox = jnp.dot(p.astype(BF16), vb_ref[:, c0:c1], preferred_element_type=F32)
        gx = z_ref[:, qoff + D_X + c0:qoff + D_X + c1]
        o_ref[:, D_RNN + D_POOL + c0:D_RNN + D_POOL + c1] = (ox * _silu(gx)).astype(BF16)

    @pl.when(l == last)
    def _():
        newh_ref[0] = h_carry[...]
        tail_row = lambda j: (nrow - j) * SUBLANES + SUBLANES - 1
        for j in range(1, CONV_W):
            newconv_ref[0, CONV_W - 1 - j:CONV_W - j, :] = z_ref[tail_row(j):tail_row(j) + 1, 0:D_RNN]
        for j in range(1, POOL_HIST + 1):
            newpool_ref[0, POOL_HIST - j:POOL_HIST - j + 1, :] = (
                z_ref[tail_row(j):tail_row(j) + 1, pcol:pcol + D_POOL])


def _prompt_mix(z, mem_k, mem_v, conv_w, conv_b, wax, b_a, b_x, lam, wpool, pscale,
                batch, seq, tm):
    nl = seq // tm
    zw = 2 * D_MIX
    const2 = lambda b, l: (0, 0)
    const3 = lambda b, l: (0, 0, 0)
    kern = functools.partial(_prompt_mix_kernel, tm=tm)
    return pl.pallas_call(
        kern,
        grid=(batch, nl),
        in_specs=[
            pl.BlockSpec((tm, zw), lambda b, l: (b * nl + l, 0)),
            pl.BlockSpec((1, N_MEM, D_X), lambda b, l: (b, 0, 0)),
            pl.BlockSpec((1, N_MEM, D_X), lambda b, l: (b, 0, 0)),
            pl.BlockSpec((CONV_W, D_RNN), const2),
            pl.BlockSpec((1, D_RNN), const2),
            pl.BlockSpec((N_RNN_BLOCKS, RNN_BLOCK, 2 * RNN_BLOCK), const3),
            pl.BlockSpec((1, D_RNN), const2),
            pl.BlockSpec((1, D_RNN), const2),
            pl.BlockSpec((1, D_RNN), const2),
            pl.BlockSpec((len(POOL_WINDOWS), POOL_GROUP, POOL_GROUP), const3),
            pl.BlockSpec((1, D_POOL), const2),
        ],
        out_specs=[
            pl.BlockSpec((tm, D_MIX), lambda b, l: (b * nl + l, 0)),
            pl.BlockSpec((1, 1, D_RNN), lambda b, l: (b, 0, 0)),
            pl.BlockSpec((1, CONV_W - 1, D_RNN), lambda b, l: (b, 0, 0)),
            pl.BlockSpec((1, POOL_HIST, D_POOL), lambda b, l: (b, 0, 0)),
        ],
        out_shape=[
            jax.ShapeDtypeStruct((batch * seq, D_MIX), BF16),
            jax.ShapeDtypeStruct((batch, 1, D_RNN), F32),
            jax.ShapeDtypeStruct((batch, CONV_W - 1, D_RNN), F32),
            jax.ShapeDtypeStruct((batch, POOL_HIST, D_POOL), F32),
        ],
        scratch_shapes=[
            pltpu.VMEM((CONV_W - 1, SUBLANES, D_RNN), F32),
            pltpu.VMEM((POOL_HIST, SUBLANES, D_POOL), F32),
            pltpu.VMEM((1, D_RNN), F32),
            pltpu.VMEM((N_MEM, D_X), BF16),
            pltpu.VMEM((N_MEM, D_X), BF16),
            pltpu.VMEM((tm, D_RNN), F32),
            pltpu.VMEM((tm, D_RNN), F32),
        ],
        compiler_params=pltpu.CompilerParams(
            dimension_semantics=("arbitrary", "arbitrary"),
            vmem_limit_bytes=VMEM_LIMIT),
        name="prompt_mix",
    )(z, mem_k, mem_v, conv_w, conv_b, wax, b_a, b_x, lam, wpool, pscale)


def _cache_rows(c):
    nb = c.shape[0]
    c = c.reshape(nb, N_MEM, N_XHEADS, XHEAD_DIM // LANES, LANES)
    return c.transpose(0, 1, 3, 2, 4).reshape(nb, N_MEM * SUBLANES, LANES)


def _sample_attn_block(q_ref, k_ref, v_ref, o_ref, bb):
    halves = XHEAD_DIM // LANES
    assert halves * N_XHEADS == SUBLANES
    r = lax.broadcasted_iota(jnp.int32, (SUBLANES, LANES), 0)
    c = lax.broadcasted_iota(jnp.int32, (SUBLANES, LANES), 1)
    diag = (c % SUBLANES) == r
    first_half = r < N_XHEADS
    nchunk = N_MEM * SUBLANES // LANES
    scores = []
    for j in range(bb):
        qn = jnp.concatenate(
            [q_ref[j:j + 1, (h * halves + t) * LANES:(h * halves + t + 1) * LANES]
             for t in range(halves) for h in range(N_XHEADS)], axis=0)
        scores.append(lax.dot_general(qn.astype(BF16), k_ref[j].astype(BF16),
                                      (((1,), (1,)), ((), ())), preferred_element_type=F32)
                      * (XHEAD_DIM ** -0.5))
    probs = []
    for j in range(bb):
        s = scores[j]
        chunks = []
        for ci in range(nchunk):
            sm = jnp.where(diag, s[:, ci * LANES:(ci + 1) * LANES], 0.0)
            other = pltpu.roll(sm, N_XHEADS, 0)
            other = jnp.where(first_half, pltpu.roll(other, LANES - N_XHEADS, 1),
                              pltpu.roll(other, N_XHEADS, 1))
            chunks.append(jnp.where(diag, sm + other, -jnp.inf))
        t_full = jnp.concatenate(chunks, axis=1)
        e = jnp.exp(t_full - jnp.max(t_full, axis=1, keepdims=True))
        probs.append((e / jnp.sum(e, axis=1, keepdims=True)).astype(BF16))
    for j in range(bb):
        o = jnp.dot(probs[j], v_ref[j].astype(BF16), preferred_element_type=F32)
        for t in range(halves):
            for h in range(N_XHEADS):
                col = (h * halves + t) * LANES
                o_ref[j:j + 1, col:col + LANES] = o[t * N_XHEADS + h:t * N_XHEADS + h + 1, :]


ATT_BB = 2


def _prompt_proj_kernel(x_ref, g_ref, w_ref, perm_ref, q_ref, k_ref, v_ref, o_ref, attn_ref, u_ref,
                        *, period, per):
    j = pl.program_id(1)
    step = pl.program_id(0) * pl.num_programs(1) + j

    @pl.when(j == 0)
    def _():
        x = x_ref[...]
        u = (x * _rms_scale(x) * g_ref[...]).astype(BF16)
        for r0 in range(0, u.shape[0], MIX_TM):
            u_ref[r0:r0 + MIX_TM, :] = jnp.dot(
                perm_ref[...], u[r0:r0 + MIX_TM], preferred_element_type=F32).astype(BF16)

    o_ref[...] = jnp.dot(u_ref[...], w_ref[...], preferred_element_type=F32)

    @pl.when(step % period < per)
    def _():
        _sample_attn_block(q_ref.at[0], k_ref, v_ref, attn_ref.at[0], ATT_BB)


def _prompt_proj(x, g, w, perm, q, cache_k, cache_v, tm, tn):
    m, k = x.shape
    n = w.shape[1]
    nblk = q.shape[0]
    nsteps = (m // tm) * (n // tn)
    common = math.gcd(nsteps, nblk)
    period, per = nsteps // common, nblk // common
    assert per <= period
    ncol = n // tn

    def blk(i, j):
        step = i * ncol + j
        return ((step // period) * per + jnp.minimum(step % period, per - 1), 0, 0)

    return pl.pallas_call(
        functools.partial(_prompt_proj_kernel, period=period, per=per),
        grid=(m // tm, ncol),
        in_specs=[
            pl.BlockSpec((tm, k), lambda i, j: (i, 0)),
            pl.BlockSpec((1, k), lambda i, j: (0, 0)),
            pl.BlockSpec((k, tn), lambda i, j: (0, j)),
            pl.BlockSpec(perm.shape, lambda i, j: (0, 0)),
            pl.BlockSpec((1, ATT_BB, D_X), blk),
            pl.BlockSpec((ATT_BB, N_MEM * SUBLANES, LANES), blk),
            pl.BlockSpec((ATT_BB, N_MEM * SUBLANES, LANES), blk),
        ],
        out_specs=[
            pl.BlockSpec((tm, tn), lambda i, j: (i, j)),
            pl.BlockSpec((1, ATT_BB, D_X), blk),
        ],
        out_shape=[
            jax.ShapeDtypeStruct((m, n), F32),
            jax.ShapeDtypeStruct(q.shape, F32),
        ],
        scratch_shapes=[pltpu.VMEM((tm, k), BF16)],
        compiler_params=pltpu.CompilerParams(
            dimension_semantics=("arbitrary", "arbitrary"),
            vmem_limit_bytes=VMEM_LIMIT),
        name="prompt_proj",
    )(x, g, w, perm, q, cache_k, cache_v)


def _sample_mix_kernel(z_ref, attn_ref, conv_ref, h_ref, pool_ref,
                       convw_ref, convb_ref, wax_ref, ba_ref, bx_ref, lam_ref, wpool_ref,
                       pscale_ref, o_ref, newh_ref, newconv_ref, newpool_ref):
    xr = z_ref[:, 0:D_RNN]
    xc = convb_ref[...] + convw_ref[CONV_W - 1:CONV_W, :] * xr
    for k in range(CONV_W - 1):
        xc = xc + convw_ref[k:k + 1, :] * conv_ref[k]
    for k in range(CONV_W - 2):
        newconv_ref[k] = conv_ref[k + 1]
    newconv_ref[CONV_W - 2] = xr

    sp = _softplus(-lam_ref[...])
    for n in range(N_RNN_BLOCKS):
        c0, c1 = n * RNN_BLOCK, (n + 1) * RNN_BLOCK
        a, b = _rglru_block(xc[:, c0:c1], wax_ref[n], ba_ref[:, c0:c1], bx_ref[:, c0:c1],
                            sp[:, c0:c1])
        h = a * h_ref[:, c0:c1] + b
        newh_ref[:, c0:c1] = h
        o_ref[:, c0:c1] = (h * _silu(z_ref[:, D_RNN + c0:D_RNN + c1])).astype(BF16)

    xp = z_ref[:, 2 * D_RNN:2 * D_RNN + D_POOL]
    for k in range(POOL_HIST - 1):
        newpool_ref[k] = pool_ref[k + 1]
    newpool_ref[POOL_HIST - 1] = xp
    for g, w in enumerate(POOL_WINDOWS):
        c0, c1 = g * POOL_GROUP, (g + 1) * POOL_GROUP
        xg = xp[:, c0:c1]
        tot = xg
        for j in range(1, w):
            tot = tot + pool_ref[POOL_HIST - j, :, c0:c1]
        cnt = float(min(PAST_LEN + 1, w))
        d = tot / cnt - xg
        og = jnp.dot(d.astype(BF16), wpool_ref[g], preferred_element_type=F32)
        gp = z_ref[:, 2 * D_RNN + D_POOL + c0:2 * D_RNN + D_POOL + c1]
        o_ref[:, D_RNN + c0:D_RNN + c1] = (og * pscale_ref[:, c0:c1] * _silu(gp)).astype(BF16)

    gx = z_ref[:, 2 * D_RNN + 2 * D_POOL + D_X:2 * D_MIX]
    o_ref[:, D_RNN + D_POOL:] = (attn_ref[...] * _silu(gx)).astype(BF16)


def _sample_mix(z, attn, conv, h, pool, conv_w, conv_b, wax, b_a, b_x, lam, wpool, pscale, tb):
    nb = z.shape[0]
    zw = 2 * D_MIX
    rows = lambda i: (i, 0)
    const2 = lambda i: (0, 0)
    const3 = lambda i: (0, 0, 0)
    hist = lambda i: (0, i, 0)
    return pl.pallas_call(
        _sample_mix_kernel,
        grid=(nb // tb,),
        in_specs=[
            pl.BlockSpec((tb, zw), rows),
            pl.BlockSpec((tb, D_X), rows),
            pl.BlockSpec((CONV_W - 1, tb, D_RNN), hist),
            pl.BlockSpec((tb, D_RNN), rows),
            pl.BlockSpec((POOL_HIST, tb, D_POOL), hist),
            pl.BlockSpec((CONV_W, D_RNN), const2),
            pl.BlockSpec((1, D_RNN), const2),
            pl.BlockSpec((N_RNN_BLOCKS, RNN_BLOCK, 2 * RNN_BLOCK), const3),
            pl.BlockSpec((1, D_RNN), const2),
            pl.BlockSpec((1, D_RNN), const2),
            pl.BlockSpec((1, D_RNN), const2),
            pl.BlockSpec((len(POOL_WINDOWS), POOL_GROUP, POOL_GROUP), const3),
            pl.BlockSpec((1, D_POOL), const2),
        ],
        out_specs=[
            pl.BlockSpec((tb, D_MIX), rows),
            pl.BlockSpec((tb, D_RNN), rows),
            pl.BlockSpec((CONV_W - 1, tb, D_RNN), hist),
            pl.BlockSpec((POOL_HIST, tb, D_POOL), hist),
        ],
        out_shape=[
            jax.ShapeDtypeStruct((nb, D_MIX), BF16),
            jax.ShapeDtypeStruct((nb, D_RNN), F32),
            jax.ShapeDtypeStruct((CONV_W - 1, nb, D_RNN), F32),
            jax.ShapeDtypeStruct((POOL_HIST, nb, D_POOL), F32),
        ],
        compiler_params=pltpu.CompilerParams(
            dimension_semantics=("arbitrary",),
            vmem_limit_bytes=VMEM_LIMIT),
        name="sample_mix",
    )(z, attn, conv, h, pool, conv_w, conv_b, wax, b_a, b_x, lam, wpool, pscale)


def _branch_out_kernel(o_ref, gates_ref, x_ref, wb_ref, wo_ref, gpost_ref, *rest, interleaved):
    y_ref = rest[-1]
    merged = None
    for j, (r0, r1) in enumerate(((0, D_RNN), (D_RNN, D_RNN + D_POOL), (D_RNN + D_POOL, D_MIX))):
        yj = jnp.dot(o_ref[:, r0:r1], wb_ref[r0:r1, :], preferred_element_type=F32)
        term = _sigmoid(gates_ref[:, j * D_MODEL:(j + 1) * D_MODEL]) * yj
        merged = term if merged is None else merged + term
    merged = merged.astype(BF16)
    if interleaved:
        merged = jnp.dot(rest[0][...], merged, preferred_element_type=F32).astype(BF16)
    out = jnp.dot(merged, wo_ref[...], preferred_element_type=F32)
    y_ref[...] = x_ref[...] + out * _rms_scale(out) * gpost_ref[...]


def _branch_out(o, z, x, wb, wo, g_post, tm, unperm=None):
    m = x.shape[0]
    gw = N_BRANCH * D_MODEL
    gblk = (2 * D_MIX) // gw
    resident = pl.Buffered(1)
    in_specs = [
        pl.BlockSpec((tm, D_MIX), lambda i: (i, 0)),
        pl.BlockSpec((tm, gw), lambda i: (i, gblk)),
        pl.BlockSpec((tm, D_MODEL), lambda i: (i, 0)),
        pl.BlockSpec((D_MIX, D_MODEL), lambda i: (0, 0), pipeline_mode=resident),
        pl.BlockSpec((D_MODEL, D_MODEL), lambda i: (0, 0), pipeline_mode=resident),
        pl.BlockSpec((1, D_MODEL), lambda i: (0, 0)),
    ]
    args = [o, z, x, wb, wo, g_post]
    if unperm is not None:
        assert unperm.shape == (tm, tm)
        in_specs.append(pl.BlockSpec(unperm.shape, lambda i: (0, 0)))
        args.append(unperm)
    return pl.pallas_call(
        functools.partial(_branch_out_kernel, interleaved=unperm is not None),
        grid=(m // tm,),
        in_specs=in_specs,
        out_specs=pl.BlockSpec((tm, D_MODEL), lambda i: (i, 0)),
        out_shape=jax.ShapeDtypeStruct((m, D_MODEL), F32),
        compiler_params=pltpu.CompilerParams(
            dimension_semantics=("arbitrary",),
            vmem_limit_bytes=VMEM_LIMIT),
        name="branch_out",
    )(*args)


WROWS = 512
PER_BRANCH = D_RNN // WROWS
assert D_RNN == D_POOL == D_X and D_RNN % WROWS == 0 and D_MODEL % WROWS == 0
N_WB_BLOCKS = N_BRANCH * PER_BRANCH
N_WOUT_BLOCKS = D_MODEL // WROWS


def _branch_out_cast_kernel(o_ref, gates_ref, x_ref, wb_ref, wo_ref, gpost_ref,
                            y_ref, wbb_ref, wob_ref, merged_ref, out_ref):
    s = pl.program_id(0)

    @pl.when(s < N_WB_BLOCKS)
    def _():
        w = wb_ref[...].astype(BF16)
        wbb_ref[...] = w
        term = _sigmoid(gates_ref[...]) * jnp.dot(o_ref[...], w, preferred_element_type=F32)

        @pl.when(s == 0)
        def _():
            merged_ref[...] = term

        @pl.when(s > 0)
        def _():
            merged_ref[...] += term

    for kb in range(N_WOUT_BLOCKS):
        @pl.when(s == N_WB_BLOCKS + kb)
        def _(kb=kb):
            w = wo_ref[...].astype(BF16)
            wob_ref[...] = w
            part = jnp.dot(merged_ref[:, kb * WROWS:(kb + 1) * WROWS].astype(BF16), w,
                           preferred_element_type=F32)
            if kb == 0:
                out_ref[...] = part
            else:
                out_ref[...] += part

    @pl.when(s == N_WB_BLOCKS + N_WOUT_BLOCKS - 1)
    def _():
        out = out_ref[...]
        y_ref[...] = x_ref[...] + out * _rms_scale(out) * gpost_ref[...]


def _branch_out_cast(o, z, x, wb, wo, g_post):
    m = x.shape[0]
    gblk0 = (2 * D_MIX) // D_MODEL
    wb_blk = lambda s: jnp.minimum(s, N_WB_BLOCKS - 1)
    wo_blk = lambda s: jnp.maximum(s - N_WB_BLOCKS, 0)
    return pl.pallas_call(
        _branch_out_cast_kernel,
        grid=(N_WB_BLOCKS + N_WOUT_BLOCKS,),
        in_specs=[
            pl.BlockSpec((m, WROWS), lambda s: (0, wb_blk(s))),
            pl.BlockSpec((m, D_MODEL), lambda s: (0, gblk0 + wb_blk(s) // PER_BRANCH)),
            pl.BlockSpec((m, D_MODEL), lambda s: (0, 0)),
            pl.BlockSpec((WROWS, D_MODEL), lambda s: (wb_blk(s), 0)),
            pl.BlockSpec((WROWS, D_MODEL), lambda s: (wo_blk(s), 0)),
            pl.BlockSpec((1, D_MODEL), lambda s: (0, 0)),
        ],
        out_specs=[
            pl.BlockSpec((m, D_MODEL), lambda s: (0, 0)),
            pl.BlockSpec((WROWS, D_MODEL), lambda s: (wb_blk(s), 0)),
            pl.BlockSpec((WROWS, D_MODEL), lambda s: (wo_blk(s), 0)),
        ],
        out_shape=[
            jax.ShapeDtypeStruct((m, D_MODEL), F32),
            jax.ShapeDtypeStruct(wb.shape, BF16),
            jax.ShapeDtypeStruct(wo.shape, BF16),
        ],
        scratch_shapes=[pltpu.VMEM((m, D_MODEL), F32), pltpu.VMEM((m, D_MODEL), F32)],
        compiler_params=pltpu.CompilerParams(
            dimension_semantics=("arbitrary",),
            vmem_limit_bytes=VMEM_LIMIT),
        name="branch_out_cast",
    )(o, z, x, wb, wo, g_post)


def _mem_kv_kernel(x_ref, g_ref, w_ref, k_ref, v_ref, u_ref):
    j = pl.program_id(1)

    @pl.when(j == 0)
    def _():
        x = x_ref[...]
        u_ref[...] = (x * _rms_scale(x) * g_ref[...]).astype(BF16)

    res = jnp.dot(u_ref[...], w_ref[...].astype(BF16), preferred_element_type=F32)

    @pl.when(j == 0)
    def _():
        k_ref[...] = res

    @pl.when(j == 1)
    def _():
        v_ref[...] = res


def _mem_kv(x, g, w, tm):
    m, k = x.shape
    assert w.shape[1] == 2 * D_X
    half = pl.BlockSpec((tm, D_X), lambda i, j: (i, 0))
    return pl.pallas_call(
        _mem_kv_kernel,
        grid=(m // tm, 2),
        in_specs=[
            pl.BlockSpec((tm, k), lambda i, j: (i, 0)),
            pl.BlockSpec((1, k), lambda i, j: (0, 0)),
            pl.BlockSpec((k, D_X), lambda i, j: (0, j)),
        ],
        out_specs=[half, half],
        out_shape=[jax.ShapeDtypeStruct((m, D_X), F32)] * 2,
        scratch_shapes=[pltpu.VMEM((tm, k), BF16)],
        compiler_params=pltpu.CompilerParams(
            dimension_semantics=("arbitrary", "arbitrary"),
            vmem_limit_bytes=VMEM_LIMIT),
        name="mem_kv",
    )(x, g, w)


def kernel(x_prompt, x_sample, mem_prompt, state_rglru_h, state_conv, state_pool, cache_mem_k, cache_mem_v, g_pre, w_in, conv_w, conv_b, w_rg_a, b_rg_a, w_rg_x, b_rg_x, lru_lambda, w_pool, pool_scale, g_mem, w_kv, w_branch, w_out, g_post):
    batch, seq, _ = x_prompt.shape
    nb = x_sample.shape[0]
    depth = g_pre.shape[0]
    assert depth == 1 and x_sample.shape[1] == 1

    l = 0
    row = lambda v: v.reshape(1, -1)
    wax = jnp.concatenate([w_rg_a[l], w_rg_x[l]], axis=-1).astype(BF16)
    wpool = w_pool[l].astype(BF16)
    mix_params = (conv_w[l], row(conv_b[l]), wax, row(b_rg_a[l]), row(b_rg_x[l]),
                  row(lru_lambda[l]), wpool, row(pool_scale[l]))

    xp2 = x_prompt.reshape(batch * seq, D_MODEL)
    xs2 = x_sample.reshape(nb, D_MODEL)
    mem2 = mem_prompt.reshape(batch * N_MEM, D_MODEL)

    z_s, w_in_b = _sample_proj(xs2, row(g_pre[l]), w_in[l], tn=1024)
    qoff = 2 * D_RNN + 2 * D_POOL
    q_s = z_s[:, qoff:qoff + D_X].reshape(nb // ATT_BB, ATT_BB, D_X)

    perm = _chunk_interleave()
    z_p, attn_s = _prompt_proj(xp2, row(g_pre[l]), w_in_b, perm, q_s,
                               _cache_rows(cache_mem_k[l]), _cache_rows(cache_mem_v[l]),
                               tm=1024, tn=1024)
    attn_s = attn_s.reshape(nb, D_X)

    o_s, h_s, c_s, p_s = _sample_mix(
        z_s, attn_s, state_conv[l].transpose(1, 0, 2), state_rglru_h[l],
        state_pool[l].transpose(1, 0, 2), *mix_params, tb=32)
    y_s, w_br_b, w_out_b = _branch_out_cast(o_s, z_s, xs2, w_branch[l], w_out[l], row(g_post[l]))

    mem_k, mem_v = _mem_kv(mem2, row(g_mem[l]), w_kv[l], tm=512)
    mem_k = mem_k.reshape(batch, N_MEM, D_X)
    mem_v = mem_v.reshape(batch, N_MEM, D_X)

    o_p, h_p, c_p, p_p = _prompt_mix(z_p, mem_k, mem_v, *mix_params,
                                     batch=batch, seq=seq, tm=MIX_TM)
    y_p = _branch_out(o_p, z_p, xp2, w_br_b, w_out_b, row(g_post[l]), tm=MIX_TM, unperm=perm.T)

    return (
        y_p.reshape(batch, seq, D_MODEL),
        y_s.reshape(nb, 1, D_MODEL),
        h_p.reshape(1, batch, D_RNN),
        c_p.reshape(1, batch, CONV_W - 1, D_RNN),
        p_p.reshape(1, batch, POOL_HIST, D_POOL),
        mem_k.reshape(1, batch, N_MEM, N_XHEADS, XHEAD_DIM),
        mem_v.reshape(1, batch, N_MEM, N_XHEADS, XHEAD_DIM),
        h_s.reshape(1, nb, D_RNN),
        c_s.transpose(1, 0, 2)[None],
        p_s.transpose(1, 0, 2)[None],
    )
```

```python
import functools
import math

import jax
import jax.numpy as jnp
from jax import lax
from jax.experimental import pallas as pl
from jax.experimental.pallas import tpu as pltpu

D_MODEL = 2048
PAST_LEN = 16384
D_RNN = 1024
N_RNN_BLOCKS = 8
RNN_BLOCK = D_RNN // N_RNN_BLOCKS
CONV_W = 4
LRU_C = 8.0
D_POOL = 1024
POOL_WINDOWS = (2, 4, 8, 16)
POOL_GROUP = D_POOL // len(POOL_WINDOWS)
POOL_HIST = max(POOL_WINDOWS) - 1
N_MEM = 256
N_XHEADS = 4
XHEAD_DIM = 256
D_X = N_XHEADS * XHEAD_DIM
N_BRANCH = 3
D_MIX = D_RNN + D_POOL + D_X
D_IN = 2 * D_MIX + N_BRANCH * D_MODEL
EPS = 1e-6

SUBLANES = 8
LANES = 128
VMEM_LIMIT = 56 * 1024 * 1024
MIX_TM = 256

BF16 = jnp.bfloat16
F32 = jnp.float32

NEG_LOG2_E = -1.4426950408889634


def _sigmoid(x):
    return 1.0 / (1.0 + jnp.exp2(x * NEG_LOG2_E))


def _silu(x):
    return x * _sigmoid(x)


def _softplus(x):
    return jnp.maximum(x, 0.0) + jnp.log1p(jnp.exp(-jnp.abs(x)))


def _rms_scale(x):
    return lax.rsqrt(jnp.mean(x * x, axis=-1, keepdims=True) + EPS)


def _chunk_interleave():
    nrow = MIX_TM // SUBLANES
    p = jnp.arange(MIX_TM)
    token = (p % SUBLANES) * nrow + p // SUBLANES
    return (token[:, None] == jnp.arange(MIX_TM)[None, :]).astype(BF16)


def _sample_proj_kernel(x_ref, g_ref, w_ref, o_ref, wb_ref, u_ref):
    @pl.when(pl.program_id(0) == 0)
    def _():
        x = x_ref[...]
        u_ref[...] = (x * _rms_scale(x) * g_ref[...]).astype(BF16)

    w = w_ref[...].astype(BF16)
    wb_ref[...] = w
    o_ref[...] = jnp.dot(u_ref[...], w, preferred_element_type=F32)


def _sample_proj(x, g, w, tn):
    m, k = x.shape
    n = w.shape[1]
    return pl.pallas_call(
        _sample_proj_kernel,
        grid=(n // tn,),
        in_specs=[
            pl.BlockSpec((m, k), lambda j: (0, 0)),
            pl.BlockSpec((1, k), lambda j: (0, 0)),
            pl.BlockSpec((k, tn), lambda j: (0, j)),
        ],
        out_specs=[
            pl.BlockSpec((m, tn), lambda j: (0, j)),
            pl.BlockSpec((k, tn), lambda j: (0, j)),
        ],
        out_shape=[
            jax.ShapeDtypeStruct((m, n), F32),
            jax.ShapeDtypeStruct((k, n), BF16),
        ],
        scratch_shapes=[pltpu.VMEM((m, k), BF16)],
        compiler_params=pltpu.CompilerParams(
            dimension_semantics=("arbitrary",),
            vmem_limit_bytes=VMEM_LIMIT),
        name="sample_proj",
    )(x, g, w)


def _rglru_block(xc, wax, ba, bx, sp):
    ri = jnp.dot(xc.astype(BF16), wax, preferred_element_type=F32)
    r = _sigmoid(ri[:, :RNN_BLOCK] + ba)
    i = _sigmoid(ri[:, RNN_BLOCK:] + bx)
    a = jnp.exp((-LRU_C) * r * sp)
    one_m = 1.0 - a * a
    mult = jnp.where(one_m > 0.0, one_m * lax.rsqrt(one_m), 0.0)
    return a, mult * i * xc


def _prompt_mix_kernel(z_ref, k_ref, v_ref, convw_ref, convb_ref, wax_ref, ba_ref, bx_ref,
                       lam_ref, wpool_ref, pscale_ref,
                       o_ref, newh_ref, newconv_ref, newpool_ref,
                       conv_carry, pool_carry, h_carry, kb_ref, vb_ref, ac_scr, hl_scr, *, tm):
    l = pl.program_id(1)
    last = pl.num_programs(1) - 1
    nrow = tm // SUBLANES

    @pl.when(l == 0)
    def _():
        conv_carry[...] = jnp.zeros(conv_carry.shape, F32)
        pool_carry[...] = jnp.zeros(pool_carry.shape, F32)
        h_carry[...] = jnp.zeros(h_carry.shape, F32)
        kb_ref[...] = k_ref[0].astype(BF16)
        vb_ref[...] = v_ref[0].astype(BF16)

    chunk_id = lax.broadcasted_iota(jnp.int32, (SUBLANES, LANES), 0)
    first_chunk = chunk_id == 0

    def load_groups(col, width=LANES):
        return [z_ref[r * SUBLANES:(r + 1) * SUBLANES, col:col + width] for r in range(nrow)]

    def store_groups(col, rows, width=LANES):
        o_ref[:, col:col + width] = jnp.concatenate(rows, axis=0).astype(BF16)

    def history(tail_group, carry_ref, j, c0):
        tail = pltpu.roll(tail_group, 1, 0)
        prev = jnp.where(first_chunk, carry_ref[j - 1, :, c0:c0 + LANES], tail)
        carry_ref[j - 1, :, c0:c0 + LANES] = tail
        return prev

    sp = _softplus(-lam_ref[...])
    for n in range(N_RNN_BLOCKS):
        c0, c1 = n * RNN_BLOCK, (n + 1) * RNN_BLOCK
        xs = load_groups(c0)
        ext = [history(xs[nrow - j], conv_carry, j, c0) for j in range(CONV_W - 1, 0, -1)] + xs
        cw = [jnp.broadcast_to(convw_ref[k:k + 1, c0:c1], (SUBLANES, LANES)) for k in range(CONV_W)]
        cb = jnp.broadcast_to(convb_ref[:, c0:c1], (SUBLANES, LANES))
        xc = []
        for r in range(nrow):
            acc = cb + cw[0] * ext[r]
            for k in range(1, CONV_W):
                acc = acc + cw[k] * ext[r + k]
            xc.append(acc)
        a, b = _rglru_block(jnp.concatenate(xc, axis=0), wax_ref[n], ba_ref[:, c0:c1],
                            bx_ref[:, c0:c1], sp[:, c0:c1])
        ac_scr[:, c0:c1] = a
        hl_scr[:, c0:c1] = b

    acc_a = ac_scr[0:SUBLANES, :]
    acc_h = hl_scr[0:SUBLANES, :]
    for r in range(1, nrow):
        rows = slice(r * SUBLANES, (r + 1) * SUBLANES)
        ar = ac_scr[rows, :]
        acc_h = ar * acc_h + hl_scr[rows, :]
        acc_a = ar * acc_a
        ac_scr[rows, :] = acc_a
        hl_scr[rows, :] = acc_h
    h_in = h_carry[...]
    entering = []
    for c in range(SUBLANES):
        entering.append(h_in)
        h_in = acc_a[c:c + 1] * h_in + acc_h[c:c + 1]
    h_carry[...] = h_in
    h_enter = jnp.concatenate(entering, axis=0)
    for n in range(N_RNN_BLOCKS):
        c0, c1 = n * RNN_BLOCK, (n + 1) * RNN_BLOCK
        gr = load_groups(D_RNN + c0)
        store_groups(c0, [(hl_scr[r * SUBLANES:(r + 1) * SUBLANES, c0:c1]
                           + ac_scr[r * SUBLANES:(r + 1) * SUBLANES, c0:c1] * h_enter[:, c0:c1])
                          * _silu(gr[r]) for r in range(nrow)])

    pcol = 2 * D_RNN
    blocks = [(w, c0) for g, w in enumerate(POOL_WINDOWS)
              for c0 in range(g * POOL_GROUP, (g + 1) * POOL_GROUP, LANES)]

    def group(c0, r):
        return z_ref[r * SUBLANES:(r + 1) * SUBLANES, pcol + c0:pcol + c0 + LANES]

    def mean_minus_token(tot, w, c0, r):
        if r < w - 1:
            pos1 = l * tm + chunk_id * nrow + (r + 1)
            mean = tot / jnp.minimum(pos1, w).astype(F32)
        else:
            mean = tot * (1.0 / w)
        return mean - group(c0, r)

    hist, tot = {}, {}
    for w, c0 in blocks:
        hist[c0] = [history(group(c0, nrow - j), pool_carry, j, c0) for j in range(1, w)]
        t = group(c0, 0)
        for h in hist[c0]:
            t = t + h
        tot[c0] = t
        hl_scr[0:SUBLANES, c0:c0 + LANES] = mean_minus_token(t, w, c0, 0)
    for r in range(1, nrow):
        for w, c0 in blocks:
            leaving = group(c0, r - w) if r >= w else hist[c0][w - r - 1]
            tot[c0] = tot[c0] + (group(c0, r) - leaving)
            hl_scr[r * SUBLANES:(r + 1) * SUBLANES, c0:c0 + LANES] = mean_minus_token(
                tot[c0], w, c0, r)
    for g, w in enumerate(POOL_WINDOWS):
        c0, c1 = g * POOL_GROUP, (g + 1) * POOL_GROUP
        og = jnp.dot(hl_scr[:, c0:c1].astype(BF16), wpool_ref[g], preferred_element_type=F32)
        gp = z_ref[:, pcol + D_POOL + c0:pcol + D_POOL + c1]
        o_ref[:, D_RNN + c0:D_RNN + c1] = (og * pscale_ref[:, c0:c1] * _silu(gp)).astype(BF16)

    qoff = 2 * D_RNN + 2 * D_POOL
    for hd in range(N_XHEADS):
        c0, c1 = hd * XHEAD_DIM, (hd + 1) * XHEAD_DIM
        q = z_ref[:, qoff + c0:qoff + c1].astype(BF16)
        s = lax.dot_general(q, kb_ref[:, c0:c1], (((1,), (1,)), ((), ())),
                            preferred_element_type=F32) * (XHEAD_DIM ** -0.5)
        p = jnp.exp(s - jnp.max(s, axis=-1, keepdims=True))
        p = p / jnp.sum(p, axis=-1, keepdims=True)
        ox = jnp.dot(p.astype(BF16), vb_ref[:, c0:c1], preferred_element_type=F32)
        gx = z_ref[:, qoff + D_X + c0:qoff + D_X + c1]
        o_ref[:, D_RNN + D_POOL + c0:D_RNN + D_POOL + c1] = (ox * _silu(gx)).astype(BF16)

    @pl.when(l == last)
    def _():
        newh_ref[0] = h_carry[...]
        tail_row = lambda j: (nrow - j) * SUBLANES + SUBLANES - 1
        for j in range(1, CONV_W):
            newconv_ref[0, CONV_W - 1 - j:CONV_W - j, :] = z_ref[tail_row(j):tail_row(j) + 1, 0:D_RNN]
        for j in range(1, POOL_HIST + 1):
            newpool_ref[0, POOL_HIST - j:POOL_HIST - j + 1, :] = (
                z_ref[tail_row(j):tail_row(j) + 1, pcol:pcol + D_POOL])


def _prompt_mix(z, mem_k, mem_v, conv_w, conv_b, wax, b_a, b_x, lam, wpool, pscale,
                batch, seq, tm):
    nl = seq // tm
    zw = 2 * D_MIX
    const2 = lambda b, l: (0, 0)
    const3 = lambda b, l: (0, 0, 0)
    kern = functools.partial(_prompt_mix_kernel, tm=tm)
    return pl.pallas_call(
        kern,
        grid=(batch, nl),
        in_specs=[
            pl.BlockSpec((tm, zw), lambda b, l: (b * nl + l, 0)),
            pl.BlockSpec((1, N_MEM, D_X), lambda b, l: (b, 0, 0)),
            pl.BlockSpec((1, N_MEM, D_X), lambda b, l: (b, 0, 0)),
            pl.BlockSpec((CONV_W, D_RNN), const2),
            pl.BlockSpec((1, D_RNN), const2),
            pl.BlockSpec((N_RNN_BLOCKS, RNN_BLOCK, 2 * RNN_BLOCK), const3),
            pl.BlockSpec((1, D_RNN), const2),
            pl.BlockSpec((1, D_RNN), const2),
            pl.BlockSpec((1, D_RNN), const2),
            pl.BlockSpec((len(POOL_WINDOWS), POOL_GROUP, POOL_GROUP), const3),
            pl.BlockSpec((1, D_POOL), const2),
        ],
        out_specs=[
            pl.BlockSpec((tm, D_MIX), lambda b, l: (b * nl + l, 0)),
            pl.BlockSpec((1, 1, D_RNN), lambda b, l: (b, 0, 0)),
            pl.BlockSpec((1, CONV_W - 1, D_RNN), lambda b, l: (b, 0, 0)),
            pl.BlockSpec((1, POOL_HIST, D_POOL), lambda b, l: (b, 0, 0)),
        ],
        out_shape=[
            jax.ShapeDtypeStruct((batch * seq, D_MIX), BF16),
            jax.ShapeDtypeStruct((batch, 1, D_RNN), F32),
            jax.ShapeDtypeStruct((batch, CONV_W - 1, D_RNN), F32),
            jax.ShapeDtypeStruct((batch, POOL_HIST, D_POOL), F32),
        ],
        scratch_shapes=[
            pltpu.VMEM((CONV_W - 1, SUBLANES, D_RNN), F32),
            pltpu.VMEM((POOL_HIST, SUBLANES, D_POOL), F32),
            pltpu.VMEM((1, D_RNN), F32),
            pltpu.VMEM((N_MEM, D_X), BF16),
            pltpu.VMEM((N_MEM, D_X), BF16),
            pltpu.VMEM((tm, D_RNN), F32),
            pltpu.VMEM((tm, D_RNN), F32),
        ],
        compiler_params=pltpu.CompilerParams(
            dimension_semantics=("arbitrary", "arbitrary"),
            vmem_limit_bytes=VMEM_LIMIT),
        name="prompt_mix",
    )(z, mem_k, mem_v, conv_w, conv_b, wax, b_a, b_x, lam, wpool, pscale)


def _cache_rows(c):
    nb = c.shape[0]
    c = c.reshape(nb, N_MEM, N_XHEADS, XHEAD_DIM // LANES, LANES)
    return c.transpose(0, 1, 3, 2, 4).reshape(nb, N_MEM * SUBLANES, LANES)


def _sample_attn_block(q_ref, k_ref, v_ref, o_ref, bb):
    halves = XHEAD_DIM // LANES
    assert halves * N_XHEADS == SUBLANES
    r = lax.broadcasted_iota(jnp.int32, (SUBLANES, LANES), 0)
    c = lax.broadcasted_iota(jnp.int32, (SUBLANES, LANES), 1)
    diag = (c % SUBLANES) == r
    first_half = r < N_XHEADS
    nchunk = N_MEM * SUBLANES // LANES
    scores = []
    for j in range(bb):
        qn = jnp.concatenate(
            [q_ref[j:j + 1, (h * halves + t) * LANES:(h * halves + t + 1) * LANES]
             for t in range(halves) for h in range(N_XHEADS)], axis=0)
        scores.append(lax.dot_general(qn.astype(BF16), k_ref[j].astype(BF16),
                                      (((1,), (1,)), ((), ())), preferred_element_type=F32)
                      * (XHEAD_DIM ** -0.5))
    probs = []
    for j in range(bb):
        s = scores[j]
        chunks = []
        for ci in range(nchunk):
            sm = jnp.where(diag, s[:, ci * LANES:(ci + 1) * LANES], 0.0)
            other = pltpu.roll(sm, N_XHEADS, 0)
            other = jnp.where(first_half, pltpu.roll(other, LANES - N_XHEADS, 1),
                              pltpu.roll(other, N_XHEADS, 1))
            chunks.append(jnp.where(diag, sm + other, -jnp.inf))
        t_full = jnp.concatenate(chunks, axis=1)
        e = jnp.exp(t_full - jnp.max(t_full, axis=1, keepdims=True))
        probs.append((e / jnp.sum(e, axis=1, keepdims=True)).astype(BF16))
    for j in range(bb):
        o = jnp.dot(probs[j], v_ref[j].astype(BF16), preferred_element_type=F32)
        for t in range(halves):
            for h in range(N_XHEADS):
                col = (h * halves + t) * LANES
                o_ref[j:j + 1, col:col + LANES] = o[t * N_XHEADS + h:t * N_XHEADS + h + 1, :]


ATT_BB = 4


def _prompt_proj_kernel(x_ref, g_ref, w_ref, perm_ref, q_ref, k_ref, v_ref, o_ref, attn_ref, u_ref,
                        *, period, per):
    j = pl.program_id(1)
    step = pl.program_id(0) * pl.num_programs(1) + j

    @pl.when(j == 0)
    def _():
        x = x_ref[...]
        u = (x * _rms_scale(x) * g_ref[...]).astype(BF16)
        for r0 in range(0, u.shape[0], MIX_TM):
            u_ref[r0:r0 + MIX_TM, :] = jnp.dot(
                perm_ref[...], u[r0:r0 + MIX_TM], preferred_element_type=F32).astype(BF16)

    o_ref[...] = jnp.dot(u_ref[...], w_ref[...], preferred_element_type=F32)

    @pl.when(step % period < per)
    def _():
        _sample_attn_block(q_ref.at[0], k_ref, v_ref, attn_ref.at[0], ATT_BB)


def _prompt_proj(x, g, w, perm, q, cache_k, cache_v, tm, tn):
    m, k = x.shape
    n = w.shape[1]
    nblk = q.shape[0]
    nsteps = (m // tm) * (n // tn)
    common = math.gcd(nsteps, nblk)
    period, per = nsteps // common, nblk // common
    assert per <= period
    ncol = n // tn

    def blk(i, j):
        step = i * ncol + j
        return ((step // period) * per + jnp.minimum(step % period, per - 1), 0, 0)

    return pl.pallas_call(
        functools.partial(_prompt_proj_kernel, period=period, per=per),
        grid=(m // tm, ncol),
        in_specs=[
            pl.BlockSpec((tm, k), lambda i, j: (i, 0)),
            pl.BlockSpec((1, k), lambda i, j: (0, 0)),
            pl.BlockSpec((k, tn), lambda i, j: (0, j)),
            pl.BlockSpec(perm.shape, lambda i, j: (0, 0)),
            pl.BlockSpec((1, ATT_BB, D_X), blk),
            pl.BlockSpec((ATT_BB, N_MEM * SUBLANES, LANES), blk),
            pl.BlockSpec((ATT_BB, N_MEM * SUBLANES, LANES), blk),
        ],
        out_specs=[
            pl.BlockSpec((tm, tn), lambda i, j: (i, j)),
            pl.BlockSpec((1, ATT_BB, D_X), blk),
        ],
        out_shape=[
            jax.ShapeDtypeStruct((m, n), F32),
            jax.ShapeDtypeStruct(q.shape, F32),
        ],
        scratch_shapes=[pltpu.VMEM((tm, k), BF16)],
        compiler_params=pltpu.CompilerParams(
            dimension_semantics=("arbitrary", "arbitrary"),
            vmem_limit_bytes=VMEM_LIMIT),
        name="prompt_proj",
    )(x, g, w, perm, q, cache_k, cache_v)


def _sample_mix_kernel(z_ref, attn_ref, conv_ref, h_ref, pool_ref,
                       convw_ref, convb_ref, wax_ref, ba_ref, bx_ref, lam_ref, wpool_ref,
                       pscale_ref, o_ref, newh_ref, newconv_ref, newpool_ref):
    xr = z_ref[:, 0:D_RNN]
    xc = convb_ref[...] + convw_ref[CONV_W - 1:CONV_W, :] * xr
    for k in range(CONV_W - 1):
        xc = xc + convw_ref[k:k + 1, :] * conv_ref[k]
    for k in range(CONV_W - 2):
        newconv_ref[k] = conv_ref[k + 1]
    newconv_ref[CONV_W - 2] = xr

    sp = _softplus(-lam_ref[...])
    for n in range(N_RNN_BLOCKS):
        c0, c1 = n * RNN_BLOCK, (n + 1) * RNN_BLOCK
        a, b = _rglru_block(xc[:, c0:c1], wax_ref[n], ba_ref[:, c0:c1], bx_ref[:, c0:c1],
                            sp[:, c0:c1])
        h = a * h_ref[:, c0:c1] + b
        newh_ref[:, c0:c1] = h
        o_ref[:, c0:c1] = (h * _silu(z_ref[:, D_RNN + c0:D_RNN + c1])).astype(BF16)

    xp = z_ref[:, 2 * D_RNN:2 * D_RNN + D_POOL]
    for k in range(POOL_HIST - 1):
        newpool_ref[k] = pool_ref[k + 1]
    newpool_ref[POOL_HIST - 1] = xp
    for g, w in enumerate(POOL_WINDOWS):
        c0, c1 = g * POOL_GROUP, (g + 1) * POOL_GROUP
        xg = xp[:, c0:c1]
        tot = xg
        for j in range(1, w):
            tot = tot + pool_ref[POOL_HIST - j, :, c0:c1]
        cnt = float(min(PAST_LEN + 1, w))
        d = tot / cnt - xg
        og = jnp.dot(d.astype(BF16), wpool_ref[g], preferred_element_type=F32)
        gp = z_ref[:, 2 * D_RNN + D_POOL + c0:2 * D_RNN + D_POOL + c1]
        o_ref[:, D_RNN + c0:D_RNN + c1] = (og * pscale_ref[:, c0:c1] * _silu(gp)).astype(BF16)

    gx = z_ref[:, 2 * D_RNN + 2 * D_POOL + D_X:2 * D_MIX]
    o_ref[:, D_RNN + D_POOL:] = (attn_ref[...] * _silu(gx)).astype(BF16)


def _sample_mix(z, attn, conv, h, pool, conv_w, conv_b, wax, b_a, b_x, lam, wpool, pscale, tb):
    nb = z.shape[0]
    zw = 2 * D_MIX
    rows = lambda i: (i, 0)
    const2 = lambda i: (0, 0)
    const3 = lambda i: (0, 0, 0)
    hist = lambda i: (0, i, 0)
    return pl.pallas_call(
        _sample_mix_kernel,
        grid=(nb // tb,),
        in_specs=[
            pl.BlockSpec((tb, zw), rows),
            pl.BlockSpec((tb, D_X), rows),
            pl.BlockSpec((CONV_W - 1, tb, D_RNN), hist),
            pl.BlockSpec((tb, D_RNN), rows),
            pl.BlockSpec((POOL_HIST, tb, D_POOL), hist),
            pl.BlockSpec((CONV_W, D_RNN), const2),
            pl.BlockSpec((1, D_RNN), const2),
            pl.BlockSpec((N_RNN_BLOCKS, RNN_BLOCK, 2 * RNN_BLOCK), const3),
            pl.BlockSpec((1, D_RNN), const2),
            pl.BlockSpec((1, D_RNN), const2),
            pl.BlockSpec((1, D_RNN), const2),
            pl.BlockSpec((len(POOL_WINDOWS), POOL_GROUP, POOL_GROUP), const3),
            pl.BlockSpec((1, D_POOL), const2),
        ],
        out_specs=[
            pl.BlockSpec((tb, D_MIX), rows),
            pl.BlockSpec((tb, D_RNN), rows),
            pl.BlockSpec((CONV_W - 1, tb, D_RNN), hist),
            pl.BlockSpec((POOL_HIST, tb, D_POOL), hist),
        ],
        out_shape=[
            jax.ShapeDtypeStruct((nb, D_MIX), BF16),
            jax.ShapeDtypeStruct((nb, D_RNN), F32),
            jax.ShapeDtypeStruct((CONV_W - 1, nb, D_RNN), F32),
            jax.ShapeDtypeStruct((POOL_HIST, nb, D_POOL), F32),
        ],
        compiler_params=pltpu.CompilerParams(
            dimension_semantics=("arbitrary",),
            vmem_limit_bytes=VMEM_LIMIT),
        name="sample_mix",
    )(z, attn, conv, h, pool, conv_w, conv_b, wax, b_a, b_x, lam, wpool, pscale)


def _branch_out_kernel(o_ref, gates_ref, x_ref, wb_ref, wo_ref, gpost_ref, *rest, interleaved):
    y_ref = rest[-1]
    merged = None
    for j, (r0, r1) in enumerate(((0, D_RNN), (D_RNN, D_RNN + D_POOL), (D_RNN + D_POOL, D_MIX))):
        yj = jnp.dot(o_ref[:, r0:r1], wb_ref[r0:r1, :], preferred_element_type=F32)
        term = _sigmoid(gates_ref[:, j * D_MODEL:(j + 1) * D_MODEL]) * yj
        merged = term if merged is None else merged + term
    merged = merged.astype(BF16)
    if interleaved:
        merged = jnp.dot(rest[0][...], merged, preferred_element_type=F32).astype(BF16)
    out = jnp.dot(merged, wo_ref[...], preferred_element_type=F32)
    y_ref[...] = x_ref[...] + out * _rms_scale(out) * gpost_ref[...]


def _branch_out(o, z, x, wb, wo, g_post, tm, unperm=None):
    m = x.shape[0]
    gw = N_BRANCH * D_MODEL
    gblk = (2 * D_MIX) // gw
    resident = pl.Buffered(1)
    in_specs = [
        pl.BlockSpec((tm, D_MIX), lambda i: (i, 0)),
        pl.BlockSpec((tm, gw), lambda i: (i, gblk)),
        pl.BlockSpec((tm, D_MODEL), lambda i: (i, 0)),
        pl.BlockSpec((D_MIX, D_MODEL), lambda i: (0, 0), pipeline_mode=resident),
        pl.BlockSpec((D_MODEL, D_MODEL), lambda i: (0, 0), pipeline_mode=resident),
        pl.BlockSpec((1, D_MODEL), lambda i: (0, 0)),
    ]
    args = [o, z, x, wb, wo, g_post]
    if unperm is not None:
        assert unperm.shape == (tm, tm)
        in_specs.append(pl.BlockSpec(unperm.shape, lambda i: (0, 0)))
        args.append(unperm)
    return pl.pallas_call(
        functools.partial(_branch_out_kernel, interleaved=unperm is not None),
        grid=(m // tm,),
        in_specs=in_specs,
        out_specs=pl.BlockSpec((tm, D_MODEL), lambda i: (i, 0)),
        out_shape=jax.ShapeDtypeStruct((m, D_MODEL), F32),
        compiler_params=pltpu.CompilerParams(
            dimension_semantics=("arbitrary",),
            vmem_limit_bytes=VMEM_LIMIT),
        name="branch_out",
    )(*args)


WROWS = 512
PER_BRANCH = D_RNN // WROWS
assert D_RNN == D_POOL == D_X and D_RNN % WROWS == 0 and D_MODEL % WROWS == 0
N_WB_BLOCKS = N_BRANCH * PER_BRANCH
N_WOUT_BLOCKS = D_MODEL // WROWS


def _branch_out_cast_kernel(o_ref, gates_ref, x_ref, wb_ref, wo_ref, gpost_ref,
                            y_ref, wbb_ref, wob_ref, merged_ref, out_ref):
    s = pl.program_id(0)

    @pl.when(s < N_WB_BLOCKS)
    def _():
        w = wb_ref[...].astype(BF16)
        wbb_ref[...] = w
        term = _sigmoid(gates_ref[...]) * jnp.dot(o_ref[...], w, preferred_element_type=F32)

        @pl.when(s == 0)
        def _():
            merged_ref[...] = term

        @pl.when(s > 0)
        def _():
            merged_ref[...] += term

    for kb in range(N_WOUT_BLOCKS):
        @pl.when(s == N_WB_BLOCKS + kb)
        def _(kb=kb):
            w = wo_ref[...].astype(BF16)
            wob_ref[...] = w
            part = jnp.dot(merged_ref[:, kb * WROWS:(kb + 1) * WROWS].astype(BF16), w,
                           preferred_element_type=F32)
            if kb == 0:
                out_ref[...] = part
            else:
                out_ref[...] += part

    @pl.when(s == N_WB_BLOCKS + N_WOUT_BLOCKS - 1)
    def _():
        out = out_ref[...]
        y_ref[...] = x_ref[...] + out * _rms_scale(out) * gpost_ref[...]


def _branch_out_cast(o, z, x, wb, wo, g_post):
    m = x.shape[0]
    gblk0 = (2 * D_MIX) // D_MODEL
    wb_blk = lambda s: jnp.minimum(s, N_WB_BLOCKS - 1)
    wo_blk = lambda s: jnp.maximum(s - N_WB_BLOCKS, 0)
    return pl.pallas_call(
        _branch_out_cast_kernel,
        grid=(N_WB_BLOCKS + N_WOUT_BLOCKS,),
        in_specs=[
            pl.BlockSpec((m, WROWS), lambda s: (0, wb_blk(s))),
            pl.BlockSpec((m, D_MODEL), lambda s: (0, gblk0 + wb_blk(s) // PER_BRANCH)),
            pl.BlockSpec((m, D_MODEL), lambda s: (0, 0)),
            pl.BlockSpec((WROWS, D_MODEL), lambda s: (wb_blk(s), 0)),
            pl.BlockSpec((WROWS, D_MODEL), lambda s: (wo_blk(s), 0)),
            pl.BlockSpec((1, D_MODEL), lambda s: (0, 0)),
        ],
        out_specs=[
            pl.BlockSpec((m, D_MODEL), lambda s: (0, 0)),
            pl.BlockSpec((WROWS, D_MODEL), lambda s: (wb_blk(s), 0)),
            pl.BlockSpec((WROWS, D_MODEL), lambda s: (wo_blk(s), 0)),
        ],
        out_shape=[
            jax.ShapeDtypeStruct((m, D_MODEL), F32),
            jax.ShapeDtypeStruct(wb.shape, BF16),
            jax.ShapeDtypeStruct(wo.shape, BF16),
        ],
        scratch_shapes=[pltpu.VMEM((m, D_MODEL), F32), pltpu.VMEM((m, D_MODEL), F32)],
        compiler_params=pltpu.CompilerParams(
            dimension_semantics=("arbitrary",),
            vmem_limit_bytes=VMEM_LIMIT),
        name="branch_out_cast",
    )(o, z, x, wb, wo, g_post)


def _mem_kv_kernel(x_ref, g_ref, w_ref, k_ref, v_ref, u_ref):
    j = pl.program_id(1)

    @pl.when(j == 0)
    def _():
        x = x_ref[...]
        u_ref[...] = (x * _rms_scale(x) * g_ref[...]).astype(BF16)

    res = jnp.dot(u_ref[...], w_ref[...].astype(BF16), preferred_element_type=F32)

    @pl.when(j == 0)
    def _():
        k_ref[...] = res

    @pl.when(j == 1)
    def _():
        v_ref[...] = res


def _mem_kv(x, g, w, tm):
    m, k = x.shape
    assert w.shape[1] == 2 * D_X
    half = pl.BlockSpec((tm, D_X), lambda i, j: (i, 0))
    return pl.pallas_call(
        _mem_kv_kernel,
        grid=(m // tm, 2),
        in_specs=[
            pl.BlockSpec((tm, k), lambda i, j: (i, 0)),
            pl.BlockSpec((1, k), lambda i, j: (0, 0)),
            pl.BlockSpec((k, D_X), lambda i, j: (0, j)),
        ],
        out_specs=[half, half],
        out_shape=[jax.ShapeDtypeStruct((m, D_X), F32)] * 2,
        scratch_shapes=[pltpu.VMEM((tm, k), BF16)],
        compiler_params=pltpu.CompilerParams(
            dimension_semantics=("arbitrary", "arbitrary"),
            vmem_limit_bytes=VMEM_LIMIT),
        name="mem_kv",
    )(x, g, w)


def kernel(x_prompt, x_sample, mem_prompt, state_rglru_h, state_conv, state_pool, cache_mem_k, cache_mem_v, g_pre, w_in, conv_w, conv_b, w_rg_a, b_rg_a, w_rg_x, b_rg_x, lru_lambda, w_pool, pool_scale, g_mem, w_kv, w_branch, w_out, g_post):
    batch, seq, _ = x_prompt.shape
    nb = x_sample.shape[0]
    depth = g_pre.shape[0]
    assert depth == 1 and x_sample.shape[1] == 1

    l = 0
    row = lambda v: v.reshape(1, -1)
    wax = jnp.concatenate([w_rg_a[l], w_rg_x[l]], axis=-1).astype(BF16)
    wpool = w_pool[l].astype(BF16)
    mix_params = (conv_w[l], row(conv_b[l]), wax, row(b_rg_a[l]), row(b_rg_x[l]),
                  row(lru_lambda[l]), wpool, row(pool_scale[l]))

    xp2 = x_prompt.reshape(batch * seq, D_MODEL)
    xs2 = x_sample.reshape(nb, D_MODEL)
    mem2 = mem_prompt.reshape(batch * N_MEM, D_MODEL)

    z_s, w_in_b = _sample_proj(xs2, row(g_pre[l]), w_in[l], tn=1024)
    qoff = 2 * D_RNN + 2 * D_POOL
    q_s = z_s[:, qoff:qoff + D_X].reshape(nb // ATT_BB, ATT_BB, D_X)

    perm = _chunk_interleave()
    z_p, attn_s = _prompt_proj(xp2, row(g_pre[l]), w_in_b, perm, q_s,
                               _cache_rows(cache_mem_k[l]), _cache_rows(cache_mem_v[l]),
                               tm=1024, tn=512)
    attn_s = attn_s.reshape(nb, D_X)

    o_s, h_s, c_s, p_s = _sample_mix(
        z_s, attn_s, state_conv[l].transpose(1, 0, 2), state_rglru_h[l],
        state_pool[l].transpose(1, 0, 2), *mix_params, tb=32)
    y_s, w_br_b, w_out_b = _branch_out_cast(o_s, z_s, xs2, w_branch[l], w_out[l], row(g_post[l]))

    mem_k, mem_v = _mem_kv(mem2, row(g_mem[l]), w_kv[l], tm=512)
    mem_k = mem_k.reshape(batch, N_MEM, D_X)
    mem_v = mem_v.reshape(batch, N_MEM, D_X)

    o_p, h_p, c_p, p_p = _prompt_mix(z_p, mem_k, mem_v, *mix_params,
                                     batch=batch, seq=seq, tm=MIX_TM)
    y_p = _branch_out(o_p, z_p, xp2, w_br_b, w_out_b, row(g_post[l]), tm=MIX_TM, unperm=perm.T)

    return (
        y_p.reshape(batch, seq, D_MODEL),
        y_s.reshape(nb, 1, D_MODEL),
        h_p.reshape(1, batch, D_RNN),
        c_p.reshape(1, batch, CONV_W - 1, D_RNN),
        p_p.reshape(1, batch, POOL_HIST, D_POOL),
        mem_k.reshape(1, batch, N_MEM, N_XHEADS, XHEAD_DIM),
        mem_v.reshape(1, batch, N_MEM, N_XHEADS, XHEAD_DIM),
        h_s.reshape(1, nb, D_RNN),
        c_s.transpose(1, 0, 2)[None],
        p_s.transpose(1, 0, 2)[None],
    )
```

```python
import functools
import math

import jax
import jax.numpy as jnp
from jax import lax
from jax.experimental import pallas as pl
from jax.experimental.pallas import tpu as pltpu

D_MODEL = 2048
PAST_LEN = 16384
D_RNN = 1024
N_RNN_BLOCKS = 8
RNN_BLOCK = D_RNN // N_RNN_BLOCKS
CONV_W = 4
LRU_C = 8.0
D_POOL = 1024
POOL_WINDOWS = (2, 4, 8, 16)
POOL_GROUP = D_POOL // len(POOL_WINDOWS)
POOL_HIST = max(POOL_WINDOWS) - 1
N_MEM = 256
N_XHEADS = 4
XHEAD_DIM = 256
D_X = N_XHEADS * XHEAD_DIM
N_BRANCH = 3
D_MIX = D_RNN + D_POOL + D_X
D_IN = 2 * D_MIX + N_BRANCH * D_MODEL
EPS = 1e-6

SUBLANES = 8
LANES = 128
VMEM_LIMIT = 56 * 1024 * 1024
PROJ_VMEM_LIMIT = 60 * 1024 * 1024
MIX_TM = 256

BF16 = jnp.bfloat16
F32 = jnp.float32

NEG_LOG2_E = -1.4426950408889634


def _sigmoid(x):
    return 1.0 / (1.0 + jnp.exp2(x * NEG_LOG2_E))


def _silu(x):
    return x * _sigmoid(x)


def _softplus(x):
    return jnp.maximum(x, 0.0) + jnp.log1p(jnp.exp(-jnp.abs(x)))


def _rms_scale(x):
    return lax.rsqrt(jnp.mean(x * x, axis=-1, keepdims=True) + EPS)


def _chunk_interleave():
    nrow = MIX_TM // SUBLANES
    p = jnp.arange(MIX_TM)
    token = (p % SUBLANES) * nrow + p // SUBLANES
    return (token[:, None] == jnp.arange(MIX_TM)[None, :]).astype(BF16)


def _sample_proj_kernel(x_ref, g_ref, w_ref, o_ref, wb_ref, u_ref):
    @pl.when(pl.program_id(0) == 0)
    def _():
        x = x_ref[...]
        u_ref[...] = (x * _rms_scale(x) * g_ref[...]).astype(BF16)

    w = w_ref[...].astype(BF16)
    wb_ref[...] = w
    o_ref[...] = jnp.dot(u_ref[...], w, preferred_element_type=F32)


def _sample_proj(x, g, w, tn):
    m, k = x.shape
    n = w.shape[1]
    return pl.pallas_call(
        _sample_proj_kernel,
        grid=(n // tn,),
        in_specs=[
            pl.BlockSpec((m, k), lambda j: (0, 0)),
            pl.BlockSpec((1, k), lambda j: (0, 0)),
            pl.BlockSpec((k, tn), lambda j: (0, j)),
        ],
        out_specs=[
            pl.BlockSpec((m, tn), lambda j: (0, j)),
            pl.BlockSpec((k, tn), lambda j: (0, j)),
        ],
        out_shape=[
            jax.ShapeDtypeStruct((m, n), F32),
            jax.ShapeDtypeStruct((k, n), BF16),
        ],
        scratch_shapes=[pltpu.VMEM((m, k), BF16)],
        compiler_params=pltpu.CompilerParams(
            dimension_semantics=("arbitrary",),
            vmem_limit_bytes=VMEM_LIMIT),
        name="sample_proj",
    )(x, g, w)


def _rglru_block(xc, wax, ba, bx, sp):
    ri = jnp.dot(xc.astype(BF16), wax, preferred_element_type=F32)
    r = _sigmoid(ri[:, :RNN_BLOCK] + ba)
    i = _sigmoid(ri[:, RNN_BLOCK:] + bx)
    a = jnp.exp((-LRU_C) * r * sp)
    one_m = 1.0 - a * a
    mult = jnp.where(one_m > 0.0, one_m * lax.rsqrt(one_m), 0.0)
    return a, mult * i * xc


def _prompt_mix_kernel(z_ref, k_ref, v_ref, convw_ref, convb_ref, wax_ref, ba_ref, bx_ref,
                       lam_ref, wpool_ref, pscale_ref,
                       o_ref, newh_ref, newconv_ref, newpool_ref,
                       conv_carry, pool_carry, h_carry, kb_ref, vb_ref, ac_scr, hl_scr, *, tm):
    l = pl.program_id(1)
    last = pl.num_programs(1) - 1
    nrow = tm // SUBLANES

    @pl.when(l == 0)
    def _():
        conv_carry[...] = jnp.zeros(conv_carry.shape, F32)
        pool_carry[...] = jnp.zeros(pool_carry.shape, F32)
        h_carry[...] = jnp.zeros(h_carry.shape, F32)
        kb_ref[...] = k_ref[0].astype(BF16)
        vb_ref[...] = v_ref[0].astype(BF16)

    chunk_id = lax.broadcasted_iota(jnp.int32, (SUBLANES, LANES), 0)
    first_chunk = chunk_id == 0

    def load_groups(col, width=LANES):
        return [z_ref[r * SUBLANES:(r + 1) * SUBLANES, col:col + width] for r in range(nrow)]

    def store_groups(col, rows, width=LANES):
        o_ref[:, col:col + width] = jnp.concatenate(rows, axis=0).astype(BF16)

    def history(tail_group, carry_ref, j, c0):
        tail = pltpu.roll(tail_group, 1, 0)
        prev = jnp.where(first_chunk, carry_ref[j - 1, :, c0:c0 + LANES], tail)
        carry_ref[j - 1, :, c0:c0 + LANES] = tail
        return prev

    sp = _softplus(-lam_ref[...])
    for n in range(N_RNN_BLOCKS):
        c0, c1 = n * RNN_BLOCK, (n + 1) * RNN_BLOCK
        xs = load_groups(c0)
        ext = [history(xs[nrow - j], conv_carry, j, c0) for j in range(CONV_W - 1, 0, -1)] + xs
        cw = [jnp.broadcast_to(convw_ref[k:k + 1, c0:c1], (SUBLANES, LANES)) for k in range(CONV_W)]
        cb = jnp.broadcast_to(convb_ref[:, c0:c1], (SUBLANES, LANES))
        xc = []
        for r in range(nrow):
            acc = cb + cw[0] * ext[r]
            for k in range(1, CONV_W):
                acc = acc + cw[k] * ext[r + k]
            xc.append(acc)
        a, b = _rglru_block(jnp.concatenate(xc, axis=0), wax_ref[n], ba_ref[:, c0:c1],
                            bx_ref[:, c0:c1], sp[:, c0:c1])
        ac_scr[:, c0:c1] = a
        hl_scr[:, c0:c1] = b

    acc_a = ac_scr[0:SUBLANES, :]
    acc_h = hl_scr[0:SUBLANES, :]
    for r in range(1, nrow):
        rows = slice(r * SUBLANES, (r + 1) * SUBLANES)
        ar = ac_scr[rows, :]
        acc_h = ar * acc_h + hl_scr[rows, :]
        acc_a = ar * acc_a
        ac_scr[rows, :] = acc_a
        hl_scr[rows, :] = acc_h
    h_in = h_carry[...]
    entering = []
    for c in range(SUBLANES):
        entering.append(h_in)
        h_in = acc_a[c:c + 1] * h_in + acc_h[c:c + 1]
    h_carry[...] = h_in
    h_enter = jnp.concatenate(entering, axis=0)
    for n in range(N_RNN_BLOCKS):
        c0, c1 = n * RNN_BLOCK, (n + 1) * RNN_BLOCK
        gr = load_groups(D_RNN + c0)
        store_groups(c0, [(hl_scr[r * SUBLANES:(r + 1) * SUBLANES, c0:c1]
                           + ac_scr[r * SUBLANES:(r + 1) * SUBLANES, c0:c1] * h_enter[:, c0:c1])
                          * _silu(gr[r]) for r in range(nrow)])

    pcol = 2 * D_RNN
    blocks = [(w, c0) for g, w in enumerate(POOL_WINDOWS)
              for c0 in range(g * POOL_GROUP, (g + 1) * POOL_GROUP, LANES)]

    def group(c0, r):
        return z_ref[r * SUBLANES:(r + 1) * SUBLANES, pcol + c0:pcol + c0 + LANES]

    def mean_minus_token(tot, w, c0, r):
        if r < w - 1:
            pos1 = l * tm + chunk_id * nrow + (r + 1)
            mean = tot / jnp.minimum(pos1, w).astype(F32)
        else:
            mean = tot * (1.0 / w)
        return mean - group(c0, r)

    hist, tot = {}, {}
    for w, c0 in blocks:
        hist[c0] = [history(group(c0, nrow - j), pool_carry, j, c0) for j in range(1, w)]
        t = group(c0, 0)
        for h in hist[c0]:
            t = t + h
        tot[c0] = t
        hl_scr[0:SUBLANES, c0:c0 + LANES] = mean_minus_token(t, w, c0, 0)
    for r in range(1, nrow):
        for w, c0 in blocks:
            leaving = group(c0, r - w) if r >= w else hist[c0][w - r - 1]
            tot[c0] = tot[c0] + (group(c0, r) - leaving)
            hl_scr[r * SUBLANES:(r + 1) * SUBLANES, c0:c0 + LANES] = mean_minus_token(
                tot[c0], w, c0, r)
    for g, w in enumerate(POOL_WINDOWS):
        c0, c1 = g * POOL_GROUP, (g + 1) * POOL_GROUP
        og = jnp.dot(hl_scr[:, c0:c1].astype(BF16), wpool_ref[g], preferred_element_type=F32)
        gp = z_ref[:, pcol + D_POOL + c0:pcol + D_POOL + c1]
        o_ref[:, D_RNN + c0:D_RNN + c1] = (og * pscale_ref[:, c0:c1] * _silu(gp)).astype(BF16)

    qoff = 2 * D_RNN + 2 * D_POOL
    for hd in range(N_XHEADS):
        c0, c1 = hd * XHEAD_DIM, (hd + 1) * XHEAD_DIM
        q = z_ref[:, qoff + c0:qoff + c1].astype(BF16)
        s = lax.dot_general(q, kb_ref[:, c0:c1], (((1,), (1,)), ((), ())),
                            preferred_element_type=F32) * (XHEAD_DIM ** -0.5)
        p = jnp.exp(s - jnp.max(s, axis=-1, keepdims=True))
        p = p / jnp.sum(p, axis=-1, keepdims=True)
        ox = jnp.dot(p.astype(BF16), vb_ref[:, c0:c1], preferred_element_type=F32)
        gx = z_ref[:, qoff + D_X + c0:qoff + D_X + c1]
        o_ref[:, D_RNN + D_POOL + c0:D_RNN + D_POOL + c1] = (ox * _silu(gx)).astype(BF16)

    @pl.when(l == last)
    def _():
        newh_ref[0] = h_carry[...]
        tail_row = lambda j: (nrow - j) * SUBLANES + SUBLANES - 1
        for j in range(1, CONV_W):
            newconv_ref[0, CONV_W - 1 - j:CONV_W - j, :] = z_ref[tail_row(j):tail_row(j) + 1, 0:D_RNN]
        for j in range(1, POOL_HIST + 1):
            newpool_ref[0, POOL_HIST - j:POOL_HIST - j + 1, :] = (
                z_ref[tail_row(j):tail_row(j) + 1, pcol:pcol + D_POOL])


def _prompt_mix(z, mem_k, mem_v, conv_w, conv_b, wax, b_a, b_x, lam, wpool, pscale,
                batch, seq, tm):
    nl = seq // tm
    zw = 2 * D_MIX
    const2 = lambda b, l: (0, 0)
    const3 = lambda b, l: (0, 0, 0)
    kern = functools.partial(_prompt_mix_kernel, tm=tm)
    return pl.pallas_call(
        kern,
        grid=(batch, nl),
        in_specs=[
            pl.BlockSpec((tm, zw), lambda b, l: (b * nl + l, 0)),
            pl.BlockSpec((1, N_MEM, D_X), lambda b, l: (b, 0, 0)),
            pl.BlockSpec((1, N_MEM, D_X), lambda b, l: (b, 0, 0)),
            pl.BlockSpec((CONV_W, D_RNN), const2),
            pl.BlockSpec((1, D_RNN), const2),
            pl.BlockSpec((N_RNN_BLOCKS, RNN_BLOCK, 2 * RNN_BLOCK), const3),
            pl.BlockSpec((1, D_RNN), const2),
            pl.BlockSpec((1, D_RNN), const2),
            pl.BlockSpec((1, D_RNN), const2),
            pl.BlockSpec((len(POOL_WINDOWS), POOL_GROUP, POOL_GROUP), const3),
            pl.BlockSpec((1, D_POOL), const2),
        ],
        out_specs=[
            pl.BlockSpec((tm, D_MIX), lambda b, l: (b * nl + l, 0)),
            pl.BlockSpec((1, 1, D_RNN), lambda b, l: (b, 0, 0)),
            pl.BlockSpec((1, CONV_W - 1, D_RNN), lambda b, l: (b, 0, 0)),
            pl.BlockSpec((1, POOL_HIST, D_POOL), lambda b, l: (b, 0, 0)),
        ],
        out_shape=[
            jax.ShapeDtypeStruct((batch * seq, D_MIX), BF16),
            jax.ShapeDtypeStruct((batch, 1, D_RNN), F32),
            jax.ShapeDtypeStruct((batch, CONV_W - 1, D_RNN), F32),
            jax.ShapeDtypeStruct((batch, POOL_HIST, D_POOL), F32),
        ],
        scratch_shapes=[
            pltpu.VMEM((CONV_W - 1, SUBLANES, D_RNN), F32),
            pltpu.VMEM((POOL_HIST, SUBLANES, D_POOL), F32),
            pltpu.VMEM((1, D_RNN), F32),
            pltpu.VMEM((N_MEM, D_X), BF16),
            pltpu.VMEM((N_MEM, D_X), BF16),
            pltpu.VMEM((tm, D_RNN), F32),
            pltpu.VMEM((tm, D_RNN), F32),
        ],
        compiler_params=pltpu.CompilerParams(
            dimension_semantics=("arbitrary", "arbitrary"),
            vmem_limit_bytes=VMEM_LIMIT),
        name="prompt_mix",
    )(z, mem_k, mem_v, conv_w, conv_b, wax, b_a, b_x, lam, wpool, pscale)


def _cache_rows(c):
    nb = c.shape[0]
    c = c.reshape(nb, N_MEM, N_XHEADS, XHEAD_DIM // LANES, LANES)
    return c.transpose(0, 1, 3, 2, 4).reshape(nb, N_MEM * SUBLANES, LANES)


def _sample_attn_block(q_ref, k_ref, v_ref, o_ref, bb):
    halves = XHEAD_DIM // LANES
    assert halves * N_XHEADS == SUBLANES
    r = lax.broadcasted_iota(jnp.int32, (SUBLANES, LANES), 0)
    c = lax.broadcasted_iota(jnp.int32, (SUBLANES, LANES), 1)
    diag = (c % SUBLANES) == r
    first_half = r < N_XHEADS
    nchunk = N_MEM * SUBLANES // LANES
    scores = []
    for j in range(bb):
        qn = jnp.concatenate(
            [q_ref[j:j + 1, (h * halves + t) * LANES:(h * halves + t + 1) * LANES]
             for t in range(halves) for h in range(N_XHEADS)], axis=0)
        scores.append(lax.dot_general(qn.astype(BF16), k_ref[j].astype(BF16),
                                      (((1,), (1,)), ((), ())), preferred_element_type=F32)
                      * (XHEAD_DIM ** -0.5))
    probs = []
    for j in range(bb):
        s = scores[j]
        chunks = []
        for ci in range(nchunk):
            sm = jnp.where(diag, s[:, ci * LANES:(ci + 1) * LANES], 0.0)
            other = pltpu.roll(sm, N_XHEADS, 0)
            other = jnp.where(first_half, pltpu.roll(other, LANES - N_XHEADS, 1),
                              pltpu.roll(other, N_XHEADS, 1))
            chunks.append(jnp.where(diag, sm + other, -jnp.inf))
        t_full = jnp.concatenate(chunks, axis=1)
        e = jnp.exp(t_full - jnp.max(t_full, axis=1, keepdims=True))
        probs.append((e / jnp.sum(e, axis=1, keepdims=True)).astype(BF16))
    for j in range(bb):
        o = jnp.dot(probs[j], v_ref[j].astype(BF16), preferred_element_type=F32)
        for t in range(halves):
            for h in range(N_XHEADS):
                col = (h * halves + t) * LANES
                o_ref[j:j + 1, col:col + LANES] = o[t * N_XHEADS + h:t * N_XHEADS + h + 1, :]


ATT_BB = 4


def _prompt_proj_kernel(x_ref, g_ref, w_ref, perm_ref, q_ref, k_ref, v_ref, o_ref, attn_ref, u_ref,
                        *, period, per):
    j = pl.program_id(1)
    step = pl.program_id(0) * pl.num_programs(1) + j

    @pl.when(j == 0)
    def _():
        x = x_ref[...]
        u = (x * _rms_scale(x) * g_ref[...]).astype(BF16)
        for r0 in range(0, u.shape[0], MIX_TM):
            u_ref[r0:r0 + MIX_TM, :] = jnp.dot(
                perm_ref[...], u[r0:r0 + MIX_TM], preferred_element_type=F32).astype(BF16)

    o_ref[...] = jnp.dot(u_ref[...], w_ref[...], preferred_element_type=F32)

    @pl.when(step % period < per)
    def _():
        _sample_attn_block(q_ref.at[0], k_ref, v_ref, attn_ref.at[0], ATT_BB)


def _prompt_proj(x, g, w, perm, q, cache_k, cache_v, tm, tn):
    m, k = x.shape
    n = w.shape[1]
    nblk = q.shape[0]
    nsteps = (m // tm) * (n // tn)
    common = math.gcd(nsteps, nblk)
    period, per = nsteps // common, nblk // common
    assert per <= period
    ncol = n // tn

    def blk(i, j):
        step = i * ncol + j
        return ((step // period) * per + jnp.minimum(step % period, per - 1), 0, 0)

    return pl.pallas_call(
        functools.partial(_prompt_proj_kernel, period=period, per=per),
        grid=(m // tm, ncol),
        in_specs=[
            pl.BlockSpec((tm, k), lambda i, j: (i, 0)),
            pl.BlockSpec((1, k), lambda i, j: (0, 0)),
            pl.BlockSpec((k, tn), lambda i, j: (0, j)),
            pl.BlockSpec(perm.shape, lambda i, j: (0, 0)),
            pl.BlockSpec((1, ATT_BB, D_X), blk),
            pl.BlockSpec((ATT_BB, N_MEM * SUBLANES, LANES), blk),
            pl.BlockSpec((ATT_BB, N_MEM * SUBLANES, LANES), blk),
        ],
        out_specs=[
            pl.BlockSpec((tm, tn), lambda i, j: (i, j)),
            pl.BlockSpec((1, ATT_BB, D_X), blk),
        ],
        out_shape=[
            jax.ShapeDtypeStruct((m, n), F32),
            jax.ShapeDtypeStruct(q.shape, F32),
        ],
        scratch_shapes=[pltpu.VMEM((tm, k), BF16)],
        compiler_params=pltpu.CompilerParams(
            dimension_semantics=("arbitrary", "arbitrary"),
            vmem_limit_bytes=PROJ_VMEM_LIMIT),
        name="prompt_proj",
    )(x, g, w, perm, q, cache_k, cache_v)


def _sample_mix_kernel(z_ref, attn_ref, conv_ref, h_ref, pool_ref,
                       convw_ref, convb_ref, wax_ref, ba_ref, bx_ref, lam_ref, wpool_ref,
                       pscale_ref, o_ref, newh_ref, newconv_ref, newpool_ref):
    xr = z_ref[:, 0:D_RNN]
    xc = convb_ref[...] + convw_ref[CONV_W - 1:CONV_W, :] * xr
    for k in range(CONV_W - 1):
        xc = xc + convw_ref[k:k + 1, :] * conv_ref[k]
    for k in range(CONV_W - 2):
        newconv_ref[k] = conv_ref[k + 1]
    newconv_ref[CONV_W - 2] = xr

    sp = _softplus(-lam_ref[...])
    for n in range(N_RNN_BLOCKS):
        c0, c1 = n * RNN_BLOCK, (n + 1) * RNN_BLOCK
        a, b = _rglru_block(xc[:, c0:c1], wax_ref[n], ba_ref[:, c0:c1], bx_ref[:, c0:c1],
                            sp[:, c0:c1])
        h = a * h_ref[:, c0:c1] + b
        newh_ref[:, c0:c1] = h
        o_ref[:, c0:c1] = (h * _silu(z_ref[:, D_RNN + c0:D_RNN + c1])).astype(BF16)

    xp = z_ref[:, 2 * D_RNN:2 * D_RNN + D_POOL]
    for k in range(POOL_HIST - 1):
        newpool_ref[k] = pool_ref[k + 1]
    newpool_ref[POOL_HIST - 1] = xp
    for g, w in enumerate(POOL_WINDOWS):
        c0, c1 = g * POOL_GROUP, (g + 1) * POOL_GROUP
        xg = xp[:, c0:c1]
        tot = xg
        for j in range(1, w):
            tot = tot + pool_ref[POOL_HIST - j, :, c0:c1]
        cnt = float(min(PAST_LEN + 1, w))
        d = tot / cnt - xg
        og = jnp.dot(d.astype(BF16), wpool_ref[g], preferred_element_type=F32)
        gp = z_ref[:, 2 * D_RNN + D_POOL + c0:2 * D_RNN + D_POOL + c1]
        o_ref[:, D_RNN + c0:D_RNN + c1] = (og * pscale_ref[:, c0:c1] * _silu(gp)).astype(BF16)

    gx = z_ref[:, 2 * D_RNN + 2 * D_POOL + D_X:2 * D_MIX]
    o_ref[:, D_RNN + D_POOL:] = (attn_ref[...] * _silu(gx)).astype(BF16)


def _sample_mix(z, attn, conv, h, pool, conv_w, conv_b, wax, b_a, b_x, lam, wpool, pscale, tb):
    nb = z.shape[0]
    zw = 2 * D_MIX
    rows = lambda i: (i, 0)
    const2 = lambda i: (0, 0)
    const3 = lambda i: (0, 0, 0)
    hist = lambda i: (0, i, 0)
    return pl.pallas_call(
        _sample_mix_kernel,
        grid=(nb // tb,),
        in_specs=[
            pl.BlockSpec((tb, zw), rows),
            pl.BlockSpec((tb, D_X), rows),
            pl.BlockSpec((CONV_W - 1, tb, D_RNN), hist),
            pl.BlockSpec((tb, D_RNN), rows),
            pl.BlockSpec((POOL_HIST, tb, D_POOL), hist),
            pl.BlockSpec((CONV_W, D_RNN), const2),
            pl.BlockSpec((1, D_RNN), const2),
            pl.BlockSpec((N_RNN_BLOCKS, RNN_BLOCK, 2 * RNN_BLOCK), const3),
            pl.BlockSpec((1, D_RNN), const2),
            pl.BlockSpec((1, D_RNN), const2),
            pl.BlockSpec((1, D_RNN), const2),
            pl.BlockSpec((len(POOL_WINDOWS), POOL_GROUP, POOL_GROUP), const3),
            pl.BlockSpec((1, D_POOL), const2),
        ],
        out_specs=[
            pl.BlockSpec((tb, D_MIX), rows),
            pl.BlockSpec((tb, D_RNN), rows),
            pl.BlockSpec((CONV_W - 1, tb, D_RNN), hist),
            pl.BlockSpec((POOL_HIST, tb, D_POOL), hist),
        ],
        out_shape=[
            jax.ShapeDtypeStruct((nb, D_MIX), BF16),
            jax.ShapeDtypeStruct((nb, D_RNN), F32),
            jax.ShapeDtypeStruct((CONV_W - 1, nb, D_RNN), F32),
            jax.ShapeDtypeStruct((POOL_HIST, nb, D_POOL), F32),
        ],
        compiler_params=pltpu.CompilerParams(
            dimension_semantics=("arbitrary",),
            vmem_limit_bytes=VMEM_LIMIT),
        name="sample_mix",
    )(z, attn, conv, h, pool, conv_w, conv_b, wax, b_a, b_x, lam, wpool, pscale)


def _branch_out_kernel(o_ref, gates_ref, x_ref, wb_ref, wo_ref, gpost_ref, *rest, interleaved):
    y_ref = rest[-1]
    merged = None
    for j, (r0, r1) in enumerate(((0, D_RNN), (D_RNN, D_RNN + D_POOL), (D_RNN + D_POOL, D_MIX))):
        yj = jnp.dot(o_ref[:, r0:r1], wb_ref[r0:r1, :], preferred_element_type=F32)
        term = _sigmoid(gates_ref[:, j * D_MODEL:(j + 1) * D_MODEL]) * yj
        merged = term if merged is None else merged + term
    merged = merged.astype(BF16)
    if interleaved:
        merged = jnp.dot(rest[0][...], merged, preferred_element_type=F32).astype(BF16)
    out = jnp.dot(merged, wo_ref[...], preferred_element_type=F32)
    y_ref[...] = x_ref[...] + out * _rms_scale(out) * gpost_ref[...]


def _branch_out(o, z, x, wb, wo, g_post, tm, unperm=None):
    m = x.shape[0]
    gw = N_BRANCH * D_MODEL
    gblk = (2 * D_MIX) // gw
    resident = pl.Buffered(1)
    in_specs = [
        pl.BlockSpec((tm, D_MIX), lambda i: (i, 0)),
        pl.BlockSpec((tm, gw), lambda i: (i, gblk)),
        pl.BlockSpec((tm, D_MODEL), lambda i: (i, 0)),
        pl.BlockSpec((D_MIX, D_MODEL), lambda i: (0, 0), pipeline_mode=resident),
        pl.BlockSpec((D_MODEL, D_MODEL), lambda i: (0, 0), pipeline_mode=resident),
        pl.BlockSpec((1, D_MODEL), lambda i: (0, 0)),
    ]
    args = [o, z, x, wb, wo, g_post]
    if unperm is not None:
        assert unperm.shape == (tm, tm)
        in_specs.append(pl.BlockSpec(unperm.shape, lambda i: (0, 0)))
        args.append(unperm)
    return pl.pallas_call(
        functools.partial(_branch_out_kernel, interleaved=unperm is not None),
        grid=(m // tm,),
        in_specs=in_specs,
        out_specs=pl.BlockSpec((tm, D_MODEL), lambda i: (i, 0)),
        out_shape=jax.ShapeDtypeStruct((m, D_MODEL), F32),
        compiler_params=pltpu.CompilerParams(
            dimension_semantics=("arbitrary",),
            vmem_limit_bytes=VMEM_LIMIT),
        name="branch_out",
    )(*args)


WROWS = 512
PER_BRANCH = D_RNN // WROWS
assert D_RNN == D_POOL == D_X and D_RNN % WROWS == 0 and D_MODEL % WROWS == 0
N_WB_BLOCKS = N_BRANCH * PER_BRANCH
N_WOUT_BLOCKS = D_MODEL // WROWS


def _branch_out_cast_kernel(o_ref, gates_ref, x_ref, wb_ref, wo_ref, gpost_ref,
                            y_ref, wbb_ref, wob_ref, merged_ref, out_ref):
    s = pl.program_id(0)

    @pl.when(s < N_WB_BLOCKS)
    def _():
        w = wb_ref[...].astype(BF16)
        wbb_ref[...] = w
        term = _sigmoid(gates_ref[...]) * jnp.dot(o_ref[...], w, preferred_element_type=F32)

        @pl.when(s == 0)
        def _():
            merged_ref[...] = term

        @pl.when(s > 0)
        def _():
            merged_ref[...] += term

    for kb in range(N_WOUT_BLOCKS):
        @pl.when(s == N_WB_BLOCKS + kb)
        def _(kb=kb):
            w = wo_ref[...].astype(BF16)
            wob_ref[...] = w
            part = jnp.dot(merged_ref[:, kb * WROWS:(kb + 1) * WROWS].astype(BF16), w,
                           preferred_element_type=F32)
            if kb == 0:
                out_ref[...] = part
            else:
                out_ref[...] += part

    @pl.when(s == N_WB_BLOCKS + N_WOUT_BLOCKS - 1)
    def _():
        out = out_ref[...]
        y_ref[...] = x_ref[...] + out * _rms_scale(out) * gpost_ref[...]


def _branch_out_cast(o, z, x, wb, wo, g_post):
    m = x.shape[0]
    gblk0 = (2 * D_MIX) // D_MODEL
    wb_blk = lambda s: jnp.minimum(s, N_WB_BLOCKS - 1)
    wo_blk = lambda s: jnp.maximum(s - N_WB_BLOCKS, 0)
    return pl.pallas_call(
        _branch_out_cast_kernel,
        grid=(N_WB_BLOCKS + N_WOUT_BLOCKS,),
        in_specs=[
            pl.BlockSpec((m, WROWS), lambda s: (0, wb_blk(s))),
            pl.BlockSpec((m, D_MODEL), lambda s: (0, gblk0 + wb_blk(s) // PER_BRANCH)),
            pl.BlockSpec((m, D_MODEL), lambda s: (0, 0)),
            pl.BlockSpec((WROWS, D_MODEL), lambda s: (wb_blk(s), 0)),
            pl.BlockSpec((WROWS, D_MODEL), lambda s: (wo_blk(s), 0)),
            pl.BlockSpec((1, D_MODEL), lambda s: (0, 0)),
        ],
        out_specs=[
            pl.BlockSpec((m, D_MODEL), lambda s: (0, 0)),
            pl.BlockSpec((WROWS, D_MODEL), lambda s: (wb_blk(s), 0)),
            pl.BlockSpec((WROWS, D_MODEL), lambda s: (wo_blk(s), 0)),
        ],
        out_shape=[
            jax.ShapeDtypeStruct((m, D_MODEL), F32),
            jax.ShapeDtypeStruct(wb.shape, BF16),
            jax.ShapeDtypeStruct(wo.shape, BF16),
        ],
        scratch_shapes=[pltpu.VMEM((m, D_MODEL), F32), pltpu.VMEM((m, D_MODEL), F32)],
        compiler_params=pltpu.CompilerParams(
            dimension_semantics=("arbitrary",),
            vmem_limit_bytes=VMEM_LIMIT),
        name="branch_out_cast",
    )(o, z, x, wb, wo, g_post)


def _mem_kv_kernel(x_ref, g_ref, w_ref, k_ref, v_ref, u_ref):
    j = pl.program_id(1)

    @pl.when(j == 0)
    def _():
        x = x_ref[...]
        u_ref[...] = (x * _rms_scale(x) * g_ref[...]).astype(BF16)

    res = jnp.dot(u_ref[...], w_ref[...].astype(BF16), preferred_element_type=F32)

    @pl.when(j == 0)
    def _():
        k_ref[...] = res

    @pl.when(j == 1)
    def _():
        v_ref[...] = res


def _mem_kv(x, g, w, tm):
    m, k = x.shape
    assert w.shape[1] == 2 * D_X
    half = pl.BlockSpec((tm, D_X), lambda i, j: (i, 0))
    return pl.pallas_call(
        _mem_kv_kernel,
        grid=(m // tm, 2),
        in_specs=[
            pl.BlockSpec((tm, k), lambda i, j: (i, 0)),
            pl.BlockSpec((1, k), lambda i, j: (0, 0)),
            pl.BlockSpec((k, D_X), lambda i, j: (0, j)),
        ],
        out_specs=[half, half],
        out_shape=[jax.ShapeDtypeStruct((m, D_X), F32)] * 2,
        scratch_shapes=[pltpu.VMEM((tm, k), BF16)],
        compiler_params=pltpu.CompilerParams(
            dimension_semantics=("arbitrary", "arbitrary"),
            vmem_limit_bytes=VMEM_LIMIT),
        name="mem_kv",
    )(x, g, w)


def kernel(x_prompt, x_sample, mem_prompt, state_rglru_h, state_conv, state_pool, cache_mem_k, cache_mem_v, g_pre, w_in, conv_w, conv_b, w_rg_a, b_rg_a, w_rg_x, b_rg_x, lru_lambda, w_pool, pool_scale, g_mem, w_kv, w_branch, w_out, g_post):
    batch, seq, _ = x_prompt.shape
    nb = x_sample.shape[0]
    depth = g_pre.shape[0]
    assert depth == 1 and x_sample.shape[1] == 1

    l = 0
    row = lambda v: v.reshape(1, -1)
    wax = jnp.concatenate([w_rg_a[l], w_rg_x[l]], axis=-1).astype(BF16)
    wpool = w_pool[l].astype(BF16)
    mix_params = (conv_w[l], row(conv_b[l]), wax, row(b_rg_a[l]), row(b_rg_x[l]),
                  row(lru_lambda[l]), wpool, row(pool_scale[l]))

    xp2 = x_prompt.reshape(batch * seq, D_MODEL)
    xs2 = x_sample.reshape(nb, D_MODEL)
    mem2 = mem_prompt.reshape(batch * N_MEM, D_MODEL)

    z_s, w_in_b = _sample_proj(xs2, row(g_pre[l]), w_in[l], tn=1024)
    qoff = 2 * D_RNN + 2 * D_POOL
    q_s = z_s[:, qoff:qoff + D_X].reshape(nb // ATT_BB, ATT_BB, D_X)

    perm = _chunk_interleave()
    z_p, attn_s = _prompt_proj(xp2, row(g_pre[l]), w_in_b, perm, q_s,
                               _cache_rows(cache_mem_k[l]), _cache_rows(cache_mem_v[l]),
                               tm=1024, tn=1024)
    attn_s = attn_s.reshape(nb, D_X)

    o_s, h_s, c_s, p_s = _sample_mix(
        z_s, attn_s, state_conv[l].transpose(1, 0, 2), state_rglru_h[l],
        state_pool[l].transpose(1, 0, 2), *mix_params, tb=32)
    y_s, w_br_b, w_out_b = _branch_out_cast(o_s, z_s, xs2, w_branch[l], w_out[l], row(g_post[l]))

    mem_k, mem_v = _mem_kv(mem2, row(g_mem[l]), w_kv[l], tm=512)
    mem_k = mem_k.reshape(batch, N_MEM, D_X)
    mem_v = mem_v.reshape(batch, N_MEM, D_X)

    o_p, h_p, c_p, p_p = _prompt_mix(z_p, mem_k, mem_v, *mix_params,
                                     batch=batch, seq=seq, tm=MIX_TM)
    y_p = _branch_out(o_p, z_p, xp2, w_br_b, w_out_b, row(g_post[l]), tm=MIX_TM, unperm=perm.T)

    return (
        y_p.reshape(batch, seq, D_MODEL),
        y_s.reshape(nb, 1, D_MODEL),
        h_p.reshape(1, batch, D_RNN),
        c_p.reshape(1, batch, CONV_W - 1, D_RNN),
        p_p.reshape(1, batch, POOL_HIST, D_POOL),
        mem_k.reshape(1, batch, N_MEM, N_XHEADS, XHEAD_DIM),
        mem_v.reshape(1, batch, N_MEM, N_XHEADS, XHEAD_DIM),
        h_s.reshape(1, nb, D_RNN),
        c_s.transpose(1, 0, 2)[None],
        p_s.transpose(1, 0, 2)[None],
    )
```

```python
import functools

import jax
import jax.numpy as jnp
from jax import lax
from jax.experimental import pallas as pl
from jax.experimental.pallas import tpu as pltpu

D_MODEL = 2048
PAST_LEN = 16384
D_RNN = 1024
N_RNN_BLOCKS = 8
RNN_BLOCK = D_RNN // N_RNN_BLOCKS
CONV_W = 4
LRU_C = 8.0
D_POOL = 1024
POOL_WINDOWS = (2, 4, 8, 16)
POOL_GROUP = D_POOL // len(POOL_WINDOWS)
POOL_HIST = max(POOL_WINDOWS) - 1
N_MEM = 256
N_XHEADS = 4
XHEAD_DIM = 256
D_X = N_XHEADS * XHEAD_DIM
N_BRANCH = 3
D_MIX = D_RNN + D_POOL + D_X
D_IN = 2 * D_MIX + N_BRANCH * D_MODEL
EPS = 1e-6

SUBLANES = 8
LANES = 128
VMEM_LIMIT = 56 * 1024 * 1024
MIX_TM = 256

BF16 = jnp.bfloat16
F32 = jnp.float32

NEG_LOG2_E = -1.4426950408889634


def _sigmoid(x):
    return 1.0 / (1.0 + jnp.exp2(x * NEG_LOG2_E))


def _silu(x):
    return x * _sigmoid(x)


def _softplus(x):
    return jnp.maximum(x, 0.0) + jnp.log1p(jnp.exp(-jnp.abs(x)))


def _rms_scale(x):
    return lax.rsqrt(jnp.mean(x * x, axis=-1, keepdims=True) + EPS)


def _chunk_interleave():
    nrow = MIX_TM // SUBLANES
    p = jnp.arange(MIX_TM)
    token = (p % SUBLANES) * nrow + p // SUBLANES
    return (token[:, None] == jnp.arange(MIX_TM)[None, :]).astype(BF16)


def _sample_proj_kernel(x_ref, g_ref, w_ref, o_ref, wb_ref, u_ref):
    @pl.when(pl.program_id(0) == 0)
    def _():
        x = x_ref[...]
        u_ref[...] = (x * _rms_scale(x) * g_ref[...]).astype(BF16)

    w = w_ref[...].astype(BF16)
    wb_ref[...] = w
    o_ref[...] = jnp.dot(u_ref[...], w, preferred_element_type=F32)


def _sample_proj(x, g, w, tn):
    m, k = x.shape
    n = w.shape[1]
    return pl.pallas_call(
        _sample_proj_kernel,
        grid=(n // tn,),
        in_specs=[
            pl.BlockSpec((m, k), lambda j: (0, 0)),
            pl.BlockSpec((1, k), lambda j: (0, 0)),
            pl.BlockSpec((k, tn), lambda j: (0, j)),
        ],
        out_specs=[
            pl.BlockSpec((m, tn), lambda j: (0, j)),
            pl.BlockSpec((k, tn), lambda j: (0, j)),
        ],
        out_shape=[
            jax.ShapeDtypeStruct((m, n), F32),
            jax.ShapeDtypeStruct((k, n), BF16),
        ],
        scratch_shapes=[pltpu.VMEM((m, k), BF16)],
        compiler_params=pltpu.CompilerParams(
            dimension_semantics=("arbitrary",),
            vmem_limit_bytes=VMEM_LIMIT),
        name="sample_proj",
    )(x, g, w)


def _decay_rate(lam):
    return _softplus(-lam) * (LRU_C * NEG_LOG2_E)


def _rglru_block(xc, wax, ba, bx, rate):
    ri = jnp.dot(xc.astype(BF16), wax, preferred_element_type=F32)
    r = _sigmoid(ri[:, :RNN_BLOCK] + ba)
    i = _sigmoid(ri[:, RNN_BLOCK:] + bx)
    a = jnp.exp2(r * rate)
    one_m = 1.0 - a * a
    mult = jnp.where(one_m > 0.0, one_m * lax.rsqrt(one_m), 0.0)
    return a, mult * i * xc


def _prompt_mix_kernel(z_ref, k_ref, v_ref, convw_ref, convb_ref, wax_ref, ba_ref, bx_ref,
                       lam_ref, wpool_ref, pscale_ref,
                       o_ref, newh_ref, newconv_ref, newpool_ref,
                       conv_carry, pool_carry, h_carry, kb_ref, vb_ref, ac_scr, hl_scr, *, tm):
    l = pl.program_id(1)
    last = pl.num_programs(1) - 1
    nrow = tm // SUBLANES

    @pl.when(l == 0)
    def _():
        conv_carry[...] = jnp.zeros(conv_carry.shape, F32)
        pool_carry[...] = jnp.zeros(pool_carry.shape, F32)
        h_carry[...] = jnp.zeros(h_carry.shape, F32)
        kb_ref[...] = k_ref[0].astype(BF16)
        vb_ref[...] = v_ref[0].astype(BF16)

    chunk_id = lax.broadcasted_iota(jnp.int32, (SUBLANES, LANES), 0)
    first_chunk = chunk_id == 0

    def load_groups(col, width=LANES):
        return [z_ref[r * SUBLANES:(r + 1) * SUBLANES, col:col + width] for r in range(nrow)]

    def store_groups(col, rows, width=LANES):
        o_ref[:, col:col + width] = jnp.concatenate(rows, axis=0).astype(BF16)

    def history(tail_group, carry_ref, j, c0):
        tail = pltpu.roll(tail_group, 1, 0)
        prev = jnp.where(first_chunk, carry_ref[j - 1, :, c0:c0 + LANES], tail)
        carry_ref[j - 1, :, c0:c0 + LANES] = tail
        return prev

    sp = _decay_rate(lam_ref[...])
    for n in range(N_RNN_BLOCKS):
        c0, c1 = n * RNN_BLOCK, (n + 1) * RNN_BLOCK
        xs = load_groups(c0)
        ext = [history(xs[nrow - j], conv_carry, j, c0) for j in range(CONV_W - 1, 0, -1)] + xs
        cw = [jnp.broadcast_to(convw_ref[k:k + 1, c0:c1], (SUBLANES, LANES)) for k in range(CONV_W)]
        cb = jnp.broadcast_to(convb_ref[:, c0:c1], (SUBLANES, LANES))
        xc = []
        for r in range(nrow):
            acc = cb + cw[0] * ext[r]
            for k in range(1, CONV_W):
                acc = acc + cw[k] * ext[r + k]
            xc.append(acc)
        a, b = _rglru_block(jnp.concatenate(xc, axis=0), wax_ref[n], ba_ref[:, c0:c1],
                            bx_ref[:, c0:c1], sp[:, c0:c1])
        ac_scr[:, c0:c1] = a
        hl_scr[:, c0:c1] = b

    acc_a = ac_scr[0:SUBLANES, :]
    acc_h = hl_scr[0:SUBLANES, :]
    for r in range(1, nrow):
        rows = slice(r * SUBLANES, (r + 1) * SUBLANES)
        ar = ac_scr[rows, :]
        acc_h = ar * acc_h + hl_scr[rows, :]
        acc_a = ar * acc_a
        ac_scr[rows, :] = acc_a
        hl_scr[rows, :] = acc_h
    h_in = h_carry[...]
    entering = []
    for c in range(SUBLANES):
        entering.append(h_in)
        h_in = acc_a[c:c + 1] * h_in + acc_h[c:c + 1]
    h_carry[...] = h_in
    h_enter = jnp.concatenate(entering, axis=0)
    for n in range(N_RNN_BLOCKS):
        c0, c1 = n * RNN_BLOCK, (n + 1) * RNN_BLOCK
        gr = load_groups(D_RNN + c0)
        store_groups(c0, [(hl_scr[r * SUBLANES:(r + 1) * SUBLANES, c0:c1]
                           + ac_scr[r * SUBLANES:(r + 1) * SUBLANES, c0:c1] * h_enter[:, c0:c1])
                          * _silu(gr[r]) for r in range(nrow)])

    pcol = 2 * D_RNN
    blocks = [(w, c0) for g, w in enumerate(POOL_WINDOWS)
              for c0 in range(g * POOL_GROUP, (g + 1) * POOL_GROUP, LANES)]

    def group(c0, r):
        return z_ref[r * SUBLANES:(r + 1) * SUBLANES, pcol + c0:pcol + c0 + LANES]

    def mean_minus_token(tot, w, c0, r):
        if r < w - 1:
            pos1 = l * tm + chunk_id * nrow + (r + 1)
            mean = tot / jnp.minimum(pos1, w).astype(F32)
        else:
            mean = tot * (1.0 / w)
        return mean - group(c0, r)

    hist, tot = {}, {}
    for w, c0 in blocks:
        hist[c0] = [history(group(c0, nrow - j), pool_carry, j, c0) for j in range(1, w)]
        t = group(c0, 0)
        for h in hist[c0]:
            t = t + h
        tot[c0] = t
        hl_scr[0:SUBLANES, c0:c0 + LANES] = mean_minus_token(t, w, c0, 0)
    for r in range(1, nrow):
        for w, c0 in blocks:
            leaving = group(c0, r - w) if r >= w else hist[c0][w - r - 1]
            tot[c0] = tot[c0] + (group(c0, r) - leaving)
            hl_scr[r * SUBLANES:(r + 1) * SUBLANES, c0:c0 + LANES] = mean_minus_token(
                tot[c0], w, c0, r)
    for g, w in enumerate(POOL_WINDOWS):
        c0, c1 = g * POOL_GROUP, (g + 1) * POOL_GROUP
        og = jnp.dot(hl_scr[:, c0:c1].astype(BF16), wpool_ref[g], preferred_element_type=F32)
        gp = z_ref[:, pcol + D_POOL + c0:pcol + D_POOL + c1]
        o_ref[:, D_RNN + c0:D_RNN + c1] = (og * pscale_ref[:, c0:c1] * _silu(gp)).astype(BF16)

    qoff = 2 * D_RNN + 2 * D_POOL
    for hd in range(N_XHEADS):
        c0, c1 = hd * XHEAD_DIM, (hd + 1) * XHEAD_DIM
        q = z_ref[:, qoff + c0:qoff + c1].astype(BF16)
        s = lax.dot_general(q, kb_ref[:, c0:c1], (((1,), (1,)), ((), ())),
                            preferred_element_type=F32) * (XHEAD_DIM ** -0.5)
        p = jnp.exp(s - jnp.max(s, axis=-1, keepdims=True))
        p = p / jnp.sum(p, axis=-1, keepdims=True)
        ox = jnp.dot(p.astype(BF16), vb_ref[:, c0:c1], preferred_element_type=F32)
        gx = z_ref[:, qoff + D_X + c0:qoff + D_X + c1]
        o_ref[:, D_RNN + D_POOL + c0:D_RNN + D_POOL + c1] = (ox * _silu(gx)).astype(BF16)

    @pl.when(l == last)
    def _():
        newh_ref[0] = h_carry[...]
        tail_row = lambda j: (nrow - j) * SUBLANES + SUBLANES - 1
        for j in range(1, CONV_W):
            newconv_ref[0, CONV_W - 1 - j:CONV_W - j, :] = z_ref[tail_row(j):tail_row(j) + 1, 0:D_RNN]
        for j in range(1, POOL_HIST + 1):
            newpool_ref[0, POOL_HIST - j:POOL_HIST - j + 1, :] = (
                z_ref[tail_row(j):tail_row(j) + 1, pcol:pcol + D_POOL])


def _prompt_mix(z, mem_k, mem_v, conv_w, conv_b, wax, b_a, b_x, lam, wpool, pscale,
                batch, seq, tm):
    nl = seq // tm
    zw = 2 * D_MIX
    const2 = lambda b, l: (0, 0)
    const3 = lambda b, l: (0, 0, 0)
    kern = functools.partial(_prompt_mix_kernel, tm=tm)
    return pl.pallas_call(
        kern,
        grid=(batch, nl),
        in_specs=[
            pl.BlockSpec((tm, zw), lambda b, l: (b * nl + l, 0)),
            pl.BlockSpec((1, N_MEM, D_X), lambda b, l: (b, 0, 0)),
            pl.BlockSpec((1, N_MEM, D_X), lambda b, l: (b, 0, 0)),
            pl.BlockSpec((CONV_W, D_RNN), const2),
            pl.BlockSpec((1, D_RNN), const2),
            pl.BlockSpec((N_RNN_BLOCKS, RNN_BLOCK, 2 * RNN_BLOCK), const3),
            pl.BlockSpec((1, D_RNN), const2),
            pl.BlockSpec((1, D_RNN), const2),
            pl.BlockSpec((1, D_RNN), const2),
            pl.BlockSpec((len(POOL_WINDOWS), POOL_GROUP, POOL_GROUP), const3),
            pl.BlockSpec((1, D_POOL), const2),
        ],
        out_specs=[
            pl.BlockSpec((tm, D_MIX), lambda b, l: (b * nl + l, 0)),
            pl.BlockSpec((1, 1, D_RNN), lambda b, l: (b, 0, 0)),
            pl.BlockSpec((1, CONV_W - 1, D_RNN), lambda b, l: (b, 0, 0)),
            pl.BlockSpec((1, POOL_HIST, D_POOL), lambda b, l: (b, 0, 0)),
        ],
        out_shape=[
            jax.ShapeDtypeStruct((batch * seq, D_MIX), BF16),
            jax.ShapeDtypeStruct((batch, 1, D_RNN), F32),
            jax.ShapeDtypeStruct((batch, CONV_W - 1, D_RNN), F32),
            jax.ShapeDtypeStruct((batch, POOL_HIST, D_POOL), F32),
        ],
        scratch_shapes=[
            pltpu.VMEM((CONV_W - 1, SUBLANES, D_RNN), F32),
            pltpu.VMEM((POOL_HIST, SUBLANES, D_POOL), F32),
            pltpu.VMEM((1, D_RNN), F32),
            pltpu.VMEM((N_MEM, D_X), BF16),
            pltpu.VMEM((N_MEM, D_X), BF16),
            pltpu.VMEM((tm, D_RNN), F32),
            pltpu.VMEM((tm, D_RNN), F32),
        ],
        compiler_params=pltpu.CompilerParams(
            dimension_semantics=("arbitrary", "arbitrary"),
            vmem_limit_bytes=VMEM_LIMIT),
        name="prompt_mix",
    )(z, mem_k, mem_v, conv_w, conv_b, wax, b_a, b_x, lam, wpool, pscale)


def _cache_rows(c):
    nb = c.shape[0]
    c = c.reshape(nb, N_MEM, N_XHEADS, XHEAD_DIM // LANES, LANES)
    return c.transpose(0, 1, 3, 2, 4).reshape(nb, N_MEM * SUBLANES, LANES)


def _sample_attn_block(q_ref, k_ref, v_ref, o_ref, bb):
    halves = XHEAD_DIM // LANES
    assert halves * N_XHEADS == SUBLANES
    r = lax.broadcasted_iota(jnp.int32, (SUBLANES, LANES), 0)
    c = lax.broadcasted_iota(jnp.int32, (SUBLANES, LANES), 1)
    diag = (c % SUBLANES) == r
    first_half = r < N_XHEADS
    nchunk = N_MEM * SUBLANES // LANES
    scores = []
    for j in range(bb):
        qn = jnp.concatenate(
            [q_ref[j:j + 1, (h * halves + t) * LANES:(h * halves + t + 1) * LANES]
             for t in range(halves) for h in range(N_XHEADS)], axis=0)
        scores.append(lax.dot_general(qn.astype(BF16), k_ref[j].astype(BF16),
                                      (((1,), (1,)), ((), ())), preferred_element_type=F32)
                      * (XHEAD_DIM ** -0.5))
    probs = []
    for j in range(bb):
        s = scores[j]
        chunks = []
        for ci in range(nchunk):
            sm = jnp.where(diag, s[:, ci * LANES:(ci + 1) * LANES], 0.0)
            other = pltpu.roll(sm, N_XHEADS, 0)
            other = jnp.where(first_half, pltpu.roll(other, LANES - N_XHEADS, 1),
                              pltpu.roll(other, N_XHEADS, 1))
            chunks.append(jnp.where(diag, sm + other, -jnp.inf))
        t_full = jnp.concatenate(chunks, axis=1)
        e = jnp.exp(t_full - jnp.max(t_full, axis=1, keepdims=True))
        probs.append((e / jnp.sum(e, axis=1, keepdims=True)).astype(BF16))
    for j in range(bb):
        o = jnp.dot(probs[j], v_ref[j].astype(BF16), preferred_element_type=F32)
        for t in range(halves):
            for h in range(N_XHEADS):
                col = (h * halves + t) * LANES
                o_ref[j:j + 1, col:col + LANES] = o[t * N_XHEADS + h:t * N_XHEADS + h + 1, :]


def _prompt_proj_kernel(x_ref, g_ref, w_ref, perm_ref, o_ref, u_ref):
    @pl.when(pl.program_id(1) == 0)
    def _():
        x = x_ref[...]
        u = (x * _rms_scale(x) * g_ref[...]).astype(BF16)
        for r0 in range(0, u.shape[0], MIX_TM):
            u_ref[r0:r0 + MIX_TM, :] = jnp.dot(
                perm_ref[...], u[r0:r0 + MIX_TM], preferred_element_type=F32).astype(BF16)

    o_ref[...] = jnp.dot(u_ref[...], w_ref[...], preferred_element_type=F32)


def _prompt_proj(x, g, w, perm, tm, tn):
    m, k = x.shape
    n = w.shape[1]
    return pl.pallas_call(
        _prompt_proj_kernel,
        grid=(m // tm, n // tn),
        in_specs=[
            pl.BlockSpec((tm, k), lambda i, j: (i, 0)),
            pl.BlockSpec((1, k), lambda i, j: (0, 0)),
            pl.BlockSpec((k, tn), lambda i, j: (0, j)),
            pl.BlockSpec(perm.shape, lambda i, j: (0, 0)),
        ],
        out_specs=pl.BlockSpec((tm, tn), lambda i, j: (i, j)),
        out_shape=jax.ShapeDtypeStruct((m, n), F32),
        scratch_shapes=[pltpu.VMEM((tm, k), BF16)],
        compiler_params=pltpu.CompilerParams(
            dimension_semantics=("arbitrary", "arbitrary"),
            vmem_limit_bytes=VMEM_LIMIT),
        name="prompt_proj",
    )(x, g, w, perm)


def _sample_attn_kernel(q_ref, k_ref, v_ref, o_ref, *, bb):
    _sample_attn_block(q_ref, k_ref, v_ref, o_ref, bb)


def _sample_attn(z, cache_k, cache_v, bb):
    nb = z.shape[0]
    qblk = (2 * D_RNN + 2 * D_POOL) // D_X
    return pl.pallas_call(
        functools.partial(_sample_attn_kernel, bb=bb),
        grid=(nb // bb,),
        in_specs=[
            pl.BlockSpec((bb, D_X), lambda i: (i, qblk)),
            pl.BlockSpec((bb, N_MEM * SUBLANES, LANES), lambda i: (i, 0, 0)),
            pl.BlockSpec((bb, N_MEM * SUBLANES, LANES), lambda i: (i, 0, 0)),
        ],
        out_specs=pl.BlockSpec((bb, D_X), lambda i: (i, 0)),
        out_shape=jax.ShapeDtypeStruct((nb, D_X), F32),
        compiler_params=pltpu.CompilerParams(
            dimension_semantics=("arbitrary",),
            vmem_limit_bytes=VMEM_LIMIT),
        name="sample_attn",
    )(z, cache_k, cache_v)


def _sample_mix_kernel(z_ref, attn_ref, conv_ref, h_ref, pool_ref,
                       convw_ref, convb_ref, wax_ref, ba_ref, bx_ref, lam_ref, wpool_ref,
                       pscale_ref, o_ref, newh_ref, newconv_ref, newpool_ref):
    xr = z_ref[:, 0:D_RNN]
    xc = convb_ref[...] + convw_ref[CONV_W - 1:CONV_W, :] * xr
    for k in range(CONV_W - 1):
        xc = xc + convw_ref[k:k + 1, :] * conv_ref[k]
    for k in range(CONV_W - 2):
        newconv_ref[k] = conv_ref[k + 1]
    newconv_ref[CONV_W - 2] = xr

    sp = _decay_rate(lam_ref[...])
    for n in range(N_RNN_BLOCKS):
        c0, c1 = n * RNN_BLOCK, (n + 1) * RNN_BLOCK
        a, b = _rglru_block(xc[:, c0:c1], wax_ref[n], ba_ref[:, c0:c1], bx_ref[:, c0:c1],
                            sp[:, c0:c1])
        h = a * h_ref[:, c0:c1] + b
        newh_ref[:, c0:c1] = h
        o_ref[:, c0:c1] = (h * _silu(z_ref[:, D_RNN + c0:D_RNN + c1])).astype(BF16)

    xp = z_ref[:, 2 * D_RNN:2 * D_RNN + D_POOL]
    for k in range(POOL_HIST - 1):
        newpool_ref[k] = pool_ref[k + 1]
    newpool_ref[POOL_HIST - 1] = xp
    for g, w in enumerate(POOL_WINDOWS):
        c0, c1 = g * POOL_GROUP, (g + 1) * POOL_GROUP
        xg = xp[:, c0:c1]
        tot = xg
        for j in range(1, w):
            tot = tot + pool_ref[POOL_HIST - j, :, c0:c1]
        cnt = float(min(PAST_LEN + 1, w))
        d = tot / cnt - xg
        og = jnp.dot(d.astype(BF16), wpool_ref[g], preferred_element_type=F32)
        gp = z_ref[:, 2 * D_RNN + D_POOL + c0:2 * D_RNN + D_POOL + c1]
        o_ref[:, D_RNN + c0:D_RNN + c1] = (og * pscale_ref[:, c0:c1] * _silu(gp)).astype(BF16)

    gx = z_ref[:, 2 * D_RNN + 2 * D_POOL + D_X:2 * D_MIX]
    o_ref[:, D_RNN + D_POOL:] = (attn_ref[...] * _silu(gx)).astype(BF16)


def _sample_mix(z, attn, conv, h, pool, conv_w, conv_b, wax, b_a, b_x, lam, wpool, pscale, tb):
    nb = z.shape[0]
    zw = 2 * D_MIX
    rows = lambda i: (i, 0)
    const2 = lambda i: (0, 0)
    const3 = lambda i: (0, 0, 0)
    hist = lambda i: (0, i, 0)
    return pl.pallas_call(
        _sample_mix_kernel,
        grid=(nb // tb,),
        in_specs=[
            pl.BlockSpec((tb, zw), rows),
            pl.BlockSpec((tb, D_X), rows),
            pl.BlockSpec((CONV_W - 1, tb, D_RNN), hist),
            pl.BlockSpec((tb, D_RNN), rows),
            pl.BlockSpec((POOL_HIST, tb, D_POOL), hist),
            pl.BlockSpec((CONV_W, D_RNN), const2),
            pl.BlockSpec((1, D_RNN), const2),
            pl.BlockSpec((N_RNN_BLOCKS, RNN_BLOCK, 2 * RNN_BLOCK), const3),
            pl.BlockSpec((1, D_RNN), const2),
            pl.BlockSpec((1, D_RNN), const2),
            pl.BlockSpec((1, D_RNN), const2),
            pl.BlockSpec((len(POOL_WINDOWS), POOL_GROUP, POOL_GROUP), const3),
            pl.BlockSpec((1, D_POOL), const2),
        ],
        out_specs=[
            pl.BlockSpec((tb, D_MIX), rows),
            pl.BlockSpec((tb, D_RNN), rows),
            pl.BlockSpec((CONV_W - 1, tb, D_RNN), hist),
            pl.BlockSpec((POOL_HIST, tb, D_POOL), hist),
        ],
        out_shape=[
            jax.ShapeDtypeStruct((nb, D_MIX), BF16),
            jax.ShapeDtypeStruct((nb, D_RNN), F32),
            jax.ShapeDtypeStruct((CONV_W - 1, nb, D_RNN), F32),
            jax.ShapeDtypeStruct((POOL_HIST, nb, D_POOL), F32),
        ],
        compiler_params=pltpu.CompilerParams(
            dimension_semantics=("arbitrary",),
            vmem_limit_bytes=VMEM_LIMIT),
        name="sample_mix",
    )(z, attn, conv, h, pool, conv_w, conv_b, wax, b_a, b_x, lam, wpool, pscale)


def _branch_out_kernel(o_ref, gates_ref, x_ref, wb_ref, wo_ref, gpost_ref, *rest, interleaved):
    y_ref = rest[-1]
    merged = None
    for j, (r0, r1) in enumerate(((0, D_RNN), (D_RNN, D_RNN + D_POOL), (D_RNN + D_POOL, D_MIX))):
        yj = jnp.dot(o_ref[:, r0:r1], wb_ref[r0:r1, :], preferred_element_type=F32)
        term = _sigmoid(gates_ref[:, j * D_MODEL:(j + 1) * D_MODEL]) * yj
        merged = term if merged is None else merged + term
    merged = merged.astype(BF16)
    if interleaved:
        merged = jnp.dot(rest[0][...], merged, preferred_element_type=F32).astype(BF16)
    out = jnp.dot(merged, wo_ref[...], preferred_element_type=F32)
    y_ref[...] = x_ref[...] + (out * gpost_ref[...]) * _rms_scale(out)


def _branch_out(o, z, x, wb, wo, g_post, tm, unperm=None):
    m = x.shape[0]
    gw = N_BRANCH * D_MODEL
    gblk = (2 * D_MIX) // gw
    resident = pl.Buffered(1)
    in_specs = [
        pl.BlockSpec((tm, D_MIX), lambda i: (i, 0)),
        pl.BlockSpec((tm, gw), lambda i: (i, gblk)),
        pl.BlockSpec((tm, D_MODEL), lambda i: (i, 0)),
        pl.BlockSpec((D_MIX, D_MODEL), lambda i: (0, 0), pipeline_mode=resident),
        pl.BlockSpec((D_MODEL, D_MODEL), lambda i: (0, 0), pipeline_mode=resident),
        pl.BlockSpec((1, D_MODEL), lambda i: (0, 0)),
    ]
    args = [o, z, x, wb, wo, g_post]
    if unperm is not None:
        assert unperm.shape == (tm, tm)
        in_specs.append(pl.BlockSpec(unperm.shape, lambda i: (0, 0)))
        args.append(unperm)
    return pl.pallas_call(
        functools.partial(_branch_out_kernel, interleaved=unperm is not None),
        grid=(m // tm,),
        in_specs=in_specs,
        out_specs=pl.BlockSpec((tm, D_MODEL), lambda i: (i, 0)),
        out_shape=jax.ShapeDtypeStruct((m, D_MODEL), F32),
        compiler_params=pltpu.CompilerParams(
            dimension_semantics=("arbitrary",),
            vmem_limit_bytes=VMEM_LIMIT),
        name="branch_out",
    )(*args)


WROWS = 512
PER_BRANCH = D_RNN // WROWS
assert D_RNN == D_POOL == D_X and D_RNN % WROWS == 0 and D_MODEL % WROWS == 0
N_WB_BLOCKS = N_BRANCH * PER_BRANCH
N_WOUT_BLOCKS = D_MODEL // WROWS


def _branch_out_cast_kernel(o_ref, gates_ref, x_ref, wb_ref, wo_ref, gpost_ref,
                            y_ref, wbb_ref, wob_ref, merged_ref, out_ref):
    s = pl.program_id(0)

    @pl.when(s < N_WB_BLOCKS)
    def _():
        w = wb_ref[...].astype(BF16)
        wbb_ref[...] = w
        term = _sigmoid(gates_ref[...]) * jnp.dot(o_ref[...], w, preferred_element_type=F32)

        @pl.when(s == 0)
        def _():
            merged_ref[...] = term

        @pl.when(s > 0)
        def _():
            merged_ref[...] += term

    for kb in range(N_WOUT_BLOCKS):
        @pl.when(s == N_WB_BLOCKS + kb)
        def _(kb=kb):
            w = wo_ref[...].astype(BF16)
            wob_ref[...] = w
            part = jnp.dot(merged_ref[:, kb * WROWS:(kb + 1) * WROWS].astype(BF16), w,
                           preferred_element_type=F32)
            if kb == 0:
                out_ref[...] = part
            else:
                out_ref[...] += part

    @pl.when(s == N_WB_BLOCKS + N_WOUT_BLOCKS - 1)
    def _():
        out = out_ref[...]
        y_ref[...] = x_ref[...] + out * _rms_scale(out) * gpost_ref[...]


def _branch_out_cast(o, z, x, wb, wo, g_post):
    m = x.shape[0]
    gblk0 = (2 * D_MIX) // D_MODEL
    wb_blk = lambda s: jnp.minimum(s, N_WB_BLOCKS - 1)
    wo_blk = lambda s: jnp.maximum(s - N_WB_BLOCKS, 0)
    return pl.pallas_call(
        _branch_out_cast_kernel,
        grid=(N_WB_BLOCKS + N_WOUT_BLOCKS,),
        in_specs=[
            pl.BlockSpec((m, WROWS), lambda s: (0, wb_blk(s))),
            pl.BlockSpec((m, D_MODEL), lambda s: (0, gblk0 + wb_blk(s) // PER_BRANCH)),
            pl.BlockSpec((m, D_MODEL), lambda s: (0, 0)),
            pl.BlockSpec((WROWS, D_MODEL), lambda s: (wb_blk(s), 0)),
            pl.BlockSpec((WROWS, D_MODEL), lambda s: (wo_blk(s), 0)),
            pl.BlockSpec((1, D_MODEL), lambda s: (0, 0)),
        ],
        out_specs=[
            pl.BlockSpec((m, D_MODEL), lambda s: (0, 0)),
            pl.BlockSpec((WROWS, D_MODEL), lambda s: (wb_blk(s), 0)),
            pl.BlockSpec((WROWS, D_MODEL), lambda s: (wo_blk(s), 0)),
        ],
        out_shape=[
            jax.ShapeDtypeStruct((m, D_MODEL), F32),
            jax.ShapeDtypeStruct(wb.shape, BF16),
            jax.ShapeDtypeStruct(wo.shape, BF16),
        ],
        scratch_shapes=[pltpu.VMEM((m, D_MODEL), F32), pltpu.VMEM((m, D_MODEL), F32)],
        compiler_params=pltpu.CompilerParams(
            dimension_semantics=("arbitrary",),
            vmem_limit_bytes=VMEM_LIMIT),
        name="branch_out_cast",
    )(o, z, x, wb, wo, g_post)


def _mem_kv_kernel(x_ref, g_ref, w_ref, k_ref, v_ref, u_ref):
    j = pl.program_id(1)

    @pl.when(j == 0)
    def _():
        x = x_ref[...]
        u_ref[...] = (x * _rms_scale(x) * g_ref[...]).astype(BF16)

    res = jnp.dot(u_ref[...], w_ref[...].astype(BF16), preferred_element_type=F32)

    @pl.when(j == 0)
    def _():
        k_ref[...] = res

    @pl.when(j == 1)
    def _():
        v_ref[...] = res


def _mem_kv(x, g, w, tm):
    m, k = x.shape
    assert w.shape[1] == 2 * D_X
    half = pl.BlockSpec((tm, D_X), lambda i, j: (i, 0))
    return pl.pallas_call(
        _mem_kv_kernel,
        grid=(m // tm, 2),
        in_specs=[
            pl.BlockSpec((tm, k), lambda i, j: (i, 0)),
            pl.BlockSpec((1, k), lambda i, j: (0, 0)),
            pl.BlockSpec((k, D_X), lambda i, j: (0, j)),
        ],
        out_specs=[half, half],
        out_shape=[jax.ShapeDtypeStruct((m, D_X), F32)] * 2,
        scratch_shapes=[pltpu.VMEM((tm, k), BF16)],
        compiler_params=pltpu.CompilerParams(
            dimension_semantics=("arbitrary", "arbitrary"),
            vmem_limit_bytes=VMEM_LIMIT),
        name="mem_kv",
    )(x, g, w)


def kernel(x_prompt, x_sample, mem_prompt, state_rglru_h, state_conv, state_pool, cache_mem_k, cache_mem_v, g_pre, w_in, conv_w, conv_b, w_rg_a, b_rg_a, w_rg_x, b_rg_x, lru_lambda, w_pool, pool_scale, g_mem, w_kv, w_branch, w_out, g_post):
    batch, seq, _ = x_prompt.shape
    nb = x_sample.shape[0]
    depth = g_pre.shape[0]
    assert depth == 1 and x_sample.shape[1] == 1

    l = 0
    row = lambda v: v.reshape(1, -1)
    wax = jnp.concatenate([w_rg_a[l], w_rg_x[l]], axis=-1).astype(BF16)
    wpool = w_pool[l].astype(BF16)
    mix_params = (conv_w[l], row(conv_b[l]), wax, row(b_rg_a[l]), row(b_rg_x[l]),
                  row(lru_lambda[l]), wpool, row(pool_scale[l]))

    xp2 = x_prompt.reshape(batch * seq, D_MODEL)
    xs2 = x_sample.reshape(nb, D_MODEL)
    mem2 = mem_prompt.reshape(batch * N_MEM, D_MODEL)

    z_s, w_in_b = _sample_proj(xs2, row(g_pre[l]), w_in[l], tn=1024)
    attn_s = _sample_attn(z_s, _cache_rows(cache_mem_k[l]), _cache_rows(cache_mem_v[l]), bb=8)

    o_s, h_s, c_s, p_s = _sample_mix(
        z_s, attn_s, state_conv[l].transpose(1, 0, 2), state_rglru_h[l],
        state_pool[l].transpose(1, 0, 2), *mix_params, tb=32)
    y_s, w_br_b, w_out_b = _branch_out_cast(o_s, z_s, xs2, w_branch[l], w_out[l], row(g_post[l]))

    mem_k, mem_v = _mem_kv(mem2, row(g_mem[l]), w_kv[l], tm=512)
    mem_k = mem_k.reshape(batch, N_MEM, D_X)
    mem_v = mem_v.reshape(batch, N_MEM, D_X)

    perm = _chunk_interleave()
    z_p = _prompt_proj(xp2, row(g_pre[l]), w_in_b, perm, tm=1024, tn=1536)
    o_p, h_p, c_p, p_p = _prompt_mix(z_p, mem_k, mem_v, *mix_params,
                                     batch=batch, seq=seq, tm=MIX_TM)
    y_p = _branch_out(o_p, z_p, xp2, w_br_b, w_out_b, row(g_post[l]), tm=MIX_TM, unperm=perm.T)

    return (
        y_p.reshape(batch, seq, D_MODEL),
        y_s.reshape(nb, 1, D_MODEL),
        h_p.reshape(1, batch, D_RNN),
        c_p.reshape(1, batch, CONV_W - 1, D_RNN),
        p_p.reshape(1, batch, POOL_HIST, D_POOL),
        mem_k.reshape(1, batch, N_MEM, N_XHEADS, XHEAD_DIM),
        mem_v.reshape(1, batch, N_MEM, N_XHEADS, XHEAD_DIM),
        h_s.reshape(1, nb, D_RNN),
        c_s.transpose(1, 0, 2)[None],
        p_s.transpose(1, 0, 2)[None],
    )
```

```python
import functools

import jax
import jax.numpy as jnp
from jax import lax
from jax.experimental import pallas as pl
from jax.experimental.pallas import tpu as pltpu

D_MODEL = 2048
PAST_LEN = 16384
D_RNN = 1024
N_RNN_BLOCKS = 8
RNN_BLOCK = D_RNN // N_RNN_BLOCKS
CONV_W = 4
LRU_C = 8.0
D_POOL = 1024
POOL_WINDOWS = (2, 4, 8, 16)
POOL_GROUP = D_POOL // len(POOL_WINDOWS)
POOL_HIST = max(POOL_WINDOWS) - 1
N_MEM = 256
N_XHEADS = 4
XHEAD_DIM = 256
D_X = N_XHEADS * XHEAD_DIM
N_BRANCH = 3
D_MIX = D_RNN + D_POOL + D_X
D_IN = 2 * D_MIX + N_BRANCH * D_MODEL
EPS = 1e-6

SUBLANES = 8
LANES = 128
VMEM_LIMIT = 56 * 1024 * 1024
PROJ_VMEM_LIMIT = 60 * 1024 * 1024
MIX_TM = 256

BF16 = jnp.bfloat16
F32 = jnp.float32

NEG_LOG2_E = -1.4426950408889634


def _sigmoid(x):
    return 1.0 / (1.0 + jnp.exp2(x * NEG_LOG2_E))


def _silu(x):
    return x * _sigmoid(x)


def _softplus(x):
    return jnp.maximum(x, 0.0) + jnp.log1p(jnp.exp(-jnp.abs(x)))


def _rms_scale(x):
    return lax.rsqrt(jnp.mean(x * x, axis=-1, keepdims=True) + EPS)


def _chunk_interleave():
    nrow = MIX_TM // SUBLANES
    p = jnp.arange(MIX_TM)
    token = (p % SUBLANES) * nrow + p // SUBLANES
    return (token[:, None] == jnp.arange(MIX_TM)[None, :]).astype(BF16)


def _sample_proj_kernel(x_ref, g_ref, w_ref, o_ref, wb_ref, u_ref):
    @pl.when(pl.program_id(0) == 0)
    def _():
        x = x_ref[...]
        u_ref[...] = (x * _rms_scale(x) * g_ref[...]).astype(BF16)

    w = w_ref[...].astype(BF16)
    wb_ref[...] = w
    o_ref[...] = jnp.dot(u_ref[...], w, preferred_element_type=F32)


def _sample_proj(x, g, w, tn):
    m, k = x.shape
    n = w.shape[1]
    return pl.pallas_call(
        _sample_proj_kernel,
        grid=(n // tn,),
        in_specs=[
            pl.BlockSpec((m, k), lambda j: (0, 0)),
            pl.BlockSpec((1, k), lambda j: (0, 0)),
            pl.BlockSpec((k, tn), lambda j: (0, j)),
        ],
        out_specs=[
            pl.BlockSpec((m, tn), lambda j: (0, j)),
            pl.BlockSpec((k, tn), lambda j: (0, j)),
        ],
        out_shape=[
            jax.ShapeDtypeStruct((m, n), F32),
            jax.ShapeDtypeStruct((k, n), BF16),
        ],
        scratch_shapes=[pltpu.VMEM((m, k), BF16)],
        compiler_params=pltpu.CompilerParams(
            dimension_semantics=("arbitrary",),
            vmem_limit_bytes=VMEM_LIMIT),
        name="sample_proj",
    )(x, g, w)


def _decay_rate(lam):
    return _softplus(-lam) * (LRU_C * NEG_LOG2_E)


def _rglru_block(xc, wax, ba, bx, rate):
    ri = jnp.dot(xc.astype(BF16), wax, preferred_element_type=F32)
    r = _sigmoid(ri[:, :RNN_BLOCK] + ba)
    i = _sigmoid(ri[:, RNN_BLOCK:] + bx)
    a = jnp.exp2(r * rate)
    one_m = 1.0 - a * a
    mult = jnp.where(one_m > 0.0, one_m * lax.rsqrt(one_m), 0.0)
    return a, mult * i * xc


def _prompt_mix_kernel(z_ref, k_ref, v_ref, convw_ref, convb_ref, wax_ref, ba_ref, bx_ref,
                       lam_ref, wpool_ref, pscale_ref,
                       o_ref, newh_ref, newconv_ref, newpool_ref,
                       conv_carry, pool_carry, h_carry, kb_ref, vb_ref, ac_scr, hl_scr, *, tm):
    l = pl.program_id(1)
    last = pl.num_programs(1) - 1
    nrow = tm // SUBLANES

    @pl.when(l == 0)
    def _():
        conv_carry[...] = jnp.zeros(conv_carry.shape, F32)
        pool_carry[...] = jnp.zeros(pool_carry.shape, F32)
        h_carry[...] = jnp.zeros(h_carry.shape, F32)
        kb_ref[...] = k_ref[0].astype(BF16)
        vb_ref[...] = v_ref[0].astype(BF16)

    chunk_id = lax.broadcasted_iota(jnp.int32, (SUBLANES, LANES), 0)
    first_chunk = chunk_id == 0

    def load_groups(col, width=LANES):
        return [z_ref[r * SUBLANES:(r + 1) * SUBLANES, col:col + width] for r in range(nrow)]

    def store_groups(col, rows, width=LANES):
        o_ref[:, col:col + width] = jnp.concatenate(rows, axis=0).astype(BF16)

    def history(tail_group, carry_ref, j, c0):
        tail = pltpu.roll(tail_group, 1, 0)
        prev = jnp.where(first_chunk, carry_ref[j - 1, :, c0:c0 + LANES], tail)
        carry_ref[j - 1, :, c0:c0 + LANES] = tail
        return prev

    sp = _decay_rate(lam_ref[...])
    for n in range(N_RNN_BLOCKS):
        c0, c1 = n * RNN_BLOCK, (n + 1) * RNN_BLOCK
        xs = load_groups(c0)
        ext = [history(xs[nrow - j], conv_carry, j, c0) for j in range(CONV_W - 1, 0, -1)] + xs
        cw = [jnp.broadcast_to(convw_ref[k:k + 1, c0:c1], (SUBLANES, LANES)) for k in range(CONV_W)]
        cb = jnp.broadcast_to(convb_ref[:, c0:c1], (SUBLANES, LANES))
        xc = []
        for r in range(nrow):
            acc = cb + cw[0] * ext[r]
            for k in range(1, CONV_W):
                acc = acc + cw[k] * ext[r + k]
            xc.append(acc)
        a, b = _rglru_block(jnp.concatenate(xc, axis=0), wax_ref[n], ba_ref[:, c0:c1],
                            bx_ref[:, c0:c1], sp[:, c0:c1])
        ac_scr[:, c0:c1] = a
        hl_scr[:, c0:c1] = b

    acc_a = ac_scr[0:SUBLANES, :]
    acc_h = hl_scr[0:SUBLANES, :]
    for r in range(1, nrow):
        rows = slice(r * SUBLANES, (r + 1) * SUBLANES)
        ar = ac_scr[rows, :]
        acc_h = ar * acc_h + hl_scr[rows, :]
        acc_a = ar * acc_a
        ac_scr[rows, :] = acc_a
        hl_scr[rows, :] = acc_h
    h_in = h_carry[...]
    entering = []
    for c in range(SUBLANES):
        entering.append(h_in)
        h_in = acc_a[c:c + 1] * h_in + acc_h[c:c + 1]
    h_carry[...] = h_in
    h_enter = jnp.concatenate(entering, axis=0)
    for n in range(N_RNN_BLOCKS):
        c0, c1 = n * RNN_BLOCK, (n + 1) * RNN_BLOCK
        gr = load_groups(D_RNN + c0)
        store_groups(c0, [(hl_scr[r * SUBLANES:(r + 1) * SUBLANES, c0:c1]
                           + ac_scr[r * SUBLANES:(r + 1) * SUBLANES, c0:c1] * h_enter[:, c0:c1])
                          * _silu(gr[r]) for r in range(nrow)])

    pcol = 2 * D_RNN
    blocks = [(w, c0) for g, w in enumerate(POOL_WINDOWS)
              for c0 in range(g * POOL_GROUP, (g + 1) * POOL_GROUP, LANES)]

    def group(c0, r):
        return z_ref[r * SUBLANES:(r + 1) * SUBLANES, pcol + c0:pcol + c0 + LANES]

    def mean_minus_token(tot, w, c0, r):
        if r < w - 1:
            pos1 = l * tm + chunk_id * nrow + (r + 1)
            mean = tot / jnp.minimum(pos1, w).astype(F32)
        else:
            mean = tot * (1.0 / w)
        return mean - group(c0, r)

    hist, tot = {}, {}
    for w, c0 in blocks:
        hist[c0] = [history(group(c0, nrow - j), pool_carry, j, c0) for j in range(1, w)]
        t = group(c0, 0)
        for h in hist[c0]:
            t = t + h
        tot[c0] = t
        hl_scr[0:SUBLANES, c0:c0 + LANES] = mean_minus_token(t, w, c0, 0)
    for r in range(1, nrow):
        for w, c0 in blocks:
            leaving = group(c0, r - w) if r >= w else hist[c0][w - r - 1]
            tot[c0] = tot[c0] + (group(c0, r) - leaving)
            hl_scr[r * SUBLANES:(r + 1) * SUBLANES, c0:c0 + LANES] = mean_minus_token(
                tot[c0], w, c0, r)
    for g, w in enumerate(POOL_WINDOWS):
        c0, c1 = g * POOL_GROUP, (g + 1) * POOL_GROUP
        og = jnp.dot(hl_scr[:, c0:c1].astype(BF16), wpool_ref[g], preferred_element_type=F32)
        gp = z_ref[:, pcol + D_POOL + c0:pcol + D_POOL + c1]
        o_ref[:, D_RNN + c0:D_RNN + c1] = (og * pscale_ref[:, c0:c1] * _silu(gp)).astype(BF16)

    qoff = 2 * D_RNN + 2 * D_POOL
    for hd in range(N_XHEADS):
        c0, c1 = hd * XHEAD_DIM, (hd + 1) * XHEAD_DIM
        q = z_ref[:, qoff + c0:qoff + c1].astype(BF16)
        s = lax.dot_general(q, kb_ref[:, c0:c1], (((1,), (1,)), ((), ())),
                            preferred_element_type=F32) * (XHEAD_DIM ** -0.5)
        p = jnp.exp(s - jnp.max(s, axis=-1, keepdims=True))
        p = p / jnp.sum(p, axis=-1, keepdims=True)
        ox = jnp.dot(p.astype(BF16), vb_ref[:, c0:c1], preferred_element_type=F32)
        gx = z_ref[:, qoff + D_X + c0:qoff + D_X + c1]
        o_ref[:, D_RNN + D_POOL + c0:D_RNN + D_POOL + c1] = (ox * _silu(gx)).astype(BF16)

    @pl.when(l == last)
    def _():
        newh_ref[0] = h_carry[...]
        tail_row = lambda j: (nrow - j) * SUBLANES + SUBLANES - 1
        for j in range(1, CONV_W):
            newconv_ref[0, CONV_W - 1 - j:CONV_W - j, :] = z_ref[tail_row(j):tail_row(j) + 1, 0:D_RNN]
        for j in range(1, POOL_HIST + 1):
            newpool_ref[0, POOL_HIST - j:POOL_HIST - j + 1, :] = (
                z_ref[tail_row(j):tail_row(j) + 1, pcol:pcol + D_POOL])


def _prompt_mix(z, mem_k, mem_v, conv_w, conv_b, wax, b_a, b_x, lam, wpool, pscale,
                batch, seq, tm):
    nl = seq // tm
    zw = 2 * D_MIX
    const2 = lambda b, l: (0, 0)
    const3 = lambda b, l: (0, 0, 0)
    kern = functools.partial(_prompt_mix_kernel, tm=tm)
    return pl.pallas_call(
        kern,
        grid=(batch, nl),
        in_specs=[
            pl.BlockSpec((tm, zw), lambda b, l: (b * nl + l, 0)),
            pl.BlockSpec((1, N_MEM, D_X), lambda b, l: (b, 0, 0)),
            pl.BlockSpec((1, N_MEM, D_X), lambda b, l: (b, 0, 0)),
            pl.BlockSpec((CONV_W, D_RNN), const2),
            pl.BlockSpec((1, D_RNN), const2),
            pl.BlockSpec((N_RNN_BLOCKS, RNN_BLOCK, 2 * RNN_BLOCK), const3),
            pl.BlockSpec((1, D_RNN), const2),
            pl.BlockSpec((1, D_RNN), const2),
            pl.BlockSpec((1, D_RNN), const2),
            pl.BlockSpec((len(POOL_WINDOWS), POOL_GROUP, POOL_GROUP), const3),
            pl.BlockSpec((1, D_POOL), const2),
        ],
        out_specs=[
            pl.BlockSpec((tm, D_MIX), lambda b, l: (b * nl + l, 0)),
            pl.BlockSpec((1, 1, D_RNN), lambda b, l: (b, 0, 0)),
            pl.BlockSpec((1, CONV_W - 1, D_RNN), lambda b, l: (b, 0, 0)),
            pl.BlockSpec((1, POOL_HIST, D_POOL), lambda b, l: (b, 0, 0)),
        ],
        out_shape=[
            jax.ShapeDtypeStruct((batch * seq, D_MIX), BF16),
            jax.ShapeDtypeStruct((batch, 1, D_RNN), F32),
            jax.ShapeDtypeStruct((batch, CONV_W - 1, D_RNN), F32),
            jax.ShapeDtypeStruct((batch, POOL_HIST, D_POOL), F32),
        ],
        scratch_shapes=[
            pltpu.VMEM((CONV_W - 1, SUBLANES, D_RNN), F32),
            pltpu.VMEM((POOL_HIST, SUBLANES, D_POOL), F32),
            pltpu.VMEM((1, D_RNN), F32),
            pltpu.VMEM((N_MEM, D_X), BF16),
            pltpu.VMEM((N_MEM, D_X), BF16),
            pltpu.VMEM((tm, D_RNN), F32),
            pltpu.VMEM((tm, D_RNN), F32),
        ],
        compiler_params=pltpu.CompilerParams(
            dimension_semantics=("arbitrary", "arbitrary"),
            vmem_limit_bytes=VMEM_LIMIT),
        name="prompt_mix",
    )(z, mem_k, mem_v, conv_w, conv_b, wax, b_a, b_x, lam, wpool, pscale)


def _cache_rows(c):
    nb = c.shape[0]
    c = c.reshape(nb, N_MEM, N_XHEADS, XHEAD_DIM // LANES, LANES)
    return c.transpose(0, 1, 3, 2, 4).reshape(nb, N_MEM * SUBLANES, LANES)


def _sample_attn_block(q_ref, k_ref, v_ref, o_ref, bb):
    halves = XHEAD_DIM // LANES
    assert halves * N_XHEADS == SUBLANES
    r = lax.broadcasted_iota(jnp.int32, (SUBLANES, LANES), 0)
    c = lax.broadcasted_iota(jnp.int32, (SUBLANES, LANES), 1)
    diag = (c % SUBLANES) == r
    first_half = r < N_XHEADS
    nchunk = N_MEM * SUBLANES // LANES
    scores = []
    for j in range(bb):
        qn = jnp.concatenate(
            [q_ref[j:j + 1, (h * halves + t) * LANES:(h * halves + t + 1) * LANES]
             for t in range(halves) for h in range(N_XHEADS)], axis=0)
        scores.append(lax.dot_general(qn.astype(BF16), k_ref[j].astype(BF16),
                                      (((1,), (1,)), ((), ())), preferred_element_type=F32)
                      * (XHEAD_DIM ** -0.5))
    probs = []
    for j in range(bb):
        s = scores[j]
        chunks = []
        for ci in range(nchunk):
            sm = jnp.where(diag, s[:, ci * LANES:(ci + 1) * LANES], 0.0)
            other = pltpu.roll(sm, N_XHEADS, 0)
            other = jnp.where(first_half, pltpu.roll(other, LANES - N_XHEADS, 1),
                              pltpu.roll(other, N_XHEADS, 1))
            chunks.append(jnp.where(diag, sm + other, -jnp.inf))
        t_full = jnp.concatenate(chunks, axis=1)
        e = jnp.exp(t_full - jnp.max(t_full, axis=1, keepdims=True))
        probs.append((e / jnp.sum(e, axis=1, keepdims=True)).astype(BF16))
    for j in range(bb):
        o = jnp.dot(probs[j], v_ref[j].astype(BF16), preferred_element_type=F32)
        for t in range(halves):
            for h in range(N_XHEADS):
                col = (h * halves + t) * LANES
                o_ref[j:j + 1, col:col + LANES] = o[t * N_XHEADS + h:t * N_XHEADS + h + 1, :]


def _prompt_proj_kernel(x_ref, g_ref, w_ref, perm_ref, o_ref, u_ref):
    @pl.when(pl.program_id(1) == 0)
    def _():
        x = x_ref[...]
        u = (x * _rms_scale(x) * g_ref[...]).astype(BF16)
        for r0 in range(0, u.shape[0], MIX_TM):
            u_ref[r0:r0 + MIX_TM, :] = jnp.dot(
                perm_ref[...], u[r0:r0 + MIX_TM], preferred_element_type=F32).astype(BF16)

    o_ref[...] = jnp.dot(u_ref[...], w_ref[...], preferred_element_type=F32)


def _prompt_proj(x, g, w, perm, tm, tn):
    m, k = x.shape
    n = w.shape[1]
    return pl.pallas_call(
        _prompt_proj_kernel,
        grid=(m // tm, n // tn),
        in_specs=[
            pl.BlockSpec((tm, k), lambda i, j: (i, 0)),
            pl.BlockSpec((1, k), lambda i, j: (0, 0)),
            pl.BlockSpec((k, tn), lambda i, j: (0, j)),
            pl.BlockSpec(perm.shape, lambda i, j: (0, 0)),
        ],
        out_specs=pl.BlockSpec((tm, tn), lambda i, j: (i, j)),
        out_shape=jax.ShapeDtypeStruct((m, n), F32),
        scratch_shapes=[pltpu.VMEM((tm, k), BF16)],
        compiler_params=pltpu.CompilerParams(
            dimension_semantics=("arbitrary", "arbitrary"),
            vmem_limit_bytes=PROJ_VMEM_LIMIT),
        name="prompt_proj",
    )(x, g, w, perm)


def _sample_attn_kernel(q_ref, k_ref, v_ref, o_ref, *, bb):
    _sample_attn_block(q_ref, k_ref, v_ref, o_ref, bb)


def _sample_attn(z, cache_k, cache_v, bb):
    nb = z.shape[0]
    qblk = (2 * D_RNN + 2 * D_POOL) // D_X
    return pl.pallas_call(
        functools.partial(_sample_attn_kernel, bb=bb),
        grid=(nb // bb,),
        in_specs=[
            pl.BlockSpec((bb, D_X), lambda i: (i, qblk)),
            pl.BlockSpec((bb, N_MEM * SUBLANES, LANES), lambda i: (i, 0, 0)),
            pl.BlockSpec((bb, N_MEM * SUBLANES, LANES), lambda i: (i, 0, 0)),
        ],
        out_specs=pl.BlockSpec((bb, D_X), lambda i: (i, 0)),
        out_shape=jax.ShapeDtypeStruct((nb, D_X), F32),
        compiler_params=pltpu.CompilerParams(
            dimension_semantics=("arbitrary",),
            vmem_limit_bytes=VMEM_LIMIT),
        name="sample_attn",
    )(z, cache_k, cache_v)


def _sample_mix_kernel(z_ref, attn_ref, conv_ref, h_ref, pool_ref,
                       convw_ref, convb_ref, wax_ref, ba_ref, bx_ref, lam_ref, wpool_ref,
                       pscale_ref, o_ref, newh_ref, newconv_ref, newpool_ref):
    xr = z_ref[:, 0:D_RNN]
    xc = convb_ref[...] + convw_ref[CONV_W - 1:CONV_W, :] * xr
    for k in range(CONV_W - 1):
        xc = xc + convw_ref[k:k + 1, :] * conv_ref[k]
    for k in range(CONV_W - 2):
        newconv_ref[k] = conv_ref[k + 1]
    newconv_ref[CONV_W - 2] = xr

    sp = _decay_rate(lam_ref[...])
    for n in range(N_RNN_BLOCKS):
        c0, c1 = n * RNN_BLOCK, (n + 1) * RNN_BLOCK
        a, b = _rglru_block(xc[:, c0:c1], wax_ref[n], ba_ref[:, c0:c1], bx_ref[:, c0:c1],
                            sp[:, c0:c1])
        h = a * h_ref[:, c0:c1] + b
        newh_ref[:, c0:c1] = h
        o_ref[:, c0:c1] = (h * _silu(z_ref[:, D_RNN + c0:D_RNN + c1])).astype(BF16)

    xp = z_ref[:, 2 * D_RNN:2 * D_RNN + D_POOL]
    for k in range(POOL_HIST - 1):
        newpool_ref[k] = pool_ref[k + 1]
    newpool_ref[POOL_HIST - 1] = xp
    for g, w in enumerate(POOL_WINDOWS):
        c0, c1 = g * POOL_GROUP, (g + 1) * POOL_GROUP
        xg = xp[:, c0:c1]
        tot = xg
        for j in range(1, w):
            tot = tot + pool_ref[POOL_HIST - j, :, c0:c1]
        cnt = float(min(PAST_LEN + 1, w))
        d = tot / cnt - xg
        og = jnp.dot(d.astype(BF16), wpool_ref[g], preferred_element_type=F32)
        gp = z_ref[:, 2 * D_RNN + D_POOL + c0:2 * D_RNN + D_POOL + c1]
        o_ref[:, D_RNN + c0:D_RNN + c1] = (og * pscale_ref[:, c0:c1] * _silu(gp)).astype(BF16)

    gx = z_ref[:, 2 * D_RNN + 2 * D_POOL + D_X:2 * D_MIX]
    o_ref[:, D_RNN + D_POOL:] = (attn_ref[...] * _silu(gx)).astype(BF16)


def _sample_mix(z, attn, conv, h, pool, conv_w, conv_b, wax, b_a, b_x, lam, wpool, pscale, tb):
    nb = z.shape[0]
    zw = 2 * D_MIX
    rows = lambda i: (i, 0)
    const2 = lambda i: (0, 0)
    const3 = lambda i: (0, 0, 0)
    hist = lambda i: (0, i, 0)
    return pl.pallas_call(
        _sample_mix_kernel,
        grid=(nb // tb,),
        in_specs=[
            pl.BlockSpec((tb, zw), rows),
            pl.BlockSpec((tb, D_X), rows),
            pl.BlockSpec((CONV_W - 1, tb, D_RNN), hist),
            pl.BlockSpec((tb, D_RNN), rows),
            pl.BlockSpec((POOL_HIST, tb, D_POOL), hist),
            pl.BlockSpec((CONV_W, D_RNN), const2),
            pl.BlockSpec((1, D_RNN), const2),
            pl.BlockSpec((N_RNN_BLOCKS, RNN_BLOCK, 2 * RNN_BLOCK), const3),
            pl.BlockSpec((1, D_RNN), const2),
            pl.BlockSpec((1, D_RNN), const2),
            pl.BlockSpec((1, D_RNN), const2),
            pl.BlockSpec((len(POOL_WINDOWS), POOL_GROUP, POOL_GROUP), const3),
            pl.BlockSpec((1, D_POOL), const2),
        ],
        out_specs=[
            pl.BlockSpec((tb, D_MIX), rows),
            pl.BlockSpec((tb, D_RNN), rows),
            pl.BlockSpec((CONV_W - 1, tb, D_RNN), hist),
            pl.BlockSpec((POOL_HIST, tb, D_POOL), hist),
        ],
        out_shape=[
            jax.ShapeDtypeStruct((nb, D_MIX), BF16),
            jax.ShapeDtypeStruct((nb, D_RNN), F32),
            jax.ShapeDtypeStruct((CONV_W - 1, nb, D_RNN), F32),
            jax.ShapeDtypeStruct((POOL_HIST, nb, D_POOL), F32),
        ],
        compiler_params=pltpu.CompilerParams(
            dimension_semantics=("arbitrary",),
            vmem_limit_bytes=VMEM_LIMIT),
        name="sample_mix",
    )(z, attn, conv, h, pool, conv_w, conv_b, wax, b_a, b_x, lam, wpool, pscale)


def _branch_out_kernel(o_ref, gates_ref, x_ref, wb_ref, wo_ref, gpost_ref, *rest, interleaved):
    y_ref = rest[-1]
    merged = None
    for j, (r0, r1) in enumerate(((0, D_RNN), (D_RNN, D_RNN + D_POOL), (D_RNN + D_POOL, D_MIX))):
        yj = jnp.dot(o_ref[:, r0:r1], wb_ref[r0:r1, :], preferred_element_type=F32)
        term = _sigmoid(gates_ref[:, j * D_MODEL:(j + 1) * D_MODEL]) * yj
        merged = term if merged is None else merged + term
    merged = merged.astype(BF16)
    if interleaved:
        merged = jnp.dot(rest[0][...], merged, preferred_element_type=F32).astype(BF16)
    out = jnp.dot(merged, wo_ref[...], preferred_element_type=F32)
    y_ref[...] = x_ref[...] + (out * gpost_ref[...]) * _rms_scale(out)


def _branch_out(o, z, x, wb, wo, g_post, tm, unperm=None):
    m = x.shape[0]
    gw = N_BRANCH * D_MODEL
    gblk = (2 * D_MIX) // gw
    resident = pl.Buffered(1)
    in_specs = [
        pl.BlockSpec((tm, D_MIX), lambda i: (i, 0)),
        pl.BlockSpec((tm, gw), lambda i: (i, gblk)),
        pl.BlockSpec((tm, D_MODEL), lambda i: (i, 0)),
        pl.BlockSpec((D_MIX, D_MODEL), lambda i: (0, 0), pipeline_mode=resident),
        pl.BlockSpec((D_MODEL, D_MODEL), lambda i: (0, 0), pipeline_mode=resident),
        pl.BlockSpec((1, D_MODEL), lambda i: (0, 0)),
    ]
    args = [o, z, x, wb, wo, g_post]
    if unperm is not None:
        assert unperm.shape == (tm, tm)
        in_specs.append(pl.BlockSpec(unperm.shape, lambda i: (0, 0)))
        args.append(unperm)
    return pl.pallas_call(
        functools.partial(_branch_out_kernel, interleaved=unperm is not None),
        grid=(m // tm,),
        in_specs=in_specs,
        out_specs=pl.BlockSpec((tm, D_MODEL), lambda i: (i, 0)),
        out_shape=jax.ShapeDtypeStruct((m, D_MODEL), F32),
        compiler_params=pltpu.CompilerParams(
            dimension_semantics=("arbitrary",),
            vmem_limit_bytes=VMEM_LIMIT),
        name="branch_out",
    )(*args)


WROWS = 512
PER_BRANCH = D_RNN // WROWS
assert D_RNN == D_POOL == D_X and D_RNN % WROWS == 0 and D_MODEL % WROWS == 0
N_WB_BLOCKS = N_BRANCH * PER_BRANCH
N_WOUT_BLOCKS = D_MODEL // WROWS


def _branch_out_cast_kernel(o_ref, gates_ref, x_ref, wb_ref, wo_ref, gpost_ref,
                            y_ref, wbb_ref, wob_ref, merged_ref, out_ref):
    s = pl.program_id(0)

    @pl.when(s < N_WB_BLOCKS)
    def _():
        w = wb_ref[...].astype(BF16)
        wbb_ref[...] = w
        term = _sigmoid(gates_ref[...]) * jnp.dot(o_ref[...], w, preferred_element_type=F32)

        @pl.when(s == 0)
        def _():
            merged_ref[...] = term

        @pl.when(s > 0)
        def _():
            merged_ref[...] += term

    for kb in range(N_WOUT_BLOCKS):
        @pl.when(s == N_WB_BLOCKS + kb)
        def _(kb=kb):
            w = wo_ref[...].astype(BF16)
            wob_ref[...] = w
            part = jnp.dot(merged_ref[:, kb * WROWS:(kb + 1) * WROWS].astype(BF16), w,
                           preferred_element_type=F32)
            if kb == 0:
                out_ref[...] = part
            else:
                out_ref[...] += part

    @pl.when(s == N_WB_BLOCKS + N_WOUT_BLOCKS - 1)
    def _():
        out = out_ref[...]
        y_ref[...] = x_ref[...] + out * _rms_scale(out) * gpost_ref[...]


def _branch_out_cast(o, z, x, wb, wo, g_post):
    m = x.shape[0]
    gblk0 = (2 * D_MIX) // D_MODEL
    wb_blk = lambda s: jnp.minimum(s, N_WB_BLOCKS - 1)
    wo_blk = lambda s: jnp.maximum(s - N_WB_BLOCKS, 0)
    return pl.pallas_call(
        _branch_out_cast_kernel,
        grid=(N_WB_BLOCKS + N_WOUT_BLOCKS,),
        in_specs=[
            pl.BlockSpec((m, WROWS), lambda s: (0, wb_blk(s))),
            pl.BlockSpec((m, D_MODEL), lambda s: (0, gblk0 + wb_blk(s) // PER_BRANCH)),
            pl.BlockSpec((m, D_MODEL), lambda s: (0, 0)),
            pl.BlockSpec((WROWS, D_MODEL), lambda s: (wb_blk(s), 0)),
            pl.BlockSpec((WROWS, D_MODEL), lambda s: (wo_blk(s), 0)),
            pl.BlockSpec((1, D_MODEL), lambda s: (0, 0)),
        ],
        out_specs=[
            pl.BlockSpec((m, D_MODEL), lambda s: (0, 0)),
            pl.BlockSpec((WROWS, D_MODEL), lambda s: (wb_blk(s), 0)),
            pl.BlockSpec((WROWS, D_MODEL), lambda s: (wo_blk(s), 0)),
        ],
        out_shape=[
            jax.ShapeDtypeStruct((m, D_MODEL), F32),
            jax.ShapeDtypeStruct(wb.shape, BF16),
            jax.ShapeDtypeStruct(wo.shape, BF16),
        ],
        scratch_shapes=[pltpu.VMEM((m, D_MODEL), F32), pltpu.VMEM((m, D_MODEL), F32)],
        compiler_params=pltpu.CompilerParams(
            dimension_semantics=("arbitrary",),
            vmem_limit_bytes=VMEM_LIMIT),
        name="branch_out_cast",
    )(o, z, x, wb, wo, g_post)


def _mem_kv_kernel(x_ref, g_ref, w_ref, k_ref, v_ref, u_ref):
    j = pl.program_id(1)

    @pl.when(j == 0)
    def _():
        x = x_ref[...]
        u_ref[...] = (x * _rms_scale(x) * g_ref[...]).astype(BF16)

    res = jnp.dot(u_ref[...], w_ref[...].astype(BF16), preferred_element_type=F32)

    @pl.when(j == 0)
    def _():
        k_ref[...] = res

    @pl.when(j == 1)
    def _():
        v_ref[...] = res


def _mem_kv(x, g, w, tm):
    m, k = x.shape
    assert w.shape[1] == 2 * D_X
    half = pl.BlockSpec((tm, D_X), lambda i, j: (i, 0))
    return pl.pallas_call(
        _mem_kv_kernel,
        grid=(m // tm, 2),
        in_specs=[
            pl.BlockSpec((tm, k), lambda i, j: (i, 0)),
            pl.BlockSpec((1, k), lambda i, j: (0, 0)),
            pl.BlockSpec((k, D_X), lambda i, j: (0, j)),
        ],
        out_specs=[half, half],
        out_shape=[jax.ShapeDtypeStruct((m, D_X), F32)] * 2,
        scratch_shapes=[pltpu.VMEM((tm, k), BF16)],
        compiler_params=pltpu.CompilerParams(
            dimension_semantics=("arbitrary", "arbitrary"),
            vmem_limit_bytes=VMEM_LIMIT),
        name="mem_kv",
    )(x, g, w)


def kernel(x_prompt, x_sample, mem_prompt, state_rglru_h, state_conv, state_pool, cache_mem_k, cache_mem_v, g_pre, w_in, conv_w, conv_b, w_rg_a, b_rg_a, w_rg_x, b_rg_x, lru_lambda, w_pool, pool_scale, g_mem, w_kv, w_branch, w_out, g_post):
    batch, seq, _ = x_prompt.shape
    nb = x_sample.shape[0]
    depth = g_pre.shape[0]
    assert depth == 1 and x_sample.shape[1] == 1

    l = 0
    row = lambda v: v.reshape(1, -1)
    wax = jnp.concatenate([w_rg_a[l], w_rg_x[l]], axis=-1).astype(BF16)
    wpool = w_pool[l].astype(BF16)
    mix_params = (conv_w[l], row(conv_b[l]), wax, row(b_rg_a[l]), row(b_rg_x[l]),
                  row(lru_lambda[l]), wpool, row(pool_scale[l]))

    xp2 = x_prompt.reshape(batch * seq, D_MODEL)
    xs2 = x_sample.reshape(nb, D_MODEL)
    mem2 = mem_prompt.reshape(batch * N_MEM, D_MODEL)

    z_s, w_in_b = _sample_proj(xs2, row(g_pre[l]), w_in[l], tn=1024)
    attn_s = _sample_attn(z_s, _cache_rows(cache_mem_k[l]), _cache_rows(cache_mem_v[l]), bb=8)

    o_s, h_s, c_s, p_s = _sample_mix(
        z_s, attn_s, state_conv[l].transpose(1, 0, 2), state_rglru_h[l],
        state_pool[l].transpose(1, 0, 2), *mix_params, tb=32)
    y_s, w_br_b, w_out_b = _branch_out_cast(o_s, z_s, xs2, w_branch[l], w_out[l], row(g_post[l]))

    mem_k, mem_v = _mem_kv(mem2, row(g_mem[l]), w_kv[l], tm=512)
    mem_k = mem_k.reshape(batch, N_MEM, D_X)
    mem_v = mem_v.reshape(batch, N_MEM, D_X)

    perm = _chunk_interleave()
    z_p = _prompt_proj(xp2, row(g_pre[l]), w_in_b, perm, tm=1024, tn=2048)
    o_p, h_p, c_p, p_p = _prompt_mix(z_p, mem_k, mem_v, *mix_params,
                                     batch=batch, seq=seq, tm=MIX_TM)
    y_p = _branch_out(o_p, z_p, xp2, w_br_b, w_out_b, row(g_post[l]), tm=MIX_TM, unperm=perm.T)

    return (
        y_p.reshape(batch, seq, D_MODEL),
        y_s.reshape(nb, 1, D_MODEL),
        h_p.reshape(1, batch, D_RNN),
        c_p.reshape(1, batch, CONV_W - 1, D_RNN),
        p_p.reshape(1, batch, POOL_HIST, D_POOL),
        mem_k.reshape(1, batch, N_MEM, N_XHEADS, XHEAD_DIM),
        mem_v.reshape(1, batch, N_MEM, N_XHEADS, XHEAD_DIM),
        h_s.reshape(1, nb, D_RNN),
        c_s.transpose(1, 0, 2)[None],
        p_s.transpose(1, 0, 2)[None],
    )
```

```python
import functools

import jax
import jax.numpy as jnp
from jax import lax
from jax.experimental import pallas as pl
from jax.experimental.pallas import tpu as pltpu

D_MODEL = 2048
PAST_LEN = 16384
D_RNN = 1024
N_RNN_BLOCKS = 8
RNN_BLOCK = D_RNN // N_RNN_BLOCKS
CONV_W = 4
LRU_C = 8.0
D_POOL = 1024
POOL_WINDOWS = (2, 4, 8, 16)
POOL_GROUP = D_POOL // len(POOL_WINDOWS)
POOL_HIST = max(POOL_WINDOWS) - 1
N_MEM = 256
N_XHEADS = 4
XHEAD_DIM = 256
D_X = N_XHEADS * XHEAD_DIM
N_BRANCH = 3
D_MIX = D_RNN + D_POOL + D_X
D_IN = 2 * D_MIX + N_BRANCH * D_MODEL
EPS = 1e-6

SUBLANES = 8
LANES = 128
VMEM_LIMIT = 56 * 1024 * 1024
PROJ_VMEM_LIMIT = 60 * 1024 * 1024
MIX_TM = 256
PROJ_TM, PROJ_TN = 1024, 2048
SAMPLE_PROJ_TN = 1024
ATTN_BB = 8
SAMPLE_MIX_TB = 32
KV_TM = 512

BF16 = jnp.bfloat16
F32 = jnp.float32

NEG_LOG2_E = -1.4426950408889634


def _sigmoid(x):
    return 1.0 / (1.0 + jnp.exp2(x * NEG_LOG2_E))


def _silu(x):
    return x * _sigmoid(x)


def _softplus(x):
    return jnp.maximum(x, 0.0) + jnp.log1p(jnp.exp(-jnp.abs(x)))


def _rms_scale(x):
    return lax.rsqrt(jnp.mean(x * x, axis=-1, keepdims=True) + EPS)


def _chunk_interleave():
    nrow = MIX_TM // SUBLANES
    p = jnp.arange(MIX_TM)
    token = (p % SUBLANES) * nrow + p // SUBLANES
    return (token[:, None] == jnp.arange(MIX_TM)[None, :]).astype(BF16)


def _sample_proj_kernel(x_ref, g_ref, w_ref, o_ref, wb_ref, u_ref):
    @pl.when(pl.program_id(0) == 0)
    def _():
        x = x_ref[...]
        u_ref[...] = (x * _rms_scale(x) * g_ref[...]).astype(BF16)

    w = w_ref[...].astype(BF16)
    wb_ref[...] = w
    o_ref[...] = jnp.dot(u_ref[...], w, preferred_element_type=F32)


def _sample_proj(x, g, w, tn):
    m, k = x.shape
    n = w.shape[1]
    return pl.pallas_call(
        _sample_proj_kernel,
        grid=(n // tn,),
        in_specs=[
            pl.BlockSpec((m, k), lambda j: (0, 0)),
            pl.BlockSpec((1, k), lambda j: (0, 0)),
            pl.BlockSpec((k, tn), lambda j: (0, j)),
        ],
        out_specs=[
            pl.BlockSpec((m, tn), lambda j: (0, j)),
            pl.BlockSpec((k, tn), lambda j: (0, j)),
        ],
        out_shape=[
            jax.ShapeDtypeStruct((m, n), F32),
            jax.ShapeDtypeStruct((k, n), BF16),
        ],
        scratch_shapes=[pltpu.VMEM((m, k), BF16)],
        compiler_params=pltpu.CompilerParams(
            dimension_semantics=("arbitrary",),
            vmem_limit_bytes=VMEM_LIMIT),
        name="sample_proj",
    )(x, g, w)


def _decay_rate(lam):
    return _softplus(-lam) * (LRU_C * NEG_LOG2_E)


def _rglru_block(xc, wax, ba, bx, rate):
    ri = jnp.dot(xc.astype(BF16), wax, preferred_element_type=F32)
    r = _sigmoid(ri[:, :RNN_BLOCK] + ba)
    i = _sigmoid(ri[:, RNN_BLOCK:] + bx)
    a = jnp.exp2(r * rate)
    one_m = 1.0 - a * a
    mult = jnp.where(one_m > 0.0, one_m * lax.rsqrt(one_m), 0.0)
    return a, mult * i * xc


def _prompt_mix_kernel(z_ref, k_ref, v_ref, convw_ref, convb_ref, wax_ref, ba_ref, bx_ref,
                       lam_ref, wpool_ref, pscale_ref,
                       o_ref, newh_ref, newconv_ref, newpool_ref,
                       conv_carry, pool_carry, h_carry, kb_ref, vb_ref, ac_scr, hl_scr, *, tm):
    l = pl.program_id(1)
    last = pl.num_programs(1) - 1
    nrow = tm // SUBLANES

    @pl.when(l == 0)
    def _():
        conv_carry[...] = jnp.zeros(conv_carry.shape, F32)
        pool_carry[...] = jnp.zeros(pool_carry.shape, F32)
        h_carry[...] = jnp.zeros(h_carry.shape, F32)
        kb_ref[...] = k_ref[0].astype(BF16)
        vb_ref[...] = v_ref[0].astype(BF16)

    chunk_id = lax.broadcasted_iota(jnp.int32, (SUBLANES, LANES), 0)
    first_chunk = chunk_id == 0

    def load_groups(col, width=LANES):
        return [z_ref[r * SUBLANES:(r + 1) * SUBLANES, col:col + width] for r in range(nrow)]

    def store_groups(col, rows, width=LANES):
        o_ref[:, col:col + width] = jnp.concatenate(rows, axis=0).astype(BF16)

    def history(tail_group, carry_ref, j, c0):
        tail = pltpu.roll(tail_group, 1, 0)
        prev = jnp.where(first_chunk, carry_ref[j - 1, :, c0:c0 + LANES], tail)
        carry_ref[j - 1, :, c0:c0 + LANES] = tail
        return prev

    rate = _decay_rate(lam_ref[...])
    for n in range(N_RNN_BLOCKS):
        c0, c1 = n * RNN_BLOCK, (n + 1) * RNN_BLOCK
        xs = load_groups(c0)
        ext = [history(xs[nrow - j], conv_carry, j, c0) for j in range(CONV_W - 1, 0, -1)] + xs
        cw = [jnp.broadcast_to(convw_ref[k:k + 1, c0:c1], (SUBLANES, LANES)) for k in range(CONV_W)]
        cb = jnp.broadcast_to(convb_ref[:, c0:c1], (SUBLANES, LANES))
        xc = []
        for r in range(nrow):
            acc = cb + cw[0] * ext[r]
            for k in range(1, CONV_W):
                acc = acc + cw[k] * ext[r + k]
            xc.append(acc)
        a, b = _rglru_block(jnp.concatenate(xc, axis=0), wax_ref[n], ba_ref[:, c0:c1],
                            bx_ref[:, c0:c1], rate[:, c0:c1])
        ac_scr[:, c0:c1] = a
        hl_scr[:, c0:c1] = b

    acc_a = ac_scr[0:SUBLANES, :]
    acc_h = hl_scr[0:SUBLANES, :]
    for r in range(1, nrow):
        rows = slice(r * SUBLANES, (r + 1) * SUBLANES)
        ar = ac_scr[rows, :]
        acc_h = ar * acc_h + hl_scr[rows, :]
        acc_a = ar * acc_a
        ac_scr[rows, :] = acc_a
        hl_scr[rows, :] = acc_h
    h_in = h_carry[...]
    entering = []
    for c in range(SUBLANES):
        entering.append(h_in)
        h_in = acc_a[c:c + 1] * h_in + acc_h[c:c + 1]
    h_carry[...] = h_in
    h_enter = jnp.concatenate(entering, axis=0)
    for n in range(N_RNN_BLOCKS):
        c0, c1 = n * RNN_BLOCK, (n + 1) * RNN_BLOCK
        gr = load_groups(D_RNN + c0)
        store_groups(c0, [(hl_scr[r * SUBLANES:(r + 1) * SUBLANES, c0:c1]
                           + ac_scr[r * SUBLANES:(r + 1) * SUBLANES, c0:c1] * h_enter[:, c0:c1])
                          * _silu(gr[r]) for r in range(nrow)])

    pcol = 2 * D_RNN
    blocks = [(w, c0) for g, w in enumerate(POOL_WINDOWS)
              for c0 in range(g * POOL_GROUP, (g + 1) * POOL_GROUP, LANES)]

    def group(c0, r):
        return z_ref[r * SUBLANES:(r + 1) * SUBLANES, pcol + c0:pcol + c0 + LANES]

    def mean_minus_token(tot, w, c0, r):
        if r < w - 1:
            pos1 = l * tm + chunk_id * nrow + (r + 1)
            mean = tot / jnp.minimum(pos1, w).astype(F32)
        else:
            mean = tot * (1.0 / w)
        return mean - group(c0, r)

    hist, tot = {}, {}
    for w, c0 in blocks:
        hist[c0] = [history(group(c0, nrow - j), pool_carry, j, c0) for j in range(1, w)]
        t = group(c0, 0)
        for h in hist[c0]:
            t = t + h
        tot[c0] = t
        hl_scr[0:SUBLANES, c0:c0 + LANES] = mean_minus_token(t, w, c0, 0)
    for r in range(1, nrow):
        for w, c0 in blocks:
            leaving = group(c0, r - w) if r >= w else hist[c0][w - r - 1]
            tot[c0] = tot[c0] + (group(c0, r) - leaving)
            hl_scr[r * SUBLANES:(r + 1) * SUBLANES, c0:c0 + LANES] = mean_minus_token(
                tot[c0], w, c0, r)
    for g, w in enumerate(POOL_WINDOWS):
        c0, c1 = g * POOL_GROUP, (g + 1) * POOL_GROUP
        og = jnp.dot(hl_scr[:, c0:c1].astype(BF16), wpool_ref[g], preferred_element_type=F32)
        gp = z_ref[:, pcol + D_POOL + c0:pcol + D_POOL + c1]
        o_ref[:, D_RNN + c0:D_RNN + c1] = (og * pscale_ref[:, c0:c1] * _silu(gp)).astype(BF16)

    qoff = 2 * D_RNN + 2 * D_POOL
    for hd in range(N_XHEADS):
        c0, c1 = hd * XHEAD_DIM, (hd + 1) * XHEAD_DIM
        q = z_ref[:, qoff + c0:qoff + c1].astype(BF16)
        s = lax.dot_general(q, kb_ref[:, c0:c1], (((1,), (1,)), ((), ())),
                            preferred_element_type=F32) * (XHEAD_DIM ** -0.5)
        p = jnp.exp(s - jnp.max(s, axis=-1, keepdims=True))
        p = p / jnp.sum(p, axis=-1, keepdims=True)
        ox = jnp.dot(p.astype(BF16), vb_ref[:, c0:c1], preferred_element_type=F32)
        gx = z_ref[:, qoff + D_X + c0:qoff + D_X + c1]
        o_ref[:, D_RNN + D_POOL + c0:D_RNN + D_POOL + c1] = (ox * _silu(gx)).astype(BF16)

    @pl.when(l == last)
    def _():
        newh_ref[0] = h_carry[...]
        tail_row = lambda j: (nrow - j) * SUBLANES + SUBLANES - 1
        for j in range(1, CONV_W):
            newconv_ref[0, CONV_W - 1 - j:CONV_W - j, :] = z_ref[tail_row(j):tail_row(j) + 1, 0:D_RNN]
        for j in range(1, POOL_HIST + 1):
            newpool_ref[0, POOL_HIST - j:POOL_HIST - j + 1, :] = (
                z_ref[tail_row(j):tail_row(j) + 1, pcol:pcol + D_POOL])


def _prompt_mix(z, mem_k, mem_v, conv_w, conv_b, wax, b_a, b_x, lam, wpool, pscale,
                batch, seq, tm):
    nl = seq // tm
    zw = 2 * D_MIX
    const2 = lambda b, l: (0, 0)
    const3 = lambda b, l: (0, 0, 0)
    kern = functools.partial(_prompt_mix_kernel, tm=tm)
    return pl.pallas_call(
        kern,
        grid=(batch, nl),
        in_specs=[
            pl.BlockSpec((tm, zw), lambda b, l: (b * nl + l, 0)),
            pl.BlockSpec((1, N_MEM, D_X), lambda b, l: (b, 0, 0)),
            pl.BlockSpec((1, N_MEM, D_X), lambda b, l: (b, 0, 0)),
            pl.BlockSpec((CONV_W, D_RNN), const2),
            pl.BlockSpec((1, D_RNN), const2),
            pl.BlockSpec((N_RNN_BLOCKS, RNN_BLOCK, 2 * RNN_BLOCK), const3),
            pl.BlockSpec((1, D_RNN), const2),
            pl.BlockSpec((1, D_RNN), const2),
            pl.BlockSpec((1, D_RNN), const2),
            pl.BlockSpec((len(POOL_WINDOWS), POOL_GROUP, POOL_GROUP), const3),
            pl.BlockSpec((1, D_POOL), const2),
        ],
        out_specs=[
            pl.BlockSpec((tm, D_MIX), lambda b, l: (b * nl + l, 0)),
            pl.BlockSpec((1, 1, D_RNN), lambda b, l: (b, 0, 0)),
            pl.BlockSpec((1, CONV_W - 1, D_RNN), lambda b, l: (b, 0, 0)),
            pl.BlockSpec((1, POOL_HIST, D_POOL), lambda b, l: (b, 0, 0)),
        ],
        out_shape=[
            jax.ShapeDtypeStruct((batch * seq, D_MIX), BF16),
            jax.ShapeDtypeStruct((batch, 1, D_RNN), F32),
            jax.ShapeDtypeStruct((batch, CONV_W - 1, D_RNN), F32),
            jax.ShapeDtypeStruct((batch, POOL_HIST, D_POOL), F32),
        ],
        scratch_shapes=[
            pltpu.VMEM((CONV_W - 1, SUBLANES, D_RNN), F32),
            pltpu.VMEM((POOL_HIST, SUBLANES, D_POOL), F32),
            pltpu.VMEM((1, D_RNN), F32),
            pltpu.VMEM((N_MEM, D_X), BF16),
            pltpu.VMEM((N_MEM, D_X), BF16),
            pltpu.VMEM((tm, D_RNN), F32),
            pltpu.VMEM((tm, D_RNN), F32),
        ],
        compiler_params=pltpu.CompilerParams(
            dimension_semantics=("arbitrary", "arbitrary"),
            vmem_limit_bytes=VMEM_LIMIT),
        name="prompt_mix",
    )(z, mem_k, mem_v, conv_w, conv_b, wax, b_a, b_x, lam, wpool, pscale)


def _cache_rows(c):
    nb = c.shape[0]
    c = c.reshape(nb, N_MEM, N_XHEADS, XHEAD_DIM // LANES, LANES)
    return c.transpose(0, 1, 3, 2, 4).reshape(nb, N_MEM * SUBLANES, LANES)


def _sample_attn_block(q_ref, k_ref, v_ref, o_ref, bb):
    halves = XHEAD_DIM // LANES
    assert halves * N_XHEADS == SUBLANES
    r = lax.broadcasted_iota(jnp.int32, (SUBLANES, LANES), 0)
    c = lax.broadcasted_iota(jnp.int32, (SUBLANES, LANES), 1)
    diag = (c % SUBLANES) == r
    first_half = r < N_XHEADS
    nchunk = N_MEM * SUBLANES // LANES
    scores = []
    for j in range(bb):
        qn = jnp.concatenate(
            [q_ref[j:j + 1, (h * halves + t) * LANES:(h * halves + t + 1) * LANES]
             for t in range(halves) for h in range(N_XHEADS)], axis=0)
        scores.append(lax.dot_general(qn.astype(BF16), k_ref[j].astype(BF16),
                                      (((1,), (1,)), ((), ())), preferred_element_type=F32)
                      * (XHEAD_DIM ** -0.5))
    probs = []
    for j in range(bb):
        s = scores[j]
        chunks = []
        for ci in range(nchunk):
            sm = jnp.where(diag, s[:, ci * LANES:(ci + 1) * LANES], 0.0)
            other = pltpu.roll(sm, N_XHEADS, 0)
            other = jnp.where(first_half, pltpu.roll(other, LANES - N_XHEADS, 1),
                              pltpu.roll(other, N_XHEADS, 1))
            chunks.append(jnp.where(diag, sm + other, -jnp.inf))
        t_full = jnp.concatenate(chunks, axis=1)
        e = jnp.exp(t_full - jnp.max(t_full, axis=1, keepdims=True))
        probs.append((e / jnp.sum(e, axis=1, keepdims=True)).astype(BF16))
    for j in range(bb):
        o = jnp.dot(probs[j], v_ref[j].astype(BF16), preferred_element_type=F32)
        for t in range(halves):
            for h in range(N_XHEADS):
                col = (h * halves + t) * LANES
                o_ref[j:j + 1, col:col + LANES] = o[t * N_XHEADS + h:t * N_XHEADS + h + 1, :]


def _prompt_proj_kernel(x_ref, g_ref, w_ref, perm_ref, o_ref, u_ref):
    @pl.when(pl.program_id(1) == 0)
    def _():
        x = x_ref[...]
        u = (x * _rms_scale(x) * g_ref[...]).astype(BF16)
        for r0 in range(0, u.shape[0], MIX_TM):
            u_ref[r0:r0 + MIX_TM, :] = jnp.dot(
                perm_ref[...], u[r0:r0 + MIX_TM], preferred_element_type=F32).astype(BF16)

    o_ref[...] = jnp.dot(u_ref[...], w_ref[...], preferred_element_type=F32)


def _prompt_proj(x, g, w, perm, tm, tn):
    m, k = x.shape
    n = w.shape[1]
    return pl.pallas_call(
        _prompt_proj_kernel,
        grid=(m // tm, n // tn),
        in_specs=[
            pl.BlockSpec((tm, k), lambda i, j: (i, 0)),
            pl.BlockSpec((1, k), lambda i, j: (0, 0)),
            pl.BlockSpec((k, tn), lambda i, j: (0, j)),
            pl.BlockSpec(perm.shape, lambda i, j: (0, 0)),
        ],
        out_specs=pl.BlockSpec((tm, tn), lambda i, j: (i, j)),
        out_shape=jax.ShapeDtypeStruct((m, n), F32),
        scratch_shapes=[pltpu.VMEM((tm, k), BF16)],
        compiler_params=pltpu.CompilerParams(
            dimension_semantics=("arbitrary", "arbitrary"),
            vmem_limit_bytes=PROJ_VMEM_LIMIT),
        name="prompt_proj",
    )(x, g, w, perm)


def _sample_attn_kernel(q_ref, k_ref, v_ref, o_ref, *, bb):
    _sample_attn_block(q_ref, k_ref, v_ref, o_ref, bb)


def _sample_attn(z, cache_k, cache_v, bb):
    nb = z.shape[0]
    qblk = (2 * D_RNN + 2 * D_POOL) // D_X
    return pl.pallas_call(
        functools.partial(_sample_attn_kernel, bb=bb),
        grid=(nb // bb,),
        in_specs=[
            pl.BlockSpec((bb, D_X), lambda i: (i, qblk)),
            pl.BlockSpec((bb, N_MEM * SUBLANES, LANES), lambda i: (i, 0, 0)),
            pl.BlockSpec((bb, N_MEM * SUBLANES, LANES), lambda i: (i, 0, 0)),
        ],
        out_specs=pl.BlockSpec((bb, D_X), lambda i: (i, 0)),
        out_shape=jax.ShapeDtypeStruct((nb, D_X), F32),
        compiler_params=pltpu.CompilerParams(
            dimension_semantics=("arbitrary",),
            vmem_limit_bytes=VMEM_LIMIT),
        name="sample_attn",
    )(z, cache_k, cache_v)


def _sample_mix_kernel(z_ref, attn_ref, conv_ref, h_ref, pool_ref,
                       convw_ref, convb_ref, wax_ref, ba_ref, bx_ref, lam_ref, wpool_ref,
                       pscale_ref, o_ref, newh_ref, newconv_ref, newpool_ref):
    xr = z_ref[:, 0:D_RNN]
    xc = convb_ref[...] + convw_ref[CONV_W - 1:CONV_W, :] * xr
    for k in range(CONV_W - 1):
        xc = xc + convw_ref[k:k + 1, :] * conv_ref[k]
    for k in range(CONV_W - 2):
        newconv_ref[k] = conv_ref[k + 1]
    newconv_ref[CONV_W - 2] = xr

    rate = _decay_rate(lam_ref[...])
    for n in range(N_RNN_BLOCKS):
        c0, c1 = n * RNN_BLOCK, (n + 1) * RNN_BLOCK
        a, b = _rglru_block(xc[:, c0:c1], wax_ref[n], ba_ref[:, c0:c1], bx_ref[:, c0:c1],
                            rate[:, c0:c1])
        h = a * h_ref[:, c0:c1] + b
        newh_ref[:, c0:c1] = h
        o_ref[:, c0:c1] = (h * _silu(z_ref[:, D_RNN + c0:D_RNN + c1])).astype(BF16)

    xp = z_ref[:, 2 * D_RNN:2 * D_RNN + D_POOL]
    for k in range(POOL_HIST - 1):
        newpool_ref[k] = pool_ref[k + 1]
    newpool_ref[POOL_HIST - 1] = xp
    for g, w in enumerate(POOL_WINDOWS):
        c0, c1 = g * POOL_GROUP, (g + 1) * POOL_GROUP
        xg = xp[:, c0:c1]
        tot = xg
        for j in range(1, w):
            tot = tot + pool_ref[POOL_HIST - j, :, c0:c1]
        cnt = float(min(PAST_LEN + 1, w))
        d = tot / cnt - xg
        og = jnp.dot(d.astype(BF16), wpool_ref[g], preferred_element_type=F32)
        gp = z_ref[:, 2 * D_RNN + D_POOL + c0:2 * D_RNN + D_POOL + c1]
        o_ref[:, D_RNN + c0:D_RNN + c1] = (og * pscale_ref[:, c0:c1] * _silu(gp)).astype(BF16)

    gx = z_ref[:, 2 * D_RNN + 2 * D_POOL + D_X:2 * D_MIX]
    o_ref[:, D_RNN + D_POOL:] = (attn_ref[...] * _silu(gx)).astype(BF16)


def _sample_mix(z, attn, conv, h, pool, conv_w, conv_b, wax, b_a, b_x, lam, wpool, pscale, tb):
    nb = z.shape[0]
    zw = 2 * D_MIX
    rows = lambda i: (i, 0)
    const2 = lambda i: (0, 0)
    const3 = lambda i: (0, 0, 0)
    hist = lambda i: (0, i, 0)
    return pl.pallas_call(
        _sample_mix_kernel,
        grid=(nb // tb,),
        in_specs=[
            pl.BlockSpec((tb, zw), rows),
            pl.BlockSpec((tb, D_X), rows),
            pl.BlockSpec((CONV_W - 1, tb, D_RNN), hist),
            pl.BlockSpec((tb, D_RNN), rows),
            pl.BlockSpec((POOL_HIST, tb, D_POOL), hist),
            pl.BlockSpec((CONV_W, D_RNN), const2),
            pl.BlockSpec((1, D_RNN), const2),
            pl.BlockSpec((N_RNN_BLOCKS, RNN_BLOCK, 2 * RNN_BLOCK), const3),
            pl.BlockSpec((1, D_RNN), const2),
            pl.BlockSpec((1, D_RNN), const2),
            pl.BlockSpec((1, D_RNN), const2),
            pl.BlockSpec((len(POOL_WINDOWS), POOL_GROUP, POOL_GROUP), const3),
            pl.BlockSpec((1, D_POOL), const2),
        ],
        out_specs=[
            pl.BlockSpec((tb, D_MIX), rows),
            pl.BlockSpec((tb, D_RNN), rows),
            pl.BlockSpec((CONV_W - 1, tb, D_RNN), hist),
            pl.BlockSpec((POOL_HIST, tb, D_POOL), hist),
        ],
        out_shape=[
            jax.ShapeDtypeStruct((nb, D_MIX), BF16),
            jax.ShapeDtypeStruct((nb, D_RNN), F32),
            jax.ShapeDtypeStruct((CONV_W - 1, nb, D_RNN), F32),
            jax.ShapeDtypeStruct((POOL_HIST, nb, D_POOL), F32),
        ],
        compiler_params=pltpu.CompilerParams(
            dimension_semantics=("arbitrary",),
            vmem_limit_bytes=VMEM_LIMIT),
        name="sample_mix",
    )(z, attn, conv, h, pool, conv_w, conv_b, wax, b_a, b_x, lam, wpool, pscale)


def _branch_out_kernel(o_ref, gates_ref, x_ref, wb_ref, wo_ref, gpost_ref, *rest, interleaved):
    y_ref = rest[-1]
    merged = None
    for j, (r0, r1) in enumerate(((0, D_RNN), (D_RNN, D_RNN + D_POOL), (D_RNN + D_POOL, D_MIX))):
        yj = jnp.dot(o_ref[:, r0:r1], wb_ref[r0:r1, :], preferred_element_type=F32)
        term = _sigmoid(gates_ref[:, j * D_MODEL:(j + 1) * D_MODEL]) * yj
        merged = term if merged is None else merged + term
    merged = merged.astype(BF16)
    if interleaved:
        merged = jnp.dot(rest[0][...], merged, preferred_element_type=F32).astype(BF16)
    out = jnp.dot(merged, wo_ref[...], preferred_element_type=F32)
    y_ref[...] = x_ref[...] + (out * gpost_ref[...]) * _rms_scale(out)


def _branch_out(o, z, x, wb, wo, g_post, tm, unperm=None):
    m = x.shape[0]
    gw = N_BRANCH * D_MODEL
    gblk = (2 * D_MIX) // gw
    resident = pl.Buffered(1)
    in_specs = [
        pl.BlockSpec((tm, D_MIX), lambda i: (i, 0)),
        pl.BlockSpec((tm, gw), lambda i: (i, gblk)),
        pl.BlockSpec((tm, D_MODEL), lambda i: (i, 0)),
        pl.BlockSpec((D_MIX, D_MODEL), lambda i: (0, 0), pipeline_mode=resident),
        pl.BlockSpec((D_MODEL, D_MODEL), lambda i: (0, 0), pipeline_mode=resident),
        pl.BlockSpec((1, D_MODEL), lambda i: (0, 0)),
    ]
    args = [o, z, x, wb, wo, g_post]
    if unperm is not None:
        assert unperm.shape == (tm, tm)
        in_specs.append(pl.BlockSpec(unperm.shape, lambda i: (0, 0)))
        args.append(unperm)
    return pl.pallas_call(
        functools.partial(_branch_out_kernel, interleaved=unperm is not None),
        grid=(m // tm,),
        in_specs=in_specs,
        out_specs=pl.BlockSpec((tm, D_MODEL), lambda i: (i, 0)),
        out_shape=jax.ShapeDtypeStruct((m, D_MODEL), F32),
        compiler_params=pltpu.CompilerParams(
            dimension_semantics=("arbitrary",),
            vmem_limit_bytes=VMEM_LIMIT),
        name="branch_out",
    )(*args)


WROWS = 512
PER_BRANCH = D_RNN // WROWS
assert D_RNN == D_POOL == D_X and D_RNN % WROWS == 0 and D_MODEL % WROWS == 0
N_WB_BLOCKS = N_BRANCH * PER_BRANCH
N_WOUT_BLOCKS = D_MODEL // WROWS


def _branch_out_cast_kernel(o_ref, gates_ref, x_ref, wb_ref, wo_ref, gpost_ref,
                            y_ref, wbb_ref, wob_ref, merged_ref, out_ref):
    s = pl.program_id(0)

    @pl.when(s < N_WB_BLOCKS)
    def _():
        w = wb_ref[...].astype(BF16)
        wbb_ref[...] = w
        term = _sigmoid(gates_ref[...]) * jnp.dot(o_ref[...], w, preferred_element_type=F32)

        @pl.when(s == 0)
        def _():
            merged_ref[...] = term

        @pl.when(s > 0)
        def _():
            merged_ref[...] += term

    for kb in range(N_WOUT_BLOCKS):
        @pl.when(s == N_WB_BLOCKS + kb)
        def _(kb=kb):
            w = wo_ref[...].astype(BF16)
            wob_ref[...] = w
            part = jnp.dot(merged_ref[:, kb * WROWS:(kb + 1) * WROWS].astype(BF16), w,
                           preferred_element_type=F32)
            if kb == 0:
                out_ref[...] = part
            else:
                out_ref[...] += part

    @pl.when(s == N_WB_BLOCKS + N_WOUT_BLOCKS - 1)
    def _():
        out = out_ref[...]
        y_ref[...] = x_ref[...] + out * _rms_scale(out) * gpost_ref[...]


def _branch_out_cast(o, z, x, wb, wo, g_post):
    m = x.shape[0]
    gblk0 = (2 * D_MIX) // D_MODEL
    wb_blk = lambda s: jnp.minimum(s, N_WB_BLOCKS - 1)
    wo_blk = lambda s: jnp.maximum(s - N_WB_BLOCKS, 0)
    return pl.pallas_call(
        _branch_out_cast_kernel,
        grid=(N_WB_BLOCKS + N_WOUT_BLOCKS,),
        in_specs=[
            pl.BlockSpec((m, WROWS), lambda s: (0, wb_blk(s))),
            pl.BlockSpec((m, D_MODEL), lambda s: (0, gblk0 + wb_blk(s) // PER_BRANCH)),
            pl.BlockSpec((m, D_MODEL), lambda s: (0, 0)),
            pl.BlockSpec((WROWS, D_MODEL), lambda s: (wb_blk(s), 0)),
            pl.BlockSpec((WROWS, D_MODEL), lambda s: (wo_blk(s), 0)),
            pl.BlockSpec((1, D_MODEL), lambda s: (0, 0)),
        ],
        out_specs=[
            pl.BlockSpec((m, D_MODEL), lambda s: (0, 0)),
            pl.BlockSpec((WROWS, D_MODEL), lambda s: (wb_blk(s), 0)),
            pl.BlockSpec((WROWS, D_MODEL), lambda s: (wo_blk(s), 0)),
        ],
        out_shape=[
            jax.ShapeDtypeStruct((m, D_MODEL), F32),
            jax.ShapeDtypeStruct(wb.shape, BF16),
            jax.ShapeDtypeStruct(wo.shape, BF16),
        ],
        scratch_shapes=[pltpu.VMEM((m, D_MODEL), F32), pltpu.VMEM((m, D_MODEL), F32)],
        compiler_params=pltpu.CompilerParams(
            dimension_semantics=("arbitrary",),
            vmem_limit_bytes=VMEM_LIMIT),
        name="branch_out_cast",
    )(o, z, x, wb, wo, g_post)


def _mem_kv_kernel(x_ref, g_ref, w_ref, k_ref, v_ref, u_ref):
    j = pl.program_id(1)

    @pl.when(j == 0)
    def _():
        x = x_ref[...]
        u_ref[...] = (x * _rms_scale(x) * g_ref[...]).astype(BF16)

    res = jnp.dot(u_ref[...], w_ref[...].astype(BF16), preferred_element_type=F32)

    @pl.when(j == 0)
    def _():
        k_ref[...] = res

    @pl.when(j == 1)
    def _():
        v_ref[...] = res


def _mem_kv(x, g, w, tm):
    m, k = x.shape
    assert w.shape[1] == 2 * D_X
    half = pl.BlockSpec((tm, D_X), lambda i, j: (i, 0))
    return pl.pallas_call(
        _mem_kv_kernel,
        grid=(m // tm, 2),
        in_specs=[
            pl.BlockSpec((tm, k), lambda i, j: (i, 0)),
            pl.BlockSpec((1, k), lambda i, j: (0, 0)),
            pl.BlockSpec((k, D_X), lambda i, j: (0, j)),
        ],
        out_specs=[half, half],
        out_shape=[jax.ShapeDtypeStruct((m, D_X), F32)] * 2,
        scratch_shapes=[pltpu.VMEM((tm, k), BF16)],
        compiler_params=pltpu.CompilerParams(
            dimension_semantics=("arbitrary", "arbitrary"),
            vmem_limit_bytes=VMEM_LIMIT),
        name="mem_kv",
    )(x, g, w)


def kernel(x_prompt, x_sample, mem_prompt, state_rglru_h, state_conv, state_pool, cache_mem_k, cache_mem_v, g_pre, w_in, conv_w, conv_b, w_rg_a, b_rg_a, w_rg_x, b_rg_x, lru_lambda, w_pool, pool_scale, g_mem, w_kv, w_branch, w_out, g_post):
    batch, seq, _ = x_prompt.shape
    nb = x_sample.shape[0]
    depth = g_pre.shape[0]
    assert depth == 1 and x_sample.shape[1] == 1

    l = 0
    row = lambda v: v.reshape(1, -1)
    wax = jnp.concatenate([w_rg_a[l], w_rg_x[l]], axis=-1).astype(BF16)
    wpool = w_pool[l].astype(BF16)
    mix_params = (conv_w[l], row(conv_b[l]), wax, row(b_rg_a[l]), row(b_rg_x[l]),
                  row(lru_lambda[l]), wpool, row(pool_scale[l]))

    xp2 = x_prompt.reshape(batch * seq, D_MODEL)
    xs2 = x_sample.reshape(nb, D_MODEL)
    mem2 = mem_prompt.reshape(batch * N_MEM, D_MODEL)

    z_s, w_in_b = _sample_proj(xs2, row(g_pre[l]), w_in[l], tn=SAMPLE_PROJ_TN)
    attn_s = _sample_attn(z_s, _cache_rows(cache_mem_k[l]), _cache_rows(cache_mem_v[l]), bb=ATTN_BB)

    o_s, h_s, c_s, p_s = _sample_mix(
        z_s, attn_s, state_conv[l].transpose(1, 0, 2), state_rglru_h[l],
        state_pool[l].transpose(1, 0, 2), *mix_params, tb=SAMPLE_MIX_TB)
    y_s, w_br_b, w_out_b = _branch_out_cast(o_s, z_s, xs2, w_branch[l], w_out[l], row(g_post[l]))

    mem_k, mem_v = _mem_kv(mem2, row(g_mem[l]), w_kv[l], tm=KV_TM)
    mem_k = mem_k.reshape(batch, N_MEM, D_X)
    mem_v = mem_v.reshape(batch, N_MEM, D_X)

    perm = _chunk_interleave()
    z_p = _prompt_proj(xp2, row(g_pre[l]), w_in_b, perm, tm=PROJ_TM, tn=PROJ_TN)
    o_p, h_p, c_p, p_p = _prompt_mix(z_p, mem_k, mem_v, *mix_params,
                                     batch=batch, seq=seq, tm=MIX_TM)
    y_p = _branch_out(o_p, z_p, xp2, w_br_b, w_out_b, row(g_post[l]), tm=MIX_TM, unperm=perm.T)

    return (
        y_p.reshape(batch, seq, D_MODEL),
        y_s.reshape(nb, 1, D_MODEL),
        h_p.reshape(1, batch, D_RNN),
        c_p.reshape(1, batch, CONV_W - 1, D_RNN),
        p_p.reshape(1, batch, POOL_HIST, D_POOL),
        mem_k.reshape(1, batch, N_MEM, N_XHEADS, XHEAD_DIM),
        mem_v.reshape(1, batch, N_MEM, N_XHEADS, XHEAD_DIM),
        h_s.reshape(1, nb, D_RNN),
        c_s.transpose(1, 0, 2)[None],
        p_s.transpose(1, 0, 2)[None],
    )
```

```python
import functools

import jax
import jax.numpy as jnp
from jax import lax
from jax.experimental import pallas as pl
from jax.experimental.pallas import tpu as pltpu

D_MODEL = 2048
PAST_LEN = 16384
D_RNN = 1024
N_RNN_BLOCKS = 8
RNN_BLOCK = D_RNN // N_RNN_BLOCKS
CONV_W = 4
LRU_C = 8.0
D_POOL = 1024
POOL_WINDOWS = (2, 4, 8, 16)
POOL_GROUP = D_POOL // len(POOL_WINDOWS)
POOL_HIST = max(POOL_WINDOWS) - 1
N_MEM = 256
N_XHEADS = 4
XHEAD_DIM = 256
D_X = N_XHEADS * XHEAD_DIM
N_BRANCH = 3
D_MIX = D_RNN + D_POOL + D_X
D_IN = 2 * D_MIX + N_BRANCH * D_MODEL
EPS = 1e-6

SUBLANES = 8
LANES = 128
VMEM_LIMIT = 56 * 1024 * 1024
PROJ_VMEM_LIMIT = 60 * 1024 * 1024
MIX_TM = 256
PROJ_TM, PROJ_TN = 1024, 2048
SAMPLE_PROJ_TN = 1024
ATTN_BB = 4
SAMPLE_MIX_TB = 32
KV_TM = 512

BF16 = jnp.bfloat16
F32 = jnp.float32

NEG_LOG2_E = -1.4426950408889634


def _sigmoid(x):
    return 1.0 / (1.0 + jnp.exp2(x * NEG_LOG2_E))


def _silu(x):
    return x * _sigmoid(x)


def _softplus(x):
    return jnp.maximum(x, 0.0) + jnp.log1p(jnp.exp(-jnp.abs(x)))


def _rms_scale(x):
    return lax.rsqrt(jnp.mean(x * x, axis=-1, keepdims=True) + EPS)


def _chunk_interleave():
    nrow = MIX_TM // SUBLANES
    p = jnp.arange(MIX_TM)
    token = (p % SUBLANES) * nrow + p // SUBLANES
    return (token[:, None] == jnp.arange(MIX_TM)[None, :]).astype(BF16)


def _sample_proj_kernel(x_ref, g_ref, w_ref, o_ref, wb_ref, u_ref):
    @pl.when(pl.program_id(0) == 0)
    def _():
        x = x_ref[...]
        u_ref[...] = (x * _rms_scale(x) * g_ref[...]).astype(BF16)

    w = w_ref[...].astype(BF16)
    wb_ref[...] = w
    o_ref[...] = jnp.dot(u_ref[...], w, preferred_element_type=F32)


def _sample_proj(x, g, w, tn):
    m, k = x.shape
    n = w.shape[1]
    return pl.pallas_call(
        _sample_proj_kernel,
        grid=(n // tn,),
        in_specs=[
            pl.BlockSpec((m, k), lambda j: (0, 0)),
            pl.BlockSpec((1, k), lambda j: (0, 0)),
            pl.BlockSpec((k, tn), lambda j: (0, j)),
        ],
        out_specs=[
            pl.BlockSpec((m, tn), lambda j: (0, j)),
            pl.BlockSpec((k, tn), lambda j: (0, j)),
        ],
        out_shape=[
            jax.ShapeDtypeStruct((m, n), F32),
            jax.ShapeDtypeStruct((k, n), BF16),
        ],
        scratch_shapes=[pltpu.VMEM((m, k), BF16)],
        compiler_params=pltpu.CompilerParams(
            dimension_semantics=("arbitrary",),
            vmem_limit_bytes=VMEM_LIMIT),
        name="sample_proj",
    )(x, g, w)


def _decay_rate(lam):
    return _softplus(-lam) * (LRU_C * NEG_LOG2_E)


def _rglru_block(xc, wax, ba, bx, rate):
    ri = jnp.dot(xc.astype(BF16), wax, preferred_element_type=F32)
    r = _sigmoid(ri[:, :RNN_BLOCK] + ba)
    i = _sigmoid(ri[:, RNN_BLOCK:] + bx)
    a = jnp.exp2(r * rate)
    one_m = 1.0 - a * a
    mult = jnp.where(one_m > 0.0, one_m * lax.rsqrt(one_m), 0.0)
    return a, mult * i * xc


def _prompt_mix_kernel(z_ref, k_ref, v_ref, convw_ref, convb_ref, wax_ref, ba_ref, bx_ref,
                       lam_ref, wpool_ref, pscale_ref, sq_ref, sk_ref, sv_ref,
                       o_ref, newh_ref, newconv_ref, newpool_ref, sattn_ref,
                       conv_carry, pool_carry, h_carry, kb_ref, vb_ref, ac_scr, hl_scr, *, tm):
    l = pl.program_id(1)
    last = pl.num_programs(1) - 1
    nrow = tm // SUBLANES

    _sample_attn_block(sq_ref.at[0], sk_ref, sv_ref, sattn_ref.at[0], ATTN_BB)

    @pl.when(l == 0)
    def _():
        conv_carry[...] = jnp.zeros(conv_carry.shape, F32)
        pool_carry[...] = jnp.zeros(pool_carry.shape, F32)
        h_carry[...] = jnp.zeros(h_carry.shape, F32)
        kb_ref[...] = k_ref[0].astype(BF16)
        vb_ref[...] = v_ref[0].astype(BF16)

    chunk_id = lax.broadcasted_iota(jnp.int32, (SUBLANES, LANES), 0)
    first_chunk = chunk_id == 0

    def load_groups(col, width=LANES):
        return [z_ref[r * SUBLANES:(r + 1) * SUBLANES, col:col + width] for r in range(nrow)]

    def store_groups(col, rows, width=LANES):
        o_ref[:, col:col + width] = jnp.concatenate(rows, axis=0).astype(BF16)

    def history(tail_group, carry_ref, j, c0):
        tail = pltpu.roll(tail_group, 1, 0)
        prev = jnp.where(first_chunk, carry_ref[j - 1, :, c0:c0 + LANES], tail)
        carry_ref[j - 1, :, c0:c0 + LANES] = tail
        return prev

    rate = _decay_rate(lam_ref[...])
    for n in range(N_RNN_BLOCKS):
        c0, c1 = n * RNN_BLOCK, (n + 1) * RNN_BLOCK
        xs = load_groups(c0)
        ext = [history(xs[nrow - j], conv_carry, j, c0) for j in range(CONV_W - 1, 0, -1)] + xs
        cw = [jnp.broadcast_to(convw_ref[k:k + 1, c0:c1], (SUBLANES, LANES)) for k in range(CONV_W)]
        cb = jnp.broadcast_to(convb_ref[:, c0:c1], (SUBLANES, LANES))
        xc = []
        for r in range(nrow):
            acc = cb + cw[0] * ext[r]
            for k in range(1, CONV_W):
                acc = acc + cw[k] * ext[r + k]
            xc.append(acc)
        a, b = _rglru_block(jnp.concatenate(xc, axis=0), wax_ref[n], ba_ref[:, c0:c1],
                            bx_ref[:, c0:c1], rate[:, c0:c1])
        ac_scr[:, c0:c1] = a
        hl_scr[:, c0:c1] = b

    acc_a = ac_scr[0:SUBLANES, :]
    acc_h = hl_scr[0:SUBLANES, :]
    for r in range(1, nrow):
        rows = slice(r * SUBLANES, (r + 1) * SUBLANES)
        ar = ac_scr[rows, :]
        acc_h = ar * acc_h + hl_scr[rows, :]
        acc_a = ar * acc_a
        ac_scr[rows, :] = acc_a
        hl_scr[rows, :] = acc_h
    h_in = h_carry[...]
    entering = []
    for c in range(SUBLANES):
        entering.append(h_in)
        h_in = acc_a[c:c + 1] * h_in + acc_h[c:c + 1]
    h_carry[...] = h_in
    h_enter = jnp.concatenate(entering, axis=0)
    for n in range(N_RNN_BLOCKS):
        c0, c1 = n * RNN_BLOCK, (n + 1) * RNN_BLOCK
        gr = load_groups(D_RNN + c0)
        store_groups(c0, [(hl_scr[r * SUBLANES:(r + 1) * SUBLANES, c0:c1]
                           + ac_scr[r * SUBLANES:(r + 1) * SUBLANES, c0:c1] * h_enter[:, c0:c1])
                          * _silu(gr[r]) for r in range(nrow)])

    pcol = 2 * D_RNN
    blocks = [(w, c0) for g, w in enumerate(POOL_WINDOWS)
              for c0 in range(g * POOL_GROUP, (g + 1) * POOL_GROUP, LANES)]

    def group(c0, r):
        return z_ref[r * SUBLANES:(r + 1) * SUBLANES, pcol + c0:pcol + c0 + LANES]

    def mean_minus_token(tot, w, c0, r):
        if r < w - 1:
            pos1 = l * tm + chunk_id * nrow + (r + 1)
            mean = tot / jnp.minimum(pos1, w).astype(F32)
        else:
            mean = tot * (1.0 / w)
        return mean - group(c0, r)

    hist, tot = {}, {}
    for w, c0 in blocks:
        hist[c0] = [history(group(c0, nrow - j), pool_carry, j, c0) for j in range(1, w)]
        t = group(c0, 0)
        for h in hist[c0]:
            t = t + h
        tot[c0] = t
        hl_scr[0:SUBLANES, c0:c0 + LANES] = mean_minus_token(t, w, c0, 0)
    for r in range(1, nrow):
        for w, c0 in blocks:
            leaving = group(c0, r - w) if r >= w else hist[c0][w - r - 1]
            tot[c0] = tot[c0] + (group(c0, r) - leaving)
            hl_scr[r * SUBLANES:(r + 1) * SUBLANES, c0:c0 + LANES] = mean_minus_token(
                tot[c0], w, c0, r)
    for g, w in enumerate(POOL_WINDOWS):
        c0, c1 = g * POOL_GROUP, (g + 1) * POOL_GROUP
        og = jnp.dot(hl_scr[:, c0:c1].astype(BF16), wpool_ref[g], preferred_element_type=F32)
        gp = z_ref[:, pcol + D_POOL + c0:pcol + D_POOL + c1]
        o_ref[:, D_RNN + c0:D_RNN + c1] = (og * pscale_ref[:, c0:c1] * _silu(gp)).astype(BF16)

    qoff = 2 * D_RNN + 2 * D_POOL
    for hd in range(N_XHEADS):
        c0, c1 = hd * XHEAD_DIM, (hd + 1) * XHEAD_DIM
        q = z_ref[:, qoff + c0:qoff + c1].astype(BF16)
        s = lax.dot_general(q, kb_ref[:, c0:c1], (((1,), (1,)), ((), ())),
                            preferred_element_type=F32) * (XHEAD_DIM ** -0.5)
        p = jnp.exp(s - jnp.max(s, axis=-1, keepdims=True))
        p = p / jnp.sum(p, axis=-1, keepdims=True)
        ox = jnp.dot(p.astype(BF16), vb_ref[:, c0:c1], preferred_element_type=F32)
        gx = z_ref[:, qoff + D_X + c0:qoff + D_X + c1]
        o_ref[:, D_RNN + D_POOL + c0:D_RNN + D_POOL + c1] = (ox * _silu(gx)).astype(BF16)

    @pl.when(l == last)
    def _():
        newh_ref[0] = h_carry[...]
        tail_row = lambda j: (nrow - j) * SUBLANES + SUBLANES - 1
        for j in range(1, CONV_W):
            newconv_ref[0, CONV_W - 1 - j:CONV_W - j, :] = z_ref[tail_row(j):tail_row(j) + 1, 0:D_RNN]
        for j in range(1, POOL_HIST + 1):
            newpool_ref[0, POOL_HIST - j:POOL_HIST - j + 1, :] = (
                z_ref[tail_row(j):tail_row(j) + 1, pcol:pcol + D_POOL])


def _prompt_mix(z, mem_k, mem_v, conv_w, conv_b, wax, b_a, b_x, lam, wpool, pscale,
                sample_q, cache_k, cache_v, batch, seq, tm):
    nl = seq // tm
    assert sample_q.shape[0] == batch * nl
    side = lambda b, l: (b * nl + l, 0, 0)
    zw = 2 * D_MIX
    const2 = lambda b, l: (0, 0)
    const3 = lambda b, l: (0, 0, 0)
    kern = functools.partial(_prompt_mix_kernel, tm=tm)
    return pl.pallas_call(
        kern,
        grid=(batch, nl),
        in_specs=[
            pl.BlockSpec((tm, zw), lambda b, l: (b * nl + l, 0)),
            pl.BlockSpec((1, N_MEM, D_X), lambda b, l: (b, 0, 0)),
            pl.BlockSpec((1, N_MEM, D_X), lambda b, l: (b, 0, 0)),
            pl.BlockSpec((CONV_W, D_RNN), const2),
            pl.BlockSpec((1, D_RNN), const2),
            pl.BlockSpec((N_RNN_BLOCKS, RNN_BLOCK, 2 * RNN_BLOCK), const3),
            pl.BlockSpec((1, D_RNN), const2),
            pl.BlockSpec((1, D_RNN), const2),
            pl.BlockSpec((1, D_RNN), const2),
            pl.BlockSpec((len(POOL_WINDOWS), POOL_GROUP, POOL_GROUP), const3),
            pl.BlockSpec((1, D_POOL), const2),
            pl.BlockSpec((1, ATTN_BB, D_X), side),
            pl.BlockSpec((ATTN_BB, N_MEM * SUBLANES, LANES), side),
            pl.BlockSpec((ATTN_BB, N_MEM * SUBLANES, LANES), side),
        ],
        out_specs=[
            pl.BlockSpec((tm, D_MIX), lambda b, l: (b * nl + l, 0)),
            pl.BlockSpec((1, 1, D_RNN), lambda b, l: (b, 0, 0)),
            pl.BlockSpec((1, CONV_W - 1, D_RNN), lambda b, l: (b, 0, 0)),
            pl.BlockSpec((1, POOL_HIST, D_POOL), lambda b, l: (b, 0, 0)),
            pl.BlockSpec((1, ATTN_BB, D_X), side),
        ],
        out_shape=[
            jax.ShapeDtypeStruct((batch * seq, D_MIX), BF16),
            jax.ShapeDtypeStruct((batch, 1, D_RNN), F32),
            jax.ShapeDtypeStruct((batch, CONV_W - 1, D_RNN), F32),
            jax.ShapeDtypeStruct((batch, POOL_HIST, D_POOL), F32),
            jax.ShapeDtypeStruct(sample_q.shape, F32),
        ],
        scratch_shapes=[
            pltpu.VMEM((CONV_W - 1, SUBLANES, D_RNN), F32),
            pltpu.VMEM((POOL_HIST, SUBLANES, D_POOL), F32),
            pltpu.VMEM((1, D_RNN), F32),
            pltpu.VMEM((N_MEM, D_X), BF16),
            pltpu.VMEM((N_MEM, D_X), BF16),
            pltpu.VMEM((tm, D_RNN), F32),
            pltpu.VMEM((tm, D_RNN), F32),
        ],
        compiler_params=pltpu.CompilerParams(
            dimension_semantics=("arbitrary", "arbitrary"),
            vmem_limit_bytes=VMEM_LIMIT),
        name="prompt_mix",
    )(z, mem_k, mem_v, conv_w, conv_b, wax, b_a, b_x, lam, wpool, pscale,
      sample_q, cache_k, cache_v)


def _cache_rows(c):
    nb = c.shape[0]
    c = c.reshape(nb, N_MEM, N_XHEADS, XHEAD_DIM // LANES, LANES)
    return c.transpose(0, 1, 3, 2, 4).reshape(nb, N_MEM * SUBLANES, LANES)


def _sample_attn_block(q_ref, k_ref, v_ref, o_ref, bb):
    halves = XHEAD_DIM // LANES
    assert halves * N_XHEADS == SUBLANES
    r = lax.broadcasted_iota(jnp.int32, (SUBLANES, LANES), 0)
    c = lax.broadcasted_iota(jnp.int32, (SUBLANES, LANES), 1)
    diag = (c % SUBLANES) == r
    first_half = r < N_XHEADS
    nchunk = N_MEM * SUBLANES // LANES
    scores = []
    for j in range(bb):
        qn = jnp.concatenate(
            [q_ref[j:j + 1, (h * halves + t) * LANES:(h * halves + t + 1) * LANES]
             for t in range(halves) for h in range(N_XHEADS)], axis=0)
        scores.append(lax.dot_general(qn.astype(BF16), k_ref[j].astype(BF16),
                                      (((1,), (1,)), ((), ())), preferred_element_type=F32)
                      * (XHEAD_DIM ** -0.5))
    probs = []
    for j in range(bb):
        s = scores[j]
        chunks = []
        for ci in range(nchunk):
            sm = jnp.where(diag, s[:, ci * LANES:(ci + 1) * LANES], 0.0)
            other = pltpu.roll(sm, N_XHEADS, 0)
            other = jnp.where(first_half, pltpu.roll(other, LANES - N_XHEADS, 1),
                              pltpu.roll(other, N_XHEADS, 1))
            chunks.append(jnp.where(diag, sm + other, -jnp.inf))
        t_full = jnp.concatenate(chunks, axis=1)
        e = jnp.exp(t_full - jnp.max(t_full, axis=1, keepdims=True))
        probs.append((e / jnp.sum(e, axis=1, keepdims=True)).astype(BF16))
    for j in range(bb):
        o = jnp.dot(probs[j], v_ref[j].astype(BF16), preferred_element_type=F32)
        for t in range(halves):
            for h in range(N_XHEADS):
                col = (h * halves + t) * LANES
                o_ref[j:j + 1, col:col + LANES] = o[t * N_XHEADS + h:t * N_XHEADS + h + 1, :]


def _prompt_proj_kernel(x_ref, g_ref, w_ref, perm_ref, o_ref, u_ref):
    @pl.when(pl.program_id(1) == 0)
    def _():
        x = x_ref[...]
        u = (x * _rms_scale(x) * g_ref[...]).astype(BF16)
        for r0 in range(0, u.shape[0], MIX_TM):
            u_ref[r0:r0 + MIX_TM, :] = jnp.dot(
                perm_ref[...], u[r0:r0 + MIX_TM], preferred_element_type=F32).astype(BF16)

    o_ref[...] = jnp.dot(u_ref[...], w_ref[...], preferred_element_type=F32)


def _prompt_proj(x, g, w, perm, tm, tn):
    m, k = x.shape
    n = w.shape[1]
    return pl.pallas_call(
        _prompt_proj_kernel,
        grid=(m // tm, n // tn),
        in_specs=[
            pl.BlockSpec((tm, k), lambda i, j: (i, 0)),
            pl.BlockSpec((1, k), lambda i, j: (0, 0)),
            pl.BlockSpec((k, tn), lambda i, j: (0, j)),
            pl.BlockSpec(perm.shape, lambda i, j: (0, 0)),
        ],
        out_specs=pl.BlockSpec((tm, tn), lambda i, j: (i, j)),
        out_shape=jax.ShapeDtypeStruct((m, n), F32),
        scratch_shapes=[pltpu.VMEM((tm, k), BF16)],
        compiler_params=pltpu.CompilerParams(
            dimension_semantics=("arbitrary", "arbitrary"),
            vmem_limit_bytes=PROJ_VMEM_LIMIT),
        name="prompt_proj",
    )(x, g, w, perm)


def _sample_attn_kernel(q_ref, k_ref, v_ref, o_ref, *, bb):
    _sample_attn_block(q_ref, k_ref, v_ref, o_ref, bb)


def _sample_attn(z, cache_k, cache_v, bb):
    nb = z.shape[0]
    qblk = (2 * D_RNN + 2 * D_POOL) // D_X
    return pl.pallas_call(
        functools.partial(_sample_attn_kernel, bb=bb),
        grid=(nb // bb,),
        in_specs=[
            pl.BlockSpec((bb, D_X), lambda i: (i, qblk)),
            pl.BlockSpec((bb, N_MEM * SUBLANES, LANES), lambda i: (i, 0, 0)),
            pl.BlockSpec((bb, N_MEM * SUBLANES, LANES), lambda i: (i, 0, 0)),
        ],
        out_specs=pl.BlockSpec((bb, D_X), lambda i: (i, 0)),
        out_shape=jax.ShapeDtypeStruct((nb, D_X), F32),
        compiler_params=pltpu.CompilerParams(
            dimension_semantics=("arbitrary",),
            vmem_limit_bytes=VMEM_LIMIT),
        name="sample_attn",
    )(z, cache_k, cache_v)


def _sample_mix_kernel(z_ref, attn_ref, conv_ref, h_ref, pool_ref,
                       convw_ref, convb_ref, wax_ref, ba_ref, bx_ref, lam_ref, wpool_ref,
                       pscale_ref, o_ref, newh_ref, newconv_ref, newpool_ref):
    xr = z_ref[:, 0:D_RNN]
    xc = convb_ref[...] + convw_ref[CONV_W - 1:CONV_W, :] * xr
    for k in range(CONV_W - 1):
        xc = xc + convw_ref[k:k + 1, :] * conv_ref[k]
    for k in range(CONV_W - 2):
        newconv_ref[k] = conv_ref[k + 1]
    newconv_ref[CONV_W - 2] = xr

    rate = _decay_rate(lam_ref[...])
    for n in range(N_RNN_BLOCKS):
        c0, c1 = n * RNN_BLOCK, (n + 1) * RNN_BLOCK
        a, b = _rglru_block(xc[:, c0:c1], wax_ref[n], ba_ref[:, c0:c1], bx_ref[:, c0:c1],
                            rate[:, c0:c1])
        h = a * h_ref[:, c0:c1] + b
        newh_ref[:, c0:c1] = h
        o_ref[:, c0:c1] = (h * _silu(z_ref[:, D_RNN + c0:D_RNN + c1])).astype(BF16)

    xp = z_ref[:, 2 * D_RNN:2 * D_RNN + D_POOL]
    for k in range(POOL_HIST - 1):
        newpool_ref[k] = pool_ref[k + 1]
    newpool_ref[POOL_HIST - 1] = xp
    for g, w in enumerate(POOL_WINDOWS):
        c0, c1 = g * POOL_GROUP, (g + 1) * POOL_GROUP
        xg = xp[:, c0:c1]
        tot = xg
        for j in range(1, w):
            tot = tot + pool_ref[POOL_HIST - j, :, c0:c1]
        cnt = float(min(PAST_LEN + 1, w))
        d = tot / cnt - xg
        og = jnp.dot(d.astype(BF16), wpool_ref[g], preferred_element_type=F32)
        gp = z_ref[:, 2 * D_RNN + D_POOL + c0:2 * D_RNN + D_POOL + c1]
        o_ref[:, D_RNN + c0:D_RNN + c1] = (og * pscale_ref[:, c0:c1] * _silu(gp)).astype(BF16)

    gx = z_ref[:, 2 * D_RNN + 2 * D_POOL + D_X:2 * D_MIX]
    o_ref[:, D_RNN + D_POOL:] = (attn_ref[...] * _silu(gx)).astype(BF16)


def _sample_mix(z, attn, conv, h, pool, conv_w, conv_b, wax, b_a, b_x, lam, wpool, pscale, tb):
    nb = z.shape[0]
    zw = 2 * D_MIX
    rows = lambda i: (i, 0)
    const2 = lambda i: (0, 0)
    const3 = lambda i: (0, 0, 0)
    hist = lambda i: (0, i, 0)
    return pl.pallas_call(
        _sample_mix_kernel,
        grid=(nb // tb,),
        in_specs=[
            pl.BlockSpec((tb, zw), rows),
            pl.BlockSpec((tb, D_X), rows),
            pl.BlockSpec((CONV_W - 1, tb, D_RNN), hist),
            pl.BlockSpec((tb, D_RNN), rows),
            pl.BlockSpec((POOL_HIST, tb, D_POOL), hist),
            pl.BlockSpec((CONV_W, D_RNN), const2),
            pl.BlockSpec((1, D_RNN), const2),
            pl.BlockSpec((N_RNN_BLOCKS, RNN_BLOCK, 2 * RNN_BLOCK), const3),
            pl.BlockSpec((1, D_RNN), const2),
            pl.BlockSpec((1, D_RNN), const2),
            pl.BlockSpec((1, D_RNN), const2),
            pl.BlockSpec((len(POOL_WINDOWS), POOL_GROUP, POOL_GROUP), const3),
            pl.BlockSpec((1, D_POOL), const2),
        ],
        out_specs=[
            pl.BlockSpec((tb, D_MIX), rows),
            pl.BlockSpec((tb, D_RNN), rows),
            pl.BlockSpec((CONV_W - 1, tb, D_RNN), hist),
            pl.BlockSpec((POOL_HIST, tb, D_POOL), hist),
        ],
        out_shape=[
            jax.ShapeDtypeStruct((nb, D_MIX), BF16),
            jax.ShapeDtypeStruct((nb, D_RNN), F32),
            jax.ShapeDtypeStruct((CONV_W - 1, nb, D_RNN), F32),
            jax.ShapeDtypeStruct((POOL_HIST, nb, D_POOL), F32),
        ],
        compiler_params=pltpu.CompilerParams(
            dimension_semantics=("arbitrary",),
            vmem_limit_bytes=VMEM_LIMIT),
        name="sample_mix",
    )(z, attn, conv, h, pool, conv_w, conv_b, wax, b_a, b_x, lam, wpool, pscale)


def _branch_out_kernel(o_ref, gates_ref, x_ref, wb_ref, wo_ref, gpost_ref, *rest, interleaved):
    y_ref = rest[-1]
    merged = None
    for j, (r0, r1) in enumerate(((0, D_RNN), (D_RNN, D_RNN + D_POOL), (D_RNN + D_POOL, D_MIX))):
        yj = jnp.dot(o_ref[:, r0:r1], wb_ref[r0:r1, :], preferred_element_type=F32)
        term = _sigmoid(gates_ref[:, j * D_MODEL:(j + 1) * D_MODEL]) * yj
        merged = term if merged is None else merged + term
    merged = merged.astype(BF16)
    if interleaved:
        merged = jnp.dot(rest[0][...], merged, preferred_element_type=F32).astype(BF16)
    out = jnp.dot(merged, wo_ref[...], preferred_element_type=F32)
    y_ref[...] = x_ref[...] + (out * gpost_ref[...]) * _rms_scale(out)


def _branch_out(o, z, x, wb, wo, g_post, tm, unperm=None):
    m = x.shape[0]
    gw = N_BRANCH * D_MODEL
    gblk = (2 * D_MIX) // gw
    resident = pl.Buffered(1)
    in_specs = [
        pl.BlockSpec((tm, D_MIX), lambda i: (i, 0)),
        pl.BlockSpec((tm, gw), lambda i: (i, gblk)),
        pl.BlockSpec((tm, D_MODEL), lambda i: (i, 0)),
        pl.BlockSpec((D_MIX, D_MODEL), lambda i: (0, 0), pipeline_mode=resident),
        pl.BlockSpec((D_MODEL, D_MODEL), lambda i: (0, 0), pipeline_mode=resident),
        pl.BlockSpec((1, D_MODEL), lambda i: (0, 0)),
    ]
    args = [o, z, x, wb, wo, g_post]
    if unperm is not None:
        assert unperm.shape == (tm, tm)
        in_specs.append(pl.BlockSpec(unperm.shape, lambda i: (0, 0)))
        args.append(unperm)
    return pl.pallas_call(
        functools.partial(_branch_out_kernel, interleaved=unperm is not None),
        grid=(m // tm,),
        in_specs=in_specs,
        out_specs=pl.BlockSpec((tm, D_MODEL), lambda i: (i, 0)),
        out_shape=jax.ShapeDtypeStruct((m, D_MODEL), F32),
        compiler_params=pltpu.CompilerParams(
            dimension_semantics=("arbitrary",),
            vmem_limit_bytes=VMEM_LIMIT),
        name="branch_out",
    )(*args)


WROWS = 512
PER_BRANCH = D_RNN // WROWS
assert D_RNN == D_POOL == D_X and D_RNN % WROWS == 0 and D_MODEL % WROWS == 0
N_WB_BLOCKS = N_BRANCH * PER_BRANCH
N_WOUT_BLOCKS = D_MODEL // WROWS


def _branch_out_cast_kernel(o_ref, gates_ref, x_ref, wb_ref, wo_ref, gpost_ref,
                            y_ref, wbb_ref, wob_ref, merged_ref, out_ref):
    s = pl.program_id(0)

    @pl.when(s < N_WB_BLOCKS)
    def _():
        w = wb_ref[...].astype(BF16)
        wbb_ref[...] = w
        term = _sigmoid(gates_ref[...]) * jnp.dot(o_ref[...], w, preferred_element_type=F32)

        @pl.when(s == 0)
        def _():
            merged_ref[...] = term

        @pl.when(s > 0)
        def _():
            merged_ref[...] += term

    for kb in range(N_WOUT_BLOCKS):
        @pl.when(s == N_WB_BLOCKS + kb)
        def _(kb=kb):
            w = wo_ref[...].astype(BF16)
            wob_ref[...] = w
            part = jnp.dot(merged_ref[:, kb * WROWS:(kb + 1) * WROWS].astype(BF16), w,
                           preferred_element_type=F32)
            if kb == 0:
                out_ref[...] = part
            else:
                out_ref[...] += part

    @pl.when(s == N_WB_BLOCKS + N_WOUT_BLOCKS - 1)
    def _():
        out = out_ref[...]
        y_ref[...] = x_ref[...] + out * _rms_scale(out) * gpost_ref[...]


def _branch_out_cast(o, z, x, wb, wo, g_post):
    m = x.shape[0]
    gblk0 = (2 * D_MIX) // D_MODEL
    wb_blk = lambda s: jnp.minimum(s, N_WB_BLOCKS - 1)
    wo_blk = lambda s: jnp.maximum(s - N_WB_BLOCKS, 0)
    return pl.pallas_call(
        _branch_out_cast_kernel,
        grid=(N_WB_BLOCKS + N_WOUT_BLOCKS,),
        in_specs=[
            pl.BlockSpec((m, WROWS), lambda s: (0, wb_blk(s))),
            pl.BlockSpec((m, D_MODEL), lambda s: (0, gblk0 + wb_blk(s) // PER_BRANCH)),
            pl.BlockSpec((m, D_MODEL), lambda s: (0, 0)),
            pl.BlockSpec((WROWS, D_MODEL), lambda s: (wb_blk(s), 0)),
            pl.BlockSpec((WROWS, D_MODEL), lambda s: (wo_blk(s), 0)),
            pl.BlockSpec((1, D_MODEL), lambda s: (0, 0)),
        ],
        out_specs=[
            pl.BlockSpec((m, D_MODEL), lambda s: (0, 0)),
            pl.BlockSpec((WROWS, D_MODEL), lambda s: (wb_blk(s), 0)),
            pl.BlockSpec((WROWS, D_MODEL), lambda s: (wo_blk(s), 0)),
        ],
        out_shape=[
            jax.ShapeDtypeStruct((m, D_MODEL), F32),
            jax.ShapeDtypeStruct(wb.shape, BF16),
            jax.ShapeDtypeStruct(wo.shape, BF16),
        ],
        scratch_shapes=[pltpu.VMEM((m, D_MODEL), F32), pltpu.VMEM((m, D_MODEL), F32)],
        compiler_params=pltpu.CompilerParams(
            dimension_semantics=("arbitrary",),
            vmem_limit_bytes=VMEM_LIMIT),
        name="branch_out_cast",
    )(o, z, x, wb, wo, g_post)


def _mem_kv_kernel(x_ref, g_ref, w_ref, k_ref, v_ref, u_ref):
    j = pl.program_id(1)

    @pl.when(j == 0)
    def _():
        x = x_ref[...]
        u_ref[...] = (x * _rms_scale(x) * g_ref[...]).astype(BF16)

    res = jnp.dot(u_ref[...], w_ref[...].astype(BF16), preferred_element_type=F32)

    @pl.when(j == 0)
    def _():
        k_ref[...] = res

    @pl.when(j == 1)
    def _():
        v_ref[...] = res


def _mem_kv(x, g, w, tm):
    m, k = x.shape
    assert w.shape[1] == 2 * D_X
    half = pl.BlockSpec((tm, D_X), lambda i, j: (i, 0))
    return pl.pallas_call(
        _mem_kv_kernel,
        grid=(m // tm, 2),
        in_specs=[
            pl.BlockSpec((tm, k), lambda i, j: (i, 0)),
            pl.BlockSpec((1, k), lambda i, j: (0, 0)),
            pl.BlockSpec((k, D_X), lambda i, j: (0, j)),
        ],
        out_specs=[half, half],
        out_shape=[jax.ShapeDtypeStruct((m, D_X), F32)] * 2,
        scratch_shapes=[pltpu.VMEM((tm, k), BF16)],
        compiler_params=pltpu.CompilerParams(
            dimension_semantics=("arbitrary", "arbitrary"),
            vmem_limit_bytes=VMEM_LIMIT),
        name="mem_kv",
    )(x, g, w)


def kernel(x_prompt, x_sample, mem_prompt, state_rglru_h, state_conv, state_pool, cache_mem_k, cache_mem_v, g_pre, w_in, conv_w, conv_b, w_rg_a, b_rg_a, w_rg_x, b_rg_x, lru_lambda, w_pool, pool_scale, g_mem, w_kv, w_branch, w_out, g_post):
    batch, seq, _ = x_prompt.shape
    nb = x_sample.shape[0]
    depth = g_pre.shape[0]
    assert depth == 1 and x_sample.shape[1] == 1

    l = 0
    row = lambda v: v.reshape(1, -1)
    wax = jnp.concatenate([w_rg_a[l], w_rg_x[l]], axis=-1).astype(BF16)
    wpool = w_pool[l].astype(BF16)
    mix_params = (conv_w[l], row(conv_b[l]), wax, row(b_rg_a[l]), row(b_rg_x[l]),
                  row(lru_lambda[l]), wpool, row(pool_scale[l]))

    xp2 = x_prompt.reshape(batch * seq, D_MODEL)
    xs2 = x_sample.reshape(nb, D_MODEL)
    mem2 = mem_prompt.reshape(batch * N_MEM, D_MODEL)

    z_s, w_in_b = _sample_proj(xs2, row(g_pre[l]), w_in[l], tn=SAMPLE_PROJ_TN)
    qoff = 2 * D_RNN + 2 * D_POOL
    q_s = z_s[:, qoff:qoff + D_X].reshape(nb // ATTN_BB, ATTN_BB, D_X)

    mem_k, mem_v = _mem_kv(mem2, row(g_mem[l]), w_kv[l], tm=KV_TM)
    mem_k = mem_k.reshape(batch, N_MEM, D_X)
    mem_v = mem_v.reshape(batch, N_MEM, D_X)

    perm = _chunk_interleave()
    z_p = _prompt_proj(xp2, row(g_pre[l]), w_in_b, perm, tm=PROJ_TM, tn=PROJ_TN)
    o_p, h_p, c_p, p_p, attn_s = _prompt_mix(
        z_p, mem_k, mem_v, *mix_params, q_s, _cache_rows(cache_mem_k[l]),
        _cache_rows(cache_mem_v[l]), batch=batch, seq=seq, tm=MIX_TM)
    attn_s = attn_s.reshape(nb, D_X)

    o_s, h_s, c_s, p_s = _sample_mix(
        z_s, attn_s, state_conv[l].transpose(1, 0, 2), state_rglru_h[l],
        state_pool[l].transpose(1, 0, 2), *mix_params, tb=SAMPLE_MIX_TB)
    y_s, w_br_b, w_out_b = _branch_out_cast(o_s, z_s, xs2, w_branch[l], w_out[l], row(g_post[l]))

    y_p = _branch_out(o_p, z_p, xp2, w_br_b, w_out_b, row(g_post[l]), tm=MIX_TM, unperm=perm.T)

    return (
        y_p.reshape(batch, seq, D_MODEL),
        y_s.reshape(nb, 1, D_MODEL),
        h_p.reshape(1, batch, D_RNN),
        c_p.reshape(1, batch, CONV_W - 1, D_RNN),
        p_p.reshape(1, batch, POOL_HIST, D_POOL),
        mem_k.reshape(1, batch, N_MEM, N_XHEADS, XHEAD_DIM),
        mem_v.reshape(1, batch, N_MEM, N_XHEADS, XHEAD_DIM),
        h_s.reshape(1, nb, D_RNN),
        c_s.transpose(1, 0, 2)[None],
        p_s.transpose(1, 0, 2)[None],
    )
```

```python
import functools

import jax
import jax.numpy as jnp
from jax import lax
from jax.experimental import pallas as pl
from jax.experimental.pallas import tpu as pltpu

D_MODEL = 2048
PAST_LEN = 16384
D_RNN = 1024
N_RNN_BLOCKS = 8
RNN_BLOCK = D_RNN // N_RNN_BLOCKS
CONV_W = 4
LRU_C = 8.0
D_POOL = 1024
POOL_WINDOWS = (2, 4, 8, 16)
POOL_GROUP = D_POOL // len(POOL_WINDOWS)
POOL_HIST = max(POOL_WINDOWS) - 1
N_MEM = 256
N_XHEADS = 4
XHEAD_DIM = 256
D_X = N_XHEADS * XHEAD_DIM
N_BRANCH = 3
D_MIX = D_RNN + D_POOL + D_X
D_IN = 2 * D_MIX + N_BRANCH * D_MODEL
EPS = 1e-6

SUBLANES = 8
LANES = 128
VMEM_LIMIT = 56 * 1024 * 1024
PROJ_VMEM_LIMIT = 60 * 1024 * 1024
MIX_TM = 256
PROJ_TM, PROJ_TN = 1024, 2048
SAMPLE_PROJ_TN = 1024
ATTN_BB = 4
SAMPLE_MIX_TB = 32
KV_TM = 512

BF16 = jnp.bfloat16
F32 = jnp.float32

NEG_LOG2_E = -1.4426950408889634


def _sigmoid(x):
    return 1.0 / (1.0 + jnp.exp2(x * NEG_LOG2_E))


def _silu(x):
    return x * _sigmoid(x)


def _softplus(x):
    return jnp.maximum(x, 0.0) + jnp.log1p(jnp.exp(-jnp.abs(x)))


def _rms_scale(x):
    return lax.rsqrt(jnp.mean(x * x, axis=-1, keepdims=True) + EPS)


def _chunk_interleave():
    nrow = MIX_TM // SUBLANES
    p = jnp.arange(MIX_TM)
    token = (p % SUBLANES) * nrow + p // SUBLANES
    return (token[:, None] == jnp.arange(MIX_TM)[None, :]).astype(BF16)


def _sample_proj_kernel(x_ref, g_ref, w_ref, o_ref, wb_ref, u_ref):
    @pl.when(pl.program_id(0) == 0)
    def _():
        x = x_ref[...]
        u_ref[...] = (x * _rms_scale(x) * g_ref[...]).astype(BF16)

    w = w_ref[...].astype(BF16)
    wb_ref[...] = w
    o_ref[...] = jnp.dot(u_ref[...], w, preferred_element_type=F32)


def _sample_proj(x, g, w, tn):
    m, k = x.shape
    n = w.shape[1]
    return pl.pallas_call(
        _sample_proj_kernel,
        grid=(n // tn,),
        in_specs=[
            pl.BlockSpec((m, k), lambda j: (0, 0)),
            pl.BlockSpec((1, k), lambda j: (0, 0)),
            pl.BlockSpec((k, tn), lambda j: (0, j)),
        ],
        out_specs=[
            pl.BlockSpec((m, tn), lambda j: (0, j)),
            pl.BlockSpec((k, tn), lambda j: (0, j)),
        ],
        out_shape=[
            jax.ShapeDtypeStruct((m, n), F32),
            jax.ShapeDtypeStruct((k, n), BF16),
        ],
        scratch_shapes=[pltpu.VMEM((m, k), BF16)],
        compiler_params=pltpu.CompilerParams(
            dimension_semantics=("arbitrary",),
            vmem_limit_bytes=VMEM_LIMIT),
        name="sample_proj",
    )(x, g, w)


def _decay_rate(lam):
    return _softplus(-lam) * (LRU_C * NEG_LOG2_E)


def _rglru_block(xc, wax, ba, bx, rate):
    ri = jnp.dot(xc.astype(BF16), wax, preferred_element_type=F32)
    r = _sigmoid(ri[:, :RNN_BLOCK] + ba)
    i = _sigmoid(ri[:, RNN_BLOCK:] + bx)
    a = jnp.exp2(r * rate)
    one_m = 1.0 - a * a
    mult = jnp.where(one_m > 0.0, one_m * lax.rsqrt(one_m), 0.0)
    return a, mult * i * xc


def _prompt_mix_kernel(z_ref, k_ref, v_ref, convw_ref, convb_ref, wax_ref, ba_ref, bx_ref,
                       lam_ref, wpool_ref, pscale_ref, sq_ref, sk_ref, sv_ref,
                       o_ref, newh_ref, newconv_ref, newpool_ref, sattn_ref,
                       conv_carry, pool_carry, h_carry, kb_ref, vb_ref, ac_scr, hl_scr, *, tm):
    l = pl.program_id(1)
    last = pl.num_programs(1) - 1
    nrow = tm // SUBLANES

    _sample_attn_block(sq_ref.at[0], sk_ref, sv_ref, sattn_ref.at[0], ATTN_BB)

    @pl.when(l == 0)
    def _():
        conv_carry[...] = jnp.zeros(conv_carry.shape, F32)
        pool_carry[...] = jnp.zeros(pool_carry.shape, F32)
        h_carry[...] = jnp.zeros(h_carry.shape, F32)
        kb_ref[...] = k_ref[0].astype(BF16)
        vb_ref[...] = v_ref[0].astype(BF16)

    chunk_id = lax.broadcasted_iota(jnp.int32, (SUBLANES, LANES), 0)
    first_chunk = chunk_id == 0

    def load_groups(col, width=LANES):
        return [z_ref[r * SUBLANES:(r + 1) * SUBLANES, col:col + width] for r in range(nrow)]

    def store_groups(col, rows, width=LANES):
        o_ref[:, col:col + width] = jnp.concatenate(rows, axis=0).astype(BF16)

    def history(tail_group, carry_ref, j, c0):
        tail = pltpu.roll(tail_group, 1, 0)
        prev = jnp.where(first_chunk, carry_ref[j - 1, :, c0:c0 + LANES], tail)
        carry_ref[j - 1, :, c0:c0 + LANES] = tail
        return prev

    rate = _decay_rate(lam_ref[...])
    for n in range(N_RNN_BLOCKS):
        c0, c1 = n * RNN_BLOCK, (n + 1) * RNN_BLOCK
        xs = load_groups(c0)
        ext = [history(xs[nrow - j], conv_carry, j, c0) for j in range(CONV_W - 1, 0, -1)] + xs
        cw = [jnp.broadcast_to(convw_ref[k:k + 1, c0:c1], (SUBLANES, LANES)) for k in range(CONV_W)]
        cb = jnp.broadcast_to(convb_ref[:, c0:c1], (SUBLANES, LANES))
        xc = []
        for r in range(nrow):
            acc = cb + cw[0] * ext[r]
            for k in range(1, CONV_W):
                acc = acc + cw[k] * ext[r + k]
            xc.append(acc)
        a, b = _rglru_block(jnp.concatenate(xc, axis=0), wax_ref[n], ba_ref[:, c0:c1],
                            bx_ref[:, c0:c1], rate[:, c0:c1])
        ac_scr[:, c0:c1] = a
        hl_scr[:, c0:c1] = b

    acc_a = ac_scr[0:SUBLANES, :]
    acc_h = hl_scr[0:SUBLANES, :]
    for r in range(1, nrow):
        rows = slice(r * SUBLANES, (r + 1) * SUBLANES)
        ar = ac_scr[rows, :]
        acc_h = ar * acc_h + hl_scr[rows, :]
        acc_a = ar * acc_a
        ac_scr[rows, :] = acc_a
        hl_scr[rows, :] = acc_h
    h_in = h_carry[...]
    entering = []
    for c in range(SUBLANES):
        entering.append(h_in)
        h_in = acc_a[c:c + 1] * h_in + acc_h[c:c + 1]
    h_carry[...] = h_in
    h_enter = jnp.concatenate(entering, axis=0)
    for n in range(N_RNN_BLOCKS):
        c0, c1 = n * RNN_BLOCK, (n + 1) * RNN_BLOCK
        gr = load_groups(D_RNN + c0)
        store_groups(c0, [(hl_scr[r * SUBLANES:(r + 1) * SUBLANES, c0:c1]
                           + ac_scr[r * SUBLANES:(r + 1) * SUBLANES, c0:c1] * h_enter[:, c0:c1])
                          * _silu(gr[r]) for r in range(nrow)])

    pcol = 2 * D_RNN
    blocks = [(w, c0) for g, w in enumerate(POOL_WINDOWS)
              for c0 in range(g * POOL_GROUP, (g + 1) * POOL_GROUP, LANES)]

    def group(c0, r):
        return z_ref[r * SUBLANES:(r + 1) * SUBLANES, pcol + c0:pcol + c0 + LANES]

    def mean_minus_token(tot, w, c0, r):
        if r < w - 1:
            pos1 = l * tm + chunk_id * nrow + (r + 1)
            mean = tot / jnp.minimum(pos1, w).astype(F32)
        else:
            mean = tot * (1.0 / w)
        return mean - group(c0, r)

    hist, tot = {}, {}
    for w, c0 in blocks:
        hist[c0] = [history(group(c0, nrow - j), pool_carry, j, c0) for j in range(1, w)]
        t = group(c0, 0)
        for h in hist[c0]:
            t = t + h
        tot[c0] = t
        hl_scr[0:SUBLANES, c0:c0 + LANES] = mean_minus_token(t, w, c0, 0)
    for r in range(1, nrow):
        for w, c0 in blocks:
            leaving = group(c0, r - w) if r >= w else hist[c0][w - r - 1]
            tot[c0] = tot[c0] + (group(c0, r) - leaving)
            hl_scr[r * SUBLANES:(r + 1) * SUBLANES, c0:c0 + LANES] = mean_minus_token(
                tot[c0], w, c0, r)
    for g, w in enumerate(POOL_WINDOWS):
        c0, c1 = g * POOL_GROUP, (g + 1) * POOL_GROUP
        og = jnp.dot(hl_scr[:, c0:c1].astype(BF16), wpool_ref[g], preferred_element_type=F32)
        gp = z_ref[:, pcol + D_POOL + c0:pcol + D_POOL + c1]
        o_ref[:, D_RNN + c0:D_RNN + c1] = (og * pscale_ref[:, c0:c1] * _silu(gp)).astype(BF16)

    qoff = 2 * D_RNN + 2 * D_POOL
    for hd in range(N_XHEADS):
        c0, c1 = hd * XHEAD_DIM, (hd + 1) * XHEAD_DIM
        q = z_ref[:, qoff + c0:qoff + c1].astype(BF16)
        s = lax.dot_general(q, kb_ref[:, c0:c1], (((1,), (1,)), ((), ())),
                            preferred_element_type=F32) * (XHEAD_DIM ** -0.5)
        p = jnp.exp(s - jnp.max(s, axis=-1, keepdims=True))
        p = p / jnp.sum(p, axis=-1, keepdims=True)
        ox = jnp.dot(p.astype(BF16), vb_ref[:, c0:c1], preferred_element_type=F32)
        gx = z_ref[:, qoff + D_X + c0:qoff + D_X + c1]
        o_ref[:, D_RNN + D_POOL + c0:D_RNN + D_POOL + c1] = (ox * _silu(gx)).astype(BF16)

    @pl.when(l == last)
    def _():
        newh_ref[0] = h_carry[...]
        tail_row = lambda j: (nrow - j) * SUBLANES + SUBLANES - 1
        for j in range(1, CONV_W):
            newconv_ref[0, CONV_W - 1 - j:CONV_W - j, :] = z_ref[tail_row(j):tail_row(j) + 1, 0:D_RNN]
        for j in range(1, POOL_HIST + 1):
            newpool_ref[0, POOL_HIST - j:POOL_HIST - j + 1, :] = (
                z_ref[tail_row(j):tail_row(j) + 1, pcol:pcol + D_POOL])


def _prompt_mix(z, mem_k, mem_v, conv_w, conv_b, wax, b_a, b_x, lam, wpool, pscale,
                sample_q, cache_k, cache_v, batch, seq, tm):
    nl = seq // tm
    assert sample_q.shape[0] == batch * nl
    side = lambda b, l: (b * nl + l, 0, 0)
    zw = 2 * D_MIX
    const2 = lambda b, l: (0, 0)
    const3 = lambda b, l: (0, 0, 0)
    kern = functools.partial(_prompt_mix_kernel, tm=tm)
    return pl.pallas_call(
        kern,
        grid=(batch, nl),
        in_specs=[
            pl.BlockSpec((tm, zw), lambda b, l: (b * nl + l, 0)),
            pl.BlockSpec((1, N_MEM, D_X), lambda b, l: (b, 0, 0)),
            pl.BlockSpec((1, N_MEM, D_X), lambda b, l: (b, 0, 0)),
            pl.BlockSpec((CONV_W, D_RNN), const2),
            pl.BlockSpec((1, D_RNN), const2),
            pl.BlockSpec((N_RNN_BLOCKS, RNN_BLOCK, 2 * RNN_BLOCK), const3),
            pl.BlockSpec((1, D_RNN), const2),
            pl.BlockSpec((1, D_RNN), const2),
            pl.BlockSpec((1, D_RNN), const2),
            pl.BlockSpec((len(POOL_WINDOWS), POOL_GROUP, POOL_GROUP), const3),
            pl.BlockSpec((1, D_POOL), const2),
            pl.BlockSpec((1, ATTN_BB, D_X), side),
            pl.BlockSpec((ATTN_BB, N_MEM * SUBLANES, LANES), side),
            pl.BlockSpec((ATTN_BB, N_MEM * SUBLANES, LANES), side),
        ],
        out_specs=[
            pl.BlockSpec((tm, D_MIX), lambda b, l: (b * nl + l, 0)),
            pl.BlockSpec((1, 1, D_RNN), lambda b, l: (b, 0, 0)),
            pl.BlockSpec((1, CONV_W - 1, D_RNN), lambda b, l: (b, 0, 0)),
            pl.BlockSpec((1, POOL_HIST, D_POOL), lambda b, l: (b, 0, 0)),
            pl.BlockSpec((1, ATTN_BB, D_X), side),
        ],
        out_shape=[
            jax.ShapeDtypeStruct((batch * seq, D_MIX), BF16),
            jax.ShapeDtypeStruct((batch, 1, D_RNN), F32),
            jax.ShapeDtypeStruct((batch, CONV_W - 1, D_RNN), F32),
            jax.ShapeDtypeStruct((batch, POOL_HIST, D_POOL), F32),
            jax.ShapeDtypeStruct(sample_q.shape, F32),
        ],
        scratch_shapes=[
            pltpu.VMEM((CONV_W - 1, SUBLANES, D_RNN), F32),
            pltpu.VMEM((POOL_HIST, SUBLANES, D_POOL), F32),
            pltpu.VMEM((1, D_RNN), F32),
            pltpu.VMEM((N_MEM, D_X), BF16),
            pltpu.VMEM((N_MEM, D_X), BF16),
            pltpu.VMEM((tm, D_RNN), F32),
            pltpu.VMEM((tm, D_RNN), F32),
        ],
        compiler_params=pltpu.CompilerParams(
            dimension_semantics=("arbitrary", "arbitrary"),
            vmem_limit_bytes=VMEM_LIMIT),
        name="prompt_mix",
    )(z, mem_k, mem_v, conv_w, conv_b, wax, b_a, b_x, lam, wpool, pscale,
      sample_q, cache_k, cache_v)


def _cache_rows(c):
    nb = c.shape[0]
    c = c.reshape(nb, N_MEM, N_XHEADS, XHEAD_DIM // LANES, LANES)
    return c.transpose(0, 1, 3, 2, 4).reshape(nb, N_MEM * SUBLANES, LANES)


def _sample_attn_block(q_ref, k_ref, v_ref, o_ref, bb):
    halves = XHEAD_DIM // LANES
    assert halves * N_XHEADS == SUBLANES
    r = lax.broadcasted_iota(jnp.int32, (SUBLANES, LANES), 0)
    c = lax.broadcasted_iota(jnp.int32, (SUBLANES, LANES), 1)
    diag = (c % SUBLANES) == r
    first_half = r < N_XHEADS
    nchunk = N_MEM * SUBLANES // LANES
    scores = []
    for j in range(bb):
        qn = jnp.concatenate(
            [q_ref[j:j + 1, (h * halves + t) * LANES:(h * halves + t + 1) * LANES]
             for t in range(halves) for h in range(N_XHEADS)], axis=0)
        scores.append(lax.dot_general(qn.astype(BF16), k_ref[j].astype(BF16),
                                      (((1,), (1,)), ((), ())), preferred_element_type=F32)
                      * (XHEAD_DIM ** -0.5))
    probs = []
    for j in range(bb):
        s = scores[j]
        chunks = []
        for ci in range(nchunk):
            sm = jnp.where(diag, s[:, ci * LANES:(ci + 1) * LANES], 0.0)
            other = pltpu.roll(sm, N_XHEADS, 0)
            other = jnp.where(first_half, pltpu.roll(other, LANES - N_XHEADS, 1),
                              pltpu.roll(other, N_XHEADS, 1))
            chunks.append(jnp.where(diag, sm + other, -jnp.inf))
        t_full = jnp.concatenate(chunks, axis=1)
        e = jnp.exp(t_full - jnp.max(t_full, axis=1, keepdims=True))
        probs.append((e / jnp.sum(e, axis=1, keepdims=True)).astype(BF16))
    for j in range(bb):
        o = jnp.dot(probs[j], v_ref[j].astype(BF16), preferred_element_type=F32)
        for t in range(halves):
            for h in range(N_XHEADS):
                col = (h * halves + t) * LANES
                o_ref[j:j + 1, col:col + LANES] = o[t * N_XHEADS + h:t * N_XHEADS + h + 1, :]


def _prompt_proj_kernel(x_ref, g_ref, w_ref, perm_ref, o_ref, u_ref):
    @pl.when(pl.program_id(1) == 0)
    def _():
        x = x_ref[...]
        u = (x * _rms_scale(x) * g_ref[...]).astype(BF16)
        for r0 in range(0, u.shape[0], MIX_TM):
            u_ref[r0:r0 + MIX_TM, :] = jnp.dot(
                perm_ref[...], u[r0:r0 + MIX_TM], preferred_element_type=F32).astype(BF16)

    o_ref[...] = jnp.dot(u_ref[...], w_ref[...], preferred_element_type=F32)


def _prompt_proj(x, g, w, perm, tm, tn):
    m, k = x.shape
    n = w.shape[1]
    return pl.pallas_call(
        _prompt_proj_kernel,
        grid=(m // tm, n // tn),
        in_specs=[
            pl.BlockSpec((tm, k), lambda i, j: (i, 0)),
            pl.BlockSpec((1, k), lambda i, j: (0, 0)),
            pl.BlockSpec((k, tn), lambda i, j: (0, j)),
            pl.BlockSpec(perm.shape, lambda i, j: (0, 0)),
        ],
        out_specs=pl.BlockSpec((tm, tn), lambda i, j: (i, j)),
        out_shape=jax.ShapeDtypeStruct((m, n), F32),
        scratch_shapes=[pltpu.VMEM((tm, k), BF16)],
        compiler_params=pltpu.CompilerParams(
            dimension_semantics=("arbitrary", "arbitrary"),
            vmem_limit_bytes=PROJ_VMEM_LIMIT),
        name="prompt_proj",
    )(x, g, w, perm)


def _sample_mix_kernel(z_ref, attn_ref, conv_ref, h_ref, pool_ref,
                       convw_ref, convb_ref, wax_ref, ba_ref, bx_ref, lam_ref, wpool_ref,
                       pscale_ref, o_ref, newh_ref, newconv_ref, newpool_ref):
    xr = z_ref[:, 0:D_RNN]
    xc = convb_ref[...] + convw_ref[CONV_W - 1:CONV_W, :] * xr
    for k in range(CONV_W - 1):
        xc = xc + convw_ref[k:k + 1, :] * conv_ref[k]
    for k in range(CONV_W - 2):
        newconv_ref[k] = conv_ref[k + 1]
    newconv_ref[CONV_W - 2] = xr

    rate = _decay_rate(lam_ref[...])
    for n in range(N_RNN_BLOCKS):
        c0, c1 = n * RNN_BLOCK, (n + 1) * RNN_BLOCK
        a, b = _rglru_block(xc[:, c0:c1], wax_ref[n], ba_ref[:, c0:c1], bx_ref[:, c0:c1],
                            rate[:, c0:c1])
        h = a * h_ref[:, c0:c1] + b
        newh_ref[:, c0:c1] = h
        o_ref[:, c0:c1] = (h * _silu(z_ref[:, D_RNN + c0:D_RNN + c1])).astype(BF16)

    xp = z_ref[:, 2 * D_RNN:2 * D_RNN + D_POOL]
    for k in range(POOL_HIST - 1):
        newpool_ref[k] = pool_ref[k + 1]
    newpool_ref[POOL_HIST - 1] = xp
    for g, w in enumerate(POOL_WINDOWS):
        c0, c1 = g * POOL_GROUP, (g + 1) * POOL_GROUP
        xg = xp[:, c0:c1]
        tot = xg
        for j in range(1, w):
            tot = tot + pool_ref[POOL_HIST - j, :, c0:c1]
        cnt = float(min(PAST_LEN + 1, w))
        d = tot / cnt - xg
        og = jnp.dot(d.astype(BF16), wpool_ref[g], preferred_element_type=F32)
        gp = z_ref[:, 2 * D_RNN + D_POOL + c0:2 * D_RNN + D_POOL + c1]
        o_ref[:, D_RNN + c0:D_RNN + c1] = (og * pscale_ref[:, c0:c1] * _silu(gp)).astype(BF16)

    gx = z_ref[:, 2 * D_RNN + 2 * D_POOL + D_X:2 * D_MIX]
    o_ref[:, D_RNN + D_POOL:] = (attn_ref[...] * _silu(gx)).astype(BF16)


def _sample_mix(z, attn, conv, h, pool, conv_w, conv_b, wax, b_a, b_x, lam, wpool, pscale, tb):
    nb = z.shape[0]
    zw = 2 * D_MIX
    rows = lambda i: (i, 0)
    const2 = lambda i: (0, 0)
    const3 = lambda i: (0, 0, 0)
    hist = lambda i: (0, i, 0)
    return pl.pallas_call(
        _sample_mix_kernel,
        grid=(nb // tb,),
        in_specs=[
            pl.BlockSpec((tb, zw), rows),
            pl.BlockSpec((tb, D_X), rows),
            pl.BlockSpec((CONV_W - 1, tb, D_RNN), hist),
            pl.BlockSpec((tb, D_RNN), rows),
            pl.BlockSpec((POOL_HIST, tb, D_POOL), hist),
            pl.BlockSpec((CONV_W, D_RNN), const2),
            pl.BlockSpec((1, D_RNN), const2),
            pl.BlockSpec((N_RNN_BLOCKS, RNN_BLOCK, 2 * RNN_BLOCK), const3),
            pl.BlockSpec((1, D_RNN), const2),
            pl.BlockSpec((1, D_RNN), const2),
            pl.BlockSpec((1, D_RNN), const2),
            pl.BlockSpec((len(POOL_WINDOWS), POOL_GROUP, POOL_GROUP), const3),
            pl.BlockSpec((1, D_POOL), const2),
        ],
        out_specs=[
            pl.BlockSpec((tb, D_MIX), rows),
            pl.BlockSpec((tb, D_RNN), rows),
            pl.BlockSpec((CONV_W - 1, tb, D_RNN), hist),
            pl.BlockSpec((POOL_HIST, tb, D_POOL), hist),
        ],
        out_shape=[
            jax.ShapeDtypeStruct((nb, D_MIX), BF16),
            jax.ShapeDtypeStruct((nb, D_RNN), F32),
            jax.ShapeDtypeStruct((CONV_W - 1, nb, D_RNN), F32),
            jax.ShapeDtypeStruct((POOL_HIST, nb, D_POOL), F32),
        ],
        compiler_params=pltpu.CompilerParams(
            dimension_semantics=("arbitrary",),
            vmem_limit_bytes=VMEM_LIMIT),
        name="sample_mix",
    )(z, attn, conv, h, pool, conv_w, conv_b, wax, b_a, b_x, lam, wpool, pscale)


def _branch_out_kernel(o_ref, gates_ref, x_ref, wb_ref, wo_ref, gpost_ref, *rest, interleaved):
    y_ref = rest[-1]
    merged = None
    for j, (r0, r1) in enumerate(((0, D_RNN), (D_RNN, D_RNN + D_POOL), (D_RNN + D_POOL, D_MIX))):
        yj = jnp.dot(o_ref[:, r0:r1], wb_ref[r0:r1, :], preferred_element_type=F32)
        term = _sigmoid(gates_ref[:, j * D_MODEL:(j + 1) * D_MODEL]) * yj
        merged = term if merged is None else merged + term
    merged = merged.astype(BF16)
    if interleaved:
        merged = jnp.dot(rest[0][...], merged, preferred_element_type=F32).astype(BF16)
    out = jnp.dot(merged, wo_ref[...], preferred_element_type=F32)
    y_ref[...] = x_ref[...] + (out * gpost_ref[...]) * _rms_scale(out)


def _branch_out(o, z, x, wb, wo, g_post, tm, unperm=None):
    m = x.shape[0]
    gw = N_BRANCH * D_MODEL
    gblk = (2 * D_MIX) // gw
    resident = pl.Buffered(1)
    in_specs = [
        pl.BlockSpec((tm, D_MIX), lambda i: (i, 0)),
        pl.BlockSpec((tm, gw), lambda i: (i, gblk)),
        pl.BlockSpec((tm, D_MODEL), lambda i: (i, 0)),
        pl.BlockSpec((D_MIX, D_MODEL), lambda i: (0, 0), pipeline_mode=resident),
        pl.BlockSpec((D_MODEL, D_MODEL), lambda i: (0, 0), pipeline_mode=resident),
        pl.BlockSpec((1, D_MODEL), lambda i: (0, 0)),
    ]
    args = [o, z, x, wb, wo, g_post]
    if unperm is not None:
        assert unperm.shape == (tm, tm)
        in_specs.append(pl.BlockSpec(unperm.shape, lambda i: (0, 0)))
        args.append(unperm)
    return pl.pallas_call(
        functools.partial(_branch_out_kernel, interleaved=unperm is not None),
        grid=(m // tm,),
        in_specs=in_specs,
        out_specs=pl.BlockSpec((tm, D_MODEL), lambda i: (i, 0)),
        out_shape=jax.ShapeDtypeStruct((m, D_MODEL), F32),
        compiler_params=pltpu.CompilerParams(
            dimension_semantics=("arbitrary",),
            vmem_limit_bytes=VMEM_LIMIT),
        name="branch_out",
    )(*args)


WROWS = 512
PER_BRANCH = D_RNN // WROWS
assert D_RNN == D_POOL == D_X and D_RNN % WROWS == 0 and D_MODEL % WROWS == 0
N_WB_BLOCKS = N_BRANCH * PER_BRANCH
N_WOUT_BLOCKS = D_MODEL // WROWS


def _branch_out_cast_kernel(o_ref, gates_ref, x_ref, wb_ref, wo_ref, gpost_ref,
                            y_ref, wbb_ref, wob_ref, merged_ref, out_ref):
    s = pl.program_id(0)

    @pl.when(s < N_WB_BLOCKS)
    def _():
        w = wb_ref[...].astype(BF16)
        wbb_ref[...] = w
        term = _sigmoid(gates_ref[...]) * jnp.dot(o_ref[...], w, preferred_element_type=F32)

        @pl.when(s == 0)
        def _():
            merged_ref[...] = term

        @pl.when(s > 0)
        def _():
            merged_ref[...] += term

    for kb in range(N_WOUT_BLOCKS):
        @pl.when(s == N_WB_BLOCKS + kb)
        def _(kb=kb):
            w = wo_ref[...].astype(BF16)
            wob_ref[...] = w
            part = jnp.dot(merged_ref[:, kb * WROWS:(kb + 1) * WROWS].astype(BF16), w,
                           preferred_element_type=F32)
            if kb == 0:
                out_ref[...] = part
            else:
                out_ref[...] += part

    @pl.when(s == N_WB_BLOCKS + N_WOUT_BLOCKS - 1)
    def _():
        out = out_ref[...]
        y_ref[...] = x_ref[...] + out * _rms_scale(out) * gpost_ref[...]


def _branch_out_cast(o, z, x, wb, wo, g_post):
    m = x.shape[0]
    gblk0 = (2 * D_MIX) // D_MODEL
    wb_blk = lambda s: jnp.minimum(s, N_WB_BLOCKS - 1)
    wo_blk = lambda s: jnp.maximum(s - N_WB_BLOCKS, 0)
    return pl.pallas_call(
        _branch_out_cast_kernel,
        grid=(N_WB_BLOCKS + N_WOUT_BLOCKS,),
        in_specs=[
            pl.BlockSpec((m, WROWS), lambda s: (0, wb_blk(s))),
            pl.BlockSpec((m, D_MODEL), lambda s: (0, gblk0 + wb_blk(s) // PER_BRANCH)),
            pl.BlockSpec((m, D_MODEL), lambda s: (0, 0)),
            pl.BlockSpec((WROWS, D_MODEL), lambda s: (wb_blk(s), 0)),
            pl.BlockSpec((WROWS, D_MODEL), lambda s: (wo_blk(s), 0)),
            pl.BlockSpec((1, D_MODEL), lambda s: (0, 0)),
        ],
        out_specs=[
            pl.BlockSpec((m, D_MODEL), lambda s: (0, 0)),
            pl.BlockSpec((WROWS, D_MODEL), lambda s: (wb_blk(s), 0)),
            pl.BlockSpec((WROWS, D_MODEL), lambda s: (wo_blk(s), 0)),
        ],
        out_shape=[
            jax.ShapeDtypeStruct((m, D_MODEL), F32),
            jax.ShapeDtypeStruct(wb.shape, BF16),
            jax.ShapeDtypeStruct(wo.shape, BF16),
        ],
        scratch_shapes=[pltpu.VMEM((m, D_MODEL), F32), pltpu.VMEM((m, D_MODEL), F32)],
        compiler_params=pltpu.CompilerParams(
            dimension_semantics=("arbitrary",),
            vmem_limit_bytes=VMEM_LIMIT),
        name="branch_out_cast",
    )(o, z, x, wb, wo, g_post)


def _mem_kv_kernel(x_ref, g_ref, w_ref, k_ref, v_ref, u_ref):
    j = pl.program_id(1)

    @pl.when(j == 0)
    def _():
        x = x_ref[...]
        u_ref[...] = (x * _rms_scale(x) * g_ref[...]).astype(BF16)

    res = jnp.dot(u_ref[...], w_ref[...].astype(BF16), preferred_element_type=F32)

    @pl.when(j == 0)
    def _():
        k_ref[...] = res

    @pl.when(j == 1)
    def _():
        v_ref[...] = res


def _mem_kv(x, g, w, tm):
    m, k = x.shape
    assert w.shape[1] == 2 * D_X
    half = pl.BlockSpec((tm, D_X), lambda i, j: (i, 0))
    return pl.pallas_call(
        _mem_kv_kernel,
        grid=(m // tm, 2),
        in_specs=[
            pl.BlockSpec((tm, k), lambda i, j: (i, 0)),
            pl.BlockSpec((1, k), lambda i, j: (0, 0)),
            pl.BlockSpec((k, D_X), lambda i, j: (0, j)),
        ],
        out_specs=[half, half],
        out_shape=[jax.ShapeDtypeStruct((m, D_X), F32)] * 2,
        scratch_shapes=[pltpu.VMEM((tm, k), BF16)],
        compiler_params=pltpu.CompilerParams(
            dimension_semantics=("arbitrary", "arbitrary"),
            vmem_limit_bytes=VMEM_LIMIT),
        name="mem_kv",
    )(x, g, w)


def kernel(x_prompt, x_sample, mem_prompt, state_rglru_h, state_conv, state_pool, cache_mem_k, cache_mem_v, g_pre, w_in, conv_w, conv_b, w_rg_a, b_rg_a, w_rg_x, b_rg_x, lru_lambda, w_pool, pool_scale, g_mem, w_kv, w_branch, w_out, g_post):
    batch, seq, _ = x_prompt.shape
    nb = x_sample.shape[0]
    depth = g_pre.shape[0]
    assert depth == 1 and x_sample.shape[1] == 1

    l = 0
    row = lambda v: v.reshape(1, -1)
    wax = jnp.concatenate([w_rg_a[l], w_rg_x[l]], axis=-1).astype(BF16)
    wpool = w_pool[l].astype(BF16)
    mix_params = (conv_w[l], row(conv_b[l]), wax, row(b_rg_a[l]), row(b_rg_x[l]),
                  row(lru_lambda[l]), wpool, row(pool_scale[l]))

    xp2 = x_prompt.reshape(batch * seq, D_MODEL)
    xs2 = x_sample.reshape(nb, D_MODEL)
    mem2 = mem_prompt.reshape(batch * N_MEM, D_MODEL)

    z_s, w_in_b = _sample_proj(xs2, row(g_pre[l]), w_in[l], tn=SAMPLE_PROJ_TN)
    qoff = 2 * D_RNN + 2 * D_POOL
    q_s = z_s[:, qoff:qoff + D_X].reshape(nb // ATTN_BB, ATTN_BB, D_X)

    mem_k, mem_v = _mem_kv(mem2, row(g_mem[l]), w_kv[l], tm=KV_TM)
    mem_k = mem_k.reshape(batch, N_MEM, D_X)
    mem_v = mem_v.reshape(batch, N_MEM, D_X)

    perm = _chunk_interleave()
    z_p = _prompt_proj(xp2, row(g_pre[l]), w_in_b, perm, tm=PROJ_TM, tn=PROJ_TN)
    o_p, h_p, c_p, p_p, attn_s = _prompt_mix(
        z_p, mem_k, mem_v, *mix_params, q_s, _cache_rows(cache_mem_k[l]),
        _cache_rows(cache_mem_v[l]), batch=batch, seq=seq, tm=MIX_TM)
    attn_s = attn_s.reshape(nb, D_X)

    o_s, h_s, c_s, p_s = _sample_mix(
        z_s, attn_s, state_conv[l].transpose(1, 0, 2), state_rglru_h[l],
        state_pool[l].transpose(1, 0, 2), *mix_params, tb=SAMPLE_MIX_TB)
    y_s, w_br_b, w_out_b = _branch_out_cast(o_s, z_s, xs2, w_branch[l], w_out[l], row(g_post[l]))

    y_p = _branch_out(o_p, z_p, xp2, w_br_b, w_out_b, row(g_post[l]), tm=MIX_TM, unperm=perm.T)

    return (
        y_p.reshape(batch, seq, D_MODEL),
        y_s.reshape(nb, 1, D_MODEL),
        h_p.reshape(1, batch, D_RNN),
        c_p.reshape(1, batch, CONV_W - 1, D_RNN),
        p_p.reshape(1, batch, POOL_HIST, D_POOL),
        mem_k.reshape(1, batch, N_MEM, N_XHEADS, XHEAD_DIM),
        mem_v.reshape(1, batch, N_MEM, N_XHEADS, XHEAD_DIM),
        h_s.reshape(1, nb, D_RNN),
        c_s.transpose(1, 0, 2)[None],
        p_s.transpose(1, 0, 2)[None],
    )
```

```python
import functools

import jax
import jax.numpy as jnp
from jax import lax
from jax.experimental import pallas as pl
from jax.experimental.pallas import tpu as pltpu

D_MODEL = 2048
PAST_LEN = 16384
D_RNN = 1024
N_RNN_BLOCKS = 8
RNN_BLOCK = D_RNN // N_RNN_BLOCKS
CONV_W = 4
LRU_C = 8.0
D_POOL = 1024
POOL_WINDOWS = (2, 4, 8, 16)
POOL_GROUP = D_POOL // len(POOL_WINDOWS)
POOL_HIST = max(POOL_WINDOWS) - 1
N_MEM = 256
N_XHEADS = 4
XHEAD_DIM = 256
D_X = N_XHEADS * XHEAD_DIM
N_BRANCH = 3
D_MIX = D_RNN + D_POOL + D_X
D_IN = 2 * D_MIX + N_BRANCH * D_MODEL
EPS = 1e-6

SUBLANES = 8
LANES = 128
VMEM_LIMIT = 56 * 1024 * 1024
BIG_VMEM_LIMIT = 60 * 1024 * 1024
MIX_TM = 256
PROJ_TM, PROJ_TN = 1024, 2048
SAMPLE_PROJ_TN = 1024
ATTN_BB = 4
SAMPLE_MIX_TB = 32
KV_TM = 1024

BF16 = jnp.bfloat16
F32 = jnp.float32

NEG_LOG2_E = -1.4426950408889634


def _sigmoid(x):
    return 1.0 / (1.0 + jnp.exp2(x * NEG_LOG2_E))


def _silu(x):
    return x * _sigmoid(x)


def _softplus(x):
    return jnp.maximum(x, 0.0) + jnp.log1p(jnp.exp(-jnp.abs(x)))


def _rms_scale(x):
    return lax.rsqrt(jnp.mean(x * x, axis=-1, keepdims=True) + EPS)


def _chunk_interleave():
    nrow = MIX_TM // SUBLANES
    p = jnp.arange(MIX_TM)
    token = (p % SUBLANES) * nrow + p // SUBLANES
    return (token[:, None] == jnp.arange(MIX_TM)[None, :]).astype(BF16)


def _sample_proj_kernel(x_ref, g_ref, w_ref, o_ref, wb_ref, u_ref):
    @pl.when(pl.program_id(0) == 0)
    def _():
        x = x_ref[...]
        u_ref[...] = (x * _rms_scale(x) * g_ref[...]).astype(BF16)

    w = w_ref[...].astype(BF16)
    wb_ref[...] = w
    o_ref[...] = jnp.dot(u_ref[...], w, preferred_element_type=F32)


def _sample_proj(x, g, w, tn):
    m, k = x.shape
    n = w.shape[1]
    return pl.pallas_call(
        _sample_proj_kernel,
        grid=(n // tn,),
        in_specs=[
            pl.BlockSpec((m, k), lambda j: (0, 0)),
            pl.BlockSpec((1, k), lambda j: (0, 0)),
            pl.BlockSpec((k, tn), lambda j: (0, j)),
        ],
        out_specs=[
            pl.BlockSpec((m, tn), lambda j: (0, j)),
            pl.BlockSpec((k, tn), lambda j: (0, j)),
        ],
        out_shape=[
            jax.ShapeDtypeStruct((m, n), F32),
            jax.ShapeDtypeStruct((k, n), BF16),
        ],
        scratch_shapes=[pltpu.VMEM((m, k), BF16)],
        compiler_params=pltpu.CompilerParams(
            dimension_semantics=("arbitrary",),
            vmem_limit_bytes=VMEM_LIMIT),
        name="sample_proj",
    )(x, g, w)


def _decay_rate(lam):
    return _softplus(-lam) * (LRU_C * NEG_LOG2_E)


def _rglru_block(xc, wax, ba, bx, rate):
    ri = jnp.dot(xc.astype(BF16), wax, preferred_element_type=F32)
    r = _sigmoid(ri[:, :RNN_BLOCK] + ba)
    i = _sigmoid(ri[:, RNN_BLOCK:] + bx)
    a = jnp.exp2(r * rate)
    one_m = 1.0 - a * a
    mult = jnp.where(one_m > 0.0, one_m * lax.rsqrt(one_m), 0.0)
    return a, mult * i * xc


def _prompt_mix_kernel(z_ref, k_ref, v_ref, convw_ref, convb_ref, wax_ref, ba_ref, bx_ref,
                       lam_ref, wpool_ref, pscale_ref, sq_ref, sk_ref, sv_ref,
                       o_ref, newh_ref, newconv_ref, newpool_ref, sattn_ref,
                       conv_carry, pool_carry, h_carry, kb_ref, vb_ref, ac_scr, hl_scr, *, tm):
    l = pl.program_id(1)
    last = pl.num_programs(1) - 1
    nrow = tm // SUBLANES

    _sample_attn_block(sq_ref.at[0], sk_ref, sv_ref, sattn_ref.at[0], ATTN_BB)

    @pl.when(l == 0)
    def _():
        conv_carry[...] = jnp.zeros(conv_carry.shape, F32)
        pool_carry[...] = jnp.zeros(pool_carry.shape, F32)
        h_carry[...] = jnp.zeros(h_carry.shape, F32)
        kb_ref[...] = k_ref[0].astype(BF16)
        vb_ref[...] = v_ref[0].astype(BF16)

    chunk_id = lax.broadcasted_iota(jnp.int32, (SUBLANES, LANES), 0)
    first_chunk = chunk_id == 0

    def load_groups(col, width=LANES):
        return [z_ref[r * SUBLANES:(r + 1) * SUBLANES, col:col + width] for r in range(nrow)]

    def store_groups(col, rows, width=LANES):
        o_ref[:, col:col + width] = jnp.concatenate(rows, axis=0).astype(BF16)

    def history(tail_group, carry_ref, j, c0):
        tail = pltpu.roll(tail_group, 1, 0)
        prev = jnp.where(first_chunk, carry_ref[j - 1, :, c0:c0 + LANES], tail)
        carry_ref[j - 1, :, c0:c0 + LANES] = tail
        return prev

    rate = _decay_rate(lam_ref[...])
    for n in range(N_RNN_BLOCKS):
        c0, c1 = n * RNN_BLOCK, (n + 1) * RNN_BLOCK
        xs = load_groups(c0)
        ext = [history(xs[nrow - j], conv_carry, j, c0) for j in range(CONV_W - 1, 0, -1)] + xs
        cw = [jnp.broadcast_to(convw_ref[k:k + 1, c0:c1], (SUBLANES, LANES)) for k in range(CONV_W)]
        cb = jnp.broadcast_to(convb_ref[:, c0:c1], (SUBLANES, LANES))
        xc = []
        for r in range(nrow):
            acc = cb + cw[0] * ext[r]
            for k in range(1, CONV_W):
                acc = acc + cw[k] * ext[r + k]
            xc.append(acc)
        a, b = _rglru_block(jnp.concatenate(xc, axis=0), wax_ref[n], ba_ref[:, c0:c1],
                            bx_ref[:, c0:c1], rate[:, c0:c1])
        ac_scr[:, c0:c1] = a
        hl_scr[:, c0:c1] = b

    acc_a = ac_scr[0:SUBLANES, :]
    acc_h = hl_scr[0:SUBLANES, :]
    for r in range(1, nrow):
        rows = slice(r * SUBLANES, (r + 1) * SUBLANES)
        ar = ac_scr[rows, :]
        acc_h = ar * acc_h + hl_scr[rows, :]
        acc_a = ar * acc_a
        ac_scr[rows, :] = acc_a
        hl_scr[rows, :] = acc_h
    h_in = h_carry[...]
    entering = []
    for c in range(SUBLANES):
        entering.append(h_in)
        h_in = acc_a[c:c + 1] * h_in + acc_h[c:c + 1]
    h_carry[...] = h_in
    h_enter = jnp.concatenate(entering, axis=0)
    for n in range(N_RNN_BLOCKS):
        c0, c1 = n * RNN_BLOCK, (n + 1) * RNN_BLOCK
        gr = load_groups(D_RNN + c0)
        store_groups(c0, [(hl_scr[r * SUBLANES:(r + 1) * SUBLANES, c0:c1]
                           + ac_scr[r * SUBLANES:(r + 1) * SUBLANES, c0:c1] * h_enter[:, c0:c1])
                          * _silu(gr[r]) for r in range(nrow)])

    pcol = 2 * D_RNN
    blocks = [(w, c0) for g, w in enumerate(POOL_WINDOWS)
              for c0 in range(g * POOL_GROUP, (g + 1) * POOL_GROUP, LANES)]

    def group(c0, r):
        return z_ref[r * SUBLANES:(r + 1) * SUBLANES, pcol + c0:pcol + c0 + LANES]

    def mean_minus_token(tot, w, c0, r):
        if r < w - 1:
            pos1 = l * tm + chunk_id * nrow + (r + 1)
            mean = tot / jnp.minimum(pos1, w).astype(F32)
        else:
            mean = tot * (1.0 / w)
        return mean - group(c0, r)

    hist, tot = {}, {}
    for w, c0 in blocks:
        hist[c0] = [history(group(c0, nrow - j), pool_carry, j, c0) for j in range(1, w)]
        t = group(c0, 0)
        for h in hist[c0]:
            t = t + h
        tot[c0] = t
        hl_scr[0:SUBLANES, c0:c0 + LANES] = mean_minus_token(t, w, c0, 0)
    for r in range(1, nrow):
        for w, c0 in blocks:
            leaving = group(c0, r - w) if r >= w else hist[c0][w - r - 1]
            tot[c0] = tot[c0] + (group(c0, r) - leaving)
            hl_scr[r * SUBLANES:(r + 1) * SUBLANES, c0:c0 + LANES] = mean_minus_token(
                tot[c0], w, c0, r)
    for g, w in enumerate(POOL_WINDOWS):
        c0, c1 = g * POOL_GROUP, (g + 1) * POOL_GROUP
        og = jnp.dot(hl_scr[:, c0:c1].astype(BF16), wpool_ref[g], preferred_element_type=F32)
        gp = z_ref[:, pcol + D_POOL + c0:pcol + D_POOL + c1]
        o_ref[:, D_RNN + c0:D_RNN + c1] = (og * pscale_ref[:, c0:c1] * _silu(gp)).astype(BF16)

    qoff = 2 * D_RNN + 2 * D_POOL
    for hd in range(N_XHEADS):
        c0, c1 = hd * XHEAD_DIM, (hd + 1) * XHEAD_DIM
        q = z_ref[:, qoff + c0:qoff + c1].astype(BF16)
        s = lax.dot_general(q, kb_ref[:, c0:c1], (((1,), (1,)), ((), ())),
                            preferred_element_type=F32) * (XHEAD_DIM ** -0.5)
        p = jnp.exp(s - jnp.max(s, axis=-1, keepdims=True))
        p = p / jnp.sum(p, axis=-1, keepdims=True)
        ox = jnp.dot(p.astype(BF16), vb_ref[:, c0:c1], preferred_element_type=F32)
        gx = z_ref[:, qoff + D_X + c0:qoff + D_X + c1]
        o_ref[:, D_RNN + D_POOL + c0:D_RNN + D_POOL + c1] = (ox * _silu(gx)).astype(BF16)

    @pl.when(l == last)
    def _():
        newh_ref[0] = h_carry[...]
        tail_row = lambda j: (nrow - j) * SUBLANES + SUBLANES - 1
        for j in range(1, CONV_W):
            newconv_ref[0, CONV_W - 1 - j:CONV_W - j, :] = z_ref[tail_row(j):tail_row(j) + 1, 0:D_RNN]
        for j in range(1, POOL_HIST + 1):
            newpool_ref[0, POOL_HIST - j:POOL_HIST - j + 1, :] = (
                z_ref[tail_row(j):tail_row(j) + 1, pcol:pcol + D_POOL])


def _prompt_mix(z, mem_k, mem_v, conv_w, conv_b, wax, b_a, b_x, lam, wpool, pscale,
                sample_q, cache_k, cache_v, batch, seq, tm):
    nl = seq // tm
    assert sample_q.shape[0] == batch * nl
    side = lambda b, l: (b * nl + l, 0, 0)
    zw = 2 * D_MIX
    const2 = lambda b, l: (0, 0)
    const3 = lambda b, l: (0, 0, 0)
    kern = functools.partial(_prompt_mix_kernel, tm=tm)
    return pl.pallas_call(
        kern,
        grid=(batch, nl),
        in_specs=[
            pl.BlockSpec((tm, zw), lambda b, l: (b * nl + l, 0)),
            pl.BlockSpec((1, N_MEM, D_X), lambda b, l: (b, 0, 0)),
            pl.BlockSpec((1, N_MEM, D_X), lambda b, l: (b, 0, 0)),
            pl.BlockSpec((CONV_W, D_RNN), const2),
            pl.BlockSpec((1, D_RNN), const2),
            pl.BlockSpec((N_RNN_BLOCKS, RNN_BLOCK, 2 * RNN_BLOCK), const3),
            pl.BlockSpec((1, D_RNN), const2),
            pl.BlockSpec((1, D_RNN), const2),
            pl.BlockSpec((1, D_RNN), const2),
            pl.BlockSpec((len(POOL_WINDOWS), POOL_GROUP, POOL_GROUP), const3),
            pl.BlockSpec((1, D_POOL), const2),
            pl.BlockSpec((1, ATTN_BB, D_X), side),
            pl.BlockSpec((ATTN_BB, N_MEM * SUBLANES, LANES), side),
            pl.BlockSpec((ATTN_BB, N_MEM * SUBLANES, LANES), side),
        ],
        out_specs=[
            pl.BlockSpec((tm, D_MIX), lambda b, l: (b * nl + l, 0)),
            pl.BlockSpec((1, 1, D_RNN), lambda b, l: (b, 0, 0)),
            pl.BlockSpec((1, CONV_W - 1, D_RNN), lambda b, l: (b, 0, 0)),
            pl.BlockSpec((1, POOL_HIST, D_POOL), lambda b, l: (b, 0, 0)),
            pl.BlockSpec((1, ATTN_BB, D_X), side),
        ],
        out_shape=[
            jax.ShapeDtypeStruct((batch * seq, D_MIX), BF16),
            jax.ShapeDtypeStruct((batch, 1, D_RNN), F32),
            jax.ShapeDtypeStruct((batch, CONV_W - 1, D_RNN), F32),
            jax.ShapeDtypeStruct((batch, POOL_HIST, D_POOL), F32),
            jax.ShapeDtypeStruct(sample_q.shape, F32),
        ],
        scratch_shapes=[
            pltpu.VMEM((CONV_W - 1, SUBLANES, D_RNN), F32),
            pltpu.VMEM((POOL_HIST, SUBLANES, D_POOL), F32),
            pltpu.VMEM((1, D_RNN), F32),
            pltpu.VMEM((N_MEM, D_X), BF16),
            pltpu.VMEM((N_MEM, D_X), BF16),
            pltpu.VMEM((tm, D_RNN), F32),
            pltpu.VMEM((tm, D_RNN), F32),
        ],
        compiler_params=pltpu.CompilerParams(
            dimension_semantics=("arbitrary", "arbitrary"),
            vmem_limit_bytes=VMEM_LIMIT),
        name="prompt_mix",
    )(z, mem_k, mem_v, conv_w, conv_b, wax, b_a, b_x, lam, wpool, pscale,
      sample_q, cache_k, cache_v)


def _cache_rows(c):
    nb = c.shape[0]
    c = c.reshape(nb, N_MEM, N_XHEADS, XHEAD_DIM // LANES, LANES)
    return c.transpose(0, 1, 3, 2, 4).reshape(nb, N_MEM * SUBLANES, LANES)


def _sample_attn_block(q_ref, k_ref, v_ref, o_ref, bb):
    halves = XHEAD_DIM // LANES
    assert halves * N_XHEADS == SUBLANES
    r = lax.broadcasted_iota(jnp.int32, (SUBLANES, LANES), 0)
    c = lax.broadcasted_iota(jnp.int32, (SUBLANES, LANES), 1)
    diag = (c % SUBLANES) == r
    first_half = r < N_XHEADS
    nchunk = N_MEM * SUBLANES // LANES
    scores = []
    for j in range(bb):
        qn = jnp.concatenate(
            [q_ref[j:j + 1, (h * halves + t) * LANES:(h * halves + t + 1) * LANES]
             for t in range(halves) for h in range(N_XHEADS)], axis=0)
        scores.append(lax.dot_general(qn.astype(BF16), k_ref[j].astype(BF16),
                                      (((1,), (1,)), ((), ())), preferred_element_type=F32)
                      * (XHEAD_DIM ** -0.5))
    probs = []
    for j in range(bb):
        s = scores[j]
        chunks = []
        for ci in range(nchunk):
            sm = jnp.where(diag, s[:, ci * LANES:(ci + 1) * LANES], 0.0)
            other = pltpu.roll(sm, N_XHEADS, 0)
            other = jnp.where(first_half, pltpu.roll(other, LANES - N_XHEADS, 1),
                              pltpu.roll(other, N_XHEADS, 1))
            chunks.append(jnp.where(diag, sm + other, -jnp.inf))
        t_full = jnp.concatenate(chunks, axis=1)
        e = jnp.exp(t_full - jnp.max(t_full, axis=1, keepdims=True))
        probs.append((e / jnp.sum(e, axis=1, keepdims=True)).astype(BF16))
    for j in range(bb):
        o = jnp.dot(probs[j], v_ref[j].astype(BF16), preferred_element_type=F32)
        for t in range(halves):
            for h in range(N_XHEADS):
                col = (h * halves + t) * LANES
                o_ref[j:j + 1, col:col + LANES] = o[t * N_XHEADS + h:t * N_XHEADS + h + 1, :]


def _prompt_proj_kernel(x_ref, g_ref, w_ref, perm_ref, o_ref, u_ref):
    @pl.when(pl.program_id(1) == 0)
    def _():
        x = x_ref[...]
        u = (x * _rms_scale(x) * g_ref[...]).astype(BF16)
        for r0 in range(0, u.shape[0], MIX_TM):
            u_ref[r0:r0 + MIX_TM, :] = jnp.dot(
                perm_ref[...], u[r0:r0 + MIX_TM], preferred_element_type=F32).astype(BF16)

    o_ref[...] = jnp.dot(u_ref[...], w_ref[...], preferred_element_type=F32)


def _prompt_proj(x, g, w, perm, tm, tn):
    m, k = x.shape
    n = w.shape[1]
    return pl.pallas_call(
        _prompt_proj_kernel,
        grid=(m // tm, n // tn),
        in_specs=[
            pl.BlockSpec((tm, k), lambda i, j: (i, 0)),
            pl.BlockSpec((1, k), lambda i, j: (0, 0)),
            pl.BlockSpec((k, tn), lambda i, j: (0, j)),
            pl.BlockSpec(perm.shape, lambda i, j: (0, 0)),
        ],
        out_specs=pl.BlockSpec((tm, tn), lambda i, j: (i, j)),
        out_shape=jax.ShapeDtypeStruct((m, n), F32),
        scratch_shapes=[pltpu.VMEM((tm, k), BF16)],
        compiler_params=pltpu.CompilerParams(
            dimension_semantics=("arbitrary", "arbitrary"),
            vmem_limit_bytes=BIG_VMEM_LIMIT),
        name="prompt_proj",
    )(x, g, w, perm)


def _sample_mix_kernel(z_ref, attn_ref, conv_ref, h_ref, pool_ref,
                       convw_ref, convb_ref, wax_ref, ba_ref, bx_ref, lam_ref, wpool_ref,
                       pscale_ref, o_ref, newh_ref, newconv_ref, newpool_ref):
    xr = z_ref[:, 0:D_RNN]
    xc = convb_ref[...] + convw_ref[CONV_W - 1:CONV_W, :] * xr
    for k in range(CONV_W - 1):
        xc = xc + convw_ref[k:k + 1, :] * conv_ref[k]
    for k in range(CONV_W - 2):
        newconv_ref[k] = conv_ref[k + 1]
    newconv_ref[CONV_W - 2] = xr

    rate = _decay_rate(lam_ref[...])
    for n in range(N_RNN_BLOCKS):
        c0, c1 = n * RNN_BLOCK, (n + 1) * RNN_BLOCK
        a, b = _rglru_block(xc[:, c0:c1], wax_ref[n], ba_ref[:, c0:c1], bx_ref[:, c0:c1],
                            rate[:, c0:c1])
        h = a * h_ref[:, c0:c1] + b
        newh_ref[:, c0:c1] = h
        o_ref[:, c0:c1] = (h * _silu(z_ref[:, D_RNN + c0:D_RNN + c1])).astype(BF16)

    xp = z_ref[:, 2 * D_RNN:2 * D_RNN + D_POOL]
    for k in range(POOL_HIST - 1):
        newpool_ref[k] = pool_ref[k + 1]
    newpool_ref[POOL_HIST - 1] = xp
    for g, w in enumerate(POOL_WINDOWS):
        c0, c1 = g * POOL_GROUP, (g + 1) * POOL_GROUP
        xg = xp[:, c0:c1]
        tot = xg
        for j in range(1, w):
            tot = tot + pool_ref[POOL_HIST - j, :, c0:c1]
        cnt = float(min(PAST_LEN + 1, w))
        d = tot / cnt - xg
        og = jnp.dot(d.astype(BF16), wpool_ref[g], preferred_element_type=F32)
        gp = z_ref[:, 2 * D_RNN + D_POOL + c0:2 * D_RNN + D_POOL + c1]
        o_ref[:, D_RNN + c0:D_RNN + c1] = (og * pscale_ref[:, c0:c1] * _silu(gp)).astype(BF16)

    gx = z_ref[:, 2 * D_RNN + 2 * D_POOL + D_X:2 * D_MIX]
    o_ref[:, D_RNN + D_POOL:] = (attn_ref[...] * _silu(gx)).astype(BF16)


def _sample_mix(z, attn, conv, h, pool, conv_w, conv_b, wax, b_a, b_x, lam, wpool, pscale, tb):
    nb = z.shape[0]
    zw = 2 * D_MIX
    rows = lambda i: (i, 0)
    const2 = lambda i: (0, 0)
    const3 = lambda i: (0, 0, 0)
    hist = lambda i: (0, i, 0)
    return pl.pallas_call(
        _sample_mix_kernel,
        grid=(nb // tb,),
        in_specs=[
            pl.BlockSpec((tb, zw), rows),
            pl.BlockSpec((tb, D_X), rows),
            pl.BlockSpec((CONV_W - 1, tb, D_RNN), hist),
            pl.BlockSpec((tb, D_RNN), rows),
            pl.BlockSpec((POOL_HIST, tb, D_POOL), hist),
            pl.BlockSpec((CONV_W, D_RNN), const2),
            pl.BlockSpec((1, D_RNN), const2),
            pl.BlockSpec((N_RNN_BLOCKS, RNN_BLOCK, 2 * RNN_BLOCK), const3),
            pl.BlockSpec((1, D_RNN), const2),
            pl.BlockSpec((1, D_RNN), const2),
            pl.BlockSpec((1, D_RNN), const2),
            pl.BlockSpec((len(POOL_WINDOWS), POOL_GROUP, POOL_GROUP), const3),
            pl.BlockSpec((1, D_POOL), const2),
        ],
        out_specs=[
            pl.BlockSpec((tb, D_MIX), rows),
            pl.BlockSpec((tb, D_RNN), rows),
            pl.BlockSpec((CONV_W - 1, tb, D_RNN), hist),
            pl.BlockSpec((POOL_HIST, tb, D_POOL), hist),
        ],
        out_shape=[
            jax.ShapeDtypeStruct((nb, D_MIX), BF16),
            jax.ShapeDtypeStruct((nb, D_RNN), F32),
            jax.ShapeDtypeStruct((CONV_W - 1, nb, D_RNN), F32),
            jax.ShapeDtypeStruct((POOL_HIST, nb, D_POOL), F32),
        ],
        compiler_params=pltpu.CompilerParams(
            dimension_semantics=("arbitrary",),
            vmem_limit_bytes=VMEM_LIMIT),
        name="sample_mix",
    )(z, attn, conv, h, pool, conv_w, conv_b, wax, b_a, b_x, lam, wpool, pscale)


def _branch_out_kernel(o_ref, gates_ref, x_ref, wb_ref, wo_ref, gpost_ref, *rest, interleaved):
    y_ref = rest[-1]
    merged = None
    for j, (r0, r1) in enumerate(((0, D_RNN), (D_RNN, D_RNN + D_POOL), (D_RNN + D_POOL, D_MIX))):
        yj = jnp.dot(o_ref[:, r0:r1], wb_ref[r0:r1, :], preferred_element_type=F32)
        term = _sigmoid(gates_ref[:, j * D_MODEL:(j + 1) * D_MODEL]) * yj
        merged = term if merged is None else merged + term
    merged = merged.astype(BF16)
    if interleaved:
        merged = jnp.dot(rest[0][...], merged, preferred_element_type=F32).astype(BF16)
    out = jnp.dot(merged, wo_ref[...], preferred_element_type=F32)
    y_ref[...] = x_ref[...] + (out * gpost_ref[...]) * _rms_scale(out)


def _branch_out(o, z, x, wb, wo, g_post, tm, unperm=None):
    m = x.shape[0]
    gw = N_BRANCH * D_MODEL
    gblk = (2 * D_MIX) // gw
    resident = pl.Buffered(1)
    in_specs = [
        pl.BlockSpec((tm, D_MIX), lambda i: (i, 0)),
        pl.BlockSpec((tm, gw), lambda i: (i, gblk)),
        pl.BlockSpec((tm, D_MODEL), lambda i: (i, 0)),
        pl.BlockSpec((D_MIX, D_MODEL), lambda i: (0, 0), pipeline_mode=resident),
        pl.BlockSpec((D_MODEL, D_MODEL), lambda i: (0, 0), pipeline_mode=resident),
        pl.BlockSpec((1, D_MODEL), lambda i: (0, 0)),
    ]
    args = [o, z, x, wb, wo, g_post]
    if unperm is not None:
        assert unperm.shape == (tm, tm)
        in_specs.append(pl.BlockSpec(unperm.shape, lambda i: (0, 0)))
        args.append(unperm)
    return pl.pallas_call(
        functools.partial(_branch_out_kernel, interleaved=unperm is not None),
        grid=(m // tm,),
        in_specs=in_specs,
        out_specs=pl.BlockSpec((tm, D_MODEL), lambda i: (i, 0)),
        out_shape=jax.ShapeDtypeStruct((m, D_MODEL), F32),
        compiler_params=pltpu.CompilerParams(
            dimension_semantics=("arbitrary",),
            vmem_limit_bytes=VMEM_LIMIT),
        name="branch_out",
    )(*args)


WROWS = 1024
PER_BRANCH = D_RNN // WROWS
assert D_RNN == D_POOL == D_X and D_RNN % WROWS == 0 and D_MODEL % WROWS == 0
N_WB_BLOCKS = N_BRANCH * PER_BRANCH
N_WOUT_BLOCKS = D_MODEL // WROWS


def _branch_out_cast_kernel(o_ref, gates_ref, x_ref, wb_ref, wo_ref, gpost_ref,
                            y_ref, wbb_ref, wob_ref, merged_ref, out_ref):
    s = pl.program_id(0)

    @pl.when(s < N_WB_BLOCKS)
    def _():
        w = wb_ref[...].astype(BF16)
        wbb_ref[...] = w
        term = _sigmoid(gates_ref[...]) * jnp.dot(o_ref[...], w, preferred_element_type=F32)

        @pl.when(s == 0)
        def _():
            merged_ref[...] = term

        @pl.when(s > 0)
        def _():
            merged_ref[...] += term

    for kb in range(N_WOUT_BLOCKS):
        @pl.when(s == N_WB_BLOCKS + kb)
        def _(kb=kb):
            w = wo_ref[...].astype(BF16)
            wob_ref[...] = w
            part = jnp.dot(merged_ref[:, kb * WROWS:(kb + 1) * WROWS].astype(BF16), w,
                           preferred_element_type=F32)
            if kb == 0:
                out_ref[...] = part
            else:
                out_ref[...] += part

    @pl.when(s == N_WB_BLOCKS + N_WOUT_BLOCKS - 1)
    def _():
        out = out_ref[...]
        y_ref[...] = x_ref[...] + out * _rms_scale(out) * gpost_ref[...]


def _branch_out_cast(o, z, x, wb, wo, g_post):
    m = x.shape[0]
    gblk0 = (2 * D_MIX) // D_MODEL
    wb_blk = lambda s: jnp.minimum(s, N_WB_BLOCKS - 1)
    wo_blk = lambda s: jnp.maximum(s - N_WB_BLOCKS, 0)
    return pl.pallas_call(
        _branch_out_cast_kernel,
        grid=(N_WB_BLOCKS + N_WOUT_BLOCKS,),
        in_specs=[
            pl.BlockSpec((m, WROWS), lambda s: (0, wb_blk(s))),
            pl.BlockSpec((m, D_MODEL), lambda s: (0, gblk0 + wb_blk(s) // PER_BRANCH)),
            pl.BlockSpec((m, D_MODEL), lambda s: (0, 0)),
            pl.BlockSpec((WROWS, D_MODEL), lambda s: (wb_blk(s), 0)),
            pl.BlockSpec((WROWS, D_MODEL), lambda s: (wo_blk(s), 0)),
            pl.BlockSpec((1, D_MODEL), lambda s: (0, 0)),
        ],
        out_specs=[
            pl.BlockSpec((m, D_MODEL), lambda s: (0, 0)),
            pl.BlockSpec((WROWS, D_MODEL), lambda s: (wb_blk(s), 0)),
            pl.BlockSpec((WROWS, D_MODEL), lambda s: (wo_blk(s), 0)),
        ],
        out_shape=[
            jax.ShapeDtypeStruct((m, D_MODEL), F32),
            jax.ShapeDtypeStruct(wb.shape, BF16),
            jax.ShapeDtypeStruct(wo.shape, BF16),
        ],
        scratch_shapes=[pltpu.VMEM((m, D_MODEL), F32), pltpu.VMEM((m, D_MODEL), F32)],
        compiler_params=pltpu.CompilerParams(
            dimension_semantics=("arbitrary",),
            vmem_limit_bytes=BIG_VMEM_LIMIT),
        name="branch_out_cast",
    )(o, z, x, wb, wo, g_post)


def _mem_kv_kernel(x_ref, g_ref, w_ref, k_ref, v_ref, u_ref):
    j = pl.program_id(1)

    @pl.when(j == 0)
    def _():
        x = x_ref[...]
        u_ref[...] = (x * _rms_scale(x) * g_ref[...]).astype(BF16)

    res = jnp.dot(u_ref[...], w_ref[...].astype(BF16), preferred_element_type=F32)

    @pl.when(j == 0)
    def _():
        k_ref[...] = res

    @pl.when(j == 1)
    def _():
        v_ref[...] = res


def _mem_kv(x, g, w, tm):
    m, k = x.shape
    assert w.shape[1] == 2 * D_X
    half = pl.BlockSpec((tm, D_X), lambda i, j: (i, 0))
    return pl.pallas_call(
        _mem_kv_kernel,
        grid=(m // tm, 2),
        in_specs=[
            pl.BlockSpec((tm, k), lambda i, j: (i, 0)),
            pl.BlockSpec((1, k), lambda i, j: (0, 0)),
            pl.BlockSpec((k, D_X), lambda i, j: (0, j)),
        ],
        out_specs=[half, half],
        out_shape=[jax.ShapeDtypeStruct((m, D_X), F32)] * 2,
        scratch_shapes=[pltpu.VMEM((tm, k), BF16)],
        compiler_params=pltpu.CompilerParams(
            dimension_semantics=("arbitrary", "arbitrary"),
            vmem_limit_bytes=BIG_VMEM_LIMIT),
        name="mem_kv",
    )(x, g, w)


def kernel(x_prompt, x_sample, mem_prompt, state_rglru_h, state_conv, state_pool, cache_mem_k, cache_mem_v, g_pre, w_in, conv_w, conv_b, w_rg_a, b_rg_a, w_rg_x, b_rg_x, lru_lambda, w_pool, pool_scale, g_mem, w_kv, w_branch, w_out, g_post):
    batch, seq, _ = x_prompt.shape
    nb = x_sample.shape[0]
    depth = g_pre.shape[0]
    assert depth == 1 and x_sample.shape[1] == 1

    l = 0
    row = lambda v: v.reshape(1, -1)
    wax = jnp.concatenate([w_rg_a[l], w_rg_x[l]], axis=-1).astype(BF16)
    wpool = w_pool[l].astype(BF16)
    mix_params = (conv_w[l], row(conv_b[l]), wax, row(b_rg_a[l]), row(b_rg_x[l]),
                  row(lru_lambda[l]), wpool, row(pool_scale[l]))

    xp2 = x_prompt.reshape(batch * seq, D_MODEL)
    xs2 = x_sample.reshape(nb, D_MODEL)
    mem2 = mem_prompt.reshape(batch * N_MEM, D_MODEL)

    z_s, w_in_b = _sample_proj(xs2, row(g_pre[l]), w_in[l], tn=SAMPLE_PROJ_TN)
    qoff = 2 * D_RNN + 2 * D_POOL
    q_s = z_s[:, qoff:qoff + D_X].reshape(nb // ATTN_BB, ATTN_BB, D_X)

    mem_k, mem_v = _mem_kv(mem2, row(g_mem[l]), w_kv[l], tm=KV_TM)
    mem_k = mem_k.reshape(batch, N_MEM, D_X)
    mem_v = mem_v.reshape(batch, N_MEM, D_X)

    perm = _chunk_interleave()
    z_p = _prompt_proj(xp2, row(g_pre[l]), w_in_b, perm, tm=PROJ_TM, tn=PROJ_TN)
    o_p, h_p, c_p, p_p, attn_s = _prompt_mix(
        z_p, mem_k, mem_v, *mix_params, q_s, _cache_rows(cache_mem_k[l]),
        _cache_rows(cache_mem_v[l]), batch=batch, seq=seq, tm=MIX_TM)
    attn_s = attn_s.reshape(nb, D_X)

    o_s, h_s, c_s, p_s = _sample_mix(
        z_s, attn_s, state_conv[l].transpose(1, 0, 2), state_rglru_h[l],
        state_pool[l].transpose(1, 0, 2), *mix_params, tb=SAMPLE_MIX_TB)
    y_s, w_br_b, w_out_b = _branch_out_cast(o_s, z_s, xs2, w_branch[l], w_out[l], row(g_post[l]))

    y_p = _branch_out(o_p, z_p, xp2, w_br_b, w_out_b, row(g_post[l]), tm=MIX_TM, unperm=perm.T)

    return (
        y_p.reshape(batch, seq, D_MODEL),
        y_s.reshape(nb, 1, D_MODEL),
        h_p.reshape(1, batch, D_RNN),
        c_p.reshape(1, batch, CONV_W - 1, D_RNN),
        p_p.reshape(1, batch, POOL_HIST, D_POOL),
        mem_k.reshape(1, batch, N_MEM, N_XHEADS, XHEAD_DIM),
        mem_v.reshape(1, batch, N_MEM, N_XHEADS, XHEAD_DIM),
        h_s.reshape(1, nb, D_RNN),
        c_s.transpose(1, 0, 2)[None],
        p_s.transpose(1, 0, 2)[None],
    )
```

```python
import functools

import jax
import jax.numpy as jnp
from jax import lax
from jax.experimental import pallas as pl
from jax.experimental.pallas import tpu as pltpu

D_MODEL = 2048
PAST_LEN = 16384
D_RNN = 1024
N_RNN_BLOCKS = 8
RNN_BLOCK = D_RNN // N_RNN_BLOCKS
CONV_W = 4
LRU_C = 8.0
D_POOL = 1024
POOL_WINDOWS = (2, 4, 8, 16)
POOL_GROUP = D_POOL // len(POOL_WINDOWS)
POOL_HIST = max(POOL_WINDOWS) - 1
N_MEM = 256
N_XHEADS = 4
XHEAD_DIM = 256
D_X = N_XHEADS * XHEAD_DIM
N_BRANCH = 3
D_MIX = D_RNN + D_POOL + D_X
D_IN = 2 * D_MIX + N_BRANCH * D_MODEL
EPS = 1e-6

SUBLANES = 8
LANES = 128
VMEM_LIMIT = 56 * 1024 * 1024
BIG_VMEM_LIMIT = 60 * 1024 * 1024
MIX_TM = 256
PROJ_TM, PROJ_TN = 1024, 2048
SAMPLE_PROJ_TN = 1024
ATTN_BB = 4
SAMPLE_MIX_TB = 32
KV_TM = 1024

BF16 = jnp.bfloat16
F32 = jnp.float32

NEG_LOG2_E = -1.4426950408889634


def _sigmoid(x):
    return 1.0 / (1.0 + jnp.exp2(x * NEG_LOG2_E))


def _silu(x):
    return x * _sigmoid(x)


def _softplus(x):
    return jnp.maximum(x, 0.0) + jnp.log1p(jnp.exp(-jnp.abs(x)))


def _rms_scale(x):
    return lax.rsqrt(jnp.mean(x * x, axis=-1, keepdims=True) + EPS)


def _chunk_interleave():
    nrow = MIX_TM // SUBLANES
    p = jnp.arange(MIX_TM)
    token = (p % SUBLANES) * nrow + p // SUBLANES
    return (token[:, None] == jnp.arange(MIX_TM)[None, :]).astype(BF16)


def _sample_proj_kernel(x_ref, g_ref, w_ref, o_ref, wb_ref, u_ref):
    @pl.when(pl.program_id(0) == 0)
    def _():
        x = x_ref[...]
        u_ref[...] = (x * _rms_scale(x) * g_ref[...]).astype(BF16)

    w = w_ref[...].astype(BF16)
    wb_ref[...] = w
    o_ref[...] = jnp.dot(u_ref[...], w, preferred_element_type=F32)


def _sample_proj(x, g, w, tn):
    m, k = x.shape
    n = w.shape[1]
    return pl.pallas_call(
        _sample_proj_kernel,
        grid=(n // tn,),
        in_specs=[
            pl.BlockSpec((m, k), lambda j: (0, 0)),
            pl.BlockSpec((1, k), lambda j: (0, 0)),
            pl.BlockSpec((k, tn), lambda j: (0, j)),
        ],
        out_specs=[
            pl.BlockSpec((m, tn), lambda j: (0, j)),
            pl.BlockSpec((k, tn), lambda j: (0, j)),
        ],
        out_shape=[
            jax.ShapeDtypeStruct((m, n), F32),
            jax.ShapeDtypeStruct((k, n), BF16),
        ],
        scratch_shapes=[pltpu.VMEM((m, k), BF16)],
        compiler_params=pltpu.CompilerParams(
            dimension_semantics=("arbitrary",),
            vmem_limit_bytes=VMEM_LIMIT),
        name="sample_proj",
    )(x, g, w)


def _decay_rate(lam):
    return _softplus(-lam) * (LRU_C * NEG_LOG2_E)


def _rglru_block(xc, wax, ba, bx, rate):
    ri = jnp.dot(xc.astype(BF16), wax, preferred_element_type=F32)
    r = _sigmoid(ri[:, :RNN_BLOCK] + ba)
    i = _sigmoid(ri[:, RNN_BLOCK:] + bx)
    a = jnp.exp2(r * rate)
    one_m = 1.0 - a * a
    mult = jnp.where(one_m > 0.0, one_m * lax.rsqrt(one_m), 0.0)
    return a, mult * i * xc


def _prompt_mix_kernel(z_ref, k_ref, v_ref, convw_ref, convb_ref, wax_ref, ba_ref, bx_ref,
                       lam_ref, wpool_ref, pscale_ref, unperm_ref, sq_ref, sk_ref, sv_ref,
                       o_ref, newh_ref, newconv_ref, newpool_ref, sattn_ref,
                       conv_carry, pool_carry, h_carry, kb_ref, vb_ref, ac_scr, hl_scr, op_scr,
                       *, tm):
    l = pl.program_id(1)
    last = pl.num_programs(1) - 1
    nrow = tm // SUBLANES

    _sample_attn_block(sq_ref.at[0], sk_ref, sv_ref, sattn_ref.at[0], ATTN_BB)

    @pl.when(l == 0)
    def _():
        conv_carry[...] = jnp.zeros(conv_carry.shape, F32)
        pool_carry[...] = jnp.zeros(pool_carry.shape, F32)
        h_carry[...] = jnp.zeros(h_carry.shape, F32)
        kb_ref[...] = k_ref[0].astype(BF16)
        vb_ref[...] = v_ref[0].astype(BF16)

    chunk_id = lax.broadcasted_iota(jnp.int32, (SUBLANES, LANES), 0)
    first_chunk = chunk_id == 0

    def load_groups(col, width=LANES):
        return [z_ref[r * SUBLANES:(r + 1) * SUBLANES, col:col + width] for r in range(nrow)]

    def put(col, width, val):
        op_scr[:, col:col + width] = val.astype(BF16)

    def store_groups(col, rows, width=LANES):
        put(col, width, jnp.concatenate(rows, axis=0))

    def history(tail_group, carry_ref, j, c0):
        tail = pltpu.roll(tail_group, 1, 0)
        prev = jnp.where(first_chunk, carry_ref[j - 1, :, c0:c0 + LANES], tail)
        carry_ref[j - 1, :, c0:c0 + LANES] = tail
        return prev

    rate = _decay_rate(lam_ref[...])
    for n in range(N_RNN_BLOCKS):
        c0, c1 = n * RNN_BLOCK, (n + 1) * RNN_BLOCK
        xs = load_groups(c0)
        ext = [history(xs[nrow - j], conv_carry, j, c0) for j in range(CONV_W - 1, 0, -1)] + xs
        cw = [jnp.broadcast_to(convw_ref[k:k + 1, c0:c1], (SUBLANES, LANES)) for k in range(CONV_W)]
        cb = jnp.broadcast_to(convb_ref[:, c0:c1], (SUBLANES, LANES))
        xc = []
        for r in range(nrow):
            acc = cb + cw[0] * ext[r]
            for k in range(1, CONV_W):
                acc = acc + cw[k] * ext[r + k]
            xc.append(acc)
        a, b = _rglru_block(jnp.concatenate(xc, axis=0), wax_ref[n], ba_ref[:, c0:c1],
                            bx_ref[:, c0:c1], rate[:, c0:c1])
        ac_scr[:, c0:c1] = a
        hl_scr[:, c0:c1] = b

    acc_a = ac_scr[0:SUBLANES, :]
    acc_h = hl_scr[0:SUBLANES, :]
    for r in range(1, nrow):
        rows = slice(r * SUBLANES, (r + 1) * SUBLANES)
        ar = ac_scr[rows, :]
        acc_h = ar * acc_h + hl_scr[rows, :]
        acc_a = ar * acc_a
        ac_scr[rows, :] = acc_a
        hl_scr[rows, :] = acc_h
    h_in = h_carry[...]
    entering = []
    for c in range(SUBLANES):
        entering.append(h_in)
        h_in = acc_a[c:c + 1] * h_in + acc_h[c:c + 1]
    h_carry[...] = h_in
    h_enter = jnp.concatenate(entering, axis=0)
    for n in range(N_RNN_BLOCKS):
        c0, c1 = n * RNN_BLOCK, (n + 1) * RNN_BLOCK
        gr = load_groups(D_RNN + c0)
        store_groups(c0, [(hl_scr[r * SUBLANES:(r + 1) * SUBLANES, c0:c1]
                           + ac_scr[r * SUBLANES:(r + 1) * SUBLANES, c0:c1] * h_enter[:, c0:c1])
                          * _silu(gr[r]) for r in range(nrow)])

    pcol = 2 * D_RNN
    blocks = [(w, c0) for g, w in enumerate(POOL_WINDOWS)
              for c0 in range(g * POOL_GROUP, (g + 1) * POOL_GROUP, LANES)]

    def group(c0, r):
        return z_ref[r * SUBLANES:(r + 1) * SUBLANES, pcol + c0:pcol + c0 + LANES]

    def mean_minus_token(tot, w, c0, r):
        if r < w - 1:
            pos1 = l * tm + chunk_id * nrow + (r + 1)
            mean = tot / jnp.minimum(pos1, w).astype(F32)
        else:
            mean = tot * (1.0 / w)
        return mean - group(c0, r)

    hist, tot = {}, {}
    for w, c0 in blocks:
        hist[c0] = [history(group(c0, nrow - j), pool_carry, j, c0) for j in range(1, w)]
        t = group(c0, 0)
        for h in hist[c0]:
            t = t + h
        tot[c0] = t
        hl_scr[0:SUBLANES, c0:c0 + LANES] = mean_minus_token(t, w, c0, 0)
    for r in range(1, nrow):
        for w, c0 in blocks:
            leaving = group(c0, r - w) if r >= w else hist[c0][w - r - 1]
            tot[c0] = tot[c0] + (group(c0, r) - leaving)
            hl_scr[r * SUBLANES:(r + 1) * SUBLANES, c0:c0 + LANES] = mean_minus_token(
                tot[c0], w, c0, r)
    for g, w in enumerate(POOL_WINDOWS):
        c0, c1 = g * POOL_GROUP, (g + 1) * POOL_GROUP
        og = jnp.dot(hl_scr[:, c0:c1].astype(BF16), wpool_ref[g], preferred_element_type=F32)
        gp = z_ref[:, pcol + D_POOL + c0:pcol + D_POOL + c1]
        put(D_RNN + c0, POOL_GROUP, og * pscale_ref[:, c0:c1] * _silu(gp))

    qoff = 2 * D_RNN + 2 * D_POOL
    for hd in range(N_XHEADS):
        c0, c1 = hd * XHEAD_DIM, (hd + 1) * XHEAD_DIM
        q = z_ref[:, qoff + c0:qoff + c1].astype(BF16)
        s = lax.dot_general(q, kb_ref[:, c0:c1], (((1,), (1,)), ((), ())),
                            preferred_element_type=F32) * (XHEAD_DIM ** -0.5)
        p = jnp.exp(s - jnp.max(s, axis=-1, keepdims=True))
        p = p / jnp.sum(p, axis=-1, keepdims=True)
        ox = jnp.dot(p.astype(BF16), vb_ref[:, c0:c1], preferred_element_type=F32)
        gx = z_ref[:, qoff + D_X + c0:qoff + D_X + c1]
        put(D_RNN + D_POOL + c0, XHEAD_DIM, ox * _silu(gx))

    o_ref[...] = jnp.dot(unperm_ref[...], op_scr[...], preferred_element_type=F32).astype(BF16)

    @pl.when(l == last)
    def _():
        newh_ref[0] = h_carry[...]
        tail_row = lambda j: (nrow - j) * SUBLANES + SUBLANES - 1
        for j in range(1, CONV_W):
            newconv_ref[0, CONV_W - 1 - j:CONV_W - j, :] = z_ref[tail_row(j):tail_row(j) + 1, 0:D_RNN]
        for j in range(1, POOL_HIST + 1):
            newpool_ref[0, POOL_HIST - j:POOL_HIST - j + 1, :] = (
                z_ref[tail_row(j):tail_row(j) + 1, pcol:pcol + D_POOL])


def _prompt_mix(z, mem_k, mem_v, conv_w, conv_b, wax, b_a, b_x, lam, wpool, pscale, unperm,
                sample_q, cache_k, cache_v, batch, seq, tm):
    nl = seq // tm
    assert sample_q.shape[0] == batch * nl
    side = lambda b, l: (b * nl + l, 0, 0)
    zw = 2 * D_MIX
    const2 = lambda b, l: (0, 0)
    const3 = lambda b, l: (0, 0, 0)
    kern = functools.partial(_prompt_mix_kernel, tm=tm)
    return pl.pallas_call(
        kern,
        grid=(batch, nl),
        in_specs=[
            pl.BlockSpec((tm, zw), lambda b, l: (b * nl + l, 0)),
            pl.BlockSpec((1, N_MEM, D_X), lambda b, l: (b, 0, 0)),
            pl.BlockSpec((1, N_MEM, D_X), lambda b, l: (b, 0, 0)),
            pl.BlockSpec((CONV_W, D_RNN), const2),
            pl.BlockSpec((1, D_RNN), const2),
            pl.BlockSpec((N_RNN_BLOCKS, RNN_BLOCK, 2 * RNN_BLOCK), const3),
            pl.BlockSpec((1, D_RNN), const2),
            pl.BlockSpec((1, D_RNN), const2),
            pl.BlockSpec((1, D_RNN), const2),
            pl.BlockSpec((len(POOL_WINDOWS), POOL_GROUP, POOL_GROUP), const3),
            pl.BlockSpec((1, D_POOL), const2),
            pl.BlockSpec((tm, tm), const2),
            pl.BlockSpec((1, ATTN_BB, D_X), side),
            pl.BlockSpec((ATTN_BB, N_MEM * SUBLANES, LANES), side),
            pl.BlockSpec((ATTN_BB, N_MEM * SUBLANES, LANES), side),
        ],
        out_specs=[
            pl.BlockSpec((tm, D_MIX), lambda b, l: (b * nl + l, 0)),
            pl.BlockSpec((1, 1, D_RNN), lambda b, l: (b, 0, 0)),
            pl.BlockSpec((1, CONV_W - 1, D_RNN), lambda b, l: (b, 0, 0)),
            pl.BlockSpec((1, POOL_HIST, D_POOL), lambda b, l: (b, 0, 0)),
            pl.BlockSpec((1, ATTN_BB, D_X), side),
        ],
        out_shape=[
            jax.ShapeDtypeStruct((batch * seq, D_MIX), BF16),
            jax.ShapeDtypeStruct((batch, 1, D_RNN), F32),
            jax.ShapeDtypeStruct((batch, CONV_W - 1, D_RNN), F32),
            jax.ShapeDtypeStruct((batch, POOL_HIST, D_POOL), F32),
            jax.ShapeDtypeStruct(sample_q.shape, F32),
        ],
        scratch_shapes=[
            pltpu.VMEM((CONV_W - 1, SUBLANES, D_RNN), F32),
            pltpu.VMEM((POOL_HIST, SUBLANES, D_POOL), F32),
            pltpu.VMEM((1, D_RNN), F32),
            pltpu.VMEM((N_MEM, D_X), BF16),
            pltpu.VMEM((N_MEM, D_X), BF16),
            pltpu.VMEM((tm, D_RNN), F32),
            pltpu.VMEM((tm, D_RNN), F32),
            pltpu.VMEM((tm, D_MIX), BF16),
        ],
        compiler_params=pltpu.CompilerParams(
            dimension_semantics=("arbitrary", "arbitrary"),
            vmem_limit_bytes=VMEM_LIMIT),
        name="prompt_mix",
    )(z, mem_k, mem_v, conv_w, conv_b, wax, b_a, b_x, lam, wpool, pscale, unperm,
      sample_q, cache_k, cache_v)


def _cache_rows(c):
    nb = c.shape[0]
    c = c.reshape(nb, N_MEM, N_XHEADS, XHEAD_DIM // LANES, LANES)
    return c.transpose(0, 1, 3, 2, 4).reshape(nb, N_MEM * SUBLANES, LANES)


def _sample_attn_block(q_ref, k_ref, v_ref, o_ref, bb):
    halves = XHEAD_DIM // LANES
    assert halves * N_XHEADS == SUBLANES
    r = lax.broadcasted_iota(jnp.int32, (SUBLANES, LANES), 0)
    c = lax.broadcasted_iota(jnp.int32, (SUBLANES, LANES), 1)
    diag = (c % SUBLANES) == r
    first_half = r < N_XHEADS
    nchunk = N_MEM * SUBLANES // LANES
    scores = []
    for j in range(bb):
        qn = jnp.concatenate(
            [q_ref[j:j + 1, (h * halves + t) * LANES:(h * halves + t + 1) * LANES]
             for t in range(halves) for h in range(N_XHEADS)], axis=0)
        scores.append(lax.dot_general(qn.astype(BF16), k_ref[j].astype(BF16),
                                      (((1,), (1,)), ((), ())), preferred_element_type=F32)
                      * (XHEAD_DIM ** -0.5))
    probs = []
    for j in range(bb):
        s = scores[j]
        chunks = []
        for ci in range(nchunk):
            sm = jnp.where(diag, s[:, ci * LANES:(ci + 1) * LANES], 0.0)
            other = pltpu.roll(sm, N_XHEADS, 0)
            other = jnp.where(first_half, pltpu.roll(other, LANES - N_XHEADS, 1),
                              pltpu.roll(other, N_XHEADS, 1))
            chunks.append(jnp.where(diag, sm + other, -jnp.inf))
        t_full = jnp.concatenate(chunks, axis=1)
        e = jnp.exp(t_full - jnp.max(t_full, axis=1, keepdims=True))
        probs.append((e / jnp.sum(e, axis=1, keepdims=True)).astype(BF16))
    for j in range(bb):
        o = jnp.dot(probs[j], v_ref[j].astype(BF16), preferred_element_type=F32)
        for t in range(halves):
            for h in range(N_XHEADS):
                col = (h * halves + t) * LANES
                o_ref[j:j + 1, col:col + LANES] = o[t * N_XHEADS + h:t * N_XHEADS + h + 1, :]


def _prompt_proj_kernel(x_ref, g_ref, w_ref, perm_ref, o_ref, u_ref, up_ref, *, mix_steps):
    j = pl.program_id(1)

    @pl.when(j == 0)
    def _():
        x = x_ref[...]
        u = (x * _rms_scale(x) * g_ref[...]).astype(BF16)
        u_ref[...] = u
        for r0 in range(0, u.shape[0], MIX_TM):
            up_ref[r0:r0 + MIX_TM, :] = jnp.dot(
                perm_ref[...], u[r0:r0 + MIX_TM], preferred_element_type=F32).astype(BF16)

    @pl.when(j < mix_steps)
    def _():
        o_ref[...] = jnp.dot(up_ref[...], w_ref[...], preferred_element_type=F32)

    @pl.when(j >= mix_steps)
    def _():
        o_ref[...] = jnp.dot(u_ref[...], w_ref[...], preferred_element_type=F32)


def _prompt_proj(x, g, w, perm, tm, tn):
    m, k = x.shape
    n = w.shape[1]
    assert (2 * D_MIX) % tn == 0
    return pl.pallas_call(
        functools.partial(_prompt_proj_kernel, mix_steps=2 * D_MIX // tn),
        grid=(m // tm, n // tn),
        in_specs=[
            pl.BlockSpec((tm, k), lambda i, j: (i, 0)),
            pl.BlockSpec((1, k), lambda i, j: (0, 0)),
            pl.BlockSpec((k, tn), lambda i, j: (0, j)),
            pl.BlockSpec(perm.shape, lambda i, j: (0, 0)),
        ],
        out_specs=pl.BlockSpec((tm, tn), lambda i, j: (i, j)),
        out_shape=jax.ShapeDtypeStruct((m, n), F32),
        scratch_shapes=[pltpu.VMEM((tm, k), BF16), pltpu.VMEM((tm, k), BF16)],
        compiler_params=pltpu.CompilerParams(
            dimension_semantics=("arbitrary", "arbitrary"),
            vmem_limit_bytes=BIG_VMEM_LIMIT),
        name="prompt_proj",
    )(x, g, w, perm)


def _sample_mix_kernel(z_ref, attn_ref, conv_ref, h_ref, pool_ref,
                       convw_ref, convb_ref, wax_ref, ba_ref, bx_ref, lam_ref, wpool_ref,
                       pscale_ref, o_ref, newh_ref, newconv_ref, newpool_ref):
    xr = z_ref[:, 0:D_RNN]
    xc = convb_ref[...] + convw_ref[CONV_W - 1:CONV_W, :] * xr
    for k in range(CONV_W - 1):
        xc = xc + convw_ref[k:k + 1, :] * conv_ref[k]
    for k in range(CONV_W - 2):
        newconv_ref[k] = conv_ref[k + 1]
    newconv_ref[CONV_W - 2] = xr

    rate = _decay_rate(lam_ref[...])
    for n in range(N_RNN_BLOCKS):
        c0, c1 = n * RNN_BLOCK, (n + 1) * RNN_BLOCK
        a, b = _rglru_block(xc[:, c0:c1], wax_ref[n], ba_ref[:, c0:c1], bx_ref[:, c0:c1],
                            rate[:, c0:c1])
        h = a * h_ref[:, c0:c1] + b
        newh_ref[:, c0:c1] = h
        o_ref[:, c0:c1] = (h * _silu(z_ref[:, D_RNN + c0:D_RNN + c1])).astype(BF16)

    xp = z_ref[:, 2 * D_RNN:2 * D_RNN + D_POOL]
    for k in range(POOL_HIST - 1):
        newpool_ref[k] = pool_ref[k + 1]
    newpool_ref[POOL_HIST - 1] = xp
    for g, w in enumerate(POOL_WINDOWS):
        c0, c1 = g * POOL_GROUP, (g + 1) * POOL_GROUP
        xg = xp[:, c0:c1]
        tot = xg
        for j in range(1, w):
            tot = tot + pool_ref[POOL_HIST - j, :, c0:c1]
        cnt = float(min(PAST_LEN + 1, w))
        d = tot / cnt - xg
        og = jnp.dot(d.astype(BF16), wpool_ref[g], preferred_element_type=F32)
        gp = z_ref[:, 2 * D_RNN + D_POOL + c0:2 * D_RNN + D_POOL + c1]
        o_ref[:, D_RNN + c0:D_RNN + c1] = (og * pscale_ref[:, c0:c1] * _silu(gp)).astype(BF16)

    gx = z_ref[:, 2 * D_RNN + 2 * D_POOL + D_X:2 * D_MIX]
    o_ref[:, D_RNN + D_POOL:] = (attn_ref[...] * _silu(gx)).astype(BF16)


def _sample_mix(z, attn, conv, h, pool, conv_w, conv_b, wax, b_a, b_x, lam, wpool, pscale, tb):
    nb = z.shape[0]
    zw = 2 * D_MIX
    rows = lambda i: (i, 0)
    const2 = lambda i: (0, 0)
    const3 = lambda i: (0, 0, 0)
    hist = lambda i: (0, i, 0)
    return pl.pallas_call(
        _sample_mix_kernel,
        grid=(nb // tb,),
        in_specs=[
            pl.BlockSpec((tb, zw), rows),
            pl.BlockSpec((tb, D_X), rows),
            pl.BlockSpec((CONV_W - 1, tb, D_RNN), hist),
            pl.BlockSpec((tb, D_RNN), rows),
            pl.BlockSpec((POOL_HIST, tb, D_POOL), hist),
            pl.BlockSpec((CONV_W, D_RNN), const2),
            pl.BlockSpec((1, D_RNN), const2),
            pl.BlockSpec((N_RNN_BLOCKS, RNN_BLOCK, 2 * RNN_BLOCK), const3),
            pl.BlockSpec((1, D_RNN), const2),
            pl.BlockSpec((1, D_RNN), const2),
            pl.BlockSpec((1, D_RNN), const2),
            pl.BlockSpec((len(POOL_WINDOWS), POOL_GROUP, POOL_GROUP), const3),
            pl.BlockSpec((1, D_POOL), const2),
        ],
        out_specs=[
            pl.BlockSpec((tb, D_MIX), rows),
            pl.BlockSpec((tb, D_RNN), rows),
            pl.BlockSpec((CONV_W - 1, tb, D_RNN), hist),
            pl.BlockSpec((POOL_HIST, tb, D_POOL), hist),
        ],
        out_shape=[
            jax.ShapeDtypeStruct((nb, D_MIX), BF16),
            jax.ShapeDtypeStruct((nb, D_RNN), F32),
            jax.ShapeDtypeStruct((CONV_W - 1, nb, D_RNN), F32),
            jax.ShapeDtypeStruct((POOL_HIST, nb, D_POOL), F32),
        ],
        compiler_params=pltpu.CompilerParams(
            dimension_semantics=("arbitrary",),
            vmem_limit_bytes=VMEM_LIMIT),
        name="sample_mix",
    )(z, attn, conv, h, pool, conv_w, conv_b, wax, b_a, b_x, lam, wpool, pscale)


def _branch_out_kernel(o_ref, gates_ref, x_ref, wb_ref, wo_ref, gpost_ref, *rest, interleaved):
    y_ref = rest[-1]
    merged = None
    for j, (r0, r1) in enumerate(((0, D_RNN), (D_RNN, D_RNN + D_POOL), (D_RNN + D_POOL, D_MIX))):
        yj = jnp.dot(o_ref[:, r0:r1], wb_ref[r0:r1, :], preferred_element_type=F32)
        term = _sigmoid(gates_ref[:, j * D_MODEL:(j + 1) * D_MODEL]) * yj
        merged = term if merged is None else merged + term
    merged = merged.astype(BF16)
    if interleaved:
        merged = jnp.dot(rest[0][...], merged, preferred_element_type=F32).astype(BF16)
    out = jnp.dot(merged, wo_ref[...], preferred_element_type=F32)
    y_ref[...] = x_ref[...] + (out * gpost_ref[...]) * _rms_scale(out)


def _branch_out(o, z, x, wb, wo, g_post, tm, unperm=None):
    m = x.shape[0]
    gw = N_BRANCH * D_MODEL
    gblk = (2 * D_MIX) // gw
    resident = pl.Buffered(1)
    in_specs = [
        pl.BlockSpec((tm, D_MIX), lambda i: (i, 0)),
        pl.BlockSpec((tm, gw), lambda i: (i, gblk)),
        pl.BlockSpec((tm, D_MODEL), lambda i: (i, 0)),
        pl.BlockSpec((D_MIX, D_MODEL), lambda i: (0, 0), pipeline_mode=resident),
        pl.BlockSpec((D_MODEL, D_MODEL), lambda i: (0, 0), pipeline_mode=resident),
        pl.BlockSpec((1, D_MODEL), lambda i: (0, 0)),
    ]
    args = [o, z, x, wb, wo, g_post]
    if unperm is not None:
        assert unperm.shape == (tm, tm)
        in_specs.append(pl.BlockSpec(unperm.shape, lambda i: (0, 0)))
        args.append(unperm)
    return pl.pallas_call(
        functools.partial(_branch_out_kernel, interleaved=unperm is not None),
        grid=(m // tm,),
        in_specs=in_specs,
        out_specs=pl.BlockSpec((tm, D_MODEL), lambda i: (i, 0)),
        out_shape=jax.ShapeDtypeStruct((m, D_MODEL), F32),
        compiler_params=pltpu.CompilerParams(
            dimension_semantics=("arbitrary",),
            vmem_limit_bytes=VMEM_LIMIT),
        name="branch_out",
    )(*args)


WROWS = 1024
PER_BRANCH = D_RNN // WROWS
assert D_RNN == D_POOL == D_X and D_RNN % WROWS == 0 and D_MODEL % WROWS == 0
N_WB_BLOCKS = N_BRANCH * PER_BRANCH
N_WOUT_BLOCKS = D_MODEL // WROWS


def _branch_out_cast_kernel(o_ref, gates_ref, x_ref, wb_ref, wo_ref, gpost_ref,
                            y_ref, wbb_ref, wob_ref, merged_ref, out_ref):
    s = pl.program_id(0)

    @pl.when(s < N_WB_BLOCKS)
    def _():
        w = wb_ref[...].astype(BF16)
        wbb_ref[...] = w
        term = _sigmoid(gates_ref[...]) * jnp.dot(o_ref[...], w, preferred_element_type=F32)

        @pl.when(s == 0)
        def _():
            merged_ref[...] = term

        @pl.when(s > 0)
        def _():
            merged_ref[...] += term

    for kb in range(N_WOUT_BLOCKS):
        @pl.when(s == N_WB_BLOCKS + kb)
        def _(kb=kb):
            w = wo_ref[...].astype(BF16)
            wob_ref[...] = w
            part = jnp.dot(merged_ref[:, kb * WROWS:(kb + 1) * WROWS].astype(BF16), w,
                           preferred_element_type=F32)
            if kb == 0:
                out_ref[...] = part
            else:
                out_ref[...] += part

    @pl.when(s == N_WB_BLOCKS + N_WOUT_BLOCKS - 1)
    def _():
        out = out_ref[...]
        y_ref[...] = x_ref[...] + out * _rms_scale(out) * gpost_ref[...]


def _branch_out_cast(o, z, x, wb, wo, g_post):
    m = x.shape[0]
    gblk0 = (2 * D_MIX) // D_MODEL
    wb_blk = lambda s: jnp.minimum(s, N_WB_BLOCKS - 1)
    wo_blk = lambda s: jnp.maximum(s - N_WB_BLOCKS, 0)
    return pl.pallas_call(
        _branch_out_cast_kernel,
        grid=(N_WB_BLOCKS + N_WOUT_BLOCKS,),
        in_specs=[
            pl.BlockSpec((m, WROWS), lambda s: (0, wb_blk(s))),
            pl.BlockSpec((m, D_MODEL), lambda s: (0, gblk0 + wb_blk(s) // PER_BRANCH)),
            pl.BlockSpec((m, D_MODEL), lambda s: (0, 0)),
            pl.BlockSpec((WROWS, D_MODEL), lambda s: (wb_blk(s), 0)),
            pl.BlockSpec((WROWS, D_MODEL), lambda s: (wo_blk(s), 0)),
            pl.BlockSpec((1, D_MODEL), lambda s: (0, 0)),
        ],
        out_specs=[
            pl.BlockSpec((m, D_MODEL), lambda s: (0, 0)),
            pl.BlockSpec((WROWS, D_MODEL), lambda s: (wb_blk(s), 0)),
            pl.BlockSpec((WROWS, D_MODEL), lambda s: (wo_blk(s), 0)),
        ],
        out_shape=[
            jax.ShapeDtypeStruct((m, D_MODEL), F32),
            jax.ShapeDtypeStruct(wb.shape, BF16),
            jax.ShapeDtypeStruct(wo.shape, BF16),
        ],
        scratch_shapes=[pltpu.VMEM((m, D_MODEL), F32), pltpu.VMEM((m, D_MODEL), F32)],
        compiler_params=pltpu.CompilerParams(
            dimension_semantics=("arbitrary",),
            vmem_limit_bytes=BIG_VMEM_LIMIT),
        name="branch_out_cast",
    )(o, z, x, wb, wo, g_post)


def _mem_kv_kernel(x_ref, g_ref, w_ref, k_ref, v_ref, u_ref):
    j = pl.program_id(1)

    @pl.when(j == 0)
    def _():
        x = x_ref[...]
        u_ref[...] = (x * _rms_scale(x) * g_ref[...]).astype(BF16)

    res = jnp.dot(u_ref[...], w_ref[...].astype(BF16), preferred_element_type=F32)

    @pl.when(j == 0)
    def _():
        k_ref[...] = res

    @pl.when(j == 1)
    def _():
        v_ref[...] = res


def _mem_kv(x, g, w, tm):
    m, k = x.shape
    assert w.shape[1] == 2 * D_X
    half = pl.BlockSpec((tm, D_X), lambda i, j: (i, 0))
    return pl.pallas_call(
        _mem_kv_kernel,
        grid=(m // tm, 2),
        in_specs=[
            pl.BlockSpec((tm, k), lambda i, j: (i, 0)),
            pl.BlockSpec((1, k), lambda i, j: (0, 0)),
            pl.BlockSpec((k, D_X), lambda i, j: (0, j)),
        ],
        out_specs=[half, half],
        out_shape=[jax.ShapeDtypeStruct((m, D_X), F32)] * 2,
        scratch_shapes=[pltpu.VMEM((tm, k), BF16)],
        compiler_params=pltpu.CompilerParams(
            dimension_semantics=("arbitrary", "arbitrary"),
            vmem_limit_bytes=BIG_VMEM_LIMIT),
        name="mem_kv",
    )(x, g, w)


def kernel(x_prompt, x_sample, mem_prompt, state_rglru_h, state_conv, state_pool, cache_mem_k, cache_mem_v, g_pre, w_in, conv_w, conv_b, w_rg_a, b_rg_a, w_rg_x, b_rg_x, lru_lambda, w_pool, pool_scale, g_mem, w_kv, w_branch, w_out, g_post):
    batch, seq, _ = x_prompt.shape
    nb = x_sample.shape[0]
    depth = g_pre.shape[0]
    assert depth == 1 and x_sample.shape[1] == 1

    l = 0
    row = lambda v: v.reshape(1, -1)
    wax = jnp.concatenate([w_rg_a[l], w_rg_x[l]], axis=-1).astype(BF16)
    wpool = w_pool[l].astype(BF16)
    mix_params = (conv_w[l], row(conv_b[l]), wax, row(b_rg_a[l]), row(b_rg_x[l]),
                  row(lru_lambda[l]), wpool, row(pool_scale[l]))

    xp2 = x_prompt.reshape(batch * seq, D_MODEL)
    xs2 = x_sample.reshape(nb, D_MODEL)
    mem2 = mem_prompt.reshape(batch * N_MEM, D_MODEL)

    z_s, w_in_b = _sample_proj(xs2, row(g_pre[l]), w_in[l], tn=SAMPLE_PROJ_TN)
    qoff = 2 * D_RNN + 2 * D_POOL
    q_s = z_s[:, qoff:qoff + D_X].reshape(nb // ATTN_BB, ATTN_BB, D_X)

    mem_k, mem_v = _mem_kv(mem2, row(g_mem[l]), w_kv[l], tm=KV_TM)
    mem_k = mem_k.reshape(batch, N_MEM, D_X)
    mem_v = mem_v.reshape(batch, N_MEM, D_X)

    perm = _chunk_interleave()
    z_p = _prompt_proj(xp2, row(g_pre[l]), w_in_b, perm, tm=PROJ_TM, tn=PROJ_TN)
    o_p, h_p, c_p, p_p, attn_s = _prompt_mix(
        z_p, mem_k, mem_v, *mix_params, perm.T, q_s, _cache_rows(cache_mem_k[l]),
        _cache_rows(cache_mem_v[l]), batch=batch, seq=seq, tm=MIX_TM)
    attn_s = attn_s.reshape(nb, D_X)

    o_s, h_s, c_s, p_s = _sample_mix(
        z_s, attn_s, state_conv[l].transpose(1, 0, 2), state_rglru_h[l],
        state_pool[l].transpose(1, 0, 2), *mix_params, tb=SAMPLE_MIX_TB)
    y_s, w_br_b, w_out_b = _branch_out_cast(o_s, z_s, xs2, w_branch[l], w_out[l], row(g_post[l]))

    y_p = _branch_out(o_p, z_p, xp2, w_br_b, w_out_b, row(g_post[l]), tm=MIX_TM)

    return (
        y_p.reshape(batch, seq, D_MODEL),
        y_s.reshape(nb, 1, D_MODEL),
        h_p.reshape(1, batch, D_RNN),
        c_p.reshape(1, batch, CONV_W - 1, D_RNN),
        p_p.reshape(1, batch, POOL_HIST, D_POOL),
        mem_k.reshape(1, batch, N_MEM, N_XHEADS, XHEAD_DIM),
        mem_v.reshape(1, batch, N_MEM, N_XHEADS, XHEAD_DIM),
        h_s.reshape(1, nb, D_RNN),
        c_s.transpose(1, 0, 2)[None],
        p_s.transpose(1, 0, 2)[None],
    )
```

```python
import functools

import jax
import jax.numpy as jnp
from jax import lax
from jax.experimental import pallas as pl
from jax.experimental.pallas import tpu as pltpu

D_MODEL = 2048
PAST_LEN = 16384
D_RNN = 1024
N_RNN_BLOCKS = 8
RNN_BLOCK = D_RNN // N_RNN_BLOCKS
CONV_W = 4
LRU_C = 8.0
D_POOL = 1024
POOL_WINDOWS = (2, 4, 8, 16)
POOL_GROUP = D_POOL // len(POOL_WINDOWS)
POOL_HIST = max(POOL_WINDOWS) - 1
N_MEM = 256
N_XHEADS = 4
XHEAD_DIM = 256
D_X = N_XHEADS * XHEAD_DIM
N_BRANCH = 3
D_MIX = D_RNN + D_POOL + D_X
D_IN = 2 * D_MIX + N_BRANCH * D_MODEL
EPS = 1e-6

SUBLANES = 8
LANES = 128
VMEM_LIMIT = 56 * 1024 * 1024
BIG_VMEM_LIMIT = 60 * 1024 * 1024
MIX_TM = 256
BRANCH_TM = 256
PROJ_TM, PROJ_TN = 1024, 2048
SAMPLE_PROJ_TN = 1024
ATTN_BB = 4
SAMPLE_MIX_TB = 64
KV_TM = 1024

BF16 = jnp.bfloat16
F32 = jnp.float32

NEG_LOG2_E = -1.4426950408889634


def _sigmoid(x):
    return 1.0 / (1.0 + jnp.exp2(x * NEG_LOG2_E))


def _silu(x):
    return x * _sigmoid(x)


def _softplus(x):
    return jnp.maximum(x, 0.0) + jnp.log1p(jnp.exp(-jnp.abs(x)))


def _rms_scale(x):
    return lax.rsqrt(jnp.mean(x * x, axis=-1, keepdims=True) + EPS)


def _chunk_interleave():
    nrow = MIX_TM // SUBLANES
    p = jnp.arange(MIX_TM)
    token = (p % SUBLANES) * nrow + p // SUBLANES
    return (token[:, None] == jnp.arange(MIX_TM)[None, :]).astype(BF16)


def _sample_proj_kernel(x_ref, g_ref, w_ref, o_ref, wb_ref, u_ref):
    @pl.when(pl.program_id(0) == 0)
    def _():
        x = x_ref[...]
        u_ref[...] = (x * _rms_scale(x) * g_ref[...]).astype(BF16)

    w = w_ref[...].astype(BF16)
    wb_ref[...] = w
    o_ref[...] = jnp.dot(u_ref[...], w, preferred_element_type=F32)


def _sample_proj(x, g, w, tn):
    m, k = x.shape
    n = w.shape[1]
    return pl.pallas_call(
        _sample_proj_kernel,
        grid=(n // tn,),
        in_specs=[
            pl.BlockSpec((m, k), lambda j: (0, 0)),
            pl.BlockSpec((1, k), lambda j: (0, 0)),
            pl.BlockSpec((k, tn), lambda j: (0, j)),
        ],
        out_specs=[
            pl.BlockSpec((m, tn), lambda j: (0, j)),
            pl.BlockSpec((k, tn), lambda j: (0, j)),
        ],
        out_shape=[
            jax.ShapeDtypeStruct((m, n), F32),
            jax.ShapeDtypeStruct((k, n), BF16),
        ],
        scratch_shapes=[pltpu.VMEM((m, k), BF16)],
        compiler_params=pltpu.CompilerParams(
            dimension_semantics=("arbitrary",),
            vmem_limit_bytes=VMEM_LIMIT),
        name="sample_proj",
    )(x, g, w)


def _decay_rate(lam):
    return _softplus(-lam) * (LRU_C * NEG_LOG2_E)


def _rglru_block(xc, wax, ba, bx, rate):
    ri = jnp.dot(xc.astype(BF16), wax, preferred_element_type=F32)
    r = _sigmoid(ri[:, :RNN_BLOCK] + ba)
    i = _sigmoid(ri[:, RNN_BLOCK:] + bx)
    a = jnp.exp2(r * rate)
    one_m = 1.0 - a * a
    mult = jnp.where(one_m > 0.0, one_m * lax.rsqrt(one_m), 0.0)
    return a, mult * i * xc


def _prompt_mix_kernel(z_ref, k_ref, v_ref, convw_ref, convb_ref, wax_ref, ba_ref, bx_ref,
                       lam_ref, wpool_ref, pscale_ref, unperm_ref, sq_ref, sk_ref, sv_ref,
                       o_ref, newh_ref, newconv_ref, newpool_ref, sattn_ref,
                       conv_carry, pool_carry, h_carry, kb_ref, vb_ref, ac_scr, hl_scr, op_scr,
                       *, tm):
    l = pl.program_id(1)
    last = pl.num_programs(1) - 1
    nrow = tm // SUBLANES

    _sample_attn_block(sq_ref.at[0], sk_ref, sv_ref, sattn_ref.at[0], ATTN_BB)

    @pl.when(l == 0)
    def _():
        conv_carry[...] = jnp.zeros(conv_carry.shape, F32)
        pool_carry[...] = jnp.zeros(pool_carry.shape, F32)
        h_carry[...] = jnp.zeros(h_carry.shape, F32)
        kb_ref[...] = k_ref[0].astype(BF16)
        vb_ref[...] = v_ref[0].astype(BF16)

    chunk_id = lax.broadcasted_iota(jnp.int32, (SUBLANES, LANES), 0)
    first_chunk = chunk_id == 0

    def load_groups(col, width=LANES):
        return [z_ref[r * SUBLANES:(r + 1) * SUBLANES, col:col + width] for r in range(nrow)]

    def put(col, width, val):
        op_scr[:, col:col + width] = val.astype(BF16)

    def store_groups(col, rows, width=LANES):
        put(col, width, jnp.concatenate(rows, axis=0))

    def history(tail_group, carry_ref, j, c0):
        tail = pltpu.roll(tail_group, 1, 0)
        prev = jnp.where(first_chunk, carry_ref[j - 1, :, c0:c0 + LANES], tail)
        carry_ref[j - 1, :, c0:c0 + LANES] = tail
        return prev

    rate = _decay_rate(lam_ref[...])
    for n in range(N_RNN_BLOCKS):
        c0, c1 = n * RNN_BLOCK, (n + 1) * RNN_BLOCK
        xs = load_groups(c0)
        ext = [history(xs[nrow - j], conv_carry, j, c0) for j in range(CONV_W - 1, 0, -1)] + xs
        cw = [jnp.broadcast_to(convw_ref[k:k + 1, c0:c1], (SUBLANES, LANES)) for k in range(CONV_W)]
        cb = jnp.broadcast_to(convb_ref[:, c0:c1], (SUBLANES, LANES))
        xc = []
        for r in range(nrow):
            acc = cb + cw[0] * ext[r]
            for k in range(1, CONV_W):
                acc = acc + cw[k] * ext[r + k]
            xc.append(acc)
        a, b = _rglru_block(jnp.concatenate(xc, axis=0), wax_ref[n], ba_ref[:, c0:c1],
                            bx_ref[:, c0:c1], rate[:, c0:c1])
        ac_scr[:, c0:c1] = a
        hl_scr[:, c0:c1] = b

    acc_a = ac_scr[0:SUBLANES, :]
    acc_h = hl_scr[0:SUBLANES, :]
    for r in range(1, nrow):
        rows = slice(r * SUBLANES, (r + 1) * SUBLANES)
        ar = ac_scr[rows, :]
        acc_h = ar * acc_h + hl_scr[rows, :]
        acc_a = ar * acc_a
        ac_scr[rows, :] = acc_a
        hl_scr[rows, :] = acc_h
    h_in = h_carry[...]
    entering = []
    for c in range(SUBLANES):
        entering.append(h_in)
        h_in = acc_a[c:c + 1] * h_in + acc_h[c:c + 1]
    h_carry[...] = h_in
    h_enter = jnp.concatenate(entering, axis=0)
    for n in range(N_RNN_BLOCKS):
        c0, c1 = n * RNN_BLOCK, (n + 1) * RNN_BLOCK
        gr = load_groups(D_RNN + c0)
        store_groups(c0, [(hl_scr[r * SUBLANES:(r + 1) * SUBLANES, c0:c1]
                           + ac_scr[r * SUBLANES:(r + 1) * SUBLANES, c0:c1] * h_enter[:, c0:c1])
                          * _silu(gr[r]) for r in range(nrow)])

    pcol = 2 * D_RNN
    blocks = [(w, c0) for g, w in enumerate(POOL_WINDOWS)
              for c0 in range(g * POOL_GROUP, (g + 1) * POOL_GROUP, LANES)]

    def group(c0, r):
        return z_ref[r * SUBLANES:(r + 1) * SUBLANES, pcol + c0:pcol + c0 + LANES]

    def mean_minus_token(tot, w, c0, r):
        if r < w - 1:
            pos1 = l * tm + chunk_id * nrow + (r + 1)
            mean = tot / jnp.minimum(pos1, w).astype(F32)
        else:
            mean = tot * (1.0 / w)
        return mean - group(c0, r)

    hist, tot = {}, {}
    for w, c0 in blocks:
        hist[c0] = [history(group(c0, nrow - j), pool_carry, j, c0) for j in range(1, w)]
        t = group(c0, 0)
        for h in hist[c0]:
            t = t + h
        tot[c0] = t
        hl_scr[0:SUBLANES, c0:c0 + LANES] = mean_minus_token(t, w, c0, 0)
    for r in range(1, nrow):
        for w, c0 in blocks:
            leaving = group(c0, r - w) if r >= w else hist[c0][w - r - 1]
            tot[c0] = tot[c0] + (group(c0, r) - leaving)
            hl_scr[r * SUBLANES:(r + 1) * SUBLANES, c0:c0 + LANES] = mean_minus_token(
                tot[c0], w, c0, r)
    for g, w in enumerate(POOL_WINDOWS):
        c0, c1 = g * POOL_GROUP, (g + 1) * POOL_GROUP
        og = jnp.dot(hl_scr[:, c0:c1].astype(BF16), wpool_ref[g], preferred_element_type=F32)
        gp = z_ref[:, pcol + D_POOL + c0:pcol + D_POOL + c1]
        put(D_RNN + c0, POOL_GROUP, og * pscale_ref[:, c0:c1] * _silu(gp))

    qoff = 2 * D_RNN + 2 * D_POOL
    for hd in range(N_XHEADS):
        c0, c1 = hd * XHEAD_DIM, (hd + 1) * XHEAD_DIM
        q = z_ref[:, qoff + c0:qoff + c1].astype(BF16)
        s = lax.dot_general(q, kb_ref[:, c0:c1], (((1,), (1,)), ((), ())),
                            preferred_element_type=F32) * (XHEAD_DIM ** -0.5)
        p = jnp.exp(s - jnp.max(s, axis=-1, keepdims=True))
        p = p / jnp.sum(p, axis=-1, keepdims=True)
        ox = jnp.dot(p.astype(BF16), vb_ref[:, c0:c1], preferred_element_type=F32)
        gx = z_ref[:, qoff + D_X + c0:qoff + D_X + c1]
        put(D_RNN + D_POOL + c0, XHEAD_DIM, ox * _silu(gx))

    o_ref[...] = jnp.dot(unperm_ref[...], op_scr[...], preferred_element_type=F32).astype(BF16)

    @pl.when(l == last)
    def _():
        newh_ref[0] = h_carry[...]
        tail_row = lambda j: (nrow - j) * SUBLANES + SUBLANES - 1
        for j in range(1, CONV_W):
            newconv_ref[0, CONV_W - 1 - j:CONV_W - j, :] = z_ref[tail_row(j):tail_row(j) + 1, 0:D_RNN]
        for j in range(1, POOL_HIST + 1):
            newpool_ref[0, POOL_HIST - j:POOL_HIST - j + 1, :] = (
                z_ref[tail_row(j):tail_row(j) + 1, pcol:pcol + D_POOL])


def _prompt_mix(z, mem_k, mem_v, conv_w, conv_b, wax, b_a, b_x, lam, wpool, pscale, unperm,
                sample_q, cache_k, cache_v, batch, seq, tm):
    nl = seq // tm
    assert sample_q.shape[0] == batch * nl
    side = lambda b, l: (b * nl + l, 0, 0)
    zw = 2 * D_MIX
    const2 = lambda b, l: (0, 0)
    const3 = lambda b, l: (0, 0, 0)
    kern = functools.partial(_prompt_mix_kernel, tm=tm)
    return pl.pallas_call(
        kern,
        grid=(batch, nl),
        in_specs=[
            pl.BlockSpec((tm, zw), lambda b, l: (b * nl + l, 0)),
            pl.BlockSpec((1, N_MEM, D_X), lambda b, l: (b, 0, 0)),
            pl.BlockSpec((1, N_MEM, D_X), lambda b, l: (b, 0, 0)),
            pl.BlockSpec((CONV_W, D_RNN), const2),
            pl.BlockSpec((1, D_RNN), const2),
            pl.BlockSpec((N_RNN_BLOCKS, RNN_BLOCK, 2 * RNN_BLOCK), const3),
            pl.BlockSpec((1, D_RNN), const2),
            pl.BlockSpec((1, D_RNN), const2),
            pl.BlockSpec((1, D_RNN), const2),
            pl.BlockSpec((len(POOL_WINDOWS), POOL_GROUP, POOL_GROUP), const3),
            pl.BlockSpec((1, D_POOL), const2),
            pl.BlockSpec((tm, tm), const2),
            pl.BlockSpec((1, ATTN_BB, D_X), side),
            pl.BlockSpec((ATTN_BB, N_MEM * SUBLANES, LANES), side),
            pl.BlockSpec((ATTN_BB, N_MEM * SUBLANES, LANES), side),
        ],
        out_specs=[
            pl.BlockSpec((tm, D_MIX), lambda b, l: (b * nl + l, 0)),
            pl.BlockSpec((1, 1, D_RNN), lambda b, l: (b, 0, 0)),
            pl.BlockSpec((1, CONV_W - 1, D_RNN), lambda b, l: (b, 0, 0)),
            pl.BlockSpec((1, POOL_HIST, D_POOL), lambda b, l: (b, 0, 0)),
            pl.BlockSpec((1, ATTN_BB, D_X), side),
        ],
        out_shape=[
            jax.ShapeDtypeStruct((batch * seq, D_MIX), BF16),
            jax.ShapeDtypeStruct((batch, 1, D_RNN), F32),
            jax.ShapeDtypeStruct((batch, CONV_W - 1, D_RNN), F32),
            jax.ShapeDtypeStruct((batch, POOL_HIST, D_POOL), F32),
            jax.ShapeDtypeStruct(sample_q.shape, F32),
        ],
        scratch_shapes=[
            pltpu.VMEM((CONV_W - 1, SUBLANES, D_RNN), F32),
            pltpu.VMEM((POOL_HIST, SUBLANES, D_POOL), F32),
            pltpu.VMEM((1, D_RNN), F32),
            pltpu.VMEM((N_MEM, D_X), BF16),
            pltpu.VMEM((N_MEM, D_X), BF16),
            pltpu.VMEM((tm, D_RNN), F32),
            pltpu.VMEM((tm, D_RNN), F32),
            pltpu.VMEM((tm, D_MIX), BF16),
        ],
        compiler_params=pltpu.CompilerParams(
            dimension_semantics=("arbitrary", "arbitrary"),
            vmem_limit_bytes=VMEM_LIMIT),
        name="prompt_mix",
    )(z, mem_k, mem_v, conv_w, conv_b, wax, b_a, b_x, lam, wpool, pscale, unperm,
      sample_q, cache_k, cache_v)


def _cache_rows(c):
    nb = c.shape[0]
    c = c.reshape(nb, N_MEM, N_XHEADS, XHEAD_DIM // LANES, LANES)
    return c.transpose(0, 1, 3, 2, 4).reshape(nb, N_MEM * SUBLANES, LANES)


def _sample_attn_block(q_ref, k_ref, v_ref, o_ref, bb):
    halves = XHEAD_DIM // LANES
    assert halves * N_XHEADS == SUBLANES
    r = lax.broadcasted_iota(jnp.int32, (SUBLANES, LANES), 0)
    c = lax.broadcasted_iota(jnp.int32, (SUBLANES, LANES), 1)
    diag = (c % SUBLANES) == r
    first_half = r < N_XHEADS
    nchunk = N_MEM * SUBLANES // LANES
    scores = []
    for j in range(bb):
        qn = jnp.concatenate(
            [q_ref[j:j + 1, (h * halves + t) * LANES:(h * halves + t + 1) * LANES]
             for t in range(halves) for h in range(N_XHEADS)], axis=0)
        scores.append(lax.dot_general(qn.astype(BF16), k_ref[j].astype(BF16),
                                      (((1,), (1,)), ((), ())), preferred_element_type=F32)
                      * (XHEAD_DIM ** -0.5))
    probs = []
    for j in range(bb):
        s = scores[j]
        chunks = []
        for ci in range(nchunk):
            sm = jnp.where(diag, s[:, ci * LANES:(ci + 1) * LANES], 0.0)
            other = pltpu.roll(sm, N_XHEADS, 0)
            other = jnp.where(first_half, pltpu.roll(other, LANES - N_XHEADS, 1),
                              pltpu.roll(other, N_XHEADS, 1))
            chunks.append(jnp.where(diag, sm + other, -jnp.inf))
        t_full = jnp.concatenate(chunks, axis=1)
        e = jnp.exp(t_full - jnp.max(t_full, axis=1, keepdims=True))
        probs.append((e / jnp.sum(e, axis=1, keepdims=True)).astype(BF16))
    for j in range(bb):
        o = jnp.dot(probs[j], v_ref[j].astype(BF16), preferred_element_type=F32)
        for t in range(halves):
            for h in range(N_XHEADS):
                col = (h * halves + t) * LANES
                o_ref[j:j + 1, col:col + LANES] = o[t * N_XHEADS + h:t * N_XHEADS + h + 1, :]


def _prompt_proj_kernel(x_ref, g_ref, w_ref, perm_ref, o_ref, u_ref, up_ref, *, mix_steps):
    j = pl.program_id(1)

    @pl.when(j == 0)
    def _():
        x = x_ref[...]
        u = (x * _rms_scale(x) * g_ref[...]).astype(BF16)
        u_ref[...] = u
        for r0 in range(0, u.shape[0], MIX_TM):
            up_ref[r0:r0 + MIX_TM, :] = jnp.dot(
                perm_ref[...], u[r0:r0 + MIX_TM], preferred_element_type=F32).astype(BF16)

    @pl.when(j < mix_steps)
    def _():
        o_ref[...] = jnp.dot(up_ref[...], w_ref[...], preferred_element_type=F32)

    @pl.when(j >= mix_steps)
    def _():
        o_ref[...] = jnp.dot(u_ref[...], w_ref[...], preferred_element_type=F32)


def _prompt_proj(x, g, w, perm, tm, tn):
    m, k = x.shape
    n = w.shape[1]
    assert (2 * D_MIX) % tn == 0
    return pl.pallas_call(
        functools.partial(_prompt_proj_kernel, mix_steps=2 * D_MIX // tn),
        grid=(m // tm, n // tn),
        in_specs=[
            pl.BlockSpec((tm, k), lambda i, j: (i, 0)),
            pl.BlockSpec((1, k), lambda i, j: (0, 0)),
            pl.BlockSpec((k, tn), lambda i, j: (0, j)),
            pl.BlockSpec(perm.shape, lambda i, j: (0, 0)),
        ],
        out_specs=pl.BlockSpec((tm, tn), lambda i, j: (i, j)),
        out_shape=jax.ShapeDtypeStruct((m, n), F32),
        scratch_shapes=[pltpu.VMEM((tm, k), BF16), pltpu.VMEM((tm, k), BF16)],
        compiler_params=pltpu.CompilerParams(
            dimension_semantics=("arbitrary", "arbitrary"),
            vmem_limit_bytes=BIG_VMEM_LIMIT),
        name="prompt_proj",
    )(x, g, w, perm)


def _sample_mix_kernel(z_ref, attn_ref, conv_ref, h_ref, pool_ref,
                       convw_ref, convb_ref, wax_ref, ba_ref, bx_ref, lam_ref, wpool_ref,
                       pscale_ref, o_ref, newh_ref, newconv_ref, newpool_ref):
    xr = z_ref[:, 0:D_RNN]
    xc = convb_ref[...] + convw_ref[CONV_W - 1:CONV_W, :] * xr
    for k in range(CONV_W - 1):
        xc = xc + convw_ref[k:k + 1, :] * conv_ref[k]
    for k in range(CONV_W - 2):
        newconv_ref[k] = conv_ref[k + 1]
    newconv_ref[CONV_W - 2] = xr

    rate = _decay_rate(lam_ref[...])
    for n in range(N_RNN_BLOCKS):
        c0, c1 = n * RNN_BLOCK, (n + 1) * RNN_BLOCK
        a, b = _rglru_block(xc[:, c0:c1], wax_ref[n], ba_ref[:, c0:c1], bx_ref[:, c0:c1],
                            rate[:, c0:c1])
        h = a * h_ref[:, c0:c1] + b
        newh_ref[:, c0:c1] = h
        o_ref[:, c0:c1] = (h * _silu(z_ref[:, D_RNN + c0:D_RNN + c1])).astype(BF16)

    xp = z_ref[:, 2 * D_RNN:2 * D_RNN + D_POOL]
    for k in range(POOL_HIST - 1):
        newpool_ref[k] = pool_ref[k + 1]
    newpool_ref[POOL_HIST - 1] = xp
    for g, w in enumerate(POOL_WINDOWS):
        c0, c1 = g * POOL_GROUP, (g + 1) * POOL_GROUP
        xg = xp[:, c0:c1]
        tot = xg
        for j in range(1, w):
            tot = tot + pool_ref[POOL_HIST - j, :, c0:c1]
        cnt = float(min(PAST_LEN + 1, w))
        d = tot / cnt - xg
        og = jnp.dot(d.astype(BF16), wpool_ref[g], preferred_element_type=F32)
        gp = z_ref[:, 2 * D_RNN + D_POOL + c0:2 * D_RNN + D_POOL + c1]
        o_ref[:, D_RNN + c0:D_RNN + c1] = (og * pscale_ref[:, c0:c1] * _silu(gp)).astype(BF16)

    gx = z_ref[:, 2 * D_RNN + 2 * D_POOL + D_X:2 * D_MIX]
    o_ref[:, D_RNN + D_POOL:] = (attn_ref[...] * _silu(gx)).astype(BF16)


def _sample_mix(z, attn, conv, h, pool, conv_w, conv_b, wax, b_a, b_x, lam, wpool, pscale, tb):
    nb = z.shape[0]
    zw = 2 * D_MIX
    rows = lambda i: (i, 0)
    const2 = lambda i: (0, 0)
    const3 = lambda i: (0, 0, 0)
    hist = lambda i: (0, i, 0)
    return pl.pallas_call(
        _sample_mix_kernel,
        grid=(nb // tb,),
        in_specs=[
            pl.BlockSpec((tb, zw), rows),
            pl.BlockSpec((tb, D_X), rows),
            pl.BlockSpec((CONV_W - 1, tb, D_RNN), hist),
            pl.BlockSpec((tb, D_RNN), rows),
            pl.BlockSpec((POOL_HIST, tb, D_POOL), hist),
            pl.BlockSpec((CONV_W, D_RNN), const2),
            pl.BlockSpec((1, D_RNN), const2),
            pl.BlockSpec((N_RNN_BLOCKS, RNN_BLOCK, 2 * RNN_BLOCK), const3),
            pl.BlockSpec((1, D_RNN), const2),
            pl.BlockSpec((1, D_RNN), const2),
            pl.BlockSpec((1, D_RNN), const2),
            pl.BlockSpec((len(POOL_WINDOWS), POOL_GROUP, POOL_GROUP), const3),
            pl.BlockSpec((1, D_POOL), const2),
        ],
        out_specs=[
            pl.BlockSpec((tb, D_MIX), rows),
            pl.BlockSpec((tb, D_RNN), rows),
            pl.BlockSpec((CONV_W - 1, tb, D_RNN), hist),
            pl.BlockSpec((POOL_HIST, tb, D_POOL), hist),
        ],
        out_shape=[
            jax.ShapeDtypeStruct((nb, D_MIX), BF16),
            jax.ShapeDtypeStruct((nb, D_RNN), F32),
            jax.ShapeDtypeStruct((CONV_W - 1, nb, D_RNN), F32),
            jax.ShapeDtypeStruct((POOL_HIST, nb, D_POOL), F32),
        ],
        compiler_params=pltpu.CompilerParams(
            dimension_semantics=("arbitrary",),
            vmem_limit_bytes=VMEM_LIMIT),
        name="sample_mix",
    )(z, attn, conv, h, pool, conv_w, conv_b, wax, b_a, b_x, lam, wpool, pscale)


def _branch_out_kernel(o_ref, gates_ref, x_ref, wb_ref, wo_ref, gpost_ref, y_ref):
    merged = None
    for j, (r0, r1) in enumerate(((0, D_RNN), (D_RNN, D_RNN + D_POOL), (D_RNN + D_POOL, D_MIX))):
        yj = jnp.dot(o_ref[:, r0:r1], wb_ref[r0:r1, :], preferred_element_type=F32)
        term = _sigmoid(gates_ref[:, j * D_MODEL:(j + 1) * D_MODEL]) * yj
        merged = term if merged is None else merged + term
    out = jnp.dot(merged.astype(BF16), wo_ref[...], preferred_element_type=F32)
    y_ref[...] = x_ref[...] + (out * gpost_ref[...]) * _rms_scale(out)


def _branch_out(o, z, x, wb, wo, g_post, tm):
    m = x.shape[0]
    gw = N_BRANCH * D_MODEL
    gblk = (2 * D_MIX) // gw
    resident = pl.Buffered(1)
    return pl.pallas_call(
        _branch_out_kernel,
        grid=(m // tm,),
        in_specs=[
            pl.BlockSpec((tm, D_MIX), lambda i: (i, 0)),
            pl.BlockSpec((tm, gw), lambda i: (i, gblk)),
            pl.BlockSpec((tm, D_MODEL), lambda i: (i, 0)),
            pl.BlockSpec((D_MIX, D_MODEL), lambda i: (0, 0), pipeline_mode=resident),
            pl.BlockSpec((D_MODEL, D_MODEL), lambda i: (0, 0), pipeline_mode=resident),
            pl.BlockSpec((1, D_MODEL), lambda i: (0, 0)),
        ],
        out_specs=pl.BlockSpec((tm, D_MODEL), lambda i: (i, 0)),
        out_shape=jax.ShapeDtypeStruct((m, D_MODEL), F32),
        compiler_params=pltpu.CompilerParams(
            dimension_semantics=("arbitrary",),
            vmem_limit_bytes=VMEM_LIMIT),
        name="branch_out",
    )(o, z, x, wb, wo, g_post)


WROWS = 1024
PER_BRANCH = D_RNN // WROWS
assert D_RNN == D_POOL == D_X and D_RNN % WROWS == 0 and D_MODEL % WROWS == 0
N_WB_BLOCKS = N_BRANCH * PER_BRANCH
N_WOUT_BLOCKS = D_MODEL // WROWS


def _branch_out_cast_kernel(o_ref, gates_ref, x_ref, wb_ref, wo_ref, gpost_ref,
                            y_ref, wbb_ref, wob_ref, merged_ref, out_ref):
    s = pl.program_id(0)

    @pl.when(s < N_WB_BLOCKS)
    def _():
        w = wb_ref[...].astype(BF16)
        wbb_ref[...] = w
        term = _sigmoid(gates_ref[...]) * jnp.dot(o_ref[...], w, preferred_element_type=F32)

        @pl.when(s == 0)
        def _():
            merged_ref[...] = term

        @pl.when(s > 0)
        def _():
            merged_ref[...] += term

    for kb in range(N_WOUT_BLOCKS):
        @pl.when(s == N_WB_BLOCKS + kb)
        def _(kb=kb):
            w = wo_ref[...].astype(BF16)
            wob_ref[...] = w
            part = jnp.dot(merged_ref[:, kb * WROWS:(kb + 1) * WROWS].astype(BF16), w,
                           preferred_element_type=F32)
            if kb == 0:
                out_ref[...] = part
            else:
                out_ref[...] += part

    @pl.when(s == N_WB_BLOCKS + N_WOUT_BLOCKS - 1)
    def _():
        out = out_ref[...]
        y_ref[...] = x_ref[...] + out * _rms_scale(out) * gpost_ref[...]


def _branch_out_cast(o, z, x, wb, wo, g_post):
    m = x.shape[0]
    gblk0 = (2 * D_MIX) // D_MODEL
    wb_blk = lambda s: jnp.minimum(s, N_WB_BLOCKS - 1)
    wo_blk = lambda s: jnp.maximum(s - N_WB_BLOCKS, 0)
    return pl.pallas_call(
        _branch_out_cast_kernel,
        grid=(N_WB_BLOCKS + N_WOUT_BLOCKS,),
        in_specs=[
            pl.BlockSpec((m, WROWS), lambda s: (0, wb_blk(s))),
            pl.BlockSpec((m, D_MODEL), lambda s: (0, gblk0 + wb_blk(s) // PER_BRANCH)),
            pl.BlockSpec((m, D_MODEL), lambda s: (0, 0)),
            pl.BlockSpec((WROWS, D_MODEL), lambda s: (wb_blk(s), 0)),
            pl.BlockSpec((WROWS, D_MODEL), lambda s: (wo_blk(s), 0)),
            pl.BlockSpec((1, D_MODEL), lambda s: (0, 0)),
        ],
        out_specs=[
            pl.BlockSpec((m, D_MODEL), lambda s: (0, 0)),
            pl.BlockSpec((WROWS, D_MODEL), lambda s: (wb_blk(s), 0)),
            pl.BlockSpec((WROWS, D_MODEL), lambda s: (wo_blk(s), 0)),
        ],
        out_shape=[
            jax.ShapeDtypeStruct((m, D_MODEL), F32),
            jax.ShapeDtypeStruct(wb.shape, BF16),
            jax.ShapeDtypeStruct(wo.shape, BF16),
        ],
        scratch_shapes=[pltpu.VMEM((m, D_MODEL), F32), pltpu.VMEM((m, D_MODEL), F32)],
        compiler_params=pltpu.CompilerParams(
            dimension_semantics=("arbitrary",),
            vmem_limit_bytes=BIG_VMEM_LIMIT),
        name="branch_out_cast",
    )(o, z, x, wb, wo, g_post)


def _mem_kv_kernel(x_ref, g_ref, w_ref, k_ref, v_ref, u_ref):
    j = pl.program_id(1)

    @pl.when(j == 0)
    def _():
        x = x_ref[...]
        u_ref[...] = (x * _rms_scale(x) * g_ref[...]).astype(BF16)

    res = jnp.dot(u_ref[...], w_ref[...].astype(BF16), preferred_element_type=F32)

    @pl.when(j == 0)
    def _():
        k_ref[...] = res

    @pl.when(j == 1)
    def _():
        v_ref[...] = res


def _mem_kv(x, g, w, tm):
    m, k = x.shape
    assert w.shape[1] == 2 * D_X
    half = pl.BlockSpec((tm, D_X), lambda i, j: (i, 0))
    return pl.pallas_call(
        _mem_kv_kernel,
        grid=(m // tm, 2),
        in_specs=[
            pl.BlockSpec((tm, k), lambda i, j: (i, 0)),
            pl.BlockSpec((1, k), lambda i, j: (0, 0)),
            pl.BlockSpec((k, D_X), lambda i, j: (0, j)),
        ],
        out_specs=[half, half],
        out_shape=[jax.ShapeDtypeStruct((m, D_X), F32)] * 2,
        scratch_shapes=[pltpu.VMEM((tm, k), BF16)],
        compiler_params=pltpu.CompilerParams(
            dimension_semantics=("arbitrary", "arbitrary"),
            vmem_limit_bytes=BIG_VMEM_LIMIT),
        name="mem_kv",
    )(x, g, w)


def kernel(x_prompt, x_sample, mem_prompt, state_rglru_h, state_conv, state_pool, cache_mem_k, cache_mem_v, g_pre, w_in, conv_w, conv_b, w_rg_a, b_rg_a, w_rg_x, b_rg_x, lru_lambda, w_pool, pool_scale, g_mem, w_kv, w_branch, w_out, g_post):
    batch, seq, _ = x_prompt.shape
    nb = x_sample.shape[0]
    depth = g_pre.shape[0]
    assert depth == 1 and x_sample.shape[1] == 1

    l = 0
    row = lambda v: v.reshape(1, -1)
    wax = jnp.concatenate([w_rg_a[l], w_rg_x[l]], axis=-1).astype(BF16)
    wpool = w_pool[l].astype(BF16)
    mix_params = (conv_w[l], row(conv_b[l]), wax, row(b_rg_a[l]), row(b_rg_x[l]),
                  row(lru_lambda[l]), wpool, row(pool_scale[l]))

    xp2 = x_prompt.reshape(batch * seq, D_MODEL)
    xs2 = x_sample.reshape(nb, D_MODEL)
    mem2 = mem_prompt.reshape(batch * N_MEM, D_MODEL)

    z_s, w_in_b = _sample_proj(xs2, row(g_pre[l]), w_in[l], tn=SAMPLE_PROJ_TN)
    qoff = 2 * D_RNN + 2 * D_POOL
    q_s = z_s[:, qoff:qoff + D_X].reshape(nb // ATTN_BB, ATTN_BB, D_X)

    mem_k, mem_v = _mem_kv(mem2, row(g_mem[l]), w_kv[l], tm=KV_TM)
    mem_k = mem_k.reshape(batch, N_MEM, D_X)
    mem_v = mem_v.reshape(batch, N_MEM, D_X)

    perm = _chunk_interleave()
    z_p = _prompt_proj(xp2, row(g_pre[l]), w_in_b, perm, tm=PROJ_TM, tn=PROJ_TN)
    o_p, h_p, c_p, p_p, attn_s = _prompt_mix(
        z_p, mem_k, mem_v, *mix_params, perm.T, q_s, _cache_rows(cache_mem_k[l]),
        _cache_rows(cache_mem_v[l]), batch=batch, seq=seq, tm=MIX_TM)
    attn_s = attn_s.reshape(nb, D_X)

    o_s, h_s, c_s, p_s = _sample_mix(
        z_s, attn_s, state_conv[l].transpose(1, 0, 2), state_rglru_h[l],
        state_pool[l].transpose(1, 0, 2), *mix_params, tb=SAMPLE_MIX_TB)
    y_s, w_br_b, w_out_b = _branch_out_cast(o_s, z_s, xs2, w_branch[l], w_out[l], row(g_post[l]))

    y_p = _branch_out(o_p, z_p, xp2, w_br_b, w_out_b, row(g_post[l]), tm=BRANCH_TM)

    return (
        y_p.reshape(batch, seq, D_MODEL),
        y_s.reshape(nb, 1, D_MODEL),
        h_p.reshape(1, batch, D_RNN),
        c_p.reshape(1, batch, CONV_W - 1, D_RNN),
        p_p.reshape(1, batch, POOL_HIST, D_POOL),
        mem_k.reshape(1, batch, N_MEM, N_XHEADS, XHEAD_DIM),
        mem_v.reshape(1, batch, N_MEM, N_XHEADS, XHEAD_DIM),
        h_s.reshape(1, nb, D_RNN),
        c_s.transpose(1, 0, 2)[None],
        p_s.transpose(1, 0, 2)[None],
    )
```

```python
import functools

import jax
import jax.numpy as jnp
from jax import lax
from jax.experimental import pallas as pl
from jax.experimental.pallas import tpu as pltpu

D_MODEL = 2048
PAST_LEN = 16384
D_RNN = 1024
N_RNN_BLOCKS = 8
RNN_BLOCK = D_RNN // N_RNN_BLOCKS
CONV_W = 4
LRU_C = 8.0
D_POOL = 1024
POOL_WINDOWS = (2, 4, 8, 16)
POOL_GROUP = D_POOL // len(POOL_WINDOWS)
POOL_HIST = max(POOL_WINDOWS) - 1
N_MEM = 256
N_XHEADS = 4
XHEAD_DIM = 256
D_X = N_XHEADS * XHEAD_DIM
N_BRANCH = 3
D_MIX = D_RNN + D_POOL + D_X
D_IN = 2 * D_MIX + N_BRANCH * D_MODEL
EPS = 1e-6

SUBLANES = 8
LANES = 128
VMEM_LIMIT = 56 * 1024 * 1024
BIG_VMEM_LIMIT = 60 * 1024 * 1024
MIX_TM = 256
BRANCH_TM = 256
PROJ_TM, PROJ_TN = 1024, 2048
SAMPLE_PROJ_TN = 1024
ATTN_BB = 4
SAMPLE_MIX_TB = 64
KV_TM = 1024

BF16 = jnp.bfloat16
F32 = jnp.float32

NEG_LOG2_E = -1.4426950408889634


def _sigmoid(x):
    return 1.0 / (1.0 + jnp.exp2(x * NEG_LOG2_E))


def _silu(x):
    return x * _sigmoid(x)


def _softplus(x):
    return jnp.maximum(x, 0.0) + jnp.log1p(jnp.exp(-jnp.abs(x)))


def _rms_scale(x):
    return lax.rsqrt(jnp.mean(x * x, axis=-1, keepdims=True) + EPS)


def _chunk_interleave():
    nrow = MIX_TM // SUBLANES
    p = jnp.arange(MIX_TM)
    token = (p % SUBLANES) * nrow + p // SUBLANES
    return (token[:, None] == jnp.arange(MIX_TM)[None, :]).astype(BF16)


def _sample_proj_kernel(x_ref, g_ref, w_ref, o_ref, wb_ref, u_ref):
    @pl.when(pl.program_id(0) == 0)
    def _():
        x = x_ref[...]
        u_ref[...] = (x * _rms_scale(x) * g_ref[...]).astype(BF16)

    w = w_ref[...].astype(BF16)
    wb_ref[...] = w
    o_ref[...] = jnp.dot(u_ref[...], w, preferred_element_type=F32)


def _sample_proj(x, g, w, tn):
    m, k = x.shape
    n = w.shape[1]
    return pl.pallas_call(
        _sample_proj_kernel,
        grid=(n // tn,),
        in_specs=[
            pl.BlockSpec((m, k), lambda j: (0, 0)),
            pl.BlockSpec((1, k), lambda j: (0, 0)),
            pl.BlockSpec((k, tn), lambda j: (0, j)),
        ],
        out_specs=[
            pl.BlockSpec((m, tn), lambda j: (0, j)),
            pl.BlockSpec((k, tn), lambda j: (0, j)),
        ],
        out_shape=[
            jax.ShapeDtypeStruct((m, n), F32),
            jax.ShapeDtypeStruct((k, n), BF16),
        ],
        scratch_shapes=[pltpu.VMEM((m, k), BF16)],
        compiler_params=pltpu.CompilerParams(
            dimension_semantics=("arbitrary",),
            vmem_limit_bytes=VMEM_LIMIT),
        name="sample_proj",
    )(x, g, w)


def _decay_rate(lam):
    return _softplus(-lam) * (LRU_C * NEG_LOG2_E)


def _rglru_block(xc, wax, ba, bx, rate):
    ri = jnp.dot(xc.astype(BF16), wax, preferred_element_type=F32)
    r = _sigmoid(ri[:, :RNN_BLOCK] + ba)
    i = _sigmoid(ri[:, RNN_BLOCK:] + bx)
    a = jnp.exp2(r * rate)
    one_m = 1.0 - a * a
    mult = jnp.where(one_m > 0.0, one_m * lax.rsqrt(one_m), 0.0)
    return a, mult * i * xc


def _prompt_mix_kernel(z_ref, k_ref, v_ref, convw_ref, convb_ref, wax_ref, ba_ref, bx_ref,
                       lam_ref, wpool_ref, pscale_ref, unperm_ref, sq_ref, sk_ref, sv_ref,
                       o_ref, newh_ref, newconv_ref, newpool_ref, sattn_ref,
                       conv_carry, pool_carry, h_carry, kb_ref, vb_ref, ac_scr, hl_scr, op_scr,
                       *, tm):
    l = pl.program_id(1)
    last = pl.num_programs(1) - 1
    nrow = tm // SUBLANES

    @pl.when(l == 0)
    def _():
        conv_carry[...] = jnp.zeros(conv_carry.shape, F32)
        pool_carry[...] = jnp.zeros(pool_carry.shape, F32)
        h_carry[...] = jnp.zeros(h_carry.shape, F32)
        kb_ref[...] = k_ref[0].astype(BF16)
        vb_ref[...] = v_ref[0].astype(BF16)

    side_scores = _sample_attn_scores(sq_ref.at[0], sk_ref, ATTN_BB)

    chunk_id = lax.broadcasted_iota(jnp.int32, (SUBLANES, LANES), 0)
    first_chunk = chunk_id == 0

    def load_groups(col, width=LANES):
        return [z_ref[r * SUBLANES:(r + 1) * SUBLANES, col:col + width] for r in range(nrow)]

    def put(col, width, val):
        op_scr[:, col:col + width] = val.astype(BF16)

    def store_groups(col, rows, width=LANES):
        put(col, width, jnp.concatenate(rows, axis=0))

    def history(tail_group, carry_ref, j, c0):
        tail = pltpu.roll(tail_group, 1, 0)
        prev = jnp.where(first_chunk, carry_ref[j - 1, :, c0:c0 + LANES], tail)
        carry_ref[j - 1, :, c0:c0 + LANES] = tail
        return prev

    rate = _decay_rate(lam_ref[...])
    for n in range(N_RNN_BLOCKS):
        c0, c1 = n * RNN_BLOCK, (n + 1) * RNN_BLOCK
        xs = load_groups(c0)
        ext = [history(xs[nrow - j], conv_carry, j, c0) for j in range(CONV_W - 1, 0, -1)] + xs
        cw = [jnp.broadcast_to(convw_ref[k:k + 1, c0:c1], (SUBLANES, LANES)) for k in range(CONV_W)]
        cb = jnp.broadcast_to(convb_ref[:, c0:c1], (SUBLANES, LANES))
        xc = []
        for r in range(nrow):
            acc = cb + cw[0] * ext[r]
            for k in range(1, CONV_W):
                acc = acc + cw[k] * ext[r + k]
            xc.append(acc)
        a, b = _rglru_block(jnp.concatenate(xc, axis=0), wax_ref[n], ba_ref[:, c0:c1],
                            bx_ref[:, c0:c1], rate[:, c0:c1])
        ac_scr[:, c0:c1] = a
        hl_scr[:, c0:c1] = b

    side_probs = _sample_attn_probs(side_scores)

    acc_a = ac_scr[0:SUBLANES, :]
    acc_h = hl_scr[0:SUBLANES, :]
    for r in range(1, nrow):
        rows = slice(r * SUBLANES, (r + 1) * SUBLANES)
        ar = ac_scr[rows, :]
        acc_h = ar * acc_h + hl_scr[rows, :]
        acc_a = ar * acc_a
        ac_scr[rows, :] = acc_a
        hl_scr[rows, :] = acc_h
    h_in = h_carry[...]
    entering = []
    for c in range(SUBLANES):
        entering.append(h_in)
        h_in = acc_a[c:c + 1] * h_in + acc_h[c:c + 1]
    h_carry[...] = h_in
    h_enter = jnp.concatenate(entering, axis=0)
    for n in range(N_RNN_BLOCKS):
        c0, c1 = n * RNN_BLOCK, (n + 1) * RNN_BLOCK
        gr = load_groups(D_RNN + c0)
        store_groups(c0, [(hl_scr[r * SUBLANES:(r + 1) * SUBLANES, c0:c1]
                           + ac_scr[r * SUBLANES:(r + 1) * SUBLANES, c0:c1] * h_enter[:, c0:c1])
                          * _silu(gr[r]) for r in range(nrow)])

    _sample_attn_values(side_probs, sv_ref, sattn_ref.at[0])

    pcol = 2 * D_RNN
    blocks = [(w, c0) for g, w in enumerate(POOL_WINDOWS)
              for c0 in range(g * POOL_GROUP, (g + 1) * POOL_GROUP, LANES)]

    def group(c0, r):
        return z_ref[r * SUBLANES:(r + 1) * SUBLANES, pcol + c0:pcol + c0 + LANES]

    def mean_minus_token(tot, w, c0, r):
        if r < w - 1:
            pos1 = l * tm + chunk_id * nrow + (r + 1)
            mean = tot / jnp.minimum(pos1, w).astype(F32)
        else:
            mean = tot * (1.0 / w)
        return mean - group(c0, r)

    hist, tot = {}, {}
    for w, c0 in blocks:
        hist[c0] = [history(group(c0, nrow - j), pool_carry, j, c0) for j in range(1, w)]
        t = group(c0, 0)
        for h in hist[c0]:
            t = t + h
        tot[c0] = t
        hl_scr[0:SUBLANES, c0:c0 + LANES] = mean_minus_token(t, w, c0, 0)
    for r in range(1, nrow):
        for w, c0 in blocks:
            leaving = group(c0, r - w) if r >= w else hist[c0][w - r - 1]
            tot[c0] = tot[c0] + (group(c0, r) - leaving)
            hl_scr[r * SUBLANES:(r + 1) * SUBLANES, c0:c0 + LANES] = mean_minus_token(
                tot[c0], w, c0, r)
    for g, w in enumerate(POOL_WINDOWS):
        c0, c1 = g * POOL_GROUP, (g + 1) * POOL_GROUP
        og = jnp.dot(hl_scr[:, c0:c1].astype(BF16), wpool_ref[g], preferred_element_type=F32)
        gp = z_ref[:, pcol + D_POOL + c0:pcol + D_POOL + c1]
        put(D_RNN + c0, POOL_GROUP, og * pscale_ref[:, c0:c1] * _silu(gp))

    qoff = 2 * D_RNN + 2 * D_POOL
    for hd in range(N_XHEADS):
        c0, c1 = hd * XHEAD_DIM, (hd + 1) * XHEAD_DIM
        q = z_ref[:, qoff + c0:qoff + c1].astype(BF16)
        s = lax.dot_general(q, kb_ref[:, c0:c1], (((1,), (1,)), ((), ())),
                            preferred_element_type=F32) * (XHEAD_DIM ** -0.5)
        p = jnp.exp(s - jnp.max(s, axis=-1, keepdims=True))
        p = p / jnp.sum(p, axis=-1, keepdims=True)
        ox = jnp.dot(p.astype(BF16), vb_ref[:, c0:c1], preferred_element_type=F32)
        gx = z_ref[:, qoff + D_X + c0:qoff + D_X + c1]
        put(D_RNN + D_POOL + c0, XHEAD_DIM, ox * _silu(gx))

    o_ref[...] = jnp.dot(unperm_ref[...], op_scr[...], preferred_element_type=F32).astype(BF16)

    @pl.when(l == last)
    def _():
        newh_ref[0] = h_carry[...]
        tail_row = lambda j: (nrow - j) * SUBLANES + SUBLANES - 1
        for j in range(1, CONV_W):
            newconv_ref[0, CONV_W - 1 - j:CONV_W - j, :] = z_ref[tail_row(j):tail_row(j) + 1, 0:D_RNN]
        for j in range(1, POOL_HIST + 1):
            newpool_ref[0, POOL_HIST - j:POOL_HIST - j + 1, :] = (
                z_ref[tail_row(j):tail_row(j) + 1, pcol:pcol + D_POOL])


def _prompt_mix(z, mem_k, mem_v, conv_w, conv_b, wax, b_a, b_x, lam, wpool, pscale, unperm,
                sample_q, cache_k, cache_v, batch, seq, tm):
    nl = seq // tm
    assert sample_q.shape[0] == batch * nl
    side = lambda b, l: (b * nl + l, 0, 0)
    zw = 2 * D_MIX
    const2 = lambda b, l: (0, 0)
    const3 = lambda b, l: (0, 0, 0)
    kern = functools.partial(_prompt_mix_kernel, tm=tm)
    return pl.pallas_call(
        kern,
        grid=(batch, nl),
        in_specs=[
            pl.BlockSpec((tm, zw), lambda b, l: (b * nl + l, 0)),
            pl.BlockSpec((1, N_MEM, D_X), lambda b, l: (b, 0, 0)),
            pl.BlockSpec((1, N_MEM, D_X), lambda b, l: (b, 0, 0)),
            pl.BlockSpec((CONV_W, D_RNN), const2),
            pl.BlockSpec((1, D_RNN), const2),
            pl.BlockSpec((N_RNN_BLOCKS, RNN_BLOCK, 2 * RNN_BLOCK), const3),
            pl.BlockSpec((1, D_RNN), const2),
            pl.BlockSpec((1, D_RNN), const2),
            pl.BlockSpec((1, D_RNN), const2),
            pl.BlockSpec((len(POOL_WINDOWS), POOL_GROUP, POOL_GROUP), const3),
            pl.BlockSpec((1, D_POOL), const2),
            pl.BlockSpec((tm, tm), const2),
            pl.BlockSpec((1, ATTN_BB, D_X), side),
            pl.BlockSpec((ATTN_BB, N_MEM * SUBLANES, LANES), side),
            pl.BlockSpec((ATTN_BB, N_MEM * SUBLANES, LANES), side),
        ],
        out_specs=[
            pl.BlockSpec((tm, D_MIX), lambda b, l: (b * nl + l, 0)),
            pl.BlockSpec((1, 1, D_RNN), lambda b, l: (b, 0, 0)),
            pl.BlockSpec((1, CONV_W - 1, D_RNN), lambda b, l: (b, 0, 0)),
            pl.BlockSpec((1, POOL_HIST, D_POOL), lambda b, l: (b, 0, 0)),
            pl.BlockSpec((1, ATTN_BB, D_X), side),
        ],
        out_shape=[
            jax.ShapeDtypeStruct((batch * seq, D_MIX), BF16),
            jax.ShapeDtypeStruct((batch, 1, D_RNN), F32),
            jax.ShapeDtypeStruct((batch, CONV_W - 1, D_RNN), F32),
            jax.ShapeDtypeStruct((batch, POOL_HIST, D_POOL), F32),
            jax.ShapeDtypeStruct(sample_q.shape, F32),
        ],
        scratch_shapes=[
            pltpu.VMEM((CONV_W - 1, SUBLANES, D_RNN), F32),
            pltpu.VMEM((POOL_HIST, SUBLANES, D_POOL), F32),
            pltpu.VMEM((1, D_RNN), F32),
            pltpu.VMEM((N_MEM, D_X), BF16),
            pltpu.VMEM((N_MEM, D_X), BF16),
            pltpu.VMEM((tm, D_RNN), F32),
            pltpu.VMEM((tm, D_RNN), F32),
            pltpu.VMEM((tm, D_MIX), BF16),
        ],
        compiler_params=pltpu.CompilerParams(
            dimension_semantics=("arbitrary", "arbitrary"),
            vmem_limit_bytes=VMEM_LIMIT),
        name="prompt_mix",
    )(z, mem_k, mem_v, conv_w, conv_b, wax, b_a, b_x, lam, wpool, pscale, unperm,
      sample_q, cache_k, cache_v)


def _cache_rows(c):
    nb = c.shape[0]
    c = c.reshape(nb, N_MEM, N_XHEADS, XHEAD_DIM // LANES, LANES)
    return c.transpose(0, 1, 3, 2, 4).reshape(nb, N_MEM * SUBLANES, LANES)


def _sample_attn_scores(q_ref, k_ref, bb):
    halves = XHEAD_DIM // LANES
    assert halves * N_XHEADS == SUBLANES
    scores = []
    for j in range(bb):
        qn = jnp.concatenate(
            [q_ref[j:j + 1, (h * halves + t) * LANES:(h * halves + t + 1) * LANES]
             for t in range(halves) for h in range(N_XHEADS)], axis=0)
        scores.append(lax.dot_general(qn.astype(BF16), k_ref[j].astype(BF16),
                                      (((1,), (1,)), ((), ())), preferred_element_type=F32)
                      * (XHEAD_DIM ** -0.5))
    return scores


def _sample_attn_probs(scores):
    r = lax.broadcasted_iota(jnp.int32, (SUBLANES, LANES), 0)
    c = lax.broadcasted_iota(jnp.int32, (SUBLANES, LANES), 1)
    diag = (c % SUBLANES) == r
    first_half = r < N_XHEADS
    nchunk = N_MEM * SUBLANES // LANES
    probs = []
    for s in scores:
        chunks = []
        for ci in range(nchunk):
            sm = jnp.where(diag, s[:, ci * LANES:(ci + 1) * LANES], 0.0)
            other = pltpu.roll(sm, N_XHEADS, 0)
            other = jnp.where(first_half, pltpu.roll(other, LANES - N_XHEADS, 1),
                              pltpu.roll(other, N_XHEADS, 1))
            chunks.append(jnp.where(diag, sm + other, -jnp.inf))
        t_full = jnp.concatenate(chunks, axis=1)
        e = jnp.exp(t_full - jnp.max(t_full, axis=1, keepdims=True))
        probs.append((e / jnp.sum(e, axis=1, keepdims=True)).astype(BF16))
    return probs


def _sample_attn_values(probs, v_ref, o_ref):
    halves = XHEAD_DIM // LANES
    for j, p in enumerate(probs):
        o = jnp.dot(p, v_ref[j].astype(BF16), preferred_element_type=F32)
        for t in range(halves):
            for h in range(N_XHEADS):
                col = (h * halves + t) * LANES
                o_ref[j:j + 1, col:col + LANES] = o[t * N_XHEADS + h:t * N_XHEADS + h + 1, :]


def _prompt_proj_kernel(x_ref, g_ref, w_ref, perm_ref, o_ref, u_ref, up_ref, *, mix_steps):
    j = pl.program_id(1)

    @pl.when(j == 0)
    def _():
        x = x_ref[...]
        u = (x * _rms_scale(x) * g_ref[...]).astype(BF16)
        u_ref[...] = u
        for r0 in range(0, u.shape[0], MIX_TM):
            up_ref[r0:r0 + MIX_TM, :] = jnp.dot(
                perm_ref[...], u[r0:r0 + MIX_TM], preferred_element_type=F32).astype(BF16)

    @pl.when(j < mix_steps)
    def _():
        o_ref[...] = jnp.dot(up_ref[...], w_ref[...], preferred_element_type=F32)

    @pl.when(j >= mix_steps)
    def _():
        o_ref[...] = jnp.dot(u_ref[...], w_ref[...], preferred_element_type=F32)


def _prompt_proj(x, g, w, perm, tm, tn):
    m, k = x.shape
    n = w.shape[1]
    assert (2 * D_MIX) % tn == 0
    return pl.pallas_call(
        functools.partial(_prompt_proj_kernel, mix_steps=2 * D_MIX // tn),
        grid=(m // tm, n // tn),
        in_specs=[
            pl.BlockSpec((tm, k), lambda i, j: (i, 0)),
            pl.BlockSpec((1, k), lambda i, j: (0, 0)),
            pl.BlockSpec((k, tn), lambda i, j: (0, j)),
            pl.BlockSpec(perm.shape, lambda i, j: (0, 0)),
        ],
        out_specs=pl.BlockSpec((tm, tn), lambda i, j: (i, j)),
        out_shape=jax.ShapeDtypeStruct((m, n), F32),
        scratch_shapes=[pltpu.VMEM((tm, k), BF16), pltpu.VMEM((tm, k), BF16)],
        compiler_params=pltpu.CompilerParams(
            dimension_semantics=("arbitrary", "arbitrary"),
            vmem_limit_bytes=BIG_VMEM_LIMIT),
        name="prompt_proj",
    )(x, g, w, perm)


def _sample_mix_kernel(z_ref, attn_ref, conv_ref, h_ref, pool_ref,
                       convw_ref, convb_ref, wax_ref, ba_ref, bx_ref, lam_ref, wpool_ref,
                       pscale_ref, o_ref, newh_ref, newconv_ref, newpool_ref):
    xr = z_ref[:, 0:D_RNN]
    xc = convb_ref[...] + convw_ref[CONV_W - 1:CONV_W, :] * xr
    for k in range(CONV_W - 1):
        xc = xc + convw_ref[k:k + 1, :] * conv_ref[k]
    for k in range(CONV_W - 2):
        newconv_ref[k] = conv_ref[k + 1]
    newconv_ref[CONV_W - 2] = xr

    rate = _decay_rate(lam_ref[...])
    for n in range(N_RNN_BLOCKS):
        c0, c1 = n * RNN_BLOCK, (n + 1) * RNN_BLOCK
        a, b = _rglru_block(xc[:, c0:c1], wax_ref[n], ba_ref[:, c0:c1], bx_ref[:, c0:c1],
                            rate[:, c0:c1])
        h = a * h_ref[:, c0:c1] + b
        newh_ref[:, c0:c1] = h
        o_ref[:, c0:c1] = (h * _silu(z_ref[:, D_RNN + c0:D_RNN + c1])).astype(BF16)

    xp = z_ref[:, 2 * D_RNN:2 * D_RNN + D_POOL]
    for k in range(POOL_HIST - 1):
        newpool_ref[k] = pool_ref[k + 1]
    newpool_ref[POOL_HIST - 1] = xp
    for g, w in enumerate(POOL_WINDOWS):
        c0, c1 = g * POOL_GROUP, (g + 1) * POOL_GROUP
        xg = xp[:, c0:c1]
        tot = xg
        for j in range(1, w):
            tot = tot + pool_ref[POOL_HIST - j, :, c0:c1]
        cnt = float(min(PAST_LEN + 1, w))
        d = tot / cnt - xg
        og = jnp.dot(d.astype(BF16), wpool_ref[g], preferred_element_type=F32)
        gp = z_ref[:, 2 * D_RNN + D_POOL + c0:2 * D_RNN + D_POOL + c1]
        o_ref[:, D_RNN + c0:D_RNN + c1] = (og * pscale_ref[:, c0:c1] * _silu(gp)).astype(BF16)

    gx = z_ref[:, 2 * D_RNN + 2 * D_POOL + D_X:2 * D_MIX]
    o_ref[:, D_RNN + D_POOL:] = (attn_ref[...] * _silu(gx)).astype(BF16)


def _sample_mix(z, attn, conv, h, pool, conv_w, conv_b, wax, b_a, b_x, lam, wpool, pscale, tb):
    nb = z.shape[0]
    zw = 2 * D_MIX
    rows = lambda i: (i, 0)
    const2 = lambda i: (0, 0)
    const3 = lambda i: (0, 0, 0)
    hist = lambda i: (0, i, 0)
    return pl.pallas_call(
        _sample_mix_kernel,
        grid=(nb // tb,),
        in_specs=[
            pl.BlockSpec((tb, zw), rows),
            pl.BlockSpec((tb, D_X), rows),
            pl.BlockSpec((CONV_W - 1, tb, D_RNN), hist),
            pl.BlockSpec((tb, D_RNN), rows),
            pl.BlockSpec((POOL_HIST, tb, D_POOL), hist),
            pl.BlockSpec((CONV_W, D_RNN), const2),
            pl.BlockSpec((1, D_RNN), const2),
            pl.BlockSpec((N_RNN_BLOCKS, RNN_BLOCK, 2 * RNN_BLOCK), const3),
            pl.BlockSpec((1, D_RNN), const2),
            pl.BlockSpec((1, D_RNN), const2),
            pl.BlockSpec((1, D_RNN), const2),
            pl.BlockSpec((len(POOL_WINDOWS), POOL_GROUP, POOL_GROUP), const3),
            pl.BlockSpec((1, D_POOL), const2),
        ],
        out_specs=[
            pl.BlockSpec((tb, D_MIX), rows),
            pl.BlockSpec((tb, D_RNN), rows),
            pl.BlockSpec((CONV_W - 1, tb, D_RNN), hist),
            pl.BlockSpec((POOL_HIST, tb, D_POOL), hist),
        ],
        out_shape=[
            jax.ShapeDtypeStruct((nb, D_MIX), BF16),
            jax.ShapeDtypeStruct((nb, D_RNN), F32),
            jax.ShapeDtypeStruct((CONV_W - 1, nb, D_RNN), F32),
            jax.ShapeDtypeStruct((POOL_HIST, nb, D_POOL), F32),
        ],
        compiler_params=pltpu.CompilerParams(
            dimension_semantics=("arbitrary",),
            vmem_limit_bytes=VMEM_LIMIT),
        name="sample_mix",
    )(z, attn, conv, h, pool, conv_w, conv_b, wax, b_a, b_x, lam, wpool, pscale)


def _branch_out_kernel(o_ref, gates_ref, x_ref, wb_ref, wo_ref, gpost_ref, y_ref):
    merged = None
    for j, (r0, r1) in enumerate(((0, D_RNN), (D_RNN, D_RNN + D_POOL), (D_RNN + D_POOL, D_MIX))):
        yj = jnp.dot(o_ref[:, r0:r1], wb_ref[r0:r1, :], preferred_element_type=F32)
        term = _sigmoid(gates_ref[:, j * D_MODEL:(j + 1) * D_MODEL]) * yj
        merged = term if merged is None else merged + term
    out = jnp.dot(merged.astype(BF16), wo_ref[...], preferred_element_type=F32)
    y_ref[...] = x_ref[...] + (out * gpost_ref[...]) * _rms_scale(out)


def _branch_out(o, z, x, wb, wo, g_post, tm):
    m = x.shape[0]
    gw = N_BRANCH * D_MODEL
    gblk = (2 * D_MIX) // gw
    resident = pl.Buffered(1)
    return pl.pallas_call(
        _branch_out_kernel,
        grid=(m // tm,),
        in_specs=[
            pl.BlockSpec((tm, D_MIX), lambda i: (i, 0)),
            pl.BlockSpec((tm, gw), lambda i: (i, gblk)),
            pl.BlockSpec((tm, D_MODEL), lambda i: (i, 0)),
            pl.BlockSpec((D_MIX, D_MODEL), lambda i: (0, 0), pipeline_mode=resident),
            pl.BlockSpec((D_MODEL, D_MODEL), lambda i: (0, 0), pipeline_mode=resident),
            pl.BlockSpec((1, D_MODEL), lambda i: (0, 0)),
        ],
        out_specs=pl.BlockSpec((tm, D_MODEL), lambda i: (i, 0)),
        out_shape=jax.ShapeDtypeStruct((m, D_MODEL), F32),
        compiler_params=pltpu.CompilerParams(
            dimension_semantics=("arbitrary",),
            vmem_limit_bytes=VMEM_LIMIT),
        name="branch_out",
    )(o, z, x, wb, wo, g_post)


WROWS = 1024
PER_BRANCH = D_RNN // WROWS
assert D_RNN == D_POOL == D_X and D_RNN % WROWS == 0 and D_MODEL % WROWS == 0
N_WB_BLOCKS = N_BRANCH * PER_BRANCH
N_WOUT_BLOCKS = D_MODEL // WROWS


def _branch_out_cast_kernel(o_ref, gates_ref, x_ref, wb_ref, wo_ref, gpost_ref,
                            y_ref, wbb_ref, wob_ref, merged_ref, out_ref):
    s = pl.program_id(0)

    @pl.when(s < N_WB_BLOCKS)
    def _():
        w = wb_ref[...].astype(BF16)
        wbb_ref[...] = w
        term = _sigmoid(gates_ref[...]) * jnp.dot(o_ref[...], w, preferred_element_type=F32)

        @pl.when(s == 0)
        def _():
            merged_ref[...] = term

        @pl.when(s > 0)
        def _():
            merged_ref[...] += term

    for kb in range(N_WOUT_BLOCKS):
        @pl.when(s == N_WB_BLOCKS + kb)
        def _(kb=kb):
            w = wo_ref[...].astype(BF16)
            wob_ref[...] = w
            part = jnp.dot(merged_ref[:, kb * WROWS:(kb + 1) * WROWS].astype(BF16), w,
                           preferred_element_type=F32)
            if kb == 0:
                out_ref[...] = part
            else:
                out_ref[...] += part

    @pl.when(s == N_WB_BLOCKS + N_WOUT_BLOCKS - 1)
    def _():
        out = out_ref[...]
        y_ref[...] = x_ref[...] + out * _rms_scale(out) * gpost_ref[...]


def _branch_out_cast(o, z, x, wb, wo, g_post):
    m = x.shape[0]
    gblk0 = (2 * D_MIX) // D_MODEL
    wb_blk = lambda s: jnp.minimum(s, N_WB_BLOCKS - 1)
    wo_blk = lambda s: jnp.maximum(s - N_WB_BLOCKS, 0)
    return pl.pallas_call(
        _branch_out_cast_kernel,
        grid=(N_WB_BLOCKS + N_WOUT_BLOCKS,),
        in_specs=[
            pl.BlockSpec((m, WROWS), lambda s: (0, wb_blk(s))),
            pl.BlockSpec((m, D_MODEL), lambda s: (0, gblk0 + wb_blk(s) // PER_BRANCH)),
            pl.BlockSpec((m, D_MODEL), lambda s: (0, 0)),
            pl.BlockSpec((WROWS, D_MODEL), lambda s: (wb_blk(s), 0)),
            pl.BlockSpec((WROWS, D_MODEL), lambda s: (wo_blk(s), 0)),
            pl.BlockSpec((1, D_MODEL), lambda s: (0, 0)),
        ],
        out_specs=[
            pl.BlockSpec((m, D_MODEL), lambda s: (0, 0)),
            pl.BlockSpec((WROWS, D_MODEL), lambda s: (wb_blk(s), 0)),
            pl.BlockSpec((WROWS, D_MODEL), lambda s: (wo_blk(s), 0)),
        ],
        out_shape=[
            jax.ShapeDtypeStruct((m, D_MODEL), F32),
            jax.ShapeDtypeStruct(wb.shape, BF16),
            jax.ShapeDtypeStruct(wo.shape, BF16),
        ],
        scratch_shapes=[pltpu.VMEM((m, D_MODEL), F32), pltpu.VMEM((m, D_MODEL), F32)],
        compiler_params=pltpu.CompilerParams(
            dimension_semantics=("arbitrary",),
            vmem_limit_bytes=BIG_VMEM_LIMIT),
        name="branch_out_cast",
    )(o, z, x, wb, wo, g_post)


def _mem_kv_kernel(x_ref, g_ref, w_ref, k_ref, v_ref, u_ref):
    j = pl.program_id(1)

    @pl.when(j == 0)
    def _():
        x = x_ref[...]
        u_ref[...] = (x * _rms_scale(x) * g_ref[...]).astype(BF16)

    res = jnp.dot(u_ref[...], w_ref[...].astype(BF16), preferred_element_type=F32)

    @pl.when(j == 0)
    def _():
        k_ref[...] = res

    @pl.when(j == 1)
    def _():
        v_ref[...] = res


def _mem_kv(x, g, w, tm):
    m, k = x.shape
    assert w.shape[1] == 2 * D_X
    half = pl.BlockSpec((tm, D_X), lambda i, j: (i, 0))
    return pl.pallas_call(
        _mem_kv_kernel,
        grid=(m // tm, 2),
        in_specs=[
            pl.BlockSpec((tm, k), lambda i, j: (i, 0)),
            pl.BlockSpec((1, k), lambda i, j: (0, 0)),
            pl.BlockSpec((k, D_X), lambda i, j: (0, j)),
        ],
        out_specs=[half, half],
        out_shape=[jax.ShapeDtypeStruct((m, D_X), F32)] * 2,
        scratch_shapes=[pltpu.VMEM((tm, k), BF16)],
        compiler_params=pltpu.CompilerParams(
            dimension_semantics=("arbitrary", "arbitrary"),
            vmem_limit_bytes=BIG_VMEM_LIMIT),
        name="mem_kv",
    )(x, g, w)


def kernel(x_prompt, x_sample, mem_prompt, state_rglru_h, state_conv, state_pool, cache_mem_k, cache_mem_v, g_pre, w_in, conv_w, conv_b, w_rg_a, b_rg_a, w_rg_x, b_rg_x, lru_lambda, w_pool, pool_scale, g_mem, w_kv, w_branch, w_out, g_post):
    batch, seq, _ = x_prompt.shape
    nb = x_sample.shape[0]
    depth = g_pre.shape[0]
    assert depth == 1 and x_sample.shape[1] == 1

    l = 0
    row = lambda v: v.reshape(1, -1)
    wax = jnp.concatenate([w_rg_a[l], w_rg_x[l]], axis=-1).astype(BF16)
    wpool = w_pool[l].astype(BF16)
    mix_params = (conv_w[l], row(conv_b[l]), wax, row(b_rg_a[l]), row(b_rg_x[l]),
                  row(lru_lambda[l]), wpool, row(pool_scale[l]))

    xp2 = x_prompt.reshape(batch * seq, D_MODEL)
    xs2 = x_sample.reshape(nb, D_MODEL)
    mem2 = mem_prompt.reshape(batch * N_MEM, D_MODEL)

    z_s, w_in_b = _sample_proj(xs2, row(g_pre[l]), w_in[l], tn=SAMPLE_PROJ_TN)
    qoff = 2 * D_RNN + 2 * D_POOL
    q_s = z_s[:, qoff:qoff + D_X].reshape(nb // ATTN_BB, ATTN_BB, D_X)

    mem_k, mem_v = _mem_kv(mem2, row(g_mem[l]), w_kv[l], tm=KV_TM)
    mem_k = mem_k.reshape(batch, N_MEM, D_X)
    mem_v = mem_v.reshape(batch, N_MEM, D_X)

    perm = _chunk_interleave()
    z_p = _prompt_proj(xp2, row(g_pre[l]), w_in_b, perm, tm=PROJ_TM, tn=PROJ_TN)
    o_p, h_p, c_p, p_p, attn_s = _prompt_mix(
        z_p, mem_k, mem_v, *mix_params, perm.T, q_s, _cache_rows(cache_mem_k[l]),
        _cache_rows(cache_mem_v[l]), batch=batch, seq=seq, tm=MIX_TM)
    attn_s = attn_s.reshape(nb, D_X)

    o_s, h_s, c_s, p_s = _sample_mix(
        z_s, attn_s, state_conv[l].transpose(1, 0, 2), state_rglru_h[l],
        state_pool[l].transpose(1, 0, 2), *mix_params, tb=SAMPLE_MIX_TB)
    y_s, w_br_b, w_out_b = _branch_out_cast(o_s, z_s, xs2, w_branch[l], w_out[l], row(g_post[l]))

    y_p = _branch_out(o_p, z_p, xp2, w_br_b, w_out_b, row(g_post[l]), tm=BRANCH_TM)

    return (
        y_p.reshape(batch, seq, D_MODEL),
        y_s.reshape(nb, 1, D_MODEL),
        h_p.reshape(1, batch, D_RNN),
        c_p.reshape(1, batch, CONV_W - 1, D_RNN),
        p_p.reshape(1, batch, POOL_HIST, D_POOL),
        mem_k.reshape(1, batch, N_MEM, N_XHEADS, XHEAD_DIM),
        mem_v.reshape(1, batch, N_MEM, N_XHEADS, XHEAD_DIM),
        h_s.reshape(1, nb, D_RNN),
        c_s.transpose(1, 0, 2)[None],
        p_s.transpose(1, 0, 2)[None],
    )
```

```python
import functools

import jax
import jax.numpy as jnp
from jax import lax
from jax.experimental import pallas as pl
from jax.experimental.pallas import tpu as pltpu

D_MODEL = 2048
PAST_LEN = 16384
D_RNN = 1024
N_RNN_BLOCKS = 8
RNN_BLOCK = D_RNN // N_RNN_BLOCKS
CONV_W = 4
LRU_C = 8.0
D_POOL = 1024
POOL_WINDOWS = (2, 4, 8, 16)
POOL_GROUP = D_POOL // len(POOL_WINDOWS)
POOL_HIST = max(POOL_WINDOWS) - 1
N_MEM = 256
N_XHEADS = 4
XHEAD_DIM = 256
D_X = N_XHEADS * XHEAD_DIM
N_BRANCH = 3
D_MIX = D_RNN + D_POOL + D_X
D_IN = 2 * D_MIX + N_BRANCH * D_MODEL
EPS = 1e-6

SUBLANES = 8
LANES = 128
VMEM_LIMIT = 56 * 1024 * 1024
BIG_VMEM_LIMIT = 60 * 1024 * 1024
MIX_TM = 256
BRANCH_TM = 256
PROJ_TM, PROJ_TN = 1024, 2048
SAMPLE_PROJ_TN = 2048
ATTN_BB = 4
SAMPLE_MIX_TB = 64
KV_TM = 1024

BF16 = jnp.bfloat16
F32 = jnp.float32

NEG_LOG2_E = -1.4426950408889634


def _sigmoid(x):
    return 1.0 / (1.0 + jnp.exp2(x * NEG_LOG2_E))


def _silu(x):
    return x * _sigmoid(x)


def _softplus(x):
    return jnp.maximum(x, 0.0) + jnp.log1p(jnp.exp(-jnp.abs(x)))


def _rms_scale(x):
    return lax.rsqrt(jnp.mean(x * x, axis=-1, keepdims=True) + EPS)


def _chunk_interleave():
    nrow = MIX_TM // SUBLANES
    p = jnp.arange(MIX_TM)
    token = (p % SUBLANES) * nrow + p // SUBLANES
    return (token[:, None] == jnp.arange(MIX_TM)[None, :]).astype(BF16)


def _sample_proj_kernel(x_ref, g_ref, w_ref, o_ref, wb_ref, u_ref):
    @pl.when(pl.program_id(0) == 0)
    def _():
        x = x_ref[...]
        u_ref[...] = (x * _rms_scale(x) * g_ref[...]).astype(BF16)

    w = w_ref[...].astype(BF16)
    wb_ref[...] = w
    o_ref[...] = jnp.dot(u_ref[...], w, preferred_element_type=F32)


def _sample_proj(x, g, w, tn):
    m, k = x.shape
    n = w.shape[1]
    return pl.pallas_call(
        _sample_proj_kernel,
        grid=(n // tn,),
        in_specs=[
            pl.BlockSpec((m, k), lambda j: (0, 0)),
            pl.BlockSpec((1, k), lambda j: (0, 0)),
            pl.BlockSpec((k, tn), lambda j: (0, j)),
        ],
        out_specs=[
            pl.BlockSpec((m, tn), lambda j: (0, j)),
            pl.BlockSpec((k, tn), lambda j: (0, j)),
        ],
        out_shape=[
            jax.ShapeDtypeStruct((m, n), F32),
            jax.ShapeDtypeStruct((k, n), BF16),
        ],
        scratch_shapes=[pltpu.VMEM((m, k), BF16)],
        compiler_params=pltpu.CompilerParams(
            dimension_semantics=("arbitrary",),
            vmem_limit_bytes=BIG_VMEM_LIMIT),
        name="sample_proj",
    )(x, g, w)


def _decay_rate(lam):
    return _softplus(-lam) * (LRU_C * NEG_LOG2_E)


def _rglru_block(xc, wax, ba, bx, rate):
    ri = jnp.dot(xc.astype(BF16), wax, preferred_element_type=F32)
    r = _sigmoid(ri[:, :RNN_BLOCK] + ba)
    i = _sigmoid(ri[:, RNN_BLOCK:] + bx)
    a = jnp.exp2(r * rate)
    one_m = 1.0 - a * a
    mult = jnp.where(one_m > 0.0, one_m * lax.rsqrt(one_m), 0.0)
    return a, mult * i * xc


def _prompt_mix_kernel(z_ref, k_ref, v_ref, convw_ref, convb_ref, wax_ref, ba_ref, bx_ref,
                       lam_ref, wpool_ref, pscale_ref, unperm_ref, sq_ref, sk_ref, sv_ref,
                       o_ref, newh_ref, newconv_ref, newpool_ref, sattn_ref,
                       conv_carry, pool_carry, h_carry, kb_ref, vb_ref, ac_scr, hl_scr, op_scr,
                       *, tm):
    l = pl.program_id(1)
    last = pl.num_programs(1) - 1
    nrow = tm // SUBLANES

    @pl.when(l == 0)
    def _():
        conv_carry[...] = jnp.zeros(conv_carry.shape, F32)
        pool_carry[...] = jnp.zeros(pool_carry.shape, F32)
        h_carry[...] = jnp.zeros(h_carry.shape, F32)
        kb_ref[...] = k_ref[0].astype(BF16)
        vb_ref[...] = v_ref[0].astype(BF16)

    side_scores = _sample_attn_scores(sq_ref.at[0], sk_ref, ATTN_BB)

    chunk_id = lax.broadcasted_iota(jnp.int32, (SUBLANES, LANES), 0)
    first_chunk = chunk_id == 0

    def load_groups(col, width=LANES):
        return [z_ref[r * SUBLANES:(r + 1) * SUBLANES, col:col + width] for r in range(nrow)]

    def put(col, width, val):
        op_scr[:, col:col + width] = val.astype(BF16)

    def store_groups(col, rows, width=LANES):
        put(col, width, jnp.concatenate(rows, axis=0))

    def history(tail_group, carry_ref, j, c0):
        tail = pltpu.roll(tail_group, 1, 0)
        prev = jnp.where(first_chunk, carry_ref[j - 1, :, c0:c0 + LANES], tail)
        carry_ref[j - 1, :, c0:c0 + LANES] = tail
        return prev

    rate = _decay_rate(lam_ref[...])
    for n in range(N_RNN_BLOCKS):
        c0, c1 = n * RNN_BLOCK, (n + 1) * RNN_BLOCK
        xs = load_groups(c0)
        ext = [history(xs[nrow - j], conv_carry, j, c0) for j in range(CONV_W - 1, 0, -1)] + xs
        cw = [jnp.broadcast_to(convw_ref[k:k + 1, c0:c1], (SUBLANES, LANES)) for k in range(CONV_W)]
        cb = jnp.broadcast_to(convb_ref[:, c0:c1], (SUBLANES, LANES))
        xc = []
        for r in range(nrow):
            acc = cb + cw[0] * ext[r]
            for k in range(1, CONV_W):
                acc = acc + cw[k] * ext[r + k]
            xc.append(acc)
        a, b = _rglru_block(jnp.concatenate(xc, axis=0), wax_ref[n], ba_ref[:, c0:c1],
                            bx_ref[:, c0:c1], rate[:, c0:c1])
        ac_scr[:, c0:c1] = a
        hl_scr[:, c0:c1] = b

    side_probs = _sample_attn_probs(side_scores)

    acc_a = ac_scr[0:SUBLANES, :]
    acc_h = hl_scr[0:SUBLANES, :]
    for r in range(1, nrow):
        rows = slice(r * SUBLANES, (r + 1) * SUBLANES)
        ar = ac_scr[rows, :]
        acc_h = ar * acc_h + hl_scr[rows, :]
        acc_a = ar * acc_a
        ac_scr[rows, :] = acc_a
        hl_scr[rows, :] = acc_h
    h_in = h_carry[...]
    entering = []
    for c in range(SUBLANES):
        entering.append(h_in)
        h_in = acc_a[c:c + 1] * h_in + acc_h[c:c + 1]
    h_carry[...] = h_in
    h_enter = jnp.concatenate(entering, axis=0)
    for n in range(N_RNN_BLOCKS):
        c0, c1 = n * RNN_BLOCK, (n + 1) * RNN_BLOCK
        gr = load_groups(D_RNN + c0)
        store_groups(c0, [(hl_scr[r * SUBLANES:(r + 1) * SUBLANES, c0:c1]
                           + ac_scr[r * SUBLANES:(r + 1) * SUBLANES, c0:c1] * h_enter[:, c0:c1])
                          * _silu(gr[r]) for r in range(nrow)])

    _sample_attn_values(side_probs, sv_ref, sattn_ref.at[0])

    pcol = 2 * D_RNN
    blocks = [(w, c0) for g, w in enumerate(POOL_WINDOWS)
              for c0 in range(g * POOL_GROUP, (g + 1) * POOL_GROUP, LANES)]

    def group(c0, r):
        return z_ref[r * SUBLANES:(r + 1) * SUBLANES, pcol + c0:pcol + c0 + LANES]

    def mean_minus_token(tot, w, c0, r):
        if r < w - 1:
            pos1 = l * tm + chunk_id * nrow + (r + 1)
            mean = tot / jnp.minimum(pos1, w).astype(F32)
        else:
            mean = tot * (1.0 / w)
        return mean - group(c0, r)

    hist, tot = {}, {}
    for w, c0 in blocks:
        hist[c0] = [history(group(c0, nrow - j), pool_carry, j, c0) for j in range(1, w)]
        t = group(c0, 0)
        for h in hist[c0]:
            t = t + h
        tot[c0] = t
        hl_scr[0:SUBLANES, c0:c0 + LANES] = mean_minus_token(t, w, c0, 0)
    for r in range(1, nrow):
        for w, c0 in blocks:
            leaving = group(c0, r - w) if r >= w else hist[c0][w - r - 1]
            tot[c0] = tot[c0] + (group(c0, r) - leaving)
            hl_scr[r * SUBLANES:(r + 1) * SUBLANES, c0:c0 + LANES] = mean_minus_token(
                tot[c0], w, c0, r)
    for g, w in enumerate(POOL_WINDOWS):
        c0, c1 = g * POOL_GROUP, (g + 1) * POOL_GROUP
        og = jnp.dot(hl_scr[:, c0:c1].astype(BF16), wpool_ref[g], preferred_element_type=F32)
        gp = z_ref[:, pcol + D_POOL + c0:pcol + D_POOL + c1]
        put(D_RNN + c0, POOL_GROUP, og * pscale_ref[:, c0:c1] * _silu(gp))

    qoff = 2 * D_RNN + 2 * D_POOL
    for hd in range(N_XHEADS):
        c0, c1 = hd * XHEAD_DIM, (hd + 1) * XHEAD_DIM
        q = z_ref[:, qoff + c0:qoff + c1].astype(BF16)
        s = lax.dot_general(q, kb_ref[:, c0:c1], (((1,), (1,)), ((), ())),
                            preferred_element_type=F32) * (XHEAD_DIM ** -0.5)
        p = jnp.exp(s - jnp.max(s, axis=-1, keepdims=True))
        p = p / jnp.sum(p, axis=-1, keepdims=True)
        ox = jnp.dot(p.astype(BF16), vb_ref[:, c0:c1], preferred_element_type=F32)
        gx = z_ref[:, qoff + D_X + c0:qoff + D_X + c1]
        put(D_RNN + D_POOL + c0, XHEAD_DIM, ox * _silu(gx))

    o_ref[...] = jnp.dot(unperm_ref[...], op_scr[...], preferred_element_type=F32).astype(BF16)

    @pl.when(l == last)
    def _():
        newh_ref[0] = h_carry[...]
        tail_row = lambda j: (nrow - j) * SUBLANES + SUBLANES - 1
        for j in range(1, CONV_W):
            newconv_ref[0, CONV_W - 1 - j:CONV_W - j, :] = z_ref[tail_row(j):tail_row(j) + 1, 0:D_RNN]
        for j in range(1, POOL_HIST + 1):
            newpool_ref[0, POOL_HIST - j:POOL_HIST - j + 1, :] = (
                z_ref[tail_row(j):tail_row(j) + 1, pcol:pcol + D_POOL])


def _prompt_mix(z, mem_k, mem_v, conv_w, conv_b, wax, b_a, b_x, lam, wpool, pscale, unperm,
                sample_q, cache_k, cache_v, batch, seq, tm):
    nl = seq // tm
    assert sample_q.shape[0] == batch * nl
    side = lambda b, l: (b * nl + l, 0, 0)
    zw = 2 * D_MIX
    const2 = lambda b, l: (0, 0)
    const3 = lambda b, l: (0, 0, 0)
    kern = functools.partial(_prompt_mix_kernel, tm=tm)
    return pl.pallas_call(
        kern,
        grid=(batch, nl),
        in_specs=[
            pl.BlockSpec((tm, zw), lambda b, l: (b * nl + l, 0)),
            pl.BlockSpec((1, N_MEM, D_X), lambda b, l: (b, 0, 0)),
            pl.BlockSpec((1, N_MEM, D_X), lambda b, l: (b, 0, 0)),
            pl.BlockSpec((CONV_W, D_RNN), const2),
            pl.BlockSpec((1, D_RNN), const2),
            pl.BlockSpec((N_RNN_BLOCKS, RNN_BLOCK, 2 * RNN_BLOCK), const3),
            pl.BlockSpec((1, D_RNN), const2),
            pl.BlockSpec((1, D_RNN), const2),
            pl.BlockSpec((1, D_RNN), const2),
            pl.BlockSpec((len(POOL_WINDOWS), POOL_GROUP, POOL_GROUP), const3),
            pl.BlockSpec((1, D_POOL), const2),
            pl.BlockSpec((tm, tm), const2),
            pl.BlockSpec((1, ATTN_BB, D_X), side),
            pl.BlockSpec((ATTN_BB, N_MEM * SUBLANES, LANES), side),
            pl.BlockSpec((ATTN_BB, N_MEM * SUBLANES, LANES), side),
        ],
        out_specs=[
            pl.BlockSpec((tm, D_MIX), lambda b, l: (b * nl + l, 0)),
            pl.BlockSpec((1, 1, D_RNN), lambda b, l: (b, 0, 0)),
            pl.BlockSpec((1, CONV_W - 1, D_RNN), lambda b, l: (b, 0, 0)),
            pl.BlockSpec((1, POOL_HIST, D_POOL), lambda b, l: (b, 0, 0)),
            pl.BlockSpec((1, ATTN_BB, D_X), side),
        ],
        out_shape=[
            jax.ShapeDtypeStruct((batch * seq, D_MIX), BF16),
            jax.ShapeDtypeStruct((batch, 1, D_RNN), F32),
            jax.ShapeDtypeStruct((batch, CONV_W - 1, D_RNN), F32),
            jax.ShapeDtypeStruct((batch, POOL_HIST, D_POOL), F32),
            jax.ShapeDtypeStruct(sample_q.shape, F32),
        ],
        scratch_shapes=[
            pltpu.VMEM((CONV_W - 1, SUBLANES, D_RNN), F32),
            pltpu.VMEM((POOL_HIST, SUBLANES, D_POOL), F32),
            pltpu.VMEM((1, D_RNN), F32),
            pltpu.VMEM((N_MEM, D_X), BF16),
            pltpu.VMEM((N_MEM, D_X), BF16),
            pltpu.VMEM((tm, D_RNN), F32),
            pltpu.VMEM((tm, D_RNN), F32),
            pltpu.VMEM((tm, D_MIX), BF16),
        ],
        compiler_params=pltpu.CompilerParams(
            dimension_semantics=("arbitrary", "arbitrary"),
            vmem_limit_bytes=VMEM_LIMIT),
        name="prompt_mix",
    )(z, mem_k, mem_v, conv_w, conv_b, wax, b_a, b_x, lam, wpool, pscale, unperm,
      sample_q, cache_k, cache_v)


def _cache_rows(c):
    nb = c.shape[0]
    c = c.reshape(nb, N_MEM, N_XHEADS, XHEAD_DIM // LANES, LANES)
    return c.transpose(0, 1, 3, 2, 4).reshape(nb, N_MEM * SUBLANES, LANES)


def _sample_attn_scores(q_ref, k_ref, bb):
    halves = XHEAD_DIM // LANES
    assert halves * N_XHEADS == SUBLANES
    scores = []
    for j in range(bb):
        qn = jnp.concatenate(
            [q_ref[j:j + 1, (h * halves + t) * LANES:(h * halves + t + 1) * LANES]
             for t in range(halves) for h in range(N_XHEADS)], axis=0)
        scores.append(lax.dot_general(qn.astype(BF16), k_ref[j].astype(BF16),
                                      (((1,), (1,)), ((), ())), preferred_element_type=F32)
                      * (XHEAD_DIM ** -0.5))
    return scores


def _sample_attn_probs(scores):
    r = lax.broadcasted_iota(jnp.int32, (SUBLANES, LANES), 0)
    c = lax.broadcasted_iota(jnp.int32, (SUBLANES, LANES), 1)
    diag = (c % SUBLANES) == r
    first_half = r < N_XHEADS
    nchunk = N_MEM * SUBLANES // LANES
    probs = []
    for s in scores:
        chunks = []
        for ci in range(nchunk):
            sm = jnp.where(diag, s[:, ci * LANES:(ci + 1) * LANES], 0.0)
            other = pltpu.roll(sm, N_XHEADS, 0)
            other = jnp.where(first_half, pltpu.roll(other, LANES - N_XHEADS, 1),
                              pltpu.roll(other, N_XHEADS, 1))
            chunks.append(jnp.where(diag, sm + other, -jnp.inf))
        t_full = jnp.concatenate(chunks, axis=1)
        e = jnp.exp(t_full - jnp.max(t_full, axis=1, keepdims=True))
        probs.append((e / jnp.sum(e, axis=1, keepdims=True)).astype(BF16))
    return probs


def _sample_attn_values(probs, v_ref, o_ref):
    halves = XHEAD_DIM // LANES
    for j, p in enumerate(probs):
        o = jnp.dot(p, v_ref[j].astype(BF16), preferred_element_type=F32)
        for t in range(halves):
            for h in range(N_XHEADS):
                col = (h * halves + t) * LANES
                o_ref[j:j + 1, col:col + LANES] = o[t * N_XHEADS + h:t * N_XHEADS + h + 1, :]


def _prompt_proj_kernel(x_ref, g_ref, w_ref, perm_ref, o_ref, u_ref, up_ref, *, mix_steps):
    j = pl.program_id(1)

    @pl.when(j == 0)
    def _():
        x = x_ref[...]
        u = (x * _rms_scale(x) * g_ref[...]).astype(BF16)
        u_ref[...] = u
        for r0 in range(0, u.shape[0], MIX_TM):
            up_ref[r0:r0 + MIX_TM, :] = jnp.dot(
                perm_ref[...], u[r0:r0 + MIX_TM], preferred_element_type=F32).astype(BF16)

    @pl.when(j < mix_steps)
    def _():
        o_ref[...] = jnp.dot(up_ref[...], w_ref[...], preferred_element_type=F32)

    @pl.when(j >= mix_steps)
    def _():
        o_ref[...] = jnp.dot(u_ref[...], w_ref[...], preferred_element_type=F32)


def _prompt_proj(x, g, w, perm, tm, tn):
    m, k = x.shape
    n = w.shape[1]
    assert (2 * D_MIX) % tn == 0
    return pl.pallas_call(
        functools.partial(_prompt_proj_kernel, mix_steps=2 * D_MIX // tn),
        grid=(m // tm, n // tn),
        in_specs=[
            pl.BlockSpec((tm, k), lambda i, j: (i, 0)),
            pl.BlockSpec((1, k), lambda i, j: (0, 0)),
            pl.BlockSpec((k, tn), lambda i, j: (0, j)),
            pl.BlockSpec(perm.shape, lambda i, j: (0, 0)),
        ],
        out_specs=pl.BlockSpec((tm, tn), lambda i, j: (i, j)),
        out_shape=jax.ShapeDtypeStruct((m, n), F32),
        scratch_shapes=[pltpu.VMEM((tm, k), BF16), pltpu.VMEM((tm, k), BF16)],
        compiler_params=pltpu.CompilerParams(
            dimension_semantics=("arbitrary", "arbitrary"),
            vmem_limit_bytes=BIG_VMEM_LIMIT),
        name="prompt_proj",
    )(x, g, w, perm)


def _sample_mix_kernel(z_ref, attn_ref, conv_ref, h_ref, pool_ref,
                       convw_ref, convb_ref, wax_ref, ba_ref, bx_ref, lam_ref, wpool_ref,
                       pscale_ref, o_ref, newh_ref, newconv_ref, newpool_ref):
    xr = z_ref[:, 0:D_RNN]
    xc = convb_ref[...] + convw_ref[CONV_W - 1:CONV_W, :] * xr
    for k in range(CONV_W - 1):
        xc = xc + convw_ref[k:k + 1, :] * conv_ref[k]
    for k in range(CONV_W - 2):
        newconv_ref[k] = conv_ref[k + 1]
    newconv_ref[CONV_W - 2] = xr

    rate = _decay_rate(lam_ref[...])
    for n in range(N_RNN_BLOCKS):
        c0, c1 = n * RNN_BLOCK, (n + 1) * RNN_BLOCK
        a, b = _rglru_block(xc[:, c0:c1], wax_ref[n], ba_ref[:, c0:c1], bx_ref[:, c0:c1],
                            rate[:, c0:c1])
        h = a * h_ref[:, c0:c1] + b
        newh_ref[:, c0:c1] = h
        o_ref[:, c0:c1] = (h * _silu(z_ref[:, D_RNN + c0:D_RNN + c1])).astype(BF16)

    xp = z_ref[:, 2 * D_RNN:2 * D_RNN + D_POOL]
    for k in range(POOL_HIST - 1):
        newpool_ref[k] = pool_ref[k + 1]
    newpool_ref[POOL_HIST - 1] = xp
    for g, w in enumerate(POOL_WINDOWS):
        c0, c1 = g * POOL_GROUP, (g + 1) * POOL_GROUP
        xg = xp[:, c0:c1]
        tot = xg
        for j in range(1, w):
            tot = tot + pool_ref[POOL_HIST - j, :, c0:c1]
        cnt = float(min(PAST_LEN + 1, w))
        d = tot / cnt - xg
        og = jnp.dot(d.astype(BF16), wpool_ref[g], preferred_element_type=F32)
        gp = z_ref[:, 2 * D_RNN + D_POOL + c0:2 * D_RNN + D_POOL + c1]
        o_ref[:, D_RNN + c0:D_RNN + c1] = (og * pscale_ref[:, c0:c1] * _silu(gp)).astype(BF16)

    gx = z_ref[:, 2 * D_RNN + 2 * D_POOL + D_X:2 * D_MIX]
    o_ref[:, D_RNN + D_POOL:] = (attn_ref[...] * _silu(gx)).astype(BF16)


def _sample_mix(z, attn, conv, h, pool, conv_w, conv_b, wax, b_a, b_x, lam, wpool, pscale, tb):
    nb = z.shape[0]
    zw = 2 * D_MIX
    rows = lambda i: (i, 0)
    const2 = lambda i: (0, 0)
    const3 = lambda i: (0, 0, 0)
    hist = lambda i: (0, i, 0)
    return pl.pallas_call(
        _sample_mix_kernel,
        grid=(nb // tb,),
        in_specs=[
            pl.BlockSpec((tb, zw), rows),
            pl.BlockSpec((tb, D_X), rows),
            pl.BlockSpec((CONV_W - 1, tb, D_RNN), hist),
            pl.BlockSpec((tb, D_RNN), rows),
            pl.BlockSpec((POOL_HIST, tb, D_POOL), hist),
            pl.BlockSpec((CONV_W, D_RNN), const2),
            pl.BlockSpec((1, D_RNN), const2),
            pl.BlockSpec((N_RNN_BLOCKS, RNN_BLOCK, 2 * RNN_BLOCK), const3),
            pl.BlockSpec((1, D_RNN), const2),
            pl.BlockSpec((1, D_RNN), const2),
            pl.BlockSpec((1, D_RNN), const2),
            pl.BlockSpec((len(POOL_WINDOWS), POOL_GROUP, POOL_GROUP), const3),
            pl.BlockSpec((1, D_POOL), const2),
        ],
        out_specs=[
            pl.BlockSpec((tb, D_MIX), rows),
            pl.BlockSpec((tb, D_RNN), rows),
            pl.BlockSpec((CONV_W - 1, tb, D_RNN), hist),
            pl.BlockSpec((POOL_HIST, tb, D_POOL), hist),
        ],
        out_shape=[
            jax.ShapeDtypeStruct((nb, D_MIX), BF16),
            jax.ShapeDtypeStruct((nb, D_RNN), F32),
            jax.ShapeDtypeStruct((CONV_W - 1, nb, D_RNN), F32),
            jax.ShapeDtypeStruct((POOL_HIST, nb, D_POOL), F32),
        ],
        compiler_params=pltpu.CompilerParams(
            dimension_semantics=("arbitrary",),
            vmem_limit_bytes=VMEM_LIMIT),
        name="sample_mix",
    )(z, attn, conv, h, pool, conv_w, conv_b, wax, b_a, b_x, lam, wpool, pscale)


def _branch_out_kernel(o_ref, gates_ref, x_ref, wb_ref, wo_ref, gpost_ref, y_ref):
    merged = None
    for j, (r0, r1) in enumerate(((0, D_RNN), (D_RNN, D_RNN + D_POOL), (D_RNN + D_POOL, D_MIX))):
        yj = jnp.dot(o_ref[:, r0:r1], wb_ref[r0:r1, :], preferred_element_type=F32)
        term = _sigmoid(gates_ref[:, j * D_MODEL:(j + 1) * D_MODEL]) * yj
        merged = term if merged is None else merged + term
    out = jnp.dot(merged.astype(BF16), wo_ref[...], preferred_element_type=F32)
    y_ref[...] = x_ref[...] + (out * gpost_ref[...]) * _rms_scale(out)


def _branch_out(o, z, x, wb, wo, g_post, tm):
    m = x.shape[0]
    gw = N_BRANCH * D_MODEL
    gblk = (2 * D_MIX) // gw
    resident = pl.Buffered(1)
    return pl.pallas_call(
        _branch_out_kernel,
        grid=(m // tm,),
        in_specs=[
            pl.BlockSpec((tm, D_MIX), lambda i: (i, 0)),
            pl.BlockSpec((tm, gw), lambda i: (i, gblk)),
            pl.BlockSpec((tm, D_MODEL), lambda i: (i, 0)),
            pl.BlockSpec((D_MIX, D_MODEL), lambda i: (0, 0), pipeline_mode=resident),
            pl.BlockSpec((D_MODEL, D_MODEL), lambda i: (0, 0), pipeline_mode=resident),
            pl.BlockSpec((1, D_MODEL), lambda i: (0, 0)),
        ],
        out_specs=pl.BlockSpec((tm, D_MODEL), lambda i: (i, 0)),
        out_shape=jax.ShapeDtypeStruct((m, D_MODEL), F32),
        compiler_params=pltpu.CompilerParams(
            dimension_semantics=("arbitrary",),
            vmem_limit_bytes=VMEM_LIMIT),
        name="branch_out",
    )(o, z, x, wb, wo, g_post)


WROWS = 1024
PER_BRANCH = D_RNN // WROWS
assert D_RNN == D_POOL == D_X and D_RNN % WROWS == 0 and D_MODEL % WROWS == 0
N_WB_BLOCKS = N_BRANCH * PER_BRANCH
N_WOUT_BLOCKS = D_MODEL // WROWS


def _branch_out_cast_kernel(o_ref, gates_ref, x_ref, wb_ref, wo_ref, gpost_ref,
                            y_ref, wbb_ref, wob_ref, merged_ref, out_ref):
    s = pl.program_id(0)

    @pl.when(s < N_WB_BLOCKS)
    def _():
        w = wb_ref[...].astype(BF16)
        wbb_ref[...] = w
        term = _sigmoid(gates_ref[...]) * jnp.dot(o_ref[...], w, preferred_element_type=F32)

        @pl.when(s == 0)
        def _():
            merged_ref[...] = term

        @pl.when(s > 0)
        def _():
            merged_ref[...] += term

    for kb in range(N_WOUT_BLOCKS):
        @pl.when(s == N_WB_BLOCKS + kb)
        def _(kb=kb):
            w = wo_ref[...].astype(BF16)
            wob_ref[...] = w
            part = jnp.dot(merged_ref[:, kb * WROWS:(kb + 1) * WROWS].astype(BF16), w,
                           preferred_element_type=F32)
            if kb == 0:
                out_ref[...] = part
            else:
                out_ref[...] += part

    @pl.when(s == N_WB_BLOCKS + N_WOUT_BLOCKS - 1)
    def _():
        out = out_ref[...]
        y_ref[...] = x_ref[...] + out * _rms_scale(out) * gpost_ref[...]


def _branch_out_cast(o, z, x, wb, wo, g_post):
    m = x.shape[0]
    gblk0 = (2 * D_MIX) // D_MODEL
    wb_blk = lambda s: jnp.minimum(s, N_WB_BLOCKS - 1)
    wo_blk = lambda s: jnp.maximum(s - N_WB_BLOCKS, 0)
    return pl.pallas_call(
        _branch_out_cast_kernel,
        grid=(N_WB_BLOCKS + N_WOUT_BLOCKS,),
        in_specs=[
            pl.BlockSpec((m, WROWS), lambda s: (0, wb_blk(s))),
            pl.BlockSpec((m, D_MODEL), lambda s: (0, gblk0 + wb_blk(s) // PER_BRANCH)),
            pl.BlockSpec((m, D_MODEL), lambda s: (0, 0)),
            pl.BlockSpec((WROWS, D_MODEL), lambda s: (wb_blk(s), 0)),
            pl.BlockSpec((WROWS, D_MODEL), lambda s: (wo_blk(s), 0)),
            pl.BlockSpec((1, D_MODEL), lambda s: (0, 0)),
        ],
        out_specs=[
            pl.BlockSpec((m, D_MODEL), lambda s: (0, 0)),
            pl.BlockSpec((WROWS, D_MODEL), lambda s: (wb_blk(s), 0)),
            pl.BlockSpec((WROWS, D_MODEL), lambda s: (wo_blk(s), 0)),
        ],
        out_shape=[
            jax.ShapeDtypeStruct((m, D_MODEL), F32),
            jax.ShapeDtypeStruct(wb.shape, BF16),
            jax.ShapeDtypeStruct(wo.shape, BF16),
        ],
        scratch_shapes=[pltpu.VMEM((m, D_MODEL), F32), pltpu.VMEM((m, D_MODEL), F32)],
        compiler_params=pltpu.CompilerParams(
            dimension_semantics=("arbitrary",),
            vmem_limit_bytes=BIG_VMEM_LIMIT),
        name="branch_out_cast",
    )(o, z, x, wb, wo, g_post)


def _mem_kv_kernel(x_ref, g_ref, w_ref, k_ref, v_ref, u_ref):
    j = pl.program_id(1)

    @pl.when(j == 0)
    def _():
        x = x_ref[...]
        u_ref[...] = (x * _rms_scale(x) * g_ref[...]).astype(BF16)

    res = jnp.dot(u_ref[...], w_ref[...].astype(BF16), preferred_element_type=F32)

    @pl.when(j == 0)
    def _():
        k_ref[...] = res

    @pl.when(j == 1)
    def _():
        v_ref[...] = res


def _mem_kv(x, g, w, tm):
    m, k = x.shape
    assert w.shape[1] == 2 * D_X
    half = pl.BlockSpec((tm, D_X), lambda i, j: (i, 0))
    return pl.pallas_call(
        _mem_kv_kernel,
        grid=(m // tm, 2),
        in_specs=[
            pl.BlockSpec((tm, k), lambda i, j: (i, 0)),
            pl.BlockSpec((1, k), lambda i, j: (0, 0)),
            pl.BlockSpec((k, D_X), lambda i, j: (0, j)),
        ],
        out_specs=[half, half],
        out_shape=[jax.ShapeDtypeStruct((m, D_X), F32)] * 2,
        scratch_shapes=[pltpu.VMEM((tm, k), BF16)],
        compiler_params=pltpu.CompilerParams(
            dimension_semantics=("arbitrary", "arbitrary"),
            vmem_limit_bytes=BIG_VMEM_LIMIT),
        name="mem_kv",
    )(x, g, w)


def kernel(x_prompt, x_sample, mem_prompt, state_rglru_h, state_conv, state_pool, cache_mem_k, cache_mem_v, g_pre, w_in, conv_w, conv_b, w_rg_a, b_rg_a, w_rg_x, b_rg_x, lru_lambda, w_pool, pool_scale, g_mem, w_kv, w_branch, w_out, g_post):
    batch, seq, _ = x_prompt.shape
    nb = x_sample.shape[0]
    depth = g_pre.shape[0]
    assert depth == 1 and x_sample.shape[1] == 1

    l = 0
    row = lambda v: v.reshape(1, -1)
    wax = jnp.concatenate([w_rg_a[l], w_rg_x[l]], axis=-1).astype(BF16)
    wpool = w_pool[l].astype(BF16)
    mix_params = (conv_w[l], row(conv_b[l]), wax, row(b_rg_a[l]), row(b_rg_x[l]),
                  row(lru_lambda[l]), wpool, row(pool_scale[l]))

    xp2 = x_prompt.reshape(batch * seq, D_MODEL)
    xs2 = x_sample.reshape(nb, D_MODEL)
    mem2 = mem_prompt.reshape(batch * N_MEM, D_MODEL)

    z_s, w_in_b = _sample_proj(xs2, row(g_pre[l]), w_in[l], tn=SAMPLE_PROJ_TN)
    qoff = 2 * D_RNN + 2 * D_POOL
    q_s = z_s[:, qoff:qoff + D_X].reshape(nb // ATTN_BB, ATTN_BB, D_X)

    mem_k, mem_v = _mem_kv(mem2, row(g_mem[l]), w_kv[l], tm=KV_TM)
    mem_k = mem_k.reshape(batch, N_MEM, D_X)
    mem_v = mem_v.reshape(batch, N_MEM, D_X)

    perm = _chunk_interleave()
    z_p = _prompt_proj(xp2, row(g_pre[l]), w_in_b, perm, tm=PROJ_TM, tn=PROJ_TN)
    o_p, h_p, c_p, p_p, attn_s = _prompt_mix(
        z_p, mem_k, mem_v, *mix_params, perm.T, q_s, _cache_rows(cache_mem_k[l]),
        _cache_rows(cache_mem_v[l]), batch=batch, seq=seq, tm=MIX_TM)
    attn_s = attn_s.reshape(nb, D_X)

    o_s, h_s, c_s, p_s = _sample_mix(
        z_s, attn_s, state_conv[l].transpose(1, 0, 2), state_rglru_h[l],
        state_pool[l].transpose(1, 0, 2), *mix_params, tb=SAMPLE_MIX_TB)
    y_s, w_br_b, w_out_b = _branch_out_cast(o_s, z_s, xs2, w_branch[l], w_out[l], row(g_post[l]))

    y_p = _branch_out(o_p, z_p, xp2, w_br_b, w_out_b, row(g_post[l]), tm=BRANCH_TM)

    return (
        y_p.reshape(batch, seq, D_MODEL),
        y_s.reshape(nb, 1, D_MODEL),
        h_p.reshape(1, batch, D_RNN),
        c_p.reshape(1, batch, CONV_W - 1, D_RNN),
        p_p.reshape(1, batch, POOL_HIST, D_POOL),
        mem_k.reshape(1, batch, N_MEM, N_XHEADS, XHEAD_DIM),
        mem_v.reshape(1, batch, N_MEM, N_XHEADS, XHEAD_DIM),
        h_s.reshape(1, nb, D_RNN),
        c_s.transpose(1, 0, 2)[None],
        p_s.transpose(1, 0, 2)[None],
    )
```

```python
import functools

import jax
import jax.numpy as jnp
from jax import lax
from jax.experimental import pallas as pl
from jax.experimental.pallas import tpu as pltpu

D_MODEL = 2048
PAST_LEN = 16384
D_RNN = 1024
N_RNN_BLOCKS = 8
RNN_BLOCK = D_RNN // N_RNN_BLOCKS
CONV_W = 4
LRU_C = 8.0
D_POOL = 1024
POOL_WINDOWS = (2, 4, 8, 16)
POOL_GROUP = D_POOL // len(POOL_WINDOWS)
POOL_HIST = max(POOL_WINDOWS) - 1
N_MEM = 256
N_XHEADS = 4
XHEAD_DIM = 256
D_X = N_XHEADS * XHEAD_DIM
N_BRANCH = 3
D_MIX = D_RNN + D_POOL + D_X
D_IN = 2 * D_MIX + N_BRANCH * D_MODEL
EPS = 1e-6

SUBLANES = 8
LANES = 128
VMEM_LIMIT = 56 * 1024 * 1024
BIG_VMEM_LIMIT = 60 * 1024 * 1024
MIX_TM = 256
BRANCH_TM = 256
PROJ_TM, PROJ_TN = 1024, 2048
SAMPLE_PROJ_TN = 768
ATTN_BB = 4
SAMPLE_MIX_TB = 64
KV_TM = 1024

BF16 = jnp.bfloat16
F32 = jnp.float32

NEG_LOG2_E = -1.4426950408889634


def _sigmoid(x):
    return 1.0 / (1.0 + jnp.exp2(x * NEG_LOG2_E))


def _silu(x):
    return x * _sigmoid(x)


def _softplus(x):
    return jnp.maximum(x, 0.0) + jnp.log1p(jnp.exp(-jnp.abs(x)))


def _rms_scale(x):
    return lax.rsqrt(jnp.mean(x * x, axis=-1, keepdims=True) + EPS)


def _chunk_interleave():
    nrow = MIX_TM // SUBLANES
    p = jnp.arange(MIX_TM)
    token = (p % SUBLANES) * nrow + p // SUBLANES
    return (token[:, None] == jnp.arange(MIX_TM)[None, :]).astype(BF16)


def _sample_proj_kernel(x_ref, g_ref, w_ref, mem_ref, gm_ref, wkv_ref,
                        o_ref, wb_ref, k_ref, v_ref, u_ref, um_ref, *, k_steps):
    j = pl.program_id(0)

    @pl.when(j == 0)
    def _():
        x = x_ref[...]
        u_ref[...] = (x * _rms_scale(x) * g_ref[...]).astype(BF16)
        mem = mem_ref[...]
        um_ref[...] = (mem * _rms_scale(mem) * gm_ref[...]).astype(BF16)

    w = w_ref[...].astype(BF16)
    wb_ref[...] = w
    o_ref[...] = jnp.dot(u_ref[...], w, preferred_element_type=F32)

    kv = jnp.dot(um_ref[...], wkv_ref[...].astype(BF16), preferred_element_type=F32)

    @pl.when(j < k_steps)
    def _():
        k_ref[...] = kv

    @pl.when(j >= k_steps)
    def _():
        v_ref[...] = kv


def _sample_proj(x, g, w, mem, g_mem, w_kv, tn):
    m, k = x.shape
    n = w.shape[1]
    steps = n // tn
    mrows = mem.shape[0]
    assert w_kv.shape[1] == 2 * D_X == steps * LANES
    k_steps = D_X // LANES
    return pl.pallas_call(
        functools.partial(_sample_proj_kernel, k_steps=k_steps),
        grid=(steps,),
        in_specs=[
            pl.BlockSpec((m, k), lambda j: (0, 0)),
            pl.BlockSpec((1, k), lambda j: (0, 0)),
            pl.BlockSpec((k, tn), lambda j: (0, j)),
            pl.BlockSpec((mrows, k), lambda j: (0, 0), pipeline_mode=pl.Buffered(1)),
            pl.BlockSpec((1, k), lambda j: (0, 0)),
            pl.BlockSpec((k, LANES), lambda j: (0, j)),
        ],
        out_specs=[
            pl.BlockSpec((m, tn), lambda j: (0, j)),
            pl.BlockSpec((k, tn), lambda j: (0, j)),
            pl.BlockSpec((mrows, LANES), lambda j: (0, jnp.minimum(j, k_steps - 1))),
            pl.BlockSpec((mrows, LANES), lambda j: (0, jnp.maximum(j - k_steps, 0))),
        ],
        out_shape=[
            jax.ShapeDtypeStruct((m, n), F32),
            jax.ShapeDtypeStruct((k, n), BF16),
            jax.ShapeDtypeStruct((mrows, D_X), F32),
            jax.ShapeDtypeStruct((mrows, D_X), F32),
        ],
        scratch_shapes=[pltpu.VMEM((m, k), BF16), pltpu.VMEM((mrows, k), BF16)],
        compiler_params=pltpu.CompilerParams(
            dimension_semantics=("arbitrary",),
            vmem_limit_bytes=VMEM_LIMIT),
        name="sample_proj",
    )(x, g, w, mem, g_mem, w_kv)


def _decay_rate(lam):
    return _softplus(-lam) * (LRU_C * NEG_LOG2_E)


def _rglru_block(xc, wax, ba, bx, rate):
    ri = jnp.dot(xc.astype(BF16), wax, preferred_element_type=F32)
    r = _sigmoid(ri[:, :RNN_BLOCK] + ba)
    i = _sigmoid(ri[:, RNN_BLOCK:] + bx)
    a = jnp.exp2(r * rate)
    one_m = 1.0 - a * a
    mult = jnp.where(one_m > 0.0, one_m * lax.rsqrt(one_m), 0.0)
    return a, mult * i * xc


def _prompt_mix_kernel(z_ref, k_ref, v_ref, convw_ref, convb_ref, wax_ref, ba_ref, bx_ref,
                       lam_ref, wpool_ref, pscale_ref, unperm_ref, sq_ref, sk_ref, sv_ref,
                       o_ref, newh_ref, newconv_ref, newpool_ref, sattn_ref,
                       conv_carry, pool_carry, h_carry, kb_ref, vb_ref, ac_scr, hl_scr, op_scr,
                       *, tm):
    l = pl.program_id(1)
    last = pl.num_programs(1) - 1
    nrow = tm // SUBLANES

    @pl.when(l == 0)
    def _():
        conv_carry[...] = jnp.zeros(conv_carry.shape, F32)
        pool_carry[...] = jnp.zeros(pool_carry.shape, F32)
        h_carry[...] = jnp.zeros(h_carry.shape, F32)
        kb_ref[...] = k_ref[0].astype(BF16)
        vb_ref[...] = v_ref[0].astype(BF16)

    side_scores = _sample_attn_scores(sq_ref.at[0], sk_ref, ATTN_BB)

    chunk_id = lax.broadcasted_iota(jnp.int32, (SUBLANES, LANES), 0)
    first_chunk = chunk_id == 0

    def load_groups(col, width=LANES):
        return [z_ref[r * SUBLANES:(r + 1) * SUBLANES, col:col + width] for r in range(nrow)]

    def put(col, width, val):
        op_scr[:, col:col + width] = val.astype(BF16)

    def store_groups(col, rows, width=LANES):
        put(col, width, jnp.concatenate(rows, axis=0))

    def history(tail_group, carry_ref, j, c0):
        tail = pltpu.roll(tail_group, 1, 0)
        prev = jnp.where(first_chunk, carry_ref[j - 1, :, c0:c0 + LANES], tail)
        carry_ref[j - 1, :, c0:c0 + LANES] = tail
        return prev

    rate = _decay_rate(lam_ref[...])
    for n in range(N_RNN_BLOCKS):
        c0, c1 = n * RNN_BLOCK, (n + 1) * RNN_BLOCK
        xs = load_groups(c0)
        ext = [history(xs[nrow - j], conv_carry, j, c0) for j in range(CONV_W - 1, 0, -1)] + xs
        cw = [jnp.broadcast_to(convw_ref[k:k + 1, c0:c1], (SUBLANES, LANES)) for k in range(CONV_W)]
        cb = jnp.broadcast_to(convb_ref[:, c0:c1], (SUBLANES, LANES))
        xc = []
        for r in range(nrow):
            acc = cb + cw[0] * ext[r]
            for k in range(1, CONV_W):
                acc = acc + cw[k] * ext[r + k]
            xc.append(acc)
        a, b = _rglru_block(jnp.concatenate(xc, axis=0), wax_ref[n], ba_ref[:, c0:c1],
                            bx_ref[:, c0:c1], rate[:, c0:c1])
        ac_scr[:, c0:c1] = a
        hl_scr[:, c0:c1] = b

    side_probs = _sample_attn_probs(side_scores)

    acc_a = ac_scr[0:SUBLANES, :]
    acc_h = hl_scr[0:SUBLANES, :]
    for r in range(1, nrow):
        rows = slice(r * SUBLANES, (r + 1) * SUBLANES)
        ar = ac_scr[rows, :]
        acc_h = ar * acc_h + hl_scr[rows, :]
        acc_a = ar * acc_a
        ac_scr[rows, :] = acc_a
        hl_scr[rows, :] = acc_h
    h_in = h_carry[...]
    entering = []
    for c in range(SUBLANES):
        entering.append(h_in)
        h_in = acc_a[c:c + 1] * h_in + acc_h[c:c + 1]
    h_carry[...] = h_in
    h_enter = jnp.concatenate(entering, axis=0)
    for n in range(N_RNN_BLOCKS):
        c0, c1 = n * RNN_BLOCK, (n + 1) * RNN_BLOCK
        gr = load_groups(D_RNN + c0)
        store_groups(c0, [(hl_scr[r * SUBLANES:(r + 1) * SUBLANES, c0:c1]
                           + ac_scr[r * SUBLANES:(r + 1) * SUBLANES, c0:c1] * h_enter[:, c0:c1])
                          * _silu(gr[r]) for r in range(nrow)])

    _sample_attn_values(side_probs, sv_ref, sattn_ref.at[0])

    pcol = 2 * D_RNN
    blocks = [(w, c0) for g, w in enumerate(POOL_WINDOWS)
              for c0 in range(g * POOL_GROUP, (g + 1) * POOL_GROUP, LANES)]

    def group(c0, r):
        return z_ref[r * SUBLANES:(r + 1) * SUBLANES, pcol + c0:pcol + c0 + LANES]

    def mean_minus_token(tot, w, c0, r):
        if r < w - 1:
            pos1 = l * tm + chunk_id * nrow + (r + 1)
            mean = tot / jnp.minimum(pos1, w).astype(F32)
        else:
            mean = tot * (1.0 / w)
        return mean - group(c0, r)

    hist, tot = {}, {}
    for w, c0 in blocks:
        hist[c0] = [history(group(c0, nrow - j), pool_carry, j, c0) for j in range(1, w)]
        t = group(c0, 0)
        for h in hist[c0]:
            t = t + h
        tot[c0] = t
        hl_scr[0:SUBLANES, c0:c0 + LANES] = mean_minus_token(t, w, c0, 0)
    for r in range(1, nrow):
        for w, c0 in blocks:
            leaving = group(c0, r - w) if r >= w else hist[c0][w - r - 1]
            tot[c0] = tot[c0] + (group(c0, r) - leaving)
            hl_scr[r * SUBLANES:(r + 1) * SUBLANES, c0:c0 + LANES] = mean_minus_token(
                tot[c0], w, c0, r)
    for g, w in enumerate(POOL_WINDOWS):
        c0, c1 = g * POOL_GROUP, (g + 1) * POOL_GROUP
        og = jnp.dot(hl_scr[:, c0:c1].astype(BF16), wpool_ref[g], preferred_element_type=F32)
        gp = z_ref[:, pcol + D_POOL + c0:pcol + D_POOL + c1]
        put(D_RNN + c0, POOL_GROUP, og * pscale_ref[:, c0:c1] * _silu(gp))

    qoff = 2 * D_RNN + 2 * D_POOL
    for hd in range(N_XHEADS):
        c0, c1 = hd * XHEAD_DIM, (hd + 1) * XHEAD_DIM
        q = z_ref[:, qoff + c0:qoff + c1].astype(BF16)
        s = lax.dot_general(q, kb_ref[:, c0:c1], (((1,), (1,)), ((), ())),
                            preferred_element_type=F32) * (XHEAD_DIM ** -0.5)
        p = jnp.exp(s - jnp.max(s, axis=-1, keepdims=True))
        p = p / jnp.sum(p, axis=-1, keepdims=True)
        ox = jnp.dot(p.astype(BF16), vb_ref[:, c0:c1], preferred_element_type=F32)
        gx = z_ref[:, qoff + D_X + c0:qoff + D_X + c1]
        put(D_RNN + D_POOL + c0, XHEAD_DIM, ox * _silu(gx))

    o_ref[...] = jnp.dot(unperm_ref[...], op_scr[...], preferred_element_type=F32).astype(BF16)

    @pl.when(l == last)
    def _():
        newh_ref[0] = h_carry[...]
        tail_row = lambda j: (nrow - j) * SUBLANES + SUBLANES - 1
        for j in range(1, CONV_W):
            newconv_ref[0, CONV_W - 1 - j:CONV_W - j, :] = z_ref[tail_row(j):tail_row(j) + 1, 0:D_RNN]
        for j in range(1, POOL_HIST + 1):
            newpool_ref[0, POOL_HIST - j:POOL_HIST - j + 1, :] = (
                z_ref[tail_row(j):tail_row(j) + 1, pcol:pcol + D_POOL])


def _prompt_mix(z, mem_k, mem_v, conv_w, conv_b, wax, b_a, b_x, lam, wpool, pscale, unperm,
                sample_q, cache_k, cache_v, batch, seq, tm):
    nl = seq // tm
    assert sample_q.shape[0] == batch * nl
    side = lambda b, l: (b * nl + l, 0, 0)
    zw = 2 * D_MIX
    const2 = lambda b, l: (0, 0)
    const3 = lambda b, l: (0, 0, 0)
    kern = functools.partial(_prompt_mix_kernel, tm=tm)
    return pl.pallas_call(
        kern,
        grid=(batch, nl),
        in_specs=[
            pl.BlockSpec((tm, zw), lambda b, l: (b * nl + l, 0)),
            pl.BlockSpec((1, N_MEM, D_X), lambda b, l: (b, 0, 0)),
            pl.BlockSpec((1, N_MEM, D_X), lambda b, l: (b, 0, 0)),
            pl.BlockSpec((CONV_W, D_RNN), const2),
            pl.BlockSpec((1, D_RNN), const2),
            pl.BlockSpec((N_RNN_BLOCKS, RNN_BLOCK, 2 * RNN_BLOCK), const3),
            pl.BlockSpec((1, D_RNN), const2),
            pl.BlockSpec((1, D_RNN), const2),
            pl.BlockSpec((1, D_RNN), const2),
            pl.BlockSpec((len(POOL_WINDOWS), POOL_GROUP, POOL_GROUP), const3),
            pl.BlockSpec((1, D_POOL), const2),
            pl.BlockSpec((tm, tm), const2),
            pl.BlockSpec((1, ATTN_BB, D_X), side),
            pl.BlockSpec((ATTN_BB, N_MEM * SUBLANES, LANES), side),
            pl.BlockSpec((ATTN_BB, N_MEM * SUBLANES, LANES), side),
        ],
        out_specs=[
            pl.BlockSpec((tm, D_MIX), lambda b, l: (b * nl + l, 0)),
            pl.BlockSpec((1, 1, D_RNN), lambda b, l: (b, 0, 0)),
            pl.BlockSpec((1, CONV_W - 1, D_RNN), lambda b, l: (b, 0, 0)),
            pl.BlockSpec((1, POOL_HIST, D_POOL), lambda b, l: (b, 0, 0)),
            pl.BlockSpec((1, ATTN_BB, D_X), side),
        ],
        out_shape=[
            jax.ShapeDtypeStruct((batch * seq, D_MIX), BF16),
            jax.ShapeDtypeStruct((batch, 1, D_RNN), F32),
            jax.ShapeDtypeStruct((batch, CONV_W - 1, D_RNN), F32),
            jax.ShapeDtypeStruct((batch, POOL_HIST, D_POOL), F32),
            jax.ShapeDtypeStruct(sample_q.shape, F32),
        ],
        scratch_shapes=[
            pltpu.VMEM((CONV_W - 1, SUBLANES, D_RNN), F32),
            pltpu.VMEM((POOL_HIST, SUBLANES, D_POOL), F32),
            pltpu.VMEM((1, D_RNN), F32),
            pltpu.VMEM((N_MEM, D_X), BF16),
            pltpu.VMEM((N_MEM, D_X), BF16),
            pltpu.VMEM((tm, D_RNN), F32),
            pltpu.VMEM((tm, D_RNN), F32),
            pltpu.VMEM((tm, D_MIX), BF16),
        ],
        compiler_params=pltpu.CompilerParams(
            dimension_semantics=("arbitrary", "arbitrary"),
            vmem_limit_bytes=VMEM_LIMIT),
        name="prompt_mix",
    )(z, mem_k, mem_v, conv_w, conv_b, wax, b_a, b_x, lam, wpool, pscale, unperm,
      sample_q, cache_k, cache_v)


def _cache_rows(c):
    nb = c.shape[0]
    c = c.reshape(nb, N_MEM, N_XHEADS, XHEAD_DIM // LANES, LANES)
    return c.transpose(0, 1, 3, 2, 4).reshape(nb, N_MEM * SUBLANES, LANES)


def _sample_attn_scores(q_ref, k_ref, bb):
    halves = XHEAD_DIM // LANES
    assert halves * N_XHEADS == SUBLANES
    scores = []
    for j in range(bb):
        qn = jnp.concatenate(
            [q_ref[j:j + 1, (h * halves + t) * LANES:(h * halves + t + 1) * LANES]
             for t in range(halves) for h in range(N_XHEADS)], axis=0)
        scores.append(lax.dot_general(qn.astype(BF16), k_ref[j].astype(BF16),
                                      (((1,), (1,)), ((), ())), preferred_element_type=F32)
                      * (XHEAD_DIM ** -0.5))
    return scores


def _sample_attn_probs(scores):
    r = lax.broadcasted_iota(jnp.int32, (SUBLANES, LANES), 0)
    c = lax.broadcasted_iota(jnp.int32, (SUBLANES, LANES), 1)
    diag = (c % SUBLANES) == r
    first_half = r < N_XHEADS
    nchunk = N_MEM * SUBLANES // LANES
    probs = []
    for s in scores:
        chunks = []
        for ci in range(nchunk):
            sm = jnp.where(diag, s[:, ci * LANES:(ci + 1) * LANES], 0.0)
            other = pltpu.roll(sm, N_XHEADS, 0)
            other = jnp.where(first_half, pltpu.roll(other, LANES - N_XHEADS, 1),
                              pltpu.roll(other, N_XHEADS, 1))
            chunks.append(jnp.where(diag, sm + other, -jnp.inf))
        t_full = jnp.concatenate(chunks, axis=1)
        e = jnp.exp(t_full - jnp.max(t_full, axis=1, keepdims=True))
        probs.append((e / jnp.sum(e, axis=1, keepdims=True)).astype(BF16))
    return probs


def _sample_attn_values(probs, v_ref, o_ref):
    halves = XHEAD_DIM // LANES
    for j, p in enumerate(probs):
        o = jnp.dot(p, v_ref[j].astype(BF16), preferred_element_type=F32)
        for t in range(halves):
            for h in range(N_XHEADS):
                col = (h * halves + t) * LANES
                o_ref[j:j + 1, col:col + LANES] = o[t * N_XHEADS + h:t * N_XHEADS + h + 1, :]


def _prompt_proj_kernel(x_ref, g_ref, w_ref, perm_ref, o_ref, u_ref, up_ref, *, mix_steps):
    j = pl.program_id(1)

    @pl.when(j == 0)
    def _():
        x = x_ref[...]
        u = (x * _rms_scale(x) * g_ref[...]).astype(BF16)
        u_ref[...] = u
        for r0 in range(0, u.shape[0], MIX_TM):
            up_ref[r0:r0 + MIX_TM, :] = jnp.dot(
                perm_ref[...], u[r0:r0 + MIX_TM], preferred_element_type=F32).astype(BF16)

    @pl.when(j < mix_steps)
    def _():
        o_ref[...] = jnp.dot(up_ref[...], w_ref[...], preferred_element_type=F32)

    @pl.when(j >= mix_steps)
    def _():
        o_ref[...] = jnp.dot(u_ref[...], w_ref[...], preferred_element_type=F32)


def _prompt_proj(x, g, w, perm, tm, tn):
    m, k = x.shape
    n = w.shape[1]
    assert (2 * D_MIX) % tn == 0
    return pl.pallas_call(
        functools.partial(_prompt_proj_kernel, mix_steps=2 * D_MIX // tn),
        grid=(m // tm, n // tn),
        in_specs=[
            pl.BlockSpec((tm, k), lambda i, j: (i, 0)),
            pl.BlockSpec((1, k), lambda i, j: (0, 0)),
            pl.BlockSpec((k, tn), lambda i, j: (0, j)),
            pl.BlockSpec(perm.shape, lambda i, j: (0, 0)),
        ],
        out_specs=pl.BlockSpec((tm, tn), lambda i, j: (i, j)),
        out_shape=jax.ShapeDtypeStruct((m, n), F32),
        scratch_shapes=[pltpu.VMEM((tm, k), BF16), pltpu.VMEM((tm, k), BF16)],
        compiler_params=pltpu.CompilerParams(
            dimension_semantics=("arbitrary", "arbitrary"),
            vmem_limit_bytes=BIG_VMEM_LIMIT),
        name="prompt_proj",
    )(x, g, w, perm)


def _sample_mix_kernel(z_ref, attn_ref, conv_ref, h_ref, pool_ref,
                       convw_ref, convb_ref, wax_ref, ba_ref, bx_ref, lam_ref, wpool_ref,
                       pscale_ref, o_ref, newh_ref, newconv_ref, newpool_ref):
    xr = z_ref[:, 0:D_RNN]
    xc = convb_ref[...] + convw_ref[CONV_W - 1:CONV_W, :] * xr
    for k in range(CONV_W - 1):
        xc = xc + convw_ref[k:k + 1, :] * conv_ref[k]
    for k in range(CONV_W - 2):
        newconv_ref[k] = conv_ref[k + 1]
    newconv_ref[CONV_W - 2] = xr

    rate = _decay_rate(lam_ref[...])
    for n in range(N_RNN_BLOCKS):
        c0, c1 = n * RNN_BLOCK, (n + 1) * RNN_BLOCK
        a, b = _rglru_block(xc[:, c0:c1], wax_ref[n], ba_ref[:, c0:c1], bx_ref[:, c0:c1],
                            rate[:, c0:c1])
        h = a * h_ref[:, c0:c1] + b
        newh_ref[:, c0:c1] = h
        o_ref[:, c0:c1] = (h * _silu(z_ref[:, D_RNN + c0:D_RNN + c1])).astype(BF16)

    xp = z_ref[:, 2 * D_RNN:2 * D_RNN + D_POOL]
    for k in range(POOL_HIST - 1):
        newpool_ref[k] = pool_ref[k + 1]
    newpool_ref[POOL_HIST - 1] = xp
    for g, w in enumerate(POOL_WINDOWS):
        c0, c1 = g * POOL_GROUP, (g + 1) * POOL_GROUP
        xg = xp[:, c0:c1]
        tot = xg
        for j in range(1, w):
            tot = tot + pool_ref[POOL_HIST - j, :, c0:c1]
        cnt = float(min(PAST_LEN + 1, w))
        d = tot / cnt - xg
        og = jnp.dot(d.astype(BF16), wpool_ref[g], preferred_element_type=F32)
        gp = z_ref[:, 2 * D_RNN + D_POOL + c0:2 * D_RNN + D_POOL + c1]
        o_ref[:, D_RNN + c0:D_RNN + c1] = (og * pscale_ref[:, c0:c1] * _silu(gp)).astype(BF16)

    gx = z_ref[:, 2 * D_RNN + 2 * D_POOL + D_X:2 * D_MIX]
    o_ref[:, D_RNN + D_POOL:] = (attn_ref[...] * _silu(gx)).astype(BF16)


def _sample_mix(z, attn, conv, h, pool, conv_w, conv_b, wax, b_a, b_x, lam, wpool, pscale, tb):
    nb = z.shape[0]
    zw = 2 * D_MIX
    rows = lambda i: (i, 0)
    const2 = lambda i: (0, 0)
    const3 = lambda i: (0, 0, 0)
    hist = lambda i: (0, i, 0)
    return pl.pallas_call(
        _sample_mix_kernel,
        grid=(nb // tb,),
        in_specs=[
            pl.BlockSpec((tb, zw), rows),
            pl.BlockSpec((tb, D_X), rows),
            pl.BlockSpec((CONV_W - 1, tb, D_RNN), hist),
            pl.BlockSpec((tb, D_RNN), rows),
            pl.BlockSpec((POOL_HIST, tb, D_POOL), hist),
            pl.BlockSpec((CONV_W, D_RNN), const2),
            pl.BlockSpec((1, D_RNN), const2),
            pl.BlockSpec((N_RNN_BLOCKS, RNN_BLOCK, 2 * RNN_BLOCK), const3),
            pl.BlockSpec((1, D_RNN), const2),
            pl.BlockSpec((1, D_RNN), const2),
            pl.BlockSpec((1, D_RNN), const2),
            pl.BlockSpec((len(POOL_WINDOWS), POOL_GROUP, POOL_GROUP), const3),
            pl.BlockSpec((1, D_POOL), const2),
        ],
        out_specs=[
            pl.BlockSpec((tb, D_MIX), rows),
            pl.BlockSpec((tb, D_RNN), rows),
            pl.BlockSpec((CONV_W - 1, tb, D_RNN), hist),
            pl.BlockSpec((POOL_HIST, tb, D_POOL), hist),
        ],
        out_shape=[
            jax.ShapeDtypeStruct((nb, D_MIX), BF16),
            jax.ShapeDtypeStruct((nb, D_RNN), F32),
            jax.ShapeDtypeStruct((CONV_W - 1, nb, D_RNN), F32),
            jax.ShapeDtypeStruct((POOL_HIST, nb, D_POOL), F32),
        ],
        compiler_params=pltpu.CompilerParams(
            dimension_semantics=("arbitrary",),
            vmem_limit_bytes=VMEM_LIMIT),
        name="sample_mix",
    )(z, attn, conv, h, pool, conv_w, conv_b, wax, b_a, b_x, lam, wpool, pscale)


def _branch_out_kernel(o_ref, gates_ref, x_ref, wb_ref, wo_ref, gpost_ref, y_ref):
    merged = None
    for j, (r0, r1) in enumerate(((0, D_RNN), (D_RNN, D_RNN + D_POOL), (D_RNN + D_POOL, D_MIX))):
        yj = jnp.dot(o_ref[:, r0:r1], wb_ref[r0:r1, :], preferred_element_type=F32)
        term = _sigmoid(gates_ref[:, j * D_MODEL:(j + 1) * D_MODEL]) * yj
        merged = term if merged is None else merged + term
    out = jnp.dot(merged.astype(BF16), wo_ref[...], preferred_element_type=F32)
    y_ref[...] = x_ref[...] + (out * gpost_ref[...]) * _rms_scale(out)


def _branch_out(o, z, x, wb, wo, g_post, tm):
    m = x.shape[0]
    gw = N_BRANCH * D_MODEL
    gblk = (2 * D_MIX) // gw
    resident = pl.Buffered(1)
    return pl.pallas_call(
        _branch_out_kernel,
        grid=(m // tm,),
        in_specs=[
            pl.BlockSpec((tm, D_MIX), lambda i: (i, 0)),
            pl.BlockSpec((tm, gw), lambda i: (i, gblk)),
            pl.BlockSpec((tm, D_MODEL), lambda i: (i, 0)),
            pl.BlockSpec((D_MIX, D_MODEL), lambda i: (0, 0), pipeline_mode=resident),
            pl.BlockSpec((D_MODEL, D_MODEL), lambda i: (0, 0), pipeline_mode=resident),
            pl.BlockSpec((1, D_MODEL), lambda i: (0, 0)),
        ],
        out_specs=pl.BlockSpec((tm, D_MODEL), lambda i: (i, 0)),
        out_shape=jax.ShapeDtypeStruct((m, D_MODEL), F32),
        compiler_params=pltpu.CompilerParams(
            dimension_semantics=("arbitrary",),
            vmem_limit_bytes=VMEM_LIMIT),
        name="branch_out",
    )(o, z, x, wb, wo, g_post)


WROWS = 1024
PER_BRANCH = D_RNN // WROWS
assert D_RNN == D_POOL == D_X and D_RNN % WROWS == 0 and D_MODEL % WROWS == 0
N_WB_BLOCKS = N_BRANCH * PER_BRANCH
N_WOUT_BLOCKS = D_MODEL // WROWS


def _branch_out_cast_kernel(o_ref, gates_ref, x_ref, wb_ref, wo_ref, gpost_ref,
                            y_ref, wbb_ref, wob_ref, merged_ref, out_ref):
    s = pl.program_id(0)

    @pl.when(s < N_WB_BLOCKS)
    def _():
        w = wb_ref[...].astype(BF16)
        wbb_ref[...] = w
        term = _sigmoid(gates_ref[...]) * jnp.dot(o_ref[...], w, preferred_element_type=F32)

        @pl.when(s == 0)
        def _():
            merged_ref[...] = term

        @pl.when(s > 0)
        def _():
            merged_ref[...] += term

    for kb in range(N_WOUT_BLOCKS):
        @pl.when(s == N_WB_BLOCKS + kb)
        def _(kb=kb):
            w = wo_ref[...].astype(BF16)
            wob_ref[...] = w
            part = jnp.dot(merged_ref[:, kb * WROWS:(kb + 1) * WROWS].astype(BF16), w,
                           preferred_element_type=F32)
            if kb == 0:
                out_ref[...] = part
            else:
                out_ref[...] += part

    @pl.when(s == N_WB_BLOCKS + N_WOUT_BLOCKS - 1)
    def _():
        out = out_ref[...]
        y_ref[...] = x_ref[...] + out * _rms_scale(out) * gpost_ref[...]


def _branch_out_cast(o, z, x, wb, wo, g_post):
    m = x.shape[0]
    gblk0 = (2 * D_MIX) // D_MODEL
    wb_blk = lambda s: jnp.minimum(s, N_WB_BLOCKS - 1)
    wo_blk = lambda s: jnp.maximum(s - N_WB_BLOCKS, 0)
    return pl.pallas_call(
        _branch_out_cast_kernel,
        grid=(N_WB_BLOCKS + N_WOUT_BLOCKS,),
        in_specs=[
            pl.BlockSpec((m, WROWS), lambda s: (0, wb_blk(s))),
            pl.BlockSpec((m, D_MODEL), lambda s: (0, gblk0 + wb_blk(s) // PER_BRANCH)),
            pl.BlockSpec((m, D_MODEL), lambda s: (0, 0)),
            pl.BlockSpec((WROWS, D_MODEL), lambda s: (wb_blk(s), 0)),
            pl.BlockSpec((WROWS, D_MODEL), lambda s: (wo_blk(s), 0)),
            pl.BlockSpec((1, D_MODEL), lambda s: (0, 0)),
        ],
        out_specs=[
            pl.BlockSpec((m, D_MODEL), lambda s: (0, 0)),
            pl.BlockSpec((WROWS, D_MODEL), lambda s: (wb_blk(s), 0)),
            pl.BlockSpec((WROWS, D_MODEL), lambda s: (wo_blk(s), 0)),
        ],
        out_shape=[
            jax.ShapeDtypeStruct((m, D_MODEL), F32),
            jax.ShapeDtypeStruct(wb.shape, BF16),
            jax.ShapeDtypeStruct(wo.shape, BF16),
        ],
        scratch_shapes=[pltpu.VMEM((m, D_MODEL), F32), pltpu.VMEM((m, D_MODEL), F32)],
        compiler_params=pltpu.CompilerParams(
            dimension_semantics=("arbitrary",),
            vmem_limit_bytes=BIG_VMEM_LIMIT),
        name="branch_out_cast",
    )(o, z, x, wb, wo, g_post)


def _mem_kv_kernel(x_ref, g_ref, w_ref, k_ref, v_ref, u_ref):
    j = pl.program_id(1)

    @pl.when(j == 0)
    def _():
        x = x_ref[...]
        u_ref[...] = (x * _rms_scale(x) * g_ref[...]).astype(BF16)

    res = jnp.dot(u_ref[...], w_ref[...].astype(BF16), preferred_element_type=F32)

    @pl.when(j == 0)
    def _():
        k_ref[...] = res

    @pl.when(j == 1)
    def _():
        v_ref[...] = res


def _mem_kv(x, g, w, tm):
    m, k = x.shape
    assert w.shape[1] == 2 * D_X
    half = pl.BlockSpec((tm, D_X), lambda i, j: (i, 0))
    return pl.pallas_call(
        _mem_kv_kernel,
        grid=(m // tm, 2),
        in_specs=[
            pl.BlockSpec((tm, k), lambda i, j: (i, 0)),
            pl.BlockSpec((1, k), lambda i, j: (0, 0)),
            pl.BlockSpec((k, D_X), lambda i, j: (0, j)),
        ],
        out_specs=[half, half],
        out_shape=[jax.ShapeDtypeStruct((m, D_X), F32)] * 2,
        scratch_shapes=[pltpu.VMEM((tm, k), BF16)],
        compiler_params=pltpu.CompilerParams(
            dimension_semantics=("arbitrary", "arbitrary"),
            vmem_limit_bytes=BIG_VMEM_LIMIT),
        name="mem_kv",
    )(x, g, w)


def kernel(x_prompt, x_sample, mem_prompt, state_rglru_h, state_conv, state_pool, cache_mem_k, cache_mem_v, g_pre, w_in, conv_w, conv_b, w_rg_a, b_rg_a, w_rg_x, b_rg_x, lru_lambda, w_pool, pool_scale, g_mem, w_kv, w_branch, w_out, g_post):
    batch, seq, _ = x_prompt.shape
    nb = x_sample.shape[0]
    depth = g_pre.shape[0]
    assert depth == 1 and x_sample.shape[1] == 1

    l = 0
    row = lambda v: v.reshape(1, -1)
    wax = jnp.concatenate([w_rg_a[l], w_rg_x[l]], axis=-1).astype(BF16)
    wpool = w_pool[l].astype(BF16)
    mix_params = (conv_w[l], row(conv_b[l]), wax, row(b_rg_a[l]), row(b_rg_x[l]),
                  row(lru_lambda[l]), wpool, row(pool_scale[l]))

    xp2 = x_prompt.reshape(batch * seq, D_MODEL)
    xs2 = x_sample.reshape(nb, D_MODEL)
    mem2 = mem_prompt.reshape(batch * N_MEM, D_MODEL)

    z_s, w_in_b, mem_k, mem_v = _sample_proj(xs2, row(g_pre[l]), w_in[l], mem2, row(g_mem[l]),
                                             w_kv[l], tn=SAMPLE_PROJ_TN)
    qoff = 2 * D_RNN + 2 * D_POOL
    q_s = z_s[:, qoff:qoff + D_X].reshape(nb // ATTN_BB, ATTN_BB, D_X)

    mem_k = mem_k.reshape(batch, N_MEM, D_X)
    mem_v = mem_v.reshape(batch, N_MEM, D_X)

    perm = _chunk_interleave()
    z_p = _prompt_proj(xp2, row(g_pre[l]), w_in_b, perm, tm=PROJ_TM, tn=PROJ_TN)
    o_p, h_p, c_p, p_p, attn_s = _prompt_mix(
        z_p, mem_k, mem_v, *mix_params, perm.T, q_s, _cache_rows(cache_mem_k[l]),
        _cache_rows(cache_mem_v[l]), batch=batch, seq=seq, tm=MIX_TM)
    attn_s = attn_s.reshape(nb, D_X)

    o_s, h_s, c_s, p_s = _sample_mix(
        z_s, attn_s, state_conv[l].transpose(1, 0, 2), state_rglru_h[l],
        state_pool[l].transpose(1, 0, 2), *mix_params, tb=SAMPLE_MIX_TB)
    y_s, w_br_b, w_out_b = _branch_out_cast(o_s, z_s, xs2, w_branch[l], w_out[l], row(g_post[l]))

    y_p = _branch_out(o_p, z_p, xp2, w_br_b, w_out_b, row(g_post[l]), tm=BRANCH_TM)

    return (
        y_p.reshape(batch, seq, D_MODEL),
        y_s.reshape(nb, 1, D_MODEL),
        h_p.reshape(1, batch, D_RNN),
        c_p.reshape(1, batch, CONV_W - 1, D_RNN),
        p_p.reshape(1, batch, POOL_HIST, D_POOL),
        mem_k.reshape(1, batch, N_MEM, N_XHEADS, XHEAD_DIM),
        mem_v.reshape(1, batch, N_MEM, N_XHEADS, XHEAD_DIM),
        h_s.reshape(1, nb, D_RNN),
        c_s.transpose(1, 0, 2)[None],
        p_s.transpose(1, 0, 2)[None],
    )
```

```python
import functools

import jax
import jax.numpy as jnp
from jax import lax
from jax.experimental import pallas as pl
from jax.experimental.pallas import tpu as pltpu

D_MODEL = 2048
PAST_LEN = 16384
D_RNN = 1024
N_RNN_BLOCKS = 8
RNN_BLOCK = D_RNN // N_RNN_BLOCKS
CONV_W = 4
LRU_C = 8.0
D_POOL = 1024
POOL_WINDOWS = (2, 4, 8, 16)
POOL_GROUP = D_POOL // len(POOL_WINDOWS)
POOL_HIST = max(POOL_WINDOWS) - 1
N_MEM = 256
N_XHEADS = 4
XHEAD_DIM = 256
D_X = N_XHEADS * XHEAD_DIM
N_BRANCH = 3
D_MIX = D_RNN + D_POOL + D_X
D_IN = 2 * D_MIX + N_BRANCH * D_MODEL
EPS = 1e-6

SUBLANES = 8
LANES = 128
VMEM_LIMIT = 56 * 1024 * 1024
BIG_VMEM_LIMIT = 60 * 1024 * 1024
MIX_TM = 256
BRANCH_TM = 256
PROJ_TM, PROJ_TN = 1024, 2048
SAMPLE_PROJ_TN = 768
ATTN_BB = 4
SAMPLE_MIX_TB = 64

BF16 = jnp.bfloat16
F32 = jnp.float32

NEG_LOG2_E = -1.4426950408889634


def _sigmoid(x):
    return 1.0 / (1.0 + jnp.exp2(x * NEG_LOG2_E))


def _silu(x):
    return x * _sigmoid(x)


def _softplus(x):
    return jnp.maximum(x, 0.0) + jnp.log1p(jnp.exp(-jnp.abs(x)))


def _rms_scale(x):
    return lax.rsqrt(jnp.mean(x * x, axis=-1, keepdims=True) + EPS)


def _chunk_interleave():
    nrow = MIX_TM // SUBLANES
    p = jnp.arange(MIX_TM)
    token = (p % SUBLANES) * nrow + p // SUBLANES
    return (token[:, None] == jnp.arange(MIX_TM)[None, :]).astype(BF16)


def _sample_proj_kernel(x_ref, g_ref, w_ref, mem_ref, gm_ref, wkv_ref,
                        o_ref, wb_ref, k_ref, v_ref, u_ref, um_ref, *, k_steps):
    j = pl.program_id(0)

    @pl.when(j == 0)
    def _():
        x = x_ref[...]
        u_ref[...] = (x * _rms_scale(x) * g_ref[...]).astype(BF16)
        mem = mem_ref[...]
        um_ref[...] = (mem * _rms_scale(mem) * gm_ref[...]).astype(BF16)

    w = w_ref[...].astype(BF16)
    wb_ref[...] = w
    o_ref[...] = jnp.dot(u_ref[...], w, preferred_element_type=F32)

    kv = jnp.dot(um_ref[...], wkv_ref[...].astype(BF16), preferred_element_type=F32)

    @pl.when(j < k_steps)
    def _():
        k_ref[...] = kv

    @pl.when(j >= k_steps)
    def _():
        v_ref[...] = kv


def _sample_proj(x, g, w, mem, g_mem, w_kv, tn):
    m, k = x.shape
    n = w.shape[1]
    steps = n // tn
    mrows = mem.shape[0]
    kv_tn = 2 * D_X // steps
    assert D_X % kv_tn == 0 and kv_tn % LANES == 0
    k_steps = D_X // kv_tn
    return pl.pallas_call(
        functools.partial(_sample_proj_kernel, k_steps=k_steps),
        grid=(steps,),
        in_specs=[
            pl.BlockSpec((m, k), lambda j: (0, 0)),
            pl.BlockSpec((1, k), lambda j: (0, 0)),
            pl.BlockSpec((k, tn), lambda j: (0, j)),
            pl.BlockSpec((mrows, k), lambda j: (0, 0), pipeline_mode=pl.Buffered(1)),
            pl.BlockSpec((1, k), lambda j: (0, 0)),
            pl.BlockSpec((k, kv_tn), lambda j: (0, j)),
        ],
        out_specs=[
            pl.BlockSpec((m, tn), lambda j: (0, j)),
            pl.BlockSpec((k, tn), lambda j: (0, j)),
            pl.BlockSpec((mrows, kv_tn), lambda j: (0, jnp.minimum(j, k_steps - 1))),
            pl.BlockSpec((mrows, kv_tn), lambda j: (0, jnp.maximum(j - k_steps, 0))),
        ],
        out_shape=[
            jax.ShapeDtypeStruct((m, n), F32),
            jax.ShapeDtypeStruct((k, n), BF16),
            jax.ShapeDtypeStruct((mrows, D_X), F32),
            jax.ShapeDtypeStruct((mrows, D_X), F32),
        ],
        scratch_shapes=[pltpu.VMEM((m, k), BF16), pltpu.VMEM((mrows, k), BF16)],
        compiler_params=pltpu.CompilerParams(
            dimension_semantics=("arbitrary",),
            vmem_limit_bytes=VMEM_LIMIT),
        name="sample_proj",
    )(x, g, w, mem, g_mem, w_kv)


def _decay_rate(lam):
    return _softplus(-lam) * (LRU_C * NEG_LOG2_E)


def _rglru_block(xc, wax, ba, bx, rate):
    ri = jnp.dot(xc.astype(BF16), wax, preferred_element_type=F32)
    r = _sigmoid(ri[:, :RNN_BLOCK] + ba)
    i = _sigmoid(ri[:, RNN_BLOCK:] + bx)
    a = jnp.exp2(r * rate)
    one_m = 1.0 - a * a
    mult = jnp.where(one_m > 0.0, one_m * lax.rsqrt(one_m), 0.0)
    return a, mult * i * xc


def _prompt_mix_kernel(z_ref, k_ref, v_ref, convw_ref, convb_ref, wax_ref, ba_ref, bx_ref,
                       lam_ref, wpool_ref, pscale_ref, unperm_ref, sq_ref, sk_ref, sv_ref,
                       o_ref, newh_ref, newconv_ref, newpool_ref, sattn_ref,
                       conv_carry, pool_carry, h_carry, kb_ref, vb_ref, ac_scr, hl_scr, op_scr,
                       *, tm):
    l = pl.program_id(1)
    last = pl.num_programs(1) - 1
    nrow = tm // SUBLANES

    @pl.when(l == 0)
    def _():
        conv_carry[...] = jnp.zeros(conv_carry.shape, F32)
        pool_carry[...] = jnp.zeros(pool_carry.shape, F32)
        h_carry[...] = jnp.zeros(h_carry.shape, F32)
        kb_ref[...] = k_ref[0].astype(BF16)
        vb_ref[...] = v_ref[0].astype(BF16)

    side_scores = _sample_attn_scores(sq_ref.at[0], sk_ref, ATTN_BB)

    chunk_id = lax.broadcasted_iota(jnp.int32, (SUBLANES, LANES), 0)
    first_chunk = chunk_id == 0

    def load_groups(col, width=LANES):
        return [z_ref[r * SUBLANES:(r + 1) * SUBLANES, col:col + width] for r in range(nrow)]

    def put(col, width, val):
        op_scr[:, col:col + width] = val.astype(BF16)

    def store_groups(col, rows, width=LANES):
        put(col, width, jnp.concatenate(rows, axis=0))

    def history(tail_group, carry_ref, j, c0):
        tail = pltpu.roll(tail_group, 1, 0)
        prev = jnp.where(first_chunk, carry_ref[j - 1, :, c0:c0 + LANES], tail)
        carry_ref[j - 1, :, c0:c0 + LANES] = tail
        return prev

    rate = _decay_rate(lam_ref[...])
    for n in range(N_RNN_BLOCKS):
        c0, c1 = n * RNN_BLOCK, (n + 1) * RNN_BLOCK
        xs = load_groups(c0)
        ext = [history(xs[nrow - j], conv_carry, j, c0) for j in range(CONV_W - 1, 0, -1)] + xs
        cw = [jnp.broadcast_to(convw_ref[k:k + 1, c0:c1], (SUBLANES, LANES)) for k in range(CONV_W)]
        cb = jnp.broadcast_to(convb_ref[:, c0:c1], (SUBLANES, LANES))
        xc = []
        for r in range(nrow):
            acc = cb + cw[0] * ext[r]
            for k in range(1, CONV_W):
                acc = acc + cw[k] * ext[r + k]
            xc.append(acc)
        a, b = _rglru_block(jnp.concatenate(xc, axis=0), wax_ref[n], ba_ref[:, c0:c1],
                            bx_ref[:, c0:c1], rate[:, c0:c1])
        ac_scr[:, c0:c1] = a
        hl_scr[:, c0:c1] = b

    side_probs = _sample_attn_probs(side_scores)

    acc_a = ac_scr[0:SUBLANES, :]
    acc_h = hl_scr[0:SUBLANES, :]
    for r in range(1, nrow):
        rows = slice(r * SUBLANES, (r + 1) * SUBLANES)
        ar = ac_scr[rows, :]
        acc_h = ar * acc_h + hl_scr[rows, :]
        acc_a = ar * acc_a
        ac_scr[rows, :] = acc_a
        hl_scr[rows, :] = acc_h
    h_in = h_carry[...]
    entering = []
    for c in range(SUBLANES):
        entering.append(h_in)
        h_in = acc_a[c:c + 1] * h_in + acc_h[c:c + 1]
    h_carry[...] = h_in
    h_enter = jnp.concatenate(entering, axis=0)
    for n in range(N_RNN_BLOCKS):
        c0, c1 = n * RNN_BLOCK, (n + 1) * RNN_BLOCK
        gr = load_groups(D_RNN + c0)
        store_groups(c0, [(hl_scr[r * SUBLANES:(r + 1) * SUBLANES, c0:c1]
                           + ac_scr[r * SUBLANES:(r + 1) * SUBLANES, c0:c1] * h_enter[:, c0:c1])
                          * _silu(gr[r]) for r in range(nrow)])

    _sample_attn_values(side_probs, sv_ref, sattn_ref.at[0])

    pcol = 2 * D_RNN
    blocks = [(w, c0) for g, w in enumerate(POOL_WINDOWS)
              for c0 in range(g * POOL_GROUP, (g + 1) * POOL_GROUP, LANES)]

    def group(c0, r):
        return z_ref[r * SUBLANES:(r + 1) * SUBLANES, pcol + c0:pcol + c0 + LANES]

    def mean_minus_token(tot, w, c0, r):
        if r < w - 1:
            pos1 = l * tm + chunk_id * nrow + (r + 1)
            mean = tot / jnp.minimum(pos1, w).astype(F32)
        else:
            mean = tot * (1.0 / w)
        return mean - group(c0, r)

    hist, tot = {}, {}
    for w, c0 in blocks:
        hist[c0] = [history(group(c0, nrow - j), pool_carry, j, c0) for j in range(1, w)]
        t = group(c0, 0)
        for h in hist[c0]:
            t = t + h
        tot[c0] = t
        hl_scr[0:SUBLANES, c0:c0 + LANES] = mean_minus_token(t, w, c0, 0)
    for r in range(1, nrow):
        for w, c0 in blocks:
            leaving = group(c0, r - w) if r >= w else hist[c0][w - r - 1]
            tot[c0] = tot[c0] + (group(c0, r) - leaving)
            hl_scr[r * SUBLANES:(r + 1) * SUBLANES, c0:c0 + LANES] = mean_minus_token(
                tot[c0], w, c0, r)
    for g, w in enumerate(POOL_WINDOWS):
        c0, c1 = g * POOL_GROUP, (g + 1) * POOL_GROUP
        og = jnp.dot(hl_scr[:, c0:c1].astype(BF16), wpool_ref[g], preferred_element_type=F32)
        gp = z_ref[:, pcol + D_POOL + c0:pcol + D_POOL + c1]
        put(D_RNN + c0, POOL_GROUP, og * pscale_ref[:, c0:c1] * _silu(gp))

    qoff = 2 * D_RNN + 2 * D_POOL
    for hd in range(N_XHEADS):
        c0, c1 = hd * XHEAD_DIM, (hd + 1) * XHEAD_DIM
        q = z_ref[:, qoff + c0:qoff + c1].astype(BF16)
        s = lax.dot_general(q, kb_ref[:, c0:c1], (((1,), (1,)), ((), ())),
                            preferred_element_type=F32) * (XHEAD_DIM ** -0.5)
        p = jnp.exp(s - jnp.max(s, axis=-1, keepdims=True))
        p = p / jnp.sum(p, axis=-1, keepdims=True)
        ox = jnp.dot(p.astype(BF16), vb_ref[:, c0:c1], preferred_element_type=F32)
        gx = z_ref[:, qoff + D_X + c0:qoff + D_X + c1]
        put(D_RNN + D_POOL + c0, XHEAD_DIM, ox * _silu(gx))

    o_ref[...] = jnp.dot(unperm_ref[...], op_scr[...], preferred_element_type=F32).astype(BF16)

    @pl.when(l == last)
    def _():
        newh_ref[0] = h_carry[...]
        tail_row = lambda j: (nrow - j) * SUBLANES + SUBLANES - 1
        for j in range(1, CONV_W):
            newconv_ref[0, CONV_W - 1 - j:CONV_W - j, :] = z_ref[tail_row(j):tail_row(j) + 1, 0:D_RNN]
        for j in range(1, POOL_HIST + 1):
            newpool_ref[0, POOL_HIST - j:POOL_HIST - j + 1, :] = (
                z_ref[tail_row(j):tail_row(j) + 1, pcol:pcol + D_POOL])


def _prompt_mix(z, mem_k, mem_v, conv_w, conv_b, wax, b_a, b_x, lam, wpool, pscale, unperm,
                sample_q, cache_k, cache_v, batch, seq, tm):
    nl = seq // tm
    assert sample_q.shape[0] == batch * nl
    side = lambda b, l: (b * nl + l, 0, 0)
    zw = 2 * D_MIX
    const2 = lambda b, l: (0, 0)
    const3 = lambda b, l: (0, 0, 0)
    kern = functools.partial(_prompt_mix_kernel, tm=tm)
    return pl.pallas_call(
        kern,
        grid=(batch, nl),
        in_specs=[
            pl.BlockSpec((tm, zw), lambda b, l: (b * nl + l, 0)),
            pl.BlockSpec((1, N_MEM, D_X), lambda b, l: (b, 0, 0)),
            pl.BlockSpec((1, N_MEM, D_X), lambda b, l: (b, 0, 0)),
            pl.BlockSpec((CONV_W, D_RNN), const2),
            pl.BlockSpec((1, D_RNN), const2),
            pl.BlockSpec((N_RNN_BLOCKS, RNN_BLOCK, 2 * RNN_BLOCK), const3),
            pl.BlockSpec((1, D_RNN), const2),
            pl.BlockSpec((1, D_RNN), const2),
            pl.BlockSpec((1, D_RNN), const2),
            pl.BlockSpec((len(POOL_WINDOWS), POOL_GROUP, POOL_GROUP), const3),
            pl.BlockSpec((1, D_POOL), const2),
            pl.BlockSpec((tm, tm), const2),
            pl.BlockSpec((1, ATTN_BB, D_X), side),
            pl.BlockSpec((ATTN_BB, N_MEM * SUBLANES, LANES), side),
            pl.BlockSpec((ATTN_BB, N_MEM * SUBLANES, LANES), side),
        ],
        out_specs=[
            pl.BlockSpec((tm, D_MIX), lambda b, l: (b * nl + l, 0)),
            pl.BlockSpec((1, 1, D_RNN), lambda b, l: (b, 0, 0)),
            pl.BlockSpec((1, CONV_W - 1, D_RNN), lambda b, l: (b, 0, 0)),
            pl.BlockSpec((1, POOL_HIST, D_POOL), lambda b, l: (b, 0, 0)),
            pl.BlockSpec((1, ATTN_BB, D_X), side),
        ],
        out_shape=[
            jax.ShapeDtypeStruct((batch * seq, D_MIX), BF16),
            jax.ShapeDtypeStruct((batch, 1, D_RNN), F32),
            jax.ShapeDtypeStruct((batch, CONV_W - 1, D_RNN), F32),
            jax.ShapeDtypeStruct((batch, POOL_HIST, D_POOL), F32),
            jax.ShapeDtypeStruct(sample_q.shape, F32),
        ],
        scratch_shapes=[
            pltpu.VMEM((CONV_W - 1, SUBLANES, D_RNN), F32),
            pltpu.VMEM((POOL_HIST, SUBLANES, D_POOL), F32),
            pltpu.VMEM((1, D_RNN), F32),
            pltpu.VMEM((N_MEM, D_X), BF16),
            pltpu.VMEM((N_MEM, D_X), BF16),
            pltpu.VMEM((tm, D_RNN), F32),
            pltpu.VMEM((tm, D_RNN), F32),
            pltpu.VMEM((tm, D_MIX), BF16),
        ],
        compiler_params=pltpu.CompilerParams(
            dimension_semantics=("arbitrary", "arbitrary"),
            vmem_limit_bytes=VMEM_LIMIT),
        name="prompt_mix",
    )(z, mem_k, mem_v, conv_w, conv_b, wax, b_a, b_x, lam, wpool, pscale, unperm,
      sample_q, cache_k, cache_v)


def _cache_rows(c):
    nb = c.shape[0]
    c = c.reshape(nb, N_MEM, N_XHEADS, XHEAD_DIM // LANES, LANES)
    return c.transpose(0, 1, 3, 2, 4).reshape(nb, N_MEM * SUBLANES, LANES)


def _sample_attn_scores(q_ref, k_ref, bb):
    halves = XHEAD_DIM // LANES
    assert halves * N_XHEADS == SUBLANES
    scores = []
    for j in range(bb):
        qn = jnp.concatenate(
            [q_ref[j:j + 1, (h * halves + t) * LANES:(h * halves + t + 1) * LANES]
             for t in range(halves) for h in range(N_XHEADS)], axis=0)
        scores.append(lax.dot_general(qn.astype(BF16), k_ref[j].astype(BF16),
                                      (((1,), (1,)), ((), ())), preferred_element_type=F32)
                      * (XHEAD_DIM ** -0.5))
    return scores


def _sample_attn_probs(scores):
    r = lax.broadcasted_iota(jnp.int32, (SUBLANES, LANES), 0)
    c = lax.broadcasted_iota(jnp.int32, (SUBLANES, LANES), 1)
    diag = (c % SUBLANES) == r
    first_half = r < N_XHEADS
    nchunk = N_MEM * SUBLANES // LANES
    probs = []
    for s in scores:
        chunks = []
        for ci in range(nchunk):
            sm = jnp.where(diag, s[:, ci * LANES:(ci + 1) * LANES], 0.0)
            other = pltpu.roll(sm, N_XHEADS, 0)
            other = jnp.where(first_half, pltpu.roll(other, LANES - N_XHEADS, 1),
                              pltpu.roll(other, N_XHEADS, 1))
            chunks.append(jnp.where(diag, sm + other, -jnp.inf))
        t_full = jnp.concatenate(chunks, axis=1)
        e = jnp.exp(t_full - jnp.max(t_full, axis=1, keepdims=True))
        probs.append((e / jnp.sum(e, axis=1, keepdims=True)).astype(BF16))
    return probs


def _sample_attn_values(probs, v_ref, o_ref):
    halves = XHEAD_DIM // LANES
    for j, p in enumerate(probs):
        o = jnp.dot(p, v_ref[j].astype(BF16), preferred_element_type=F32)
        for t in range(halves):
            for h in range(N_XHEADS):
                col = (h * halves + t) * LANES
                o_ref[j:j + 1, col:col + LANES] = o[t * N_XHEADS + h:t * N_XHEADS + h + 1, :]


def _prompt_proj_kernel(x_ref, g_ref, w_ref, perm_ref, o_ref, u_ref, up_ref, *, mix_steps):
    j = pl.program_id(1)

    @pl.when(j == 0)
    def _():
        x = x_ref[...]
        u = (x * _rms_scale(x) * g_ref[...]).astype(BF16)
        u_ref[...] = u
        for r0 in range(0, u.shape[0], MIX_TM):
            up_ref[r0:r0 + MIX_TM, :] = jnp.dot(
                perm_ref[...], u[r0:r0 + MIX_TM], preferred_element_type=F32).astype(BF16)

    @pl.when(j < mix_steps)
    def _():
        o_ref[...] = jnp.dot(up_ref[...], w_ref[...], preferred_element_type=F32)

    @pl.when(j >= mix_steps)
    def _():
        o_ref[...] = jnp.dot(u_ref[...], w_ref[...], preferred_element_type=F32)


def _prompt_proj(x, g, w, perm, tm, tn):
    m, k = x.shape
    n = w.shape[1]
    assert (2 * D_MIX) % tn == 0
    return pl.pallas_call(
        functools.partial(_prompt_proj_kernel, mix_steps=2 * D_MIX // tn),
        grid=(m // tm, n // tn),
        in_specs=[
            pl.BlockSpec((tm, k), lambda i, j: (i, 0)),
            pl.BlockSpec((1, k), lambda i, j: (0, 0)),
            pl.BlockSpec((k, tn), lambda i, j: (0, j)),
            pl.BlockSpec(perm.shape, lambda i, j: (0, 0)),
        ],
        out_specs=pl.BlockSpec((tm, tn), lambda i, j: (i, j)),
        out_shape=jax.ShapeDtypeStruct((m, n), F32),
        scratch_shapes=[pltpu.VMEM((tm, k), BF16), pltpu.VMEM((tm, k), BF16)],
        compiler_params=pltpu.CompilerParams(
            dimension_semantics=("arbitrary", "arbitrary"),
            vmem_limit_bytes=BIG_VMEM_LIMIT),
        name="prompt_proj",
    )(x, g, w, perm)


def _sample_mix_kernel(z_ref, attn_ref, conv_ref, h_ref, pool_ref,
                       convw_ref, convb_ref, wax_ref, ba_ref, bx_ref, lam_ref, wpool_ref,
                       pscale_ref, o_ref, newh_ref, newconv_ref, newpool_ref):
    xr = z_ref[:, 0:D_RNN]
    xc = convb_ref[...] + convw_ref[CONV_W - 1:CONV_W, :] * xr
    for k in range(CONV_W - 1):
        xc = xc + convw_ref[k:k + 1, :] * conv_ref[k]
    for k in range(CONV_W - 2):
        newconv_ref[k] = conv_ref[k + 1]
    newconv_ref[CONV_W - 2] = xr

    rate = _decay_rate(lam_ref[...])
    for n in range(N_RNN_BLOCKS):
        c0, c1 = n * RNN_BLOCK, (n + 1) * RNN_BLOCK
        a, b = _rglru_block(xc[:, c0:c1], wax_ref[n], ba_ref[:, c0:c1], bx_ref[:, c0:c1],
                            rate[:, c0:c1])
        h = a * h_ref[:, c0:c1] + b
        newh_ref[:, c0:c1] = h
        o_ref[:, c0:c1] = (h * _silu(z_ref[:, D_RNN + c0:D_RNN + c1])).astype(BF16)

    xp = z_ref[:, 2 * D_RNN:2 * D_RNN + D_POOL]
    for k in range(POOL_HIST - 1):
        newpool_ref[k] = pool_ref[k + 1]
    newpool_ref[POOL_HIST - 1] = xp
    for g, w in enumerate(POOL_WINDOWS):
        c0, c1 = g * POOL_GROUP, (g + 1) * POOL_GROUP
        xg = xp[:, c0:c1]
        tot = xg
        for j in range(1, w):
            tot = tot + pool_ref[POOL_HIST - j, :, c0:c1]
        cnt = float(min(PAST_LEN + 1, w))
        d = tot / cnt - xg
        og = jnp.dot(d.astype(BF16), wpool_ref[g], preferred_element_type=F32)
        gp = z_ref[:, 2 * D_RNN + D_POOL + c0:2 * D_RNN + D_POOL + c1]
        o_ref[:, D_RNN + c0:D_RNN + c1] = (og * pscale_ref[:, c0:c1] * _silu(gp)).astype(BF16)

    gx = z_ref[:, 2 * D_RNN + 2 * D_POOL + D_X:2 * D_MIX]
    o_ref[:, D_RNN + D_POOL:] = (attn_ref[...] * _silu(gx)).astype(BF16)


def _sample_mix(z, attn, conv, h, pool, conv_w, conv_b, wax, b_a, b_x, lam, wpool, pscale, tb):
    nb = z.shape[0]
    zw = 2 * D_MIX
    rows = lambda i: (i, 0)
    const2 = lambda i: (0, 0)
    const3 = lambda i: (0, 0, 0)
    hist = lambda i: (0, i, 0)
    return pl.pallas_call(
        _sample_mix_kernel,
        grid=(nb // tb,),
        in_specs=[
            pl.BlockSpec((tb, zw), rows),
            pl.BlockSpec((tb, D_X), rows),
            pl.BlockSpec((CONV_W - 1, tb, D_RNN), hist),
            pl.BlockSpec((tb, D_RNN), rows),
            pl.BlockSpec((POOL_HIST, tb, D_POOL), hist),
            pl.BlockSpec((CONV_W, D_RNN), const2),
            pl.BlockSpec((1, D_RNN), const2),
            pl.BlockSpec((N_RNN_BLOCKS, RNN_BLOCK, 2 * RNN_BLOCK), const3),
            pl.BlockSpec((1, D_RNN), const2),
            pl.BlockSpec((1, D_RNN), const2),
            pl.BlockSpec((1, D_RNN), const2),
            pl.BlockSpec((len(POOL_WINDOWS), POOL_GROUP, POOL_GROUP), const3),
            pl.BlockSpec((1, D_POOL), const2),
        ],
        out_specs=[
            pl.BlockSpec((tb, D_MIX), rows),
            pl.BlockSpec((tb, D_RNN), rows),
            pl.BlockSpec((CONV_W - 1, tb, D_RNN), hist),
            pl.BlockSpec((POOL_HIST, tb, D_POOL), hist),
        ],
        out_shape=[
            jax.ShapeDtypeStruct((nb, D_MIX), BF16),
            jax.ShapeDtypeStruct((nb, D_RNN), F32),
            jax.ShapeDtypeStruct((CONV_W - 1, nb, D_RNN), F32),
            jax.ShapeDtypeStruct((POOL_HIST, nb, D_POOL), F32),
        ],
        compiler_params=pltpu.CompilerParams(
            dimension_semantics=("arbitrary",),
            vmem_limit_bytes=VMEM_LIMIT),
        name="sample_mix",
    )(z, attn, conv, h, pool, conv_w, conv_b, wax, b_a, b_x, lam, wpool, pscale)


def _branch_out_kernel(o_ref, gates_ref, x_ref, wb_ref, wo_ref, gpost_ref, y_ref):
    merged = None
    for j, (r0, r1) in enumerate(((0, D_RNN), (D_RNN, D_RNN + D_POOL), (D_RNN + D_POOL, D_MIX))):
        yj = jnp.dot(o_ref[:, r0:r1], wb_ref[r0:r1, :], preferred_element_type=F32)
        term = _sigmoid(gates_ref[:, j * D_MODEL:(j + 1) * D_MODEL]) * yj
        merged = term if merged is None else merged + term
    out = jnp.dot(merged.astype(BF16), wo_ref[...], preferred_element_type=F32)
    y_ref[...] = x_ref[...] + (out * gpost_ref[...]) * _rms_scale(out)


def _branch_out(o, z, x, wb, wo, g_post, tm):
    m = x.shape[0]
    gw = N_BRANCH * D_MODEL
    gblk = (2 * D_MIX) // gw
    resident = pl.Buffered(1)
    return pl.pallas_call(
        _branch_out_kernel,
        grid=(m // tm,),
        in_specs=[
            pl.BlockSpec((tm, D_MIX), lambda i: (i, 0)),
            pl.BlockSpec((tm, gw), lambda i: (i, gblk)),
            pl.BlockSpec((tm, D_MODEL), lambda i: (i, 0)),
            pl.BlockSpec((D_MIX, D_MODEL), lambda i: (0, 0), pipeline_mode=resident),
            pl.BlockSpec((D_MODEL, D_MODEL), lambda i: (0, 0), pipeline_mode=resident),
            pl.BlockSpec((1, D_MODEL), lambda i: (0, 0)),
        ],
        out_specs=pl.BlockSpec((tm, D_MODEL), lambda i: (i, 0)),
        out_shape=jax.ShapeDtypeStruct((m, D_MODEL), F32),
        compiler_params=pltpu.CompilerParams(
            dimension_semantics=("arbitrary",),
            vmem_limit_bytes=VMEM_LIMIT),
        name="branch_out",
    )(o, z, x, wb, wo, g_post)


WROWS = 1024
PER_BRANCH = D_RNN // WROWS
assert D_RNN == D_POOL == D_X and D_RNN % WROWS == 0 and D_MODEL % WROWS == 0
N_WB_BLOCKS = N_BRANCH * PER_BRANCH
N_WOUT_BLOCKS = D_MODEL // WROWS


def _branch_out_cast_kernel(o_ref, gates_ref, x_ref, wb_ref, wo_ref, gpost_ref,
                            y_ref, wbb_ref, wob_ref, merged_ref, out_ref):
    s = pl.program_id(0)

    @pl.when(s < N_WB_BLOCKS)
    def _():
        w = wb_ref[...].astype(BF16)
        wbb_ref[...] = w
        term = _sigmoid(gates_ref[...]) * jnp.dot(o_ref[...], w, preferred_element_type=F32)

        @pl.when(s == 0)
        def _():
            merged_ref[...] = term

        @pl.when(s > 0)
        def _():
            merged_ref[...] += term

    for kb in range(N_WOUT_BLOCKS):
        @pl.when(s == N_WB_BLOCKS + kb)
        def _(kb=kb):
            w = wo_ref[...].astype(BF16)
            wob_ref[...] = w
            part = jnp.dot(merged_ref[:, kb * WROWS:(kb + 1) * WROWS].astype(BF16), w,
                           preferred_element_type=F32)
            if kb == 0:
                out_ref[...] = part
            else:
                out_ref[...] += part

    @pl.when(s == N_WB_BLOCKS + N_WOUT_BLOCKS - 1)
    def _():
        out = out_ref[...]
        y_ref[...] = x_ref[...] + out * _rms_scale(out) * gpost_ref[...]


def _branch_out_cast(o, z, x, wb, wo, g_post):
    m = x.shape[0]
    gblk0 = (2 * D_MIX) // D_MODEL
    wb_blk = lambda s: jnp.minimum(s, N_WB_BLOCKS - 1)
    wo_blk = lambda s: jnp.maximum(s - N_WB_BLOCKS, 0)
    return pl.pallas_call(
        _branch_out_cast_kernel,
        grid=(N_WB_BLOCKS + N_WOUT_BLOCKS,),
        in_specs=[
            pl.BlockSpec((m, WROWS), lambda s: (0, wb_blk(s))),
            pl.BlockSpec((m, D_MODEL), lambda s: (0, gblk0 + wb_blk(s) // PER_BRANCH)),
            pl.BlockSpec((m, D_MODEL), lambda s: (0, 0)),
            pl.BlockSpec((WROWS, D_MODEL), lambda s: (wb_blk(s), 0)),
            pl.BlockSpec((WROWS, D_MODEL), lambda s: (wo_blk(s), 0)),
            pl.BlockSpec((1, D_MODEL), lambda s: (0, 0)),
        ],
        out_specs=[
            pl.BlockSpec((m, D_MODEL), lambda s: (0, 0)),
            pl.BlockSpec((WROWS, D_MODEL), lambda s: (wb_blk(s), 0)),
            pl.BlockSpec((WROWS, D_MODEL), lambda s: (wo_blk(s), 0)),
        ],
        out_shape=[
            jax.ShapeDtypeStruct((m, D_MODEL), F32),
            jax.ShapeDtypeStruct(wb.shape, BF16),
            jax.ShapeDtypeStruct(wo.shape, BF16),
        ],
        scratch_shapes=[pltpu.VMEM((m, D_MODEL), F32), pltpu.VMEM((m, D_MODEL), F32)],
        compiler_params=pltpu.CompilerParams(
            dimension_semantics=("arbitrary",),
            vmem_limit_bytes=BIG_VMEM_LIMIT),
        name="branch_out_cast",
    )(o, z, x, wb, wo, g_post)


def kernel(x_prompt, x_sample, mem_prompt, state_rglru_h, state_conv, state_pool, cache_mem_k, cache_mem_v, g_pre, w_in, conv_w, conv_b, w_rg_a, b_rg_a, w_rg_x, b_rg_x, lru_lambda, w_pool, pool_scale, g_mem, w_kv, w_branch, w_out, g_post):
    batch, seq, _ = x_prompt.shape
    nb = x_sample.shape[0]
    depth = g_pre.shape[0]
    assert depth == 1 and x_sample.shape[1] == 1

    l = 0
    row = lambda v: v.reshape(1, -1)
    wax = jnp.concatenate([w_rg_a[l], w_rg_x[l]], axis=-1).astype(BF16)
    wpool = w_pool[l].astype(BF16)
    mix_params = (conv_w[l], row(conv_b[l]), wax, row(b_rg_a[l]), row(b_rg_x[l]),
                  row(lru_lambda[l]), wpool, row(pool_scale[l]))

    xp2 = x_prompt.reshape(batch * seq, D_MODEL)
    xs2 = x_sample.reshape(nb, D_MODEL)
    mem2 = mem_prompt.reshape(batch * N_MEM, D_MODEL)

    z_s, w_in_b, mem_k, mem_v = _sample_proj(xs2, row(g_pre[l]), w_in[l], mem2, row(g_mem[l]),
                                             w_kv[l], tn=SAMPLE_PROJ_TN)
    qoff = 2 * D_RNN + 2 * D_POOL
    q_s = z_s[:, qoff:qoff + D_X].reshape(nb // ATTN_BB, ATTN_BB, D_X)

    mem_k = mem_k.reshape(batch, N_MEM, D_X)
    mem_v = mem_v.reshape(batch, N_MEM, D_X)

    perm = _chunk_interleave()
    z_p = _prompt_proj(xp2, row(g_pre[l]), w_in_b, perm, tm=PROJ_TM, tn=PROJ_TN)
    o_p, h_p, c_p, p_p, attn_s = _prompt_mix(
        z_p, mem_k, mem_v, *mix_params, perm.T, q_s, _cache_rows(cache_mem_k[l]),
        _cache_rows(cache_mem_v[l]), batch=batch, seq=seq, tm=MIX_TM)
    attn_s = attn_s.reshape(nb, D_X)

    o_s, h_s, c_s, p_s = _sample_mix(
        z_s, attn_s, state_conv[l].transpose(1, 0, 2), state_rglru_h[l],
        state_pool[l].transpose(1, 0, 2), *mix_params, tb=SAMPLE_MIX_TB)
    y_s, w_br_b, w_out_b = _branch_out_cast(o_s, z_s, xs2, w_branch[l], w_out[l], row(g_post[l]))

    y_p = _branch_out(o_p, z_p, xp2, w_br_b, w_out_b, row(g_post[l]), tm=BRANCH_TM)

    return (
        y_p.reshape(batch, seq, D_MODEL),
        y_s.reshape(nb, 1, D_MODEL),
        h_p.reshape(1, batch, D_RNN),
        c_p.reshape(1, batch, CONV_W - 1, D_RNN),
        p_p.reshape(1, batch, POOL_HIST, D_POOL),
        mem_k.reshape(1, batch, N_MEM, N_XHEADS, XHEAD_DIM),
        mem_v.reshape(1, batch, N_MEM, N_XHEADS, XHEAD_DIM),
        h_s.reshape(1, nb, D_RNN),
        c_s.transpose(1, 0, 2)[None],
        p_s.transpose(1, 0, 2)[None],
    )
```

```python
import functools

import jax
import jax.numpy as jnp
from jax import lax
from jax.experimental import pallas as pl
from jax.experimental.pallas import tpu as pltpu

D_MODEL = 2048
PAST_LEN = 16384
D_RNN = 1024
N_RNN_BLOCKS = 8
RNN_BLOCK = D_RNN // N_RNN_BLOCKS
CONV_W = 4
LRU_C = 8.0
D_POOL = 1024
POOL_WINDOWS = (2, 4, 8, 16)
POOL_GROUP = D_POOL // len(POOL_WINDOWS)
POOL_HIST = max(POOL_WINDOWS) - 1
N_MEM = 256
N_XHEADS = 4
XHEAD_DIM = 256
D_X = N_XHEADS * XHEAD_DIM
N_BRANCH = 3
D_MIX = D_RNN + D_POOL + D_X
D_IN = 2 * D_MIX + N_BRANCH * D_MODEL
EPS = 1e-6

SUBLANES = 8
LANES = 128
VMEM_LIMIT = 56 * 1024 * 1024
BIG_VMEM_LIMIT = 60 * 1024 * 1024
MIX_TM = 256
BRANCH_TM = 256
PROJ_TM, PROJ_TN = 1024, 2048
SAMPLE_PROJ_TN = 768
ATTN_BB = 4
CACHE_RING = 3
SAMPLE_MIX_TB = 64

BF16 = jnp.bfloat16
F32 = jnp.float32

NEG_LOG2_E = -1.4426950408889634


def _sigmoid(x):
    return 1.0 / (1.0 + jnp.exp2(x * NEG_LOG2_E))


def _silu(x):
    return x * _sigmoid(x)


def _softplus(x):
    return jnp.maximum(x, 0.0) + jnp.log1p(jnp.exp(-jnp.abs(x)))


def _rms_scale(x):
    return lax.rsqrt(jnp.mean(x * x, axis=-1, keepdims=True) + EPS)


def _chunk_interleave():
    nrow = MIX_TM // SUBLANES
    p = jnp.arange(MIX_TM)
    token = (p % SUBLANES) * nrow + p // SUBLANES
    return (token[:, None] == jnp.arange(MIX_TM)[None, :]).astype(BF16)


def _sample_proj_kernel(x_ref, g_ref, w_ref, mem_ref, gm_ref, wkv_ref,
                        o_ref, wb_ref, k_ref, v_ref, u_ref, um_ref, *, k_steps):
    j = pl.program_id(0)

    @pl.when(j == 0)
    def _():
        x = x_ref[...]
        u_ref[...] = (x * _rms_scale(x) * g_ref[...]).astype(BF16)
        mem = mem_ref[...]
        um_ref[...] = (mem * _rms_scale(mem) * gm_ref[...]).astype(BF16)

    w = w_ref[...].astype(BF16)
    wb_ref[...] = w
    o_ref[...] = jnp.dot(u_ref[...], w, preferred_element_type=F32)

    kv = jnp.dot(um_ref[...], wkv_ref[...].astype(BF16), preferred_element_type=F32)

    @pl.when(j < k_steps)
    def _():
        k_ref[...] = kv

    @pl.when(j >= k_steps)
    def _():
        v_ref[...] = kv


def _sample_proj(x, g, w, mem, g_mem, w_kv, tn):
    m, k = x.shape
    n = w.shape[1]
    steps = n // tn
    mrows = mem.shape[0]
    kv_tn = 2 * D_X // steps
    assert D_X % kv_tn == 0 and kv_tn % LANES == 0
    k_steps = D_X // kv_tn
    return pl.pallas_call(
        functools.partial(_sample_proj_kernel, k_steps=k_steps),
        grid=(steps,),
        in_specs=[
            pl.BlockSpec((m, k), lambda j: (0, 0)),
            pl.BlockSpec((1, k), lambda j: (0, 0)),
            pl.BlockSpec((k, tn), lambda j: (0, j)),
            pl.BlockSpec((mrows, k), lambda j: (0, 0), pipeline_mode=pl.Buffered(1)),
            pl.BlockSpec((1, k), lambda j: (0, 0)),
            pl.BlockSpec((k, kv_tn), lambda j: (0, j)),
        ],
        out_specs=[
            pl.BlockSpec((m, tn), lambda j: (0, j)),
            pl.BlockSpec((k, tn), lambda j: (0, j)),
            pl.BlockSpec((mrows, kv_tn), lambda j: (0, jnp.minimum(j, k_steps - 1))),
            pl.BlockSpec((mrows, kv_tn), lambda j: (0, jnp.maximum(j - k_steps, 0))),
        ],
        out_shape=[
            jax.ShapeDtypeStruct((m, n), F32),
            jax.ShapeDtypeStruct((k, n), BF16),
            jax.ShapeDtypeStruct((mrows, D_X), F32),
            jax.ShapeDtypeStruct((mrows, D_X), F32),
        ],
        scratch_shapes=[pltpu.VMEM((m, k), BF16), pltpu.VMEM((mrows, k), BF16)],
        compiler_params=pltpu.CompilerParams(
            dimension_semantics=("arbitrary",),
            vmem_limit_bytes=VMEM_LIMIT),
        name="sample_proj",
    )(x, g, w, mem, g_mem, w_kv)


def _decay_rate(lam):
    return _softplus(-lam) * (LRU_C * NEG_LOG2_E)


def _rglru_block(xc, wax, ba, bx, rate):
    ri = jnp.dot(xc.astype(BF16), wax, preferred_element_type=F32)
    r = _sigmoid(ri[:, :RNN_BLOCK] + ba)
    i = _sigmoid(ri[:, RNN_BLOCK:] + bx)
    a = jnp.exp2(r * rate)
    one_m = 1.0 - a * a
    mult = jnp.where(one_m > 0.0, one_m * lax.rsqrt(one_m), 0.0)
    return a, mult * i * xc


def _prompt_mix_kernel(z_ref, k_ref, v_ref, convw_ref, convb_ref, wax_ref, ba_ref, bx_ref,
                       lam_ref, wpool_ref, pscale_ref, unperm_ref, sq_ref, sk_hbm, sv_hbm,
                       o_ref, newh_ref, newconv_ref, newpool_ref, sattn_ref,
                       conv_carry, pool_carry, h_carry, kb_ref, vb_ref, ac_scr, hl_scr, op_scr,
                       sk_ring, sv_ring, ring_sem, *, tm):
    l = pl.program_id(1)
    last = pl.num_programs(1) - 1
    nrow = tm // SUBLANES

    @pl.when(l == 0)
    def _():
        conv_carry[...] = jnp.zeros(conv_carry.shape, F32)
        pool_carry[...] = jnp.zeros(pool_carry.shape, F32)
        h_carry[...] = jnp.zeros(h_carry.shape, F32)
        kb_ref[...] = k_ref[0].astype(BF16)
        vb_ref[...] = v_ref[0].astype(BF16)

    step = pl.program_id(0) * pl.num_programs(1) + l
    nsteps = pl.num_programs(0) * pl.num_programs(1)

    def cache_copies(blk, slot):
        rows = pl.ds(blk * ATTN_BB, ATTN_BB)
        return (pltpu.make_async_copy(sk_hbm.at[rows], sk_ring.at[slot], ring_sem.at[slot, 0]),
                pltpu.make_async_copy(sv_hbm.at[rows], sv_ring.at[slot], ring_sem.at[slot, 1]))

    @pl.when(step == 0)
    def _():
        for first in range(CACHE_RING - 1):
            for cp in cache_copies(first, first):
                cp.start()

    ahead = step + (CACHE_RING - 1)

    @pl.when(ahead < nsteps)
    def _():
        for cp in cache_copies(ahead, ahead % CACHE_RING):
            cp.start()

    slot = step % CACHE_RING
    for cp in cache_copies(step, slot):
        cp.wait()
    sk_ref = sk_ring.at[slot]
    sv_ref = sv_ring.at[slot]
    side_scores = _sample_attn_scores(sq_ref.at[0], sk_ref, ATTN_BB)

    chunk_id = lax.broadcasted_iota(jnp.int32, (SUBLANES, LANES), 0)
    first_chunk = chunk_id == 0

    def load_groups(col, width=LANES):
        return [z_ref[r * SUBLANES:(r + 1) * SUBLANES, col:col + width] for r in range(nrow)]

    def put(col, width, val):
        op_scr[:, col:col + width] = val.astype(BF16)

    def store_groups(col, rows, width=LANES):
        put(col, width, jnp.concatenate(rows, axis=0))

    def history(tail_group, carry_ref, j, c0):
        tail = pltpu.roll(tail_group, 1, 0)
        prev = jnp.where(first_chunk, carry_ref[j - 1, :, c0:c0 + LANES], tail)
        carry_ref[j - 1, :, c0:c0 + LANES] = tail
        return prev

    rate = _decay_rate(lam_ref[...])
    for n in range(N_RNN_BLOCKS):
        c0, c1 = n * RNN_BLOCK, (n + 1) * RNN_BLOCK
        xs = load_groups(c0)
        ext = [history(xs[nrow - j], conv_carry, j, c0) for j in range(CONV_W - 1, 0, -1)] + xs
        cw = [jnp.broadcast_to(convw_ref[k:k + 1, c0:c1], (SUBLANES, LANES)) for k in range(CONV_W)]
        cb = jnp.broadcast_to(convb_ref[:, c0:c1], (SUBLANES, LANES))
        xc = []
        for r in range(nrow):
            acc = cb + cw[0] * ext[r]
            for k in range(1, CONV_W):
                acc = acc + cw[k] * ext[r + k]
            xc.append(acc)
        a, b = _rglru_block(jnp.concatenate(xc, axis=0), wax_ref[n], ba_ref[:, c0:c1],
                            bx_ref[:, c0:c1], rate[:, c0:c1])
        ac_scr[:, c0:c1] = a
        hl_scr[:, c0:c1] = b

    side_probs = _sample_attn_probs(side_scores)

    acc_a = ac_scr[0:SUBLANES, :]
    acc_h = hl_scr[0:SUBLANES, :]
    for r in range(1, nrow):
        rows = slice(r * SUBLANES, (r + 1) * SUBLANES)
        ar = ac_scr[rows, :]
        acc_h = ar * acc_h + hl_scr[rows, :]
        acc_a = ar * acc_a
        ac_scr[rows, :] = acc_a
        hl_scr[rows, :] = acc_h
    h_in = h_carry[...]
    entering = []
    for c in range(SUBLANES):
        entering.append(h_in)
        h_in = acc_a[c:c + 1] * h_in + acc_h[c:c + 1]
    h_carry[...] = h_in
    h_enter = jnp.concatenate(entering, axis=0)
    for n in range(N_RNN_BLOCKS):
        c0, c1 = n * RNN_BLOCK, (n + 1) * RNN_BLOCK
        gr = load_groups(D_RNN + c0)
        store_groups(c0, [(hl_scr[r * SUBLANES:(r + 1) * SUBLANES, c0:c1]
                           + ac_scr[r * SUBLANES:(r + 1) * SUBLANES, c0:c1] * h_enter[:, c0:c1])
                          * _silu(gr[r]) for r in range(nrow)])

    _sample_attn_values(side_probs, sv_ref, sattn_ref.at[0])

    pcol = 2 * D_RNN
    blocks = [(w, c0) for g, w in enumerate(POOL_WINDOWS)
              for c0 in range(g * POOL_GROUP, (g + 1) * POOL_GROUP, LANES)]

    def group(c0, r):
        return z_ref[r * SUBLANES:(r + 1) * SUBLANES, pcol + c0:pcol + c0 + LANES]

    def mean_minus_token(tot, w, c0, r):
        if r < w - 1:
            pos1 = l * tm + chunk_id * nrow + (r + 1)
            mean = tot / jnp.minimum(pos1, w).astype(F32)
        else:
            mean = tot * (1.0 / w)
        return mean - group(c0, r)

    hist, tot = {}, {}
    for w, c0 in blocks:
        hist[c0] = [history(group(c0, nrow - j), pool_carry, j, c0) for j in range(1, w)]
        t = group(c0, 0)
        for h in hist[c0]:
            t = t + h
        tot[c0] = t
        hl_scr[0:SUBLANES, c0:c0 + LANES] = mean_minus_token(t, w, c0, 0)
    for r in range(1, nrow):
        for w, c0 in blocks:
            leaving = group(c0, r - w) if r >= w else hist[c0][w - r - 1]
            tot[c0] = tot[c0] + (group(c0, r) - leaving)
            hl_scr[r * SUBLANES:(r + 1) * SUBLANES, c0:c0 + LANES] = mean_minus_token(
                tot[c0], w, c0, r)
    for g, w in enumerate(POOL_WINDOWS):
        c0, c1 = g * POOL_GROUP, (g + 1) * POOL_GROUP
        og = jnp.dot(hl_scr[:, c0:c1].astype(BF16), wpool_ref[g], preferred_element_type=F32)
        gp = z_ref[:, pcol + D_POOL + c0:pcol + D_POOL + c1]
        put(D_RNN + c0, POOL_GROUP, og * pscale_ref[:, c0:c1] * _silu(gp))

    qoff = 2 * D_RNN + 2 * D_POOL
    for hd in range(N_XHEADS):
        c0, c1 = hd * XHEAD_DIM, (hd + 1) * XHEAD_DIM
        q = z_ref[:, qoff + c0:qoff + c1].astype(BF16)
        s = lax.dot_general(q, kb_ref[:, c0:c1], (((1,), (1,)), ((), ())),
                            preferred_element_type=F32) * (XHEAD_DIM ** -0.5)
        p = jnp.exp(s - jnp.max(s, axis=-1, keepdims=True))
        p = p / jnp.sum(p, axis=-1, keepdims=True)
        ox = jnp.dot(p.astype(BF16), vb_ref[:, c0:c1], preferred_element_type=F32)
        gx = z_ref[:, qoff + D_X + c0:qoff + D_X + c1]
        put(D_RNN + D_POOL + c0, XHEAD_DIM, ox * _silu(gx))

    o_ref[...] = jnp.dot(unperm_ref[...], op_scr[...], preferred_element_type=F32).astype(BF16)

    @pl.when(l == last)
    def _():
        newh_ref[0] = h_carry[...]
        tail_row = lambda j: (nrow - j) * SUBLANES + SUBLANES - 1
        for j in range(1, CONV_W):
            newconv_ref[0, CONV_W - 1 - j:CONV_W - j, :] = z_ref[tail_row(j):tail_row(j) + 1, 0:D_RNN]
        for j in range(1, POOL_HIST + 1):
            newpool_ref[0, POOL_HIST - j:POOL_HIST - j + 1, :] = (
                z_ref[tail_row(j):tail_row(j) + 1, pcol:pcol + D_POOL])


def _prompt_mix(z, mem_k, mem_v, conv_w, conv_b, wax, b_a, b_x, lam, wpool, pscale, unperm,
                sample_q, cache_k, cache_v, batch, seq, tm):
    nl = seq // tm
    assert sample_q.shape[0] == batch * nl
    side = lambda b, l: (b * nl + l, 0, 0)
    zw = 2 * D_MIX
    const2 = lambda b, l: (0, 0)
    const3 = lambda b, l: (0, 0, 0)
    kern = functools.partial(_prompt_mix_kernel, tm=tm)
    return pl.pallas_call(
        kern,
        grid=(batch, nl),
        in_specs=[
            pl.BlockSpec((tm, zw), lambda b, l: (b * nl + l, 0)),
            pl.BlockSpec((1, N_MEM, D_X), lambda b, l: (b, 0, 0)),
            pl.BlockSpec((1, N_MEM, D_X), lambda b, l: (b, 0, 0)),
            pl.BlockSpec((CONV_W, D_RNN), const2),
            pl.BlockSpec((1, D_RNN), const2),
            pl.BlockSpec((N_RNN_BLOCKS, RNN_BLOCK, 2 * RNN_BLOCK), const3),
            pl.BlockSpec((1, D_RNN), const2),
            pl.BlockSpec((1, D_RNN), const2),
            pl.BlockSpec((1, D_RNN), const2),
            pl.BlockSpec((len(POOL_WINDOWS), POOL_GROUP, POOL_GROUP), const3),
            pl.BlockSpec((1, D_POOL), const2),
            pl.BlockSpec((tm, tm), const2),
            pl.BlockSpec((1, ATTN_BB, D_X), side),
            pl.BlockSpec(memory_space=pl.ANY),
            pl.BlockSpec(memory_space=pl.ANY),
        ],
        out_specs=[
            pl.BlockSpec((tm, D_MIX), lambda b, l: (b * nl + l, 0)),
            pl.BlockSpec((1, 1, D_RNN), lambda b, l: (b, 0, 0)),
            pl.BlockSpec((1, CONV_W - 1, D_RNN), lambda b, l: (b, 0, 0)),
            pl.BlockSpec((1, POOL_HIST, D_POOL), lambda b, l: (b, 0, 0)),
            pl.BlockSpec((1, ATTN_BB, D_X), side),
        ],
        out_shape=[
            jax.ShapeDtypeStruct((batch * seq, D_MIX), BF16),
            jax.ShapeDtypeStruct((batch, 1, D_RNN), F32),
            jax.ShapeDtypeStruct((batch, CONV_W - 1, D_RNN), F32),
            jax.ShapeDtypeStruct((batch, POOL_HIST, D_POOL), F32),
            jax.ShapeDtypeStruct(sample_q.shape, F32),
        ],
        scratch_shapes=[
            pltpu.VMEM((CONV_W - 1, SUBLANES, D_RNN), F32),
            pltpu.VMEM((POOL_HIST, SUBLANES, D_POOL), F32),
            pltpu.VMEM((1, D_RNN), F32),
            pltpu.VMEM((N_MEM, D_X), BF16),
            pltpu.VMEM((N_MEM, D_X), BF16),
            pltpu.VMEM((tm, D_RNN), F32),
            pltpu.VMEM((tm, D_RNN), F32),
            pltpu.VMEM((tm, D_MIX), BF16),
            pltpu.VMEM((CACHE_RING, ATTN_BB, N_MEM * SUBLANES, LANES), F32),
            pltpu.VMEM((CACHE_RING, ATTN_BB, N_MEM * SUBLANES, LANES), F32),
            pltpu.SemaphoreType.DMA((CACHE_RING, 2)),
        ],
        compiler_params=pltpu.CompilerParams(
            dimension_semantics=("arbitrary", "arbitrary"),
            vmem_limit_bytes=VMEM_LIMIT),
        name="prompt_mix",
    )(z, mem_k, mem_v, conv_w, conv_b, wax, b_a, b_x, lam, wpool, pscale, unperm,
      sample_q, cache_k, cache_v)


def _cache_rows(c):
    nb = c.shape[0]
    c = c.reshape(nb, N_MEM, N_XHEADS, XHEAD_DIM // LANES, LANES)
    return c.transpose(0, 1, 3, 2, 4).reshape(nb, N_MEM * SUBLANES, LANES)


def _sample_attn_scores(q_ref, k_ref, bb):
    halves = XHEAD_DIM // LANES
    assert halves * N_XHEADS == SUBLANES
    scores = []
    for j in range(bb):
        qn = jnp.concatenate(
            [q_ref[j:j + 1, (h * halves + t) * LANES:(h * halves + t + 1) * LANES]
             for t in range(halves) for h in range(N_XHEADS)], axis=0)
        scores.append(lax.dot_general(qn.astype(BF16), k_ref[j].astype(BF16),
                                      (((1,), (1,)), ((), ())), preferred_element_type=F32)
                      * (XHEAD_DIM ** -0.5))
    return scores


def _sample_attn_probs(scores):
    r = lax.broadcasted_iota(jnp.int32, (SUBLANES, LANES), 0)
    c = lax.broadcasted_iota(jnp.int32, (SUBLANES, LANES), 1)
    diag = (c % SUBLANES) == r
    first_half = r < N_XHEADS
    nchunk = N_MEM * SUBLANES // LANES
    probs = []
    for s in scores:
        chunks = []
        for ci in range(nchunk):
            sm = jnp.where(diag, s[:, ci * LANES:(ci + 1) * LANES], 0.0)
            other = pltpu.roll(sm, N_XHEADS, 0)
            other = jnp.where(first_half, pltpu.roll(other, LANES - N_XHEADS, 1),
                              pltpu.roll(other, N_XHEADS, 1))
            chunks.append(jnp.where(diag, sm + other, -jnp.inf))
        t_full = jnp.concatenate(chunks, axis=1)
        e = jnp.exp(t_full - jnp.max(t_full, axis=1, keepdims=True))
        probs.append((e / jnp.sum(e, axis=1, keepdims=True)).astype(BF16))
    return probs


def _sample_attn_values(probs, v_ref, o_ref):
    halves = XHEAD_DIM // LANES
    for j, p in enumerate(probs):
        o = jnp.dot(p, v_ref[j].astype(BF16), preferred_element_type=F32)
        for t in range(halves):
            for h in range(N_XHEADS):
                col = (h * halves + t) * LANES
                o_ref[j:j + 1, col:col + LANES] = o[t * N_XHEADS + h:t * N_XHEADS + h + 1, :]


def _prompt_proj_kernel(x_ref, g_ref, w_ref, perm_ref, o_ref, u_ref, up_ref, *, mix_steps):
    j = pl.program_id(1)

    @pl.when(j == 0)
    def _():
        x = x_ref[...]
        u = (x * _rms_scale(x) * g_ref[...]).astype(BF16)
        u_ref[...] = u
        for r0 in range(0, u.shape[0], MIX_TM):
            up_ref[r0:r0 + MIX_TM, :] = jnp.dot(
                perm_ref[...], u[r0:r0 + MIX_TM], preferred_element_type=F32).astype(BF16)

    @pl.when(j < mix_steps)
    def _():
        o_ref[...] = jnp.dot(up_ref[...], w_ref[...], preferred_element_type=F32)

    @pl.when(j >= mix_steps)
    def _():
        o_ref[...] = jnp.dot(u_ref[...], w_ref[...], preferred_element_type=F32)


def _prompt_proj(x, g, w, perm, tm, tn):
    m, k = x.shape
    n = w.shape[1]
    assert (2 * D_MIX) % tn == 0
    return pl.pallas_call(
        functools.partial(_prompt_proj_kernel, mix_steps=2 * D_MIX // tn),
        grid=(m // tm, n // tn),
        in_specs=[
            pl.BlockSpec((tm, k), lambda i, j: (i, 0)),
            pl.BlockSpec((1, k), lambda i, j: (0, 0)),
            pl.BlockSpec((k, tn), lambda i, j: (0, j)),
            pl.BlockSpec(perm.shape, lambda i, j: (0, 0)),
        ],
        out_specs=pl.BlockSpec((tm, tn), lambda i, j: (i, j)),
        out_shape=jax.ShapeDtypeStruct((m, n), F32),
        scratch_shapes=[pltpu.VMEM((tm, k), BF16), pltpu.VMEM((tm, k), BF16)],
        compiler_params=pltpu.CompilerParams(
            dimension_semantics=("arbitrary", "arbitrary"),
            vmem_limit_bytes=BIG_VMEM_LIMIT),
        name="prompt_proj",
    )(x, g, w, perm)


def _sample_mix_kernel(z_ref, attn_ref, conv_ref, h_ref, pool_ref,
                       convw_ref, convb_ref, wax_ref, ba_ref, bx_ref, lam_ref, wpool_ref,
                       pscale_ref, o_ref, newh_ref, newconv_ref, newpool_ref):
    xr = z_ref[:, 0:D_RNN]
    xc = convb_ref[...] + convw_ref[CONV_W - 1:CONV_W, :] * xr
    for k in range(CONV_W - 1):
        xc = xc + convw_ref[k:k + 1, :] * conv_ref[k]
    for k in range(CONV_W - 2):
        newconv_ref[k] = conv_ref[k + 1]
    newconv_ref[CONV_W - 2] = xr

    rate = _decay_rate(lam_ref[...])
    for n in range(N_RNN_BLOCKS):
        c0, c1 = n * RNN_BLOCK, (n + 1) * RNN_BLOCK
        a, b = _rglru_block(xc[:, c0:c1], wax_ref[n], ba_ref[:, c0:c1], bx_ref[:, c0:c1],
                            rate[:, c0:c1])
        h = a * h_ref[:, c0:c1] + b
        newh_ref[:, c0:c1] = h
        o_ref[:, c0:c1] = (h * _silu(z_ref[:, D_RNN + c0:D_RNN + c1])).astype(BF16)

    xp = z_ref[:, 2 * D_RNN:2 * D_RNN + D_POOL]
    for k in range(POOL_HIST - 1):
        newpool_ref[k] = pool_ref[k + 1]
    newpool_ref[POOL_HIST - 1] = xp
    for g, w in enumerate(POOL_WINDOWS):
        c0, c1 = g * POOL_GROUP, (g + 1) * POOL_GROUP
        xg = xp[:, c0:c1]
        tot = xg
        for j in range(1, w):
            tot = tot + pool_ref[POOL_HIST - j, :, c0:c1]
        cnt = float(min(PAST_LEN + 1, w))
        d = tot / cnt - xg
        og = jnp.dot(d.astype(BF16), wpool_ref[g], preferred_element_type=F32)
        gp = z_ref[:, 2 * D_RNN + D_POOL + c0:2 * D_RNN + D_POOL + c1]
        o_ref[:, D_RNN + c0:D_RNN + c1] = (og * pscale_ref[:, c0:c1] * _silu(gp)).astype(BF16)

    gx = z_ref[:, 2 * D_RNN + 2 * D_POOL + D_X:2 * D_MIX]
    o_ref[:, D_RNN + D_POOL:] = (attn_ref[...] * _silu(gx)).astype(BF16)


def _sample_mix(z, attn, conv, h, pool, conv_w, conv_b, wax, b_a, b_x, lam, wpool, pscale, tb):
    nb = z.shape[0]
    zw = 2 * D_MIX
    rows = lambda i: (i, 0)
    const2 = lambda i: (0, 0)
    const3 = lambda i: (0, 0, 0)
    hist = lambda i: (0, i, 0)
    return pl.pallas_call(
        _sample_mix_kernel,
        grid=(nb // tb,),
        in_specs=[
            pl.BlockSpec((tb, zw), rows),
            pl.BlockSpec((tb, D_X), rows),
            pl.BlockSpec((CONV_W - 1, tb, D_RNN), hist),
            pl.BlockSpec((tb, D_RNN), rows),
            pl.BlockSpec((POOL_HIST, tb, D_POOL), hist),
            pl.BlockSpec((CONV_W, D_RNN), const2),
            pl.BlockSpec((1, D_RNN), const2),
            pl.BlockSpec((N_RNN_BLOCKS, RNN_BLOCK, 2 * RNN_BLOCK), const3),
            pl.BlockSpec((1, D_RNN), const2),
            pl.BlockSpec((1, D_RNN), const2),
            pl.BlockSpec((1, D_RNN), const2),
            pl.BlockSpec((len(POOL_WINDOWS), POOL_GROUP, POOL_GROUP), const3),
            pl.BlockSpec((1, D_POOL), const2),
        ],
        out_specs=[
            pl.BlockSpec((tb, D_MIX), rows),
            pl.BlockSpec((tb, D_RNN), rows),
            pl.BlockSpec((CONV_W - 1, tb, D_RNN), hist),
            pl.BlockSpec((POOL_HIST, tb, D_POOL), hist),
        ],
        out_shape=[
            jax.ShapeDtypeStruct((nb, D_MIX), BF16),
            jax.ShapeDtypeStruct((nb, D_RNN), F32),
            jax.ShapeDtypeStruct((CONV_W - 1, nb, D_RNN), F32),
            jax.ShapeDtypeStruct((POOL_HIST, nb, D_POOL), F32),
        ],
        compiler_params=pltpu.CompilerParams(
            dimension_semantics=("arbitrary",),
            vmem_limit_bytes=VMEM_LIMIT),
        name="sample_mix",
    )(z, attn, conv, h, pool, conv_w, conv_b, wax, b_a, b_x, lam, wpool, pscale)


def _branch_out_kernel(o_ref, gates_ref, x_ref, wb_ref, wo_ref, gpost_ref, y_ref):
    merged = None
    for j, (r0, r1) in enumerate(((0, D_RNN), (D_RNN, D_RNN + D_POOL), (D_RNN + D_POOL, D_MIX))):
        yj = jnp.dot(o_ref[:, r0:r1], wb_ref[r0:r1, :], preferred_element_type=F32)
        term = _sigmoid(gates_ref[:, j * D_MODEL:(j + 1) * D_MODEL]) * yj
        merged = term if merged is None else merged + term
    out = jnp.dot(merged.astype(BF16), wo_ref[...], preferred_element_type=F32)
    y_ref[...] = x_ref[...] + (out * gpost_ref[...]) * _rms_scale(out)


def _branch_out(o, z, x, wb, wo, g_post, tm):
    m = x.shape[0]
    gw = N_BRANCH * D_MODEL
    gblk = (2 * D_MIX) // gw
    resident = pl.Buffered(1)
    return pl.pallas_call(
        _branch_out_kernel,
        grid=(m // tm,),
        in_specs=[
            pl.BlockSpec((tm, D_MIX), lambda i: (i, 0)),
            pl.BlockSpec((tm, gw), lambda i: (i, gblk)),
            pl.BlockSpec((tm, D_MODEL), lambda i: (i, 0)),
            pl.BlockSpec((D_MIX, D_MODEL), lambda i: (0, 0), pipeline_mode=resident),
            pl.BlockSpec((D_MODEL, D_MODEL), lambda i: (0, 0), pipeline_mode=resident),
            pl.BlockSpec((1, D_MODEL), lambda i: (0, 0)),
        ],
        out_specs=pl.BlockSpec((tm, D_MODEL), lambda i: (i, 0)),
        out_shape=jax.ShapeDtypeStruct((m, D_MODEL), F32),
        compiler_params=pltpu.CompilerParams(
            dimension_semantics=("arbitrary",),
            vmem_limit_bytes=VMEM_LIMIT),
        name="branch_out",
    )(o, z, x, wb, wo, g_post)


WROWS = 1024
PER_BRANCH = D_RNN // WROWS
assert D_RNN == D_POOL == D_X and D_RNN % WROWS == 0 and D_MODEL % WROWS == 0
N_WB_BLOCKS = N_BRANCH * PER_BRANCH
N_WOUT_BLOCKS = D_MODEL // WROWS


def _branch_out_cast_kernel(o_ref, gates_ref, x_ref, wb_ref, wo_ref, gpost_ref,
                            y_ref, wbb_ref, wob_ref, merged_ref, out_ref):
    s = pl.program_id(0)

    @pl.when(s < N_WB_BLOCKS)
    def _():
        w = wb_ref[...].astype(BF16)
        wbb_ref[...] = w
        term = _sigmoid(gates_ref[...]) * jnp.dot(o_ref[...], w, preferred_element_type=F32)

        @pl.when(s == 0)
        def _():
            merged_ref[...] = term

        @pl.when(s > 0)
        def _():
            merged_ref[...] += term

    for kb in range(N_WOUT_BLOCKS):
        @pl.when(s == N_WB_BLOCKS + kb)
        def _(kb=kb):
            w = wo_ref[...].astype(BF16)
            wob_ref[...] = w
            part = jnp.dot(merged_ref[:, kb * WROWS:(kb + 1) * WROWS].astype(BF16), w,
                           preferred_element_type=F32)
            if kb == 0:
                out_ref[...] = part
            else:
                out_ref[...] += part

    @pl.when(s == N_WB_BLOCKS + N_WOUT_BLOCKS - 1)
    def _():
        out = out_ref[...]
        y_ref[...] = x_ref[...] + out * _rms_scale(out) * gpost_ref[...]


def _branch_out_cast(o, z, x, wb, wo, g_post):
    m = x.shape[0]
    gblk0 = (2 * D_MIX) // D_MODEL
    wb_blk = lambda s: jnp.minimum(s, N_WB_BLOCKS - 1)
    wo_blk = lambda s: jnp.maximum(s - N_WB_BLOCKS, 0)
    return pl.pallas_call(
        _branch_out_cast_kernel,
        grid=(N_WB_BLOCKS + N_WOUT_BLOCKS,),
        in_specs=[
            pl.BlockSpec((m, WROWS), lambda s: (0, wb_blk(s))),
            pl.BlockSpec((m, D_MODEL), lambda s: (0, gblk0 + wb_blk(s) // PER_BRANCH)),
            pl.BlockSpec((m, D_MODEL), lambda s: (0, 0)),
            pl.BlockSpec((WROWS, D_MODEL), lambda s: (wb_blk(s), 0)),
            pl.BlockSpec((WROWS, D_MODEL), lambda s: (wo_blk(s), 0)),
            pl.BlockSpec((1, D_MODEL), lambda s: (0, 0)),
        ],
        out_specs=[
            pl.BlockSpec((m, D_MODEL), lambda s: (0, 0)),
            pl.BlockSpec((WROWS, D_MODEL), lambda s: (wb_blk(s), 0)),
            pl.BlockSpec((WROWS, D_MODEL), lambda s: (wo_blk(s), 0)),
        ],
        out_shape=[
            jax.ShapeDtypeStruct((m, D_MODEL), F32),
            jax.ShapeDtypeStruct(wb.shape, BF16),
            jax.ShapeDtypeStruct(wo.shape, BF16),
        ],
        scratch_shapes=[pltpu.VMEM((m, D_MODEL), F32), pltpu.VMEM((m, D_MODEL), F32)],
        compiler_params=pltpu.CompilerParams(
            dimension_semantics=("arbitrary",),
            vmem_limit_bytes=BIG_VMEM_LIMIT),
        name="branch_out_cast",
    )(o, z, x, wb, wo, g_post)


def kernel(x_prompt, x_sample, mem_prompt, state_rglru_h, state_conv, state_pool, cache_mem_k, cache_mem_v, g_pre, w_in, conv_w, conv_b, w_rg_a, b_rg_a, w_rg_x, b_rg_x, lru_lambda, w_pool, pool_scale, g_mem, w_kv, w_branch, w_out, g_post):
    batch, seq, _ = x_prompt.shape
    nb = x_sample.shape[0]
    depth = g_pre.shape[0]
    assert depth == 1 and x_sample.shape[1] == 1

    l = 0
    row = lambda v: v.reshape(1, -1)
    wax = jnp.concatenate([w_rg_a[l], w_rg_x[l]], axis=-1).astype(BF16)
    wpool = w_pool[l].astype(BF16)
    mix_params = (conv_w[l], row(conv_b[l]), wax, row(b_rg_a[l]), row(b_rg_x[l]),
                  row(lru_lambda[l]), wpool, row(pool_scale[l]))

    xp2 = x_prompt.reshape(batch * seq, D_MODEL)
    xs2 = x_sample.reshape(nb, D_MODEL)
    mem2 = mem_prompt.reshape(batch * N_MEM, D_MODEL)

    z_s, w_in_b, mem_k, mem_v = _sample_proj(xs2, row(g_pre[l]), w_in[l], mem2, row(g_mem[l]),
                                             w_kv[l], tn=SAMPLE_PROJ_TN)
    qoff = 2 * D_RNN + 2 * D_POOL
    q_s = z_s[:, qoff:qoff + D_X].reshape(nb // ATTN_BB, ATTN_BB, D_X)

    mem_k = mem_k.reshape(batch, N_MEM, D_X)
    mem_v = mem_v.reshape(batch, N_MEM, D_X)

    perm = _chunk_interleave()
    z_p = _prompt_proj(xp2, row(g_pre[l]), w_in_b, perm, tm=PROJ_TM, tn=PROJ_TN)
    o_p, h_p, c_p, p_p, attn_s = _prompt_mix(
        z_p, mem_k, mem_v, *mix_params, perm.T, q_s, _cache_rows(cache_mem_k[l]),
        _cache_rows(cache_mem_v[l]), batch=batch, seq=seq, tm=MIX_TM)
    attn_s = attn_s.reshape(nb, D_X)

    o_s, h_s, c_s, p_s = _sample_mix(
        z_s, attn_s, state_conv[l].transpose(1, 0, 2), state_rglru_h[l],
        state_pool[l].transpose(1, 0, 2), *mix_params, tb=SAMPLE_MIX_TB)
    y_s, w_br_b, w_out_b = _branch_out_cast(o_s, z_s, xs2, w_branch[l], w_out[l], row(g_post[l]))

    y_p = _branch_out(o_p, z_p, xp2, w_br_b, w_out_b, row(g_post[l]), tm=BRANCH_TM)

    return (
        y_p.reshape(batch, seq, D_MODEL),
        y_s.reshape(nb, 1, D_MODEL),
        h_p.reshape(1, batch, D_RNN),
        c_p.reshape(1, batch, CONV_W - 1, D_RNN),
        p_p.reshape(1, batch, POOL_HIST, D_POOL),
        mem_k.reshape(1, batch, N_MEM, N_XHEADS, XHEAD_DIM),
        mem_v.reshape(1, batch, N_MEM, N_XHEADS, XHEAD_DIM),
        h_s.reshape(1, nb, D_RNN),
        c_s.transpose(1, 0, 2)[None],
        p_s.transpose(1, 0, 2)[None],
    )
```

```python
import functools

import jax
import jax.numpy as jnp
from jax import lax
from jax.experimental import pallas as pl
from jax.experimental.pallas import tpu as pltpu

D_MODEL = 2048
PAST_LEN = 16384
D_RNN = 1024
N_RNN_BLOCKS = 8
RNN_BLOCK = D_RNN // N_RNN_BLOCKS
CONV_W = 4
LRU_C = 8.0
D_POOL = 1024
POOL_WINDOWS = (2, 4, 8, 16)
POOL_GROUP = D_POOL // len(POOL_WINDOWS)
POOL_HIST = max(POOL_WINDOWS) - 1
N_MEM = 256
N_XHEADS = 4
XHEAD_DIM = 256
D_X = N_XHEADS * XHEAD_DIM
N_BRANCH = 3
D_MIX = D_RNN + D_POOL + D_X
D_IN = 2 * D_MIX + N_BRANCH * D_MODEL
EPS = 1e-6

SUBLANES = 8
LANES = 128
VMEM_LIMIT = 56 * 1024 * 1024
BIG_VMEM_LIMIT = 60 * 1024 * 1024
MIX_TM = 256
BRANCH_TM = 256
PROJ_TM, PROJ_TN = 1024, 2048
SAMPLE_PROJ_TN = 768
ATTN_BB = 4
CACHE_RING = 3
W_RING = 3
SAMPLE_MIX_TB = 64

BF16 = jnp.bfloat16
F32 = jnp.float32

NEG_LOG2_E = -1.4426950408889634


def _sigmoid(x):
    return 1.0 / (1.0 + jnp.exp2(x * NEG_LOG2_E))


def _silu(x):
    return x * _sigmoid(x)


def _softplus(x):
    return jnp.maximum(x, 0.0) + jnp.log1p(jnp.exp(-jnp.abs(x)))


def _rms_scale(x):
    return lax.rsqrt(jnp.mean(x * x, axis=-1, keepdims=True) + EPS)


def _chunk_interleave():
    nrow = MIX_TM // SUBLANES
    p = jnp.arange(MIX_TM)
    token = (p % SUBLANES) * nrow + p // SUBLANES
    return (token[:, None] == jnp.arange(MIX_TM)[None, :]).astype(BF16)


def _sample_proj_kernel(x_ref, g_ref, w_hbm, mem_ref, gm_ref, wkv_ref,
                        o_ref, wb_ref, k_ref, v_ref, u_ref, um_ref, w_ring, ring_sem,
                        *, k_steps, tn):
    j = pl.program_id(0)
    nsteps = pl.num_programs(0)

    def w_copy(blk, slot):
        return pltpu.make_async_copy(w_hbm.at[:, pl.ds(blk * tn, tn)], w_ring.at[slot],
                                     ring_sem.at[slot])

    @pl.when(j == 0)
    def _():
        for first in range(W_RING - 1):
            w_copy(first, first).start()

    ahead = j + (W_RING - 1)

    @pl.when(ahead < nsteps)
    def _():
        w_copy(ahead, ahead % W_RING).start()

    @pl.when(j == 0)
    def _():
        x = x_ref[...]
        u_ref[...] = (x * _rms_scale(x) * g_ref[...]).astype(BF16)
        mem = mem_ref[...]
        um_ref[...] = (mem * _rms_scale(mem) * gm_ref[...]).astype(BF16)

    slot = j % W_RING
    w_copy(j, slot).wait()
    w = w_ring[slot].astype(BF16)
    wb_ref[...] = w
    o_ref[...] = jnp.dot(u_ref[...], w, preferred_element_type=F32)

    kv = jnp.dot(um_ref[...], wkv_ref[...].astype(BF16), preferred_element_type=F32)

    @pl.when(j < k_steps)
    def _():
        k_ref[...] = kv

    @pl.when(j >= k_steps)
    def _():
        v_ref[...] = kv


def _sample_proj(x, g, w, mem, g_mem, w_kv, tn):
    m, k = x.shape
    n = w.shape[1]
    steps = n // tn
    mrows = mem.shape[0]
    kv_tn = 2 * D_X // steps
    assert D_X % kv_tn == 0 and kv_tn % LANES == 0
    k_steps = D_X // kv_tn
    return pl.pallas_call(
        functools.partial(_sample_proj_kernel, k_steps=k_steps, tn=tn),
        grid=(steps,),
        in_specs=[
            pl.BlockSpec((m, k), lambda j: (0, 0)),
            pl.BlockSpec((1, k), lambda j: (0, 0)),
            pl.BlockSpec(memory_space=pl.ANY),
            pl.BlockSpec((mrows, k), lambda j: (0, 0), pipeline_mode=pl.Buffered(1)),
            pl.BlockSpec((1, k), lambda j: (0, 0)),
            pl.BlockSpec((k, kv_tn), lambda j: (0, j)),
        ],
        out_specs=[
            pl.BlockSpec((m, tn), lambda j: (0, j)),
            pl.BlockSpec((k, tn), lambda j: (0, j)),
            pl.BlockSpec((mrows, kv_tn), lambda j: (0, jnp.minimum(j, k_steps - 1))),
            pl.BlockSpec((mrows, kv_tn), lambda j: (0, jnp.maximum(j - k_steps, 0))),
        ],
        out_shape=[
            jax.ShapeDtypeStruct((m, n), F32),
            jax.ShapeDtypeStruct((k, n), BF16),
            jax.ShapeDtypeStruct((mrows, D_X), F32),
            jax.ShapeDtypeStruct((mrows, D_X), F32),
        ],
        scratch_shapes=[pltpu.VMEM((m, k), BF16), pltpu.VMEM((mrows, k), BF16),
                        pltpu.VMEM((W_RING, k, tn), F32), pltpu.SemaphoreType.DMA((W_RING,))],
        compiler_params=pltpu.CompilerParams(
            dimension_semantics=("arbitrary",),
            vmem_limit_bytes=VMEM_LIMIT),
        name="sample_proj",
    )(x, g, w, mem, g_mem, w_kv)


def _decay_rate(lam):
    return _softplus(-lam) * (LRU_C * NEG_LOG2_E)


def _rglru_block(xc, wax, ba, bx, rate):
    ri = jnp.dot(xc.astype(BF16), wax, preferred_element_type=F32)
    r = _sigmoid(ri[:, :RNN_BLOCK] + ba)
    i = _sigmoid(ri[:, RNN_BLOCK:] + bx)
    a = jnp.exp2(r * rate)
    one_m = 1.0 - a * a
    mult = jnp.where(one_m > 0.0, one_m * lax.rsqrt(one_m), 0.0)
    return a, mult * i * xc


def _prompt_mix_kernel(z_ref, k_ref, v_ref, convw_ref, convb_ref, wax_ref, ba_ref, bx_ref,
                       lam_ref, wpool_ref, pscale_ref, unperm_ref, sq_ref, sk_hbm, sv_hbm,
                       o_ref, newh_ref, newconv_ref, newpool_ref, sattn_ref,
                       conv_carry, pool_carry, h_carry, kb_ref, vb_ref, ac_scr, hl_scr, op_scr,
                       sk_ring, sv_ring, ring_sem, *, tm):
    l = pl.program_id(1)
    last = pl.num_programs(1) - 1
    nrow = tm // SUBLANES

    @pl.when(l == 0)
    def _():
        conv_carry[...] = jnp.zeros(conv_carry.shape, F32)
        pool_carry[...] = jnp.zeros(pool_carry.shape, F32)
        h_carry[...] = jnp.zeros(h_carry.shape, F32)
        kb_ref[...] = k_ref[0].astype(BF16)
        vb_ref[...] = v_ref[0].astype(BF16)

    step = pl.program_id(0) * pl.num_programs(1) + l
    nsteps = pl.num_programs(0) * pl.num_programs(1)

    def cache_copies(blk, slot):
        rows = pl.ds(blk * ATTN_BB, ATTN_BB)
        return (pltpu.make_async_copy(sk_hbm.at[rows], sk_ring.at[slot], ring_sem.at[slot, 0]),
                pltpu.make_async_copy(sv_hbm.at[rows], sv_ring.at[slot], ring_sem.at[slot, 1]))

    @pl.when(step == 0)
    def _():
        for first in range(CACHE_RING - 1):
            for cp in cache_copies(first, first):
                cp.start()

    ahead = step + (CACHE_RING - 1)

    @pl.when(ahead < nsteps)
    def _():
        for cp in cache_copies(ahead, ahead % CACHE_RING):
            cp.start()

    slot = step % CACHE_RING
    for cp in cache_copies(step, slot):
        cp.wait()
    sk_ref = sk_ring.at[slot]
    sv_ref = sv_ring.at[slot]
    side_scores = _sample_attn_scores(sq_ref.at[0], sk_ref, ATTN_BB)

    chunk_id = lax.broadcasted_iota(jnp.int32, (SUBLANES, LANES), 0)
    first_chunk = chunk_id == 0

    def load_groups(col, width=LANES):
        return [z_ref[r * SUBLANES:(r + 1) * SUBLANES, col:col + width] for r in range(nrow)]

    def put(col, width, val):
        op_scr[:, col:col + width] = val.astype(BF16)

    def store_groups(col, rows, width=LANES):
        put(col, width, jnp.concatenate(rows, axis=0))

    def history(tail_group, carry_ref, j, c0):
        tail = pltpu.roll(tail_group, 1, 0)
        prev = jnp.where(first_chunk, carry_ref[j - 1, :, c0:c0 + LANES], tail)
        carry_ref[j - 1, :, c0:c0 + LANES] = tail
        return prev

    rate = _decay_rate(lam_ref[...])
    for n in range(N_RNN_BLOCKS):
        c0, c1 = n * RNN_BLOCK, (n + 1) * RNN_BLOCK
        xs = load_groups(c0)
        ext = [history(xs[nrow - j], conv_carry, j, c0) for j in range(CONV_W - 1, 0, -1)] + xs
        cw = [jnp.broadcast_to(convw_ref[k:k + 1, c0:c1], (SUBLANES, LANES)) for k in range(CONV_W)]
        cb = jnp.broadcast_to(convb_ref[:, c0:c1], (SUBLANES, LANES))
        xc = []
        for r in range(nrow):
            acc = cb + cw[0] * ext[r]
            for k in range(1, CONV_W):
                acc = acc + cw[k] * ext[r + k]
            xc.append(acc)
        a, b = _rglru_block(jnp.concatenate(xc, axis=0), wax_ref[n], ba_ref[:, c0:c1],
                            bx_ref[:, c0:c1], rate[:, c0:c1])
        ac_scr[:, c0:c1] = a
        hl_scr[:, c0:c1] = b

    side_probs = _sample_attn_probs(side_scores)

    acc_a = ac_scr[0:SUBLANES, :]
    acc_h = hl_scr[0:SUBLANES, :]
    for r in range(1, nrow):
        rows = slice(r * SUBLANES, (r + 1) * SUBLANES)
        ar = ac_scr[rows, :]
        acc_h = ar * acc_h + hl_scr[rows, :]
        acc_a = ar * acc_a
        ac_scr[rows, :] = acc_a
        hl_scr[rows, :] = acc_h
    h_in = h_carry[...]
    entering = []
    for c in range(SUBLANES):
        entering.append(h_in)
        h_in = acc_a[c:c + 1] * h_in + acc_h[c:c + 1]
    h_carry[...] = h_in
    h_enter = jnp.concatenate(entering, axis=0)
    for n in range(N_RNN_BLOCKS):
        c0, c1 = n * RNN_BLOCK, (n + 1) * RNN_BLOCK
        gr = load_groups(D_RNN + c0)
        store_groups(c0, [(hl_scr[r * SUBLANES:(r + 1) * SUBLANES, c0:c1]
                           + ac_scr[r * SUBLANES:(r + 1) * SUBLANES, c0:c1] * h_enter[:, c0:c1])
                          * _silu(gr[r]) for r in range(nrow)])

    _sample_attn_values(side_probs, sv_ref, sattn_ref.at[0])

    pcol = 2 * D_RNN
    blocks = [(w, c0) for g, w in enumerate(POOL_WINDOWS)
              for c0 in range(g * POOL_GROUP, (g + 1) * POOL_GROUP, LANES)]

    def group(c0, r):
        return z_ref[r * SUBLANES:(r + 1) * SUBLANES, pcol + c0:pcol + c0 + LANES]

    def mean_minus_token(tot, w, c0, r):
        if r < w - 1:
            pos1 = l * tm + chunk_id * nrow + (r + 1)
            mean = tot / jnp.minimum(pos1, w).astype(F32)
        else:
            mean = tot * (1.0 / w)
        return mean - group(c0, r)

    hist, tot = {}, {}
    for w, c0 in blocks:
        hist[c0] = [history(group(c0, nrow - j), pool_carry, j, c0) for j in range(1, w)]
        t = group(c0, 0)
        for h in hist[c0]:
            t = t + h
        tot[c0] = t
        hl_scr[0:SUBLANES, c0:c0 + LANES] = mean_minus_token(t, w, c0, 0)
    for r in range(1, nrow):
        for w, c0 in blocks:
            leaving = group(c0, r - w) if r >= w else hist[c0][w - r - 1]
            tot[c0] = tot[c0] + (group(c0, r) - leaving)
            hl_scr[r * SUBLANES:(r + 1) * SUBLANES, c0:c0 + LANES] = mean_minus_token(
                tot[c0], w, c0, r)
    for g, w in enumerate(POOL_WINDOWS):
        c0, c1 = g * POOL_GROUP, (g + 1) * POOL_GROUP
        og = jnp.dot(hl_scr[:, c0:c1].astype(BF16), wpool_ref[g], preferred_element_type=F32)
        gp = z_ref[:, pcol + D_POOL + c0:pcol + D_POOL + c1]
        put(D_RNN + c0, POOL_GROUP, og * pscale_ref[:, c0:c1] * _silu(gp))

    qoff = 2 * D_RNN + 2 * D_POOL
    for hd in range(N_XHEADS):
        c0, c1 = hd * XHEAD_DIM, (hd + 1) * XHEAD_DIM
        q = z_ref[:, qoff + c0:qoff + c1].astype(BF16)
        s = lax.dot_general(q, kb_ref[:, c0:c1], (((1,), (1,)), ((), ())),
                            preferred_element_type=F32) * (XHEAD_DIM ** -0.5)
        p = jnp.exp(s - jnp.max(s, axis=-1, keepdims=True))
        p = p / jnp.sum(p, axis=-1, keepdims=True)
        ox = jnp.dot(p.astype(BF16), vb_ref[:, c0:c1], preferred_element_type=F32)
        gx = z_ref[:, qoff + D_X + c0:qoff + D_X + c1]
        put(D_RNN + D_POOL + c0, XHEAD_DIM, ox * _silu(gx))

    o_ref[...] = jnp.dot(unperm_ref[...], op_scr[...], preferred_element_type=F32).astype(BF16)

    @pl.when(l == last)
    def _():
        newh_ref[0] = h_carry[...]
        tail_row = lambda j: (nrow - j) * SUBLANES + SUBLANES - 1
        for j in range(1, CONV_W):
            newconv_ref[0, CONV_W - 1 - j:CONV_W - j, :] = z_ref[tail_row(j):tail_row(j) + 1, 0:D_RNN]
        for j in range(1, POOL_HIST + 1):
            newpool_ref[0, POOL_HIST - j:POOL_HIST - j + 1, :] = (
                z_ref[tail_row(j):tail_row(j) + 1, pcol:pcol + D_POOL])


def _prompt_mix(z, mem_k, mem_v, conv_w, conv_b, wax, b_a, b_x, lam, wpool, pscale, unperm,
                sample_q, cache_k, cache_v, batch, seq, tm):
    nl = seq // tm
    assert sample_q.shape[0] == batch * nl
    side = lambda b, l: (b * nl + l, 0, 0)
    zw = 2 * D_MIX
    const2 = lambda b, l: (0, 0)
    const3 = lambda b, l: (0, 0, 0)
    kern = functools.partial(_prompt_mix_kernel, tm=tm)
    return pl.pallas_call(
        kern,
        grid=(batch, nl),
        in_specs=[
            pl.BlockSpec((tm, zw), lambda b, l: (b * nl + l, 0)),
            pl.BlockSpec((1, N_MEM, D_X), lambda b, l: (b, 0, 0)),
            pl.BlockSpec((1, N_MEM, D_X), lambda b, l: (b, 0, 0)),
            pl.BlockSpec((CONV_W, D_RNN), const2),
            pl.BlockSpec((1, D_RNN), const2),
            pl.BlockSpec((N_RNN_BLOCKS, RNN_BLOCK, 2 * RNN_BLOCK), const3),
            pl.BlockSpec((1, D_RNN), const2),
            pl.BlockSpec((1, D_RNN), const2),
            pl.BlockSpec((1, D_RNN), const2),
            pl.BlockSpec((len(POOL_WINDOWS), POOL_GROUP, POOL_GROUP), const3),
            pl.BlockSpec((1, D_POOL), const2),
            pl.BlockSpec((tm, tm), const2),
            pl.BlockSpec((1, ATTN_BB, D_X), side),
            pl.BlockSpec(memory_space=pl.ANY),
            pl.BlockSpec(memory_space=pl.ANY),
        ],
        out_specs=[
            pl.BlockSpec((tm, D_MIX), lambda b, l: (b * nl + l, 0)),
            pl.BlockSpec((1, 1, D_RNN), lambda b, l: (b, 0, 0)),
            pl.BlockSpec((1, CONV_W - 1, D_RNN), lambda b, l: (b, 0, 0)),
            pl.BlockSpec((1, POOL_HIST, D_POOL), lambda b, l: (b, 0, 0)),
            pl.BlockSpec((1, ATTN_BB, D_X), side),
        ],
        out_shape=[
            jax.ShapeDtypeStruct((batch * seq, D_MIX), BF16),
            jax.ShapeDtypeStruct((batch, 1, D_RNN), F32),
            jax.ShapeDtypeStruct((batch, CONV_W - 1, D_RNN), F32),
            jax.ShapeDtypeStruct((batch, POOL_HIST, D_POOL), F32),
            jax.ShapeDtypeStruct(sample_q.shape, F32),
        ],
        scratch_shapes=[
            pltpu.VMEM((CONV_W - 1, SUBLANES, D_RNN), F32),
            pltpu.VMEM((POOL_HIST, SUBLANES, D_POOL), F32),
            pltpu.VMEM((1, D_RNN), F32),
            pltpu.VMEM((N_MEM, D_X), BF16),
            pltpu.VMEM((N_MEM, D_X), BF16),
            pltpu.VMEM((tm, D_RNN), F32),
            pltpu.VMEM((tm, D_RNN), F32),
            pltpu.VMEM((tm, D_MIX), BF16),
            pltpu.VMEM((CACHE_RING, ATTN_BB, N_MEM * SUBLANES, LANES), F32),
            pltpu.VMEM((CACHE_RING, ATTN_BB, N_MEM * SUBLANES, LANES), F32),
            pltpu.SemaphoreType.DMA((CACHE_RING, 2)),
        ],
        compiler_params=pltpu.CompilerParams(
            dimension_semantics=("arbitrary", "arbitrary"),
            vmem_limit_bytes=VMEM_LIMIT),
        name="prompt_mix",
    )(z, mem_k, mem_v, conv_w, conv_b, wax, b_a, b_x, lam, wpool, pscale, unperm,
      sample_q, cache_k, cache_v)


def _cache_rows(c):
    nb = c.shape[0]
    c = c.reshape(nb, N_MEM, N_XHEADS, XHEAD_DIM // LANES, LANES)
    return c.transpose(0, 1, 3, 2, 4).reshape(nb, N_MEM * SUBLANES, LANES)


def _sample_attn_scores(q_ref, k_ref, bb):
    halves = XHEAD_DIM // LANES
    assert halves * N_XHEADS == SUBLANES
    scores = []
    for j in range(bb):
        qn = jnp.concatenate(
            [q_ref[j:j + 1, (h * halves + t) * LANES:(h * halves + t + 1) * LANES]
             for t in range(halves) for h in range(N_XHEADS)], axis=0)
        scores.append(lax.dot_general(qn.astype(BF16), k_ref[j].astype(BF16),
                                      (((1,), (1,)), ((), ())), preferred_element_type=F32)
                      * (XHEAD_DIM ** -0.5))
    return scores


def _sample_attn_probs(scores):
    r = lax.broadcasted_iota(jnp.int32, (SUBLANES, LANES), 0)
    c = lax.broadcasted_iota(jnp.int32, (SUBLANES, LANES), 1)
    diag = (c % SUBLANES) == r
    first_half = r < N_XHEADS
    nchunk = N_MEM * SUBLANES // LANES
    probs = []
    for s in scores:
        chunks = []
        for ci in range(nchunk):
            sm = jnp.where(diag, s[:, ci * LANES:(ci + 1) * LANES], 0.0)
            other = pltpu.roll(sm, N_XHEADS, 0)
            other = jnp.where(first_half, pltpu.roll(other, LANES - N_XHEADS, 1),
                              pltpu.roll(other, N_XHEADS, 1))
            chunks.append(jnp.where(diag, sm + other, -jnp.inf))
        t_full = jnp.concatenate(chunks, axis=1)
        e = jnp.exp(t_full - jnp.max(t_full, axis=1, keepdims=True))
        probs.append((e / jnp.sum(e, axis=1, keepdims=True)).astype(BF16))
    return probs


def _sample_attn_values(probs, v_ref, o_ref):
    halves = XHEAD_DIM // LANES
    for j, p in enumerate(probs):
        o = jnp.dot(p, v_ref[j].astype(BF16), preferred_element_type=F32)
        for t in range(halves):
            for h in range(N_XHEADS):
                col = (h * halves + t) * LANES
                o_ref[j:j + 1, col:col + LANES] = o[t * N_XHEADS + h:t * N_XHEADS + h + 1, :]


def _prompt_proj_kernel(x_ref, g_ref, w_ref, perm_ref, o_ref, u_ref, up_ref, *, mix_steps):
    j = pl.program_id(1)

    @pl.when(j == 0)
    def _():
        x = x_ref[...]
        u = (x * _rms_scale(x) * g_ref[...]).astype(BF16)
        u_ref[...] = u
        for r0 in range(0, u.shape[0], MIX_TM):
            up_ref[r0:r0 + MIX_TM, :] = jnp.dot(
                perm_ref[...], u[r0:r0 + MIX_TM], preferred_element_type=F32).astype(BF16)

    @pl.when(j < mix_steps)
    def _():
        o_ref[...] = jnp.dot(up_ref[...], w_ref[...], preferred_element_type=F32)

    @pl.when(j >= mix_steps)
    def _():
        o_ref[...] = jnp.dot(u_ref[...], w_ref[...], preferred_element_type=F32)


def _prompt_proj(x, g, w, perm, tm, tn):
    m, k = x.shape
    n = w.shape[1]
    assert (2 * D_MIX) % tn == 0
    return pl.pallas_call(
        functools.partial(_prompt_proj_kernel, mix_steps=2 * D_MIX // tn),
        grid=(m // tm, n // tn),
        in_specs=[
            pl.BlockSpec((tm, k), lambda i, j: (i, 0)),
            pl.BlockSpec((1, k), lambda i, j: (0, 0)),
            pl.BlockSpec((k, tn), lambda i, j: (0, j)),
            pl.BlockSpec(perm.shape, lambda i, j: (0, 0)),
        ],
        out_specs=pl.BlockSpec((tm, tn), lambda i, j: (i, j)),
        out_shape=jax.ShapeDtypeStruct((m, n), F32),
        scratch_shapes=[pltpu.VMEM((tm, k), BF16), pltpu.VMEM((tm, k), BF16)],
        compiler_params=pltpu.CompilerParams(
            dimension_semantics=("arbitrary", "arbitrary"),
            vmem_limit_bytes=BIG_VMEM_LIMIT),
        name="prompt_proj",
    )(x, g, w, perm)


def _sample_mix_kernel(z_ref, attn_ref, conv_ref, h_ref, pool_ref,
                       convw_ref, convb_ref, wax_ref, ba_ref, bx_ref, lam_ref, wpool_ref,
                       pscale_ref, o_ref, newh_ref, newconv_ref, newpool_ref):
    xr = z_ref[:, 0:D_RNN]
    xc = convb_ref[...] + convw_ref[CONV_W - 1:CONV_W, :] * xr
    for k in range(CONV_W - 1):
        xc = xc + convw_ref[k:k + 1, :] * conv_ref[k]
    for k in range(CONV_W - 2):
        newconv_ref[k] = conv_ref[k + 1]
    newconv_ref[CONV_W - 2] = xr

    rate = _decay_rate(lam_ref[...])
    for n in range(N_RNN_BLOCKS):
        c0, c1 = n * RNN_BLOCK, (n + 1) * RNN_BLOCK
        a, b = _rglru_block(xc[:, c0:c1], wax_ref[n], ba_ref[:, c0:c1], bx_ref[:, c0:c1],
                            rate[:, c0:c1])
        h = a * h_ref[:, c0:c1] + b
        newh_ref[:, c0:c1] = h
        o_ref[:, c0:c1] = (h * _silu(z_ref[:, D_RNN + c0:D_RNN + c1])).astype(BF16)

    xp = z_ref[:, 2 * D_RNN:2 * D_RNN + D_POOL]
    for k in range(POOL_HIST - 1):
        newpool_ref[k] = pool_ref[k + 1]
    newpool_ref[POOL_HIST - 1] = xp
    for g, w in enumerate(POOL_WINDOWS):
        c0, c1 = g * POOL_GROUP, (g + 1) * POOL_GROUP
        xg = xp[:, c0:c1]
        tot = xg
        for j in range(1, w):
            tot = tot + pool_ref[POOL_HIST - j, :, c0:c1]
        cnt = float(min(PAST_LEN + 1, w))
        d = tot / cnt - xg
        og = jnp.dot(d.astype(BF16), wpool_ref[g], preferred_element_type=F32)
        gp = z_ref[:, 2 * D_RNN + D_POOL + c0:2 * D_RNN + D_POOL + c1]
        o_ref[:, D_RNN + c0:D_RNN + c1] = (og * pscale_ref[:, c0:c1] * _silu(gp)).astype(BF16)

    gx = z_ref[:, 2 * D_RNN + 2 * D_POOL + D_X:2 * D_MIX]
    o_ref[:, D_RNN + D_POOL:] = (attn_ref[...] * _silu(gx)).astype(BF16)


def _sample_mix(z, attn, conv, h, pool, conv_w, conv_b, wax, b_a, b_x, lam, wpool, pscale, tb):
    nb = z.shape[0]
    zw = 2 * D_MIX
    rows = lambda i: (i, 0)
    const2 = lambda i: (0, 0)
    const3 = lambda i: (0, 0, 0)
    hist = lambda i: (0, i, 0)
    return pl.pallas_call(
        _sample_mix_kernel,
        grid=(nb // tb,),
        in_specs=[
            pl.BlockSpec((tb, zw), rows),
            pl.BlockSpec((tb, D_X), rows),
            pl.BlockSpec((CONV_W - 1, tb, D_RNN), hist),
            pl.BlockSpec((tb, D_RNN), rows),
            pl.BlockSpec((POOL_HIST, tb, D_POOL), hist),
            pl.BlockSpec((CONV_W, D_RNN), const2),
            pl.BlockSpec((1, D_RNN), const2),
            pl.BlockSpec((N_RNN_BLOCKS, RNN_BLOCK, 2 * RNN_BLOCK), const3),
            pl.BlockSpec((1, D_RNN), const2),
            pl.BlockSpec((1, D_RNN), const2),
            pl.BlockSpec((1, D_RNN), const2),
            pl.BlockSpec((len(POOL_WINDOWS), POOL_GROUP, POOL_GROUP), const3),
            pl.BlockSpec((1, D_POOL), const2),
        ],
        out_specs=[
            pl.BlockSpec((tb, D_MIX), rows),
            pl.BlockSpec((tb, D_RNN), rows),
            pl.BlockSpec((CONV_W - 1, tb, D_RNN), hist),
            pl.BlockSpec((POOL_HIST, tb, D_POOL), hist),
        ],
        out_shape=[
            jax.ShapeDtypeStruct((nb, D_MIX), BF16),
            jax.ShapeDtypeStruct((nb, D_RNN), F32),
            jax.ShapeDtypeStruct((CONV_W - 1, nb, D_RNN), F32),
            jax.ShapeDtypeStruct((POOL_HIST, nb, D_POOL), F32),
        ],
        compiler_params=pltpu.CompilerParams(
            dimension_semantics=("arbitrary",),
            vmem_limit_bytes=VMEM_LIMIT),
        name="sample_mix",
    )(z, attn, conv, h, pool, conv_w, conv_b, wax, b_a, b_x, lam, wpool, pscale)


def _branch_out_kernel(o_ref, gates_ref, x_ref, wb_ref, wo_ref, gpost_ref, y_ref):
    merged = None
    for j, (r0, r1) in enumerate(((0, D_RNN), (D_RNN, D_RNN + D_POOL), (D_RNN + D_POOL, D_MIX))):
        yj = jnp.dot(o_ref[:, r0:r1], wb_ref[r0:r1, :], preferred_element_type=F32)
        term = _sigmoid(gates_ref[:, j * D_MODEL:(j + 1) * D_MODEL]) * yj
        merged = term if merged is None else merged + term
    out = jnp.dot(merged.astype(BF16), wo_ref[...], preferred_element_type=F32)
    y_ref[...] = x_ref[...] + (out * gpost_ref[...]) * _rms_scale(out)


def _branch_out(o, z, x, wb, wo, g_post, tm):
    m = x.shape[0]
    gw = N_BRANCH * D_MODEL
    gblk = (2 * D_MIX) // gw
    resident = pl.Buffered(1)
    return pl.pallas_call(
        _branch_out_kernel,
        grid=(m // tm,),
        in_specs=[
            pl.BlockSpec((tm, D_MIX), lambda i: (i, 0)),
            pl.BlockSpec((tm, gw), lambda i: (i, gblk)),
            pl.BlockSpec((tm, D_MODEL), lambda i: (i, 0)),
            pl.BlockSpec((D_MIX, D_MODEL), lambda i: (0, 0), pipeline_mode=resident),
            pl.BlockSpec((D_MODEL, D_MODEL), lambda i: (0, 0), pipeline_mode=resident),
            pl.BlockSpec((1, D_MODEL), lambda i: (0, 0)),
        ],
        out_specs=pl.BlockSpec((tm, D_MODEL), lambda i: (i, 0)),
        out_shape=jax.ShapeDtypeStruct((m, D_MODEL), F32),
        compiler_params=pltpu.CompilerParams(
            dimension_semantics=("arbitrary",),
            vmem_limit_bytes=VMEM_LIMIT),
        name="branch_out",
    )(o, z, x, wb, wo, g_post)


WROWS = 1024
PER_BRANCH = D_RNN // WROWS
assert D_RNN == D_POOL == D_X and D_RNN % WROWS == 0 and D_MODEL % WROWS == 0
N_WB_BLOCKS = N_BRANCH * PER_BRANCH
N_WOUT_BLOCKS = D_MODEL // WROWS


def _branch_out_cast_kernel(o_ref, gates_ref, x_ref, wb_ref, wo_ref, gpost_ref,
                            y_ref, wbb_ref, wob_ref, merged_ref, out_ref):
    s = pl.program_id(0)

    @pl.when(s < N_WB_BLOCKS)
    def _():
        w = wb_ref[...].astype(BF16)
        wbb_ref[...] = w
        term = _sigmoid(gates_ref[...]) * jnp.dot(o_ref[...], w, preferred_element_type=F32)

        @pl.when(s == 0)
        def _():
            merged_ref[...] = term

        @pl.when(s > 0)
        def _():
            merged_ref[...] += term

    for kb in range(N_WOUT_BLOCKS):
        @pl.when(s == N_WB_BLOCKS + kb)
        def _(kb=kb):
            w = wo_ref[...].astype(BF16)
            wob_ref[...] = w
            part = jnp.dot(merged_ref[:, kb * WROWS:(kb + 1) * WROWS].astype(BF16), w,
                           preferred_element_type=F32)
            if kb == 0:
                out_ref[...] = part
            else:
                out_ref[...] += part

    @pl.when(s == N_WB_BLOCKS + N_WOUT_BLOCKS - 1)
    def _():
        out = out_ref[...]
        y_ref[...] = x_ref[...] + out * _rms_scale(out) * gpost_ref[...]


def _branch_out_cast(o, z, x, wb, wo, g_post):
    m = x.shape[0]
    gblk0 = (2 * D_MIX) // D_MODEL
    wb_blk = lambda s: jnp.minimum(s, N_WB_BLOCKS - 1)
    wo_blk = lambda s: jnp.maximum(s - N_WB_BLOCKS, 0)
    return pl.pallas_call(
        _branch_out_cast_kernel,
        grid=(N_WB_BLOCKS + N_WOUT_BLOCKS,),
        in_specs=[
            pl.BlockSpec((m, WROWS), lambda s: (0, wb_blk(s))),
            pl.BlockSpec((m, D_MODEL), lambda s: (0, gblk0 + wb_blk(s) // PER_BRANCH)),
            pl.BlockSpec((m, D_MODEL), lambda s: (0, 0)),
            pl.BlockSpec((WROWS, D_MODEL), lambda s: (wb_blk(s), 0)),
            pl.BlockSpec((WROWS, D_MODEL), lambda s: (wo_blk(s), 0)),
            pl.BlockSpec((1, D_MODEL), lambda s: (0, 0)),
        ],
        out_specs=[
            pl.BlockSpec((m, D_MODEL), lambda s: (0, 0)),
            pl.BlockSpec((WROWS, D_MODEL), lambda s: (wb_blk(s), 0)),
            pl.BlockSpec((WROWS, D_MODEL), lambda s: (wo_blk(s), 0)),
        ],
        out_shape=[
            jax.ShapeDtypeStruct((m, D_MODEL), F32),
            jax.ShapeDtypeStruct(wb.shape, BF16),
            jax.ShapeDtypeStruct(wo.shape, BF16),
        ],
        scratch_shapes=[pltpu.VMEM((m, D_MODEL), F32), pltpu.VMEM((m, D_MODEL), F32)],
        compiler_params=pltpu.CompilerParams(
            dimension_semantics=("arbitrary",),
            vmem_limit_bytes=BIG_VMEM_LIMIT),
        name="branch_out_cast",
    )(o, z, x, wb, wo, g_post)


def kernel(x_prompt, x_sample, mem_prompt, state_rglru_h, state_conv, state_pool, cache_mem_k, cache_mem_v, g_pre, w_in, conv_w, conv_b, w_rg_a, b_rg_a, w_rg_x, b_rg_x, lru_lambda, w_pool, pool_scale, g_mem, w_kv, w_branch, w_out, g_post):
    batch, seq, _ = x_prompt.shape
    nb = x_sample.shape[0]
    depth = g_pre.shape[0]
    assert depth == 1 and x_sample.shape[1] == 1

    l = 0
    row = lambda v: v.reshape(1, -1)
    wax = jnp.concatenate([w_rg_a[l], w_rg_x[l]], axis=-1).astype(BF16)
    wpool = w_pool[l].astype(BF16)
    mix_params = (conv_w[l], row(conv_b[l]), wax, row(b_rg_a[l]), row(b_rg_x[l]),
                  row(lru_lambda[l]), wpool, row(pool_scale[l]))

    xp2 = x_prompt.reshape(batch * seq, D_MODEL)
    xs2 = x_sample.reshape(nb, D_MODEL)
    mem2 = mem_prompt.reshape(batch * N_MEM, D_MODEL)

    z_s, w_in_b, mem_k, mem_v = _sample_proj(xs2, row(g_pre[l]), w_in[l], mem2, row(g_mem[l]),
                                             w_kv[l], tn=SAMPLE_PROJ_TN)
    qoff = 2 * D_RNN + 2 * D_POOL
    q_s = z_s[:, qoff:qoff + D_X].reshape(nb // ATTN_BB, ATTN_BB, D_X)

    mem_k = mem_k.reshape(batch, N_MEM, D_X)
    mem_v = mem_v.reshape(batch, N_MEM, D_X)

    perm = _chunk_interleave()
    z_p = _prompt_proj(xp2, row(g_pre[l]), w_in_b, perm, tm=PROJ_TM, tn=PROJ_TN)
    o_p, h_p, c_p, p_p, attn_s = _prompt_mix(
        z_p, mem_k, mem_v, *mix_params, perm.T, q_s, _cache_rows(cache_mem_k[l]),
        _cache_rows(cache_mem_v[l]), batch=batch, seq=seq, tm=MIX_TM)
    attn_s = attn_s.reshape(nb, D_X)

    o_s, h_s, c_s, p_s = _sample_mix(
        z_s, attn_s, state_conv[l].transpose(1, 0, 2), state_rglru_h[l],
        state_pool[l].transpose(1, 0, 2), *mix_params, tb=SAMPLE_MIX_TB)
    y_s, w_br_b, w_out_b = _branch_out_cast(o_s, z_s, xs2, w_branch[l], w_out[l], row(g_post[l]))

    y_p = _branch_out(o_p, z_p, xp2, w_br_b, w_out_b, row(g_post[l]), tm=BRANCH_TM)

    return (
        y_p.reshape(batch, seq, D_MODEL),
        y_s.reshape(nb, 1, D_MODEL),
        h_p.reshape(1, batch, D_RNN),
        c_p.reshape(1, batch, CONV_W - 1, D_RNN),
        p_p.reshape(1, batch, POOL_HIST, D_POOL),
        mem_k.reshape(1, batch, N_MEM, N_XHEADS, XHEAD_DIM),
        mem_v.reshape(1, batch, N_MEM, N_XHEADS, XHEAD_DIM),
        h_s.reshape(1, nb, D_RNN),
        c_s.transpose(1, 0, 2)[None],
        p_s.transpose(1, 0, 2)[None],
    )
```

```python
import functools

import jax
import jax.numpy as jnp
from jax import lax
from jax.experimental import pallas as pl
from jax.experimental.pallas import tpu as pltpu

D_MODEL = 2048
PAST_LEN = 16384
D_RNN = 1024
N_RNN_BLOCKS = 8
RNN_BLOCK = D_RNN // N_RNN_BLOCKS
CONV_W = 4
LRU_C = 8.0
D_POOL = 1024
POOL_WINDOWS = (2, 4, 8, 16)
POOL_GROUP = D_POOL // len(POOL_WINDOWS)
POOL_HIST = max(POOL_WINDOWS) - 1
N_MEM = 256
N_XHEADS = 4
XHEAD_DIM = 256
D_X = N_XHEADS * XHEAD_DIM
N_BRANCH = 3
D_MIX = D_RNN + D_POOL + D_X
D_IN = 2 * D_MIX + N_BRANCH * D_MODEL
EPS = 1e-6

SUBLANES = 8
LANES = 128
VMEM_LIMIT = 56 * 1024 * 1024
BIG_VMEM_LIMIT = 60 * 1024 * 1024
MIX_TM = 256
BRANCH_TM = 256
PROJ_TM, PROJ_TN = 1024, 2048
SAMPLE_PROJ_TN = 768
ATTN_BB = 4
CACHE_RING = 3
W_RING = 3
SAMPLE_MIX_TB = 64

BF16 = jnp.bfloat16
F32 = jnp.float32

NEG_LOG2_E = -1.4426950408889634


def _sigmoid(x):
    return 1.0 / (1.0 + jnp.exp2(x * NEG_LOG2_E))


def _silu(x):
    return x * _sigmoid(x)


def _softplus(x):
    return jnp.maximum(x, 0.0) + jnp.log1p(jnp.exp(-jnp.abs(x)))


def _rms_scale(x):
    return lax.rsqrt(jnp.mean(x * x, axis=-1, keepdims=True) + EPS)


def _chunk_interleave():
    nrow = MIX_TM // SUBLANES
    p = jnp.arange(MIX_TM)
    token = (p % SUBLANES) * nrow + p // SUBLANES
    return (token[:, None] == jnp.arange(MIX_TM)[None, :]).astype(BF16)


def _sample_proj_kernel(x_ref, g_ref, w_hbm, mem_ref, gm_ref, wkv_ref,
                        o_ref, wb_ref, k_ref, v_ref, u_ref, um_ref, w_ring, ring_sem,
                        *, k_steps, tn):
    j = pl.program_id(0)
    nsteps = pl.num_programs(0)

    def w_copy(blk, slot):
        return pltpu.make_async_copy(w_hbm.at[:, pl.ds(blk * tn, tn)], w_ring.at[slot],
                                     ring_sem.at[slot])

    @pl.when(j == 0)
    def _():
        for first in range(W_RING - 1):
            w_copy(first, first).start()

    ahead = j + (W_RING - 1)

    @pl.when(ahead < nsteps)
    def _():
        w_copy(ahead, ahead % W_RING).start()

    @pl.when(j == 0)
    def _():
        x = x_ref[...]
        u_ref[...] = (x * _rms_scale(x) * g_ref[...]).astype(BF16)
        mem = mem_ref[...]
        um_ref[...] = (mem * _rms_scale(mem) * gm_ref[...]).astype(BF16)

    slot = j % W_RING
    w_copy(j, slot).wait()
    w = w_ring[slot].astype(BF16)
    wb_ref[...] = w
    o_ref[...] = jnp.dot(u_ref[...], w, preferred_element_type=F32)

    kv = jnp.dot(um_ref[...], wkv_ref[...].astype(BF16), preferred_element_type=F32)

    @pl.when(j < k_steps)
    def _():
        k_ref[...] = kv

    @pl.when(j >= k_steps)
    def _():
        v_ref[...] = kv


def _sample_proj(x, g, w, mem, g_mem, w_kv, tn):
    m, k = x.shape
    n = w.shape[1]
    steps = n // tn
    mrows = mem.shape[0]
    kv_tn = 2 * D_X // steps
    assert D_X % kv_tn == 0 and kv_tn % LANES == 0
    k_steps = D_X // kv_tn
    return pl.pallas_call(
        functools.partial(_sample_proj_kernel, k_steps=k_steps, tn=tn),
        grid=(steps,),
        in_specs=[
            pl.BlockSpec((m, k), lambda j: (0, 0)),
            pl.BlockSpec((1, k), lambda j: (0, 0)),
            pl.BlockSpec(memory_space=pl.ANY),
            pl.BlockSpec((mrows, k), lambda j: (0, 0), pipeline_mode=pl.Buffered(1)),
            pl.BlockSpec((1, k), lambda j: (0, 0)),
            pl.BlockSpec((k, kv_tn), lambda j: (0, j)),
        ],
        out_specs=[
            pl.BlockSpec((m, tn), lambda j: (0, j)),
            pl.BlockSpec((k, tn), lambda j: (0, j)),
            pl.BlockSpec((mrows, kv_tn), lambda j: (0, jnp.minimum(j, k_steps - 1))),
            pl.BlockSpec((mrows, kv_tn), lambda j: (0, jnp.maximum(j - k_steps, 0))),
        ],
        out_shape=[
            jax.ShapeDtypeStruct((m, n), F32),
            jax.ShapeDtypeStruct((k, n), BF16),
            jax.ShapeDtypeStruct((mrows, D_X), F32),
            jax.ShapeDtypeStruct((mrows, D_X), F32),
        ],
        scratch_shapes=[pltpu.VMEM((m, k), BF16), pltpu.VMEM((mrows, k), BF16),
                        pltpu.VMEM((W_RING, k, tn), F32), pltpu.SemaphoreType.DMA((W_RING,))],
        compiler_params=pltpu.CompilerParams(
            dimension_semantics=("arbitrary",),
            vmem_limit_bytes=VMEM_LIMIT),
        name="sample_proj",
    )(x, g, w, mem, g_mem, w_kv)


def _decay_rate(lam):
    return _softplus(-lam) * (LRU_C * NEG_LOG2_E)


def _rglru_block(xc, wax, ba, bx, rate):
    ri = jnp.dot(xc.astype(BF16), wax, preferred_element_type=F32)
    r = _sigmoid(ri[:, :RNN_BLOCK] + ba)
    i = _sigmoid(ri[:, RNN_BLOCK:] + bx)
    a = jnp.exp2(r * rate)
    one_m = 1.0 - a * a
    mult = jnp.where(one_m > 0.0, one_m * lax.rsqrt(one_m), 0.0)
    return a, mult * i * xc


def _prompt_mix_kernel(z_hbm, k_ref, v_ref, convw_ref, convb_ref, wax_ref, ba_ref, bx_ref,
                       lam_ref, wpool_ref, pscale_ref, unperm_ref, sq_ref, sk_hbm, sv_hbm,
                       o_ref, newh_ref, newconv_ref, newpool_ref, sattn_ref,
                       conv_carry, pool_carry, h_carry, kb_ref, vb_ref, ac_scr, hl_scr, op_scr,
                       sk_ring, sv_ring, ring_sem, z_ring, z_sem, *, tm):
    l = pl.program_id(1)
    last = pl.num_programs(1) - 1
    nrow = tm // SUBLANES

    @pl.when(l == 0)
    def _():
        conv_carry[...] = jnp.zeros(conv_carry.shape, F32)
        pool_carry[...] = jnp.zeros(pool_carry.shape, F32)
        h_carry[...] = jnp.zeros(h_carry.shape, F32)
        kb_ref[...] = k_ref[0].astype(BF16)
        vb_ref[...] = v_ref[0].astype(BF16)

    step = pl.program_id(0) * pl.num_programs(1) + l
    nsteps = pl.num_programs(0) * pl.num_programs(1)

    def cache_copies(blk, slot):
        rows = pl.ds(blk * ATTN_BB, ATTN_BB)
        return (pltpu.make_async_copy(sk_hbm.at[rows], sk_ring.at[slot], ring_sem.at[slot, 0]),
                pltpu.make_async_copy(sv_hbm.at[rows], sv_ring.at[slot], ring_sem.at[slot, 1]),
                pltpu.make_async_copy(z_hbm.at[pl.ds(blk * tm, tm), pl.ds(0, 2 * D_MIX)],
                                      z_ring.at[slot], z_sem.at[slot]))

    @pl.when(step == 0)
    def _():
        for first in range(CACHE_RING - 1):
            for cp in cache_copies(first, first):
                cp.start()

    ahead = step + (CACHE_RING - 1)

    @pl.when(ahead < nsteps)
    def _():
        for cp in cache_copies(ahead, ahead % CACHE_RING):
            cp.start()

    slot = step % CACHE_RING
    for cp in cache_copies(step, slot):
        cp.wait()
    sk_ref = sk_ring.at[slot]
    sv_ref = sv_ring.at[slot]
    z_ref = z_ring.at[slot]
    side_scores = _sample_attn_scores(sq_ref.at[0], sk_ref, ATTN_BB)

    chunk_id = lax.broadcasted_iota(jnp.int32, (SUBLANES, LANES), 0)
    first_chunk = chunk_id == 0

    def load_groups(col, width=LANES):
        return [z_ref[r * SUBLANES:(r + 1) * SUBLANES, col:col + width] for r in range(nrow)]

    def put(col, width, val):
        op_scr[:, col:col + width] = val.astype(BF16)

    def store_groups(col, rows, width=LANES):
        put(col, width, jnp.concatenate(rows, axis=0))

    def history(tail_group, carry_ref, j, c0):
        tail = pltpu.roll(tail_group, 1, 0)
        prev = jnp.where(first_chunk, carry_ref[j - 1, :, c0:c0 + LANES], tail)
        carry_ref[j - 1, :, c0:c0 + LANES] = tail
        return prev

    rate = _decay_rate(lam_ref[...])
    for n in range(N_RNN_BLOCKS):
        c0, c1 = n * RNN_BLOCK, (n + 1) * RNN_BLOCK
        xs = load_groups(c0)
        ext = [history(xs[nrow - j], conv_carry, j, c0) for j in range(CONV_W - 1, 0, -1)] + xs
        cw = [jnp.broadcast_to(convw_ref[k:k + 1, c0:c1], (SUBLANES, LANES)) for k in range(CONV_W)]
        cb = jnp.broadcast_to(convb_ref[:, c0:c1], (SUBLANES, LANES))
        xc = []
        for r in range(nrow):
            acc = cb + cw[0] * ext[r]
            for k in range(1, CONV_W):
                acc = acc + cw[k] * ext[r + k]
            xc.append(acc)
        a, b = _rglru_block(jnp.concatenate(xc, axis=0), wax_ref[n], ba_ref[:, c0:c1],
                            bx_ref[:, c0:c1], rate[:, c0:c1])
        ac_scr[:, c0:c1] = a
        hl_scr[:, c0:c1] = b

    side_probs = _sample_attn_probs(side_scores)

    acc_a = ac_scr[0:SUBLANES, :]
    acc_h = hl_scr[0:SUBLANES, :]
    for r in range(1, nrow):
        rows = slice(r * SUBLANES, (r + 1) * SUBLANES)
        ar = ac_scr[rows, :]
        acc_h = ar * acc_h + hl_scr[rows, :]
        acc_a = ar * acc_a
        ac_scr[rows, :] = acc_a
        hl_scr[rows, :] = acc_h
    h_in = h_carry[...]
    entering = []
    for c in range(SUBLANES):
        entering.append(h_in)
        h_in = acc_a[c:c + 1] * h_in + acc_h[c:c + 1]
    h_carry[...] = h_in
    h_enter = jnp.concatenate(entering, axis=0)
    for n in range(N_RNN_BLOCKS):
        c0, c1 = n * RNN_BLOCK, (n + 1) * RNN_BLOCK
        gr = load_groups(D_RNN + c0)
        store_groups(c0, [(hl_scr[r * SUBLANES:(r + 1) * SUBLANES, c0:c1]
                           + ac_scr[r * SUBLANES:(r + 1) * SUBLANES, c0:c1] * h_enter[:, c0:c1])
                          * _silu(gr[r]) for r in range(nrow)])

    _sample_attn_values(side_probs, sv_ref, sattn_ref.at[0])

    pcol = 2 * D_RNN
    blocks = [(w, c0) for g, w in enumerate(POOL_WINDOWS)
              for c0 in range(g * POOL_GROUP, (g + 1) * POOL_GROUP, LANES)]

    def group(c0, r):
        return z_ref[r * SUBLANES:(r + 1) * SUBLANES, pcol + c0:pcol + c0 + LANES]

    def mean_minus_token(tot, w, c0, r):
        if r < w - 1:
            pos1 = l * tm + chunk_id * nrow + (r + 1)
            mean = tot / jnp.minimum(pos1, w).astype(F32)
        else:
            mean = tot * (1.0 / w)
        return mean - group(c0, r)

    hist, tot = {}, {}
    for w, c0 in blocks:
        hist[c0] = [history(group(c0, nrow - j), pool_carry, j, c0) for j in range(1, w)]
        t = group(c0, 0)
        for h in hist[c0]:
            t = t + h
        tot[c0] = t
        hl_scr[0:SUBLANES, c0:c0 + LANES] = mean_minus_token(t, w, c0, 0)
    for r in range(1, nrow):
        for w, c0 in blocks:
            leaving = group(c0, r - w) if r >= w else hist[c0][w - r - 1]
            tot[c0] = tot[c0] + (group(c0, r) - leaving)
            hl_scr[r * SUBLANES:(r + 1) * SUBLANES, c0:c0 + LANES] = mean_minus_token(
                tot[c0], w, c0, r)
    for g, w in enumerate(POOL_WINDOWS):
        c0, c1 = g * POOL_GROUP, (g + 1) * POOL_GROUP
        og = jnp.dot(hl_scr[:, c0:c1].astype(BF16), wpool_ref[g], preferred_element_type=F32)
        gp = z_ref[:, pcol + D_POOL + c0:pcol + D_POOL + c1]
        put(D_RNN + c0, POOL_GROUP, og * pscale_ref[:, c0:c1] * _silu(gp))

    qoff = 2 * D_RNN + 2 * D_POOL
    for hd in range(N_XHEADS):
        c0, c1 = hd * XHEAD_DIM, (hd + 1) * XHEAD_DIM
        q = z_ref[:, qoff + c0:qoff + c1].astype(BF16)
        s = lax.dot_general(q, kb_ref[:, c0:c1], (((1,), (1,)), ((), ())),
                            preferred_element_type=F32) * (XHEAD_DIM ** -0.5)
        p = jnp.exp(s - jnp.max(s, axis=-1, keepdims=True))
        p = p / jnp.sum(p, axis=-1, keepdims=True)
        ox = jnp.dot(p.astype(BF16), vb_ref[:, c0:c1], preferred_element_type=F32)
        gx = z_ref[:, qoff + D_X + c0:qoff + D_X + c1]
        put(D_RNN + D_POOL + c0, XHEAD_DIM, ox * _silu(gx))

    o_ref[...] = jnp.dot(unperm_ref[...], op_scr[...], preferred_element_type=F32).astype(BF16)

    @pl.when(l == last)
    def _():
        newh_ref[0] = h_carry[...]
        tail_row = lambda j: (nrow - j) * SUBLANES + SUBLANES - 1
        for j in range(1, CONV_W):
            newconv_ref[0, CONV_W - 1 - j:CONV_W - j, :] = z_ref[tail_row(j):tail_row(j) + 1, 0:D_RNN]
        for j in range(1, POOL_HIST + 1):
            newpool_ref[0, POOL_HIST - j:POOL_HIST - j + 1, :] = (
                z_ref[tail_row(j):tail_row(j) + 1, pcol:pcol + D_POOL])


def _prompt_mix(z, mem_k, mem_v, conv_w, conv_b, wax, b_a, b_x, lam, wpool, pscale, unperm,
                sample_q, cache_k, cache_v, batch, seq, tm):
    nl = seq // tm
    assert sample_q.shape[0] == batch * nl
    side = lambda b, l: (b * nl + l, 0, 0)
    zw = 2 * D_MIX
    const2 = lambda b, l: (0, 0)
    const3 = lambda b, l: (0, 0, 0)
    kern = functools.partial(_prompt_mix_kernel, tm=tm)
    return pl.pallas_call(
        kern,
        grid=(batch, nl),
        in_specs=[
            pl.BlockSpec(memory_space=pl.ANY),
            pl.BlockSpec((1, N_MEM, D_X), lambda b, l: (b, 0, 0)),
            pl.BlockSpec((1, N_MEM, D_X), lambda b, l: (b, 0, 0)),
            pl.BlockSpec((CONV_W, D_RNN), const2),
            pl.BlockSpec((1, D_RNN), const2),
            pl.BlockSpec((N_RNN_BLOCKS, RNN_BLOCK, 2 * RNN_BLOCK), const3),
            pl.BlockSpec((1, D_RNN), const2),
            pl.BlockSpec((1, D_RNN), const2),
            pl.BlockSpec((1, D_RNN), const2),
            pl.BlockSpec((len(POOL_WINDOWS), POOL_GROUP, POOL_GROUP), const3),
            pl.BlockSpec((1, D_POOL), const2),
            pl.BlockSpec((tm, tm), const2),
            pl.BlockSpec((1, ATTN_BB, D_X), side),
            pl.BlockSpec(memory_space=pl.ANY),
            pl.BlockSpec(memory_space=pl.ANY),
        ],
        out_specs=[
            pl.BlockSpec((tm, D_MIX), lambda b, l: (b * nl + l, 0)),
            pl.BlockSpec((1, 1, D_RNN), lambda b, l: (b, 0, 0)),
            pl.BlockSpec((1, CONV_W - 1, D_RNN), lambda b, l: (b, 0, 0)),
            pl.BlockSpec((1, POOL_HIST, D_POOL), lambda b, l: (b, 0, 0)),
            pl.BlockSpec((1, ATTN_BB, D_X), side),
        ],
        out_shape=[
            jax.ShapeDtypeStruct((batch * seq, D_MIX), BF16),
            jax.ShapeDtypeStruct((batch, 1, D_RNN), F32),
            jax.ShapeDtypeStruct((batch, CONV_W - 1, D_RNN), F32),
            jax.ShapeDtypeStruct((batch, POOL_HIST, D_POOL), F32),
            jax.ShapeDtypeStruct(sample_q.shape, F32),
        ],
        scratch_shapes=[
            pltpu.VMEM((CONV_W - 1, SUBLANES, D_RNN), F32),
            pltpu.VMEM((POOL_HIST, SUBLANES, D_POOL), F32),
            pltpu.VMEM((1, D_RNN), F32),
            pltpu.VMEM((N_MEM, D_X), BF16),
            pltpu.VMEM((N_MEM, D_X), BF16),
            pltpu.VMEM((tm, D_RNN), F32),
            pltpu.VMEM((tm, D_RNN), F32),
            pltpu.VMEM((tm, D_MIX), BF16),
            pltpu.VMEM((CACHE_RING, ATTN_BB, N_MEM * SUBLANES, LANES), F32),
            pltpu.VMEM((CACHE_RING, ATTN_BB, N_MEM * SUBLANES, LANES), F32),
            pltpu.SemaphoreType.DMA((CACHE_RING, 2)),
            pltpu.VMEM((CACHE_RING, tm, zw), F32),
            pltpu.SemaphoreType.DMA((CACHE_RING,)),
        ],
        compiler_params=pltpu.CompilerParams(
            dimension_semantics=("arbitrary", "arbitrary"),
            vmem_limit_bytes=BIG_VMEM_LIMIT),
        name="prompt_mix",
    )(z, mem_k, mem_v, conv_w, conv_b, wax, b_a, b_x, lam, wpool, pscale, unperm,
      sample_q, cache_k, cache_v)


def _cache_rows(c):
    nb = c.shape[0]
    c = c.reshape(nb, N_MEM, N_XHEADS, XHEAD_DIM // LANES, LANES)
    return c.transpose(0, 1, 3, 2, 4).reshape(nb, N_MEM * SUBLANES, LANES)


def _sample_attn_scores(q_ref, k_ref, bb):
    halves = XHEAD_DIM // LANES
    assert halves * N_XHEADS == SUBLANES
    scores = []
    for j in range(bb):
        qn = jnp.concatenate(
            [q_ref[j:j + 1, (h * halves + t) * LANES:(h * halves + t + 1) * LANES]
             for t in range(halves) for h in range(N_XHEADS)], axis=0)
        scores.append(lax.dot_general(qn.astype(BF16), k_ref[j].astype(BF16),
                                      (((1,), (1,)), ((), ())), preferred_element_type=F32)
                      * (XHEAD_DIM ** -0.5))
    return scores


def _sample_attn_probs(scores):
    r = lax.broadcasted_iota(jnp.int32, (SUBLANES, LANES), 0)
    c = lax.broadcasted_iota(jnp.int32, (SUBLANES, LANES), 1)
    diag = (c % SUBLANES) == r
    first_half = r < N_XHEADS
    nchunk = N_MEM * SUBLANES // LANES
    probs = []
    for s in scores:
        chunks = []
        for ci in range(nchunk):
            sm = jnp.where(diag, s[:, ci * LANES:(ci + 1) * LANES], 0.0)
            other = pltpu.roll(sm, N_XHEADS, 0)
            other = jnp.where(first_half, pltpu.roll(other, LANES - N_XHEADS, 1),
                              pltpu.roll(other, N_XHEADS, 1))
            chunks.append(jnp.where(diag, sm + other, -jnp.inf))
        t_full = jnp.concatenate(chunks, axis=1)
        e = jnp.exp(t_full - jnp.max(t_full, axis=1, keepdims=True))
        probs.append((e / jnp.sum(e, axis=1, keepdims=True)).astype(BF16))
    return probs


def _sample_attn_values(probs, v_ref, o_ref):
    halves = XHEAD_DIM // LANES
    for j, p in enumerate(probs):
        o = jnp.dot(p, v_ref[j].astype(BF16), preferred_element_type=F32)
        for t in range(halves):
            for h in range(N_XHEADS):
                col = (h * halves + t) * LANES
                o_ref[j:j + 1, col:col + LANES] = o[t * N_XHEADS + h:t * N_XHEADS + h + 1, :]


def _prompt_proj_kernel(x_ref, g_ref, w_ref, perm_ref, o_ref, u_ref, up_ref, *, mix_steps):
    j = pl.program_id(1)

    @pl.when(j == 0)
    def _():
        x = x_ref[...]
        u = (x * _rms_scale(x) * g_ref[...]).astype(BF16)
        u_ref[...] = u
        for r0 in range(0, u.shape[0], MIX_TM):
            up_ref[r0:r0 + MIX_TM, :] = jnp.dot(
                perm_ref[...], u[r0:r0 + MIX_TM], preferred_element_type=F32).astype(BF16)

    @pl.when(j < mix_steps)
    def _():
        o_ref[...] = jnp.dot(up_ref[...], w_ref[...], preferred_element_type=F32)

    @pl.when(j >= mix_steps)
    def _():
        o_ref[...] = jnp.dot(u_ref[...], w_ref[...], preferred_element_type=F32)


def _prompt_proj(x, g, w, perm, tm, tn):
    m, k = x.shape
    n = w.shape[1]
    assert (2 * D_MIX) % tn == 0
    return pl.pallas_call(
        functools.partial(_prompt_proj_kernel, mix_steps=2 * D_MIX // tn),
        grid=(m // tm, n // tn),
        in_specs=[
            pl.BlockSpec((tm, k), lambda i, j: (i, 0)),
            pl.BlockSpec((1, k), lambda i, j: (0, 0)),
            pl.BlockSpec((k, tn), lambda i, j: (0, j)),
            pl.BlockSpec(perm.shape, lambda i, j: (0, 0)),
        ],
        out_specs=pl.BlockSpec((tm, tn), lambda i, j: (i, j)),
        out_shape=jax.ShapeDtypeStruct((m, n), F32),
        scratch_shapes=[pltpu.VMEM((tm, k), BF16), pltpu.VMEM((tm, k), BF16)],
        compiler_params=pltpu.CompilerParams(
            dimension_semantics=("arbitrary", "arbitrary"),
            vmem_limit_bytes=BIG_VMEM_LIMIT),
        name="prompt_proj",
    )(x, g, w, perm)


def _sample_mix_kernel(z_ref, attn_ref, conv_ref, h_ref, pool_ref,
                       convw_ref, convb_ref, wax_ref, ba_ref, bx_ref, lam_ref, wpool_ref,
                       pscale_ref, o_ref, newh_ref, newconv_ref, newpool_ref):
    xr = z_ref[:, 0:D_RNN]
    xc = convb_ref[...] + convw_ref[CONV_W - 1:CONV_W, :] * xr
    for k in range(CONV_W - 1):
        xc = xc + convw_ref[k:k + 1, :] * conv_ref[k]
    for k in range(CONV_W - 2):
        newconv_ref[k] = conv_ref[k + 1]
    newconv_ref[CONV_W - 2] = xr

    rate = _decay_rate(lam_ref[...])
    for n in range(N_RNN_BLOCKS):
        c0, c1 = n * RNN_BLOCK, (n + 1) * RNN_BLOCK
        a, b = _rglru_block(xc[:, c0:c1], wax_ref[n], ba_ref[:, c0:c1], bx_ref[:, c0:c1],
                            rate[:, c0:c1])
        h = a * h_ref[:, c0:c1] + b
        newh_ref[:, c0:c1] = h
        o_ref[:, c0:c1] = (h * _silu(z_ref[:, D_RNN + c0:D_RNN + c1])).astype(BF16)

    xp = z_ref[:, 2 * D_RNN:2 * D_RNN + D_POOL]
    for k in range(POOL_HIST - 1):
        newpool_ref[k] = pool_ref[k + 1]
    newpool_ref[POOL_HIST - 1] = xp
    for g, w in enumerate(POOL_WINDOWS):
        c0, c1 = g * POOL_GROUP, (g + 1) * POOL_GROUP
        xg = xp[:, c0:c1]
        tot = xg
        for j in range(1, w):
            tot = tot + pool_ref[POOL_HIST - j, :, c0:c1]
        cnt = float(min(PAST_LEN + 1, w))
        d = tot / cnt - xg
        og = jnp.dot(d.astype(BF16), wpool_ref[g], preferred_element_type=F32)
        gp = z_ref[:, 2 * D_RNN + D_POOL + c0:2 * D_RNN + D_POOL + c1]
        o_ref[:, D_RNN + c0:D_RNN + c1] = (og * pscale_ref[:, c0:c1] * _silu(gp)).astype(BF16)

    gx = z_ref[:, 2 * D_RNN + 2 * D_POOL + D_X:2 * D_MIX]
    o_ref[:, D_RNN + D_POOL:] = (attn_ref[...] * _silu(gx)).astype(BF16)


def _sample_mix(z, attn, conv, h, pool, conv_w, conv_b, wax, b_a, b_x, lam, wpool, pscale, tb):
    nb = z.shape[0]
    zw = 2 * D_MIX
    rows = lambda i: (i, 0)
    const2 = lambda i: (0, 0)
    const3 = lambda i: (0, 0, 0)
    hist = lambda i: (0, i, 0)
    return pl.pallas_call(
        _sample_mix_kernel,
        grid=(nb // tb,),
        in_specs=[
            pl.BlockSpec((tb, zw), rows),
            pl.BlockSpec((tb, D_X), rows),
            pl.BlockSpec((CONV_W - 1, tb, D_RNN), hist),
            pl.BlockSpec((tb, D_RNN), rows),
            pl.BlockSpec((POOL_HIST, tb, D_POOL), hist),
            pl.BlockSpec((CONV_W, D_RNN), const2),
            pl.BlockSpec((1, D_RNN), const2),
            pl.BlockSpec((N_RNN_BLOCKS, RNN_BLOCK, 2 * RNN_BLOCK), const3),
            pl.BlockSpec((1, D_RNN), const2),
            pl.BlockSpec((1, D_RNN), const2),
            pl.BlockSpec((1, D_RNN), const2),
            pl.BlockSpec((len(POOL_WINDOWS), POOL_GROUP, POOL_GROUP), const3),
            pl.BlockSpec((1, D_POOL), const2),
        ],
        out_specs=[
            pl.BlockSpec((tb, D_MIX), rows),
            pl.BlockSpec((tb, D_RNN), rows),
            pl.BlockSpec((CONV_W - 1, tb, D_RNN), hist),
            pl.BlockSpec((POOL_HIST, tb, D_POOL), hist),
        ],
        out_shape=[
            jax.ShapeDtypeStruct((nb, D_MIX), BF16),
            jax.ShapeDtypeStruct((nb, D_RNN), F32),
            jax.ShapeDtypeStruct((CONV_W - 1, nb, D_RNN), F32),
            jax.ShapeDtypeStruct((POOL_HIST, nb, D_POOL), F32),
        ],
        compiler_params=pltpu.CompilerParams(
            dimension_semantics=("arbitrary",),
            vmem_limit_bytes=VMEM_LIMIT),
        name="sample_mix",
    )(z, attn, conv, h, pool, conv_w, conv_b, wax, b_a, b_x, lam, wpool, pscale)


def _branch_out_kernel(o_ref, gates_ref, x_ref, wb_ref, wo_ref, gpost_ref, y_ref):
    merged = None
    for j, (r0, r1) in enumerate(((0, D_RNN), (D_RNN, D_RNN + D_POOL), (D_RNN + D_POOL, D_MIX))):
        yj = jnp.dot(o_ref[:, r0:r1], wb_ref[r0:r1, :], preferred_element_type=F32)
        term = _sigmoid(gates_ref[:, j * D_MODEL:(j + 1) * D_MODEL]) * yj
        merged = term if merged is None else merged + term
    out = jnp.dot(merged.astype(BF16), wo_ref[...], preferred_element_type=F32)
    y_ref[...] = x_ref[...] + (out * gpost_ref[...]) * _rms_scale(out)


def _branch_out(o, z, x, wb, wo, g_post, tm):
    m = x.shape[0]
    gw = N_BRANCH * D_MODEL
    gblk = (2 * D_MIX) // gw
    resident = pl.Buffered(1)
    return pl.pallas_call(
        _branch_out_kernel,
        grid=(m // tm,),
        in_specs=[
            pl.BlockSpec((tm, D_MIX), lambda i: (i, 0)),
            pl.BlockSpec((tm, gw), lambda i: (i, gblk)),
            pl.BlockSpec((tm, D_MODEL), lambda i: (i, 0)),
            pl.BlockSpec((D_MIX, D_MODEL), lambda i: (0, 0), pipeline_mode=resident),
            pl.BlockSpec((D_MODEL, D_MODEL), lambda i: (0, 0), pipeline_mode=resident),
            pl.BlockSpec((1, D_MODEL), lambda i: (0, 0)),
        ],
        out_specs=pl.BlockSpec((tm, D_MODEL), lambda i: (i, 0)),
        out_shape=jax.ShapeDtypeStruct((m, D_MODEL), F32),
        compiler_params=pltpu.CompilerParams(
            dimension_semantics=("arbitrary",),
            vmem_limit_bytes=VMEM_LIMIT),
        name="branch_out",
    )(o, z, x, wb, wo, g_post)


WROWS = 1024
PER_BRANCH = D_RNN // WROWS
assert D_RNN == D_POOL == D_X and D_RNN % WROWS == 0 and D_MODEL % WROWS == 0
N_WB_BLOCKS = N_BRANCH * PER_BRANCH
N_WOUT_BLOCKS = D_MODEL // WROWS


def _branch_out_cast_kernel(o_ref, gates_ref, x_ref, wb_ref, wo_ref, gpost_ref,
                            y_ref, wbb_ref, wob_ref, merged_ref, out_ref):
    s = pl.program_id(0)

    @pl.when(s < N_WB_BLOCKS)
    def _():
        w = wb_ref[...].astype(BF16)
        wbb_ref[...] = w
        term = _sigmoid(gates_ref[...]) * jnp.dot(o_ref[...], w, preferred_element_type=F32)

        @pl.when(s == 0)
        def _():
            merged_ref[...] = term

        @pl.when(s > 0)
        def _():
            merged_ref[...] += term

    for kb in range(N_WOUT_BLOCKS):
        @pl.when(s == N_WB_BLOCKS + kb)
        def _(kb=kb):
            w = wo_ref[...].astype(BF16)
            wob_ref[...] = w
            part = jnp.dot(merged_ref[:, kb * WROWS:(kb + 1) * WROWS].astype(BF16), w,
                           preferred_element_type=F32)
            if kb == 0:
                out_ref[...] = part
            else:
                out_ref[...] += part

    @pl.when(s == N_WB_BLOCKS + N_WOUT_BLOCKS - 1)
    def _():
        out = out_ref[...]
        y_ref[...] = x_ref[...] + out * _rms_scale(out) * gpost_ref[...]


def _branch_out_cast(o, z, x, wb, wo, g_post):
    m = x.shape[0]
    gblk0 = (2 * D_MIX) // D_MODEL
    wb_blk = lambda s: jnp.minimum(s, N_WB_BLOCKS - 1)
    wo_blk = lambda s: jnp.maximum(s - N_WB_BLOCKS, 0)
    return pl.pallas_call(
        _branch_out_cast_kernel,
        grid=(N_WB_BLOCKS + N_WOUT_BLOCKS,),
        in_specs=[
            pl.BlockSpec((m, WROWS), lambda s: (0, wb_blk(s))),
            pl.BlockSpec((m, D_MODEL), lambda s: (0, gblk0 + wb_blk(s) // PER_BRANCH)),
            pl.BlockSpec((m, D_MODEL), lambda s: (0, 0)),
            pl.BlockSpec((WROWS, D_MODEL), lambda s: (wb_blk(s), 0)),
            pl.BlockSpec((WROWS, D_MODEL), lambda s: (wo_blk(s), 0)),
            pl.BlockSpec((1, D_MODEL), lambda s: (0, 0)),
        ],
        out_specs=[
            pl.BlockSpec((m, D_MODEL), lambda s: (0, 0)),
            pl.BlockSpec((WROWS, D_MODEL), lambda s: (wb_blk(s), 0)),
            pl.BlockSpec((WROWS, D_MODEL), lambda s: (wo_blk(s), 0)),
        ],
        out_shape=[
            jax.ShapeDtypeStruct((m, D_MODEL), F32),
            jax.ShapeDtypeStruct(wb.shape, BF16),
            jax.ShapeDtypeStruct(wo.shape, BF16),
        ],
        scratch_shapes=[pltpu.VMEM((m, D_MODEL), F32), pltpu.VMEM((m, D_MODEL), F32)],
        compiler_params=pltpu.CompilerParams(
            dimension_semantics=("arbitrary",),
            vmem_limit_bytes=BIG_VMEM_LIMIT),
        name="branch_out_cast",
    )(o, z, x, wb, wo, g_post)


def kernel(x_prompt, x_sample, mem_prompt, state_rglru_h, state_conv, state_pool, cache_mem_k, cache_mem_v, g_pre, w_in, conv_w, conv_b, w_rg_a, b_rg_a, w_rg_x, b_rg_x, lru_lambda, w_pool, pool_scale, g_mem, w_kv, w_branch, w_out, g_post):
    batch, seq, _ = x_prompt.shape
    nb = x_sample.shape[0]
    depth = g_pre.shape[0]
    assert depth == 1 and x_sample.shape[1] == 1

    l = 0
    row = lambda v: v.reshape(1, -1)
    wax = jnp.concatenate([w_rg_a[l], w_rg_x[l]], axis=-1).astype(BF16)
    wpool = w_pool[l].astype(BF16)
    mix_params = (conv_w[l], row(conv_b[l]), wax, row(b_rg_a[l]), row(b_rg_x[l]),
                  row(lru_lambda[l]), wpool, row(pool_scale[l]))

    xp2 = x_prompt.reshape(batch * seq, D_MODEL)
    xs2 = x_sample.reshape(nb, D_MODEL)
    mem2 = mem_prompt.reshape(batch * N_MEM, D_MODEL)

    z_s, w_in_b, mem_k, mem_v = _sample_proj(xs2, row(g_pre[l]), w_in[l], mem2, row(g_mem[l]),
                                             w_kv[l], tn=SAMPLE_PROJ_TN)
    qoff = 2 * D_RNN + 2 * D_POOL
    q_s = z_s[:, qoff:qoff + D_X].reshape(nb // ATTN_BB, ATTN_BB, D_X)

    mem_k = mem_k.reshape(batch, N_MEM, D_X)
    mem_v = mem_v.reshape(batch, N_MEM, D_X)

    perm = _chunk_interleave()
    z_p = _prompt_proj(xp2, row(g_pre[l]), w_in_b, perm, tm=PROJ_TM, tn=PROJ_TN)
    o_p, h_p, c_p, p_p, attn_s = _prompt_mix(
        z_p, mem_k, mem_v, *mix_params, perm.T, q_s, _cache_rows(cache_mem_k[l]),
        _cache_rows(cache_mem_v[l]), batch=batch, seq=seq, tm=MIX_TM)
    attn_s = attn_s.reshape(nb, D_X)

    o_s, h_s, c_s, p_s = _sample_mix(
        z_s, attn_s, state_conv[l].transpose(1, 0, 2), state_rglru_h[l],
        state_pool[l].transpose(1, 0, 2), *mix_params, tb=SAMPLE_MIX_TB)
    y_s, w_br_b, w_out_b = _branch_out_cast(o_s, z_s, xs2, w_branch[l], w_out[l], row(g_post[l]))

    y_p = _branch_out(o_p, z_p, xp2, w_br_b, w_out_b, row(g_post[l]), tm=BRANCH_TM)

    return (
        y_p.reshape(batch, seq, D_MODEL),
        y_s.reshape(nb, 1, D_MODEL),
        h_p.reshape(1, batch, D_RNN),
        c_p.reshape(1, batch, CONV_W - 1, D_RNN),
        p_p.reshape(1, batch, POOL_HIST, D_POOL),
        mem_k.reshape(1, batch, N_MEM, N_XHEADS, XHEAD_DIM),
        mem_v.reshape(1, batch, N_MEM, N_XHEADS, XHEAD_DIM),
        h_s.reshape(1, nb, D_RNN),
        c_s.transpose(1, 0, 2)[None],
        p_s.transpose(1, 0, 2)[None],
    )
```

```python
import functools

import jax
import jax.numpy as jnp
from jax import lax
from jax.experimental import pallas as pl
from jax.experimental.pallas import tpu as pltpu

D_MODEL = 2048
PAST_LEN = 16384
D_RNN = 1024
N_RNN_BLOCKS = 8
RNN_BLOCK = D_RNN // N_RNN_BLOCKS
CONV_W = 4
LRU_C = 8.0
D_POOL = 1024
POOL_WINDOWS = (2, 4, 8, 16)
POOL_GROUP = D_POOL // len(POOL_WINDOWS)
POOL_HIST = max(POOL_WINDOWS) - 1
N_MEM = 256
N_XHEADS = 4
XHEAD_DIM = 256
D_X = N_XHEADS * XHEAD_DIM
N_BRANCH = 3
D_MIX = D_RNN + D_POOL + D_X
D_IN = 2 * D_MIX + N_BRANCH * D_MODEL
EPS = 1e-6

SUBLANES = 8
LANES = 128
VMEM_LIMIT = 56 * 1024 * 1024
BIG_VMEM_LIMIT = 60 * 1024 * 1024
MIX_TM = 256
BRANCH_TM = 256
PROJ_TM, PROJ_TN = 1024, 2048
SAMPLE_PROJ_TN = 768
ATTN_BB = 4
CACHE_RING = 3
W_RING = 4
BOC_RING = 3
SAMPLE_MIX_TB = 64

BF16 = jnp.bfloat16
F32 = jnp.float32

NEG_LOG2_E = -1.4426950408889634


def _sigmoid(x):
    return 1.0 / (1.0 + jnp.exp2(x * NEG_LOG2_E))


def _silu(x):
    return x * _sigmoid(x)


def _softplus(x):
    return jnp.maximum(x, 0.0) + jnp.log1p(jnp.exp(-jnp.abs(x)))


def _rms_scale(x):
    return lax.rsqrt(jnp.mean(x * x, axis=-1, keepdims=True) + EPS)


def _chunk_interleave():
    nrow = MIX_TM // SUBLANES
    p = jnp.arange(MIX_TM)
    token = (p % SUBLANES) * nrow + p // SUBLANES
    return (token[:, None] == jnp.arange(MIX_TM)[None, :]).astype(BF16)


def _sample_proj_kernel(x_ref, g_ref, w_hbm, mem_ref, gm_ref, wkv_ref,
                        o_ref, wb_ref, k_ref, v_ref, u_ref, um_ref, w_ring, ring_sem,
                        *, k_steps, tn):
    j = pl.program_id(0)
    nsteps = pl.num_programs(0)

    def w_copy(blk, slot):
        return pltpu.make_async_copy(w_hbm.at[:, pl.ds(blk * tn, tn)], w_ring.at[slot],
                                     ring_sem.at[slot])

    @pl.when(j == 0)
    def _():
        for first in range(W_RING - 1):
            w_copy(first, first).start()

    ahead = j + (W_RING - 1)

    @pl.when(ahead < nsteps)
    def _():
        w_copy(ahead, ahead % W_RING).start()

    @pl.when(j == 0)
    def _():
        x = x_ref[...]
        u_ref[...] = (x * _rms_scale(x) * g_ref[...]).astype(BF16)
        mem = mem_ref[...]
        um_ref[...] = (mem * _rms_scale(mem) * gm_ref[...]).astype(BF16)

    slot = j % W_RING
    w_copy(j, slot).wait()
    w = w_ring[slot].astype(BF16)
    wb_ref[...] = w
    o_ref[...] = jnp.dot(u_ref[...], w, preferred_element_type=F32)

    kv = jnp.dot(um_ref[...], wkv_ref[...].astype(BF16), preferred_element_type=F32)

    @pl.when(j < k_steps)
    def _():
        k_ref[...] = kv

    @pl.when(j >= k_steps)
    def _():
        v_ref[...] = kv


def _sample_proj(x, g, w, mem, g_mem, w_kv, tn):
    m, k = x.shape
    n = w.shape[1]
    steps = n // tn
    mrows = mem.shape[0]
    kv_tn = 2 * D_X // steps
    assert D_X % kv_tn == 0 and kv_tn % LANES == 0
    k_steps = D_X // kv_tn
    return pl.pallas_call(
        functools.partial(_sample_proj_kernel, k_steps=k_steps, tn=tn),
        grid=(steps,),
        in_specs=[
            pl.BlockSpec((m, k), lambda j: (0, 0)),
            pl.BlockSpec((1, k), lambda j: (0, 0)),
            pl.BlockSpec(memory_space=pl.ANY),
            pl.BlockSpec((mrows, k), lambda j: (0, 0), pipeline_mode=pl.Buffered(1)),
            pl.BlockSpec((1, k), lambda j: (0, 0)),
            pl.BlockSpec((k, kv_tn), lambda j: (0, j)),
        ],
        out_specs=[
            pl.BlockSpec((m, tn), lambda j: (0, j)),
            pl.BlockSpec((k, tn), lambda j: (0, j)),
            pl.BlockSpec((mrows, kv_tn), lambda j: (0, jnp.minimum(j, k_steps - 1))),
            pl.BlockSpec((mrows, kv_tn), lambda j: (0, jnp.maximum(j - k_steps, 0))),
        ],
        out_shape=[
            jax.ShapeDtypeStruct((m, n), F32),
            jax.ShapeDtypeStruct((k, n), BF16),
            jax.ShapeDtypeStruct((mrows, D_X), F32),
            jax.ShapeDtypeStruct((mrows, D_X), F32),
        ],
        scratch_shapes=[pltpu.VMEM((m, k), BF16), pltpu.VMEM((mrows, k), BF16),
                        pltpu.VMEM((W_RING, k, tn), F32), pltpu.SemaphoreType.DMA((W_RING,))],
        compiler_params=pltpu.CompilerParams(
            dimension_semantics=("arbitrary",),
            vmem_limit_bytes=VMEM_LIMIT),
        name="sample_proj",
    )(x, g, w, mem, g_mem, w_kv)


def _decay_rate(lam):
    return _softplus(-lam) * (LRU_C * NEG_LOG2_E)


def _rglru_block(xc, wax, ba, bx, rate):
    ri = jnp.dot(xc.astype(BF16), wax, preferred_element_type=F32)
    r = _sigmoid(ri[:, :RNN_BLOCK] + ba)
    i = _sigmoid(ri[:, RNN_BLOCK:] + bx)
    a = jnp.exp2(r * rate)
    one_m = 1.0 - a * a
    mult = jnp.where(one_m > 0.0, one_m * lax.rsqrt(one_m), 0.0)
    return a, mult * i * xc


def _prompt_mix_kernel(z_ref, k_ref, v_ref, convw_ref, convb_ref, wax_ref, ba_ref, bx_ref,
                       lam_ref, wpool_ref, pscale_ref, unperm_ref, sq_ref, sk_hbm, sv_hbm,
                       o_ref, newh_ref, newconv_ref, newpool_ref, sattn_ref,
                       conv_carry, pool_carry, h_carry, kb_ref, vb_ref, ac_scr, hl_scr, op_scr,
                       sk_ring, sv_ring, ring_sem, *, tm):
    l = pl.program_id(1)
    last = pl.num_programs(1) - 1
    nrow = tm // SUBLANES

    @pl.when(l == 0)
    def _():
        conv_carry[...] = jnp.zeros(conv_carry.shape, F32)
        pool_carry[...] = jnp.zeros(pool_carry.shape, F32)
        h_carry[...] = jnp.zeros(h_carry.shape, F32)
        kb_ref[...] = k_ref[0].astype(BF16)
        vb_ref[...] = v_ref[0].astype(BF16)

    step = pl.program_id(0) * pl.num_programs(1) + l
    nsteps = pl.num_programs(0) * pl.num_programs(1)

    def cache_copies(blk, slot):
        rows = pl.ds(blk * ATTN_BB, ATTN_BB)
        return (pltpu.make_async_copy(sk_hbm.at[rows], sk_ring.at[slot], ring_sem.at[slot, 0]),
                pltpu.make_async_copy(sv_hbm.at[rows], sv_ring.at[slot], ring_sem.at[slot, 1]))

    @pl.when(step == 0)
    def _():
        for first in range(CACHE_RING - 1):
            for cp in cache_copies(first, first):
                cp.start()

    ahead = step + (CACHE_RING - 1)

    @pl.when(ahead < nsteps)
    def _():
        for cp in cache_copies(ahead, ahead % CACHE_RING):
            cp.start()

    slot = step % CACHE_RING
    for cp in cache_copies(step, slot):
        cp.wait()
    sk_ref = sk_ring.at[slot]
    sv_ref = sv_ring.at[slot]
    side_scores = _sample_attn_scores(sq_ref.at[0], sk_ref, ATTN_BB)

    chunk_id = lax.broadcasted_iota(jnp.int32, (SUBLANES, LANES), 0)
    first_chunk = chunk_id == 0

    def load_groups(col, width=LANES):
        return [z_ref[r * SUBLANES:(r + 1) * SUBLANES, col:col + width] for r in range(nrow)]

    def put(col, width, val):
        op_scr[:, col:col + width] = val.astype(BF16)

    def store_groups(col, rows, width=LANES):
        put(col, width, jnp.concatenate(rows, axis=0))

    def history(tail_group, carry_ref, j, c0):
        tail = pltpu.roll(tail_group, 1, 0)
        prev = jnp.where(first_chunk, carry_ref[j - 1, :, c0:c0 + LANES], tail)
        carry_ref[j - 1, :, c0:c0 + LANES] = tail
        return prev

    rate = _decay_rate(lam_ref[...])
    for n in range(N_RNN_BLOCKS):
        c0, c1 = n * RNN_BLOCK, (n + 1) * RNN_BLOCK
        xs = load_groups(c0)
        ext = [history(xs[nrow - j], conv_carry, j, c0) for j in range(CONV_W - 1, 0, -1)] + xs
        cw = [jnp.broadcast_to(convw_ref[k:k + 1, c0:c1], (SUBLANES, LANES)) for k in range(CONV_W)]
        cb = jnp.broadcast_to(convb_ref[:, c0:c1], (SUBLANES, LANES))
        xc = []
        for r in range(nrow):
            acc = cb + cw[0] * ext[r]
            for k in range(1, CONV_W):
                acc = acc + cw[k] * ext[r + k]
            xc.append(acc)
        a, b = _rglru_block(jnp.concatenate(xc, axis=0), wax_ref[n], ba_ref[:, c0:c1],
                            bx_ref[:, c0:c1], rate[:, c0:c1])
        ac_scr[:, c0:c1] = a
        hl_scr[:, c0:c1] = b

    side_probs = _sample_attn_probs(side_scores)

    acc_a = ac_scr[0:SUBLANES, :]
    acc_h = hl_scr[0:SUBLANES, :]
    for r in range(1, nrow):
        rows = slice(r * SUBLANES, (r + 1) * SUBLANES)
        ar = ac_scr[rows, :]
        acc_h = ar * acc_h + hl_scr[rows, :]
        acc_a = ar * acc_a
        ac_scr[rows, :] = acc_a
        hl_scr[rows, :] = acc_h
    h_in = h_carry[...]
    entering = []
    for c in range(SUBLANES):
        entering.append(h_in)
        h_in = acc_a[c:c + 1] * h_in + acc_h[c:c + 1]
    h_carry[...] = h_in
    h_enter = jnp.concatenate(entering, axis=0)
    for n in range(N_RNN_BLOCKS):
        c0, c1 = n * RNN_BLOCK, (n + 1) * RNN_BLOCK
        gr = load_groups(D_RNN + c0)
        store_groups(c0, [(hl_scr[r * SUBLANES:(r + 1) * SUBLANES, c0:c1]
                           + ac_scr[r * SUBLANES:(r + 1) * SUBLANES, c0:c1] * h_enter[:, c0:c1])
                          * _silu(gr[r]) for r in range(nrow)])

    _sample_attn_values(side_probs, sv_ref, sattn_ref.at[0])

    pcol = 2 * D_RNN
    blocks = [(w, c0) for g, w in enumerate(POOL_WINDOWS)
              for c0 in range(g * POOL_GROUP, (g + 1) * POOL_GROUP, LANES)]

    def group(c0, r):
        return z_ref[r * SUBLANES:(r + 1) * SUBLANES, pcol + c0:pcol + c0 + LANES]

    def mean_minus_token(tot, w, c0, r):
        if r < w - 1:
            pos1 = l * tm + chunk_id * nrow + (r + 1)
            mean = tot / jnp.minimum(pos1, w).astype(F32)
        else:
            mean = tot * (1.0 / w)
        return mean - group(c0, r)

    hist, tot = {}, {}
    for w, c0 in blocks:
        hist[c0] = [history(group(c0, nrow - j), pool_carry, j, c0) for j in range(1, w)]
        t = group(c0, 0)
        for h in hist[c0]:
            t = t + h
        tot[c0] = t
        hl_scr[0:SUBLANES, c0:c0 + LANES] = mean_minus_token(t, w, c0, 0)
    for r in range(1, nrow):
        for w, c0 in blocks:
            leaving = group(c0, r - w) if r >= w else hist[c0][w - r - 1]
            tot[c0] = tot[c0] + (group(c0, r) - leaving)
            hl_scr[r * SUBLANES:(r + 1) * SUBLANES, c0:c0 + LANES] = mean_minus_token(
                tot[c0], w, c0, r)
    for g, w in enumerate(POOL_WINDOWS):
        c0, c1 = g * POOL_GROUP, (g + 1) * POOL_GROUP
        og = jnp.dot(hl_scr[:, c0:c1].astype(BF16), wpool_ref[g], preferred_element_type=F32)
        gp = z_ref[:, pcol + D_POOL + c0:pcol + D_POOL + c1]
        put(D_RNN + c0, POOL_GROUP, og * pscale_ref[:, c0:c1] * _silu(gp))

    qoff = 2 * D_RNN + 2 * D_POOL
    for hd in range(N_XHEADS):
        c0, c1 = hd * XHEAD_DIM, (hd + 1) * XHEAD_DIM
        q = z_ref[:, qoff + c0:qoff + c1].astype(BF16)
        s = lax.dot_general(q, kb_ref[:, c0:c1], (((1,), (1,)), ((), ())),
                            preferred_element_type=F32) * (XHEAD_DIM ** -0.5)
        p = jnp.exp(s - jnp.max(s, axis=-1, keepdims=True))
        p = p / jnp.sum(p, axis=-1, keepdims=True)
        ox = jnp.dot(p.astype(BF16), vb_ref[:, c0:c1], preferred_element_type=F32)
        gx = z_ref[:, qoff + D_X + c0:qoff + D_X + c1]
        put(D_RNN + D_POOL + c0, XHEAD_DIM, ox * _silu(gx))

    o_ref[...] = jnp.dot(unperm_ref[...], op_scr[...], preferred_element_type=F32).astype(BF16)

    @pl.when(l == last)
    def _():
        newh_ref[0] = h_carry[...]
        tail_row = lambda j: (nrow - j) * SUBLANES + SUBLANES - 1
        for j in range(1, CONV_W):
            newconv_ref[0, CONV_W - 1 - j:CONV_W - j, :] = z_ref[tail_row(j):tail_row(j) + 1, 0:D_RNN]
        for j in range(1, POOL_HIST + 1):
            newpool_ref[0, POOL_HIST - j:POOL_HIST - j + 1, :] = (
                z_ref[tail_row(j):tail_row(j) + 1, pcol:pcol + D_POOL])


def _prompt_mix(z, mem_k, mem_v, conv_w, conv_b, wax, b_a, b_x, lam, wpool, pscale, unperm,
                sample_q, cache_k, cache_v, batch, seq, tm):
    nl = seq // tm
    assert sample_q.shape[0] == batch * nl
    side = lambda b, l: (b * nl + l, 0, 0)
    zw = 2 * D_MIX
    const2 = lambda b, l: (0, 0)
    const3 = lambda b, l: (0, 0, 0)
    kern = functools.partial(_prompt_mix_kernel, tm=tm)
    return pl.pallas_call(
        kern,
        grid=(batch, nl),
        in_specs=[
            pl.BlockSpec((tm, zw), lambda b, l: (b * nl + l, 0)),
            pl.BlockSpec((1, N_MEM, D_X), lambda b, l: (b, 0, 0)),
            pl.BlockSpec((1, N_MEM, D_X), lambda b, l: (b, 0, 0)),
            pl.BlockSpec((CONV_W, D_RNN), const2),
            pl.BlockSpec((1, D_RNN), const2),
            pl.BlockSpec((N_RNN_BLOCKS, RNN_BLOCK, 2 * RNN_BLOCK), const3),
            pl.BlockSpec((1, D_RNN), const2),
            pl.BlockSpec((1, D_RNN), const2),
            pl.BlockSpec((1, D_RNN), const2),
            pl.BlockSpec((len(POOL_WINDOWS), POOL_GROUP, POOL_GROUP), const3),
            pl.BlockSpec((1, D_POOL), const2),
            pl.BlockSpec((tm, tm), const2),
            pl.BlockSpec((1, ATTN_BB, D_X), side),
            pl.BlockSpec(memory_space=pl.ANY),
            pl.BlockSpec(memory_space=pl.ANY),
        ],
        out_specs=[
            pl.BlockSpec((tm, D_MIX), lambda b, l: (b * nl + l, 0)),
            pl.BlockSpec((1, 1, D_RNN), lambda b, l: (b, 0, 0)),
            pl.BlockSpec((1, CONV_W - 1, D_RNN), lambda b, l: (b, 0, 0)),
            pl.BlockSpec((1, POOL_HIST, D_POOL), lambda b, l: (b, 0, 0)),
            pl.BlockSpec((1, ATTN_BB, D_X), side),
        ],
        out_shape=[
            jax.ShapeDtypeStruct((batch * seq, D_MIX), BF16),
            jax.ShapeDtypeStruct((batch, 1, D_RNN), F32),
            jax.ShapeDtypeStruct((batch, CONV_W - 1, D_RNN), F32),
            jax.ShapeDtypeStruct((batch, POOL_HIST, D_POOL), F32),
            jax.ShapeDtypeStruct(sample_q.shape, F32),
        ],
        scratch_shapes=[
            pltpu.VMEM((CONV_W - 1, SUBLANES, D_RNN), F32),
            pltpu.VMEM((POOL_HIST, SUBLANES, D_POOL), F32),
            pltpu.VMEM((1, D_RNN), F32),
            pltpu.VMEM((N_MEM, D_X), BF16),
            pltpu.VMEM((N_MEM, D_X), BF16),
            pltpu.VMEM((tm, D_RNN), F32),
            pltpu.VMEM((tm, D_RNN), F32),
            pltpu.VMEM((tm, D_MIX), BF16),
            pltpu.VMEM((CACHE_RING, ATTN_BB, N_MEM * SUBLANES, LANES), F32),
            pltpu.VMEM((CACHE_RING, ATTN_BB, N_MEM * SUBLANES, LANES), F32),
            pltpu.SemaphoreType.DMA((CACHE_RING, 2)),
        ],
        compiler_params=pltpu.CompilerParams(
            dimension_semantics=("arbitrary", "arbitrary"),
            vmem_limit_bytes=VMEM_LIMIT),
        name="prompt_mix",
    )(z, mem_k, mem_v, conv_w, conv_b, wax, b_a, b_x, lam, wpool, pscale, unperm,
      sample_q, cache_k, cache_v)


def _cache_rows(c):
    nb = c.shape[0]
    c = c.reshape(nb, N_MEM, N_XHEADS, XHEAD_DIM // LANES, LANES)
    return c.transpose(0, 1, 3, 2, 4).reshape(nb, N_MEM * SUBLANES, LANES)


def _sample_attn_scores(q_ref, k_ref, bb):
    halves = XHEAD_DIM // LANES
    assert halves * N_XHEADS == SUBLANES
    scores = []
    for j in range(bb):
        qn = jnp.concatenate(
            [q_ref[j:j + 1, (h * halves + t) * LANES:(h * halves + t + 1) * LANES]
             for t in range(halves) for h in range(N_XHEADS)], axis=0)
        scores.append(lax.dot_general(qn.astype(BF16), k_ref[j].astype(BF16),
                                      (((1,), (1,)), ((), ())), preferred_element_type=F32)
                      * (XHEAD_DIM ** -0.5))
    return scores


def _sample_attn_probs(scores):
    r = lax.broadcasted_iota(jnp.int32, (SUBLANES, LANES), 0)
    c = lax.broadcasted_iota(jnp.int32, (SUBLANES, LANES), 1)
    diag = (c % SUBLANES) == r
    first_half = r < N_XHEADS
    nchunk = N_MEM * SUBLANES // LANES
    probs = []
    for s in scores:
        chunks = []
        for ci in range(nchunk):
            sm = jnp.where(diag, s[:, ci * LANES:(ci + 1) * LANES], 0.0)
            other = pltpu.roll(sm, N_XHEADS, 0)
            other = jnp.where(first_half, pltpu.roll(other, LANES - N_XHEADS, 1),
                              pltpu.roll(other, N_XHEADS, 1))
            chunks.append(jnp.where(diag, sm + other, -jnp.inf))
        t_full = jnp.concatenate(chunks, axis=1)
        e = jnp.exp(t_full - jnp.max(t_full, axis=1, keepdims=True))
        probs.append((e / jnp.sum(e, axis=1, keepdims=True)).astype(BF16))
    return probs


def _sample_attn_values(probs, v_ref, o_ref):
    halves = XHEAD_DIM // LANES
    for j, p in enumerate(probs):
        o = jnp.dot(p, v_ref[j].astype(BF16), preferred_element_type=F32)
        for t in range(halves):
            for h in range(N_XHEADS):
                col = (h * halves + t) * LANES
                o_ref[j:j + 1, col:col + LANES] = o[t * N_XHEADS + h:t * N_XHEADS + h + 1, :]


def _prompt_proj_kernel(x_ref, g_ref, w_ref, perm_ref, o_ref, u_ref, up_ref, *, mix_steps):
    j = pl.program_id(1)

    @pl.when(j == 0)
    def _():
        x = x_ref[...]
        u = (x * _rms_scale(x) * g_ref[...]).astype(BF16)
        u_ref[...] = u
        for r0 in range(0, u.shape[0], MIX_TM):
            up_ref[r0:r0 + MIX_TM, :] = jnp.dot(
                perm_ref[...], u[r0:r0 + MIX_TM], preferred_element_type=F32).astype(BF16)

    @pl.when(j < mix_steps)
    def _():
        o_ref[...] = jnp.dot(up_ref[...], w_ref[...], preferred_element_type=F32)

    @pl.when(j >= mix_steps)
    def _():
        o_ref[...] = jnp.dot(u_ref[...], w_ref[...], preferred_element_type=F32)


def _prompt_proj(x, g, w, perm, tm, tn):
    m, k = x.shape
    n = w.shape[1]
    assert (2 * D_MIX) % tn == 0
    return pl.pallas_call(
        functools.partial(_prompt_proj_kernel, mix_steps=2 * D_MIX // tn),
        grid=(m // tm, n // tn),
        in_specs=[
            pl.BlockSpec((tm, k), lambda i, j: (i, 0)),
            pl.BlockSpec((1, k), lambda i, j: (0, 0)),
            pl.BlockSpec((k, tn), lambda i, j: (0, j)),
            pl.BlockSpec(perm.shape, lambda i, j: (0, 0)),
        ],
        out_specs=pl.BlockSpec((tm, tn), lambda i, j: (i, j)),
        out_shape=jax.ShapeDtypeStruct((m, n), F32),
        scratch_shapes=[pltpu.VMEM((tm, k), BF16), pltpu.VMEM((tm, k), BF16)],
        compiler_params=pltpu.CompilerParams(
            dimension_semantics=("arbitrary", "arbitrary"),
            vmem_limit_bytes=BIG_VMEM_LIMIT),
        name="prompt_proj",
    )(x, g, w, perm)


def _sample_mix_kernel(z_ref, attn_ref, conv_ref, h_ref, pool_ref,
                       convw_ref, convb_ref, wax_ref, ba_ref, bx_ref, lam_ref, wpool_ref,
                       pscale_ref, o_ref, newh_ref, newconv_ref, newpool_ref):
    xr = z_ref[:, 0:D_RNN]
    xc = convb_ref[...] + convw_ref[CONV_W - 1:CONV_W, :] * xr
    for k in range(CONV_W - 1):
        xc = xc + convw_ref[k:k + 1, :] * conv_ref[k]
    for k in range(CONV_W - 2):
        newconv_ref[k] = conv_ref[k + 1]
    newconv_ref[CONV_W - 2] = xr

    rate = _decay_rate(lam_ref[...])
    for n in range(N_RNN_BLOCKS):
        c0, c1 = n * RNN_BLOCK, (n + 1) * RNN_BLOCK
        a, b = _rglru_block(xc[:, c0:c1], wax_ref[n], ba_ref[:, c0:c1], bx_ref[:, c0:c1],
                            rate[:, c0:c1])
        h = a * h_ref[:, c0:c1] + b
        newh_ref[:, c0:c1] = h
        o_ref[:, c0:c1] = (h * _silu(z_ref[:, D_RNN + c0:D_RNN + c1])).astype(BF16)

    xp = z_ref[:, 2 * D_RNN:2 * D_RNN + D_POOL]
    for k in range(POOL_HIST - 1):
        newpool_ref[k] = pool_ref[k + 1]
    newpool_ref[POOL_HIST - 1] = xp
    for g, w in enumerate(POOL_WINDOWS):
        c0, c1 = g * POOL_GROUP, (g + 1) * POOL_GROUP
        xg = xp[:, c0:c1]
        tot = xg
        for j in range(1, w):
            tot = tot + pool_ref[POOL_HIST - j, :, c0:c1]
        cnt = float(min(PAST_LEN + 1, w))
        d = tot / cnt - xg
        og = jnp.dot(d.astype(BF16), wpool_ref[g], preferred_element_type=F32)
        gp = z_ref[:, 2 * D_RNN + D_POOL + c0:2 * D_RNN + D_POOL + c1]
        o_ref[:, D_RNN + c0:D_RNN + c1] = (og * pscale_ref[:, c0:c1] * _silu(gp)).astype(BF16)

    gx = z_ref[:, 2 * D_RNN + 2 * D_POOL + D_X:2 * D_MIX]
    o_ref[:, D_RNN + D_POOL:] = (attn_ref[...] * _silu(gx)).astype(BF16)


def _sample_mix(z, attn, conv, h, pool, conv_w, conv_b, wax, b_a, b_x, lam, wpool, pscale, tb):
    nb = z.shape[0]
    zw = 2 * D_MIX
    rows = lambda i: (i, 0)
    const2 = lambda i: (0, 0)
    const3 = lambda i: (0, 0, 0)
    hist = lambda i: (0, i, 0)
    return pl.pallas_call(
        _sample_mix_kernel,
        grid=(nb // tb,),
        in_specs=[
            pl.BlockSpec((tb, zw), rows),
            pl.BlockSpec((tb, D_X), rows),
            pl.BlockSpec((CONV_W - 1, tb, D_RNN), hist),
            pl.BlockSpec((tb, D_RNN), rows),
            pl.BlockSpec((POOL_HIST, tb, D_POOL), hist),
            pl.BlockSpec((CONV_W, D_RNN), const2),
            pl.BlockSpec((1, D_RNN), const2),
            pl.BlockSpec((N_RNN_BLOCKS, RNN_BLOCK, 2 * RNN_BLOCK), const3),
            pl.BlockSpec((1, D_RNN), const2),
            pl.BlockSpec((1, D_RNN), const2),
            pl.BlockSpec((1, D_RNN), const2),
            pl.BlockSpec((len(POOL_WINDOWS), POOL_GROUP, POOL_GROUP), const3),
            pl.BlockSpec((1, D_POOL), const2),
        ],
        out_specs=[
            pl.BlockSpec((tb, D_MIX), rows),
            pl.BlockSpec((tb, D_RNN), rows),
            pl.BlockSpec((CONV_W - 1, tb, D_RNN), hist),
            pl.BlockSpec((POOL_HIST, tb, D_POOL), hist),
        ],
        out_shape=[
            jax.ShapeDtypeStruct((nb, D_MIX), BF16),
            jax.ShapeDtypeStruct((nb, D_RNN), F32),
            jax.ShapeDtypeStruct((CONV_W - 1, nb, D_RNN), F32),
            jax.ShapeDtypeStruct((POOL_HIST, nb, D_POOL), F32),
        ],
        compiler_params=pltpu.CompilerParams(
            dimension_semantics=("arbitrary",),
            vmem_limit_bytes=VMEM_LIMIT),
        name="sample_mix",
    )(z, attn, conv, h, pool, conv_w, conv_b, wax, b_a, b_x, lam, wpool, pscale)


def _branch_out_kernel(o_ref, gates_ref, x_ref, wb_ref, wo_ref, gpost_ref, y_ref):
    merged = None
    for j, (r0, r1) in enumerate(((0, D_RNN), (D_RNN, D_RNN + D_POOL), (D_RNN + D_POOL, D_MIX))):
        yj = jnp.dot(o_ref[:, r0:r1], wb_ref[r0:r1, :], preferred_element_type=F32)
        term = _sigmoid(gates_ref[:, j * D_MODEL:(j + 1) * D_MODEL]) * yj
        merged = term if merged is None else merged + term
    out = jnp.dot(merged.astype(BF16), wo_ref[...], preferred_element_type=F32)
    y_ref[...] = x_ref[...] + (out * gpost_ref[...]) * _rms_scale(out)


def _branch_out(o, z, x, wb, wo, g_post, tm):
    m = x.shape[0]
    gw = N_BRANCH * D_MODEL
    gblk = (2 * D_MIX) // gw
    resident = pl.Buffered(1)
    return pl.pallas_call(
        _branch_out_kernel,
        grid=(m // tm,),
        in_specs=[
            pl.BlockSpec((tm, D_MIX), lambda i: (i, 0)),
            pl.BlockSpec((tm, gw), lambda i: (i, gblk)),
            pl.BlockSpec((tm, D_MODEL), lambda i: (i, 0)),
            pl.BlockSpec((D_MIX, D_MODEL), lambda i: (0, 0), pipeline_mode=resident),
            pl.BlockSpec((D_MODEL, D_MODEL), lambda i: (0, 0), pipeline_mode=resident),
            pl.BlockSpec((1, D_MODEL), lambda i: (0, 0)),
        ],
        out_specs=pl.BlockSpec((tm, D_MODEL), lambda i: (i, 0)),
        out_shape=jax.ShapeDtypeStruct((m, D_MODEL), F32),
        compiler_params=pltpu.CompilerParams(
            dimension_semantics=("arbitrary",),
            vmem_limit_bytes=VMEM_LIMIT),
        name="branch_out",
    )(o, z, x, wb, wo, g_post)


WROWS = 1024
PER_BRANCH = D_RNN // WROWS
assert D_RNN == D_POOL == D_X and D_RNN % WROWS == 0 and D_MODEL % WROWS == 0
N_WB_BLOCKS = N_BRANCH * PER_BRANCH
N_WOUT_BLOCKS = D_MODEL // WROWS


def _branch_out_cast_kernel(o_ref, gates_ref, x_ref, wb_hbm, wo_hbm, gpost_ref,
                            y_ref, wbb_ref, wob_ref, merged_ref, out_ref, w_ring, ring_sem):
    s = pl.program_id(0)
    nsteps = N_WB_BLOCKS + N_WOUT_BLOCKS

    def w_copy(blk):
        src, first = (wb_hbm, 0) if blk < N_WB_BLOCKS else (wo_hbm, N_WB_BLOCKS)
        slot = blk % BOC_RING
        return pltpu.make_async_copy(src.at[pl.ds((blk - first) * WROWS, WROWS)],
                                     w_ring.at[slot], ring_sem.at[slot])

    for t in range(nsteps):
        @pl.when(s == t)
        def _(t=t):
            if t == 0:
                for first in range(min(BOC_RING - 1, nsteps)):
                    w_copy(first).start()
            if t + BOC_RING - 1 < nsteps:
                w_copy(t + BOC_RING - 1).start()
            w_copy(t).wait()
            w = w_ring[t % BOC_RING].astype(BF16)
            if t < N_WB_BLOCKS:
                wbb_ref[...] = w
                term = _sigmoid(gates_ref[...]) * jnp.dot(o_ref[...], w,
                                                          preferred_element_type=F32)
                if t == 0:
                    merged_ref[...] = term
                else:
                    merged_ref[...] += term
            else:
                kb = t - N_WB_BLOCKS
                wob_ref[...] = w
                part = jnp.dot(merged_ref[:, kb * WROWS:(kb + 1) * WROWS].astype(BF16), w,
                               preferred_element_type=F32)
                if kb == 0:
                    out_ref[...] = part
                else:
                    out_ref[...] += part
            if t == nsteps - 1:
                out = out_ref[...]
                y_ref[...] = x_ref[...] + out * _rms_scale(out) * gpost_ref[...]


def _branch_out_cast(o, z, x, wb, wo, g_post):
    m = x.shape[0]
    gblk0 = (2 * D_MIX) // D_MODEL
    wb_blk = lambda s: jnp.minimum(s, N_WB_BLOCKS - 1)
    wo_blk = lambda s: jnp.maximum(s - N_WB_BLOCKS, 0)
    return pl.pallas_call(
        _branch_out_cast_kernel,
        grid=(N_WB_BLOCKS + N_WOUT_BLOCKS,),
        in_specs=[
            pl.BlockSpec((m, WROWS), lambda s: (0, wb_blk(s))),
            pl.BlockSpec((m, D_MODEL), lambda s: (0, gblk0 + wb_blk(s) // PER_BRANCH)),
            pl.BlockSpec((m, D_MODEL), lambda s: (0, 0)),
            pl.BlockSpec(memory_space=pl.ANY),
            pl.BlockSpec(memory_space=pl.ANY),
            pl.BlockSpec((1, D_MODEL), lambda s: (0, 0)),
        ],
        out_specs=[
            pl.BlockSpec((m, D_MODEL), lambda s: (0, 0)),
            pl.BlockSpec((WROWS, D_MODEL), lambda s: (wb_blk(s), 0)),
            pl.BlockSpec((WROWS, D_MODEL), lambda s: (wo_blk(s), 0)),
        ],
        out_shape=[
            jax.ShapeDtypeStruct((m, D_MODEL), F32),
            jax.ShapeDtypeStruct(wb.shape, BF16),
            jax.ShapeDtypeStruct(wo.shape, BF16),
        ],
        scratch_shapes=[pltpu.VMEM((m, D_MODEL), F32), pltpu.VMEM((m, D_MODEL), F32),
                        pltpu.VMEM((BOC_RING, WROWS, D_MODEL), F32),
                        pltpu.SemaphoreType.DMA((BOC_RING,))],
        compiler_params=pltpu.CompilerParams(
            dimension_semantics=("arbitrary",),
            vmem_limit_bytes=BIG_VMEM_LIMIT),
        name="branch_out_cast",
    )(o, z, x, wb, wo, g_post)


def kernel(x_prompt, x_sample, mem_prompt, state_rglru_h, state_conv, state_pool, cache_mem_k, cache_mem_v, g_pre, w_in, conv_w, conv_b, w_rg_a, b_rg_a, w_rg_x, b_rg_x, lru_lambda, w_pool, pool_scale, g_mem, w_kv, w_branch, w_out, g_post):
    batch, seq, _ = x_prompt.shape
    nb = x_sample.shape[0]
    depth = g_pre.shape[0]
    assert depth == 1 and x_sample.shape[1] == 1

    l = 0
    row = lambda v: v.reshape(1, -1)
    wax = jnp.concatenate([w_rg_a[l], w_rg_x[l]], axis=-1).astype(BF16)
    wpool = w_pool[l].astype(BF16)
    mix_params = (conv_w[l], row(conv_b[l]), wax, row(b_rg_a[l]), row(b_rg_x[l]),
                  row(lru_lambda[l]), wpool, row(pool_scale[l]))

    xp2 = x_prompt.reshape(batch * seq, D_MODEL)
    xs2 = x_sample.reshape(nb, D_MODEL)
    mem2 = mem_prompt.reshape(batch * N_MEM, D_MODEL)

    z_s, w_in_b, mem_k, mem_v = _sample_proj(xs2, row(g_pre[l]), w_in[l], mem2, row(g_mem[l]),
                                             w_kv[l], tn=SAMPLE_PROJ_TN)
    qoff = 2 * D_RNN + 2 * D_POOL
    q_s = z_s[:, qoff:qoff + D_X].reshape(nb // ATTN_BB, ATTN_BB, D_X)

    mem_k = mem_k.reshape(batch, N_MEM, D_X)
    mem_v = mem_v.reshape(batch, N_MEM, D_X)

    perm = _chunk_interleave()
    z_p = _prompt_proj(xp2, row(g_pre[l]), w_in_b, perm, tm=PROJ_TM, tn=PROJ_TN)
    o_p, h_p, c_p, p_p, attn_s = _prompt_mix(
        z_p, mem_k, mem_v, *mix_params, perm.T, q_s, _cache_rows(cache_mem_k[l]),
        _cache_rows(cache_mem_v[l]), batch=batch, seq=seq, tm=MIX_TM)
    attn_s = attn_s.reshape(nb, D_X)

    o_s, h_s, c_s, p_s = _sample_mix(
        z_s, attn_s, state_conv[l].transpose(1, 0, 2), state_rglru_h[l],
        state_pool[l].transpose(1, 0, 2), *mix_params, tb=SAMPLE_MIX_TB)
    y_s, w_br_b, w_out_b = _branch_out_cast(o_s, z_s, xs2, w_branch[l], w_out[l], row(g_post[l]))

    y_p = _branch_out(o_p, z_p, xp2, w_br_b, w_out_b, row(g_post[l]), tm=BRANCH_TM)

    return (
        y_p.reshape(batch, seq, D_MODEL),
        y_s.reshape(nb, 1, D_MODEL),
        h_p.reshape(1, batch, D_RNN),
        c_p.reshape(1, batch, CONV_W - 1, D_RNN),
        p_p.reshape(1, batch, POOL_HIST, D_POOL),
        mem_k.reshape(1, batch, N_MEM, N_XHEADS, XHEAD_DIM),
        mem_v.reshape(1, batch, N_MEM, N_XHEADS, XHEAD_DIM),
        h_s.reshape(1, nb, D_RNN),
        c_s.transpose(1, 0, 2)[None],
        p_s.transpose(1, 0, 2)[None],
    )
```

```python
import functools

import jax
import jax.numpy as jnp
from jax import lax
from jax.experimental import pallas as pl
from jax.experimental.pallas import tpu as pltpu

D_MODEL = 2048
PAST_LEN = 16384
D_RNN = 1024
N_RNN_BLOCKS = 8
RNN_BLOCK = D_RNN // N_RNN_BLOCKS
CONV_W = 4
LRU_C = 8.0
D_POOL = 1024
POOL_WINDOWS = (2, 4, 8, 16)
POOL_GROUP = D_POOL // len(POOL_WINDOWS)
POOL_HIST = max(POOL_WINDOWS) - 1
N_MEM = 256
N_XHEADS = 4
XHEAD_DIM = 256
D_X = N_XHEADS * XHEAD_DIM
N_BRANCH = 3
D_MIX = D_RNN + D_POOL + D_X
D_IN = 2 * D_MIX + N_BRANCH * D_MODEL
EPS = 1e-6

SUBLANES = 8
LANES = 128
VMEM_LIMIT = 56 * 1024 * 1024
BIG_VMEM_LIMIT = 60 * 1024 * 1024
MIX_TM = 256
BRANCH_TM = 256
PROJ_TM, PROJ_TN = 1024, 2048
SAMPLE_PROJ_TN = 768
ATTN_BB = 4
CACHE_RING = 3
W_RING = 3
BOC_RING = 3
SAMPLE_MIX_TB = 64

BF16 = jnp.bfloat16
F32 = jnp.float32

NEG_LOG2_E = -1.4426950408889634


def _sigmoid(x):
    return 1.0 / (1.0 + jnp.exp2(x * NEG_LOG2_E))


def _silu(x):
    return x * _sigmoid(x)


def _softplus(x):
    return jnp.maximum(x, 0.0) + jnp.log1p(jnp.exp(-jnp.abs(x)))


def _rms_scale(x):
    return lax.rsqrt(jnp.mean(x * x, axis=-1, keepdims=True) + EPS)


def _chunk_interleave():
    nrow = MIX_TM // SUBLANES
    p = jnp.arange(MIX_TM)
    token = (p % SUBLANES) * nrow + p // SUBLANES
    return (token[:, None] == jnp.arange(MIX_TM)[None, :]).astype(BF16)


def _sample_proj_kernel(x_ref, g_ref, w_hbm, mem_ref, gm_ref, wkv_ref,
                        o_ref, wb_ref, k_ref, v_ref, u_ref, um_ref, w_ring, ring_sem,
                        *, k_steps, tn):
    j = pl.program_id(0)
    nsteps = pl.num_programs(0)

    def w_copy(blk, slot):
        return pltpu.make_async_copy(w_hbm.at[:, pl.ds(blk * tn, tn)], w_ring.at[slot],
                                     ring_sem.at[slot])

    @pl.when(j == 0)
    def _():
        for first in range(W_RING - 1):
            w_copy(first, first).start()

    ahead = j + (W_RING - 1)

    @pl.when(ahead < nsteps)
    def _():
        w_copy(ahead, ahead % W_RING).start()

    @pl.when(j == 0)
    def _():
        x = x_ref[...]
        u_ref[...] = (x * _rms_scale(x) * g_ref[...]).astype(BF16)
        mem = mem_ref[...]
        um_ref[...] = (mem * _rms_scale(mem) * gm_ref[...]).astype(BF16)

    slot = j % W_RING
    w_copy(j, slot).wait()
    w = w_ring[slot].astype(BF16)
    wb_ref[...] = w
    o_ref[...] = jnp.dot(u_ref[...], w, preferred_element_type=F32)

    kv = jnp.dot(um_ref[...], wkv_ref[...].astype(BF16), preferred_element_type=F32)

    @pl.when(j < k_steps)
    def _():
        k_ref[...] = kv

    @pl.when(j >= k_steps)
    def _():
        v_ref[...] = kv


def _sample_proj(x, g, w, mem, g_mem, w_kv, tn):
    m, k = x.shape
    n = w.shape[1]
    steps = n // tn
    mrows = mem.shape[0]
    kv_tn = 2 * D_X // steps
    assert D_X % kv_tn == 0 and kv_tn % LANES == 0
    k_steps = D_X // kv_tn
    return pl.pallas_call(
        functools.partial(_sample_proj_kernel, k_steps=k_steps, tn=tn),
        grid=(steps,),
        in_specs=[
            pl.BlockSpec((m, k), lambda j: (0, 0)),
            pl.BlockSpec((1, k), lambda j: (0, 0)),
            pl.BlockSpec(memory_space=pl.ANY),
            pl.BlockSpec((mrows, k), lambda j: (0, 0), pipeline_mode=pl.Buffered(1)),
            pl.BlockSpec((1, k), lambda j: (0, 0)),
            pl.BlockSpec((k, kv_tn), lambda j: (0, j)),
        ],
        out_specs=[
            pl.BlockSpec((m, tn), lambda j: (0, j)),
            pl.BlockSpec((k, tn), lambda j: (0, j)),
            pl.BlockSpec((mrows, kv_tn), lambda j: (0, jnp.minimum(j, k_steps - 1))),
            pl.BlockSpec((mrows, kv_tn), lambda j: (0, jnp.maximum(j - k_steps, 0))),
        ],
        out_shape=[
            jax.ShapeDtypeStruct((m, n), F32),
            jax.ShapeDtypeStruct((k, n), BF16),
            jax.ShapeDtypeStruct((mrows, D_X), F32),
            jax.ShapeDtypeStruct((mrows, D_X), F32),
        ],
        scratch_shapes=[pltpu.VMEM((m, k), BF16), pltpu.VMEM((mrows, k), BF16),
                        pltpu.VMEM((W_RING, k, tn), F32), pltpu.SemaphoreType.DMA((W_RING,))],
        compiler_params=pltpu.CompilerParams(
            dimension_semantics=("arbitrary",),
            vmem_limit_bytes=VMEM_LIMIT),
        name="sample_proj",
    )(x, g, w, mem, g_mem, w_kv)


def _decay_rate(lam):
    return _softplus(-lam) * (LRU_C * NEG_LOG2_E)


def _rglru_block(xc, wax, ba, bx, rate):
    ri = jnp.dot(xc.astype(BF16), wax, preferred_element_type=F32)
    r = _sigmoid(ri[:, :RNN_BLOCK] + ba)
    i = _sigmoid(ri[:, RNN_BLOCK:] + bx)
    a = jnp.exp2(r * rate)
    one_m = 1.0 - a * a
    mult = jnp.where(one_m > 0.0, one_m * lax.rsqrt(one_m), 0.0)
    return a, mult * i * xc


def _prompt_mix_kernel(z_ref, k_ref, v_ref, convw_ref, convb_ref, wax_ref, ba_ref, bx_ref,
                       lam_ref, wpool_ref, pscale_ref, unperm_ref, sq_ref, sk_hbm, sv_hbm,
                       o_ref, newh_ref, newconv_ref, newpool_ref, sattn_ref,
                       conv_carry, pool_carry, h_carry, kb_ref, vb_ref, ac_scr, hl_scr, op_scr,
                       sk_ring, sv_ring, ring_sem, *, tm):
    l = pl.program_id(1)
    last = pl.num_programs(1) - 1
    nrow = tm // SUBLANES

    @pl.when(l == 0)
    def _():
        conv_carry[...] = jnp.zeros(conv_carry.shape, F32)
        pool_carry[...] = jnp.zeros(pool_carry.shape, F32)
        h_carry[...] = jnp.zeros(h_carry.shape, F32)
        kb_ref[...] = k_ref[0].astype(BF16)
        vb_ref[...] = v_ref[0].astype(BF16)

    step = pl.program_id(0) * pl.num_programs(1) + l
    nsteps = pl.num_programs(0) * pl.num_programs(1)

    def cache_copies(blk, slot):
        rows = pl.ds(blk * ATTN_BB, ATTN_BB)
        return (pltpu.make_async_copy(sk_hbm.at[rows], sk_ring.at[slot], ring_sem.at[slot, 0]),
                pltpu.make_async_copy(sv_hbm.at[rows], sv_ring.at[slot], ring_sem.at[slot, 1]))

    @pl.when(step == 0)
    def _():
        for first in range(CACHE_RING - 1):
            for cp in cache_copies(first, first):
                cp.start()

    ahead = step + (CACHE_RING - 1)

    @pl.when(ahead < nsteps)
    def _():
        for cp in cache_copies(ahead, ahead % CACHE_RING):
            cp.start()

    slot = step % CACHE_RING
    for cp in cache_copies(step, slot):
        cp.wait()
    sk_ref = sk_ring.at[slot]
    sv_ref = sv_ring.at[slot]
    side_scores = _sample_attn_scores(sq_ref.at[0], sk_ref, ATTN_BB)

    chunk_id = lax.broadcasted_iota(jnp.int32, (SUBLANES, LANES), 0)
    first_chunk = chunk_id == 0

    def load_groups(col, width=LANES):
        return [z_ref[r * SUBLANES:(r + 1) * SUBLANES, col:col + width] for r in range(nrow)]

    def put(col, width, val):
        op_scr[:, col:col + width] = val.astype(BF16)

    def store_groups(col, rows, width=LANES):
        put(col, width, jnp.concatenate(rows, axis=0))

    def history(tail_group, carry_ref, j, c0):
        tail = pltpu.roll(tail_group, 1, 0)
        prev = jnp.where(first_chunk, carry_ref[j - 1, :, c0:c0 + LANES], tail)
        carry_ref[j - 1, :, c0:c0 + LANES] = tail
        return prev

    rate = _decay_rate(lam_ref[...])
    for n in range(N_RNN_BLOCKS):
        c0, c1 = n * RNN_BLOCK, (n + 1) * RNN_BLOCK
        xs = load_groups(c0)
        ext = [history(xs[nrow - j], conv_carry, j, c0) for j in range(CONV_W - 1, 0, -1)] + xs
        cw = [jnp.broadcast_to(convw_ref[k:k + 1, c0:c1], (SUBLANES, LANES)) for k in range(CONV_W)]
        cb = jnp.broadcast_to(convb_ref[:, c0:c1], (SUBLANES, LANES))
        xc = []
        for r in range(nrow):
            acc = cb + cw[0] * ext[r]
            for k in range(1, CONV_W):
                acc = acc + cw[k] * ext[r + k]
            xc.append(acc)
        a, b = _rglru_block(jnp.concatenate(xc, axis=0), wax_ref[n], ba_ref[:, c0:c1],
                            bx_ref[:, c0:c1], rate[:, c0:c1])
        ac_scr[:, c0:c1] = a
        hl_scr[:, c0:c1] = b

    side_probs = _sample_attn_probs(side_scores)

    acc_a = ac_scr[0:SUBLANES, :]
    acc_h = hl_scr[0:SUBLANES, :]
    for r in range(1, nrow):
        rows = slice(r * SUBLANES, (r + 1) * SUBLANES)
        ar = ac_scr[rows, :]
        acc_h = ar * acc_h + hl_scr[rows, :]
        acc_a = ar * acc_a
        ac_scr[rows, :] = acc_a
        hl_scr[rows, :] = acc_h
    h_in = h_carry[...]
    entering = []
    for c in range(SUBLANES):
        entering.append(h_in)
        h_in = acc_a[c:c + 1] * h_in + acc_h[c:c + 1]
    h_carry[...] = h_in
    h_enter = jnp.concatenate(entering, axis=0)
    for n in range(N_RNN_BLOCKS):
        c0, c1 = n * RNN_BLOCK, (n + 1) * RNN_BLOCK
        gr = load_groups(D_RNN + c0)
        store_groups(c0, [(hl_scr[r * SUBLANES:(r + 1) * SUBLANES, c0:c1]
                           + ac_scr[r * SUBLANES:(r + 1) * SUBLANES, c0:c1] * h_enter[:, c0:c1])
                          * _silu(gr[r]) for r in range(nrow)])

    _sample_attn_values(side_probs, sv_ref, sattn_ref.at[0])

    pcol = 2 * D_RNN
    blocks = [(w, c0) for g, w in enumerate(POOL_WINDOWS)
              for c0 in range(g * POOL_GROUP, (g + 1) * POOL_GROUP, LANES)]

    def group(c0, r):
        return z_ref[r * SUBLANES:(r + 1) * SUBLANES, pcol + c0:pcol + c0 + LANES]

    def mean_minus_token(tot, w, c0, r):
        if r < w - 1:
            pos1 = l * tm + chunk_id * nrow + (r + 1)
            mean = tot / jnp.minimum(pos1, w).astype(F32)
        else:
            mean = tot * (1.0 / w)
        return mean - group(c0, r)

    hist, tot = {}, {}
    for w, c0 in blocks:
        hist[c0] = [history(group(c0, nrow - j), pool_carry, j, c0) for j in range(1, w)]
        t = group(c0, 0)
        for h in hist[c0]:
            t = t + h
        tot[c0] = t
        hl_scr[0:SUBLANES, c0:c0 + LANES] = mean_minus_token(t, w, c0, 0)
    for r in range(1, nrow):
        for w, c0 in blocks:
            leaving = group(c0, r - w) if r >= w else hist[c0][w - r - 1]
            tot[c0] = tot[c0] + (group(c0, r) - leaving)
            hl_scr[r * SUBLANES:(r + 1) * SUBLANES, c0:c0 + LANES] = mean_minus_token(
                tot[c0], w, c0, r)
    for g, w in enumerate(POOL_WINDOWS):
        c0, c1 = g * POOL_GROUP, (g + 1) * POOL_GROUP
        og = jnp.dot(hl_scr[:, c0:c1].astype(BF16), wpool_ref[g], preferred_element_type=F32)
        gp = z_ref[:, pcol + D_POOL + c0:pcol + D_POOL + c1]
        put(D_RNN + c0, POOL_GROUP, og * pscale_ref[:, c0:c1] * _silu(gp))

    qoff = 2 * D_RNN + 2 * D_POOL
    for hd in range(N_XHEADS):
        c0, c1 = hd * XHEAD_DIM, (hd + 1) * XHEAD_DIM
        q = z_ref[:, qoff + c0:qoff + c1].astype(BF16)
        s = lax.dot_general(q, kb_ref[:, c0:c1], (((1,), (1,)), ((), ())),
                            preferred_element_type=F32) * (XHEAD_DIM ** -0.5)
        p = jnp.exp(s - jnp.max(s, axis=-1, keepdims=True))
        p = p / jnp.sum(p, axis=-1, keepdims=True)
        ox = jnp.dot(p.astype(BF16), vb_ref[:, c0:c1], preferred_element_type=F32)
        gx = z_ref[:, qoff + D_X + c0:qoff + D_X + c1]
        put(D_RNN + D_POOL + c0, XHEAD_DIM, ox * _silu(gx))

    o_ref[...] = jnp.dot(unperm_ref[...], op_scr[...], preferred_element_type=F32).astype(BF16)

    @pl.when(l == last)
    def _():
        newh_ref[0] = h_carry[...]
        tail_row = lambda j: (nrow - j) * SUBLANES + SUBLANES - 1
        for j in range(1, CONV_W):
            newconv_ref[0, CONV_W - 1 - j:CONV_W - j, :] = z_ref[tail_row(j):tail_row(j) + 1, 0:D_RNN]
        for j in range(1, POOL_HIST + 1):
            newpool_ref[0, POOL_HIST - j:POOL_HIST - j + 1, :] = (
                z_ref[tail_row(j):tail_row(j) + 1, pcol:pcol + D_POOL])


def _prompt_mix(z, mem_k, mem_v, conv_w, conv_b, wax, b_a, b_x, lam, wpool, pscale, unperm,
                sample_q, cache_k, cache_v, batch, seq, tm):
    nl = seq // tm
    assert sample_q.shape[0] == batch * nl
    side = lambda b, l: (b * nl + l, 0, 0)
    zw = 2 * D_MIX
    const2 = lambda b, l: (0, 0)
    const3 = lambda b, l: (0, 0, 0)
    kern = functools.partial(_prompt_mix_kernel, tm=tm)
    return pl.pallas_call(
        kern,
        grid=(batch, nl),
        in_specs=[
            pl.BlockSpec((tm, zw), lambda b, l: (b * nl + l, 0)),
            pl.BlockSpec((1, N_MEM, D_X), lambda b, l: (b, 0, 0)),
            pl.BlockSpec((1, N_MEM, D_X), lambda b, l: (b, 0, 0)),
            pl.BlockSpec((CONV_W, D_RNN), const2),
            pl.BlockSpec((1, D_RNN), const2),
            pl.BlockSpec((N_RNN_BLOCKS, RNN_BLOCK, 2 * RNN_BLOCK), const3),
            pl.BlockSpec((1, D_RNN), const2),
            pl.BlockSpec((1, D_RNN), const2),
            pl.BlockSpec((1, D_RNN), const2),
            pl.BlockSpec((len(POOL_WINDOWS), POOL_GROUP, POOL_GROUP), const3),
            pl.BlockSpec((1, D_POOL), const2),
            pl.BlockSpec((tm, tm), const2),
            pl.BlockSpec((1, ATTN_BB, D_X), side),
            pl.BlockSpec(memory_space=pl.ANY),
            pl.BlockSpec(memory_space=pl.ANY),
        ],
        out_specs=[
            pl.BlockSpec((tm, D_MIX), lambda b, l: (b * nl + l, 0)),
            pl.BlockSpec((1, 1, D_RNN), lambda b, l: (b, 0, 0)),
            pl.BlockSpec((1, CONV_W - 1, D_RNN), lambda b, l: (b, 0, 0)),
            pl.BlockSpec((1, POOL_HIST, D_POOL), lambda b, l: (b, 0, 0)),
            pl.BlockSpec((1, ATTN_BB, D_X), side),
        ],
        out_shape=[
            jax.ShapeDtypeStruct((batch * seq, D_MIX), BF16),
            jax.ShapeDtypeStruct((batch, 1, D_RNN), F32),
            jax.ShapeDtypeStruct((batch, CONV_W - 1, D_RNN), F32),
            jax.ShapeDtypeStruct((batch, POOL_HIST, D_POOL), F32),
            jax.ShapeDtypeStruct(sample_q.shape, F32),
        ],
        scratch_shapes=[
            pltpu.VMEM((CONV_W - 1, SUBLANES, D_RNN), F32),
            pltpu.VMEM((POOL_HIST, SUBLANES, D_POOL), F32),
            pltpu.VMEM((1, D_RNN), F32),
            pltpu.VMEM((N_MEM, D_X), BF16),
            pltpu.VMEM((N_MEM, D_X), BF16),
            pltpu.VMEM((tm, D_RNN), F32),
            pltpu.VMEM((tm, D_RNN), F32),
            pltpu.VMEM((tm, D_MIX), BF16),
            pltpu.VMEM((CACHE_RING, ATTN_BB, N_MEM * SUBLANES, LANES), F32),
            pltpu.VMEM((CACHE_RING, ATTN_BB, N_MEM * SUBLANES, LANES), F32),
            pltpu.SemaphoreType.DMA((CACHE_RING, 2)),
        ],
        compiler_params=pltpu.CompilerParams(
            dimension_semantics=("arbitrary", "arbitrary"),
            vmem_limit_bytes=VMEM_LIMIT),
        name="prompt_mix",
    )(z, mem_k, mem_v, conv_w, conv_b, wax, b_a, b_x, lam, wpool, pscale, unperm,
      sample_q, cache_k, cache_v)


def _cache_rows(c):
    nb = c.shape[0]
    c = c.reshape(nb, N_MEM, N_XHEADS, XHEAD_DIM // LANES, LANES)
    return c.transpose(0, 1, 3, 2, 4).reshape(nb, N_MEM * SUBLANES, LANES)


def _sample_attn_scores(q_ref, k_ref, bb):
    halves = XHEAD_DIM // LANES
    assert halves * N_XHEADS == SUBLANES
    scores = []
    for j in range(bb):
        qn = jnp.concatenate(
            [q_ref[j:j + 1, (h * halves + t) * LANES:(h * halves + t + 1) * LANES]
             for t in range(halves) for h in range(N_XHEADS)], axis=0)
        scores.append(lax.dot_general(qn.astype(BF16), k_ref[j].astype(BF16),
                                      (((1,), (1,)), ((), ())), preferred_element_type=F32)
                      * (XHEAD_DIM ** -0.5))
    return scores


def _sample_attn_probs(scores):
    r = lax.broadcasted_iota(jnp.int32, (SUBLANES, LANES), 0)
    c = lax.broadcasted_iota(jnp.int32, (SUBLANES, LANES), 1)
    diag = (c % SUBLANES) == r
    first_half = r < N_XHEADS
    nchunk = N_MEM * SUBLANES // LANES
    probs = []
    for s in scores:
        chunks = []
        for ci in range(nchunk):
            sm = jnp.where(diag, s[:, ci * LANES:(ci + 1) * LANES], 0.0)
            other = pltpu.roll(sm, N_XHEADS, 0)
            other = jnp.where(first_half, pltpu.roll(other, LANES - N_XHEADS, 1),
                              pltpu.roll(other, N_XHEADS, 1))
            chunks.append(jnp.where(diag, sm + other, -jnp.inf))
        t_full = jnp.concatenate(chunks, axis=1)
        e = jnp.exp(t_full - jnp.max(t_full, axis=1, keepdims=True))
        probs.append((e / jnp.sum(e, axis=1, keepdims=True)).astype(BF16))
    return probs


def _sample_attn_values(probs, v_ref, o_ref):
    halves = XHEAD_DIM // LANES
    for j, p in enumerate(probs):
        o = jnp.dot(p, v_ref[j].astype(BF16), preferred_element_type=F32)
        for t in range(halves):
            for h in range(N_XHEADS):
                col = (h * halves + t) * LANES
                o_ref[j:j + 1, col:col + LANES] = o[t * N_XHEADS + h:t * N_XHEADS + h + 1, :]


def _prompt_proj_kernel(x_ref, g_ref, w_ref, perm_ref, o_ref, u_ref, up_ref, *, mix_steps):
    j = pl.program_id(1)

    @pl.when(j == 0)
    def _():
        x = x_ref[...]
        u = (x * _rms_scale(x) * g_ref[...]).astype(BF16)
        u_ref[...] = u
        for r0 in range(0, u.shape[0], MIX_TM):
            up_ref[r0:r0 + MIX_TM, :] = jnp.dot(
                perm_ref[...], u[r0:r0 + MIX_TM], preferred_element_type=F32).astype(BF16)

    @pl.when(j < mix_steps)
    def _():
        o_ref[...] = jnp.dot(up_ref[...], w_ref[...], preferred_element_type=F32)

    @pl.when(j >= mix_steps)
    def _():
        o_ref[...] = jnp.dot(u_ref[...], w_ref[...], preferred_element_type=F32)


def _prompt_proj(x, g, w, perm, tm, tn):
    m, k = x.shape
    n = w.shape[1]
    assert (2 * D_MIX) % tn == 0
    return pl.pallas_call(
        functools.partial(_prompt_proj_kernel, mix_steps=2 * D_MIX // tn),
        grid=(m // tm, n // tn),
        in_specs=[
            pl.BlockSpec((tm, k), lambda i, j: (i, 0)),
            pl.BlockSpec((1, k), lambda i, j: (0, 0)),
            pl.BlockSpec((k, tn), lambda i, j: (0, j)),
            pl.BlockSpec(perm.shape, lambda i, j: (0, 0)),
        ],
        out_specs=pl.BlockSpec((tm, tn), lambda i, j: (i, j)),
        out_shape=jax.ShapeDtypeStruct((m, n), F32),
        scratch_shapes=[pltpu.VMEM((tm, k), BF16), pltpu.VMEM((tm, k), BF16)],
        compiler_params=pltpu.CompilerParams(
            dimension_semantics=("arbitrary", "arbitrary"),
            vmem_limit_bytes=BIG_VMEM_LIMIT),
        name="prompt_proj",
    )(x, g, w, perm)


def _sample_mix_kernel(z_ref, attn_ref, conv_ref, h_ref, pool_ref,
                       convw_ref, convb_ref, wax_ref, ba_ref, bx_ref, lam_ref, wpool_ref,
                       pscale_ref, o_ref, newh_ref, newconv_ref, newpool_ref):
    xr = z_ref[:, 0:D_RNN]
    xc = convb_ref[...] + convw_ref[CONV_W - 1:CONV_W, :] * xr
    for k in range(CONV_W - 1):
        xc = xc + convw_ref[k:k + 1, :] * conv_ref[k]
    for k in range(CONV_W - 2):
        newconv_ref[k] = conv_ref[k + 1]
    newconv_ref[CONV_W - 2] = xr

    rate = _decay_rate(lam_ref[...])
    for n in range(N_RNN_BLOCKS):
        c0, c1 = n * RNN_BLOCK, (n + 1) * RNN_BLOCK
        a, b = _rglru_block(xc[:, c0:c1], wax_ref[n], ba_ref[:, c0:c1], bx_ref[:, c0:c1],
                            rate[:, c0:c1])
        h = a * h_ref[:, c0:c1] + b
        newh_ref[:, c0:c1] = h
        o_ref[:, c0:c1] = (h * _silu(z_ref[:, D_RNN + c0:D_RNN + c1])).astype(BF16)

    xp = z_ref[:, 2 * D_RNN:2 * D_RNN + D_POOL]
    for k in range(POOL_HIST - 1):
        newpool_ref[k] = pool_ref[k + 1]
    newpool_ref[POOL_HIST - 1] = xp
    for g, w in enumerate(POOL_WINDOWS):
        c0, c1 = g * POOL_GROUP, (g + 1) * POOL_GROUP
        xg = xp[:, c0:c1]
        tot = xg
        for j in range(1, w):
            tot = tot + pool_ref[POOL_HIST - j, :, c0:c1]
        cnt = float(min(PAST_LEN + 1, w))
        d = tot / cnt - xg
        og = jnp.dot(d.astype(BF16), wpool_ref[g], preferred_element_type=F32)
        gp = z_ref[:, 2 * D_RNN + D_POOL + c0:2 * D_RNN + D_POOL + c1]
        o_ref[:, D_RNN + c0:D_RNN + c1] = (og * pscale_ref[:, c0:c1] * _silu(gp)).astype(BF16)

    gx = z_ref[:, 2 * D_RNN + 2 * D_POOL + D_X:2 * D_MIX]
    o_ref[:, D_RNN + D_POOL:] = (attn_ref[...] * _silu(gx)).astype(BF16)


def _sample_mix(z, attn, conv, h, pool, conv_w, conv_b, wax, b_a, b_x, lam, wpool, pscale, tb):
    nb = z.shape[0]
    zw = 2 * D_MIX
    rows = lambda i: (i, 0)
    const2 = lambda i: (0, 0)
    const3 = lambda i: (0, 0, 0)
    hist = lambda i: (0, i, 0)
    return pl.pallas_call(
        _sample_mix_kernel,
        grid=(nb // tb,),
        in_specs=[
            pl.BlockSpec((tb, zw), rows),
            pl.BlockSpec((tb, D_X), rows),
            pl.BlockSpec((CONV_W - 1, tb, D_RNN), hist),
            pl.BlockSpec((tb, D_RNN), rows),
            pl.BlockSpec((POOL_HIST, tb, D_POOL), hist),
            pl.BlockSpec((CONV_W, D_RNN), const2),
            pl.BlockSpec((1, D_RNN), const2),
            pl.BlockSpec((N_RNN_BLOCKS, RNN_BLOCK, 2 * RNN_BLOCK), const3),
            pl.BlockSpec((1, D_RNN), const2),
            pl.BlockSpec((1, D_RNN), const2),
            pl.BlockSpec((1, D_RNN), const2),
            pl.BlockSpec((len(POOL_WINDOWS), POOL_GROUP, POOL_GROUP), const3),
            pl.BlockSpec((1, D_POOL), const2),
        ],
        out_specs=[
            pl.BlockSpec((tb, D_MIX), rows),
            pl.BlockSpec((tb, D_RNN), rows),
            pl.BlockSpec((CONV_W - 1, tb, D_RNN), hist),
            pl.BlockSpec((POOL_HIST, tb, D_POOL), hist),
        ],
        out_shape=[
            jax.ShapeDtypeStruct((nb, D_MIX), BF16),
            jax.ShapeDtypeStruct((nb, D_RNN), F32),
            jax.ShapeDtypeStruct((CONV_W - 1, nb, D_RNN), F32),
            jax.ShapeDtypeStruct((POOL_HIST, nb, D_POOL), F32),
        ],
        compiler_params=pltpu.CompilerParams(
            dimension_semantics=("arbitrary",),
            vmem_limit_bytes=VMEM_LIMIT),
        name="sample_mix",
    )(z, attn, conv, h, pool, conv_w, conv_b, wax, b_a, b_x, lam, wpool, pscale)


def _branch_out_kernel(o_ref, gates_ref, x_ref, wb_ref, wo_ref, gpost_ref, y_ref):
    merged = None
    for j, (r0, r1) in enumerate(((0, D_RNN), (D_RNN, D_RNN + D_POOL), (D_RNN + D_POOL, D_MIX))):
        yj = jnp.dot(o_ref[:, r0:r1], wb_ref[r0:r1, :], preferred_element_type=F32)
        term = _sigmoid(gates_ref[:, j * D_MODEL:(j + 1) * D_MODEL]) * yj
        merged = term if merged is None else merged + term
    out = jnp.dot(merged.astype(BF16), wo_ref[...], preferred_element_type=F32)
    y_ref[...] = x_ref[...] + (out * gpost_ref[...]) * _rms_scale(out)


def _branch_out(o, z, x, wb, wo, g_post, tm):
    m = x.shape[0]
    gw = N_BRANCH * D_MODEL
    gblk = (2 * D_MIX) // gw
    resident = pl.Buffered(1)
    return pl.pallas_call(
        _branch_out_kernel,
        grid=(m // tm,),
        in_specs=[
            pl.BlockSpec((tm, D_MIX), lambda i: (i, 0)),
            pl.BlockSpec((tm, gw), lambda i: (i, gblk)),
            pl.BlockSpec((tm, D_MODEL), lambda i: (i, 0)),
            pl.BlockSpec((D_MIX, D_MODEL), lambda i: (0, 0), pipeline_mode=resident),
            pl.BlockSpec((D_MODEL, D_MODEL), lambda i: (0, 0), pipeline_mode=resident),
            pl.BlockSpec((1, D_MODEL), lambda i: (0, 0)),
        ],
        out_specs=pl.BlockSpec((tm, D_MODEL), lambda i: (i, 0)),
        out_shape=jax.ShapeDtypeStruct((m, D_MODEL), F32),
        compiler_params=pltpu.CompilerParams(
            dimension_semantics=("arbitrary",),
            vmem_limit_bytes=VMEM_LIMIT),
        name="branch_out",
    )(o, z, x, wb, wo, g_post)


WROWS = 1024
PER_BRANCH = D_RNN // WROWS
assert D_RNN == D_POOL == D_X and D_RNN % WROWS == 0 and D_MODEL % WROWS == 0
N_WB_BLOCKS = N_BRANCH * PER_BRANCH
N_WOUT_BLOCKS = D_MODEL // WROWS


def _branch_out_cast_kernel(o_ref, gates_ref, x_ref, wb_hbm, wo_hbm, gpost_ref,
                            y_ref, wbb_ref, wob_ref, merged_ref, out_ref, w_ring, ring_sem):
    s = pl.program_id(0)
    nsteps = N_WB_BLOCKS + N_WOUT_BLOCKS

    def w_copy(blk):
        src, first = (wb_hbm, 0) if blk < N_WB_BLOCKS else (wo_hbm, N_WB_BLOCKS)
        slot = blk % BOC_RING
        return pltpu.make_async_copy(src.at[pl.ds((blk - first) * WROWS, WROWS)],
                                     w_ring.at[slot], ring_sem.at[slot])

    for t in range(nsteps):
        @pl.when(s == t)
        def _(t=t):
            if t == 0:
                for first in range(min(BOC_RING - 1, nsteps)):
                    w_copy(first).start()
            if t + BOC_RING - 1 < nsteps:
                w_copy(t + BOC_RING - 1).start()
            w_copy(t).wait()
            w = w_ring[t % BOC_RING].astype(BF16)
            if t < N_WB_BLOCKS:
                wbb_ref[...] = w
                term = _sigmoid(gates_ref[...]) * jnp.dot(o_ref[...], w,
                                                          preferred_element_type=F32)
                if t == 0:
                    merged_ref[...] = term
                else:
                    merged_ref[...] += term
            else:
                kb = t - N_WB_BLOCKS
                wob_ref[...] = w
                part = jnp.dot(merged_ref[:, kb * WROWS:(kb + 1) * WROWS].astype(BF16), w,
                               preferred_element_type=F32)
                if kb == 0:
                    out_ref[...] = part
                else:
                    out_ref[...] += part
            if t == nsteps - 1:
                out = out_ref[...]
                y_ref[...] = x_ref[...] + out * _rms_scale(out) * gpost_ref[...]


def _branch_out_cast(o, z, x, wb, wo, g_post):
    m = x.shape[0]
    gblk0 = (2 * D_MIX) // D_MODEL
    wb_blk = lambda s: jnp.minimum(s, N_WB_BLOCKS - 1)
    wo_blk = lambda s: jnp.maximum(s - N_WB_BLOCKS, 0)
    return pl.pallas_call(
        _branch_out_cast_kernel,
        grid=(N_WB_BLOCKS + N_WOUT_BLOCKS,),
        in_specs=[
            pl.BlockSpec((m, WROWS), lambda s: (0, wb_blk(s))),
            pl.BlockSpec((m, D_MODEL), lambda s: (0, gblk0 + wb_blk(s) // PER_BRANCH)),
            pl.BlockSpec((m, D_MODEL), lambda s: (0, 0)),
            pl.BlockSpec(memory_space=pl.ANY),
            pl.BlockSpec(memory_space=pl.ANY),
            pl.BlockSpec((1, D_MODEL), lambda s: (0, 0)),
        ],
        out_specs=[
            pl.BlockSpec((m, D_MODEL), lambda s: (0, 0)),
            pl.BlockSpec((WROWS, D_MODEL), lambda s: (wb_blk(s), 0)),
            pl.BlockSpec((WROWS, D_MODEL), lambda s: (wo_blk(s), 0)),
        ],
        out_shape=[
            jax.ShapeDtypeStruct((m, D_MODEL), F32),
            jax.ShapeDtypeStruct(wb.shape, BF16),
            jax.ShapeDtypeStruct(wo.shape, BF16),
        ],
        scratch_shapes=[pltpu.VMEM((m, D_MODEL), F32), pltpu.VMEM((m, D_MODEL), F32),
                        pltpu.VMEM((BOC_RING, WROWS, D_MODEL), F32),
                        pltpu.SemaphoreType.DMA((BOC_RING,))],
        compiler_params=pltpu.CompilerParams(
            dimension_semantics=("arbitrary",),
            vmem_limit_bytes=BIG_VMEM_LIMIT),
        name="branch_out_cast",
    )(o, z, x, wb, wo, g_post)


def kernel(x_prompt, x_sample, mem_prompt, state_rglru_h, state_conv, state_pool, cache_mem_k, cache_mem_v, g_pre, w_in, conv_w, conv_b, w_rg_a, b_rg_a, w_rg_x, b_rg_x, lru_lambda, w_pool, pool_scale, g_mem, w_kv, w_branch, w_out, g_post):
    batch, seq, _ = x_prompt.shape
    nb = x_sample.shape[0]
    depth = g_pre.shape[0]
    assert depth == 1 and x_sample.shape[1] == 1

    l = 0
    row = lambda v: v.reshape(1, -1)
    wax = jnp.concatenate([w_rg_a[l], w_rg_x[l]], axis=-1).astype(BF16)
    wpool = w_pool[l].astype(BF16)
    mix_params = (conv_w[l], row(conv_b[l]), wax, row(b_rg_a[l]), row(b_rg_x[l]),
                  row(lru_lambda[l]), wpool, row(pool_scale[l]))

    xp2 = x_prompt.reshape(batch * seq, D_MODEL)
    xs2 = x_sample.reshape(nb, D_MODEL)
    mem2 = mem_prompt.reshape(batch * N_MEM, D_MODEL)

    z_s, w_in_b, mem_k, mem_v = _sample_proj(xs2, row(g_pre[l]), w_in[l], mem2, row(g_mem[l]),
                                             w_kv[l], tn=SAMPLE_PROJ_TN)
    qoff = 2 * D_RNN + 2 * D_POOL
    q_s = z_s[:, qoff:qoff + D_X].reshape(nb // ATTN_BB, ATTN_BB, D_X)

    mem_k = mem_k.reshape(batch, N_MEM, D_X)
    mem_v = mem_v.reshape(batch, N_MEM, D_X)

    perm = _chunk_interleave()
    z_p = _prompt_proj(xp2, row(g_pre[l]), w_in_b, perm, tm=PROJ_TM, tn=PROJ_TN)
    o_p, h_p, c_p, p_p, attn_s = _prompt_mix(
        z_p, mem_k, mem_v, *mix_params, perm.T, q_s, _cache_rows(cache_mem_k[l]),
        _cache_rows(cache_mem_v[l]), batch=batch, seq=seq, tm=MIX_TM)
    attn_s = attn_s.reshape(nb, D_X)

    o_s, h_s, c_s, p_s = _sample_mix(
        z_s, attn_s, state_conv[l].transpose(1, 0, 2), state_rglru_h[l],
        state_pool[l].transpose(1, 0, 2), *mix_params, tb=SAMPLE_MIX_TB)
    y_s, w_br_b, w_out_b = _branch_out_cast(o_s, z_s, xs2, w_branch[l], w_out[l], row(g_post[l]))

    y_p = _branch_out(o_p, z_p, xp2, w_br_b, w_out_b, row(g_post[l]), tm=BRANCH_TM)

    return (
        y_p.reshape(batch, seq, D_MODEL),
        y_s.reshape(nb, 1, D_MODEL),
        h_p.reshape(1, batch, D_RNN),
        c_p.reshape(1, batch, CONV_W - 1, D_RNN),
        p_p.reshape(1, batch, POOL_HIST, D_POOL),
        mem_k.reshape(1, batch, N_MEM, N_XHEADS, XHEAD_DIM),
        mem_v.reshape(1, batch, N_MEM, N_XHEADS, XHEAD_DIM),
        h_s.reshape(1, nb, D_RNN),
        c_s.transpose(1, 0, 2)[None],
        p_s.transpose(1, 0, 2)[None],
    )
```

```python
import functools

import jax
import jax.numpy as jnp
from jax import lax
from jax.experimental import pallas as pl
from jax.experimental.pallas import tpu as pltpu

D_MODEL = 2048
PAST_LEN = 16384
D_RNN = 1024
N_RNN_BLOCKS = 8
RNN_BLOCK = D_RNN // N_RNN_BLOCKS
CONV_W = 4
LRU_C = 8.0
D_POOL = 1024
POOL_WINDOWS = (2, 4, 8, 16)
POOL_GROUP = D_POOL // len(POOL_WINDOWS)
POOL_HIST = max(POOL_WINDOWS) - 1
N_MEM = 256
N_XHEADS = 4
XHEAD_DIM = 256
D_X = N_XHEADS * XHEAD_DIM
N_BRANCH = 3
D_MIX = D_RNN + D_POOL + D_X
D_IN = 2 * D_MIX + N_BRANCH * D_MODEL
EPS = 1e-6

SUBLANES = 8
LANES = 128
VMEM_LIMIT = 56 * 1024 * 1024
BIG_VMEM_LIMIT = 60 * 1024 * 1024
MIX_TM = 256
BRANCH_TM = 256
PROJ_TM, PROJ_TN = 1024, 2048
SAMPLE_PROJ_TN = 768
ATTN_BB = 4
CACHE_RING = 3
W_RING = 3
BOC_RING = 3
SAMPLE_MIX_TB = 64

BF16 = jnp.bfloat16
F32 = jnp.float32

NEG_LOG2_E = -1.4426950408889634


def _sigmoid(x):
    return 1.0 / (1.0 + jnp.exp2(x * NEG_LOG2_E))


def _silu(x):
    return x * _sigmoid(x)


def _softplus(x):
    return jnp.maximum(x, 0.0) + jnp.log1p(jnp.exp(-jnp.abs(x)))


def _rms_scale(x):
    return lax.rsqrt(jnp.mean(x * x, axis=-1, keepdims=True) + EPS)


def _chunk_interleave():
    nrow = MIX_TM // SUBLANES
    p = jnp.arange(MIX_TM)
    token = (p % SUBLANES) * nrow + p // SUBLANES
    return (token[:, None] == jnp.arange(MIX_TM)[None, :]).astype(BF16)


def _sample_proj_kernel(x_ref, g_ref, w_hbm, mem_ref, gm_ref, wkv_ref,
                        o_ref, wb_ref, k_ref, v_ref, u_ref, um_ref, w_ring, ring_sem,
                        *, k_steps, tn):
    j = pl.program_id(0)
    nsteps = pl.num_programs(0)

    def w_copy(blk, slot):
        return pltpu.make_async_copy(w_hbm.at[:, pl.ds(blk * tn, tn)], w_ring.at[slot],
                                     ring_sem.at[slot])

    @pl.when(j == 0)
    def _():
        for first in range(W_RING - 1):
            w_copy(first, first).start()

    ahead = j + (W_RING - 1)

    @pl.when(ahead < nsteps)
    def _():
        w_copy(ahead, ahead % W_RING).start()

    @pl.when(j == 0)
    def _():
        x = x_ref[...]
        u_ref[...] = (x * _rms_scale(x) * g_ref[...]).astype(BF16)
        mem = mem_ref[...]
        um_ref[...] = (mem * _rms_scale(mem) * gm_ref[...]).astype(BF16)

    slot = j % W_RING
    w_copy(j, slot).wait()
    w = w_ring[slot].astype(BF16)
    wb_ref[...] = w
    o_ref[...] = jnp.dot(u_ref[...], w, preferred_element_type=F32)

    kv = jnp.dot(um_ref[...], wkv_ref[...].astype(BF16), preferred_element_type=F32)

    @pl.when(j < k_steps)
    def _():
        k_ref[...] = kv

    @pl.when(j >= k_steps)
    def _():
        v_ref[...] = kv


def _sample_proj(x, g, w, mem, g_mem, w_kv, tn):
    m, k = x.shape
    n = w.shape[1]
    steps = n // tn
    mrows = mem.shape[0]
    kv_tn = 2 * D_X // steps
    assert D_X % kv_tn == 0 and kv_tn % LANES == 0
    k_steps = D_X // kv_tn
    return pl.pallas_call(
        functools.partial(_sample_proj_kernel, k_steps=k_steps, tn=tn),
        grid=(steps,),
        in_specs=[
            pl.BlockSpec((m, k), lambda j: (0, 0)),
            pl.BlockSpec((1, k), lambda j: (0, 0)),
            pl.BlockSpec(memory_space=pl.ANY),
            pl.BlockSpec((mrows, k), lambda j: (0, 0), pipeline_mode=pl.Buffered(1)),
            pl.BlockSpec((1, k), lambda j: (0, 0)),
            pl.BlockSpec((k, kv_tn), lambda j: (0, j)),
        ],
        out_specs=[
            pl.BlockSpec((m, tn), lambda j: (0, j)),
            pl.BlockSpec((k, tn), lambda j: (0, j)),
            pl.BlockSpec((mrows, kv_tn), lambda j: (0, jnp.minimum(j, k_steps - 1))),
            pl.BlockSpec((mrows, kv_tn), lambda j: (0, jnp.maximum(j - k_steps, 0))),
        ],
        out_shape=[
            jax.ShapeDtypeStruct((m, n), F32),
            jax.ShapeDtypeStruct((k, n), BF16),
            jax.ShapeDtypeStruct((mrows, D_X), F32),
            jax.ShapeDtypeStruct((mrows, D_X), F32),
        ],
        scratch_shapes=[pltpu.VMEM((m, k), BF16), pltpu.VMEM((mrows, k), BF16),
                        pltpu.VMEM((W_RING, k, tn), F32), pltpu.SemaphoreType.DMA((W_RING,))],
        compiler_params=pltpu.CompilerParams(
            dimension_semantics=("arbitrary",),
            vmem_limit_bytes=VMEM_LIMIT),
        name="sample_proj",
    )(x, g, w, mem, g_mem, w_kv)


def _decay_rate(lam):
    return _softplus(-lam) * (LRU_C * NEG_LOG2_E)


def _rglru_block(xc, wax, ba, bx, rate):
    ri = jnp.dot(xc.astype(BF16), wax, preferred_element_type=F32)
    r = _sigmoid(ri[:, :RNN_BLOCK] + ba)
    i = _sigmoid(ri[:, RNN_BLOCK:] + bx)
    a = jnp.exp2(r * rate)
    one_m = 1.0 - a * a
    mult = jnp.where(one_m > 0.0, one_m * lax.rsqrt(one_m), 0.0)
    return a, mult * i * xc


def _prompt_mix_kernel(z_ref, k_ref, v_ref, convw_ref, convb_ref, wax_ref, ba_ref, bx_ref,
                       lam_ref, wpool_ref, pscale_ref, unperm_ref, sq_ref, sk_hbm, sv_hbm,
                       o_ref, newh_ref, newconv_ref, newpool_ref, sattn_ref,
                       conv_carry, pool_carry, h_carry, kb_ref, vb_ref, ac_scr, hl_scr, op_scr,
                       sk_ring, sv_ring, ring_sem, *, tm):
    l = pl.program_id(1)
    last = pl.num_programs(1) - 1
    nrow = tm // SUBLANES

    @pl.when(l == 0)
    def _():
        conv_carry[...] = jnp.zeros(conv_carry.shape, F32)
        pool_carry[...] = jnp.zeros(pool_carry.shape, F32)
        h_carry[...] = jnp.zeros(h_carry.shape, F32)
        kb_ref[...] = k_ref[0].astype(BF16)
        vb_ref[...] = v_ref[0].astype(BF16)

    step = pl.program_id(0) * pl.num_programs(1) + l
    nsteps = pl.num_programs(0) * pl.num_programs(1)

    def cache_copies(blk, slot):
        rows = pl.ds(blk * ATTN_BB, ATTN_BB)
        return (pltpu.make_async_copy(sk_hbm.at[rows], sk_ring.at[slot], ring_sem.at[slot, 0]),
                pltpu.make_async_copy(sv_hbm.at[rows], sv_ring.at[slot], ring_sem.at[slot, 1]))

    @pl.when(step == 0)
    def _():
        for first in range(CACHE_RING - 1):
            for cp in cache_copies(first, first):
                cp.start()

    ahead = step + (CACHE_RING - 1)

    @pl.when(ahead < nsteps)
    def _():
        for cp in cache_copies(ahead, ahead % CACHE_RING):
            cp.start()

    slot = step % CACHE_RING
    for cp in cache_copies(step, slot):
        cp.wait()
    sk_ref = sk_ring.at[slot]
    sv_ref = sv_ring.at[slot]
    side_scores = _sample_attn_scores(sq_ref.at[0], sk_ref, ATTN_BB)

    chunk_id = lax.broadcasted_iota(jnp.int32, (SUBLANES, LANES), 0)
    first_chunk = chunk_id == 0

    def load_groups(col, width=LANES):
        return [z_ref[r * SUBLANES:(r + 1) * SUBLANES, col:col + width] for r in range(nrow)]

    def put(col, width, val):
        op_scr[:, col:col + width] = val.astype(BF16)

    def store_groups(col, rows, width=LANES):
        put(col, width, jnp.concatenate(rows, axis=0))

    def history(tail_group, carry_ref, j, c0):
        tail = pltpu.roll(tail_group, 1, 0)
        prev = jnp.where(first_chunk, carry_ref[j - 1, :, c0:c0 + LANES], tail)
        carry_ref[j - 1, :, c0:c0 + LANES] = tail
        return prev

    rate = _decay_rate(lam_ref[...])
    for n in range(N_RNN_BLOCKS):
        c0, c1 = n * RNN_BLOCK, (n + 1) * RNN_BLOCK
        xs = load_groups(c0)
        ext = [history(xs[nrow - j], conv_carry, j, c0) for j in range(CONV_W - 1, 0, -1)] + xs
        cw = [jnp.broadcast_to(convw_ref[k:k + 1, c0:c1], (SUBLANES, LANES)) for k in range(CONV_W)]
        cb = jnp.broadcast_to(convb_ref[:, c0:c1], (SUBLANES, LANES))
        xc = []
        for r in range(nrow):
            acc = cb + cw[0] * ext[r]
            for k in range(1, CONV_W):
                acc = acc + cw[k] * ext[r + k]
            xc.append(acc)
        a, b = _rglru_block(jnp.concatenate(xc, axis=0), wax_ref[n], ba_ref[:, c0:c1],
                            bx_ref[:, c0:c1], rate[:, c0:c1])
        ac_scr[:, c0:c1] = a
        hl_scr[:, c0:c1] = b

    side_probs = _sample_attn_probs(side_scores)

    acc_a = ac_scr[0:SUBLANES, :]
    acc_h = hl_scr[0:SUBLANES, :]
    for r in range(1, nrow):
        rows = slice(r * SUBLANES, (r + 1) * SUBLANES)
        ar = ac_scr[rows, :]
        acc_h = ar * acc_h + hl_scr[rows, :]
        acc_a = ar * acc_a
        ac_scr[rows, :] = acc_a
        hl_scr[rows, :] = acc_h
    h_in = h_carry[...]
    entering = []
    for c in range(SUBLANES):
        entering.append(h_in)
        h_in = acc_a[c:c + 1] * h_in + acc_h[c:c + 1]
    h_carry[...] = h_in
    h_enter = jnp.concatenate(entering, axis=0)
    for n in range(N_RNN_BLOCKS):
        c0, c1 = n * RNN_BLOCK, (n + 1) * RNN_BLOCK
        gr = load_groups(D_RNN + c0)
        store_groups(c0, [(hl_scr[r * SUBLANES:(r + 1) * SUBLANES, c0:c1]
                           + ac_scr[r * SUBLANES:(r + 1) * SUBLANES, c0:c1] * h_enter[:, c0:c1])
                          * _silu(gr[r]) for r in range(nrow)])

    _sample_attn_values(side_probs, sv_ref, sattn_ref.at[0])

    pcol = 2 * D_RNN
    blocks = [(w, c0) for g, w in enumerate(POOL_WINDOWS)
              for c0 in range(g * POOL_GROUP, (g + 1) * POOL_GROUP, LANES)]

    def group(c0, r):
        return z_ref[r * SUBLANES:(r + 1) * SUBLANES, pcol + c0:pcol + c0 + LANES]

    def mean_minus_token(tot, w, c0, r):
        if r < w - 1:
            pos1 = l * tm + chunk_id * nrow + (r + 1)
            mean = tot / jnp.minimum(pos1, w).astype(F32)
        else:
            mean = tot * (1.0 / w)
        return mean - group(c0, r)

    hist, tot = {}, {}
    for w, c0 in blocks:
        hist[c0] = [history(group(c0, nrow - j), pool_carry, j, c0) for j in range(1, w)]
        t = group(c0, 0)
        for h in hist[c0]:
            t = t + h
        tot[c0] = t
        hl_scr[0:SUBLANES, c0:c0 + LANES] = mean_minus_token(t, w, c0, 0)
    for r in range(1, nrow):
        for w, c0 in blocks:
            leaving = group(c0, r - w) if r >= w else hist[c0][w - r - 1]
            tot[c0] = tot[c0] + (group(c0, r) - leaving)
            hl_scr[r * SUBLANES:(r + 1) * SUBLANES, c0:c0 + LANES] = mean_minus_token(
                tot[c0], w, c0, r)
    for g, w in enumerate(POOL_WINDOWS):
        c0, c1 = g * POOL_GROUP, (g + 1) * POOL_GROUP
        og = jnp.dot(hl_scr[:, c0:c1].astype(BF16), wpool_ref[g], preferred_element_type=F32)
        gp = z_ref[:, pcol + D_POOL + c0:pcol + D_POOL + c1]
        put(D_RNN + c0, POOL_GROUP, og * pscale_ref[:, c0:c1] * _silu(gp))

    qoff = 2 * D_RNN + 2 * D_POOL
    for hd in range(N_XHEADS):
        c0, c1 = hd * XHEAD_DIM, (hd + 1) * XHEAD_DIM
        q = z_ref[:, qoff + c0:qoff + c1].astype(BF16)
        s = lax.dot_general(q, kb_ref[:, c0:c1], (((1,), (1,)), ((), ())),
                            preferred_element_type=F32) * (XHEAD_DIM ** -0.5)
        p = jnp.exp(s - jnp.max(s, axis=-1, keepdims=True))
        p = p / jnp.sum(p, axis=-1, keepdims=True)
        ox = jnp.dot(p.astype(BF16), vb_ref[:, c0:c1], preferred_element_type=F32)
        gx = z_ref[:, qoff + D_X + c0:qoff + D_X + c1]
        put(D_RNN + D_POOL + c0, XHEAD_DIM, ox * _silu(gx))

    o_ref[...] = jnp.dot(unperm_ref[...], op_scr[...], preferred_element_type=F32).astype(BF16)

    @pl.when(l == last)
    def _():
        newh_ref[0] = h_carry[...]
        tail_row = lambda j: (nrow - j) * SUBLANES + SUBLANES - 1
        for j in range(1, CONV_W):
            newconv_ref[0, CONV_W - 1 - j:CONV_W - j, :] = z_ref[tail_row(j):tail_row(j) + 1, 0:D_RNN]
        for j in range(1, POOL_HIST + 1):
            newpool_ref[0, POOL_HIST - j:POOL_HIST - j + 1, :] = (
                z_ref[tail_row(j):tail_row(j) + 1, pcol:pcol + D_POOL])


def _prompt_mix(z, mem_k, mem_v, conv_w, conv_b, wax, b_a, b_x, lam, wpool, pscale, unperm,
                sample_q, cache_k, cache_v, batch, seq, tm):
    nl = seq // tm
    assert sample_q.shape[0] == batch * nl
    side = lambda b, l: (b * nl + l, 0, 0)
    zw = 2 * D_MIX
    const2 = lambda b, l: (0, 0)
    const3 = lambda b, l: (0, 0, 0)
    kern = functools.partial(_prompt_mix_kernel, tm=tm)
    return pl.pallas_call(
        kern,
        grid=(batch, nl),
        in_specs=[
            pl.BlockSpec((tm, zw), lambda b, l: (b * nl + l, 0)),
            pl.BlockSpec((1, N_MEM, D_X), lambda b, l: (b, 0, 0)),
            pl.BlockSpec((1, N_MEM, D_X), lambda b, l: (b, 0, 0)),
            pl.BlockSpec((CONV_W, D_RNN), const2),
            pl.BlockSpec((1, D_RNN), const2),
            pl.BlockSpec((N_RNN_BLOCKS, RNN_BLOCK, 2 * RNN_BLOCK), const3),
            pl.BlockSpec((1, D_RNN), const2),
            pl.BlockSpec((1, D_RNN), const2),
            pl.BlockSpec((1, D_RNN), const2),
            pl.BlockSpec((len(POOL_WINDOWS), POOL_GROUP, POOL_GROUP), const3),
            pl.BlockSpec((1, D_POOL), const2),
            pl.BlockSpec((tm, tm), const2),
            pl.BlockSpec((1, ATTN_BB, D_X), side),
            pl.BlockSpec(memory_space=pl.ANY),
            pl.BlockSpec(memory_space=pl.ANY),
        ],
        out_specs=[
            pl.BlockSpec((tm, D_MIX), lambda b, l: (b * nl + l, 0)),
            pl.BlockSpec((1, 1, D_RNN), lambda b, l: (b, 0, 0)),
            pl.BlockSpec((1, CONV_W - 1, D_RNN), lambda b, l: (b, 0, 0)),
            pl.BlockSpec((1, POOL_HIST, D_POOL), lambda b, l: (b, 0, 0)),
            pl.BlockSpec((1, ATTN_BB, D_X), side),
        ],
        out_shape=[
            jax.ShapeDtypeStruct((batch * seq, D_MIX), BF16),
            jax.ShapeDtypeStruct((batch, 1, D_RNN), F32),
            jax.ShapeDtypeStruct((batch, CONV_W - 1, D_RNN), F32),
            jax.ShapeDtypeStruct((batch, POOL_HIST, D_POOL), F32),
            jax.ShapeDtypeStruct(sample_q.shape, F32),
        ],
        scratch_shapes=[
            pltpu.VMEM((CONV_W - 1, SUBLANES, D_RNN), F32),
            pltpu.VMEM((POOL_HIST, SUBLANES, D_POOL), F32),
            pltpu.VMEM((1, D_RNN), F32),
            pltpu.VMEM((N_MEM, D_X), BF16),
            pltpu.VMEM((N_MEM, D_X), BF16),
            pltpu.VMEM((tm, D_RNN), F32),
            pltpu.VMEM((tm, D_RNN), F32),
            pltpu.VMEM((tm, D_MIX), BF16),
            pltpu.VMEM((CACHE_RING, ATTN_BB, N_MEM * SUBLANES, LANES), F32),
            pltpu.VMEM((CACHE_RING, ATTN_BB, N_MEM * SUBLANES, LANES), F32),
            pltpu.SemaphoreType.DMA((CACHE_RING, 2)),
        ],
        compiler_params=pltpu.CompilerParams(
            dimension_semantics=("arbitrary", "arbitrary"),
            vmem_limit_bytes=VMEM_LIMIT),
        name="prompt_mix",
    )(z, mem_k, mem_v, conv_w, conv_b, wax, b_a, b_x, lam, wpool, pscale, unperm,
      sample_q, cache_k, cache_v)


def _cache_rows(c):
    nb = c.shape[0]
    c = c.reshape(nb, N_MEM, N_XHEADS, XHEAD_DIM // LANES, LANES)
    return c.transpose(0, 1, 3, 2, 4).reshape(nb, N_MEM * SUBLANES, LANES)


def _sample_attn_scores(q_ref, k_ref, bb):
    halves = XHEAD_DIM // LANES
    assert halves * N_XHEADS == SUBLANES
    scores = []
    for j in range(bb):
        qn = jnp.concatenate(
            [q_ref[j:j + 1, (h * halves + t) * LANES:(h * halves + t + 1) * LANES]
             for t in range(halves) for h in range(N_XHEADS)], axis=0)
        scores.append(lax.dot_general(qn.astype(BF16), k_ref[j].astype(BF16),
                                      (((1,), (1,)), ((), ())), preferred_element_type=F32)
                      * (XHEAD_DIM ** -0.5))
    return scores


def _sample_attn_probs(scores):
    r = lax.broadcasted_iota(jnp.int32, (SUBLANES, LANES), 0)
    c = lax.broadcasted_iota(jnp.int32, (SUBLANES, LANES), 1)
    diag = (c % SUBLANES) == r
    first_half = r < N_XHEADS
    nchunk = N_MEM * SUBLANES // LANES
    probs = []
    for s in scores:
        chunks = []
        for ci in range(nchunk):
            sm = jnp.where(diag, s[:, ci * LANES:(ci + 1) * LANES], 0.0)
            other = pltpu.roll(sm, N_XHEADS, 0)
            other = jnp.where(first_half, pltpu.roll(other, LANES - N_XHEADS, 1),
                              pltpu.roll(other, N_XHEADS, 1))
            chunks.append(jnp.where(diag, sm + other, -jnp.inf))
        t_full = jnp.concatenate(chunks, axis=1)
        e = jnp.exp(t_full - jnp.max(t_full, axis=1, keepdims=True))
        probs.append((e / jnp.sum(e, axis=1, keepdims=True)).astype(BF16))
    return probs


def _sample_attn_values(probs, v_ref, o_ref):
    halves = XHEAD_DIM // LANES
    for j, p in enumerate(probs):
        o = jnp.dot(p, v_ref[j].astype(BF16), preferred_element_type=F32)
        for t in range(halves):
            for h in range(N_XHEADS):
                col = (h * halves + t) * LANES
                o_ref[j:j + 1, col:col + LANES] = o[t * N_XHEADS + h:t * N_XHEADS + h + 1, :]


def _prompt_proj_kernel(x_ref, g_ref, w_ref, perm_ref, o_ref, u_ref, up_ref, *, mix_steps):
    j = pl.program_id(1)

    @pl.when(j == 0)
    def _():
        x = x_ref[...]
        u = (x * _rms_scale(x) * g_ref[...]).astype(BF16)
        u_ref[...] = u
        for r0 in range(0, u.shape[0], MIX_TM):
            up_ref[r0:r0 + MIX_TM, :] = jnp.dot(
                perm_ref[...], u[r0:r0 + MIX_TM], preferred_element_type=F32).astype(BF16)

    @pl.when(j < mix_steps)
    def _():
        o_ref[...] = jnp.dot(up_ref[...], w_ref[...], preferred_element_type=F32)

    @pl.when(j >= mix_steps)
    def _():
        o_ref[...] = jnp.dot(u_ref[...], w_ref[...], preferred_element_type=F32)


def _prompt_proj(x, g, w, perm, tm, tn):
    m, k = x.shape
    n = w.shape[1]
    assert (2 * D_MIX) % tn == 0
    return pl.pallas_call(
        functools.partial(_prompt_proj_kernel, mix_steps=2 * D_MIX // tn),
        grid=(m // tm, n // tn),
        in_specs=[
            pl.BlockSpec((tm, k), lambda i, j: (i, 0)),
            pl.BlockSpec((1, k), lambda i, j: (0, 0)),
            pl.BlockSpec((k, tn), lambda i, j: (0, j)),
            pl.BlockSpec(perm.shape, lambda i, j: (0, 0)),
        ],
        out_specs=pl.BlockSpec((tm, tn), lambda i, j: (i, j)),
        out_shape=jax.ShapeDtypeStruct((m, n), F32),
        scratch_shapes=[pltpu.VMEM((tm, k), BF16), pltpu.VMEM((tm, k), BF16)],
        compiler_params=pltpu.CompilerParams(
            dimension_semantics=("arbitrary", "arbitrary"),
            vmem_limit_bytes=BIG_VMEM_LIMIT),
        name="prompt_proj",
    )(x, g, w, perm)


def _sample_mix_kernel(z_ref, attn_ref, conv_ref, h_ref, pool_ref,
                       convw_ref, convb_ref, wax_ref, ba_ref, bx_ref, lam_ref, wpool_ref,
                       pscale_ref, o_ref, newh_ref, newconv_ref, newpool_ref):
    xr = z_ref[:, 0:D_RNN]
    xc = convb_ref[...] + convw_ref[CONV_W - 1:CONV_W, :] * xr
    for k in range(CONV_W - 1):
        xc = xc + convw_ref[k:k + 1, :] * conv_ref[k]
    for k in range(CONV_W - 2):
        newconv_ref[k] = conv_ref[k + 1]
    newconv_ref[CONV_W - 2] = xr

    rate = _decay_rate(lam_ref[...])
    for n in range(N_RNN_BLOCKS):
        c0, c1 = n * RNN_BLOCK, (n + 1) * RNN_BLOCK
        a, b = _rglru_block(xc[:, c0:c1], wax_ref[n], ba_ref[:, c0:c1], bx_ref[:, c0:c1],
                            rate[:, c0:c1])
        h = a * h_ref[:, c0:c1] + b
        newh_ref[:, c0:c1] = h
        o_ref[:, c0:c1] = (h * _silu(z_ref[:, D_RNN + c0:D_RNN + c1])).astype(BF16)

    xp = z_ref[:, 2 * D_RNN:2 * D_RNN + D_POOL]
    for k in range(POOL_HIST - 1):
        newpool_ref[k] = pool_ref[k + 1]
    newpool_ref[POOL_HIST - 1] = xp
    for g, w in enumerate(POOL_WINDOWS):
        c0, c1 = g * POOL_GROUP, (g + 1) * POOL_GROUP
        xg = xp[:, c0:c1]
        tot = xg
        for j in range(1, w):
            tot = tot + pool_ref[POOL_HIST - j, :, c0:c1]
        cnt = float(min(PAST_LEN + 1, w))
        d = tot / cnt - xg
        og = jnp.dot(d.astype(BF16), wpool_ref[g], preferred_element_type=F32)
        gp = z_ref[:, 2 * D_RNN + D_POOL + c0:2 * D_RNN + D_POOL + c1]
        o_ref[:, D_RNN + c0:D_RNN + c1] = (og * pscale_ref[:, c0:c1] * _silu(gp)).astype(BF16)

    gx = z_ref[:, 2 * D_RNN + 2 * D_POOL + D_X:2 * D_MIX]
    o_ref[:, D_RNN + D_POOL:] = (attn_ref[...] * _silu(gx)).astype(BF16)


def _sample_mix(z, attn, conv, h, pool, conv_w, conv_b, wax, b_a, b_x, lam, wpool, pscale, tb):
    nb = z.shape[0]
    zw = 2 * D_MIX
    rows = lambda i: (i, 0)
    const2 = lambda i: (0, 0)
    const3 = lambda i: (0, 0, 0)
    hist = lambda i: (0, i, 0)
    return pl.pallas_call(
        _sample_mix_kernel,
        grid=(nb // tb,),
        in_specs=[
            pl.BlockSpec((tb, zw), rows),
            pl.BlockSpec((tb, D_X), rows),
            pl.BlockSpec((CONV_W - 1, tb, D_RNN), hist),
            pl.BlockSpec((tb, D_RNN), rows),
            pl.BlockSpec((POOL_HIST, tb, D_POOL), hist),
            pl.BlockSpec((CONV_W, D_RNN), const2),
            pl.BlockSpec((1, D_RNN), const2),
            pl.BlockSpec((N_RNN_BLOCKS, RNN_BLOCK, 2 * RNN_BLOCK), const3),
            pl.BlockSpec((1, D_RNN), const2),
            pl.BlockSpec((1, D_RNN), const2),
            pl.BlockSpec((1, D_RNN), const2),
            pl.BlockSpec((len(POOL_WINDOWS), POOL_GROUP, POOL_GROUP), const3),
            pl.BlockSpec((1, D_POOL), const2),
        ],
        out_specs=[
            pl.BlockSpec((tb, D_MIX), rows),
            pl.BlockSpec((tb, D_RNN), rows),
            pl.BlockSpec((CONV_W - 1, tb, D_RNN), hist),
            pl.BlockSpec((POOL_HIST, tb, D_POOL), hist),
        ],
        out_shape=[
            jax.ShapeDtypeStruct((nb, D_MIX), BF16),
            jax.ShapeDtypeStruct((nb, D_RNN), F32),
            jax.ShapeDtypeStruct((CONV_W - 1, nb, D_RNN), F32),
            jax.ShapeDtypeStruct((POOL_HIST, nb, D_POOL), F32),
        ],
        compiler_params=pltpu.CompilerParams(
            dimension_semantics=("arbitrary",),
            vmem_limit_bytes=VMEM_LIMIT),
        name="sample_mix",
    )(z, attn, conv, h, pool, conv_w, conv_b, wax, b_a, b_x, lam, wpool, pscale)


def _branch_out_kernel(o_ref, gates_ref, x_ref, wb_ref, wo_ref, gpost_ref, y_ref):
    merged = None
    for j, (r0, r1) in enumerate(((0, D_RNN), (D_RNN, D_RNN + D_POOL), (D_RNN + D_POOL, D_MIX))):
        yj = jnp.dot(o_ref[:, r0:r1], wb_ref[r0:r1, :], preferred_element_type=F32)
        term = _sigmoid(gates_ref[:, j * D_MODEL:(j + 1) * D_MODEL]) * yj
        merged = term if merged is None else merged + term
    out = jnp.dot(merged.astype(BF16), wo_ref[...], preferred_element_type=F32)
    y_ref[...] = x_ref[...] + (out * gpost_ref[...]) * _rms_scale(out)


def _branch_out(o, z, x, wb, wo, g_post, tm):
    m = x.shape[0]
    gw = N_BRANCH * D_MODEL
    gblk = (2 * D_MIX) // gw
    resident = pl.Buffered(1)
    return pl.pallas_call(
        _branch_out_kernel,
        grid=(m // tm,),
        in_specs=[
            pl.BlockSpec((tm, D_MIX), lambda i: (i, 0)),
            pl.BlockSpec((tm, gw), lambda i: (i, gblk)),
            pl.BlockSpec((tm, D_MODEL), lambda i: (i, 0)),
            pl.BlockSpec((D_MIX, D_MODEL), lambda i: (0, 0), pipeline_mode=resident),
            pl.BlockSpec((D_MODEL, D_MODEL), lambda i: (0, 0), pipeline_mode=resident),
            pl.BlockSpec((1, D_MODEL), lambda i: (0, 0)),
        ],
        out_specs=pl.BlockSpec((tm, D_MODEL), lambda i: (i, 0)),
        out_shape=jax.ShapeDtypeStruct((m, D_MODEL), F32),
        compiler_params=pltpu.CompilerParams(
            dimension_semantics=("arbitrary",),
            vmem_limit_bytes=VMEM_LIMIT),
        name="branch_out",
    )(o, z, x, wb, wo, g_post)


WROWS = 512
PER_BRANCH = D_RNN // WROWS
assert D_RNN == D_POOL == D_X and D_RNN % WROWS == 0 and D_MODEL % WROWS == 0
N_WB_BLOCKS = N_BRANCH * PER_BRANCH
N_WOUT_BLOCKS = D_MODEL // WROWS


def _branch_out_cast_kernel(o_ref, gates_ref, x_ref, wb_hbm, wo_hbm, gpost_ref,
                            y_ref, wbb_ref, wob_ref, merged_ref, out_ref, w_ring, ring_sem):
    s = pl.program_id(0)
    nsteps = N_WB_BLOCKS + N_WOUT_BLOCKS

    def w_copy(blk):
        src, first = (wb_hbm, 0) if blk < N_WB_BLOCKS else (wo_hbm, N_WB_BLOCKS)
        slot = blk % BOC_RING
        return pltpu.make_async_copy(src.at[pl.ds((blk - first) * WROWS, WROWS)],
                                     w_ring.at[slot], ring_sem.at[slot])

    for t in range(nsteps):
        @pl.when(s == t)
        def _(t=t):
            if t == 0:
                for first in range(min(BOC_RING - 1, nsteps)):
                    w_copy(first).start()
            if t + BOC_RING - 1 < nsteps:
                w_copy(t + BOC_RING - 1).start()
            w_copy(t).wait()
            w = w_ring[t % BOC_RING].astype(BF16)
            if t < N_WB_BLOCKS:
                wbb_ref[...] = w
                term = _sigmoid(gates_ref[...]) * jnp.dot(o_ref[...], w,
                                                          preferred_element_type=F32)
                if t == 0:
                    merged_ref[...] = term
                else:
                    merged_ref[...] += term
            else:
                kb = t - N_WB_BLOCKS
                wob_ref[...] = w
                part = jnp.dot(merged_ref[:, kb * WROWS:(kb + 1) * WROWS].astype(BF16), w,
                               preferred_element_type=F32)
                if kb == 0:
                    out_ref[...] = part
                else:
                    out_ref[...] += part
            if t == nsteps - 1:
                out = out_ref[...]
                y_ref[...] = x_ref[...] + out * _rms_scale(out) * gpost_ref[...]


def _branch_out_cast(o, z, x, wb, wo, g_post):
    m = x.shape[0]
    gblk0 = (2 * D_MIX) // D_MODEL
    wb_blk = lambda s: jnp.minimum(s, N_WB_BLOCKS - 1)
    wo_blk = lambda s: jnp.maximum(s - N_WB_BLOCKS, 0)
    return pl.pallas_call(
        _branch_out_cast_kernel,
        grid=(N_WB_BLOCKS + N_WOUT_BLOCKS,),
        in_specs=[
            pl.BlockSpec((m, WROWS), lambda s: (0, wb_blk(s))),
            pl.BlockSpec((m, D_MODEL), lambda s: (0, gblk0 + wb_blk(s) // PER_BRANCH)),
            pl.BlockSpec((m, D_MODEL), lambda s: (0, 0)),
            pl.BlockSpec(memory_space=pl.ANY),
            pl.BlockSpec(memory_space=pl.ANY),
            pl.BlockSpec((1, D_MODEL), lambda s: (0, 0)),
        ],
        out_specs=[
            pl.BlockSpec((m, D_MODEL), lambda s: (0, 0)),
            pl.BlockSpec((WROWS, D_MODEL), lambda s: (wb_blk(s), 0)),
            pl.BlockSpec((WROWS, D_MODEL), lambda s: (wo_blk(s), 0)),
        ],
        out_shape=[
            jax.ShapeDtypeStruct((m, D_MODEL), F32),
            jax.ShapeDtypeStruct(wb.shape, BF16),
            jax.ShapeDtypeStruct(wo.shape, BF16),
        ],
        scratch_shapes=[pltpu.VMEM((m, D_MODEL), F32), pltpu.VMEM((m, D_MODEL), F32),
                        pltpu.VMEM((BOC_RING, WROWS, D_MODEL), F32),
                        pltpu.SemaphoreType.DMA((BOC_RING,))],
        compiler_params=pltpu.CompilerParams(
            dimension_semantics=("arbitrary",),
            vmem_limit_bytes=BIG_VMEM_LIMIT),
        name="branch_out_cast",
    )(o, z, x, wb, wo, g_post)


def kernel(x_prompt, x_sample, mem_prompt, state_rglru_h, state_conv, state_pool, cache_mem_k, cache_mem_v, g_pre, w_in, conv_w, conv_b, w_rg_a, b_rg_a, w_rg_x, b_rg_x, lru_lambda, w_pool, pool_scale, g_mem, w_kv, w_branch, w_out, g_post):
    batch, seq, _ = x_prompt.shape
    nb = x_sample.shape[0]
    depth = g_pre.shape[0]
    assert depth == 1 and x_sample.shape[1] == 1

    l = 0
    row = lambda v: v.reshape(1, -1)
    wax = jnp.concatenate([w_rg_a[l], w_rg_x[l]], axis=-1).astype(BF16)
    wpool = w_pool[l].astype(BF16)
    mix_params = (conv_w[l], row(conv_b[l]), wax, row(b_rg_a[l]), row(b_rg_x[l]),
                  row(lru_lambda[l]), wpool, row(pool_scale[l]))

    xp2 = x_prompt.reshape(batch * seq, D_MODEL)
    xs2 = x_sample.reshape(nb, D_MODEL)
    mem2 = mem_prompt.reshape(batch * N_MEM, D_MODEL)

    z_s, w_in_b, mem_k, mem_v = _sample_proj(xs2, row(g_pre[l]), w_in[l], mem2, row(g_mem[l]),
                                             w_kv[l], tn=SAMPLE_PROJ_TN)
    qoff = 2 * D_RNN + 2 * D_POOL
    q_s = z_s[:, qoff:qoff + D_X].reshape(nb // ATTN_BB, ATTN_BB, D_X)

    mem_k = mem_k.reshape(batch, N_MEM, D_X)
    mem_v = mem_v.reshape(batch, N_MEM, D_X)

    perm = _chunk_interleave()
    z_p = _prompt_proj(xp2, row(g_pre[l]), w_in_b, perm, tm=PROJ_TM, tn=PROJ_TN)
    o_p, h_p, c_p, p_p, attn_s = _prompt_mix(
        z_p, mem_k, mem_v, *mix_params, perm.T, q_s, _cache_rows(cache_mem_k[l]),
        _cache_rows(cache_mem_v[l]), batch=batch, seq=seq, tm=MIX_TM)
    attn_s = attn_s.reshape(nb, D_X)

    o_s, h_s, c_s, p_s = _sample_mix(
        z_s, attn_s, state_conv[l].transpose(1, 0, 2), state_rglru_h[l],
        state_pool[l].transpose(1, 0, 2), *mix_params, tb=SAMPLE_MIX_TB)
    y_s, w_br_b, w_out_b = _branch_out_cast(o_s, z_s, xs2, w_branch[l], w_out[l], row(g_post[l]))

    y_p = _branch_out(o_p, z_p, xp2, w_br_b, w_out_b, row(g_post[l]), tm=BRANCH_TM)

    return (
        y_p.reshape(batch, seq, D_MODEL),
        y_s.reshape(nb, 1, D_MODEL),
        h_p.reshape(1, batch, D_RNN),
        c_p.reshape(1, batch, CONV_W - 1, D_RNN),
        p_p.reshape(1, batch, POOL_HIST, D_POOL),
        mem_k.reshape(1, batch, N_MEM, N_XHEADS, XHEAD_DIM),
        mem_v.reshape(1, batch, N_MEM, N_XHEADS, XHEAD_DIM),
        h_s.reshape(1, nb, D_RNN),
        c_s.transpose(1, 0, 2)[None],
        p_s.transpose(1, 0, 2)[None],
    )
```

```python
import functools

import jax
import jax.numpy as jnp
from jax import lax
from jax.experimental import pallas as pl
from jax.experimental.pallas import tpu as pltpu

D_MODEL = 2048
PAST_LEN = 16384
D_RNN = 1024
N_RNN_BLOCKS = 8
RNN_BLOCK = D_RNN // N_RNN_BLOCKS
CONV_W = 4
LRU_C = 8.0
D_POOL = 1024
POOL_WINDOWS = (2, 4, 8, 16)
POOL_GROUP = D_POOL // len(POOL_WINDOWS)
POOL_HIST = max(POOL_WINDOWS) - 1
N_MEM = 256
N_XHEADS = 4
XHEAD_DIM = 256
D_X = N_XHEADS * XHEAD_DIM
N_BRANCH = 3
D_MIX = D_RNN + D_POOL + D_X
D_IN = 2 * D_MIX + N_BRANCH * D_MODEL
EPS = 1e-6

SUBLANES = 8
LANES = 128
VMEM_LIMIT = 56 * 1024 * 1024
BIG_VMEM_LIMIT = 60 * 1024 * 1024
MIX_TM = 256
BRANCH_TM = 256
PROJ_TM, PROJ_TN = 1024, 2048
SAMPLE_PROJ_TK = 128
ATTN_BB = 4
CACHE_RING = 3
W_RING = 3
BOC_RING = 3
SAMPLE_MIX_TB = 64

BF16 = jnp.bfloat16
F32 = jnp.float32

NEG_LOG2_E = -1.4426950408889634


def _sigmoid(x):
    return 1.0 / (1.0 + jnp.exp2(x * NEG_LOG2_E))


def _silu(x):
    return x * _sigmoid(x)


def _softplus(x):
    return jnp.maximum(x, 0.0) + jnp.log1p(jnp.exp(-jnp.abs(x)))


def _rms_scale(x):
    return lax.rsqrt(jnp.mean(x * x, axis=-1, keepdims=True) + EPS)


def _chunk_interleave():
    nrow = MIX_TM // SUBLANES
    p = jnp.arange(MIX_TM)
    token = (p % SUBLANES) * nrow + p // SUBLANES
    return (token[:, None] == jnp.arange(MIX_TM)[None, :]).astype(BF16)


def _sample_proj_kernel(x_ref, g_ref, w_hbm, mem_ref, gm_ref, wkv_ref,
                        o_ref, wb_ref, k_ref, v_ref, u_ref, um_ref, w_ring, ring_sem,
                        *, k_steps, tk):
    j = pl.program_id(0)
    nsteps = pl.num_programs(0)

    def w_copy(blk, slot):
        return pltpu.make_async_copy(w_hbm.at[pl.ds(blk * tk, tk)], w_ring.at[slot],
                                     ring_sem.at[slot])

    @pl.when(j == 0)
    def _():
        for first in range(W_RING - 1):
            w_copy(first, first).start()

    ahead = j + (W_RING - 1)

    @pl.when(ahead < nsteps)
    def _():
        w_copy(ahead, ahead % W_RING).start()

    @pl.when(j == 0)
    def _():
        x = x_ref[...]
        u_ref[...] = (x * _rms_scale(x) * g_ref[...]).astype(BF16)
        mem = mem_ref[...]
        um_ref[...] = (mem * _rms_scale(mem) * gm_ref[...]).astype(BF16)

    slot = j % W_RING
    w_copy(j, slot).wait()
    w = w_ring[slot].astype(BF16)
    wb_ref[...] = w
    part = jnp.dot(u_ref[:, pl.ds(pl.multiple_of(j * tk, tk), tk)], w, preferred_element_type=F32)

    @pl.when(j == 0)
    def _():
        o_ref[...] = part

    @pl.when(j > 0)
    def _():
        o_ref[...] += part

    kv = jnp.dot(um_ref[...], wkv_ref[...].astype(BF16), preferred_element_type=F32)

    @pl.when(j < k_steps)
    def _():
        k_ref[...] = kv

    @pl.when(j >= k_steps)
    def _():
        v_ref[...] = kv


def _sample_proj(x, g, w, mem, g_mem, w_kv, tk):
    m, k = x.shape
    n = w.shape[1]
    steps = k // tk
    mrows = mem.shape[0]
    kv_tn = 2 * D_X // steps
    assert D_X % kv_tn == 0 and kv_tn % LANES == 0
    k_steps = D_X // kv_tn
    return pl.pallas_call(
        functools.partial(_sample_proj_kernel, k_steps=k_steps, tk=tk),
        grid=(steps,),
        in_specs=[
            pl.BlockSpec((m, k), lambda j: (0, 0)),
            pl.BlockSpec((1, k), lambda j: (0, 0)),
            pl.BlockSpec(memory_space=pl.ANY),
            pl.BlockSpec((mrows, k), lambda j: (0, 0), pipeline_mode=pl.Buffered(1)),
            pl.BlockSpec((1, k), lambda j: (0, 0)),
            pl.BlockSpec((k, kv_tn), lambda j: (0, j)),
        ],
        out_specs=[
            pl.BlockSpec((m, n), lambda j: (0, 0)),
            pl.BlockSpec((tk, n), lambda j: (j, 0)),
            pl.BlockSpec((mrows, kv_tn), lambda j: (0, jnp.minimum(j, k_steps - 1))),
            pl.BlockSpec((mrows, kv_tn), lambda j: (0, jnp.maximum(j - k_steps, 0))),
        ],
        out_shape=[
            jax.ShapeDtypeStruct((m, n), F32),
            jax.ShapeDtypeStruct((k, n), BF16),
            jax.ShapeDtypeStruct((mrows, D_X), F32),
            jax.ShapeDtypeStruct((mrows, D_X), F32),
        ],
        scratch_shapes=[pltpu.VMEM((m, k), BF16), pltpu.VMEM((mrows, k), BF16),
                        pltpu.VMEM((W_RING, tk, n), F32), pltpu.SemaphoreType.DMA((W_RING,))],
        compiler_params=pltpu.CompilerParams(
            dimension_semantics=("arbitrary",),
            vmem_limit_bytes=BIG_VMEM_LIMIT),
        name="sample_proj",
    )(x, g, w, mem, g_mem, w_kv)


def _decay_rate(lam):
    return _softplus(-lam) * (LRU_C * NEG_LOG2_E)


def _rglru_block(xc, wax, ba, bx, rate):
    ri = jnp.dot(xc.astype(BF16), wax, preferred_element_type=F32)
    r = _sigmoid(ri[:, :RNN_BLOCK] + ba)
    i = _sigmoid(ri[:, RNN_BLOCK:] + bx)
    a = jnp.exp2(r * rate)
    one_m = 1.0 - a * a
    mult = jnp.where(one_m > 0.0, one_m * lax.rsqrt(one_m), 0.0)
    return a, mult * i * xc


def _prompt_mix_kernel(z_ref, k_ref, v_ref, convw_ref, convb_ref, wax_ref, ba_ref, bx_ref,
                       lam_ref, wpool_ref, pscale_ref, unperm_ref, sq_ref, sk_hbm, sv_hbm,
                       o_ref, newh_ref, newconv_ref, newpool_ref, sattn_ref,
                       conv_carry, pool_carry, h_carry, kb_ref, vb_ref, ac_scr, hl_scr, op_scr,
                       sk_ring, sv_ring, ring_sem, *, tm):
    l = pl.program_id(1)
    last = pl.num_programs(1) - 1
    nrow = tm // SUBLANES

    @pl.when(l == 0)
    def _():
        conv_carry[...] = jnp.zeros(conv_carry.shape, F32)
        pool_carry[...] = jnp.zeros(pool_carry.shape, F32)
        h_carry[...] = jnp.zeros(h_carry.shape, F32)
        kb_ref[...] = k_ref[0].astype(BF16)
        vb_ref[...] = v_ref[0].astype(BF16)

    step = pl.program_id(0) * pl.num_programs(1) + l
    nsteps = pl.num_programs(0) * pl.num_programs(1)

    def cache_copies(blk, slot):
        rows = pl.ds(blk * ATTN_BB, ATTN_BB)
        return (pltpu.make_async_copy(sk_hbm.at[rows], sk_ring.at[slot], ring_sem.at[slot, 0]),
                pltpu.make_async_copy(sv_hbm.at[rows], sv_ring.at[slot], ring_sem.at[slot, 1]))

    @pl.when(step == 0)
    def _():
        for first in range(CACHE_RING - 1):
            for cp in cache_copies(first, first):
                cp.start()

    ahead = step + (CACHE_RING - 1)

    @pl.when(ahead < nsteps)
    def _():
        for cp in cache_copies(ahead, ahead % CACHE_RING):
            cp.start()

    slot = step % CACHE_RING
    for cp in cache_copies(step, slot):
        cp.wait()
    sk_ref = sk_ring.at[slot]
    sv_ref = sv_ring.at[slot]
    side_scores = _sample_attn_scores(sq_ref.at[0], sk_ref, ATTN_BB)

    chunk_id = lax.broadcasted_iota(jnp.int32, (SUBLANES, LANES), 0)
    first_chunk = chunk_id == 0

    def load_groups(col, width=LANES):
        return [z_ref[r * SUBLANES:(r + 1) * SUBLANES, col:col + width] for r in range(nrow)]

    def put(col, width, val):
        op_scr[:, col:col + width] = val.astype(BF16)

    def store_groups(col, rows, width=LANES):
        put(col, width, jnp.concatenate(rows, axis=0))

    def history(tail_group, carry_ref, j, c0):
        tail = pltpu.roll(tail_group, 1, 0)
        prev = jnp.where(first_chunk, carry_ref[j - 1, :, c0:c0 + LANES], tail)
        carry_ref[j - 1, :, c0:c0 + LANES] = tail
        return prev

    rate = _decay_rate(lam_ref[...])
    for n in range(N_RNN_BLOCKS):
        c0, c1 = n * RNN_BLOCK, (n + 1) * RNN_BLOCK
        xs = load_groups(c0)
        ext = [history(xs[nrow - j], conv_carry, j, c0) for j in range(CONV_W - 1, 0, -1)] + xs
        cw = [jnp.broadcast_to(convw_ref[k:k + 1, c0:c1], (SUBLANES, LANES)) for k in range(CONV_W)]
        cb = jnp.broadcast_to(convb_ref[:, c0:c1], (SUBLANES, LANES))
        xc = []
        for r in range(nrow):
            acc = cb + cw[0] * ext[r]
            for k in range(1, CONV_W):
                acc = acc + cw[k] * ext[r + k]
            xc.append(acc)
        a, b = _rglru_block(jnp.concatenate(xc, axis=0), wax_ref[n], ba_ref[:, c0:c1],
                            bx_ref[:, c0:c1], rate[:, c0:c1])
        ac_scr[:, c0:c1] = a
        hl_scr[:, c0:c1] = b

    side_probs = _sample_attn_probs(side_scores)

    acc_a = ac_scr[0:SUBLANES, :]
    acc_h = hl_scr[0:SUBLANES, :]
    for r in range(1, nrow):
        rows = slice(r * SUBLANES, (r + 1) * SUBLANES)
        ar = ac_scr[rows, :]
        acc_h = ar * acc_h + hl_scr[rows, :]
        acc_a = ar * acc_a
        ac_scr[rows, :] = acc_a
        hl_scr[rows, :] = acc_h
    h_in = h_carry[...]
    entering = []
    for c in range(SUBLANES):
        entering.append(h_in)
        h_in = acc_a[c:c + 1] * h_in + acc_h[c:c + 1]
    h_carry[...] = h_in
    h_enter = jnp.concatenate(entering, axis=0)
    for n in range(N_RNN_BLOCKS):
        c0, c1 = n * RNN_BLOCK, (n + 1) * RNN_BLOCK
        gr = load_groups(D_RNN + c0)
        store_groups(c0, [(hl_scr[r * SUBLANES:(r + 1) * SUBLANES, c0:c1]
                           + ac_scr[r * SUBLANES:(r + 1) * SUBLANES, c0:c1] * h_enter[:, c0:c1])
                          * _silu(gr[r]) for r in range(nrow)])

    _sample_attn_values(side_probs, sv_ref, sattn_ref.at[0])

    pcol = 2 * D_RNN
    blocks = [(w, c0) for g, w in enumerate(POOL_WINDOWS)
              for c0 in range(g * POOL_GROUP, (g + 1) * POOL_GROUP, LANES)]

    def group(c0, r):
        return z_ref[r * SUBLANES:(r + 1) * SUBLANES, pcol + c0:pcol + c0 + LANES]

    def mean_minus_token(tot, w, c0, r):
        if r < w - 1:
            pos1 = l * tm + chunk_id * nrow + (r + 1)
            mean = tot / jnp.minimum(pos1, w).astype(F32)
        else:
            mean = tot * (1.0 / w)
        return mean - group(c0, r)

    hist, tot = {}, {}
    for w, c0 in blocks:
        hist[c0] = [history(group(c0, nrow - j), pool_carry, j, c0) for j in range(1, w)]
        t = group(c0, 0)
        for h in hist[c0]:
            t = t + h
        tot[c0] = t
        hl_scr[0:SUBLANES, c0:c0 + LANES] = mean_minus_token(t, w, c0, 0)
    for r in range(1, nrow):
        for w, c0 in blocks:
            leaving = group(c0, r - w) if r >= w else hist[c0][w - r - 1]
            tot[c0] = tot[c0] + (group(c0, r) - leaving)
            hl_scr[r * SUBLANES:(r + 1) * SUBLANES, c0:c0 + LANES] = mean_minus_token(
                tot[c0], w, c0, r)
    for g, w in enumerate(POOL_WINDOWS):
        c0, c1 = g * POOL_GROUP, (g + 1) * POOL_GROUP
        og = jnp.dot(hl_scr[:, c0:c1].astype(BF16), wpool_ref[g], preferred_element_type=F32)
        gp = z_ref[:, pcol + D_POOL + c0:pcol + D_POOL + c1]
        put(D_RNN + c0, POOL_GROUP, og * pscale_ref[:, c0:c1] * _silu(gp))

    qoff = 2 * D_RNN + 2 * D_POOL
    for hd in range(N_XHEADS):
        c0, c1 = hd * XHEAD_DIM, (hd + 1) * XHEAD_DIM
        q = z_ref[:, qoff + c0:qoff + c1].astype(BF16)
        s = lax.dot_general(q, kb_ref[:, c0:c1], (((1,), (1,)), ((), ())),
                            preferred_element_type=F32) * (XHEAD_DIM ** -0.5)
        p = jnp.exp(s - jnp.max(s, axis=-1, keepdims=True))
        p = p / jnp.sum(p, axis=-1, keepdims=True)
        ox = jnp.dot(p.astype(BF16), vb_ref[:, c0:c1], preferred_element_type=F32)
        gx = z_ref[:, qoff + D_X + c0:qoff + D_X + c1]
        put(D_RNN + D_POOL + c0, XHEAD_DIM, ox * _silu(gx))

    o_ref[...] = jnp.dot(unperm_ref[...], op_scr[...], preferred_element_type=F32).astype(BF16)

    @pl.when(l == last)
    def _():
        newh_ref[0] = h_carry[...]
        tail_row = lambda j: (nrow - j) * SUBLANES + SUBLANES - 1
        for j in range(1, CONV_W):
            newconv_ref[0, CONV_W - 1 - j:CONV_W - j, :] = z_ref[tail_row(j):tail_row(j) + 1, 0:D_RNN]
        for j in range(1, POOL_HIST + 1):
            newpool_ref[0, POOL_HIST - j:POOL_HIST - j + 1, :] = (
                z_ref[tail_row(j):tail_row(j) + 1, pcol:pcol + D_POOL])


def _prompt_mix(z, mem_k, mem_v, conv_w, conv_b, wax, b_a, b_x, lam, wpool, pscale, unperm,
                sample_q, cache_k, cache_v, batch, seq, tm):
    nl = seq // tm
    assert sample_q.shape[0] == batch * nl
    side = lambda b, l: (b * nl + l, 0, 0)
    zw = 2 * D_MIX
    const2 = lambda b, l: (0, 0)
    const3 = lambda b, l: (0, 0, 0)
    kern = functools.partial(_prompt_mix_kernel, tm=tm)
    return pl.pallas_call(
        kern,
        grid=(batch, nl),
        in_specs=[
            pl.BlockSpec((tm, zw), lambda b, l: (b * nl + l, 0)),
            pl.BlockSpec((1, N_MEM, D_X), lambda b, l: (b, 0, 0)),
            pl.BlockSpec((1, N_MEM, D_X), lambda b, l: (b, 0, 0)),
            pl.BlockSpec((CONV_W, D_RNN), const2),
            pl.BlockSpec((1, D_RNN), const2),
            pl.BlockSpec((N_RNN_BLOCKS, RNN_BLOCK, 2 * RNN_BLOCK), const3),
            pl.BlockSpec((1, D_RNN), const2),
            pl.BlockSpec((1, D_RNN), const2),
            pl.BlockSpec((1, D_RNN), const2),
            pl.BlockSpec((len(POOL_WINDOWS), POOL_GROUP, POOL_GROUP), const3),
            pl.BlockSpec((1, D_POOL), const2),
            pl.BlockSpec((tm, tm), const2),
            pl.BlockSpec((1, ATTN_BB, D_X), side),
            pl.BlockSpec(memory_space=pl.ANY),
            pl.BlockSpec(memory_space=pl.ANY),
        ],
        out_specs=[
            pl.BlockSpec((tm, D_MIX), lambda b, l: (b * nl + l, 0)),
            pl.BlockSpec((1, 1, D_RNN), lambda b, l: (b, 0, 0)),
            pl.BlockSpec((1, CONV_W - 1, D_RNN), lambda b, l: (b, 0, 0)),
            pl.BlockSpec((1, POOL_HIST, D_POOL), lambda b, l: (b, 0, 0)),
            pl.BlockSpec((1, ATTN_BB, D_X), side),
        ],
        out_shape=[
            jax.ShapeDtypeStruct((batch * seq, D_MIX), BF16),
            jax.ShapeDtypeStruct((batch, 1, D_RNN), F32),
            jax.ShapeDtypeStruct((batch, CONV_W - 1, D_RNN), F32),
            jax.ShapeDtypeStruct((batch, POOL_HIST, D_POOL), F32),
            jax.ShapeDtypeStruct(sample_q.shape, F32),
        ],
        scratch_shapes=[
            pltpu.VMEM((CONV_W - 1, SUBLANES, D_RNN), F32),
            pltpu.VMEM((POOL_HIST, SUBLANES, D_POOL), F32),
            pltpu.VMEM((1, D_RNN), F32),
            pltpu.VMEM((N_MEM, D_X), BF16),
            pltpu.VMEM((N_MEM, D_X), BF16),
            pltpu.VMEM((tm, D_RNN), F32),
            pltpu.VMEM((tm, D_RNN), F32),
            pltpu.VMEM((tm, D_MIX), BF16),
            pltpu.VMEM((CACHE_RING, ATTN_BB, N_MEM * SUBLANES, LANES), F32),
            pltpu.VMEM((CACHE_RING, ATTN_BB, N_MEM * SUBLANES, LANES), F32),
            pltpu.SemaphoreType.DMA((CACHE_RING, 2)),
        ],
        compiler_params=pltpu.CompilerParams(
            dimension_semantics=("arbitrary", "arbitrary"),
            vmem_limit_bytes=VMEM_LIMIT),
        name="prompt_mix",
    )(z, mem_k, mem_v, conv_w, conv_b, wax, b_a, b_x, lam, wpool, pscale, unperm,
      sample_q, cache_k, cache_v)


def _cache_rows(c):
    nb = c.shape[0]
    c = c.reshape(nb, N_MEM, N_XHEADS, XHEAD_DIM // LANES, LANES)
    return c.transpose(0, 1, 3, 2, 4).reshape(nb, N_MEM * SUBLANES, LANES)


def _sample_attn_scores(q_ref, k_ref, bb):
    halves = XHEAD_DIM // LANES
    assert halves * N_XHEADS == SUBLANES
    scores = []
    for j in range(bb):
        qn = jnp.concatenate(
            [q_ref[j:j + 1, (h * halves + t) * LANES:(h * halves + t + 1) * LANES]
             for t in range(halves) for h in range(N_XHEADS)], axis=0)
        scores.append(lax.dot_general(qn.astype(BF16), k_ref[j].astype(BF16),
                                      (((1,), (1,)), ((), ())), preferred_element_type=F32)
                      * (XHEAD_DIM ** -0.5))
    return scores


def _sample_attn_probs(scores):
    r = lax.broadcasted_iota(jnp.int32, (SUBLANES, LANES), 0)
    c = lax.broadcasted_iota(jnp.int32, (SUBLANES, LANES), 1)
    diag = (c % SUBLANES) == r
    first_half = r < N_XHEADS
    nchunk = N_MEM * SUBLANES // LANES
    probs = []
    for s in scores:
        chunks = []
        for ci in range(nchunk):
            sm = jnp.where(diag, s[:, ci * LANES:(ci + 1) * LANES], 0.0)
            other = pltpu.roll(sm, N_XHEADS, 0)
            other = jnp.where(first_half, pltpu.roll(other, LANES - N_XHEADS, 1),
                              pltpu.roll(other, N_XHEADS, 1))
            chunks.append(jnp.where(diag, sm + other, -jnp.inf))
        t_full = jnp.concatenate(chunks, axis=1)
        e = jnp.exp(t_full - jnp.max(t_full, axis=1, keepdims=True))
        probs.append((e / jnp.sum(e, axis=1, keepdims=True)).astype(BF16))
    return probs


def _sample_attn_values(probs, v_ref, o_ref):
    halves = XHEAD_DIM // LANES
    for j, p in enumerate(probs):
        o = jnp.dot(p, v_ref[j].astype(BF16), preferred_element_type=F32)
        for t in range(halves):
            for h in range(N_XHEADS):
                col = (h * halves + t) * LANES
                o_ref[j:j + 1, col:col + LANES] = o[t * N_XHEADS + h:t * N_XHEADS + h + 1, :]


def _prompt_proj_kernel(x_ref, g_ref, w_ref, perm_ref, o_ref, u_ref, up_ref, *, mix_steps):
    j = pl.program_id(1)

    @pl.when(j == 0)
    def _():
        x = x_ref[...]
        u = (x * _rms_scale(x) * g_ref[...]).astype(BF16)
        u_ref[...] = u
        for r0 in range(0, u.shape[0], MIX_TM):
            up_ref[r0:r0 + MIX_TM, :] = jnp.dot(
                perm_ref[...], u[r0:r0 + MIX_TM], preferred_element_type=F32).astype(BF16)

    @pl.when(j < mix_steps)
    def _():
        o_ref[...] = jnp.dot(up_ref[...], w_ref[...], preferred_element_type=F32)

    @pl.when(j >= mix_steps)
    def _():
        o_ref[...] = jnp.dot(u_ref[...], w_ref[...], preferred_element_type=F32)


def _prompt_proj(x, g, w, perm, tm, tn):
    m, k = x.shape
    n = w.shape[1]
    assert (2 * D_MIX) % tn == 0
    return pl.pallas_call(
        functools.partial(_prompt_proj_kernel, mix_steps=2 * D_MIX // tn),
        grid=(m // tm, n // tn),
        in_specs=[
            pl.BlockSpec((tm, k), lambda i, j: (i, 0)),
            pl.BlockSpec((1, k), lambda i, j: (0, 0)),
            pl.BlockSpec((k, tn), lambda i, j: (0, j)),
            pl.BlockSpec(perm.shape, lambda i, j: (0, 0)),
        ],
        out_specs=pl.BlockSpec((tm, tn), lambda i, j: (i, j)),
        out_shape=jax.ShapeDtypeStruct((m, n), F32),
        scratch_shapes=[pltpu.VMEM((tm, k), BF16), pltpu.VMEM((tm, k), BF16)],
        compiler_params=pltpu.CompilerParams(
            dimension_semantics=("arbitrary", "arbitrary"),
            vmem_limit_bytes=BIG_VMEM_LIMIT),
        name="prompt_proj",
    )(x, g, w, perm)


def _sample_mix_kernel(z_ref, attn_ref, conv_ref, h_ref, pool_ref,
                       convw_ref, convb_ref, wax_ref, ba_ref, bx_ref, lam_ref, wpool_ref,
                       pscale_ref, o_ref, newh_ref, newconv_ref, newpool_ref):
    xr = z_ref[:, 0:D_RNN]
    xc = convb_ref[...] + convw_ref[CONV_W - 1:CONV_W, :] * xr
    for k in range(CONV_W - 1):
        xc = xc + convw_ref[k:k + 1, :] * conv_ref[k]
    for k in range(CONV_W - 2):
        newconv_ref[k] = conv_ref[k + 1]
    newconv_ref[CONV_W - 2] = xr

    rate = _decay_rate(lam_ref[...])
    for n in range(N_RNN_BLOCKS):
        c0, c1 = n * RNN_BLOCK, (n + 1) * RNN_BLOCK
        a, b = _rglru_block(xc[:, c0:c1], wax_ref[n], ba_ref[:, c0:c1], bx_ref[:, c0:c1],
                            rate[:, c0:c1])
        h = a * h_ref[:, c0:c1] + b
        newh_ref[:, c0:c1] = h
        o_ref[:, c0:c1] = (h * _silu(z_ref[:, D_RNN + c0:D_RNN + c1])).astype(BF16)

    xp = z_ref[:, 2 * D_RNN:2 * D_RNN + D_POOL]
    for k in range(POOL_HIST - 1):
        newpool_ref[k] = pool_ref[k + 1]
    newpool_ref[POOL_HIST - 1] = xp
    for g, w in enumerate(POOL_WINDOWS):
        c0, c1 = g * POOL_GROUP, (g + 1) * POOL_GROUP
        xg = xp[:, c0:c1]
        tot = xg
        for j in range(1, w):
            tot = tot + pool_ref[POOL_HIST - j, :, c0:c1]
        cnt = float(min(PAST_LEN + 1, w))
        d = tot / cnt - xg
        og = jnp.dot(d.astype(BF16), wpool_ref[g], preferred_element_type=F32)
        gp = z_ref[:, 2 * D_RNN + D_POOL + c0:2 * D_RNN + D_POOL + c1]
        o_ref[:, D_RNN + c0:D_RNN + c1] = (og * pscale_ref[:, c0:c1] * _silu(gp)).astype(BF16)

    gx = z_ref[:, 2 * D_RNN + 2 * D_POOL + D_X:2 * D_MIX]
    o_ref[:, D_RNN + D_POOL:] = (attn_ref[...] * _silu(gx)).astype(BF16)


def _sample_mix(z, attn, conv, h, pool, conv_w, conv_b, wax, b_a, b_x, lam, wpool, pscale, tb):
    nb = z.shape[0]
    zw = 2 * D_MIX
    rows = lambda i: (i, 0)
    const2 = lambda i: (0, 0)
    const3 = lambda i: (0, 0, 0)
    hist = lambda i: (0, i, 0)
    return pl.pallas_call(
        _sample_mix_kernel,
        grid=(nb // tb,),
        in_specs=[
            pl.BlockSpec((tb, zw), rows),
            pl.BlockSpec((tb, D_X), rows),
            pl.BlockSpec((CONV_W - 1, tb, D_RNN), hist),
            pl.BlockSpec((tb, D_RNN), rows),
            pl.BlockSpec((POOL_HIST, tb, D_POOL), hist),
            pl.BlockSpec((CONV_W, D_RNN), const2),
            pl.BlockSpec((1, D_RNN), const2),
            pl.BlockSpec((N_RNN_BLOCKS, RNN_BLOCK, 2 * RNN_BLOCK), const3),
            pl.BlockSpec((1, D_RNN), const2),
            pl.BlockSpec((1, D_RNN), const2),
            pl.BlockSpec((1, D_RNN), const2),
            pl.BlockSpec((len(POOL_WINDOWS), POOL_GROUP, POOL_GROUP), const3),
            pl.BlockSpec((1, D_POOL), const2),
        ],
        out_specs=[
            pl.BlockSpec((tb, D_MIX), rows),
            pl.BlockSpec((tb, D_RNN), rows),
            pl.BlockSpec((CONV_W - 1, tb, D_RNN), hist),
            pl.BlockSpec((POOL_HIST, tb, D_POOL), hist),
        ],
        out_shape=[
            jax.ShapeDtypeStruct((nb, D_MIX), BF16),
            jax.ShapeDtypeStruct((nb, D_RNN), F32),
            jax.ShapeDtypeStruct((CONV_W - 1, nb, D_RNN), F32),
            jax.ShapeDtypeStruct((POOL_HIST, nb, D_POOL), F32),
        ],
        compiler_params=pltpu.CompilerParams(
            dimension_semantics=("arbitrary",),
            vmem_limit_bytes=VMEM_LIMIT),
        name="sample_mix",
    )(z, attn, conv, h, pool, conv_w, conv_b, wax, b_a, b_x, lam, wpool, pscale)


def _branch_out_kernel(o_ref, gates_ref, x_ref, wb_ref, wo_ref, gpost_ref, y_ref):
    merged = None
    for j, (r0, r1) in enumerate(((0, D_RNN), (D_RNN, D_RNN + D_POOL), (D_RNN + D_POOL, D_MIX))):
        yj = jnp.dot(o_ref[:, r0:r1], wb_ref[r0:r1, :], preferred_element_type=F32)
        term = _sigmoid(gates_ref[:, j * D_MODEL:(j + 1) * D_MODEL]) * yj
        merged = term if merged is None else merged + term
    out = jnp.dot(merged.astype(BF16), wo_ref[...], preferred_element_type=F32)
    y_ref[...] = x_ref[...] + (out * gpost_ref[...]) * _rms_scale(out)


def _branch_out(o, z, x, wb, wo, g_post, tm):
    m = x.shape[0]
    gw = N_BRANCH * D_MODEL
    gblk = (2 * D_MIX) // gw
    resident = pl.Buffered(1)
    return pl.pallas_call(
        _branch_out_kernel,
        grid=(m // tm,),
        in_specs=[
            pl.BlockSpec((tm, D_MIX), lambda i: (i, 0)),
            pl.BlockSpec((tm, gw), lambda i: (i, gblk)),
            pl.BlockSpec((tm, D_MODEL), lambda i: (i, 0)),
            pl.BlockSpec((D_MIX, D_MODEL), lambda i: (0, 0), pipeline_mode=resident),
            pl.BlockSpec((D_MODEL, D_MODEL), lambda i: (0, 0), pipeline_mode=resident),
            pl.BlockSpec((1, D_MODEL), lambda i: (0, 0)),
        ],
        out_specs=pl.BlockSpec((tm, D_MODEL), lambda i: (i, 0)),
        out_shape=jax.ShapeDtypeStruct((m, D_MODEL), F32),
        compiler_params=pltpu.CompilerParams(
            dimension_semantics=("arbitrary",),
            vmem_limit_bytes=VMEM_LIMIT),
        name="branch_out",
    )(o, z, x, wb, wo, g_post)


WROWS = 1024
PER_BRANCH = D_RNN // WROWS
assert D_RNN == D_POOL == D_X and D_RNN % WROWS == 0 and D_MODEL % WROWS == 0
N_WB_BLOCKS = N_BRANCH * PER_BRANCH
N_WOUT_BLOCKS = D_MODEL // WROWS


def _branch_out_cast_kernel(o_ref, gates_ref, x_ref, wb_hbm, wo_hbm, gpost_ref,
                            y_ref, wbb_ref, wob_ref, merged_ref, out_ref, w_ring, ring_sem):
    s = pl.program_id(0)
    nsteps = N_WB_BLOCKS + N_WOUT_BLOCKS

    def w_copy(blk):
        src, first = (wb_hbm, 0) if blk < N_WB_BLOCKS else (wo_hbm, N_WB_BLOCKS)
        slot = blk % BOC_RING
        return pltpu.make_async_copy(src.at[pl.ds((blk - first) * WROWS, WROWS)],
                                     w_ring.at[slot], ring_sem.at[slot])

    for t in range(nsteps):
        @pl.when(s == t)
        def _(t=t):
            if t == 0:
                for first in range(min(BOC_RING - 1, nsteps)):
                    w_copy(first).start()
            if t + BOC_RING - 1 < nsteps:
                w_copy(t + BOC_RING - 1).start()
            w_copy(t).wait()
            w = w_ring[t % BOC_RING].astype(BF16)
            if t < N_WB_BLOCKS:
                wbb_ref[...] = w
                term = _sigmoid(gates_ref[...]) * jnp.dot(o_ref[...], w,
                                                          preferred_element_type=F32)
                if t == 0:
                    merged_ref[...] = term
                else:
                    merged_ref[...] += term
            else:
                kb = t - N_WB_BLOCKS
                wob_ref[...] = w
                part = jnp.dot(merged_ref[:, kb * WROWS:(kb + 1) * WROWS].astype(BF16), w,
                               preferred_element_type=F32)
                if kb == 0:
                    out_ref[...] = part
                else:
                    out_ref[...] += part
            if t == nsteps - 1:
                out = out_ref[...]
                y_ref[...] = x_ref[...] + out * _rms_scale(out) * gpost_ref[...]


def _branch_out_cast(o, z, x, wb, wo, g_post):
    m = x.shape[0]
    gblk0 = (2 * D_MIX) // D_MODEL
    wb_blk = lambda s: jnp.minimum(s, N_WB_BLOCKS - 1)
    wo_blk = lambda s: jnp.maximum(s - N_WB_BLOCKS, 0)
    return pl.pallas_call(
        _branch_out_cast_kernel,
        grid=(N_WB_BLOCKS + N_WOUT_BLOCKS,),
        in_specs=[
            pl.BlockSpec((m, WROWS), lambda s: (0, wb_blk(s))),
            pl.BlockSpec((m, D_MODEL), lambda s: (0, gblk0 + wb_blk(s) // PER_BRANCH)),
            pl.BlockSpec((m, D_MODEL), lambda s: (0, 0)),
            pl.BlockSpec(memory_space=pl.ANY),
            pl.BlockSpec(memory_space=pl.ANY),
            pl.BlockSpec((1, D_MODEL), lambda s: (0, 0)),
        ],
        out_specs=[
            pl.BlockSpec((m, D_MODEL), lambda s: (0, 0)),
            pl.BlockSpec((WROWS, D_MODEL), lambda s: (wb_blk(s), 0)),
            pl.BlockSpec((WROWS, D_MODEL), lambda s: (wo_blk(s), 0)),
        ],
        out_shape=[
            jax.ShapeDtypeStruct((m, D_MODEL), F32),
            jax.ShapeDtypeStruct(wb.shape, BF16),
            jax.ShapeDtypeStruct(wo.shape, BF16),
        ],
        scratch_shapes=[pltpu.VMEM((m, D_MODEL), F32), pltpu.VMEM((m, D_MODEL), F32),
                        pltpu.VMEM((BOC_RING, WROWS, D_MODEL), F32),
                        pltpu.SemaphoreType.DMA((BOC_RING,))],
        compiler_params=pltpu.CompilerParams(
            dimension_semantics=("arbitrary",),
            vmem_limit_bytes=BIG_VMEM_LIMIT),
        name="branch_out_cast",
    )(o, z, x, wb, wo, g_post)


def kernel(x_prompt, x_sample, mem_prompt, state_rglru_h, state_conv, state_pool, cache_mem_k, cache_mem_v, g_pre, w_in, conv_w, conv_b, w_rg_a, b_rg_a, w_rg_x, b_rg_x, lru_lambda, w_pool, pool_scale, g_mem, w_kv, w_branch, w_out, g_post):
    batch, seq, _ = x_prompt.shape
    nb = x_sample.shape[0]
    depth = g_pre.shape[0]
    assert depth == 1 and x_sample.shape[1] == 1

    l = 0
    row = lambda v: v.reshape(1, -1)
    wax = jnp.concatenate([w_rg_a[l], w_rg_x[l]], axis=-1).astype(BF16)
    wpool = w_pool[l].astype(BF16)
    mix_params = (conv_w[l], row(conv_b[l]), wax, row(b_rg_a[l]), row(b_rg_x[l]),
                  row(lru_lambda[l]), wpool, row(pool_scale[l]))

    xp2 = x_prompt.reshape(batch * seq, D_MODEL)
    xs2 = x_sample.reshape(nb, D_MODEL)
    mem2 = mem_prompt.reshape(batch * N_MEM, D_MODEL)

    z_s, w_in_b, mem_k, mem_v = _sample_proj(xs2, row(g_pre[l]), w_in[l], mem2, row(g_mem[l]),
                                             w_kv[l], tk=SAMPLE_PROJ_TK)
    qoff = 2 * D_RNN + 2 * D_POOL
    q_s = z_s[:, qoff:qoff + D_X].reshape(nb // ATTN_BB, ATTN_BB, D_X)

    mem_k = mem_k.reshape(batch, N_MEM, D_X)
    mem_v = mem_v.reshape(batch, N_MEM, D_X)

    perm = _chunk_interleave()
    z_p = _prompt_proj(xp2, row(g_pre[l]), w_in_b, perm, tm=PROJ_TM, tn=PROJ_TN)
    o_p, h_p, c_p, p_p, attn_s = _prompt_mix(
        z_p, mem_k, mem_v, *mix_params, perm.T, q_s, _cache_rows(cache_mem_k[l]),
        _cache_rows(cache_mem_v[l]), batch=batch, seq=seq, tm=MIX_TM)
    attn_s = attn_s.reshape(nb, D_X)

    o_s, h_s, c_s, p_s = _sample_mix(
        z_s, attn_s, state_conv[l].transpose(1, 0, 2), state_rglru_h[l],
        state_pool[l].transpose(1, 0, 2), *mix_params, tb=SAMPLE_MIX_TB)
    y_s, w_br_b, w_out_b = _branch_out_cast(o_s, z_s, xs2, w_branch[l], w_out[l], row(g_post[l]))

    y_p = _branch_out(o_p, z_p, xp2, w_br_b, w_out_b, row(g_post[l]), tm=BRANCH_TM)

    return (
        y_p.reshape(batch, seq, D_MODEL),
        y_s.reshape(nb, 1, D_MODEL),
        h_p.reshape(1, batch, D_RNN),
        c_p.reshape(1, batch, CONV_W - 1, D_RNN),
        p_p.reshape(1, batch, POOL_HIST, D_POOL),
        mem_k.reshape(1, batch, N_MEM, N_XHEADS, XHEAD_DIM),
        mem_v.reshape(1, batch, N_MEM, N_XHEADS, XHEAD_DIM),
        h_s.reshape(1, nb, D_RNN),
        c_s.transpose(1, 0, 2)[None],
        p_s.transpose(1, 0, 2)[None],
    )
```

```python
import functools

import jax
import jax.numpy as jnp
from jax import lax
from jax.experimental import pallas as pl
from jax.experimental.pallas import tpu as pltpu

D_MODEL = 2048
PAST_LEN = 16384
D_RNN = 1024
N_RNN_BLOCKS = 8
RNN_BLOCK = D_RNN // N_RNN_BLOCKS
CONV_W = 4
LRU_C = 8.0
D_POOL = 1024
POOL_WINDOWS = (2, 4, 8, 16)
POOL_GROUP = D_POOL // len(POOL_WINDOWS)
POOL_HIST = max(POOL_WINDOWS) - 1
N_MEM = 256
N_XHEADS = 4
XHEAD_DIM = 256
D_X = N_XHEADS * XHEAD_DIM
N_BRANCH = 3
D_MIX = D_RNN + D_POOL + D_X
D_IN = 2 * D_MIX + N_BRANCH * D_MODEL
EPS = 1e-6

SUBLANES = 8
LANES = 128
VMEM_LIMIT = 56 * 1024 * 1024
BIG_VMEM_LIMIT = 60 * 1024 * 1024
MIX_TM = 256
BRANCH_TM = 256
PROJ_TM, PROJ_TN = 1024, 2048
SAMPLE_PROJ_TN = 768
ATTN_BB = 4
CACHE_RING = 3
W_RING = 3
BOC_RING = 3
SAMPLE_MIX_TB = 64

BF16 = jnp.bfloat16
F32 = jnp.float32

NEG_LOG2_E = -1.4426950408889634


def _sigmoid(x):
    return 1.0 / (1.0 + jnp.exp2(x * NEG_LOG2_E))


def _silu(x):
    return x * _sigmoid(x)


def _softplus(x):
    return jnp.maximum(x, 0.0) + jnp.log1p(jnp.exp(-jnp.abs(x)))


def _rms_scale(x):
    return lax.rsqrt(jnp.mean(x * x, axis=-1, keepdims=True) + EPS)


def _chunk_interleave():
    nrow = MIX_TM // SUBLANES
    p = jnp.arange(MIX_TM)
    token = (p % SUBLANES) * nrow + p // SUBLANES
    return (token[:, None] == jnp.arange(MIX_TM)[None, :]).astype(BF16)


def _sample_proj_kernel(x_ref, g_ref, w_hbm, mem_ref, gm_ref, wkv_ref,
                        o_ref, wb_ref, k_ref, v_ref, u_ref, um_ref, w_ring, ring_sem,
                        *, k_steps, tn):
    j = pl.program_id(0)
    nsteps = pl.num_programs(0)

    def w_copy(blk, slot):
        return pltpu.make_async_copy(w_hbm.at[:, pl.ds(blk * tn, tn)], w_ring.at[slot],
                                     ring_sem.at[slot])

    @pl.when(j == 0)
    def _():
        for first in range(W_RING - 1):
            w_copy(first, first).start()

    ahead = j + (W_RING - 1)

    @pl.when(ahead < nsteps)
    def _():
        w_copy(ahead, ahead % W_RING).start()

    @pl.when(j == 0)
    def _():
        x = x_ref[...]
        u_ref[...] = (x * _rms_scale(x) * g_ref[...]).astype(BF16)
        mem = mem_ref[...]
        um_ref[...] = (mem * _rms_scale(mem) * gm_ref[...]).astype(BF16)

    slot = j % W_RING
    w_copy(j, slot).wait()
    w = w_ring[slot].astype(BF16)
    wb_ref[...] = w
    o_ref[...] = jnp.dot(u_ref[...], w, preferred_element_type=F32)

    kv = jnp.dot(um_ref[...], wkv_ref[...].astype(BF16), preferred_element_type=F32)

    @pl.when(j < k_steps)
    def _():
        k_ref[...] = kv

    @pl.when(j >= k_steps)
    def _():
        v_ref[...] = kv


def _sample_proj(x, g, w, mem, g_mem, w_kv, tn):
    m, k = x.shape
    n = w.shape[1]
    steps = n // tn
    mrows = mem.shape[0]
    kv_tn = 2 * D_X // steps
    assert D_X % kv_tn == 0 and kv_tn % LANES == 0
    k_steps = D_X // kv_tn
    return pl.pallas_call(
        functools.partial(_sample_proj_kernel, k_steps=k_steps, tn=tn),
        grid=(steps,),
        in_specs=[
            pl.BlockSpec((m, k), lambda j: (0, 0)),
            pl.BlockSpec((1, k), lambda j: (0, 0)),
            pl.BlockSpec(memory_space=pl.ANY),
            pl.BlockSpec((mrows, k), lambda j: (0, 0), pipeline_mode=pl.Buffered(1)),
            pl.BlockSpec((1, k), lambda j: (0, 0)),
            pl.BlockSpec((k, kv_tn), lambda j: (0, j)),
        ],
        out_specs=[
            pl.BlockSpec((m, tn), lambda j: (0, j)),
            pl.BlockSpec((k, tn), lambda j: (0, j)),
            pl.BlockSpec((mrows, kv_tn), lambda j: (0, jnp.minimum(j, k_steps - 1))),
            pl.BlockSpec((mrows, kv_tn), lambda j: (0, jnp.maximum(j - k_steps, 0))),
        ],
        out_shape=[
            jax.ShapeDtypeStruct((m, n), F32),
            jax.ShapeDtypeStruct((k, n), BF16),
            jax.ShapeDtypeStruct((mrows, D_X), F32),
            jax.ShapeDtypeStruct((mrows, D_X), F32),
        ],
        scratch_shapes=[pltpu.VMEM((m, k), BF16), pltpu.VMEM((mrows, k), BF16),
                        pltpu.VMEM((W_RING, k, tn), F32), pltpu.SemaphoreType.DMA((W_RING,))],
        compiler_params=pltpu.CompilerParams(
            dimension_semantics=("arbitrary",),
            vmem_limit_bytes=VMEM_LIMIT),
        name="sample_proj",
    )(x, g, w, mem, g_mem, w_kv)


def _decay_rate(lam):
    return _softplus(-lam) * (LRU_C * NEG_LOG2_E)


def _rglru_block(xc, wa, wx, ba, bx, rate):
    wax = jnp.concatenate([wa, wx], axis=1).astype(BF16)
    ri = jnp.dot(xc.astype(BF16), wax, preferred_element_type=F32)
    r = _sigmoid(ri[:, :RNN_BLOCK] + ba)
    i = _sigmoid(ri[:, RNN_BLOCK:] + bx)
    a = jnp.exp2(r * rate)
    one_m = 1.0 - a * a
    mult = jnp.where(one_m > 0.0, one_m * lax.rsqrt(one_m), 0.0)
    return a, mult * i * xc


def _prompt_mix_kernel(z_ref, k_ref, v_ref, convw_ref, convb_ref, wa_ref, wx_ref, ba_ref, bx_ref,
                       lam_ref, wpool_ref, pscale_ref, unperm_ref, sq_ref, sk_hbm, sv_hbm,
                       o_ref, newh_ref, newconv_ref, newpool_ref, sattn_ref,
                       conv_carry, pool_carry, h_carry, kb_ref, vb_ref, ac_scr, hl_scr, op_scr,
                       sk_ring, sv_ring, ring_sem, *, tm):
    l = pl.program_id(1)
    last = pl.num_programs(1) - 1
    nrow = tm // SUBLANES

    @pl.when(l == 0)
    def _():
        conv_carry[...] = jnp.zeros(conv_carry.shape, F32)
        pool_carry[...] = jnp.zeros(pool_carry.shape, F32)
        h_carry[...] = jnp.zeros(h_carry.shape, F32)
        kb_ref[...] = k_ref[0].astype(BF16)
        vb_ref[...] = v_ref[0].astype(BF16)

    step = pl.program_id(0) * pl.num_programs(1) + l
    nsteps = pl.num_programs(0) * pl.num_programs(1)

    def cache_copies(blk, slot):
        rows = pl.ds(blk * ATTN_BB, ATTN_BB)
        return (pltpu.make_async_copy(sk_hbm.at[rows], sk_ring.at[slot], ring_sem.at[slot, 0]),
                pltpu.make_async_copy(sv_hbm.at[rows], sv_ring.at[slot], ring_sem.at[slot, 1]))

    @pl.when(step == 0)
    def _():
        for first in range(CACHE_RING - 1):
            for cp in cache_copies(first, first):
                cp.start()

    ahead = step + (CACHE_RING - 1)

    @pl.when(ahead < nsteps)
    def _():
        for cp in cache_copies(ahead, ahead % CACHE_RING):
            cp.start()

    slot = step % CACHE_RING
    for cp in cache_copies(step, slot):
        cp.wait()
    sk_ref = sk_ring.at[slot]
    sv_ref = sv_ring.at[slot]
    side_scores = _sample_attn_scores(sq_ref.at[0], sk_ref, ATTN_BB)

    chunk_id = lax.broadcasted_iota(jnp.int32, (SUBLANES, LANES), 0)
    first_chunk = chunk_id == 0

    def load_groups(col, width=LANES):
        return [z_ref[r * SUBLANES:(r + 1) * SUBLANES, col:col + width] for r in range(nrow)]

    def put(col, width, val):
        op_scr[:, col:col + width] = val.astype(BF16)

    def store_groups(col, rows, width=LANES):
        put(col, width, jnp.concatenate(rows, axis=0))

    def history(tail_group, carry_ref, j, c0):
        tail = pltpu.roll(tail_group, 1, 0)
        prev = jnp.where(first_chunk, carry_ref[j - 1, :, c0:c0 + LANES], tail)
        carry_ref[j - 1, :, c0:c0 + LANES] = tail
        return prev

    rate = _decay_rate(lam_ref[...])
    for n in range(N_RNN_BLOCKS):
        c0, c1 = n * RNN_BLOCK, (n + 1) * RNN_BLOCK
        xs = load_groups(c0)
        ext = [history(xs[nrow - j], conv_carry, j, c0) for j in range(CONV_W - 1, 0, -1)] + xs
        cw = [jnp.broadcast_to(convw_ref[k:k + 1, c0:c1], (SUBLANES, LANES)) for k in range(CONV_W)]
        cb = jnp.broadcast_to(convb_ref[:, c0:c1], (SUBLANES, LANES))
        xc = []
        for r in range(nrow):
            acc = cb + cw[0] * ext[r]
            for k in range(1, CONV_W):
                acc = acc + cw[k] * ext[r + k]
            xc.append(acc)
        a, b = _rglru_block(jnp.concatenate(xc, axis=0), wa_ref[n], wx_ref[n], ba_ref[:, c0:c1],
                            bx_ref[:, c0:c1], rate[:, c0:c1])
        ac_scr[:, c0:c1] = a
        hl_scr[:, c0:c1] = b

    side_probs = _sample_attn_probs(side_scores)

    acc_a = ac_scr[0:SUBLANES, :]
    acc_h = hl_scr[0:SUBLANES, :]
    for r in range(1, nrow):
        rows = slice(r * SUBLANES, (r + 1) * SUBLANES)
        ar = ac_scr[rows, :]
        acc_h = ar * acc_h + hl_scr[rows, :]
        acc_a = ar * acc_a
        ac_scr[rows, :] = acc_a
        hl_scr[rows, :] = acc_h
    h_in = h_carry[...]
    entering = []
    for c in range(SUBLANES):
        entering.append(h_in)
        h_in = acc_a[c:c + 1] * h_in + acc_h[c:c + 1]
    h_carry[...] = h_in
    h_enter = jnp.concatenate(entering, axis=0)
    for n in range(N_RNN_BLOCKS):
        c0, c1 = n * RNN_BLOCK, (n + 1) * RNN_BLOCK
        gr = load_groups(D_RNN + c0)
        store_groups(c0, [(hl_scr[r * SUBLANES:(r + 1) * SUBLANES, c0:c1]
                           + ac_scr[r * SUBLANES:(r + 1) * SUBLANES, c0:c1] * h_enter[:, c0:c1])
                          * _silu(gr[r]) for r in range(nrow)])

    _sample_attn_values(side_probs, sv_ref, sattn_ref.at[0])

    pcol = 2 * D_RNN
    blocks = [(w, c0) for g, w in enumerate(POOL_WINDOWS)
              for c0 in range(g * POOL_GROUP, (g + 1) * POOL_GROUP, LANES)]

    def group(c0, r):
        return z_ref[r * SUBLANES:(r + 1) * SUBLANES, pcol + c0:pcol + c0 + LANES]

    def mean_minus_token(tot, w, c0, r):
        if r < w - 1:
            pos1 = l * tm + chunk_id * nrow + (r + 1)
            mean = tot / jnp.minimum(pos1, w).astype(F32)
        else:
            mean = tot * (1.0 / w)
        return mean - group(c0, r)

    hist, tot = {}, {}
    for w, c0 in blocks:
        hist[c0] = [history(group(c0, nrow - j), pool_carry, j, c0) for j in range(1, w)]
        t = group(c0, 0)
        for h in hist[c0]:
            t = t + h
        tot[c0] = t
        hl_scr[0:SUBLANES, c0:c0 + LANES] = mean_minus_token(t, w, c0, 0)
    for r in range(1, nrow):
        for w, c0 in blocks:
            leaving = group(c0, r - w) if r >= w else hist[c0][w - r - 1]
            tot[c0] = tot[c0] + (group(c0, r) - leaving)
            hl_scr[r * SUBLANES:(r + 1) * SUBLANES, c0:c0 + LANES] = mean_minus_token(
                tot[c0], w, c0, r)
    for g, w in enumerate(POOL_WINDOWS):
        c0, c1 = g * POOL_GROUP, (g + 1) * POOL_GROUP
        og = jnp.dot(hl_scr[:, c0:c1].astype(BF16), wpool_ref[g].astype(BF16),
                     preferred_element_type=F32)
        gp = z_ref[:, pcol + D_POOL + c0:pcol + D_POOL + c1]
        put(D_RNN + c0, POOL_GROUP, og * pscale_ref[:, c0:c1] * _silu(gp))

    qoff = 2 * D_RNN + 2 * D_POOL
    for hd in range(N_XHEADS):
        c0, c1 = hd * XHEAD_DIM, (hd + 1) * XHEAD_DIM
        q = z_ref[:, qoff + c0:qoff + c1].astype(BF16)
        s = lax.dot_general(q, kb_ref[:, c0:c1], (((1,), (1,)), ((), ())),
                            preferred_element_type=F32) * (XHEAD_DIM ** -0.5)
        p = jnp.exp(s - jnp.max(s, axis=-1, keepdims=True))
        p = p / jnp.sum(p, axis=-1, keepdims=True)
        ox = jnp.dot(p.astype(BF16), vb_ref[:, c0:c1], preferred_element_type=F32)
        gx = z_ref[:, qoff + D_X + c0:qoff + D_X + c1]
        put(D_RNN + D_POOL + c0, XHEAD_DIM, ox * _silu(gx))

    o_ref[...] = jnp.dot(unperm_ref[...], op_scr[...], preferred_element_type=F32).astype(BF16)

    @pl.when(l == last)
    def _():
        newh_ref[0] = h_carry[...]
        tail_row = lambda j: (nrow - j) * SUBLANES + SUBLANES - 1
        for j in range(1, CONV_W):
            newconv_ref[0, CONV_W - 1 - j:CONV_W - j, :] = z_ref[tail_row(j):tail_row(j) + 1, 0:D_RNN]
        for j in range(1, POOL_HIST + 1):
            newpool_ref[0, POOL_HIST - j:POOL_HIST - j + 1, :] = (
                z_ref[tail_row(j):tail_row(j) + 1, pcol:pcol + D_POOL])


def _prompt_mix(z, mem_k, mem_v, conv_w, conv_b, wa, wx, b_a, b_x, lam, wpool, pscale, unperm,
                sample_q, cache_k, cache_v, batch, seq, tm):
    nl = seq // tm
    assert sample_q.shape[0] == batch * nl
    side = lambda b, l: (b * nl + l, 0, 0)
    zw = 2 * D_MIX
    const2 = lambda b, l: (0, 0)
    const3 = lambda b, l: (0, 0, 0)
    kern = functools.partial(_prompt_mix_kernel, tm=tm)
    return pl.pallas_call(
        kern,
        grid=(batch, nl),
        in_specs=[
            pl.BlockSpec((tm, zw), lambda b, l: (b * nl + l, 0)),
            pl.BlockSpec((1, N_MEM, D_X), lambda b, l: (b, 0, 0)),
            pl.BlockSpec((1, N_MEM, D_X), lambda b, l: (b, 0, 0)),
            pl.BlockSpec((CONV_W, D_RNN), const2),
            pl.BlockSpec((1, D_RNN), const2),
            pl.BlockSpec((N_RNN_BLOCKS, RNN_BLOCK, RNN_BLOCK), const3),
            pl.BlockSpec((N_RNN_BLOCKS, RNN_BLOCK, RNN_BLOCK), const3),
            pl.BlockSpec((1, D_RNN), const2),
            pl.BlockSpec((1, D_RNN), const2),
            pl.BlockSpec((1, D_RNN), const2),
            pl.BlockSpec((len(POOL_WINDOWS), POOL_GROUP, POOL_GROUP), const3),
            pl.BlockSpec((1, D_POOL), const2),
            pl.BlockSpec((tm, tm), const2),
            pl.BlockSpec((1, ATTN_BB, D_X), side),
            pl.BlockSpec(memory_space=pl.ANY),
            pl.BlockSpec(memory_space=pl.ANY),
        ],
        out_specs=[
            pl.BlockSpec((tm, D_MIX), lambda b, l: (b * nl + l, 0)),
            pl.BlockSpec((1, 1, D_RNN), lambda b, l: (b, 0, 0)),
            pl.BlockSpec((1, CONV_W - 1, D_RNN), lambda b, l: (b, 0, 0)),
            pl.BlockSpec((1, POOL_HIST, D_POOL), lambda b, l: (b, 0, 0)),
            pl.BlockSpec((1, ATTN_BB, D_X), side),
        ],
        out_shape=[
            jax.ShapeDtypeStruct((batch * seq, D_MIX), BF16),
            jax.ShapeDtypeStruct((batch, 1, D_RNN), F32),
            jax.ShapeDtypeStruct((batch, CONV_W - 1, D_RNN), F32),
            jax.ShapeDtypeStruct((batch, POOL_HIST, D_POOL), F32),
            jax.ShapeDtypeStruct(sample_q.shape, F32),
        ],
        scratch_shapes=[
            pltpu.VMEM((CONV_W - 1, SUBLANES, D_RNN), F32),
            pltpu.VMEM((POOL_HIST, SUBLANES, D_POOL), F32),
            pltpu.VMEM((1, D_RNN), F32),
            pltpu.VMEM((N_MEM, D_X), BF16),
            pltpu.VMEM((N_MEM, D_X), BF16),
            pltpu.VMEM((tm, D_RNN), F32),
            pltpu.VMEM((tm, D_RNN), F32),
            pltpu.VMEM((tm, D_MIX), BF16),
            pltpu.VMEM((CACHE_RING, ATTN_BB, N_MEM * SUBLANES, LANES), F32),
            pltpu.VMEM((CACHE_RING, ATTN_BB, N_MEM * SUBLANES, LANES), F32),
            pltpu.SemaphoreType.DMA((CACHE_RING, 2)),
        ],
        compiler_params=pltpu.CompilerParams(
            dimension_semantics=("arbitrary", "arbitrary"),
            vmem_limit_bytes=VMEM_LIMIT),
        name="prompt_mix",
    )(z, mem_k, mem_v, conv_w, conv_b, wa, wx, b_a, b_x, lam, wpool, pscale, unperm,
      sample_q, cache_k, cache_v)


def _cache_rows(c):
    nb = c.shape[0]
    c = c.reshape(nb, N_MEM, N_XHEADS, XHEAD_DIM // LANES, LANES)
    return c.transpose(0, 1, 3, 2, 4).reshape(nb, N_MEM * SUBLANES, LANES)


def _sample_attn_scores(q_ref, k_ref, bb):
    halves = XHEAD_DIM // LANES
    assert halves * N_XHEADS == SUBLANES
    scores = []
    for j in range(bb):
        qn = jnp.concatenate(
            [q_ref[j:j + 1, (h * halves + t) * LANES:(h * halves + t + 1) * LANES]
             for t in range(halves) for h in range(N_XHEADS)], axis=0)
        scores.append(lax.dot_general(qn.astype(BF16), k_ref[j].astype(BF16),
                                      (((1,), (1,)), ((), ())), preferred_element_type=F32)
                      * (XHEAD_DIM ** -0.5))
    return scores


def _sample_attn_probs(scores):
    r = lax.broadcasted_iota(jnp.int32, (SUBLANES, LANES), 0)
    c = lax.broadcasted_iota(jnp.int32, (SUBLANES, LANES), 1)
    diag = (c % SUBLANES) == r
    first_half = r < N_XHEADS
    nchunk = N_MEM * SUBLANES // LANES
    probs = []
    for s in scores:
        chunks = []
        for ci in range(nchunk):
            sm = jnp.where(diag, s[:, ci * LANES:(ci + 1) * LANES], 0.0)
            other = pltpu.roll(sm, N_XHEADS, 0)
            other = jnp.where(first_half, pltpu.roll(other, LANES - N_XHEADS, 1),
                              pltpu.roll(other, N_XHEADS, 1))
            chunks.append(jnp.where(diag, sm + other, -jnp.inf))
        t_full = jnp.concatenate(chunks, axis=1)
        e = jnp.exp(t_full - jnp.max(t_full, axis=1, keepdims=True))
        probs.append((e / jnp.sum(e, axis=1, keepdims=True)).astype(BF16))
    return probs


def _sample_attn_values(probs, v_ref, o_ref):
    halves = XHEAD_DIM // LANES
    for j, p in enumerate(probs):
        o = jnp.dot(p, v_ref[j].astype(BF16), preferred_element_type=F32)
        for t in range(halves):
            for h in range(N_XHEADS):
                col = (h * halves + t) * LANES
                o_ref[j:j + 1, col:col + LANES] = o[t * N_XHEADS + h:t * N_XHEADS + h + 1, :]


def _prompt_proj_kernel(x_ref, g_ref, w_ref, perm_ref, o_ref, u_ref, up_ref, *, mix_steps):
    j = pl.program_id(1)

    @pl.when(j == 0)
    def _():
        x = x_ref[...]
        u = (x * _rms_scale(x) * g_ref[...]).astype(BF16)
        u_ref[...] = u
        for r0 in range(0, u.shape[0], MIX_TM):
            up_ref[r0:r0 + MIX_TM, :] = jnp.dot(
                perm_ref[...], u[r0:r0 + MIX_TM], preferred_element_type=F32).astype(BF16)

    @pl.when(j < mix_steps)
    def _():
        o_ref[...] = jnp.dot(up_ref[...], w_ref[...], preferred_element_type=F32)

    @pl.when(j >= mix_steps)
    def _():
        o_ref[...] = jnp.dot(u_ref[...], w_ref[...], preferred_element_type=F32)


def _prompt_proj(x, g, w, perm, tm, tn):
    m, k = x.shape
    n = w.shape[1]
    assert (2 * D_MIX) % tn == 0
    return pl.pallas_call(
        functools.partial(_prompt_proj_kernel, mix_steps=2 * D_MIX // tn),
        grid=(m // tm, n // tn),
        in_specs=[
            pl.BlockSpec((tm, k), lambda i, j: (i, 0)),
            pl.BlockSpec((1, k), lambda i, j: (0, 0)),
            pl.BlockSpec((k, tn), lambda i, j: (0, j)),
            pl.BlockSpec(perm.shape, lambda i, j: (0, 0)),
        ],
        out_specs=pl.BlockSpec((tm, tn), lambda i, j: (i, j)),
        out_shape=jax.ShapeDtypeStruct((m, n), F32),
        scratch_shapes=[pltpu.VMEM((tm, k), BF16), pltpu.VMEM((tm, k), BF16)],
        compiler_params=pltpu.CompilerParams(
            dimension_semantics=("arbitrary", "arbitrary"),
            vmem_limit_bytes=BIG_VMEM_LIMIT),
        name="prompt_proj",
    )(x, g, w, perm)


def _sample_mix_kernel(z_ref, attn_ref, conv_ref, h_ref, pool_ref,
                       convw_ref, convb_ref, wa_ref, wx_ref, ba_ref, bx_ref, lam_ref, wpool_ref,
                       pscale_ref, o_ref, newh_ref, newconv_ref, newpool_ref):
    xr = z_ref[:, 0:D_RNN]
    xc = convb_ref[...] + convw_ref[CONV_W - 1:CONV_W, :] * xr
    for k in range(CONV_W - 1):
        xc = xc + convw_ref[k:k + 1, :] * conv_ref[k]
    for k in range(CONV_W - 2):
        newconv_ref[k] = conv_ref[k + 1]
    newconv_ref[CONV_W - 2] = xr

    rate = _decay_rate(lam_ref[...])
    for n in range(N_RNN_BLOCKS):
        c0, c1 = n * RNN_BLOCK, (n + 1) * RNN_BLOCK
        a, b = _rglru_block(xc[:, c0:c1], wa_ref[n], wx_ref[n], ba_ref[:, c0:c1], bx_ref[:, c0:c1],
                            rate[:, c0:c1])
        h = a * h_ref[:, c0:c1] + b
        newh_ref[:, c0:c1] = h
        o_ref[:, c0:c1] = (h * _silu(z_ref[:, D_RNN + c0:D_RNN + c1])).astype(BF16)

    xp = z_ref[:, 2 * D_RNN:2 * D_RNN + D_POOL]
    for k in range(POOL_HIST - 1):
        newpool_ref[k] = pool_ref[k + 1]
    newpool_ref[POOL_HIST - 1] = xp
    for g, w in enumerate(POOL_WINDOWS):
        c0, c1 = g * POOL_GROUP, (g + 1) * POOL_GROUP
        xg = xp[:, c0:c1]
        tot = xg
        for j in range(1, w):
            tot = tot + pool_ref[POOL_HIST - j, :, c0:c1]
        cnt = float(min(PAST_LEN + 1, w))
        d = tot / cnt - xg
        og = jnp.dot(d.astype(BF16), wpool_ref[g].astype(BF16), preferred_element_type=F32)
        gp = z_ref[:, 2 * D_RNN + D_POOL + c0:2 * D_RNN + D_POOL + c1]
        o_ref[:, D_RNN + c0:D_RNN + c1] = (og * pscale_ref[:, c0:c1] * _silu(gp)).astype(BF16)

    gx = z_ref[:, 2 * D_RNN + 2 * D_POOL + D_X:2 * D_MIX]
    o_ref[:, D_RNN + D_POOL:] = (attn_ref[...] * _silu(gx)).astype(BF16)


def _sample_mix(z, attn, conv, h, pool, conv_w, conv_b, wa, wx, b_a, b_x, lam, wpool, pscale, tb):
    nb = z.shape[0]
    zw = 2 * D_MIX
    rows = lambda i: (i, 0)
    const2 = lambda i: (0, 0)
    const3 = lambda i: (0, 0, 0)
    hist = lambda i: (0, i, 0)
    return pl.pallas_call(
        _sample_mix_kernel,
        grid=(nb // tb,),
        in_specs=[
            pl.BlockSpec((tb, zw), rows),
            pl.BlockSpec((tb, D_X), rows),
            pl.BlockSpec((CONV_W - 1, tb, D_RNN), hist),
            pl.BlockSpec((tb, D_RNN), rows),
            pl.BlockSpec((POOL_HIST, tb, D_POOL), hist),
            pl.BlockSpec((CONV_W, D_RNN), const2),
            pl.BlockSpec((1, D_RNN), const2),
            pl.BlockSpec((N_RNN_BLOCKS, RNN_BLOCK, RNN_BLOCK), const3),
            pl.BlockSpec((N_RNN_BLOCKS, RNN_BLOCK, RNN_BLOCK), const3),
            pl.BlockSpec((1, D_RNN), const2),
            pl.BlockSpec((1, D_RNN), const2),
            pl.BlockSpec((1, D_RNN), const2),
            pl.BlockSpec((len(POOL_WINDOWS), POOL_GROUP, POOL_GROUP), const3),
            pl.BlockSpec((1, D_POOL), const2),
        ],
        out_specs=[
            pl.BlockSpec((tb, D_MIX), rows),
            pl.BlockSpec((tb, D_RNN), rows),
            pl.BlockSpec((CONV_W - 1, tb, D_RNN), hist),
            pl.BlockSpec((POOL_HIST, tb, D_POOL), hist),
        ],
        out_shape=[
            jax.ShapeDtypeStruct((nb, D_MIX), BF16),
            jax.ShapeDtypeStruct((nb, D_RNN), F32),
            jax.ShapeDtypeStruct((CONV_W - 1, nb, D_RNN), F32),
            jax.ShapeDtypeStruct((POOL_HIST, nb, D_POOL), F32),
        ],
        compiler_params=pltpu.CompilerParams(
            dimension_semantics=("arbitrary",),
            vmem_limit_bytes=VMEM_LIMIT),
        name="sample_mix",
    )(z, attn, conv, h, pool, conv_w, conv_b, wa, wx, b_a, b_x, lam, wpool, pscale)


def _branch_out_kernel(o_ref, gates_ref, x_ref, wb_ref, wo_ref, gpost_ref, y_ref):
    merged = None
    for j, (r0, r1) in enumerate(((0, D_RNN), (D_RNN, D_RNN + D_POOL), (D_RNN + D_POOL, D_MIX))):
        yj = jnp.dot(o_ref[:, r0:r1], wb_ref[r0:r1, :], preferred_element_type=F32)
        term = _sigmoid(gates_ref[:, j * D_MODEL:(j + 1) * D_MODEL]) * yj
        merged = term if merged is None else merged + term
    out = jnp.dot(merged.astype(BF16), wo_ref[...], preferred_element_type=F32)
    y_ref[...] = x_ref[...] + (out * gpost_ref[...]) * _rms_scale(out)


def _branch_out(o, z, x, wb, wo, g_post, tm):
    m = x.shape[0]
    gw = N_BRANCH * D_MODEL
    gblk = (2 * D_MIX) // gw
    resident = pl.Buffered(1)
    return pl.pallas_call(
        _branch_out_kernel,
        grid=(m // tm,),
        in_specs=[
            pl.BlockSpec((tm, D_MIX), lambda i: (i, 0)),
            pl.BlockSpec((tm, gw), lambda i: (i, gblk)),
            pl.BlockSpec((tm, D_MODEL), lambda i: (i, 0)),
            pl.BlockSpec((D_MIX, D_MODEL), lambda i: (0, 0), pipeline_mode=resident),
            pl.BlockSpec((D_MODEL, D_MODEL), lambda i: (0, 0), pipeline_mode=resident),
            pl.BlockSpec((1, D_MODEL), lambda i: (0, 0)),
        ],
        out_specs=pl.BlockSpec((tm, D_MODEL), lambda i: (i, 0)),
        out_shape=jax.ShapeDtypeStruct((m, D_MODEL), F32),
        compiler_params=pltpu.CompilerParams(
            dimension_semantics=("arbitrary",),
            vmem_limit_bytes=VMEM_LIMIT),
        name="branch_out",
    )(o, z, x, wb, wo, g_post)


WROWS = 1024
PER_BRANCH = D_RNN // WROWS
assert D_RNN == D_POOL == D_X and D_RNN % WROWS == 0 and D_MODEL % WROWS == 0
N_WB_BLOCKS = N_BRANCH * PER_BRANCH
N_WOUT_BLOCKS = D_MODEL // WROWS


def _branch_out_cast_kernel(o_ref, gates_ref, x_ref, wb_hbm, wo_hbm, gpost_ref,
                            y_ref, wbb_ref, wob_ref, merged_ref, out_ref, w_ring, ring_sem):
    s = pl.program_id(0)
    nsteps = N_WB_BLOCKS + N_WOUT_BLOCKS

    def w_copy(blk):
        src, first = (wb_hbm, 0) if blk < N_WB_BLOCKS else (wo_hbm, N_WB_BLOCKS)
        slot = blk % BOC_RING
        return pltpu.make_async_copy(src.at[pl.ds((blk - first) * WROWS, WROWS)],
                                     w_ring.at[slot], ring_sem.at[slot])

    for t in range(nsteps):
        @pl.when(s == t)
        def _(t=t):
            if t == 0:
                for first in range(min(BOC_RING - 1, nsteps)):
                    w_copy(first).start()
            if t + BOC_RING - 1 < nsteps:
                w_copy(t + BOC_RING - 1).start()
            w_copy(t).wait()
            w = w_ring[t % BOC_RING].astype(BF16)
            if t < N_WB_BLOCKS:
                wbb_ref[...] = w
                term = _sigmoid(gates_ref[...]) * jnp.dot(o_ref[...], w,
                                                          preferred_element_type=F32)
                if t == 0:
                    merged_ref[...] = term
                else:
                    merged_ref[...] += term
            else:
                kb = t - N_WB_BLOCKS
                wob_ref[...] = w
                part = jnp.dot(merged_ref[:, kb * WROWS:(kb + 1) * WROWS].astype(BF16), w,
                               preferred_element_type=F32)
                if kb == 0:
                    out_ref[...] = part
                else:
                    out_ref[...] += part
            if t == nsteps - 1:
                out = out_ref[...]
                y_ref[...] = x_ref[...] + out * _rms_scale(out) * gpost_ref[...]


def _branch_out_cast(o, z, x, wb, wo, g_post):
    m = x.shape[0]
    gblk0 = (2 * D_MIX) // D_MODEL
    wb_blk = lambda s: jnp.minimum(s, N_WB_BLOCKS - 1)
    wo_blk = lambda s: jnp.maximum(s - N_WB_BLOCKS, 0)
    return pl.pallas_call(
        _branch_out_cast_kernel,
        grid=(N_WB_BLOCKS + N_WOUT_BLOCKS,),
        in_specs=[
            pl.BlockSpec((m, WROWS), lambda s: (0, wb_blk(s))),
            pl.BlockSpec((m, D_MODEL), lambda s: (0, gblk0 + wb_blk(s) // PER_BRANCH)),
            pl.BlockSpec((m, D_MODEL), lambda s: (0, 0)),
            pl.BlockSpec(memory_space=pl.ANY),
            pl.BlockSpec(memory_space=pl.ANY),
            pl.BlockSpec((1, D_MODEL), lambda s: (0, 0)),
        ],
        out_specs=[
            pl.BlockSpec((m, D_MODEL), lambda s: (0, 0)),
            pl.BlockSpec((WROWS, D_MODEL), lambda s: (wb_blk(s), 0)),
            pl.BlockSpec((WROWS, D_MODEL), lambda s: (wo_blk(s), 0)),
        ],
        out_shape=[
            jax.ShapeDtypeStruct((m, D_MODEL), F32),
            jax.ShapeDtypeStruct(wb.shape, BF16),
            jax.ShapeDtypeStruct(wo.shape, BF16),
        ],
        scratch_shapes=[pltpu.VMEM((m, D_MODEL), F32), pltpu.VMEM((m, D_MODEL), F32),
                        pltpu.VMEM((BOC_RING, WROWS, D_MODEL), F32),
                        pltpu.SemaphoreType.DMA((BOC_RING,))],
        compiler_params=pltpu.CompilerParams(
            dimension_semantics=("arbitrary",),
            vmem_limit_bytes=BIG_VMEM_LIMIT),
        name="branch_out_cast",
    )(o, z, x, wb, wo, g_post)


def kernel(x_prompt, x_sample, mem_prompt, state_rglru_h, state_conv, state_pool, cache_mem_k, cache_mem_v, g_pre, w_in, conv_w, conv_b, w_rg_a, b_rg_a, w_rg_x, b_rg_x, lru_lambda, w_pool, pool_scale, g_mem, w_kv, w_branch, w_out, g_post):
    batch, seq, _ = x_prompt.shape
    nb = x_sample.shape[0]
    depth = g_pre.shape[0]
    assert depth == 1 and x_sample.shape[1] == 1

    l = 0
    row = lambda v: v.reshape(1, -1)
    mix_params = (conv_w[l], row(conv_b[l]), w_rg_a[l], w_rg_x[l], row(b_rg_a[l]), row(b_rg_x[l]),
                  row(lru_lambda[l]), w_pool[l], row(pool_scale[l]))

    xp2 = x_prompt.reshape(batch * seq, D_MODEL)
    xs2 = x_sample.reshape(nb, D_MODEL)
    mem2 = mem_prompt.reshape(batch * N_MEM, D_MODEL)

    z_s, w_in_b, mem_k, mem_v = _sample_proj(xs2, row(g_pre[l]), w_in[l], mem2, row(g_mem[l]),
                                             w_kv[l], tn=SAMPLE_PROJ_TN)
    qoff = 2 * D_RNN + 2 * D_POOL
    q_s = z_s[:, qoff:qoff + D_X].reshape(nb // ATTN_BB, ATTN_BB, D_X)

    mem_k = mem_k.reshape(batch, N_MEM, D_X)
    mem_v = mem_v.reshape(batch, N_MEM, D_X)

    perm = _chunk_interleave()
    z_p = _prompt_proj(xp2, row(g_pre[l]), w_in_b, perm, tm=PROJ_TM, tn=PROJ_TN)
    o_p, h_p, c_p, p_p, attn_s = _prompt_mix(
        z_p, mem_k, mem_v, *mix_params, perm.T, q_s, _cache_rows(cache_mem_k[l]),
        _cache_rows(cache_mem_v[l]), batch=batch, seq=seq, tm=MIX_TM)
    attn_s = attn_s.reshape(nb, D_X)

    o_s, h_s, c_s, p_s = _sample_mix(
        z_s, attn_s, state_conv[l].transpose(1, 0, 2), state_rglru_h[l],
        state_pool[l].transpose(1, 0, 2), *mix_params, tb=SAMPLE_MIX_TB)
    y_s, w_br_b, w_out_b = _branch_out_cast(o_s, z_s, xs2, w_branch[l], w_out[l], row(g_post[l]))

    y_p = _branch_out(o_p, z_p, xp2, w_br_b, w_out_b, row(g_post[l]), tm=BRANCH_TM)

    return (
        y_p.reshape(batch, seq, D_MODEL),
        y_s.reshape(nb, 1, D_MODEL),
        h_p.reshape(1, batch, D_RNN),
        c_p.reshape(1, batch, CONV_W - 1, D_RNN),
        p_p.reshape(1, batch, POOL_HIST, D_POOL),
        mem_k.reshape(1, batch, N_MEM, N_XHEADS, XHEAD_DIM),
        mem_v.reshape(1, batch, N_MEM, N_XHEADS, XHEAD_DIM),
        h_s.reshape(1, nb, D_RNN),
        c_s.transpose(1, 0, 2)[None],
        p_s.transpose(1, 0, 2)[None],
    )
```

```python
import functools

import jax
import jax.numpy as jnp
from jax import lax
from jax.experimental import pallas as pl
from jax.experimental.pallas import tpu as pltpu

D_MODEL = 2048
PAST_LEN = 16384
D_RNN = 1024
N_RNN_BLOCKS = 8
RNN_BLOCK = D_RNN // N_RNN_BLOCKS
CONV_W = 4
LRU_C = 8.0
D_POOL = 1024
POOL_WINDOWS = (2, 4, 8, 16)
POOL_GROUP = D_POOL // len(POOL_WINDOWS)
POOL_HIST = max(POOL_WINDOWS) - 1
N_MEM = 256
N_XHEADS = 4
XHEAD_DIM = 256
D_X = N_XHEADS * XHEAD_DIM
N_BRANCH = 3
D_MIX = D_RNN + D_POOL + D_X
D_IN = 2 * D_MIX + N_BRANCH * D_MODEL
EPS = 1e-6

SUBLANES = 8
LANES = 128
VMEM_LIMIT = 56 * 1024 * 1024
BIG_VMEM_LIMIT = 60 * 1024 * 1024
MIX_TM = 256
BRANCH_TM = 256
PROJ_TM, PROJ_TN = 1024, 2048
SAMPLE_PROJ_TN = 768
ATTN_BB = 4
CACHE_RING = 3
RING_DMA_PRIORITY = 1
W_RING = 3
BOC_RING = 3
SAMPLE_MIX_TB = 64

BF16 = jnp.bfloat16
F32 = jnp.float32

NEG_LOG2_E = -1.4426950408889634


def _sigmoid(x):
    return 1.0 / (1.0 + jnp.exp2(x * NEG_LOG2_E))


def _silu(x):
    return x * _sigmoid(x)


def _softplus(x):
    return jnp.maximum(x, 0.0) + jnp.log1p(jnp.exp(-jnp.abs(x)))


def _rms_scale(x):
    return lax.rsqrt(jnp.mean(x * x, axis=-1, keepdims=True) + EPS)


def _chunk_interleave():
    nrow = MIX_TM // SUBLANES
    p = jnp.arange(MIX_TM)
    token = (p % SUBLANES) * nrow + p // SUBLANES
    return (token[:, None] == jnp.arange(MIX_TM)[None, :]).astype(BF16)


def _sample_proj_kernel(x_ref, g_ref, w_hbm, mem_ref, gm_ref, wkv_ref,
                        o_ref, wb_ref, k_ref, v_ref, u_ref, um_ref, w_ring, ring_sem,
                        *, k_steps, tn):
    j = pl.program_id(0)
    nsteps = pl.num_programs(0)

    def w_copy(blk, slot):
        return pltpu.make_async_copy(w_hbm.at[:, pl.ds(blk * tn, tn)], w_ring.at[slot],
                                     ring_sem.at[slot])

    @pl.when(j == 0)
    def _():
        for first in range(W_RING - 1):
            w_copy(first, first).start()

    ahead = j + (W_RING - 1)

    @pl.when(ahead < nsteps)
    def _():
        w_copy(ahead, ahead % W_RING).start()

    @pl.when(j == 0)
    def _():
        x = x_ref[...]
        u_ref[...] = (x * _rms_scale(x) * g_ref[...]).astype(BF16)
        mem = mem_ref[...]
        um_ref[...] = (mem * _rms_scale(mem) * gm_ref[...]).astype(BF16)

    slot = j % W_RING
    w_copy(j, slot).wait()
    w = w_ring[slot].astype(BF16)
    wb_ref[...] = w
    o_ref[...] = jnp.dot(u_ref[...], w, preferred_element_type=F32)

    kv = jnp.dot(um_ref[...], wkv_ref[...].astype(BF16), preferred_element_type=F32)

    @pl.when(j < k_steps)
    def _():
        k_ref[...] = kv

    @pl.when(j >= k_steps)
    def _():
        v_ref[...] = kv


def _sample_proj(x, g, w, mem, g_mem, w_kv, tn):
    m, k = x.shape
    n = w.shape[1]
    steps = n // tn
    mrows = mem.shape[0]
    kv_tn = 2 * D_X // steps
    assert D_X % kv_tn == 0 and kv_tn % LANES == 0
    k_steps = D_X // kv_tn
    return pl.pallas_call(
        functools.partial(_sample_proj_kernel, k_steps=k_steps, tn=tn),
        grid=(steps,),
        in_specs=[
            pl.BlockSpec((m, k), lambda j: (0, 0)),
            pl.BlockSpec((1, k), lambda j: (0, 0)),
            pl.BlockSpec(memory_space=pl.ANY),
            pl.BlockSpec((mrows, k), lambda j: (0, 0), pipeline_mode=pl.Buffered(1)),
            pl.BlockSpec((1, k), lambda j: (0, 0)),
            pl.BlockSpec((k, kv_tn), lambda j: (0, j)),
        ],
        out_specs=[
            pl.BlockSpec((m, tn), lambda j: (0, j)),
            pl.BlockSpec((k, tn), lambda j: (0, j)),
            pl.BlockSpec((mrows, kv_tn), lambda j: (0, jnp.minimum(j, k_steps - 1))),
            pl.BlockSpec((mrows, kv_tn), lambda j: (0, jnp.maximum(j - k_steps, 0))),
        ],
        out_shape=[
            jax.ShapeDtypeStruct((m, n), F32),
            jax.ShapeDtypeStruct((k, n), BF16),
            jax.ShapeDtypeStruct((mrows, D_X), F32),
            jax.ShapeDtypeStruct((mrows, D_X), F32),
        ],
        scratch_shapes=[pltpu.VMEM((m, k), BF16), pltpu.VMEM((mrows, k), BF16),
                        pltpu.VMEM((W_RING, k, tn), F32), pltpu.SemaphoreType.DMA((W_RING,))],
        compiler_params=pltpu.CompilerParams(
            dimension_semantics=("arbitrary",),
            vmem_limit_bytes=VMEM_LIMIT),
        name="sample_proj",
    )(x, g, w, mem, g_mem, w_kv)


def _decay_rate(lam):
    return _softplus(-lam) * (LRU_C * NEG_LOG2_E)


def _rglru_block(xc, wa, wx, ba, bx, rate):
    wax = jnp.concatenate([wa, wx], axis=1).astype(BF16)
    ri = jnp.dot(xc.astype(BF16), wax, preferred_element_type=F32)
    r = _sigmoid(ri[:, :RNN_BLOCK] + ba)
    i = _sigmoid(ri[:, RNN_BLOCK:] + bx)
    a = jnp.exp2(r * rate)
    one_m = 1.0 - a * a
    mult = jnp.where(one_m > 0.0, one_m * lax.rsqrt(one_m), 0.0)
    return a, mult * i * xc


def _prompt_mix_kernel(z_ref, k_ref, v_ref, convw_ref, convb_ref, wa_ref, wx_ref, ba_ref, bx_ref,
                       lam_ref, wpool_ref, pscale_ref, unperm_ref, sq_ref, sk_hbm, sv_hbm,
                       o_ref, newh_ref, newconv_ref, newpool_ref, sattn_ref,
                       conv_carry, pool_carry, h_carry, kb_ref, vb_ref, ac_scr, hl_scr, op_scr,
                       sk_ring, sv_ring, ring_sem, *, tm):
    l = pl.program_id(1)
    last = pl.num_programs(1) - 1
    nrow = tm // SUBLANES

    @pl.when(l == 0)
    def _():
        conv_carry[...] = jnp.zeros(conv_carry.shape, F32)
        pool_carry[...] = jnp.zeros(pool_carry.shape, F32)
        h_carry[...] = jnp.zeros(h_carry.shape, F32)
        kb_ref[...] = k_ref[0].astype(BF16)
        vb_ref[...] = v_ref[0].astype(BF16)

    step = pl.program_id(0) * pl.num_programs(1) + l
    nsteps = pl.num_programs(0) * pl.num_programs(1)

    def cache_copies(blk, slot):
        rows = pl.ds(blk * ATTN_BB, ATTN_BB)
        return (pltpu.make_async_copy(sk_hbm.at[rows], sk_ring.at[slot], ring_sem.at[slot, 0]),
                pltpu.make_async_copy(sv_hbm.at[rows], sv_ring.at[slot], ring_sem.at[slot, 1]))

    @pl.when(step == 0)
    def _():
        for first in range(CACHE_RING - 1):
            for cp in cache_copies(first, first):
                cp.start(priority=RING_DMA_PRIORITY)

    ahead = step + (CACHE_RING - 1)

    @pl.when(ahead < nsteps)
    def _():
        for cp in cache_copies(ahead, ahead % CACHE_RING):
            cp.start(priority=RING_DMA_PRIORITY)

    slot = step % CACHE_RING
    for cp in cache_copies(step, slot):
        cp.wait()
    sk_ref = sk_ring.at[slot]
    sv_ref = sv_ring.at[slot]
    side_scores = _sample_attn_scores(sq_ref.at[0], sk_ref, ATTN_BB)

    chunk_id = lax.broadcasted_iota(jnp.int32, (SUBLANES, LANES), 0)
    first_chunk = chunk_id == 0

    def load_groups(col, width=LANES):
        return [z_ref[r * SUBLANES:(r + 1) * SUBLANES, col:col + width] for r in range(nrow)]

    def put(col, width, val):
        op_scr[:, col:col + width] = val.astype(BF16)

    def store_groups(col, rows, width=LANES):
        put(col, width, jnp.concatenate(rows, axis=0))

    def history(tail_group, carry_ref, j, c0):
        tail = pltpu.roll(tail_group, 1, 0)
        prev = jnp.where(first_chunk, carry_ref[j - 1, :, c0:c0 + LANES], tail)
        carry_ref[j - 1, :, c0:c0 + LANES] = tail
        return prev

    rate = _decay_rate(lam_ref[...])
    for n in range(N_RNN_BLOCKS):
        c0, c1 = n * RNN_BLOCK, (n + 1) * RNN_BLOCK
        xs = load_groups(c0)
        ext = [history(xs[nrow - j], conv_carry, j, c0) for j in range(CONV_W - 1, 0, -1)] + xs
        cw = [jnp.broadcast_to(convw_ref[k:k + 1, c0:c1], (SUBLANES, LANES)) for k in range(CONV_W)]
        cb = jnp.broadcast_to(convb_ref[:, c0:c1], (SUBLANES, LANES))
        xc = []
        for r in range(nrow):
            acc = cb + cw[0] * ext[r]
            for k in range(1, CONV_W):
                acc = acc + cw[k] * ext[r + k]
            xc.append(acc)
        a, b = _rglru_block(jnp.concatenate(xc, axis=0), wa_ref[n], wx_ref[n], ba_ref[:, c0:c1],
                            bx_ref[:, c0:c1], rate[:, c0:c1])
        ac_scr[:, c0:c1] = a
        hl_scr[:, c0:c1] = b

    side_probs = _sample_attn_probs(side_scores)

    acc_a = ac_scr[0:SUBLANES, :]
    acc_h = hl_scr[0:SUBLANES, :]
    for r in range(1, nrow):
        rows = slice(r * SUBLANES, (r + 1) * SUBLANES)
        ar = ac_scr[rows, :]
        acc_h = ar * acc_h + hl_scr[rows, :]
        acc_a = ar * acc_a
        ac_scr[rows, :] = acc_a
        hl_scr[rows, :] = acc_h
    h_in = h_carry[...]
    entering = []
    for c in range(SUBLANES):
        entering.append(h_in)
        h_in = acc_a[c:c + 1] * h_in + acc_h[c:c + 1]
    h_carry[...] = h_in
    h_enter = jnp.concatenate(entering, axis=0)
    for n in range(N_RNN_BLOCKS):
        c0, c1 = n * RNN_BLOCK, (n + 1) * RNN_BLOCK
        gr = load_groups(D_RNN + c0)
        store_groups(c0, [(hl_scr[r * SUBLANES:(r + 1) * SUBLANES, c0:c1]
                           + ac_scr[r * SUBLANES:(r + 1) * SUBLANES, c0:c1] * h_enter[:, c0:c1])
                          * _silu(gr[r]) for r in range(nrow)])

    _sample_attn_values(side_probs, sv_ref, sattn_ref.at[0])

    pcol = 2 * D_RNN
    blocks = [(w, c0) for g, w in enumerate(POOL_WINDOWS)
              for c0 in range(g * POOL_GROUP, (g + 1) * POOL_GROUP, LANES)]

    def group(c0, r):
        return z_ref[r * SUBLANES:(r + 1) * SUBLANES, pcol + c0:pcol + c0 + LANES]

    def mean_minus_token(tot, w, c0, r):
        if r < w - 1:
            pos1 = l * tm + chunk_id * nrow + (r + 1)
            mean = tot / jnp.minimum(pos1, w).astype(F32)
        else:
            mean = tot * (1.0 / w)
        return mean - group(c0, r)

    hist, tot = {}, {}
    for w, c0 in blocks:
        hist[c0] = [history(group(c0, nrow - j), pool_carry, j, c0) for j in range(1, w)]
        t = group(c0, 0)
        for h in hist[c0]:
            t = t + h
        tot[c0] = t
        hl_scr[0:SUBLANES, c0:c0 + LANES] = mean_minus_token(t, w, c0, 0)
    for r in range(1, nrow):
        for w, c0 in blocks:
            leaving = group(c0, r - w) if r >= w else hist[c0][w - r - 1]
            tot[c0] = tot[c0] + (group(c0, r) - leaving)
            hl_scr[r * SUBLANES:(r + 1) * SUBLANES, c0:c0 + LANES] = mean_minus_token(
                tot[c0], w, c0, r)
    for g, w in enumerate(POOL_WINDOWS):
        c0, c1 = g * POOL_GROUP, (g + 1) * POOL_GROUP
        og = jnp.dot(hl_scr[:, c0:c1].astype(BF16), wpool_ref[g].astype(BF16),
                     preferred_element_type=F32)
        gp = z_ref[:, pcol + D_POOL + c0:pcol + D_POOL + c1]
        put(D_RNN + c0, POOL_GROUP, og * pscale_ref[:, c0:c1] * _silu(gp))

    qoff = 2 * D_RNN + 2 * D_POOL
    for hd in range(N_XHEADS):
        c0, c1 = hd * XHEAD_DIM, (hd + 1) * XHEAD_DIM
        q = z_ref[:, qoff + c0:qoff + c1].astype(BF16)
        s = lax.dot_general(q, kb_ref[:, c0:c1], (((1,), (1,)), ((), ())),
                            preferred_element_type=F32) * (XHEAD_DIM ** -0.5)
        p = jnp.exp(s - jnp.max(s, axis=-1, keepdims=True))
        p = p / jnp.sum(p, axis=-1, keepdims=True)
        ox = jnp.dot(p.astype(BF16), vb_ref[:, c0:c1], preferred_element_type=F32)
        gx = z_ref[:, qoff + D_X + c0:qoff + D_X + c1]
        put(D_RNN + D_POOL + c0, XHEAD_DIM, ox * _silu(gx))

    o_ref[...] = jnp.dot(unperm_ref[...], op_scr[...], preferred_element_type=F32).astype(BF16)

    @pl.when(l == last)
    def _():
        newh_ref[0] = h_carry[...]
        tail_row = lambda j: (nrow - j) * SUBLANES + SUBLANES - 1
        for j in range(1, CONV_W):
            newconv_ref[0, CONV_W - 1 - j:CONV_W - j, :] = z_ref[tail_row(j):tail_row(j) + 1, 0:D_RNN]
        for j in range(1, POOL_HIST + 1):
            newpool_ref[0, POOL_HIST - j:POOL_HIST - j + 1, :] = (
                z_ref[tail_row(j):tail_row(j) + 1, pcol:pcol + D_POOL])


def _prompt_mix(z, mem_k, mem_v, conv_w, conv_b, wa, wx, b_a, b_x, lam, wpool, pscale, unperm,
                sample_q, cache_k, cache_v, batch, seq, tm):
    nl = seq // tm
    assert sample_q.shape[0] == batch * nl
    side = lambda b, l: (b * nl + l, 0, 0)
    zw = 2 * D_MIX
    const2 = lambda b, l: (0, 0)
    const3 = lambda b, l: (0, 0, 0)
    kern = functools.partial(_prompt_mix_kernel, tm=tm)
    return pl.pallas_call(
        kern,
        grid=(batch, nl),
        in_specs=[
            pl.BlockSpec((tm, zw), lambda b, l: (b * nl + l, 0)),
            pl.BlockSpec((1, N_MEM, D_X), lambda b, l: (b, 0, 0)),
            pl.BlockSpec((1, N_MEM, D_X), lambda b, l: (b, 0, 0)),
            pl.BlockSpec((CONV_W, D_RNN), const2),
            pl.BlockSpec((1, D_RNN), const2),
            pl.BlockSpec((N_RNN_BLOCKS, RNN_BLOCK, RNN_BLOCK), const3),
            pl.BlockSpec((N_RNN_BLOCKS, RNN_BLOCK, RNN_BLOCK), const3),
            pl.BlockSpec((1, D_RNN), const2),
            pl.BlockSpec((1, D_RNN), const2),
            pl.BlockSpec((1, D_RNN), const2),
            pl.BlockSpec((len(POOL_WINDOWS), POOL_GROUP, POOL_GROUP), const3),
            pl.BlockSpec((1, D_POOL), const2),
            pl.BlockSpec((tm, tm), const2),
            pl.BlockSpec((1, ATTN_BB, D_X), side),
            pl.BlockSpec(memory_space=pl.ANY),
            pl.BlockSpec(memory_space=pl.ANY),
        ],
        out_specs=[
            pl.BlockSpec((tm, D_MIX), lambda b, l: (b * nl + l, 0)),
            pl.BlockSpec((1, 1, D_RNN), lambda b, l: (b, 0, 0)),
            pl.BlockSpec((1, CONV_W - 1, D_RNN), lambda b, l: (b, 0, 0)),
            pl.BlockSpec((1, POOL_HIST, D_POOL), lambda b, l: (b, 0, 0)),
            pl.BlockSpec((1, ATTN_BB, D_X), side),
        ],
        out_shape=[
            jax.ShapeDtypeStruct((batch * seq, D_MIX), BF16),
            jax.ShapeDtypeStruct((batch, 1, D_RNN), F32),
            jax.ShapeDtypeStruct((batch, CONV_W - 1, D_RNN), F32),
            jax.ShapeDtypeStruct((batch, POOL_HIST, D_POOL), F32),
            jax.ShapeDtypeStruct(sample_q.shape, F32),
        ],
        scratch_shapes=[
            pltpu.VMEM((CONV_W - 1, SUBLANES, D_RNN), F32),
            pltpu.VMEM((POOL_HIST, SUBLANES, D_POOL), F32),
            pltpu.VMEM((1, D_RNN), F32),
            pltpu.VMEM((N_MEM, D_X), BF16),
            pltpu.VMEM((N_MEM, D_X), BF16),
            pltpu.VMEM((tm, D_RNN), F32),
            pltpu.VMEM((tm, D_RNN), F32),
            pltpu.VMEM((tm, D_MIX), BF16),
            pltpu.VMEM((CACHE_RING, ATTN_BB, N_MEM * SUBLANES, LANES), F32),
            pltpu.VMEM((CACHE_RING, ATTN_BB, N_MEM * SUBLANES, LANES), F32),
            pltpu.SemaphoreType.DMA((CACHE_RING, 2)),
        ],
        compiler_params=pltpu.CompilerParams(
            dimension_semantics=("arbitrary", "arbitrary"),
            vmem_limit_bytes=VMEM_LIMIT),
        name="prompt_mix",
    )(z, mem_k, mem_v, conv_w, conv_b, wa, wx, b_a, b_x, lam, wpool, pscale, unperm,
      sample_q, cache_k, cache_v)


def _cache_rows(c):
    nb = c.shape[0]
    c = c.reshape(nb, N_MEM, N_XHEADS, XHEAD_DIM // LANES, LANES)
    return c.transpose(0, 1, 3, 2, 4).reshape(nb, N_MEM * SUBLANES, LANES)


def _sample_attn_scores(q_ref, k_ref, bb):
    halves = XHEAD_DIM // LANES
    assert halves * N_XHEADS == SUBLANES
    scores = []
    for j in range(bb):
        qn = jnp.concatenate(
            [q_ref[j:j + 1, (h * halves + t) * LANES:(h * halves + t + 1) * LANES]
             for t in range(halves) for h in range(N_XHEADS)], axis=0)
        scores.append(lax.dot_general(qn.astype(BF16), k_ref[j].astype(BF16),
                                      (((1,), (1,)), ((), ())), preferred_element_type=F32)
                      * (XHEAD_DIM ** -0.5))
    return scores


def _sample_attn_probs(scores):
    r = lax.broadcasted_iota(jnp.int32, (SUBLANES, LANES), 0)
    c = lax.broadcasted_iota(jnp.int32, (SUBLANES, LANES), 1)
    diag = (c % SUBLANES) == r
    first_half = r < N_XHEADS
    nchunk = N_MEM * SUBLANES // LANES
    probs = []
    for s in scores:
        chunks = []
        for ci in range(nchunk):
            sm = jnp.where(diag, s[:, ci * LANES:(ci + 1) * LANES], 0.0)
            other = pltpu.roll(sm, N_XHEADS, 0)
            other = jnp.where(first_half, pltpu.roll(other, LANES - N_XHEADS, 1),
                              pltpu.roll(other, N_XHEADS, 1))
            chunks.append(jnp.where(diag, sm + other, -jnp.inf))
        t_full = jnp.concatenate(chunks, axis=1)
        e = jnp.exp(t_full - jnp.max(t_full, axis=1, keepdims=True))
        probs.append((e / jnp.sum(e, axis=1, keepdims=True)).astype(BF16))
    return probs


def _sample_attn_values(probs, v_ref, o_ref):
    halves = XHEAD_DIM // LANES
    for j, p in enumerate(probs):
        o = jnp.dot(p, v_ref[j].astype(BF16), preferred_element_type=F32)
        for t in range(halves):
            for h in range(N_XHEADS):
                col = (h * halves + t) * LANES
                o_ref[j:j + 1, col:col + LANES] = o[t * N_XHEADS + h:t * N_XHEADS + h + 1, :]


def _prompt_proj_kernel(x_ref, g_ref, w_ref, perm_ref, o_ref, u_ref, up_ref, *, mix_steps):
    j = pl.program_id(1)

    @pl.when(j == 0)
    def _():
        x = x_ref[...]
        u = (x * _rms_scale(x) * g_ref[...]).astype(BF16)
        u_ref[...] = u
        for r0 in range(0, u.shape[0], MIX_TM):
            up_ref[r0:r0 + MIX_TM, :] = jnp.dot(
                perm_ref[...], u[r0:r0 + MIX_TM], preferred_element_type=F32).astype(BF16)

    @pl.when(j < mix_steps)
    def _():
        o_ref[...] = jnp.dot(up_ref[...], w_ref[...], preferred_element_type=F32)

    @pl.when(j >= mix_steps)
    def _():
        o_ref[...] = jnp.dot(u_ref[...], w_ref[...], preferred_element_type=F32)


def _prompt_proj(x, g, w, perm, tm, tn):
    m, k = x.shape
    n = w.shape[1]
    assert (2 * D_MIX) % tn == 0
    return pl.pallas_call(
        functools.partial(_prompt_proj_kernel, mix_steps=2 * D_MIX // tn),
        grid=(m // tm, n // tn),
        in_specs=[
            pl.BlockSpec((tm, k), lambda i, j: (i, 0)),
            pl.BlockSpec((1, k), lambda i, j: (0, 0)),
            pl.BlockSpec((k, tn), lambda i, j: (0, j)),
            pl.BlockSpec(perm.shape, lambda i, j: (0, 0)),
        ],
        out_specs=pl.BlockSpec((tm, tn), lambda i, j: (i, j)),
        out_shape=jax.ShapeDtypeStruct((m, n), F32),
        scratch_shapes=[pltpu.VMEM((tm, k), BF16), pltpu.VMEM((tm, k), BF16)],
        compiler_params=pltpu.CompilerParams(
            dimension_semantics=("arbitrary", "arbitrary"),
            vmem_limit_bytes=BIG_VMEM_LIMIT),
        name="prompt_proj",
    )(x, g, w, perm)


def _sample_mix_kernel(z_ref, attn_ref, conv_ref, h_ref, pool_ref,
                       convw_ref, convb_ref, wa_ref, wx_ref, ba_ref, bx_ref, lam_ref, wpool_ref,
                       pscale_ref, o_ref, newh_ref, newconv_ref, newpool_ref):
    xr = z_ref[:, 0:D_RNN]
    xc = convb_ref[...] + convw_ref[CONV_W - 1:CONV_W, :] * xr
    for k in range(CONV_W - 1):
        xc = xc + convw_ref[k:k + 1, :] * conv_ref[k]
    for k in range(CONV_W - 2):
        newconv_ref[k] = conv_ref[k + 1]
    newconv_ref[CONV_W - 2] = xr

    rate = _decay_rate(lam_ref[...])
    for n in range(N_RNN_BLOCKS):
        c0, c1 = n * RNN_BLOCK, (n + 1) * RNN_BLOCK
        a, b = _rglru_block(xc[:, c0:c1], wa_ref[n], wx_ref[n], ba_ref[:, c0:c1], bx_ref[:, c0:c1],
                            rate[:, c0:c1])
        h = a * h_ref[:, c0:c1] + b
        newh_ref[:, c0:c1] = h
        o_ref[:, c0:c1] = (h * _silu(z_ref[:, D_RNN + c0:D_RNN + c1])).astype(BF16)

    xp = z_ref[:, 2 * D_RNN:2 * D_RNN + D_POOL]
    for k in range(POOL_HIST - 1):
        newpool_ref[k] = pool_ref[k + 1]
    newpool_ref[POOL_HIST - 1] = xp
    for g, w in enumerate(POOL_WINDOWS):
        c0, c1 = g * POOL_GROUP, (g + 1) * POOL_GROUP
        xg = xp[:, c0:c1]
        tot = xg
        for j in range(1, w):
            tot = tot + pool_ref[POOL_HIST - j, :, c0:c1]
        cnt = float(min(PAST_LEN + 1, w))
        d = tot / cnt - xg
        og = jnp.dot(d.astype(BF16), wpool_ref[g].astype(BF16), preferred_element_type=F32)
        gp = z_ref[:, 2 * D_RNN + D_POOL + c0:2 * D_RNN + D_POOL + c1]
        o_ref[:, D_RNN + c0:D_RNN + c1] = (og * pscale_ref[:, c0:c1] * _silu(gp)).astype(BF16)

    gx = z_ref[:, 2 * D_RNN + 2 * D_POOL + D_X:2 * D_MIX]
    o_ref[:, D_RNN + D_POOL:] = (attn_ref[...] * _silu(gx)).astype(BF16)


def _sample_mix(z, attn, conv, h, pool, conv_w, conv_b, wa, wx, b_a, b_x, lam, wpool, pscale, tb):
    nb = z.shape[0]
    zw = 2 * D_MIX
    rows = lambda i: (i, 0)
    const2 = lambda i: (0, 0)
    const3 = lambda i: (0, 0, 0)
    hist = lambda i: (0, i, 0)
    return pl.pallas_call(
        _sample_mix_kernel,
        grid=(nb // tb,),
        in_specs=[
            pl.BlockSpec((tb, zw), rows),
            pl.BlockSpec((tb, D_X), rows),
            pl.BlockSpec((CONV_W - 1, tb, D_RNN), hist),
            pl.BlockSpec((tb, D_RNN), rows),
            pl.BlockSpec((POOL_HIST, tb, D_POOL), hist),
            pl.BlockSpec((CONV_W, D_RNN), const2),
            pl.BlockSpec((1, D_RNN), const2),
            pl.BlockSpec((N_RNN_BLOCKS, RNN_BLOCK, RNN_BLOCK), const3),
            pl.BlockSpec((N_RNN_BLOCKS, RNN_BLOCK, RNN_BLOCK), const3),
            pl.BlockSpec((1, D_RNN), const2),
            pl.BlockSpec((1, D_RNN), const2),
            pl.BlockSpec((1, D_RNN), const2),
            pl.BlockSpec((len(POOL_WINDOWS), POOL_GROUP, POOL_GROUP), const3),
            pl.BlockSpec((1, D_POOL), const2),
        ],
        out_specs=[
            pl.BlockSpec((tb, D_MIX), rows),
            pl.BlockSpec((tb, D_RNN), rows),
            pl.BlockSpec((CONV_W - 1, tb, D_RNN), hist),
            pl.BlockSpec((POOL_HIST, tb, D_POOL), hist),
        ],
        out_shape=[
            jax.ShapeDtypeStruct((nb, D_MIX), BF16),
            jax.ShapeDtypeStruct((nb, D_RNN), F32),
            jax.ShapeDtypeStruct((CONV_W - 1, nb, D_RNN), F32),
            jax.ShapeDtypeStruct((POOL_HIST, nb, D_POOL), F32),
        ],
        compiler_params=pltpu.CompilerParams(
            dimension_semantics=("arbitrary",),
            vmem_limit_bytes=VMEM_LIMIT),
        name="sample_mix",
    )(z, attn, conv, h, pool, conv_w, conv_b, wa, wx, b_a, b_x, lam, wpool, pscale)


def _branch_out_kernel(o_ref, gates_ref, x_ref, wb_ref, wo_ref, gpost_ref, y_ref):
    merged = None
    for j, (r0, r1) in enumerate(((0, D_RNN), (D_RNN, D_RNN + D_POOL), (D_RNN + D_POOL, D_MIX))):
        yj = jnp.dot(o_ref[:, r0:r1], wb_ref[r0:r1, :], preferred_element_type=F32)
        term = _sigmoid(gates_ref[:, j * D_MODEL:(j + 1) * D_MODEL]) * yj
        merged = term if merged is None else merged + term
    out = jnp.dot(merged.astype(BF16), wo_ref[...], preferred_element_type=F32)
    y_ref[...] = x_ref[...] + (out * gpost_ref[...]) * _rms_scale(out)


def _branch_out(o, z, x, wb, wo, g_post, tm):
    m = x.shape[0]
    gw = N_BRANCH * D_MODEL
    gblk = (2 * D_MIX) // gw
    resident = pl.Buffered(1)
    return pl.pallas_call(
        _branch_out_kernel,
        grid=(m // tm,),
        in_specs=[
            pl.BlockSpec((tm, D_MIX), lambda i: (i, 0)),
            pl.BlockSpec((tm, gw), lambda i: (i, gblk)),
            pl.BlockSpec((tm, D_MODEL), lambda i: (i, 0)),
            pl.BlockSpec((D_MIX, D_MODEL), lambda i: (0, 0), pipeline_mode=resident),
            pl.BlockSpec((D_MODEL, D_MODEL), lambda i: (0, 0), pipeline_mode=resident),
            pl.BlockSpec((1, D_MODEL), lambda i: (0, 0)),
        ],
        out_specs=pl.BlockSpec((tm, D_MODEL), lambda i: (i, 0)),
        out_shape=jax.ShapeDtypeStruct((m, D_MODEL), F32),
        compiler_params=pltpu.CompilerParams(
            dimension_semantics=("arbitrary",),
            vmem_limit_bytes=VMEM_LIMIT),
        name="branch_out",
    )(o, z, x, wb, wo, g_post)


WROWS = 1024
PER_BRANCH = D_RNN // WROWS
assert D_RNN == D_POOL == D_X and D_RNN % WROWS == 0 and D_MODEL % WROWS == 0
N_WB_BLOCKS = N_BRANCH * PER_BRANCH
N_WOUT_BLOCKS = D_MODEL // WROWS


def _branch_out_cast_kernel(o_ref, gates_ref, x_ref, wb_hbm, wo_hbm, gpost_ref,
                            y_ref, wbb_ref, wob_ref, merged_ref, out_ref, w_ring, ring_sem):
    s = pl.program_id(0)
    nsteps = N_WB_BLOCKS + N_WOUT_BLOCKS

    def w_copy(blk):
        src, first = (wb_hbm, 0) if blk < N_WB_BLOCKS else (wo_hbm, N_WB_BLOCKS)
        slot = blk % BOC_RING
        return pltpu.make_async_copy(src.at[pl.ds((blk - first) * WROWS, WROWS)],
                                     w_ring.at[slot], ring_sem.at[slot])

    for t in range(nsteps):
        @pl.when(s == t)
        def _(t=t):
            if t == 0:
                for first in range(min(BOC_RING - 1, nsteps)):
                    w_copy(first).start()
            if t + BOC_RING - 1 < nsteps:
                w_copy(t + BOC_RING - 1).start()
            w_copy(t).wait()
            w = w_ring[t % BOC_RING].astype(BF16)
            if t < N_WB_BLOCKS:
                wbb_ref[...] = w
                term = _sigmoid(gates_ref[...]) * jnp.dot(o_ref[...], w,
                                                          preferred_element_type=F32)
                if t == 0:
                    merged_ref[...] = term
                else:
                    merged_ref[...] += term
            else:
                kb = t - N_WB_BLOCKS
                wob_ref[...] = w
                part = jnp.dot(merged_ref[:, kb * WROWS:(kb + 1) * WROWS].astype(BF16), w,
                               preferred_element_type=F32)
                if kb == 0:
                    out_ref[...] = part
                else:
                    out_ref[...] += part
            if t == nsteps - 1:
                out = out_ref[...]
                y_ref[...] = x_ref[...] + out * _rms_scale(out) * gpost_ref[...]


def _branch_out_cast(o, z, x, wb, wo, g_post):
    m = x.shape[0]
    gblk0 = (2 * D_MIX) // D_MODEL
    wb_blk = lambda s: jnp.minimum(s, N_WB_BLOCKS - 1)
    wo_blk = lambda s: jnp.maximum(s - N_WB_BLOCKS, 0)
    return pl.pallas_call(
        _branch_out_cast_kernel,
        grid=(N_WB_BLOCKS + N_WOUT_BLOCKS,),
        in_specs=[
            pl.BlockSpec((m, WROWS), lambda s: (0, wb_blk(s))),
            pl.BlockSpec((m, D_MODEL), lambda s: (0, gblk0 + wb_blk(s) // PER_BRANCH)),
            pl.BlockSpec((m, D_MODEL), lambda s: (0, 0)),
            pl.BlockSpec(memory_space=pl.ANY),
            pl.BlockSpec(memory_space=pl.ANY),
            pl.BlockSpec((1, D_MODEL), lambda s: (0, 0)),
        ],
        out_specs=[
            pl.BlockSpec((m, D_MODEL), lambda s: (0, 0)),
            pl.BlockSpec((WROWS, D_MODEL), lambda s: (wb_blk(s), 0)),
            pl.BlockSpec((WROWS, D_MODEL), lambda s: (wo_blk(s), 0)),
        ],
        out_shape=[
            jax.ShapeDtypeStruct((m, D_MODEL), F32),
            jax.ShapeDtypeStruct(wb.shape, BF16),
            jax.ShapeDtypeStruct(wo.shape, BF16),
        ],
        scratch_shapes=[pltpu.VMEM((m, D_MODEL), F32), pltpu.VMEM((m, D_MODEL), F32),
                        pltpu.VMEM((BOC_RING, WROWS, D_MODEL), F32),
                        pltpu.SemaphoreType.DMA((BOC_RING,))],
        compiler_params=pltpu.CompilerParams(
            dimension_semantics=("arbitrary",),
            vmem_limit_bytes=BIG_VMEM_LIMIT),
        name="branch_out_cast",
    )(o, z, x, wb, wo, g_post)


def kernel(x_prompt, x_sample, mem_prompt, state_rglru_h, state_conv, state_pool, cache_mem_k, cache_mem_v, g_pre, w_in, conv_w, conv_b, w_rg_a, b_rg_a, w_rg_x, b_rg_x, lru_lambda, w_pool, pool_scale, g_mem, w_kv, w_branch, w_out, g_post):
    batch, seq, _ = x_prompt.shape
    nb = x_sample.shape[0]
    depth = g_pre.shape[0]
    assert depth == 1 and x_sample.shape[1] == 1

    l = 0
    row = lambda v: v.reshape(1, -1)
    mix_params = (conv_w[l], row(conv_b[l]), w_rg_a[l], w_rg_x[l], row(b_rg_a[l]), row(b_rg_x[l]),
                  row(lru_lambda[l]), w_pool[l], row(pool_scale[l]))

    xp2 = x_prompt.reshape(batch * seq, D_MODEL)
    xs2 = x_sample.reshape(nb, D_MODEL)
    mem2 = mem_prompt.reshape(batch * N_MEM, D_MODEL)

    z_s, w_in_b, mem_k, mem_v = _sample_proj(xs2, row(g_pre[l]), w_in[l], mem2, row(g_mem[l]),
                                             w_kv[l], tn=SAMPLE_PROJ_TN)
    qoff = 2 * D_RNN + 2 * D_POOL
    q_s = z_s[:, qoff:qoff + D_X].reshape(nb // ATTN_BB, ATTN_BB, D_X)

    mem_k = mem_k.reshape(batch, N_MEM, D_X)
    mem_v = mem_v.reshape(batch, N_MEM, D_X)

    perm = _chunk_interleave()
    z_p = _prompt_proj(xp2, row(g_pre[l]), w_in_b, perm, tm=PROJ_TM, tn=PROJ_TN)
    o_p, h_p, c_p, p_p, attn_s = _prompt_mix(
        z_p, mem_k, mem_v, *mix_params, perm.T, q_s, _cache_rows(cache_mem_k[l]),
        _cache_rows(cache_mem_v[l]), batch=batch, seq=seq, tm=MIX_TM)
    attn_s = attn_s.reshape(nb, D_X)

    o_s, h_s, c_s, p_s = _sample_mix(
        z_s, attn_s, state_conv[l].transpose(1, 0, 2), state_rglru_h[l],
        state_pool[l].transpose(1, 0, 2), *mix_params, tb=SAMPLE_MIX_TB)
    y_s, w_br_b, w_out_b = _branch_out_cast(o_s, z_s, xs2, w_branch[l], w_out[l], row(g_post[l]))

    y_p = _branch_out(o_p, z_p, xp2, w_br_b, w_out_b, row(g_post[l]), tm=BRANCH_TM)

    return (
        y_p.reshape(batch, seq, D_MODEL),
        y_s.reshape(nb, 1, D_MODEL),
        h_p.reshape(1, batch, D_RNN),
        c_p.reshape(1, batch, CONV_W - 1, D_RNN),
        p_p.reshape(1, batch, POOL_HIST, D_POOL),
        mem_k.reshape(1, batch, N_MEM, N_XHEADS, XHEAD_DIM),
        mem_v.reshape(1, batch, N_MEM, N_XHEADS, XHEAD_DIM),
        h_s.reshape(1, nb, D_RNN),
        c_s.transpose(1, 0, 2)[None],
        p_s.transpose(1, 0, 2)[None],
    )
```

```python
import functools

import jax
import jax.numpy as jnp
from jax import lax
from jax.experimental import pallas as pl
from jax.experimental.pallas import tpu as pltpu

D_MODEL = 2048
PAST_LEN = 16384
D_RNN = 1024
N_RNN_BLOCKS = 8
RNN_BLOCK = D_RNN // N_RNN_BLOCKS
CONV_W = 4
LRU_C = 8.0
D_POOL = 1024
POOL_WINDOWS = (2, 4, 8, 16)
POOL_GROUP = D_POOL // len(POOL_WINDOWS)
POOL_HIST = max(POOL_WINDOWS) - 1
N_MEM = 256
N_XHEADS = 4
XHEAD_DIM = 256
D_X = N_XHEADS * XHEAD_DIM
N_BRANCH = 3
D_MIX = D_RNN + D_POOL + D_X
D_IN = 2 * D_MIX + N_BRANCH * D_MODEL
EPS = 1e-6

SUBLANES = 8
LANES = 128
VMEM_LIMIT = 56 * 1024 * 1024
BIG_VMEM_LIMIT = 60 * 1024 * 1024
MIX_TM = 256
BRANCH_TM = 256
PROJ_TM, PROJ_TN = 1024, 2048
SAMPLE_PROJ_TN = 768
ATTN_BB = 4
CACHE_RING = 3
W_RING = 3
W_DMA_PRIORITY = 1
BOC_RING = 3
SAMPLE_MIX_TB = 64

BF16 = jnp.bfloat16
F32 = jnp.float32

NEG_LOG2_E = -1.4426950408889634


def _sigmoid(x):
    return 1.0 / (1.0 + jnp.exp2(x * NEG_LOG2_E))


def _silu(x):
    return x * _sigmoid(x)


def _softplus(x):
    return jnp.maximum(x, 0.0) + jnp.log1p(jnp.exp(-jnp.abs(x)))


def _rms_scale(x):
    return lax.rsqrt(jnp.mean(x * x, axis=-1, keepdims=True) + EPS)


def _chunk_interleave():
    nrow = MIX_TM // SUBLANES
    p = jnp.arange(MIX_TM)
    token = (p % SUBLANES) * nrow + p // SUBLANES
    return (token[:, None] == jnp.arange(MIX_TM)[None, :]).astype(BF16)


def _sample_proj_kernel(x_ref, g_ref, w_hbm, mem_ref, gm_ref, wkv_ref,
                        o_ref, wb_ref, k_ref, v_ref, u_ref, um_ref, w_ring, ring_sem,
                        *, k_steps, tn):
    j = pl.program_id(0)
    nsteps = pl.num_programs(0)

    def w_copy(blk, slot):
        return pltpu.make_async_copy(w_hbm.at[:, pl.ds(blk * tn, tn)], w_ring.at[slot],
                                     ring_sem.at[slot])

    @pl.when(j == 0)
    def _():
        for first in range(W_RING - 1):
            w_copy(first, first).start(priority=W_DMA_PRIORITY)

    ahead = j + (W_RING - 1)

    @pl.when(ahead < nsteps)
    def _():
        w_copy(ahead, ahead % W_RING).start(priority=W_DMA_PRIORITY)

    @pl.when(j == 0)
    def _():
        x = x_ref[...]
        u_ref[...] = (x * _rms_scale(x) * g_ref[...]).astype(BF16)
        mem = mem_ref[...]
        um_ref[...] = (mem * _rms_scale(mem) * gm_ref[...]).astype(BF16)

    slot = j % W_RING
    w_copy(j, slot).wait()
    w = w_ring[slot].astype(BF16)
    wb_ref[...] = w
    o_ref[...] = jnp.dot(u_ref[...], w, preferred_element_type=F32)

    kv = jnp.dot(um_ref[...], wkv_ref[...].astype(BF16), preferred_element_type=F32)

    @pl.when(j < k_steps)
    def _():
        k_ref[...] = kv

    @pl.when(j >= k_steps)
    def _():
        v_ref[...] = kv


def _sample_proj(x, g, w, mem, g_mem, w_kv, tn):
    m, k = x.shape
    n = w.shape[1]
    steps = n // tn
    mrows = mem.shape[0]
    kv_tn = 2 * D_X // steps
    assert D_X % kv_tn == 0 and kv_tn % LANES == 0
    k_steps = D_X // kv_tn
    return pl.pallas_call(
        functools.partial(_sample_proj_kernel, k_steps=k_steps, tn=tn),
        grid=(steps,),
        in_specs=[
            pl.BlockSpec((m, k), lambda j: (0, 0)),
            pl.BlockSpec((1, k), lambda j: (0, 0)),
            pl.BlockSpec(memory_space=pl.ANY),
            pl.BlockSpec((mrows, k), lambda j: (0, 0), pipeline_mode=pl.Buffered(1)),
            pl.BlockSpec((1, k), lambda j: (0, 0)),
            pl.BlockSpec((k, kv_tn), lambda j: (0, j)),
        ],
        out_specs=[
            pl.BlockSpec((m, tn), lambda j: (0, j)),
            pl.BlockSpec((k, tn), lambda j: (0, j)),
            pl.BlockSpec((mrows, kv_tn), lambda j: (0, jnp.minimum(j, k_steps - 1))),
            pl.BlockSpec((mrows, kv_tn), lambda j: (0, jnp.maximum(j - k_steps, 0))),
        ],
        out_shape=[
            jax.ShapeDtypeStruct((m, n), F32),
            jax.ShapeDtypeStruct((k, n), BF16),
            jax.ShapeDtypeStruct((mrows, D_X), F32),
            jax.ShapeDtypeStruct((mrows, D_X), F32),
        ],
        scratch_shapes=[pltpu.VMEM((m, k), BF16), pltpu.VMEM((mrows, k), BF16),
                        pltpu.VMEM((W_RING, k, tn), F32), pltpu.SemaphoreType.DMA((W_RING,))],
        compiler_params=pltpu.CompilerParams(
            dimension_semantics=("arbitrary",),
            vmem_limit_bytes=VMEM_LIMIT),
        name="sample_proj",
    )(x, g, w, mem, g_mem, w_kv)


def _decay_rate(lam):
    return _softplus(-lam) * (LRU_C * NEG_LOG2_E)


def _rglru_block(xc, wa, wx, ba, bx, rate):
    wax = jnp.concatenate([wa, wx], axis=1).astype(BF16)
    ri = jnp.dot(xc.astype(BF16), wax, preferred_element_type=F32)
    r = _sigmoid(ri[:, :RNN_BLOCK] + ba)
    i = _sigmoid(ri[:, RNN_BLOCK:] + bx)
    a = jnp.exp2(r * rate)
    one_m = 1.0 - a * a
    mult = jnp.where(one_m > 0.0, one_m * lax.rsqrt(one_m), 0.0)
    return a, mult * i * xc


def _prompt_mix_kernel(z_ref, k_ref, v_ref, convw_ref, convb_ref, wa_ref, wx_ref, ba_ref, bx_ref,
                       lam_ref, wpool_ref, pscale_ref, unperm_ref, sq_ref, sk_hbm, sv_hbm,
                       o_ref, newh_ref, newconv_ref, newpool_ref, sattn_ref,
                       conv_carry, pool_carry, h_carry, kb_ref, vb_ref, ac_scr, hl_scr, op_scr,
                       sk_ring, sv_ring, ring_sem, *, tm):
    l = pl.program_id(1)
    last = pl.num_programs(1) - 1
    nrow = tm // SUBLANES

    @pl.when(l == 0)
    def _():
        conv_carry[...] = jnp.zeros(conv_carry.shape, F32)
        pool_carry[...] = jnp.zeros(pool_carry.shape, F32)
        h_carry[...] = jnp.zeros(h_carry.shape, F32)
        kb_ref[...] = k_ref[0].astype(BF16)
        vb_ref[...] = v_ref[0].astype(BF16)

    step = pl.program_id(0) * pl.num_programs(1) + l
    nsteps = pl.num_programs(0) * pl.num_programs(1)

    def cache_copies(blk, slot):
        rows = pl.ds(blk * ATTN_BB, ATTN_BB)
        return (pltpu.make_async_copy(sk_hbm.at[rows], sk_ring.at[slot], ring_sem.at[slot, 0]),
                pltpu.make_async_copy(sv_hbm.at[rows], sv_ring.at[slot], ring_sem.at[slot, 1]))

    @pl.when(step == 0)
    def _():
        for first in range(CACHE_RING - 1):
            for cp in cache_copies(first, first):
                cp.start()

    ahead = step + (CACHE_RING - 1)

    @pl.when(ahead < nsteps)
    def _():
        for cp in cache_copies(ahead, ahead % CACHE_RING):
            cp.start()

    slot = step % CACHE_RING
    for cp in cache_copies(step, slot):
        cp.wait()
    sk_ref = sk_ring.at[slot]
    sv_ref = sv_ring.at[slot]
    side_scores = _sample_attn_scores(sq_ref.at[0], sk_ref, ATTN_BB)

    chunk_id = lax.broadcasted_iota(jnp.int32, (SUBLANES, LANES), 0)
    first_chunk = chunk_id == 0

    def load_groups(col, width=LANES):
        return [z_ref[r * SUBLANES:(r + 1) * SUBLANES, col:col + width] for r in range(nrow)]

    def put(col, width, val):
        op_scr[:, col:col + width] = val.astype(BF16)

    def store_groups(col, rows, width=LANES):
        put(col, width, jnp.concatenate(rows, axis=0))

    def history(tail_group, carry_ref, j, c0):
        tail = pltpu.roll(tail_group, 1, 0)
        prev = jnp.where(first_chunk, carry_ref[j - 1, :, c0:c0 + LANES], tail)
        carry_ref[j - 1, :, c0:c0 + LANES] = tail
        return prev

    rate = _decay_rate(lam_ref[...])
    for n in range(N_RNN_BLOCKS):
        c0, c1 = n * RNN_BLOCK, (n + 1) * RNN_BLOCK
        xs = load_groups(c0)
        ext = [history(xs[nrow - j], conv_carry, j, c0) for j in range(CONV_W - 1, 0, -1)] + xs
        cw = [jnp.broadcast_to(convw_ref[k:k + 1, c0:c1], (SUBLANES, LANES)) for k in range(CONV_W)]
        cb = jnp.broadcast_to(convb_ref[:, c0:c1], (SUBLANES, LANES))
        xc = []
        for r in range(nrow):
            acc = cb + cw[0] * ext[r]
            for k in range(1, CONV_W):
                acc = acc + cw[k] * ext[r + k]
            xc.append(acc)
        a, b = _rglru_block(jnp.concatenate(xc, axis=0), wa_ref[n], wx_ref[n], ba_ref[:, c0:c1],
                            bx_ref[:, c0:c1], rate[:, c0:c1])
        ac_scr[:, c0:c1] = a
        hl_scr[:, c0:c1] = b

    side_probs = _sample_attn_probs(side_scores)

    acc_a = ac_scr[0:SUBLANES, :]
    acc_h = hl_scr[0:SUBLANES, :]
    for r in range(1, nrow):
        rows = slice(r * SUBLANES, (r + 1) * SUBLANES)
        ar = ac_scr[rows, :]
        acc_h = ar * acc_h + hl_scr[rows, :]
        acc_a = ar * acc_a
        ac_scr[rows, :] = acc_a
        hl_scr[rows, :] = acc_h
    h_in = h_carry[...]
    entering = []
    for c in range(SUBLANES):
        entering.append(h_in)
        h_in = acc_a[c:c + 1] * h_in + acc_h[c:c + 1]
    h_carry[...] = h_in
    h_enter = jnp.concatenate(entering, axis=0)
    for n in range(N_RNN_BLOCKS):
        c0, c1 = n * RNN_BLOCK, (n + 1) * RNN_BLOCK
        gr = load_groups(D_RNN + c0)
        store_groups(c0, [(hl_scr[r * SUBLANES:(r + 1) * SUBLANES, c0:c1]
                           + ac_scr[r * SUBLANES:(r + 1) * SUBLANES, c0:c1] * h_enter[:, c0:c1])
                          * _silu(gr[r]) for r in range(nrow)])

    _sample_attn_values(side_probs, sv_ref, sattn_ref.at[0])

    pcol = 2 * D_RNN
    blocks = [(w, c0) for g, w in enumerate(POOL_WINDOWS)
              for c0 in range(g * POOL_GROUP, (g + 1) * POOL_GROUP, LANES)]

    def group(c0, r):
        return z_ref[r * SUBLANES:(r + 1) * SUBLANES, pcol + c0:pcol + c0 + LANES]

    def mean_minus_token(tot, w, c0, r):
        if r < w - 1:
            pos1 = l * tm + chunk_id * nrow + (r + 1)
            mean = tot / jnp.minimum(pos1, w).astype(F32)
        else:
            mean = tot * (1.0 / w)
        return mean - group(c0, r)

    hist, tot = {}, {}
    for w, c0 in blocks:
        hist[c0] = [history(group(c0, nrow - j), pool_carry, j, c0) for j in range(1, w)]
        t = group(c0, 0)
        for h in hist[c0]:
            t = t + h
        tot[c0] = t
        hl_scr[0:SUBLANES, c0:c0 + LANES] = mean_minus_token(t, w, c0, 0)
    for r in range(1, nrow):
        for w, c0 in blocks:
            leaving = group(c0, r - w) if r >= w else hist[c0][w - r - 1]
            tot[c0] = tot[c0] + (group(c0, r) - leaving)
            hl_scr[r * SUBLANES:(r + 1) * SUBLANES, c0:c0 + LANES] = mean_minus_token(
                tot[c0], w, c0, r)
    for g, w in enumerate(POOL_WINDOWS):
        c0, c1 = g * POOL_GROUP, (g + 1) * POOL_GROUP
        og = jnp.dot(hl_scr[:, c0:c1].astype(BF16), wpool_ref[g].astype(BF16),
                     preferred_element_type=F32)
        gp = z_ref[:, pcol + D_POOL + c0:pcol + D_POOL + c1]
        put(D_RNN + c0, POOL_GROUP, og * pscale_ref[:, c0:c1] * _silu(gp))

    qoff = 2 * D_RNN + 2 * D_POOL
    for hd in range(N_XHEADS):
        c0, c1 = hd * XHEAD_DIM, (hd + 1) * XHEAD_DIM
        q = z_ref[:, qoff + c0:qoff + c1].astype(BF16)
        s = lax.dot_general(q, kb_ref[:, c0:c1], (((1,), (1,)), ((), ())),
                            preferred_element_type=F32) * (XHEAD_DIM ** -0.5)
        p = jnp.exp(s - jnp.max(s, axis=-1, keepdims=True))
        p = p / jnp.sum(p, axis=-1, keepdims=True)
        ox = jnp.dot(p.astype(BF16), vb_ref[:, c0:c1], preferred_element_type=F32)
        gx = z_ref[:, qoff + D_X + c0:qoff + D_X + c1]
        put(D_RNN + D_POOL + c0, XHEAD_DIM, ox * _silu(gx))

    o_ref[...] = jnp.dot(unperm_ref[...], op_scr[...], preferred_element_type=F32).astype(BF16)

    @pl.when(l == last)
    def _():
        newh_ref[0] = h_carry[...]
        tail_row = lambda j: (nrow - j) * SUBLANES + SUBLANES - 1
        for j in range(1, CONV_W):
            newconv_ref[0, CONV_W - 1 - j:CONV_W - j, :] = z_ref[tail_row(j):tail_row(j) + 1, 0:D_RNN]
        for j in range(1, POOL_HIST + 1):
            newpool_ref[0, POOL_HIST - j:POOL_HIST - j + 1, :] = (
                z_ref[tail_row(j):tail_row(j) + 1, pcol:pcol + D_POOL])


def _prompt_mix(z, mem_k, mem_v, conv_w, conv_b, wa, wx, b_a, b_x, lam, wpool, pscale, unperm,
                sample_q, cache_k, cache_v, batch, seq, tm):
    nl = seq // tm
    assert sample_q.shape[0] == batch * nl
    side = lambda b, l: (b * nl + l, 0, 0)
    zw = 2 * D_MIX
    const2 = lambda b, l: (0, 0)
    const3 = lambda b, l: (0, 0, 0)
    kern = functools.partial(_prompt_mix_kernel, tm=tm)
    return pl.pallas_call(
        kern,
        grid=(batch, nl),
        in_specs=[
            pl.BlockSpec((tm, zw), lambda b, l: (b * nl + l, 0)),
            pl.BlockSpec((1, N_MEM, D_X), lambda b, l: (b, 0, 0)),
            pl.BlockSpec((1, N_MEM, D_X), lambda b, l: (b, 0, 0)),
            pl.BlockSpec((CONV_W, D_RNN), const2),
            pl.BlockSpec((1, D_RNN), const2),
            pl.BlockSpec((N_RNN_BLOCKS, RNN_BLOCK, RNN_BLOCK), const3),
            pl.BlockSpec((N_RNN_BLOCKS, RNN_BLOCK, RNN_BLOCK), const3),
            pl.BlockSpec((1, D_RNN), const2),
            pl.BlockSpec((1, D_RNN), const2),
            pl.BlockSpec((1, D_RNN), const2),
            pl.BlockSpec((len(POOL_WINDOWS), POOL_GROUP, POOL_GROUP), const3),
            pl.BlockSpec((1, D_POOL), const2),
            pl.BlockSpec((tm, tm), const2),
            pl.BlockSpec((1, ATTN_BB, D_X), side),
            pl.BlockSpec(memory_space=pl.ANY),
            pl.BlockSpec(memory_space=pl.ANY),
        ],
        out_specs=[
            pl.BlockSpec((tm, D_MIX), lambda b, l: (b * nl + l, 0)),
            pl.BlockSpec((1, 1, D_RNN), lambda b, l: (b, 0, 0)),
            pl.BlockSpec((1, CONV_W - 1, D_RNN), lambda b, l: (b, 0, 0)),
            pl.BlockSpec((1, POOL_HIST, D_POOL), lambda b, l: (b, 0, 0)),
            pl.BlockSpec((1, ATTN_BB, D_X), side),
        ],
        out_shape=[
            jax.ShapeDtypeStruct((batch * seq, D_MIX), BF16),
            jax.ShapeDtypeStruct((batch, 1, D_RNN), F32),
            jax.ShapeDtypeStruct((batch, CONV_W - 1, D_RNN), F32),
            jax.ShapeDtypeStruct((batch, POOL_HIST, D_POOL), F32),
            jax.ShapeDtypeStruct(sample_q.shape, F32),
        ],
        scratch_shapes=[
            pltpu.VMEM((CONV_W - 1, SUBLANES, D_RNN), F32),
            pltpu.VMEM((POOL_HIST, SUBLANES, D_POOL), F32),
            pltpu.VMEM((1, D_RNN), F32),
            pltpu.VMEM((N_MEM, D_X), BF16),
            pltpu.VMEM((N_MEM, D_X), BF16),
            pltpu.VMEM((tm, D_RNN), F32),
            pltpu.VMEM((tm, D_RNN), F32),
            pltpu.VMEM((tm, D_MIX), BF16),
            pltpu.VMEM((CACHE_RING, ATTN_BB, N_MEM * SUBLANES, LANES), F32),
            pltpu.VMEM((CACHE_RING, ATTN_BB, N_MEM * SUBLANES, LANES), F32),
            pltpu.SemaphoreType.DMA((CACHE_RING, 2)),
        ],
        compiler_params=pltpu.CompilerParams(
            dimension_semantics=("arbitrary", "arbitrary"),
            vmem_limit_bytes=VMEM_LIMIT),
        name="prompt_mix",
    )(z, mem_k, mem_v, conv_w, conv_b, wa, wx, b_a, b_x, lam, wpool, pscale, unperm,
      sample_q, cache_k, cache_v)


def _cache_rows(c):
    nb = c.shape[0]
    c = c.reshape(nb, N_MEM, N_XHEADS, XHEAD_DIM // LANES, LANES)
    return c.transpose(0, 1, 3, 2, 4).reshape(nb, N_MEM * SUBLANES, LANES)


def _sample_attn_scores(q_ref, k_ref, bb):
    halves = XHEAD_DIM // LANES
    assert halves * N_XHEADS == SUBLANES
    scores = []
    for j in range(bb):
        qn = jnp.concatenate(
            [q_ref[j:j + 1, (h * halves + t) * LANES:(h * halves + t + 1) * LANES]
             for t in range(halves) for h in range(N_XHEADS)], axis=0)
        scores.append(lax.dot_general(qn.astype(BF16), k_ref[j].astype(BF16),
                                      (((1,), (1,)), ((), ())), preferred_element_type=F32)
                      * (XHEAD_DIM ** -0.5))
    return scores


def _sample_attn_probs(scores):
    r = lax.broadcasted_iota(jnp.int32, (SUBLANES, LANES), 0)
    c = lax.broadcasted_iota(jnp.int32, (SUBLANES, LANES), 1)
    diag = (c % SUBLANES) == r
    first_half = r < N_XHEADS
    nchunk = N_MEM * SUBLANES // LANES
    probs = []
    for s in scores:
        chunks = []
        for ci in range(nchunk):
            sm = jnp.where(diag, s[:, ci * LANES:(ci + 1) * LANES], 0.0)
            other = pltpu.roll(sm, N_XHEADS, 0)
            other = jnp.where(first_half, pltpu.roll(other, LANES - N_XHEADS, 1),
                              pltpu.roll(other, N_XHEADS, 1))
            chunks.append(jnp.where(diag, sm + other, -jnp.inf))
        t_full = jnp.concatenate(chunks, axis=1)
        e = jnp.exp(t_full - jnp.max(t_full, axis=1, keepdims=True))
        probs.append((e / jnp.sum(e, axis=1, keepdims=True)).astype(BF16))
    return probs


def _sample_attn_values(probs, v_ref, o_ref):
    halves = XHEAD_DIM // LANES
    for j, p in enumerate(probs):
        o = jnp.dot(p, v_ref[j].astype(BF16), preferred_element_type=F32)
        for t in range(halves):
            for h in range(N_XHEADS):
                col = (h * halves + t) * LANES
                o_ref[j:j + 1, col:col + LANES] = o[t * N_XHEADS + h:t * N_XHEADS + h + 1, :]


def _prompt_proj_kernel(x_ref, g_ref, w_ref, perm_ref, o_ref, u_ref, up_ref, *, mix_steps):
    j = pl.program_id(1)

    @pl.when(j == 0)
    def _():
        x = x_ref[...]
        u = (x * _rms_scale(x) * g_ref[...]).astype(BF16)
        u_ref[...] = u
        for r0 in range(0, u.shape[0], MIX_TM):
            up_ref[r0:r0 + MIX_TM, :] = jnp.dot(
                perm_ref[...], u[r0:r0 + MIX_TM], preferred_element_type=F32).astype(BF16)

    @pl.when(j < mix_steps)
    def _():
        o_ref[...] = jnp.dot(up_ref[...], w_ref[...], preferred_element_type=F32)

    @pl.when(j >= mix_steps)
    def _():
        o_ref[...] = jnp.dot(u_ref[...], w_ref[...], preferred_element_type=F32)


def _prompt_proj(x, g, w, perm, tm, tn):
    m, k = x.shape
    n = w.shape[1]
    assert (2 * D_MIX) % tn == 0
    return pl.pallas_call(
        functools.partial(_prompt_proj_kernel, mix_steps=2 * D_MIX // tn),
        grid=(m // tm, n // tn),
        in_specs=[
            pl.BlockSpec((tm, k), lambda i, j: (i, 0)),
            pl.BlockSpec((1, k), lambda i, j: (0, 0)),
            pl.BlockSpec((k, tn), lambda i, j: (0, j)),
            pl.BlockSpec(perm.shape, lambda i, j: (0, 0)),
        ],
        out_specs=pl.BlockSpec((tm, tn), lambda i, j: (i, j)),
        out_shape=jax.ShapeDtypeStruct((m, n), F32),
        scratch_shapes=[pltpu.VMEM((tm, k), BF16), pltpu.VMEM((tm, k), BF16)],
        compiler_params=pltpu.CompilerParams(
            dimension_semantics=("arbitrary", "arbitrary"),
            vmem_limit_bytes=BIG_VMEM_LIMIT),
        name="prompt_proj",
    )(x, g, w, perm)


def _sample_mix_kernel(z_ref, attn_ref, conv_ref, h_ref, pool_ref,
                       convw_ref, convb_ref, wa_ref, wx_ref, ba_ref, bx_ref, lam_ref, wpool_ref,
                       pscale_ref, o_ref, newh_ref, newconv_ref, newpool_ref):
    xr = z_ref[:, 0:D_RNN]
    xc = convb_ref[...] + convw_ref[CONV_W - 1:CONV_W, :] * xr
    for k in range(CONV_W - 1):
        xc = xc + convw_ref[k:k + 1, :] * conv_ref[k]
    for k in range(CONV_W - 2):
        newconv_ref[k] = conv_ref[k + 1]
    newconv_ref[CONV_W - 2] = xr

    rate = _decay_rate(lam_ref[...])
    for n in range(N_RNN_BLOCKS):
        c0, c1 = n * RNN_BLOCK, (n + 1) * RNN_BLOCK
        a, b = _rglru_block(xc[:, c0:c1], wa_ref[n], wx_ref[n], ba_ref[:, c0:c1], bx_ref[:, c0:c1],
                            rate[:, c0:c1])
        h = a * h_ref[:, c0:c1] + b
        newh_ref[:, c0:c1] = h
        o_ref[:, c0:c1] = (h * _silu(z_ref[:, D_RNN + c0:D_RNN + c1])).astype(BF16)

    xp = z_ref[:, 2 * D_RNN:2 * D_RNN + D_POOL]
    for k in range(POOL_HIST - 1):
        newpool_ref[k] = pool_ref[k + 1]
    newpool_ref[POOL_HIST - 1] = xp
    for g, w in enumerate(POOL_WINDOWS):
        c0, c1 = g * POOL_GROUP, (g + 1) * POOL_GROUP
        xg = xp[:, c0:c1]
        tot = xg
        for j in range(1, w):
            tot = tot + pool_ref[POOL_HIST - j, :, c0:c1]
        cnt = float(min(PAST_LEN + 1, w))
        d = tot / cnt - xg
        og = jnp.dot(d.astype(BF16), wpool_ref[g].astype(BF16), preferred_element_type=F32)
        gp = z_ref[:, 2 * D_RNN + D_POOL + c0:2 * D_RNN + D_POOL + c1]
        o_ref[:, D_RNN + c0:D_RNN + c1] = (og * pscale_ref[:, c0:c1] * _silu(gp)).astype(BF16)

    gx = z_ref[:, 2 * D_RNN + 2 * D_POOL + D_X:2 * D_MIX]
    o_ref[:, D_RNN + D_POOL:] = (attn_ref[...] * _silu(gx)).astype(BF16)


def _sample_mix(z, attn, conv, h, pool, conv_w, conv_b, wa, wx, b_a, b_x, lam, wpool, pscale, tb):
    nb = z.shape[0]
    zw = 2 * D_MIX
    rows = lambda i: (i, 0)
    const2 = lambda i: (0, 0)
    const3 = lambda i: (0, 0, 0)
    hist = lambda i: (0, i, 0)
    return pl.pallas_call(
        _sample_mix_kernel,
        grid=(nb // tb,),
        in_specs=[
            pl.BlockSpec((tb, zw), rows),
            pl.BlockSpec((tb, D_X), rows),
            pl.BlockSpec((CONV_W - 1, tb, D_RNN), hist),
            pl.BlockSpec((tb, D_RNN), rows),
            pl.BlockSpec((POOL_HIST, tb, D_POOL), hist),
            pl.BlockSpec((CONV_W, D_RNN), const2),
            pl.BlockSpec((1, D_RNN), const2),
            pl.BlockSpec((N_RNN_BLOCKS, RNN_BLOCK, RNN_BLOCK), const3),
            pl.BlockSpec((N_RNN_BLOCKS, RNN_BLOCK, RNN_BLOCK), const3),
            pl.BlockSpec((1, D_RNN), const2),
            pl.BlockSpec((1, D_RNN), const2),
            pl.BlockSpec((1, D_RNN), const2),
            pl.BlockSpec((len(POOL_WINDOWS), POOL_GROUP, POOL_GROUP), const3),
            pl.BlockSpec((1, D_POOL), const2),
        ],
        out_specs=[
            pl.BlockSpec((tb, D_MIX), rows),
            pl.BlockSpec((tb, D_RNN), rows),
            pl.BlockSpec((CONV_W - 1, tb, D_RNN), hist),
            pl.BlockSpec((POOL_HIST, tb, D_POOL), hist),
        ],
        out_shape=[
            jax.ShapeDtypeStruct((nb, D_MIX), BF16),
            jax.ShapeDtypeStruct((nb, D_RNN), F32),
            jax.ShapeDtypeStruct((CONV_W - 1, nb, D_RNN), F32),
            jax.ShapeDtypeStruct((POOL_HIST, nb, D_POOL), F32),
        ],
        compiler_params=pltpu.CompilerParams(
            dimension_semantics=("arbitrary",),
            vmem_limit_bytes=VMEM_LIMIT),
        name="sample_mix",
    )(z, attn, conv, h, pool, conv_w, conv_b, wa, wx, b_a, b_x, lam, wpool, pscale)


def _branch_out_kernel(o_ref, gates_ref, x_ref, wb_ref, wo_ref, gpost_ref, y_ref):
    merged = None
    for j, (r0, r1) in enumerate(((0, D_RNN), (D_RNN, D_RNN + D_POOL), (D_RNN + D_POOL, D_MIX))):
        yj = jnp.dot(o_ref[:, r0:r1], wb_ref[r0:r1, :], preferred_element_type=F32)
        term = _sigmoid(gates_ref[:, j * D_MODEL:(j + 1) * D_MODEL]) * yj
        merged = term if merged is None else merged + term
    out = jnp.dot(merged.astype(BF16), wo_ref[...], preferred_element_type=F32)
    y_ref[...] = x_ref[...] + (out * gpost_ref[...]) * _rms_scale(out)


def _branch_out(o, z, x, wb, wo, g_post, tm):
    m = x.shape[0]
    gw = N_BRANCH * D_MODEL
    gblk = (2 * D_MIX) // gw
    resident = pl.Buffered(1)
    return pl.pallas_call(
        _branch_out_kernel,
        grid=(m // tm,),
        in_specs=[
            pl.BlockSpec((tm, D_MIX), lambda i: (i, 0)),
            pl.BlockSpec((tm, gw), lambda i: (i, gblk)),
            pl.BlockSpec((tm, D_MODEL), lambda i: (i, 0)),
            pl.BlockSpec((D_MIX, D_MODEL), lambda i: (0, 0), pipeline_mode=resident),
            pl.BlockSpec((D_MODEL, D_MODEL), lambda i: (0, 0), pipeline_mode=resident),
            pl.BlockSpec((1, D_MODEL), lambda i: (0, 0)),
        ],
        out_specs=pl.BlockSpec((tm, D_MODEL), lambda i: (i, 0)),
        out_shape=jax.ShapeDtypeStruct((m, D_MODEL), F32),
        compiler_params=pltpu.CompilerParams(
            dimension_semantics=("arbitrary",),
            vmem_limit_bytes=VMEM_LIMIT),
        name="branch_out",
    )(o, z, x, wb, wo, g_post)


WROWS = 1024
PER_BRANCH = D_RNN // WROWS
assert D_RNN == D_POOL == D_X and D_RNN % WROWS == 0 and D_MODEL % WROWS == 0
N_WB_BLOCKS = N_BRANCH * PER_BRANCH
N_WOUT_BLOCKS = D_MODEL // WROWS


def _branch_out_cast_kernel(o_ref, gates_ref, x_ref, wb_hbm, wo_hbm, gpost_ref,
                            y_ref, wbb_ref, wob_ref, merged_ref, out_ref, w_ring, ring_sem):
    s = pl.program_id(0)
    nsteps = N_WB_BLOCKS + N_WOUT_BLOCKS

    def w_copy(blk):
        src, first = (wb_hbm, 0) if blk < N_WB_BLOCKS else (wo_hbm, N_WB_BLOCKS)
        slot = blk % BOC_RING
        return pltpu.make_async_copy(src.at[pl.ds((blk - first) * WROWS, WROWS)],
                                     w_ring.at[slot], ring_sem.at[slot])

    for t in range(nsteps):
        @pl.when(s == t)
        def _(t=t):
            if t == 0:
                for first in range(min(BOC_RING - 1, nsteps)):
                    w_copy(first).start()
            if t + BOC_RING - 1 < nsteps:
                w_copy(t + BOC_RING - 1).start()
            w_copy(t).wait()
            w = w_ring[t % BOC_RING].astype(BF16)
            if t < N_WB_BLOCKS:
                wbb_ref[...] = w
                term = _sigmoid(gates_ref[...]) * jnp.dot(o_ref[...], w,
                                                          preferred_element_type=F32)
                if t == 0:
                    merged_ref[...] = term
                else:
                    merged_ref[...] += term
            else:
                kb = t - N_WB_BLOCKS
                wob_ref[...] = w
                part = jnp.dot(merged_ref[:, kb * WROWS:(kb + 1) * WROWS].astype(BF16), w,
                               preferred_element_type=F32)
                if kb == 0:
                    out_ref[...] = part
                else:
                    out_ref[...] += part
            if t == nsteps - 1:
                out = out_ref[...]
                y_ref[...] = x_ref[...] + out * _rms_scale(out) * gpost_ref[...]


def _branch_out_cast(o, z, x, wb, wo, g_post):
    m = x.shape[0]
    gblk0 = (2 * D_MIX) // D_MODEL
    wb_blk = lambda s: jnp.minimum(s, N_WB_BLOCKS - 1)
    wo_blk = lambda s: jnp.maximum(s - N_WB_BLOCKS, 0)
    return pl.pallas_call(
        _branch_out_cast_kernel,
        grid=(N_WB_BLOCKS + N_WOUT_BLOCKS,),
        in_specs=[
            pl.BlockSpec((m, WROWS), lambda s: (0, wb_blk(s))),
            pl.BlockSpec((m, D_MODEL), lambda s: (0, gblk0 + wb_blk(s) // PER_BRANCH)),
            pl.BlockSpec((m, D_MODEL), lambda s: (0, 0)),
            pl.BlockSpec(memory_space=pl.ANY),
            pl.BlockSpec(memory_space=pl.ANY),
            pl.BlockSpec((1, D_MODEL), lambda s: (0, 0)),
        ],
        out_specs=[
            pl.BlockSpec((m, D_MODEL), lambda s: (0, 0)),
            pl.BlockSpec((WROWS, D_MODEL), lambda s: (wb_blk(s), 0)),
            pl.BlockSpec((WROWS, D_MODEL), lambda s: (wo_blk(s), 0)),
        ],
        out_shape=[
            jax.ShapeDtypeStruct((m, D_MODEL), F32),
            jax.ShapeDtypeStruct(wb.shape, BF16),
            jax.ShapeDtypeStruct(wo.shape, BF16),
        ],
        scratch_shapes=[pltpu.VMEM((m, D_MODEL), F32), pltpu.VMEM((m, D_MODEL), F32),
                        pltpu.VMEM((BOC_RING, WROWS, D_MODEL), F32),
                        pltpu.SemaphoreType.DMA((BOC_RING,))],
        compiler_params=pltpu.CompilerParams(
            dimension_semantics=("arbitrary",),
            vmem_limit_bytes=BIG_VMEM_LIMIT),
        name="branch_out_cast",
    )(o, z, x, wb, wo, g_post)


def kernel(x_prompt, x_sample, mem_prompt, state_rglru_h, state_conv, state_pool, cache_mem_k, cache_mem_v, g_pre, w_in, conv_w, conv_b, w_rg_a, b_rg_a, w_rg_x, b_rg_x, lru_lambda, w_pool, pool_scale, g_mem, w_kv, w_branch, w_out, g_post):
    batch, seq, _ = x_prompt.shape
    nb = x_sample.shape[0]
    depth = g_pre.shape[0]
    assert depth == 1 and x_sample.shape[1] == 1

    l = 0
    row = lambda v: v.reshape(1, -1)
    mix_params = (conv_w[l], row(conv_b[l]), w_rg_a[l], w_rg_x[l], row(b_rg_a[l]), row(b_rg_x[l]),
                  row(lru_lambda[l]), w_pool[l], row(pool_scale[l]))

    xp2 = x_prompt.reshape(batch * seq, D_MODEL)
    xs2 = x_sample.reshape(nb, D_MODEL)
    mem2 = mem_prompt.reshape(batch * N_MEM, D_MODEL)

    z_s, w_in_b, mem_k, mem_v = _sample_proj(xs2, row(g_pre[l]), w_in[l], mem2, row(g_mem[l]),
                                             w_kv[l], tn=SAMPLE_PROJ_TN)
    qoff = 2 * D_RNN + 2 * D_POOL
    q_s = z_s[:, qoff:qoff + D_X].reshape(nb // ATTN_BB, ATTN_BB, D_X)

    mem_k = mem_k.reshape(batch, N_MEM, D_X)
    mem_v = mem_v.reshape(batch, N_MEM, D_X)

    perm = _chunk_interleave()
    z_p = _prompt_proj(xp2, row(g_pre[l]), w_in_b, perm, tm=PROJ_TM, tn=PROJ_TN)
    o_p, h_p, c_p, p_p, attn_s = _prompt_mix(
        z_p, mem_k, mem_v, *mix_params, perm.T, q_s, _cache_rows(cache_mem_k[l]),
        _cache_rows(cache_mem_v[l]), batch=batch, seq=seq, tm=MIX_TM)
    attn_s = attn_s.reshape(nb, D_X)

    o_s, h_s, c_s, p_s = _sample_mix(
        z_s, attn_s, state_conv[l].transpose(1, 0, 2), state_rglru_h[l],
        state_pool[l].transpose(1, 0, 2), *mix_params, tb=SAMPLE_MIX_TB)
    y_s, w_br_b, w_out_b = _branch_out_cast(o_s, z_s, xs2, w_branch[l], w_out[l], row(g_post[l]))

    y_p = _branch_out(o_p, z_p, xp2, w_br_b, w_out_b, row(g_post[l]), tm=BRANCH_TM)

    return (
        y_p.reshape(batch, seq, D_MODEL),
        y_s.reshape(nb, 1, D_MODEL),
        h_p.reshape(1, batch, D_RNN),
        c_p.reshape(1, batch, CONV_W - 1, D_RNN),
        p_p.reshape(1, batch, POOL_HIST, D_POOL),
        mem_k.reshape(1, batch, N_MEM, N_XHEADS, XHEAD_DIM),
        mem_v.reshape(1, batch, N_MEM, N_XHEADS, XHEAD_DIM),
        h_s.reshape(1, nb, D_RNN),
        c_s.transpose(1, 0, 2)[None],
        p_s.transpose(1, 0, 2)[None],
    )
```
